```python
import math
import jax, jax.numpy as jnp
from jax import lax
import numpy as np

D_MODEL = 1024
BATCH = 8
SEQ = 2048
DEPTH = 2
DEC_BATCH = 128
DEC_SEQ = 1
PAST_LEN = 8192
PAGE_SIZE = 128

HEAD_DIM = 64
N_A_LAYERS = (DEPTH + 1) // 2
N_C_LAYERS = DEPTH // 2
A_HEADS = 8
A_KV = 2
A_REP = A_HEADS // A_KV
A_Q = A_HEADS * HEAD_DIM
A_KVW = A_KV * HEAD_DIM
A_GATE = 3 * A_HEADS
CMP_LEN = 32
CMP_STRIDE = 16
SEL_BLOCK = 64
SEL_TOPK = 16
WIN_A = 512
NSA_QBLK = 64
FORCE_BONUS = 1000.0
B_WIDTH = D_MODEL // 2
B_GROUP = 16
B_GROUPS = B_WIDTH // B_GROUP
B_STATE = 64
DT_MIN = 1e-3
DT_MAX = 1e-1
IN_A = A_Q + 6 * A_KVW + A_GATE + B_WIDTH
C_HEADS = D_MODEL // HEAD_DIM
C_KV = 2
C_REP = C_HEADS // C_KV
WIN_C = 128
IN_C = (C_HEADS + 2 * C_KV) * HEAD_DIM
NUM_BUCKETS = 32
MAX_DISTANCE = 128
BIAS_HEADS = D_MODEL // HEAD_DIM
D_FF = 2816
CONV_W = 3
EPS = 1e-6

kernel_name = 'hybrid_nsa_s5_swa_convffn_step'


def rms_norm(x, g):
    xf = x.astype(jnp.float32)
    y = xf * lax.rsqrt(jnp.mean(xf * xf, axis=-1, keepdims=True) + EPS)
    return (y * g.astype(jnp.float32)).astype(x.dtype)


def rel_bucket(dist):
    n = jnp.maximum(dist, 0)
    exact = NUM_BUCKETS // 2
    nf = jnp.maximum(n, exact).astype(jnp.float32)
    large = exact + (jnp.log(nf / exact) / math.log(MAX_DISTANCE / exact) * (NUM_BUCKETS - exact)).astype(jnp.int32)
    return jnp.where(n < exact, n, jnp.minimum(large, NUM_BUCKETS - 1))


def masked_softmax(s, mask):
    s = jnp.where(mask, s.astype(jnp.float32), -jnp.inf)
    m = jnp.max(s, axis=-1, keepdims=True)
    m = jnp.where(jnp.isfinite(m), m, 0.0)
    e = jnp.exp(s - m)
    d = jnp.sum(e, axis=-1, keepdims=True)
    return e / jnp.where(d > 0, d, 1.0)


def compress(rows, pos, w1, w2):
    b, t, g, dh = rows.shape
    r = CMP_LEN // CMP_STRIDE
    nch = t // CMP_STRIDE
    chunks = rows[:, :nch * CMP_STRIDE].reshape(b, nch, CMP_STRIDE, g, dh)
    w1b = w1.reshape(r, CMP_STRIDE, dh, w1.shape[-1])
    nc = nch - r + 1
    hid = jnp.einsum('ld,ldh->h', pos, w1.reshape(CMP_LEN, dh, w1.shape[-1]))
    for j in range(r):
        hid = hid + jnp.einsum('bnsgd,sdh->bngh', chunks[:, j:j + nc], w1b[j])
    return jnp.einsum('bngh,hd->bngd', jax.nn.gelu(hid), w2)


def compressed_kv(k_rows, v_rows, cmp_pos, cmp_w1, cmp_w2, k_gain):
    kc = rms_norm(compress(k_rows, cmp_pos[0], cmp_w1[0], cmp_w2[0]), k_gain)
    vc = compress(v_rows, cmp_pos[1], cmp_w1[1], cmp_w2[1])
    c_end = jnp.arange(kc.shape[1]) * CMP_STRIDE + (CMP_LEN - 1)
    return kc, vc, c_end


def nsa_attend(q, q_pos, kc, vc, c_end, n_sel, gather_sel, kw, vw, w_pos, gates, tab):
    scale = HEAD_DIM ** -0.5
    g_n = q.shape[2]
    s_c = jnp.einsum('bqgrd,bngd->bgrqn', q, kc) * scale
    p_c = masked_softmax(s_c, c_end[None, :] <= q_pos[:, None])
    o_c = jnp.einsum('bgrqn,bngd->bqgrd', p_c.astype(vc.dtype), vc)
    s_start = jnp.arange(n_sel) * SEL_BLOCK
    c_start = c_end - (CMP_LEN - 1)
    cover = ((c_start[:, None] < s_start[None, :] + SEL_BLOCK) & (c_end[:, None] >= s_start[None, :])).astype(jnp.float32)
    imp = jnp.einsum('bgrqn,ns->bgqs', p_c, cover)
    qblk = q_pos[:, None] // SEL_BLOCK
    sblk = jnp.arange(n_sel)[None, :]
    forced = ((sblk == 0) | (sblk == qblk) | (sblk == qblk - 1)).astype(jnp.float32)
    score = jnp.where(sblk <= qblk, imp + FORCE_BONUS * forced, -jnp.inf)
    _, idx = lax.top_k(score, min(SEL_TOPK, n_sel))
    k_sel, v_sel = gather_sel(idx)
    kpos = idx[..., None] * SEL_BLOCK + jnp.arange(SEL_BLOCK)
    dist = q_pos[:, None, None] - kpos
    tab_g = jnp.transpose(tab, (1, 0, 2))
    gi = jnp.arange(g_n)[None, :, None, None, None]
    bias = jnp.moveaxis(tab_g[gi, rel_bucket(dist)], -1, 3)
    s_s = jnp.einsum('bqgrd,bgqkjd->bgqrkj', q, k_sel) * scale + bias
    shp = s_s.shape
    mask = jnp.broadcast_to((dist >= 0)[:, :, :, None], shp)
    p_s = masked_softmax(s_s.reshape(shp[:4] + (-1,)), mask.reshape(shp[:4] + (-1,))).reshape(shp)
    o_s = jnp.einsum('bgqrkj,bgqkjd->bqgrd', p_s.astype(v_sel.dtype), v_sel)
    wd = q_pos[:, None] - w_pos[None, :]
    wmask = (wd >= 0) & (wd < WIN_A) & (w_pos[None, :] >= 0)
    wbias = jnp.transpose(tab[rel_bucket(wd)], (2, 3, 0, 1))
    s_w = jnp.einsum('bqgrd,bwgd->bgrqw', q, kw) * scale + wbias
    p_w = masked_softmax(s_w, wmask)
    o_w = jnp.einsum('bgrqw,bwgd->bqgrd', p_w.astype(vw.dtype), vw)
    return gates[..., 0:1] * o_c + gates[..., 1:2] * o_s + gates[..., 2:3] * o_w


def nsa_prompt(q, gates, kc_rows, vc_rows, ks, vs, kw, vw, cmp_pos, cmp_w1, cmp_w2, k_gain, tab):
    b, t = q.shape[:2]
    kc, vc, c_end = compressed_kv(kc_rows, vc_rows, cmp_pos, cmp_w1, cmp_w2, k_gain)
    n_sel = t // SEL_BLOCK
    ks_b = ks.reshape(b, n_sel, SEL_BLOCK, A_KV, HEAD_DIM).transpose(0, 3, 1, 2, 4)
    vs_b = vs.reshape(b, n_sel, SEL_BLOCK, A_KV, HEAD_DIM).transpose(0, 3, 1, 2, 4)
    bi = jnp.arange(b)[:, None, None, None]
    gi = jnp.arange(A_KV)[None, :, None, None]

    def gather_sel(idx):
        return ks_b[bi, gi, idx], vs_b[bi, gi, idx]

    pad = ((0, 0), (WIN_A, 0), (0, 0), (0, 0))
    kw_p, vw_p = jnp.pad(kw, pad), jnp.pad(vw, pad)

    def chunk(start):
        q_pos = start + jnp.arange(NSA_QBLK)
        qc = lax.dynamic_slice_in_dim(q, start, NSA_QBLK, axis=1)
        gc = lax.dynamic_slice_in_dim(gates, start, NSA_QBLK, axis=1)
        kwc = lax.dynamic_slice_in_dim(kw_p, start, NSA_QBLK + WIN_A, axis=1)
        vwc = lax.dynamic_slice_in_dim(vw_p, start, NSA_QBLK + WIN_A, axis=1)
        w_pos = start - WIN_A + jnp.arange(NSA_QBLK + WIN_A)
        return nsa_attend(qc, q_pos, kc, vc, c_end, n_sel, gather_sel, kwc, vwc, w_pos, gc, tab)

    out = lax.map(chunk, jnp.arange(t // NSA_QBLK) * NSA_QBLK)
    return jnp.moveaxis(out, 0, 1).reshape(b, t, A_KV, A_REP, HEAD_DIM)


def nsa_sample(q, gates, new_c, new_s, win_all, past, pool_cmp, pool_sel, page_table,
               cmp_pos, cmp_w1, cmp_w2, k_gain, tab):
    db, s = q.shape[:2]
    past_c = pool_cmp[page_table].reshape(db, past, 2, A_KV, HEAD_DIM)
    rows_c = jnp.concatenate([past_c, new_c], axis=1)
    kc, vc, c_end = compressed_kv(rows_c[:, :, 0], rows_c[:, :, 1], cmp_pos, cmp_w1, cmp_w2, k_gain)
    q_pos = past + jnp.arange(s)
    n_past_blk = past // SEL_BLOCK
    n_new_blk = -(-s // SEL_BLOCK)
    n_sel = n_past_blk + n_new_blk
    bpp = PAGE_SIZE // SEL_BLOCK
    pool_b = pool_sel.reshape(pool_sel.shape[0], bpp, SEL_BLOCK, 2, A_KV, HEAD_DIM)
    new_b = jnp.pad(new_s, ((0, 0), (0, n_new_blk * SEL_BLOCK - s), (0, 0), (0, 0), (0, 0)))
    new_b = new_b.reshape(db, n_new_blk, SEL_BLOCK, 2, A_KV, HEAD_DIM)
    bi = jnp.arange(db)[:, None, None, None]
    gi = jnp.arange(A_KV)[None, :, None, None]

    def gather_sel(idx):
        pidx = jnp.clip(idx, 0, n_past_blk - 1)
        phys = page_table[bi, pidx // bpp]
        from_pool = pool_b[phys, pidx % bpp, :, :, gi]
        from_new = new_b[bi, jnp.clip(idx - n_past_blk, 0, n_new_blk - 1), :, :, gi]
        kv = jnp.where((idx < n_past_blk)[..., None, None, None], from_pool, from_new)
        return kv[..., 0, :], kv[..., 1, :]

    w_len = win_all.shape[1]
    w_pos = past + s - w_len + jnp.arange(w_len)
    return nsa_attend(q, q_pos, kc, vc, c_end, n_sel, gather_sel, win_all[:, :, 0], win_all[:, :, 1],
                      w_pos, gates, tab)


def _ssm_combine(e1, e2):
    a1r, a1i, b1r, b1i = e1
    a2r, a2i, b2r, b2i = e2
    return (a2r * a1r - a2i * a1i, a2r * a1i + a2i * a1r,
            a2r * b1r - a2i * b1i + b2r, a2r * b1i + a2i * b1r + b2i)


def s5_mix(u, h_re, h_im, a_re, a_im, log_dt, b_re, b_im, c_re, c_im, d, w_glu, b_glu):
    bsz, t, _ = u.shape
    f32 = jnp.float32
    uf = u.astype(f32)
    ug = uf.reshape(bsz, t, B_GROUPS, B_GROUP)
    dt = jnp.exp(log_dt.astype(f32))[:, None]
    ar, ai = a_re.astype(f32), a_im.astype(f32)
    mag = jnp.exp(ar * dt)
    abr, abi = mag * jnp.cos(ai * dt), mag * jnp.sin(ai * dt)
    den = ar * ar + ai * ai
    wr = ((abr - 1.0) * ar + abi * ai) / den
    wi = (abi * ar - (abr - 1.0) * ai) / den
    bu_r = jnp.einsum('gpc,btgc->btgp', b_re.astype(f32), ug)
    bu_i = jnp.einsum('gpc,btgc->btgp', b_im.astype(f32), ug)
    x_r = wr * bu_r - wi * bu_i
    x_i = wr * bu_i + wi * bu_r
    hr, hi = h_re.astype(f32), h_im.astype(f32)
    x_r = x_r.at[:, 0].add(abr * hr - abi * hi)
    x_i = x_i.at[:, 0].add(abr * hi + abi * hr)
    a_r = jnp.broadcast_to(abr, x_r.shape)
    a_i = jnp.broadcast_to(abi, x_i.shape)
    _, _, s_r, s_i = lax.associative_scan(_ssm_combine, (a_r, a_i, x_r, x_i), axis=1)
    y = jnp.einsum('gcp,btgp->btgc', c_re.astype(f32), s_r) - jnp.einsum('gcp,btgp->btgc', c_im.astype(f32), s_i)
    y = y.reshape(bsz, t, B_WIDTH) + d.astype(f32) * uf
    z = jax.nn.gelu(y)
    out = z * jax.nn.sigmoid(z @ w_glu.astype(f32) + b_glu.astype(f32))
    return out.astype(u.dtype), s_r[:, -1], s_i[:, -1]


def nsa_s5_inputs(xn, w_in, q_gain, k_gain):
    b, t, _ = xn.shape
    sizes = [A_Q] + [A_KVW] * 6 + [A_GATE, B_WIDTH]
    cuts = [int(c) for c in np.cumsum(sizes)[:-1]]
    q, kc, vc, ks, vs, kw, vw, gl, u = jnp.split(xn @ w_in, cuts, axis=-1)
    heads = lambda z: z.reshape(b, t, A_KV, HEAD_DIM)
    q = rms_norm(q.reshape(b, t, A_KV, A_REP, HEAD_DIM), q_gain)
    gates = jax.nn.sigmoid(gl).reshape(b, t, A_KV, A_REP, 3)
    return (q, rms_norm(heads(kc), k_gain), heads(vc), rms_norm(heads(ks), k_gain), heads(vs),
            rms_norm(heads(kw), k_gain), heads(vw), gates, u)


def swa_inputs(xn, w_in, q_gain, k_gain):
    b, t, _ = xn.shape
    q, k, v = jnp.split(xn @ w_in, [C_HEADS * HEAD_DIM, (C_HEADS + C_KV) * HEAD_DIM], axis=-1)
    q = rms_norm(q.reshape(b, t, C_KV, C_REP, HEAD_DIM), q_gain)
    k = rms_norm(k.reshape(b, t, C_KV, HEAD_DIM), k_gain)
    return q, k, v.reshape(b, t, C_KV, HEAD_DIM)


def swa_attend(q, q_pos, k, v, k_pos, sinks, tab):
    dist = q_pos[:, :, None] - k_pos[:, None, :]
    mask = (dist >= 0) & (dist < WIN_C) & (k_pos[:, None, :] >= 0)
    bias = jnp.transpose(tab[rel_bucket(dist)], (3, 4, 0, 1, 2))
    s = jnp.einsum('bnqgrd,bnkgd->bgrnqk', q, k).astype(jnp.float32) * (HEAD_DIM ** -0.5) + bias
    s = jnp.where(mask, s, -jnp.inf)
    sink = sinks.astype(jnp.float32).reshape(C_KV, C_REP)[:, :, None, None, None]
    m = jnp.maximum(jnp.max(s, axis=-1, keepdims=True), sink)
    e = jnp.exp(s - m)
    p = e / (jnp.sum(e, axis=-1, keepdims=True) + jnp.exp(sink - m))
    return jnp.einsum('bgrnqk,bnkgd->bnqgrd', p.astype(v.dtype), v)


def swa_prompt_attend(q, k, v, sinks, tab):
    b, t = q.shape[:2]
    nb = t // WIN_C
    qb = q.reshape(b, nb, WIN_C, C_KV, C_REP, HEAD_DIM)

    def band(x):
        xb = x.reshape(b, nb, WIN_C, C_KV, HEAD_DIM)
        prev = jnp.pad(xb[:, :-1], ((0, 0), (1, 0), (0, 0), (0, 0), (0, 0)))
        return jnp.concatenate([prev, xb], axis=2)

    q_pos = jnp.arange(t).reshape(nb, WIN_C)
    k_pos = (jnp.arange(nb)[:, None] - 1) * WIN_C + jnp.arange(2 * WIN_C)[None, :]
    o = swa_attend(qb, q_pos, band(k), band(v), k_pos, sinks, tab)
    return o.reshape(b, t, C_HEADS * HEAD_DIM)


def conv_ffn(xn, prev, w_up, w_gate, conv_w, conv_b, w_down):
    h = xn @ w_up
    g = xn @ w_gate
    hp = jnp.concatenate([prev.astype(h.dtype), h], axis=1)
    hc = lax.conv_general_dilated(hp, conv_w.astype(hp.dtype)[:, None, :], window_strides=(1,), padding='VALID',
                                  dimension_numbers=('NWC', 'WIO', 'NWC'), feature_group_count=D_FF) + conv_b
    return (jax.nn.gelu(hc) * g) @ w_down, hp[:, -(CONV_W - 1):]


def setup_inputs(seed: int = 0) -> dict:
    key = jax.random.key(seed)
    keys = iter(jax.random.split(key, 64))
    nrm = lambda shape, scale: jax.random.normal(next(keys), shape, jnp.float32) * scale
    n_pages = PAST_LEN // PAGE_SIZE
    n_used = DEC_BATCH * n_pages
    n_pool = n_used + max(1, n_used // 4)
    wa_buf = min(WIN_A, PAST_LEN)
    wc_buf = min(WIN_C, PAST_LEN)
    page_table = jax.random.permutation(next(keys), n_pool)[:n_used].reshape(DEC_BATCH, n_pages).astype(jnp.int32)
    inp = {}
    inp['x_prompt'] = nrm((BATCH, SEQ, D_MODEL), 1.0)
    inp['x_sample'] = nrm((DEC_BATCH, DEC_SEQ, D_MODEL), 1.0)
    inp['cache_nsa_cmp'] = nrm((N_A_LAYERS, n_pool, PAGE_SIZE, 2, A_KV, HEAD_DIM), 1.0)
    inp['cache_nsa_sel'] = nrm((N_A_LAYERS, n_pool, PAGE_SIZE, 2, A_KV, HEAD_DIM), 1.0)
    inp['cache_nsa_win'] = nrm((N_A_LAYERS, DEC_BATCH, wa_buf, 2, A_KV, HEAD_DIM), 1.0)
    inp['state_s5_re'] = nrm((N_A_LAYERS, DEC_BATCH, B_GROUPS, B_STATE), 0.1)
    inp['state_s5_im'] = nrm((N_A_LAYERS, DEC_BATCH, B_GROUPS, B_STATE), 0.1)
    inp['cache_swa'] = nrm((N_C_LAYERS, DEC_BATCH, wc_buf, 2, C_KV, HEAD_DIM), 1.0)
    inp['state_ffn_conv'] = nrm((DEPTH, DEC_BATCH, CONV_W - 1, D_FF), 1.0)
    inp['page_table'] = page_table
    inp['rel_bias'] = nrm((NUM_BUCKETS, BIAS_HEADS), 0.5)
    inp['norm_mix'] = 1.0 + nrm((DEPTH, D_MODEL), 0.02)
    inp['norm_ffn'] = 1.0 + nrm((DEPTH, D_MODEL), 0.02)
    inp['a_w_in'] = nrm((N_A_LAYERS, D_MODEL, IN_A), D_MODEL ** -0.5)
    inp['a_w_out'] = nrm((N_A_LAYERS, A_Q + B_WIDTH, D_MODEL), (A_Q + B_WIDTH) ** -0.5)
    inp['nsa_q_gain'] = 1.0 + nrm((N_A_LAYERS, HEAD_DIM), 0.02)
    inp['nsa_k_gain'] = 1.0 + nrm((N_A_LAYERS, HEAD_DIM), 0.02)
    inp['nsa_cmp_pos'] = nrm((N_A_LAYERS, 2, CMP_LEN, HEAD_DIM), 0.1)
    inp['nsa_cmp_w1'] = nrm((N_A_LAYERS, 2, CMP_LEN * HEAD_DIM, HEAD_DIM), (CMP_LEN * HEAD_DIM) ** -0.5)
    inp['nsa_cmp_w2'] = nrm((N_A_LAYERS, 2, HEAD_DIM, HEAD_DIM), HEAD_DIM ** -0.5)
    inp['s5_a_re'] = -0.5 + nrm((N_A_LAYERS, B_GROUPS, B_STATE), 0.01)
    inp['s5_a_im'] = math.pi * jnp.arange(B_STATE, dtype=jnp.float32) + nrm((N_A_LAYERS, B_GROUPS, B_STATE), 0.01)
    inp['s5_log_dt'] = jax.random.uniform(next(keys), (N_A_LAYERS, B_GROUPS), jnp.float32, math.log(DT_MIN), math.log(DT_MAX))
    inp['s5_b_re'] = nrm((N_A_LAYERS, B_GROUPS, B_STATE, B_GROUP), B_GROUP ** -0.5)
    inp['s5_b_im'] = nrm((N_A_LAYERS, B_GROUPS, B_STATE, B_GROUP), B_GROUP ** -0.5)
    inp['s5_c_re'] = nrm((N_A_LAYERS, B_GROUPS, B_GROUP, B_STATE), B_STATE ** -0.5)
    inp['s5_c_im'] = nrm((N_A_LAYERS, B_GROUPS, B_GROUP, B_STATE), B_STATE ** -0.5)
    inp['s5_d'] = nrm((N_A_LAYERS, B_WIDTH), 1.0)
    inp['s5_w_glu'] = nrm((N_A_LAYERS, B_WIDTH, B_WIDTH), B_WIDTH ** -0.5)
    inp['s5_b_glu'] = nrm((N_A_LAYERS, B_WIDTH), 0.02)
    inp['c_w_in'] = nrm((N_C_LAYERS, D_MODEL, IN_C), D_MODEL ** -0.5)
    inp['c_w_out'] = nrm((N_C_LAYERS, C_HEADS * HEAD_DIM, D_MODEL), (C_HEADS * HEAD_DIM) ** -0.5)
    inp['c_q_gain'] = 1.0 + nrm((N_C_LAYERS, HEAD_DIM), 0.02)
    inp['c_k_gain'] = 1.0 + nrm((N_C_LAYERS, HEAD_DIM), 0.02)
    inp['c_sinks'] = nrm((N_C_LAYERS, C_HEADS), 0.5)
    inp['ffn_w_up'] = nrm((DEPTH, D_MODEL, D_FF), D_MODEL ** -0.5)
    inp['ffn_w_gate'] = nrm((DEPTH, D_MODEL, D_FF), D_MODEL ** -0.5)
    inp['ffn_conv_w'] = nrm((DEPTH, CONV_W, D_FF), CONV_W ** -0.5)
    inp['ffn_conv_b'] = nrm((DEPTH, D_FF), 0.02)
    inp['ffn_w_down'] = nrm((DEPTH, D_FF, D_MODEL), D_FF ** -0.5)
    return inp


def reference(x_prompt, x_sample, cache_nsa_cmp, cache_nsa_sel, cache_nsa_win, state_s5_re, state_s5_im,
              cache_swa, state_ffn_conv, page_table, rel_bias, norm_mix, norm_ffn, a_w_in, a_w_out,
              nsa_q_gain, nsa_k_gain, nsa_cmp_pos, nsa_cmp_w1, nsa_cmp_w2, s5_a_re, s5_a_im, s5_log_dt,
              s5_b_re, s5_b_im, s5_c_re, s5_c_im, s5_d, s5_w_glu, s5_b_glu, c_w_in, c_w_out, c_q_gain,
              c_k_gain, c_sinks, ffn_w_up, ffn_w_gate, ffn_conv_w, ffn_conv_b, ffn_w_down):
    bp, tp, _ = x_prompt.shape
    bs, ts, _ = x_sample.shape
    past = page_table.shape[1] * PAGE_SIZE
    tab_a = rel_bias[:, :A_HEADS].reshape(NUM_BUCKETS, A_KV, A_REP)
    tab_c = rel_bias[:, :C_HEADS].reshape(NUM_BUCKETS, C_KV, C_REP)
    hp, hs = x_prompt, x_sample
    cmp_p, cmp_s, sel_p, sel_s, win_p, win_s = [], [], [], [], [], []
    s5r_p, s5i_p, s5r_s, s5i_s = [], [], [], []
    swa_p, swa_s, conv_p, conv_s = [], [], [], []
    for layer in range(DEPTH):
        i = layer // 2
        xp = rms_norm(hp, norm_mix[layer])
        xs = rms_norm(hs, norm_mix[layer])
        if layer % 2 == 0:
            qp, kcp, vcp, ksp, vsp, kwp, vwp, gp, up = nsa_s5_inputs(xp, a_w_in[i], nsa_q_gain[i], nsa_k_gain[i])
            qs, kcs, vcs, kss, vss, kws, vws, gs, us = nsa_s5_inputs(xs, a_w_in[i], nsa_q_gain[i], nsa_k_gain[i])
            cmp_args = (nsa_cmp_pos[i], nsa_cmp_w1[i], nsa_cmp_w2[i], nsa_k_gain[i], tab_a)
            o_ap = nsa_prompt(qp, gp, kcp, vcp, ksp, vsp, kwp, vwp, *cmp_args)
            new_c = jnp.stack([kcs, vcs], axis=2)
            new_sel = jnp.stack([kss, vss], axis=2)
            win_all = jnp.concatenate([cache_nsa_win[i].astype(kws.dtype), jnp.stack([kws, vws], axis=2)], axis=1)
            o_as = nsa_sample(qs, gs, new_c, new_sel, win_all, past, cache_nsa_cmp[i], cache_nsa_sel[i],
                              page_table, *cmp_args)
            s5_args = (s5_a_re[i], s5_a_im[i], s5_log_dt[i], s5_b_re[i], s5_b_im[i], s5_c_re[i], s5_c_im[i],
                       s5_d[i], s5_w_glu[i], s5_b_glu[i])
            h0 = jnp.zeros((bp, B_GROUPS, B_STATE), jnp.float32)
            o_bp, hr_p, hi_p = s5_mix(up, h0, h0, *s5_args)
            o_bs, hr_s, hi_s = s5_mix(us, state_s5_re[i], state_s5_im[i], *s5_args)
            hp = hp + jnp.concatenate([o_ap.reshape(bp, tp, A_Q), o_bp], axis=-1) @ a_w_out[i]
            hs = hs + jnp.concatenate([o_as.reshape(bs, ts, A_Q), o_bs], axis=-1) @ a_w_out[i]
            cmp_p.append(jnp.stack([kcp, vcp], axis=2))
            cmp_s.append(new_c)
            sel_p.append(jnp.stack([ksp, vsp], axis=2))
            sel_s.append(new_sel)
            win_p.append(jnp.stack([kwp, vwp], axis=2)[:, -min(WIN_A, tp):])
            win_s.append(win_all[:, -min(WIN_A, win_all.shape[1]):])
            s5r_p.append(hr_p)
            s5i_p.append(hi_p)
            s5r_s.append(hr_s)
            s5i_s.append(hi_s)
        else:
            qp, kp, vp = swa_inputs(xp, c_w_in[i], c_q_gain[i], c_k_gain[i])
            qs, k_s, v_s = swa_inputs(xs, c_w_in[i], c_q_gain[i], c_k_gain[i])
            o_cp = swa_prompt_attend(qp, kp, vp, c_sinks[i], tab_c)
            kv_all = jnp.concatenate([cache_swa[i].astype(k_s.dtype), jnp.stack([k_s, v_s], axis=2)], axis=1)
            kv_len = kv_all.shape[1]
            q_pos = (past + jnp.arange(ts))[None, :]
            k_pos = (past + ts - kv_len + jnp.arange(kv_len))[None, :]
            o_cs = swa_attend(qs[:, None], q_pos, kv_all[:, None, :, 0], kv_all[:, None, :, 1], k_pos,
                              c_sinks[i], tab_c).reshape(bs, ts, C_HEADS * HEAD_DIM)
            hp = hp + o_cp @ c_w_out[i]
            hs = hs + o_cs @ c_w_out[i]
            swa_p.append(jnp.stack([kp, vp], axis=2)[:, -min(WIN_C, tp):])
            swa_s.append(kv_all[:, -min(WIN_C, kv_len):])
        f_args = (ffn_w_up[layer], ffn_w_gate[layer], ffn_conv_w[layer], ffn_conv_b[layer], ffn_w_down[layer])
        yp, cp = conv_ffn(rms_norm(hp, norm_ffn[layer]), jnp.zeros((bp, CONV_W - 1, D_FF), hp.dtype), *f_args)
        ys, cs = conv_ffn(rms_norm(hs, norm_ffn[layer]), state_ffn_conv[layer], *f_args)
        hp = hp + yp
        hs = hs + ys
        conv_p.append(cp)
        conv_s.append(cs)
    y_prompt, y_sample = hp, hs
    nsa_cmp_prompt, nsa_cmp_sample = jnp.stack(cmp_p), jnp.stack(cmp_s)
    nsa_sel_prompt, nsa_sel_sample = jnp.stack(sel_p), jnp.stack(sel_s)
    nsa_win_prompt, nsa_win_sample = jnp.stack(win_p), jnp.stack(win_s)
    s5_re_prompt, s5_im_prompt = jnp.stack(s5r_p), jnp.stack(s5i_p)
    s5_re_sample, s5_im_sample = jnp.stack(s5r_s), jnp.stack(s5i_s)
    swa_prompt, swa_sample = jnp.stack(swa_p), jnp.stack(swa_s)
    ffn_conv_prompt, ffn_conv_sample = jnp.stack(conv_p), jnp.stack(conv_s)
    return (y_prompt, y_sample, nsa_cmp_prompt, nsa_cmp_sample, nsa_sel_prompt, nsa_sel_sample,
            nsa_win_prompt, nsa_win_sample, s5_re_prompt, s5_im_prompt, s5_re_sample, s5_im_sample,
            swa_prompt, swa_sample, ffn_conv_prompt, ffn_conv_sample)
```

```python
import functools
import math

import jax
import jax.numpy as jnp
import numpy as np
from jax import lax
from jax.experimental import pallas as pl
from jax.experimental.pallas import tpu as pltpu

D_MODEL = 1024
DEPTH = 2
PAGE_SIZE = 128
HEAD_DIM = 64
A_HEADS = 8
A_KV = 2
A_REP = A_HEADS // A_KV
A_Q = A_HEADS * HEAD_DIM
A_KVW = A_KV * HEAD_DIM
A_GATE = 3 * A_HEADS
CMP_LEN = 32
CMP_STRIDE = 16
SEL_BLOCK = 64
SEL_TOPK = 16
WIN_A = 512
NSA_QBLK = 64
FORCE_BONUS = 1000.0
B_WIDTH = D_MODEL // 2
B_GROUP = 16
B_GROUPS = B_WIDTH // B_GROUP
B_STATE = 64
C_HEADS = D_MODEL // HEAD_DIM
C_KV = 2
C_REP = C_HEADS // C_KV
WIN_C = 128
NUM_BUCKETS = 32
MAX_DISTANCE = 128
D_FF = 2816
CONV_W = 3
EPS = 1e-6

F32 = jnp.float32
BF16 = jnp.bfloat16

V7X_VMEM_LIMIT_BYTES = 56 * 1024 * 1024
FFN_ROW_TILE = 512
FFN_COL_CHUNK = 1408


def _gelu_tanh(x):
    return 0.5 * x * (1.0 + jnp.tanh(math.sqrt(2.0 / math.pi) * (x + 0.044715 * (x * x * x))))


def _rms_rows(x, gain):
    return x * lax.rsqrt(jnp.mean(x * x, axis=-1, keepdims=True) + EPS) * gain


def _const_spec(shape):
    zeros = (0,) * len(shape)
    return pl.BlockSpec(shape, lambda *_: zeros, pipeline_mode=pl.Buffered(1))


def _ffn_prompt_body(x_ref, gain_ref, prev_ref, wup_ref, wgate_ref, cw_ref, cb_ref, wdown_ref,
                     y_ref, cs_ref, hbuf_ref, *, tm, ffc):
    t = pl.program_id(1)
    x = x_ref[...]
    xn = _rms_rows(x, gain_ref[...]).astype(BF16)
    acc = x
    for c in range(D_FF // ffc):
        lo = c * ffc
        h = jnp.dot(xn, wup_ref[:, lo:lo + ffc], preferred_element_type=F32)
        g = jnp.dot(xn, wgate_ref[:, lo:lo + ffc], preferred_element_type=F32)

        @pl.when(t == 0)
        def _():
            hbuf_ref[c, 6:8, :] = prev_ref[:, lo:lo + ffc]

        hbuf_ref[c, 8:8 + tm, :] = h
        hm1 = hbuf_ref[c, 7:7 + tm, :]
        hm2 = hbuf_ref[c, 6:6 + tm, :]
        cw = cw_ref[:, lo:lo + ffc]
        hc = cw[0:1] * hm2 + cw[1:2] * hm1 + cw[2:3] * h + cb_ref[:, lo:lo + ffc]
        a = (_gelu_tanh(hc) * g).astype(BF16)
        acc = acc + jnp.dot(a, wdown_ref[lo:lo + ffc, :], preferred_element_type=F32)
        hbuf_ref[c, 0:8, :] = h[tm - 8:tm, :]
        cs_ref[:, lo:lo + ffc] = h[tm - 2:tm, :]
    y_ref[...] = acc


def _ffn_prompt(x, gain, prev, wup, wgate, cw, cb, wdown):
    b, t, d = x.shape
    tm, ffc = FFN_ROW_TILE, FFN_COL_CHUNK
    body = functools.partial(_ffn_prompt_body, tm=tm, ffc=ffc)
    return pl.pallas_call(
        body,
        grid=(b, t // tm),
        in_specs=[
            pl.BlockSpec((None, tm, d), lambda i, j: (i, j, 0)),
            _const_spec((1, d)),
            pl.BlockSpec((None, CONV_W - 1, D_FF), lambda i, j: (i, 0, 0)),
            _const_spec((d, D_FF)),
            _const_spec((d, D_FF)),
            _const_spec((CONV_W, D_FF)),
            _const_spec((1, D_FF)),
            _const_spec((D_FF, d)),
        ],
        out_specs=[
            pl.BlockSpec((None, tm, d), lambda i, j: (i, j, 0)),
            pl.BlockSpec((None, CONV_W - 1, D_FF), lambda i, j: (i, 0, 0)),
        ],
        out_shape=[
            jax.ShapeDtypeStruct((b, t, d), F32),
            jax.ShapeDtypeStruct((b, CONV_W - 1, D_FF), F32),
        ],
        scratch_shapes=[pltpu.VMEM((D_FF // ffc, 8 + tm, ffc), F32)],
        compiler_params=pltpu.CompilerParams(
            dimension_semantics=("parallel", "arbitrary"),
            vmem_limit_bytes=V7X_VMEM_LIMIT_BYTES),
        name="ffn_prompt",
    )(x, gain, prev, wup, wgate, cw, cb, wdown)


def _ffn_sample_body(x_ref, gain_ref, prev_ref, wup_ref, wgate_ref, cw_ref, cb_ref, wdown_ref,
                     y_ref, cs_ref, *, ffc):
    x = x_ref[...]
    xn = _rms_rows(x, gain_ref[...]).astype(BF16)
    acc = x
    for c in range(D_FF // ffc):
        lo = c * ffc
        h = jnp.dot(xn, wup_ref[:, lo:lo + ffc], preferred_element_type=F32)
        g = jnp.dot(xn, wgate_ref[:, lo:lo + ffc], preferred_element_type=F32)
        hm2 = prev_ref[:, lo:lo + ffc]
        hm1 = prev_ref[:, D_FF + lo:D_FF + lo + ffc]
        cw = cw_ref[:, lo:lo + ffc]
        hc = cw[0:1] * hm2 + cw[1:2] * hm1 + cw[2:3] * h + cb_ref[:, lo:lo + ffc]
        a = (_gelu_tanh(hc) * g).astype(BF16)
        acc = acc + jnp.dot(a, wdown_ref[lo:lo + ffc, :], preferred_element_type=F32)
        cs_ref[:, lo:lo + ffc] = hm1
        cs_ref[:, D_FF + lo:D_FF + lo + ffc] = h
    y_ref[...] = acc


def _ffn_sample(x, gain, prev, wup, wgate, cw, cb, wdown):
    n, d = x.shape
    body = functools.partial(_ffn_sample_body, ffc=FFN_COL_CHUNK)
    return pl.pallas_call(
        body,
        grid=(1,),
        in_specs=[
            _const_spec((n, d)),
            _const_spec((1, d)),
            _const_spec((n, (CONV_W - 1) * D_FF)),
            _const_spec((d, D_FF)),
            _const_spec((d, D_FF)),
            _const_spec((CONV_W, D_FF)),
            _const_spec((1, D_FF)),
            _const_spec((D_FF, d)),
        ],
        out_specs=[
            pl.BlockSpec((n, d), lambda i: (0, 0)),
            pl.BlockSpec((n, (CONV_W - 1) * D_FF), lambda i: (0, 0)),
        ],
        out_shape=[
            jax.ShapeDtypeStruct((n, d), F32),
            jax.ShapeDtypeStruct((n, (CONV_W - 1) * D_FF), F32),
        ],
        compiler_params=pltpu.CompilerParams(
            dimension_semantics=("arbitrary",),
            vmem_limit_bytes=V7X_VMEM_LIMIT_BYTES),
        name="ffn_sample",
    )(x, gain, prev, wup, wgate, cw, cb, wdown)


def rms_norm(x, g):
    xf = x.astype(jnp.float32)
    y = xf * lax.rsqrt(jnp.mean(xf * xf, axis=-1, keepdims=True) + EPS)
    return (y * g.astype(jnp.float32)).astype(x.dtype)


def rel_bucket(dist):
    n = jnp.maximum(dist, 0)
    exact = NUM_BUCKETS // 2
    nf = jnp.maximum(n, exact).astype(jnp.float32)
    large = exact + (jnp.log(nf / exact) / math.log(MAX_DISTANCE / exact) * (NUM_BUCKETS - exact)).astype(jnp.int32)
    return jnp.where(n < exact, n, jnp.minimum(large, NUM_BUCKETS - 1))


def masked_softmax(s, mask):
    s = jnp.where(mask, s.astype(jnp.float32), -jnp.inf)
    m = jnp.max(s, axis=-1, keepdims=True)
    m = jnp.where(jnp.isfinite(m), m, 0.0)
    e = jnp.exp(s - m)
    d = jnp.sum(e, axis=-1, keepdims=True)
    return e / jnp.where(d > 0, d, 1.0)


def compress(rows, pos, w1, w2):
    b, t, g, dh = rows.shape
    r = CMP_LEN // CMP_STRIDE
    nch = t // CMP_STRIDE
    chunks = rows[:, :nch * CMP_STRIDE].reshape(b, nch, CMP_STRIDE, g, dh)
    w1b = w1.reshape(r, CMP_STRIDE, dh, w1.shape[-1])
    nc = nch - r + 1
    hid = jnp.einsum('ld,ldh->h', pos, w1.reshape(CMP_LEN, dh, w1.shape[-1]))
    for j in range(r):
        hid = hid + jnp.einsum('bnsgd,sdh->bngh', chunks[:, j:j + nc], w1b[j])
    return jnp.einsum('bngh,hd->bngd', jax.nn.gelu(hid), w2)


def compressed_kv(k_rows, v_rows, cmp_pos, cmp_w1, cmp_w2, k_gain):
    kc = rms_norm(compress(k_rows, cmp_pos[0], cmp_w1[0], cmp_w2[0]), k_gain)
    vc = compress(v_rows, cmp_pos[1], cmp_w1[1], cmp_w2[1])
    c_end = jnp.arange(kc.shape[1]) * CMP_STRIDE + (CMP_LEN - 1)
    return kc, vc, c_end


def nsa_attend(q, q_pos, kc, vc, c_end, n_sel, gather_sel, kw, vw, w_pos, gates, tab):
    scale = HEAD_DIM ** -0.5
    g_n = q.shape[2]
    s_c = jnp.einsum('bqgrd,bngd->bgrqn', q, kc) * scale
    p_c = masked_softmax(s_c, c_end[None, :] <= q_pos[:, None])
    o_c = jnp.einsum('bgrqn,bngd->bqgrd', p_c.astype(vc.dtype), vc)
    s_start = jnp.arange(n_sel) * SEL_BLOCK
    c_start = c_end - (CMP_LEN - 1)
    cover = ((c_start[:, None] < s_start[None, :] + SEL_BLOCK) & (c_end[:, None] >= s_start[None, :])).astype(jnp.float32)
    imp = jnp.einsum('bgrqn,ns->bgqs', p_c, cover)
    qblk = q_pos[:, None] // SEL_BLOCK
    sblk = jnp.arange(n_sel)[None, :]
    forced = ((sblk == 0) | (sblk == qblk) | (sblk == qblk - 1)).astype(jnp.float32)
    score = jnp.where(sblk <= qblk, imp + FORCE_BONUS * forced, -jnp.inf)
    _, idx = lax.top_k(score, min(SEL_TOPK, n_sel))
    k_sel, v_sel = gather_sel(idx)
    kpos = idx[..., None] * SEL_BLOCK + jnp.arange(SEL_BLOCK)
    dist = q_pos[:, None, None] - kpos
    tab_g = jnp.transpose(tab, (1, 0, 2))
    gi = jnp.arange(g_n)[None, :, None, None, None]
    bias = jnp.moveaxis(tab_g[gi, rel_bucket(dist)], -1, 3)
    s_s = jnp.einsum('bqgrd,bgqkjd->bgqrkj', q, k_sel) * scale + bias
    shp = s_s.shape
    mask = jnp.broadcast_to((dist >= 0)[:, :, :, None], shp)
    p_s = masked_softmax(s_s.reshape(shp[:4] + (-1,)), mask.reshape(shp[:4] + (-1,))).reshape(shp)
    o_s = jnp.einsum('bgqrkj,bgqkjd->bqgrd', p_s.astype(v_sel.dtype), v_sel)
    wd = q_pos[:, None] - w_pos[None, :]
    wmask = (wd >= 0) & (wd < WIN_A) & (w_pos[None, :] >= 0)
    wbias = jnp.transpose(tab[rel_bucket(wd)], (2, 3, 0, 1))
    s_w = jnp.einsum('bqgrd,bwgd->bgrqw', q, kw) * scale + wbias
    p_w = masked_softmax(s_w, wmask)
    o_w = jnp.einsum('bgrqw,bwgd->bqgrd', p_w.astype(vw.dtype), vw)
    return gates[..., 0:1] * o_c + gates[..., 1:2] * o_s + gates[..., 2:3] * o_w


def nsa_prompt(q, gates, kc_rows, vc_rows, ks, vs, kw, vw, cmp_pos, cmp_w1, cmp_w2, k_gain, tab):
    b, t = q.shape[:2]
    kc, vc, c_end = compressed_kv(kc_rows, vc_rows, cmp_pos, cmp_w1, cmp_w2, k_gain)
    n_sel = t // SEL_BLOCK
    ks_b = ks.reshape(b, n_sel, SEL_BLOCK, A_KV, HEAD_DIM).transpose(0, 3, 1, 2, 4)
    vs_b = vs.reshape(b, n_sel, SEL_BLOCK, A_KV, HEAD_DIM).transpose(0, 3, 1, 2, 4)
    bi = jnp.arange(b)[:, None, None, None]
    gi = jnp.arange(A_KV)[None, :, None, None]

    def gather_sel(idx):
        return ks_b[bi, gi, idx], vs_b[bi, gi, idx]

    pad = ((0, 0), (WIN_A, 0), (0, 0), (0, 0))
    kw_p, vw_p = jnp.pad(kw, pad), jnp.pad(vw, pad)

    def chunk(start):
        q_pos = start + jnp.arange(NSA_QBLK)
        qc = lax.dynamic_slice_in_dim(q, start, NSA_QBLK, axis=1)
        gc = lax.dynamic_slice_in_dim(gates, start, NSA_QBLK, axis=1)
        kwc = lax.dynamic_slice_in_dim(kw_p, start, NSA_QBLK + WIN_A, axis=1)
        vwc = lax.dynamic_slice_in_dim(vw_p, start, NSA_QBLK + WIN_A, axis=1)
        w_pos = start - WIN_A + jnp.arange(NSA_QBLK + WIN_A)
        return nsa_attend(qc, q_pos, kc, vc, c_end, n_sel, gather_sel, kwc, vwc, w_pos, gc, tab)

    out = lax.map(chunk, jnp.arange(t // NSA_QBLK) * NSA_QBLK)
    return jnp.moveaxis(out, 0, 1).reshape(b, t, A_KV, A_REP, HEAD_DIM)


def nsa_sample(q, gates, new_c, new_s, win_all, past, pool_cmp, pool_sel, page_table,
               cmp_pos, cmp_w1, cmp_w2, k_gain, tab):
    db, s = q.shape[:2]
    past_c = pool_cmp[page_table].reshape(db, past, 2, A_KV, HEAD_DIM)
    rows_c = jnp.concatenate([past_c, new_c], axis=1)
    kc, vc, c_end = compressed_kv(rows_c[:, :, 0], rows_c[:, :, 1], cmp_pos, cmp_w1, cmp_w2, k_gain)
    q_pos = past + jnp.arange(s)
    n_past_blk = past // SEL_BLOCK
    n_new_blk = -(-s // SEL_BLOCK)
    n_sel = n_past_blk + n_new_blk
    bpp = PAGE_SIZE // SEL_BLOCK
    pool_b = pool_sel.reshape(pool_sel.shape[0], bpp, SEL_BLOCK, 2, A_KV, HEAD_DIM)
    new_b = jnp.pad(new_s, ((0, 0), (0, n_new_blk * SEL_BLOCK - s), (0, 0), (0, 0), (0, 0)))
    new_b = new_b.reshape(db, n_new_blk, SEL_BLOCK, 2, A_KV, HEAD_DIM)
    bi = jnp.arange(db)[:, None, None, None]
    gi = jnp.arange(A_KV)[None, :, None, None]

    def gather_sel(idx):
        pidx = jnp.clip(idx, 0, n_past_blk - 1)
        phys = page_table[bi, pidx // bpp]
        pool_f = pool_b.reshape((-1,) + pool_b.shape[2:])
        both = pool_f[phys * bpp + pidx % bpp]
        from_pool = jnp.stack([both[:, g, ..., g, :] for g in range(A_KV)], axis=1)
        from_new = new_b[bi, jnp.clip(idx - n_past_blk, 0, n_new_blk - 1), :, :, gi]
        kv = jnp.where((idx < n_past_blk)[..., None, None, None], from_pool, from_new)
        return kv[..., 0, :], kv[..., 1, :]

    w_len = win_all.shape[1]
    w_pos = past + s - w_len + jnp.arange(w_len)
    return nsa_attend(q, q_pos, kc, vc, c_end, n_sel, gather_sel, win_all[:, :, 0], win_all[:, :, 1],
                      w_pos, gates, tab)


def _ssm_combine(e1, e2):
    a1r, a1i, b1r, b1i = e1
    a2r, a2i, b2r, b2i = e2
    return (a2r * a1r - a2i * a1i, a2r * a1i + a2i * a1r,
            a2r * b1r - a2i * b1i + b2r, a2r * b1i + a2i * b1r + b2i)


def s5_mix(u, h_re, h_im, a_re, a_im, log_dt, b_re, b_im, c_re, c_im, d, w_glu, b_glu):
    bsz, t, _ = u.shape
    f32 = jnp.float32
    uf = u.astype(f32)
    ug = uf.reshape(bsz, t, B_GROUPS, B_GROUP)
    dt = jnp.exp(log_dt.astype(f32))[:, None]
    ar, ai = a_re.astype(f32), a_im.astype(f32)
    mag = jnp.exp(ar * dt)
    abr, abi = mag * jnp.cos(ai * dt), mag * jnp.sin(ai * dt)
    den = ar * ar + ai * ai
    wr = ((abr - 1.0) * ar + abi * ai) / den
    wi = (abi * ar - (abr - 1.0) * ai) / den
    bu_r = jnp.einsum('gpc,btgc->btgp', b_re.astype(f32), ug)
    bu_i = jnp.einsum('gpc,btgc->btgp', b_im.astype(f32), ug)
    x_r = wr * bu_r - wi * bu_i
    x_i = wr * bu_i + wi * bu_r
    hr, hi = h_re.astype(f32), h_im.astype(f32)
    x_r = x_r.at[:, 0].add(abr * hr - abi * hi)
    x_i = x_i.at[:, 0].add(abr * hi + abi * hr)
    a_r = jnp.broadcast_to(abr, x_r.shape)
    a_i = jnp.broadcast_to(abi, x_i.shape)
    _, _, s_r, s_i = lax.associative_scan(_ssm_combine, (a_r, a_i, x_r, x_i), axis=1)
    y = jnp.einsum('gcp,btgp->btgc', c_re.astype(f32), s_r) - jnp.einsum('gcp,btgp->btgc', c_im.astype(f32), s_i)
    y = y.reshape(bsz, t, B_WIDTH) + d.astype(f32) * uf
    z = jax.nn.gelu(y)
    out = z * jax.nn.sigmoid(z @ w_glu.astype(f32) + b_glu.astype(f32))
    return out.astype(u.dtype), s_r[:, -1], s_i[:, -1]


def nsa_s5_inputs(xn, w_in, q_gain, k_gain):
    b, t, _ = xn.shape
    sizes = [A_Q] + [A_KVW] * 6 + [A_GATE, B_WIDTH]
    cuts = [int(c) for c in np.cumsum(sizes)[:-1]]
    q, kc, vc, ks, vs, kw, vw, gl, u = jnp.split(xn @ w_in, cuts, axis=-1)
    heads = lambda z: z.reshape(b, t, A_KV, HEAD_DIM)
    q = rms_norm(q.reshape(b, t, A_KV, A_REP, HEAD_DIM), q_gain)
    gates = jax.nn.sigmoid(gl).reshape(b, t, A_KV, A_REP, 3)
    return (q, rms_norm(heads(kc), k_gain), heads(vc), rms_norm(heads(ks), k_gain), heads(vs),
            rms_norm(heads(kw), k_gain), heads(vw), gates, u)


def swa_inputs(xn, w_in, q_gain, k_gain):
    b, t, _ = xn.shape
    q, k, v = jnp.split(xn @ w_in, [C_HEADS * HEAD_DIM, (C_HEADS + C_KV) * HEAD_DIM], axis=-1)
    q = rms_norm(q.reshape(b, t, C_KV, C_REP, HEAD_DIM), q_gain)
    k = rms_norm(k.reshape(b, t, C_KV, HEAD_DIM), k_gain)
    return q, k, v.reshape(b, t, C_KV, HEAD_DIM)


def swa_attend(q, q_pos, k, v, k_pos, sinks, tab):
    dist = q_pos[:, :, None] - k_pos[:, None, :]
    mask = (dist >= 0) & (dist < WIN_C) & (k_pos[:, None, :] >= 0)
    bias = jnp.transpose(tab[rel_bucket(dist)], (3, 4, 0, 1, 2))
    s = jnp.einsum('bnqgrd,bnkgd->bgrnqk', q, k).astype(jnp.float32) * (HEAD_DIM ** -0.5) + bias
    s = jnp.where(mask, s, -jnp.inf)
    sink = sinks.astype(jnp.float32).reshape(C_KV, C_REP)[:, :, None, None, None]
    m = jnp.maximum(jnp.max(s, axis=-1, keepdims=True), sink)
    e = jnp.exp(s - m)
    p = e / (jnp.sum(e, axis=-1, keepdims=True) + jnp.exp(sink - m))
    return jnp.einsum('bgrnqk,bnkgd->bnqgrd', p.astype(v.dtype), v)


def swa_prompt_attend(q, k, v, sinks, tab):
    b, t = q.shape[:2]
    nb = t // WIN_C
    qb = q.reshape(b, nb, WIN_C, C_KV, C_REP, HEAD_DIM)

    def band(x):
        xb = x.reshape(b, nb, WIN_C, C_KV, HEAD_DIM)
        prev = jnp.pad(xb[:, :-1], ((0, 0), (1, 0), (0, 0), (0, 0), (0, 0)))
        return jnp.concatenate([prev, xb], axis=2)

    q_pos = jnp.arange(t).reshape(nb, WIN_C)
    k_pos = (jnp.arange(nb)[:, None] - 1) * WIN_C + jnp.arange(2 * WIN_C)[None, :]
    o = swa_attend(qb, q_pos, band(k), band(v), k_pos, sinks, tab)
    return o.reshape(b, t, C_HEADS * HEAD_DIM)


def kernel(x_prompt, x_sample, cache_nsa_cmp, cache_nsa_sel, cache_nsa_win, state_s5_re, state_s5_im,
           cache_swa, state_ffn_conv, page_table, rel_bias, norm_mix, norm_ffn, a_w_in, a_w_out,
           nsa_q_gain, nsa_k_gain, nsa_cmp_pos, nsa_cmp_w1, nsa_cmp_w2, s5_a_re, s5_a_im, s5_log_dt,
           s5_b_re, s5_b_im, s5_c_re, s5_c_im, s5_d, s5_w_glu, s5_b_glu, c_w_in, c_w_out, c_q_gain,
           c_k_gain, c_sinks, ffn_w_up, ffn_w_gate, ffn_conv_w, ffn_conv_b, ffn_w_down):
    bp, tp, _ = x_prompt.shape
    bs, ts, _ = x_sample.shape
    past = page_table.shape[1] * PAGE_SIZE
    tab_a = rel_bias[:, :A_HEADS].reshape(NUM_BUCKETS, A_KV, A_REP)
    tab_c = rel_bias[:, :C_HEADS].reshape(NUM_BUCKETS, C_KV, C_REP)
    hp, hs = x_prompt, x_sample
    cmp_p, cmp_s, sel_p, sel_s, win_p, win_s = [], [], [], [], [], []
    s5r_p, s5i_p, s5r_s, s5i_s = [], [], [], []
    swa_p, swa_s, conv_p, conv_s = [], [], [], []
    for layer in range(DEPTH):
        i = layer // 2
        xp = rms_norm(hp, norm_mix[layer])
        xs = rms_norm(hs, norm_mix[layer])
        if layer % 2 == 0:
            qp, kcp, vcp, ksp, vsp, kwp, vwp, gp, up = nsa_s5_inputs(xp, a_w_in[i], nsa_q_gain[i], nsa_k_gain[i])
            qs, kcs, vcs, kss, vss, kws, vws, gs, us = nsa_s5_inputs(xs, a_w_in[i], nsa_q_gain[i], nsa_k_gain[i])
            cmp_args = (nsa_cmp_pos[i], nsa_cmp_w1[i], nsa_cmp_w2[i], nsa_k_gain[i], tab_a)
            o_ap = nsa_prompt(qp, gp, kcp, vcp, ksp, vsp, kwp, vwp, *cmp_args)
            new_c = jnp.stack([kcs, vcs], axis=2)
            new_sel = jnp.stack([kss, vss], axis=2)
            win_all = jnp.concatenate([cache_nsa_win[i].astype(kws.dtype), jnp.stack([kws, vws], axis=2)], axis=1)
            o_as = nsa_sample(qs, gs, new_c, new_sel, win_all, past, cache_nsa_cmp[i], cache_nsa_sel[i],
                              page_table, *cmp_args)
            s5_args = (s5_a_re[i], s5_a_im[i], s5_log_dt[i], s5_b_re[i], s5_b_im[i], s5_c_re[i], s5_c_im[i],
                       s5_d[i], s5_w_glu[i], s5_b_glu[i])
            h0 = jnp.zeros((bp, B_GROUPS, B_STATE), jnp.float32)
            o_bp, hr_p, hi_p = s5_mix(up, h0, h0, *s5_args)
            o_bs, hr_s, hi_s = s5_mix(us, state_s5_re[i], state_s5_im[i], *s5_args)
            hp = hp + jnp.concatenate([o_ap.reshape(bp, tp, A_Q), o_bp], axis=-1) @ a_w_out[i]
            hs = hs + jnp.concatenate([o_as.reshape(bs, ts, A_Q), o_bs], axis=-1) @ a_w_out[i]
            cmp_p.append(jnp.stack([kcp, vcp], axis=2))
            cmp_s.append(new_c)
            sel_p.append(jnp.stack([ksp, vsp], axis=2))
            sel_s.append(new_sel)
            win_p.append(jnp.stack([kwp, vwp], axis=2)[:, -min(WIN_A, tp):])
            win_s.append(win_all[:, -min(WIN_A, win_all.shape[1]):])
            s5r_p.append(hr_p)
            s5i_p.append(hi_p)
            s5r_s.append(hr_s)
            s5i_s.append(hi_s)
        else:
            qp, kp, vp = swa_inputs(xp, c_w_in[i], c_q_gain[i], c_k_gain[i])
            qs, k_s, v_s = swa_inputs(xs, c_w_in[i], c_q_gain[i], c_k_gain[i])
            o_cp = swa_prompt_attend(qp, kp, vp, c_sinks[i], tab_c)
            kv_all = jnp.concatenate([cache_swa[i].astype(k_s.dtype), jnp.stack([k_s, v_s], axis=2)], axis=1)
            kv_len = kv_all.shape[1]
            q_pos = (past + jnp.arange(ts))[None, :]
            k_pos = (past + ts - kv_len + jnp.arange(kv_len))[None, :]
            o_cs = swa_attend(qs[:, None], q_pos, kv_all[:, None, :, 0], kv_all[:, None, :, 1], k_pos,
                              c_sinks[i], tab_c).reshape(bs, ts, C_HEADS * HEAD_DIM)
            hp = hp + o_cp @ c_w_out[i]
            hs = hs + o_cs @ c_w_out[i]
            swa_p.append(jnp.stack([kp, vp], axis=2)[:, -min(WIN_C, tp):])
            swa_s.append(kv_all[:, -min(WIN_C, kv_len):])
        wup = ffn_w_up[layer].astype(BF16)
        wgate = ffn_w_gate[layer].astype(BF16)
        wdown = ffn_w_down[layer].astype(BF16)
        gain = norm_ffn[layer].reshape(1, D_MODEL)
        cw = ffn_conv_w[layer]
        cb = ffn_conv_b[layer].reshape(1, D_FF)
        hp, cp = _ffn_prompt(hp, gain, jnp.zeros((bp, CONV_W - 1, D_FF), F32), wup, wgate, cw, cb, wdown)
        hs2, cs = _ffn_sample(hs.reshape(bs * ts, D_MODEL), gain,
                              state_ffn_conv[layer].reshape(bs, (CONV_W - 1) * D_FF), wup, wgate, cw, cb, wdown)
        hs = hs2.reshape(bs, ts, D_MODEL)
        conv_p.append(cp)
        conv_s.append(cs.reshape(bs, CONV_W - 1, D_FF))
    y_prompt, y_sample = hp, hs
    return (y_prompt, y_sample, jnp.stack(cmp_p), jnp.stack(cmp_s), jnp.stack(sel_p), jnp.stack(sel_s),
            jnp.stack(win_p), jnp.stack(win_s), jnp.stack(s5r_p), jnp.stack(s5i_p), jnp.stack(s5r_s),
            jnp.stack(s5i_s), jnp.stack(swa_p), jnp.stack(swa_s), jnp.stack(conv_p), jnp.stack(conv_s))
```

```python
import functools
import math

import jax
import jax.numpy as jnp
import numpy as np
from jax import lax
from jax.experimental import pallas as pl
from jax.experimental.pallas import tpu as pltpu

D_MODEL = 1024
DEPTH = 2
PAGE_SIZE = 128
HEAD_DIM = 64
A_HEADS = 8
A_KV = 2
A_REP = A_HEADS // A_KV
A_Q = A_HEADS * HEAD_DIM
A_KVW = A_KV * HEAD_DIM
A_GATE = 3 * A_HEADS
CMP_LEN = 32
CMP_STRIDE = 16
SEL_BLOCK = 64
SEL_TOPK = 16
WIN_A = 512
NSA_QBLK = 64
FORCE_BONUS = 1000.0
B_WIDTH = D_MODEL // 2
B_GROUP = 16
B_GROUPS = B_WIDTH // B_GROUP
B_STATE = 64
C_HEADS = D_MODEL // HEAD_DIM
C_KV = 2
C_REP = C_HEADS // C_KV
WIN_C = 128
NUM_BUCKETS = 32
MAX_DISTANCE = 128
D_FF = 2816
CONV_W = 3
EPS = 1e-6

F32 = jnp.float32
BF16 = jnp.bfloat16

V7X_VMEM_LIMIT_BYTES = 56 * 1024 * 1024
FFN_ROW_TILE = 512
FFN_COL_CHUNK = 1408


def _gelu_tanh(x):
    return 0.5 * x * (1.0 + jnp.tanh(math.sqrt(2.0 / math.pi) * (x + 0.044715 * (x * x * x))))


def _rms_rows(x, gain):
    return x * lax.rsqrt(jnp.mean(x * x, axis=-1, keepdims=True) + EPS) * gain


def _const_spec(shape):
    zeros = (0,) * len(shape)
    return pl.BlockSpec(shape, lambda *_: zeros, pipeline_mode=pl.Buffered(1))


def _ffn_prompt_body(x_ref, gain_ref, prev_ref, wup_ref, wgate_ref, cw_ref, cb_ref, wdown_ref,
                     y_ref, cs_ref, hbuf_ref, *, tm, ffc):
    t = pl.program_id(1)
    x = x_ref[...]
    xn = _rms_rows(x, gain_ref[...]).astype(BF16)
    acc = x
    for c in range(D_FF // ffc):
        lo = c * ffc
        h = jnp.dot(xn, wup_ref[:, lo:lo + ffc], preferred_element_type=F32)
        g = jnp.dot(xn, wgate_ref[:, lo:lo + ffc], preferred_element_type=F32)

        @pl.when(t == 0)
        def _():
            hbuf_ref[c, 6:8, :] = prev_ref[:, lo:lo + ffc]

        hbuf_ref[c, 8:8 + tm, :] = h
        hm1 = hbuf_ref[c, 7:7 + tm, :]
        hm2 = hbuf_ref[c, 6:6 + tm, :]
        cw = cw_ref[:, lo:lo + ffc]
        hc = cw[0:1] * hm2 + cw[1:2] * hm1 + cw[2:3] * h + cb_ref[:, lo:lo + ffc]
        a = (_gelu_tanh(hc) * g).astype(BF16)
        acc = acc + jnp.dot(a, wdown_ref[lo:lo + ffc, :], preferred_element_type=F32)
        hbuf_ref[c, 0:8, :] = h[tm - 8:tm, :]
        cs_ref[:, lo:lo + ffc] = h[tm - 2:tm, :]
    y_ref[...] = acc


def _ffn_prompt(x, gain, prev, wup, wgate, cw, cb, wdown):
    b, t, d = x.shape
    tm, ffc = FFN_ROW_TILE, FFN_COL_CHUNK
    body = functools.partial(_ffn_prompt_body, tm=tm, ffc=ffc)
    return pl.pallas_call(
        body,
        grid=(b, t // tm),
        in_specs=[
            pl.BlockSpec((None, tm, d), lambda i, j: (i, j, 0)),
            _const_spec((1, d)),
            pl.BlockSpec((None, CONV_W - 1, D_FF), lambda i, j: (i, 0, 0)),
            _const_spec((d, D_FF)),
            _const_spec((d, D_FF)),
            _const_spec((CONV_W, D_FF)),
            _const_spec((1, D_FF)),
            _const_spec((D_FF, d)),
        ],
        out_specs=[
            pl.BlockSpec((None, tm, d), lambda i, j: (i, j, 0)),
            pl.BlockSpec((None, CONV_W - 1, D_FF), lambda i, j: (i, 0, 0)),
        ],
        out_shape=[
            jax.ShapeDtypeStruct((b, t, d), F32),
            jax.ShapeDtypeStruct((b, CONV_W - 1, D_FF), F32),
        ],
        scratch_shapes=[pltpu.VMEM((D_FF // ffc, 8 + tm, ffc), F32)],
        compiler_params=pltpu.CompilerParams(
            dimension_semantics=("parallel", "arbitrary"),
            vmem_limit_bytes=V7X_VMEM_LIMIT_BYTES),
        name="ffn_prompt",
    )(x, gain, prev, wup, wgate, cw, cb, wdown)


def _ffn_sample_body(x_ref, gain_ref, prev_ref, wup_ref, wgate_ref, cw_ref, cb_ref, wdown_ref,
                     y_ref, cs_ref, *, ffc):
    x = x_ref[...]
    xn = _rms_rows(x, gain_ref[...]).astype(BF16)
    acc = x
    for c in range(D_FF // ffc):
        lo = c * ffc
        h = jnp.dot(xn, wup_ref[:, lo:lo + ffc], preferred_element_type=F32)
        g = jnp.dot(xn, wgate_ref[:, lo:lo + ffc], preferred_element_type=F32)
        hm2 = prev_ref[:, lo:lo + ffc]
        hm1 = prev_ref[:, D_FF + lo:D_FF + lo + ffc]
        cw = cw_ref[:, lo:lo + ffc]
        hc = cw[0:1] * hm2 + cw[1:2] * hm1 + cw[2:3] * h + cb_ref[:, lo:lo + ffc]
        a = (_gelu_tanh(hc) * g).astype(BF16)
        acc = acc + jnp.dot(a, wdown_ref[lo:lo + ffc, :], preferred_element_type=F32)
        cs_ref[:, lo:lo + ffc] = hm1
        cs_ref[:, D_FF + lo:D_FF + lo + ffc] = h
    y_ref[...] = acc


def _ffn_sample(x, gain, prev, wup, wgate, cw, cb, wdown):
    n, d = x.shape
    body = functools.partial(_ffn_sample_body, ffc=FFN_COL_CHUNK)
    return pl.pallas_call(
        body,
        grid=(1,),
        in_specs=[
            _const_spec((n, d)),
            _const_spec((1, d)),
            _const_spec((n, (CONV_W - 1) * D_FF)),
            _const_spec((d, D_FF)),
            _const_spec((d, D_FF)),
            _const_spec((CONV_W, D_FF)),
            _const_spec((1, D_FF)),
            _const_spec((D_FF, d)),
        ],
        out_specs=[
            pl.BlockSpec((n, d), lambda i: (0, 0)),
            pl.BlockSpec((n, (CONV_W - 1) * D_FF), lambda i: (0, 0)),
        ],
        out_shape=[
            jax.ShapeDtypeStruct((n, d), F32),
            jax.ShapeDtypeStruct((n, (CONV_W - 1) * D_FF), F32),
        ],
        compiler_params=pltpu.CompilerParams(
            dimension_semantics=("arbitrary",),
            vmem_limit_bytes=V7X_VMEM_LIMIT_BYTES),
        name="ffn_sample",
    )(x, gain, prev, wup, wgate, cw, cb, wdown)


NSA_Q_TILE = 128
NSA_FAR_TILE = 512
NSA_NEAR = 2 * NSA_Q_TILE
NEG_INF = float("-inf")


def _dot_nt(a, b):
    return lax.dot_general(a, b, (((1,), (1,)), ((), ())), preferred_element_type=F32)


def _softmax_start(s, v):
    m = jnp.max(s, axis=-1, keepdims=True)
    e = jnp.exp(s - m)
    return m, jnp.sum(e, axis=-1, keepdims=True), jnp.dot(e.astype(BF16), v, preferred_element_type=F32)


def _softmax_more(carry, s, v):
    m, l, acc = carry
    m_new = jnp.maximum(m, jnp.max(s, axis=-1, keepdims=True))
    alpha = jnp.exp(m - m_new)
    e = jnp.exp(s - m_new)
    return (m_new, alpha * l + jnp.sum(e, axis=-1, keepdims=True),
            alpha * acc + jnp.dot(e.astype(BF16), v, preferred_element_type=F32))


def _block_expand(k0, width):
    kpos = k0 + lax.broadcasted_iota(jnp.int32, (128, width), 1)
    blk = lax.broadcasted_iota(jnp.int32, (128, width), 0)
    return jnp.where((kpos >> 6) == blk, 1.0, 0.0).astype(BF16)


def _nsa_prompt_body(q_ref, gate_ref, kc_ref, vc_ref, ks_ref, vs_ref, kw_ref, vw_ref, btile_ref, cover_ref,
                     o_ref, *, tq):
    qt = pl.program_id(2)
    q0 = pl.multiple_of(qt * tq, tq)
    rows = A_REP * tq
    q = q_ref[...]
    qs = jnp.concatenate([q[:, r * HEAD_DIM:(r + 1) * HEAD_DIM] for r in range(A_REP)], axis=0)
    qs = (qs * (HEAD_DIM ** -0.5)).astype(BF16)

    def row_pos(width):
        return q0 + (lax.broadcasted_iota(jnp.int32, (rows, width), 0) & (tq - 1))

    def col_idx(width):
        return lax.broadcasted_iota(jnp.int32, (rows, width), 1)

    n_idx = col_idx(128)
    valid_c = (n_idx * CMP_STRIDE + (CMP_LEN - 1) <= row_pos(128)) & (n_idx < 127)
    s_c = jnp.where(valid_c, _dot_nt(qs, kc_ref[...]), NEG_INF)
    m_c = jnp.max(s_c, axis=-1, keepdims=True)
    m_c = jnp.where(m_c == NEG_INF, 0.0, m_c)
    e_c = jnp.exp(s_c - m_c)
    d_c = jnp.sum(e_c, axis=-1, keepdims=True)
    p_c = (e_c / jnp.where(d_c > 0, d_c, 1.0)).astype(BF16)
    o_c = jnp.dot(p_c, vc_ref[...], preferred_element_type=F32)
    p_heads = jnp.concatenate([p_c[r * tq:(r + 1) * tq] for r in range(A_REP)], axis=1)
    imp = jnp.dot(p_heads, cover_ref[...], preferred_element_type=F32)

    s_idx = lax.broadcasted_iota(jnp.int32, (tq, 128), 1)
    qblk = (q0 + lax.broadcasted_iota(jnp.int32, (tq, 128), 0)) >> 6
    forced = (s_idx == 0) | (s_idx == qblk) | (s_idx == qblk - 1)
    allowed = s_idx <= qblk
    score = jnp.where(allowed, imp + jnp.where(forced, FORCE_BONUS, 0.0), NEG_INF)
    rank = jnp.zeros((tq, 128), F32)
    for j in range(32):
        col = score[:, j:j + 1]
        beats = (col > score) | ((col == score) & (s_idx > j))
        rank = rank + jnp.where(beats, 1.0, 0.0)
    sel = jnp.where((rank < SEL_TOPK) & allowed, 1.0, 0.0).astype(BF16)
    sel = jnp.concatenate([sel] * A_REP, axis=0)

    prev0 = pl.multiple_of(jnp.maximum(q0 - tq, 0), tq)
    near_pos = jnp.where(col_idx(NSA_NEAR) < tq, prev0, q0 - tq) + col_idx(NSA_NEAR)
    near_ok = (near_pos <= row_pos(NSA_NEAR)) & ((col_idx(NSA_NEAR) >= tq) | (qt > 0))
    btile = btile_ref[...]

    def near(k_ref, v_ref, extra_mask):
        k = jnp.concatenate([k_ref[pl.ds(prev0, tq), :], k_ref[pl.ds(q0, tq), :]], axis=0)
        v = jnp.concatenate([v_ref[pl.ds(prev0, tq), :], v_ref[pl.ds(q0, tq), :]], axis=0)
        mask = near_ok if extra_mask is None else (near_ok & extra_mask)
        return _softmax_start(jnp.where(mask, _dot_nt(qs, k) + btile, NEG_INF), v)

    expand_near = jnp.concatenate([_block_expand(prev0, tq), _block_expand(q0, tq)], axis=1)
    sel_near = jnp.dot(sel, expand_near, preferred_element_type=F32) > 0.5

    far_end = q0 - tq

    def sel_far(i, carry):
        k0 = pl.multiple_of(i * NSA_FAR_TILE, NSA_FAR_TILE)
        s = _dot_nt(qs, ks_ref[pl.ds(k0, NSA_FAR_TILE), :])
        hit = jnp.dot(sel, _block_expand(k0, NSA_FAR_TILE), preferred_element_type=F32) > 0.5
        mask = hit & (k0 + col_idx(NSA_FAR_TILE) < far_end)
        return _softmax_more(carry, jnp.where(mask, s, NEG_INF), vs_ref[pl.ds(k0, NSA_FAR_TILE), :])

    n_far = (jnp.maximum(far_end, 0) + NSA_FAR_TILE - 1) // NSA_FAR_TILE
    _, l_s, acc_s = lax.fori_loop(0, n_far, sel_far, near(ks_ref, vs_ref, sel_near))
    o_s = acc_s / l_s

    w_far = WIN_A - tq
    wf0 = pl.multiple_of(jnp.maximum(q0 - WIN_A, 0), tq)
    wpos = wf0 + col_idx(w_far)
    wmask = (row_pos(w_far) - wpos < WIN_A) & (wpos < far_end)
    s_w = jnp.where(wmask, _dot_nt(qs, kw_ref[pl.ds(wf0, w_far), :]), NEG_INF)
    _, l_w, acc_w = _softmax_more(near(kw_ref, vw_ref, None), s_w, vw_ref[pl.ds(wf0, w_far), :])
    o_w = acc_w / l_w

    gates = gate_ref[...]
    outs = []
    for r in range(A_REP):
        sl = slice(r * tq, (r + 1) * tq)
        outs.append(gates[:, 3 * r:3 * r + 1] * o_c[sl] + gates[:, 3 * r + 1:3 * r + 2] * o_s[sl]
                    + gates[:, 3 * r + 2:3 * r + 3] * o_w[sl])
    o_ref[...] = jnp.concatenate(outs, axis=1)


def _near_bucket_table(tq):
    i = np.arange(tq)[:, None]
    j = np.arange(2 * tq)[None, :]
    dist = np.where(j < tq, tq + i - j, i - (j - tq))
    n = np.maximum(dist, 0)
    exact = NUM_BUCKETS // 2
    nf = np.maximum(n, exact).astype(np.float64)
    large = exact + (np.log(nf / exact) / math.log(MAX_DISTANCE / exact) * (NUM_BUCKETS - exact)).astype(np.int64)
    return np.where(n < exact, n, np.minimum(large, NUM_BUCKETS - 1)).astype(np.int32)


def _cover_matrix(n_cmp_pad, n_sel):
    n = np.arange(n_cmp_pad)
    c_start = n * CMP_STRIDE
    c_end = c_start + CMP_LEN - 1
    s_start = np.arange(128) * SEL_BLOCK
    cover = (c_start[:, None] < s_start[None, :] + SEL_BLOCK) & (c_end[:, None] >= s_start[None, :])
    cover &= (np.arange(128)[None, :] < n_sel)
    return np.tile(cover.astype(np.float32), (A_REP, 1))


def _nsa_prompt_attend(q, gates, kc, vc, ks, vs, kw, vw, tab):
    b, t, _ = q.shape
    tq = NSA_Q_TILE
    assert t % NSA_FAR_TILE == 0 and t // SEL_BLOCK <= 32 and t >= WIN_A
    near = tab[_near_bucket_table(tq)] - tab[NUM_BUCKETS - 1]
    btile = jnp.transpose(near, (2, 3, 0, 1)).reshape(A_KV, A_REP * tq, 2 * tq)
    cover = jnp.asarray(_cover_matrix(128, t // SEL_BLOCK), BF16)
    kv_spec = lambda rows: pl.BlockSpec((None, None, rows, HEAD_DIM), lambda i, g, j: (i, g, 0, 0))
    return pl.pallas_call(
        functools.partial(_nsa_prompt_body, tq=tq),
        grid=(b, A_KV, t // tq),
        in_specs=[
            pl.BlockSpec((None, tq, A_REP * HEAD_DIM), lambda i, g, j: (i, j, g)),
            pl.BlockSpec((None, None, tq, 3 * A_REP), lambda i, g, j: (i, g, j, 0)),
            kv_spec(128), kv_spec(128), kv_spec(t), kv_spec(t), kv_spec(t), kv_spec(t),
            pl.BlockSpec((None, A_REP * tq, 2 * tq), lambda i, g, j: (g, 0, 0)),
            pl.BlockSpec((A_REP * 128, 128), lambda i, g, j: (0, 0)),
        ],
        out_specs=pl.BlockSpec((None, tq, A_REP * HEAD_DIM), lambda i, g, j: (i, j, g)),
        out_shape=jax.ShapeDtypeStruct((b, t, A_Q), F32),
        compiler_params=pltpu.CompilerParams(
            dimension_semantics=("parallel", "parallel", "arbitrary"),
            vmem_limit_bytes=V7X_VMEM_LIMIT_BYTES),
        name="nsa_prompt",
    )(q, gates, kc, vc, ks, vs, kw, vw, btile, cover)


def rms_norm(x, g):
    xf = x.astype(jnp.float32)
    y = xf * lax.rsqrt(jnp.mean(xf * xf, axis=-1, keepdims=True) + EPS)
    return (y * g.astype(jnp.float32)).astype(x.dtype)


def rel_bucket(dist):
    n = jnp.maximum(dist, 0)
    exact = NUM_BUCKETS // 2
    nf = jnp.maximum(n, exact).astype(jnp.float32)
    large = exact + (jnp.log(nf / exact) / math.log(MAX_DISTANCE / exact) * (NUM_BUCKETS - exact)).astype(jnp.int32)
    return jnp.where(n < exact, n, jnp.minimum(large, NUM_BUCKETS - 1))


def masked_softmax(s, mask):
    s = jnp.where(mask, s.astype(jnp.float32), -jnp.inf)
    m = jnp.max(s, axis=-1, keepdims=True)
    m = jnp.where(jnp.isfinite(m), m, 0.0)
    e = jnp.exp(s - m)
    d = jnp.sum(e, axis=-1, keepdims=True)
    return e / jnp.where(d > 0, d, 1.0)


def compress(rows, pos, w1, w2):
    b, t, g, dh = rows.shape
    r = CMP_LEN // CMP_STRIDE
    nch = t // CMP_STRIDE
    chunks = rows[:, :nch * CMP_STRIDE].reshape(b, nch, CMP_STRIDE, g, dh)
    w1b = w1.reshape(r, CMP_STRIDE, dh, w1.shape[-1])
    nc = nch - r + 1
    hid = jnp.einsum('ld,ldh->h', pos, w1.reshape(CMP_LEN, dh, w1.shape[-1]))
    for j in range(r):
        hid = hid + jnp.einsum('bnsgd,sdh->bngh', chunks[:, j:j + nc], w1b[j])
    return jnp.einsum('bngh,hd->bngd', jax.nn.gelu(hid), w2)


def compressed_kv(k_rows, v_rows, cmp_pos, cmp_w1, cmp_w2, k_gain):
    kc = rms_norm(compress(k_rows, cmp_pos[0], cmp_w1[0], cmp_w2[0]), k_gain)
    vc = compress(v_rows, cmp_pos[1], cmp_w1[1], cmp_w2[1])
    c_end = jnp.arange(kc.shape[1]) * CMP_STRIDE + (CMP_LEN - 1)
    return kc, vc, c_end


def nsa_attend(q, q_pos, kc, vc, c_end, n_sel, gather_sel, kw, vw, w_pos, gates, tab):
    scale = HEAD_DIM ** -0.5
    g_n = q.shape[2]
    s_c = jnp.einsum('bqgrd,bngd->bgrqn', q, kc) * scale
    p_c = masked_softmax(s_c, c_end[None, :] <= q_pos[:, None])
    o_c = jnp.einsum('bgrqn,bngd->bqgrd', p_c.astype(vc.dtype), vc)
    s_start = jnp.arange(n_sel) * SEL_BLOCK
    c_start = c_end - (CMP_LEN - 1)
    cover = ((c_start[:, None] < s_start[None, :] + SEL_BLOCK) & (c_end[:, None] >= s_start[None, :])).astype(jnp.float32)
    imp = jnp.einsum('bgrqn,ns->bgqs', p_c, cover)
    qblk = q_pos[:, None] // SEL_BLOCK
    sblk = jnp.arange(n_sel)[None, :]
    forced = ((sblk == 0) | (sblk == qblk) | (sblk == qblk - 1)).astype(jnp.float32)
    score = jnp.where(sblk <= qblk, imp + FORCE_BONUS * forced, -jnp.inf)
    _, idx = lax.top_k(score, min(SEL_TOPK, n_sel))
    k_sel, v_sel = gather_sel(idx)
    kpos = idx[..., None] * SEL_BLOCK + jnp.arange(SEL_BLOCK)
    dist = q_pos[:, None, None] - kpos
    tab_g = jnp.transpose(tab, (1, 0, 2))
    gi = jnp.arange(g_n)[None, :, None, None, None]
    bias = jnp.moveaxis(tab_g[gi, rel_bucket(dist)], -1, 3)
    s_s = jnp.einsum('bqgrd,bgqkjd->bgqrkj', q, k_sel) * scale + bias
    shp = s_s.shape
    mask = jnp.broadcast_to((dist >= 0)[:, :, :, None], shp)
    p_s = masked_softmax(s_s.reshape(shp[:4] + (-1,)), mask.reshape(shp[:4] + (-1,))).reshape(shp)
    o_s = jnp.einsum('bgqrkj,bgqkjd->bqgrd', p_s.astype(v_sel.dtype), v_sel)
    wd = q_pos[:, None] - w_pos[None, :]
    wmask = (wd >= 0) & (wd < WIN_A) & (w_pos[None, :] >= 0)
    wbias = jnp.transpose(tab[rel_bucket(wd)], (2, 3, 0, 1))
    s_w = jnp.einsum('bqgrd,bwgd->bgrqw', q, kw) * scale + wbias
    p_w = masked_softmax(s_w, wmask)
    o_w = jnp.einsum('bgrqw,bwgd->bqgrd', p_w.astype(vw.dtype), vw)
    return gates[..., 0:1] * o_c + gates[..., 1:2] * o_s + gates[..., 2:3] * o_w


def nsa_prompt(q, gates, kc_rows, vc_rows, ks, vs, kw, vw, cmp_pos, cmp_w1, cmp_w2, k_gain, tab):
    b, t = q.shape[:2]
    kc, vc, c_end = compressed_kv(kc_rows, vc_rows, cmp_pos, cmp_w1, cmp_w2, k_gain)
    n_sel = t // SEL_BLOCK
    ks_b = ks.reshape(b, n_sel, SEL_BLOCK, A_KV, HEAD_DIM).transpose(0, 3, 1, 2, 4)
    vs_b = vs.reshape(b, n_sel, SEL_BLOCK, A_KV, HEAD_DIM).transpose(0, 3, 1, 2, 4)
    bi = jnp.arange(b)[:, None, None, None]
    gi = jnp.arange(A_KV)[None, :, None, None]

    def gather_sel(idx):
        return ks_b[bi, gi, idx], vs_b[bi, gi, idx]

    pad = ((0, 0), (WIN_A, 0), (0, 0), (0, 0))
    kw_p, vw_p = jnp.pad(kw, pad), jnp.pad(vw, pad)

    def chunk(start):
        q_pos = start + jnp.arange(NSA_QBLK)
        qc = lax.dynamic_slice_in_dim(q, start, NSA_QBLK, axis=1)
        gc = lax.dynamic_slice_in_dim(gates, start, NSA_QBLK, axis=1)
        kwc = lax.dynamic_slice_in_dim(kw_p, start, NSA_QBLK + WIN_A, axis=1)
        vwc = lax.dynamic_slice_in_dim(vw_p, start, NSA_QBLK + WIN_A, axis=1)
        w_pos = start - WIN_A + jnp.arange(NSA_QBLK + WIN_A)
        return nsa_attend(qc, q_pos, kc, vc, c_end, n_sel, gather_sel, kwc, vwc, w_pos, gc, tab)

    out = lax.map(chunk, jnp.arange(t // NSA_QBLK) * NSA_QBLK)
    return jnp.moveaxis(out, 0, 1).reshape(b, t, A_KV, A_REP, HEAD_DIM)


def _nsa_prompt_glue(q, gates, kc_rows, vc_rows, ks, vs, kw, vw, cmp_pos, cmp_w1, cmp_w2, k_gain, tab):
    b, t = q.shape[:2]
    kc, vc, _ = compressed_kv(kc_rows, vc_rows, cmp_pos, cmp_w1, cmp_w2, k_gain)
    pad = lambda x: jnp.pad(x, ((0, 0), (0, 128 - x.shape[1]), (0, 0), (0, 0)))
    heads_first = lambda x: jnp.transpose(x, (0, 2, 1, 3)).astype(BF16)
    o = _nsa_prompt_attend(q.reshape(b, t, A_Q),
                           jnp.transpose(gates.reshape(b, t, A_KV, 3 * A_REP), (0, 2, 1, 3)),
                           heads_first(pad(kc)), heads_first(pad(vc)), heads_first(ks), heads_first(vs),
                           heads_first(kw), heads_first(vw), tab)
    return o.reshape(b, t, A_KV, A_REP, HEAD_DIM)


def nsa_sample(q, gates, new_c, new_s, win_all, past, pool_cmp, pool_sel, page_table,
               cmp_pos, cmp_w1, cmp_w2, k_gain, tab):
    db, s = q.shape[:2]
    past_c = pool_cmp[page_table].reshape(db, past, 2, A_KV, HEAD_DIM)
    rows_c = jnp.concatenate([past_c, new_c], axis=1)
    kc, vc, c_end = compressed_kv(rows_c[:, :, 0], rows_c[:, :, 1], cmp_pos, cmp_w1, cmp_w2, k_gain)
    q_pos = past + jnp.arange(s)
    n_past_blk = past // SEL_BLOCK
    n_new_blk = -(-s // SEL_BLOCK)
    n_sel = n_past_blk + n_new_blk
    bpp = PAGE_SIZE // SEL_BLOCK
    pool_b = pool_sel.reshape(pool_sel.shape[0], bpp, SEL_BLOCK, 2, A_KV, HEAD_DIM)
    new_b = jnp.pad(new_s, ((0, 0), (0, n_new_blk * SEL_BLOCK - s), (0, 0), (0, 0), (0, 0)))
    new_b = new_b.reshape(db, n_new_blk, SEL_BLOCK, 2, A_KV, HEAD_DIM)
    bi = jnp.arange(db)[:, None, None, None]
    gi = jnp.arange(A_KV)[None, :, None, None]

    def gather_sel(idx):
        pidx = jnp.clip(idx, 0, n_past_blk - 1)
        phys = page_table[bi, pidx // bpp]
        pool_f = pool_b.reshape((-1,) + pool_b.shape[2:])
        both = pool_f[phys * bpp + pidx % bpp]
        from_pool = jnp.stack([both[:, g, ..., g, :] for g in range(A_KV)], axis=1)
        from_new = new_b[bi, jnp.clip(idx - n_past_blk, 0, n_new_blk - 1), :, :, gi]
        kv = jnp.where((idx < n_past_blk)[..., None, None, None], from_pool, from_new)
        return kv[..., 0, :], kv[..., 1, :]

    w_len = win_all.shape[1]
    w_pos = past + s - w_len + jnp.arange(w_len)
    return nsa_attend(q, q_pos, kc, vc, c_end, n_sel, gather_sel, win_all[:, :, 0], win_all[:, :, 1],
                      w_pos, gates, tab)


def _ssm_combine(e1, e2):
    a1r, a1i, b1r, b1i = e1
    a2r, a2i, b2r, b2i = e2
    return (a2r * a1r - a2i * a1i, a2r * a1i + a2i * a1r,
            a2r * b1r - a2i * b1i + b2r, a2r * b1i + a2i * b1r + b2i)


def s5_mix(u, h_re, h_im, a_re, a_im, log_dt, b_re, b_im, c_re, c_im, d, w_glu, b_glu):
    bsz, t, _ = u.shape
    f32 = jnp.float32
    uf = u.astype(f32)
    ug = uf.reshape(bsz, t, B_GROUPS, B_GROUP)
    dt = jnp.exp(log_dt.astype(f32))[:, None]
    ar, ai = a_re.astype(f32), a_im.astype(f32)
    mag = jnp.exp(ar * dt)
    abr, abi = mag * jnp.cos(ai * dt), mag * jnp.sin(ai * dt)
    den = ar * ar + ai * ai
    wr = ((abr - 1.0) * ar + abi * ai) / den
    wi = (abi * ar - (abr - 1.0) * ai) / den
    bu_r = jnp.einsum('gpc,btgc->btgp', b_re.astype(f32), ug)
    bu_i = jnp.einsum('gpc,btgc->btgp', b_im.astype(f32), ug)
    x_r = wr * bu_r - wi * bu_i
    x_i = wr * bu_i + wi * bu_r
    hr, hi = h_re.astype(f32), h_im.astype(f32)
    x_r = x_r.at[:, 0].add(abr * hr - abi * hi)
    x_i = x_i.at[:, 0].add(abr * hi + abi * hr)
    a_r = jnp.broadcast_to(abr, x_r.shape)
    a_i = jnp.broadcast_to(abi, x_i.shape)
    _, _, s_r, s_i = lax.associative_scan(_ssm_combine, (a_r, a_i, x_r, x_i), axis=1)
    y = jnp.einsum('gcp,btgp->btgc', c_re.astype(f32), s_r) - jnp.einsum('gcp,btgp->btgc', c_im.astype(f32), s_i)
    y = y.reshape(bsz, t, B_WIDTH) + d.astype(f32) * uf
    z = jax.nn.gelu(y)
    out = z * jax.nn.sigmoid(z @ w_glu.astype(f32) + b_glu.astype(f32))
    return out.astype(u.dtype), s_r[:, -1], s_i[:, -1]


def nsa_s5_inputs(xn, w_in, q_gain, k_gain):
    b, t, _ = xn.shape
    sizes = [A_Q] + [A_KVW] * 6 + [A_GATE, B_WIDTH]
    cuts = [int(c) for c in np.cumsum(sizes)[:-1]]
    q, kc, vc, ks, vs, kw, vw, gl, u = jnp.split(xn @ w_in, cuts, axis=-1)
    heads = lambda z: z.reshape(b, t, A_KV, HEAD_DIM)
    q = rms_norm(q.reshape(b, t, A_KV, A_REP, HEAD_DIM), q_gain)
    gates = jax.nn.sigmoid(gl).reshape(b, t, A_KV, A_REP, 3)
    return (q, rms_norm(heads(kc), k_gain), heads(vc), rms_norm(heads(ks), k_gain), heads(vs),
            rms_norm(heads(kw), k_gain), heads(vw), gates, u)


def swa_inputs(xn, w_in, q_gain, k_gain):
    b, t, _ = xn.shape
    q, k, v = jnp.split(xn @ w_in, [C_HEADS * HEAD_DIM, (C_HEADS + C_KV) * HEAD_DIM], axis=-1)
    q = rms_norm(q.reshape(b, t, C_KV, C_REP, HEAD_DIM), q_gain)
    k = rms_norm(k.reshape(b, t, C_KV, HEAD_DIM), k_gain)
    return q, k, v.reshape(b, t, C_KV, HEAD_DIM)


def swa_attend(q, q_pos, k, v, k_pos, sinks, tab):
    dist = q_pos[:, :, None] - k_pos[:, None, :]
    mask = (dist >= 0) & (dist < WIN_C) & (k_pos[:, None, :] >= 0)
    bias = jnp.transpose(tab[rel_bucket(dist)], (3, 4, 0, 1, 2))
    s = jnp.einsum('bnqgrd,bnkgd->bgrnqk', q, k).astype(jnp.float32) * (HEAD_DIM ** -0.5) + bias
    s = jnp.where(mask, s, -jnp.inf)
    sink = sinks.astype(jnp.float32).reshape(C_KV, C_REP)[:, :, None, None, None]
    m = jnp.maximum(jnp.max(s, axis=-1, keepdims=True), sink)
    e = jnp.exp(s - m)
    p = e / (jnp.sum(e, axis=-1, keepdims=True) + jnp.exp(sink - m))
    return jnp.einsum('bgrnqk,bnkgd->bnqgrd', p.astype(v.dtype), v)


def swa_prompt_attend(q, k, v, sinks, tab):
    b, t = q.shape[:2]
    nb = t // WIN_C
    qb = q.reshape(b, nb, WIN_C, C_KV, C_REP, HEAD_DIM)

    def band(x):
        xb = x.reshape(b, nb, WIN_C, C_KV, HEAD_DIM)
        prev = jnp.pad(xb[:, :-1], ((0, 0), (1, 0), (0, 0), (0, 0), (0, 0)))
        return jnp.concatenate([prev, xb], axis=2)

    q_pos = jnp.arange(t).reshape(nb, WIN_C)
    k_pos = (jnp.arange(nb)[:, None] - 1) * WIN_C + jnp.arange(2 * WIN_C)[None, :]
    o = swa_attend(qb, q_pos, band(k), band(v), k_pos, sinks, tab)
    return o.reshape(b, t, C_HEADS * HEAD_DIM)


def kernel(x_prompt, x_sample, cache_nsa_cmp, cache_nsa_sel, cache_nsa_win, state_s5_re, state_s5_im,
           cache_swa, state_ffn_conv, page_table, rel_bias, norm_mix, norm_ffn, a_w_in, a_w_out,
           nsa_q_gain, nsa_k_gain, nsa_cmp_pos, nsa_cmp_w1, nsa_cmp_w2, s5_a_re, s5_a_im, s5_log_dt,
           s5_b_re, s5_b_im, s5_c_re, s5_c_im, s5_d, s5_w_glu, s5_b_glu, c_w_in, c_w_out, c_q_gain,
           c_k_gain, c_sinks, ffn_w_up, ffn_w_gate, ffn_conv_w, ffn_conv_b, ffn_w_down):
    bp, tp, _ = x_prompt.shape
    bs, ts, _ = x_sample.shape
    past = page_table.shape[1] * PAGE_SIZE
    tab_a = rel_bias[:, :A_HEADS].reshape(NUM_BUCKETS, A_KV, A_REP)
    tab_c = rel_bias[:, :C_HEADS].reshape(NUM_BUCKETS, C_KV, C_REP)
    hp, hs = x_prompt, x_sample
    cmp_p, cmp_s, sel_p, sel_s, win_p, win_s = [], [], [], [], [], []
    s5r_p, s5i_p, s5r_s, s5i_s = [], [], [], []
    swa_p, swa_s, conv_p, conv_s = [], [], [], []
    for layer in range(DEPTH):
        i = layer // 2
        xp = rms_norm(hp, norm_mix[layer])
        xs = rms_norm(hs, norm_mix[layer])
        if layer % 2 == 0:
            qp, kcp, vcp, ksp, vsp, kwp, vwp, gp, up = nsa_s5_inputs(xp, a_w_in[i], nsa_q_gain[i], nsa_k_gain[i])
            qs, kcs, vcs, kss, vss, kws, vws, gs, us = nsa_s5_inputs(xs, a_w_in[i], nsa_q_gain[i], nsa_k_gain[i])
            cmp_args = (nsa_cmp_pos[i], nsa_cmp_w1[i], nsa_cmp_w2[i], nsa_k_gain[i], tab_a)
            o_ap = _nsa_prompt_glue(qp, gp, kcp, vcp, ksp, vsp, kwp, vwp, *cmp_args)
            new_c = jnp.stack([kcs, vcs], axis=2)
            new_sel = jnp.stack([kss, vss], axis=2)
            win_all = jnp.concatenate([cache_nsa_win[i].astype(kws.dtype), jnp.stack([kws, vws], axis=2)], axis=1)
            o_as = nsa_sample(qs, gs, new_c, new_sel, win_all, past, cache_nsa_cmp[i], cache_nsa_sel[i],
                              page_table, *cmp_args)
            s5_args = (s5_a_re[i], s5_a_im[i], s5_log_dt[i], s5_b_re[i], s5_b_im[i], s5_c_re[i], s5_c_im[i],
                       s5_d[i], s5_w_glu[i], s5_b_glu[i])
            h0 = jnp.zeros((bp, B_GROUPS, B_STATE), jnp.float32)
            o_bp, hr_p, hi_p = s5_mix(up, h0, h0, *s5_args)
            o_bs, hr_s, hi_s = s5_mix(us, state_s5_re[i], state_s5_im[i], *s5_args)
            hp = hp + jnp.concatenate([o_ap.reshape(bp, tp, A_Q), o_bp], axis=-1) @ a_w_out[i]
            hs = hs + jnp.concatenate([o_as.reshape(bs, ts, A_Q), o_bs], axis=-1) @ a_w_out[i]
            cmp_p.append(jnp.stack([kcp, vcp], axis=2))
            cmp_s.append(new_c)
            sel_p.append(jnp.stack([ksp, vsp], axis=2))
            sel_s.append(new_sel)
            win_p.append(jnp.stack([kwp, vwp], axis=2)[:, -min(WIN_A, tp):])
            win_s.append(win_all[:, -min(WIN_A, win_all.shape[1]):])
            s5r_p.append(hr_p)
            s5i_p.append(hi_p)
            s5r_s.append(hr_s)
            s5i_s.append(hi_s)
        else:
            qp, kp, vp = swa_inputs(xp, c_w_in[i], c_q_gain[i], c_k_gain[i])
            qs, k_s, v_s = swa_inputs(xs, c_w_in[i], c_q_gain[i], c_k_gain[i])
            o_cp = swa_prompt_attend(qp, kp, vp, c_sinks[i], tab_c)
            kv_all = jnp.concatenate([cache_swa[i].astype(k_s.dtype), jnp.stack([k_s, v_s], axis=2)], axis=1)
            kv_len = kv_all.shape[1]
            q_pos = (past + jnp.arange(ts))[None, :]
            k_pos = (past + ts - kv_len + jnp.arange(kv_len))[None, :]
            o_cs = swa_attend(qs[:, None], q_pos, kv_all[:, None, :, 0], kv_all[:, None, :, 1], k_pos,
                              c_sinks[i], tab_c).reshape(bs, ts, C_HEADS * HEAD_DIM)
            hp = hp + o_cp @ c_w_out[i]
            hs = hs + o_cs @ c_w_out[i]
            swa_p.append(jnp.stack([kp, vp], axis=2)[:, -min(WIN_C, tp):])
            swa_s.append(kv_all[:, -min(WIN_C, kv_len):])
        wup = ffn_w_up[layer].astype(BF16)
        wgate = ffn_w_gate[layer].astype(BF16)
        wdown = ffn_w_down[layer].astype(BF16)
        gain = norm_ffn[layer].reshape(1, D_MODEL)
        cw = ffn_conv_w[layer]
        cb = ffn_conv_b[layer].reshape(1, D_FF)
        hp, cp = _ffn_prompt(hp, gain, jnp.zeros((bp, CONV_W - 1, D_FF), F32), wup, wgate, cw, cb, wdown)
        hs2, cs = _ffn_sample(hs.reshape(bs * ts, D_MODEL), gain,
                              state_ffn_conv[layer].reshape(bs, (CONV_W - 1) * D_FF), wup, wgate, cw, cb, wdown)
        hs = hs2.reshape(bs, ts, D_MODEL)
        conv_p.append(cp)
        conv_s.append(cs.reshape(bs, CONV_W - 1, D_FF))
    y_prompt, y_sample = hp, hs
    return (y_prompt, y_sample, jnp.stack(cmp_p), jnp.stack(cmp_s), jnp.stack(sel_p), jnp.stack(sel_s),
            jnp.stack(win_p), jnp.stack(win_s), jnp.stack(s5r_p), jnp.stack(s5i_p), jnp.stack(s5r_s),
            jnp.stack(s5i_s), jnp.stack(swa_p), jnp.stack(swa_s), jnp.stack(conv_p), jnp.stack(conv_s))
```

```python
import functools
import math

import jax
import jax.numpy as jnp
import numpy as np
from jax import lax
from jax.experimental import pallas as pl
from jax.experimental.pallas import tpu as pltpu

D_MODEL = 1024
DEPTH = 2
PAGE_SIZE = 128
HEAD_DIM = 64
A_HEADS = 8
A_KV = 2
A_REP = A_HEADS // A_KV
A_Q = A_HEADS * HEAD_DIM
A_KVW = A_KV * HEAD_DIM
A_GATE = 3 * A_HEADS
CMP_LEN = 32
CMP_STRIDE = 16
SEL_BLOCK = 64
SEL_TOPK = 16
WIN_A = 512
NSA_QBLK = 64
FORCE_BONUS = 1000.0
B_WIDTH = D_MODEL // 2
B_GROUP = 16
B_GROUPS = B_WIDTH // B_GROUP
B_STATE = 64
C_HEADS = D_MODEL // HEAD_DIM
C_KV = 2
C_REP = C_HEADS // C_KV
WIN_C = 128
NUM_BUCKETS = 32
MAX_DISTANCE = 128
D_FF = 2816
CONV_W = 3
EPS = 1e-6

F32 = jnp.float32
BF16 = jnp.bfloat16

V7X_VMEM_LIMIT_BYTES = 56 * 1024 * 1024
FFN_ROW_TILE = 512
FFN_COL_CHUNK = 1408


def _gelu_tanh(x):
    return 0.5 * x * (1.0 + jnp.tanh(math.sqrt(2.0 / math.pi) * (x + 0.044715 * (x * x * x))))


def _rms_rows(x, gain):
    return x * lax.rsqrt(jnp.mean(x * x, axis=-1, keepdims=True) + EPS) * gain


def _const_spec(shape):
    zeros = (0,) * len(shape)
    return pl.BlockSpec(shape, lambda *_: zeros, pipeline_mode=pl.Buffered(1))


def _ffn_prompt_body(x_ref, gain_ref, prev_ref, wup_ref, wgate_ref, cw_ref, cb_ref, wdown_ref,
                     y_ref, cs_ref, hbuf_ref, *, tm, ffc):
    t = pl.program_id(1)
    x = x_ref[...]
    xn = _rms_rows(x, gain_ref[...]).astype(BF16)
    acc = x
    for c in range(D_FF // ffc):
        lo = c * ffc
        h = jnp.dot(xn, wup_ref[:, lo:lo + ffc], preferred_element_type=F32)
        g = jnp.dot(xn, wgate_ref[:, lo:lo + ffc], preferred_element_type=F32)

        @pl.when(t == 0)
        def _():
            hbuf_ref[c, 6:8, :] = prev_ref[:, lo:lo + ffc]

        hbuf_ref[c, 8:8 + tm, :] = h
        hm1 = hbuf_ref[c, 7:7 + tm, :]
        hm2 = hbuf_ref[c, 6:6 + tm, :]
        cw = cw_ref[:, lo:lo + ffc]
        hc = cw[0:1] * hm2 + cw[1:2] * hm1 + cw[2:3] * h + cb_ref[:, lo:lo + ffc]
        a = (_gelu_tanh(hc) * g).astype(BF16)
        acc = acc + jnp.dot(a, wdown_ref[lo:lo + ffc, :], preferred_element_type=F32)
        hbuf_ref[c, 0:8, :] = h[tm - 8:tm, :]
        cs_ref[:, lo:lo + ffc] = h[tm - 2:tm, :]
    y_ref[...] = acc


def _ffn_prompt(x, gain, prev, wup, wgate, cw, cb, wdown):
    b, t, d = x.shape
    tm, ffc = FFN_ROW_TILE, FFN_COL_CHUNK
    body = functools.partial(_ffn_prompt_body, tm=tm, ffc=ffc)
    return pl.pallas_call(
        body,
        grid=(b, t // tm),
        in_specs=[
            pl.BlockSpec((None, tm, d), lambda i, j: (i, j, 0)),
            _const_spec((1, d)),
            pl.BlockSpec((None, CONV_W - 1, D_FF), lambda i, j: (i, 0, 0)),
            _const_spec((d, D_FF)),
            _const_spec((d, D_FF)),
            _const_spec((CONV_W, D_FF)),
            _const_spec((1, D_FF)),
            _const_spec((D_FF, d)),
        ],
        out_specs=[
            pl.BlockSpec((None, tm, d), lambda i, j: (i, j, 0)),
            pl.BlockSpec((None, CONV_W - 1, D_FF), lambda i, j: (i, 0, 0)),
        ],
        out_shape=[
            jax.ShapeDtypeStruct((b, t, d), F32),
            jax.ShapeDtypeStruct((b, CONV_W - 1, D_FF), F32),
        ],
        scratch_shapes=[pltpu.VMEM((D_FF // ffc, 8 + tm, ffc), F32)],
        compiler_params=pltpu.CompilerParams(
            dimension_semantics=("parallel", "arbitrary"),
            vmem_limit_bytes=V7X_VMEM_LIMIT_BYTES),
        name="ffn_prompt",
    )(x, gain, prev, wup, wgate, cw, cb, wdown)


def _ffn_sample_body(x_ref, gain_ref, prev_ref, wup_ref, wgate_ref, cw_ref, cb_ref, wdown_ref,
                     y_ref, cs_ref, *, ffc):
    x = x_ref[...]
    xn = _rms_rows(x, gain_ref[...]).astype(BF16)
    acc = x
    for c in range(D_FF // ffc):
        lo = c * ffc
        h = jnp.dot(xn, wup_ref[:, lo:lo + ffc], preferred_element_type=F32)
        g = jnp.dot(xn, wgate_ref[:, lo:lo + ffc], preferred_element_type=F32)
        hm2 = prev_ref[:, lo:lo + ffc]
        hm1 = prev_ref[:, D_FF + lo:D_FF + lo + ffc]
        cw = cw_ref[:, lo:lo + ffc]
        hc = cw[0:1] * hm2 + cw[1:2] * hm1 + cw[2:3] * h + cb_ref[:, lo:lo + ffc]
        a = (_gelu_tanh(hc) * g).astype(BF16)
        acc = acc + jnp.dot(a, wdown_ref[lo:lo + ffc, :], preferred_element_type=F32)
        cs_ref[:, lo:lo + ffc] = hm1
        cs_ref[:, D_FF + lo:D_FF + lo + ffc] = h
    y_ref[...] = acc


def _ffn_sample(x, gain, prev, wup, wgate, cw, cb, wdown):
    n, d = x.shape
    body = functools.partial(_ffn_sample_body, ffc=FFN_COL_CHUNK)
    return pl.pallas_call(
        body,
        grid=(1,),
        in_specs=[
            _const_spec((n, d)),
            _const_spec((1, d)),
            _const_spec((n, (CONV_W - 1) * D_FF)),
            _const_spec((d, D_FF)),
            _const_spec((d, D_FF)),
            _const_spec((CONV_W, D_FF)),
            _const_spec((1, D_FF)),
            _const_spec((D_FF, d)),
        ],
        out_specs=[
            pl.BlockSpec((n, d), lambda i: (0, 0)),
            pl.BlockSpec((n, (CONV_W - 1) * D_FF), lambda i: (0, 0)),
        ],
        out_shape=[
            jax.ShapeDtypeStruct((n, d), F32),
            jax.ShapeDtypeStruct((n, (CONV_W - 1) * D_FF), F32),
        ],
        compiler_params=pltpu.CompilerParams(
            dimension_semantics=("arbitrary",),
            vmem_limit_bytes=V7X_VMEM_LIMIT_BYTES),
        name="ffn_sample",
    )(x, gain, prev, wup, wgate, cw, cb, wdown)


NSA_Q_TILE = 128
NSA_FAR_TILE = 512
NSA_NEAR = 2 * NSA_Q_TILE
NEG_INF = float("-inf")


def _dot_nt(a, b):
    return lax.dot_general(a, b, (((1,), (1,)), ((), ())), preferred_element_type=F32)


def _softmax_start(s, v):
    m = jnp.max(s, axis=-1, keepdims=True)
    e = jnp.exp(s - m)
    return m, jnp.sum(e, axis=-1, keepdims=True), jnp.dot(e.astype(BF16), v, preferred_element_type=F32)


def _softmax_more(carry, s, v):
    m, l, acc = carry
    m_new = jnp.maximum(m, jnp.max(s, axis=-1, keepdims=True))
    alpha = jnp.exp(m - m_new)
    e = jnp.exp(s - m_new)
    return (m_new, alpha * l + jnp.sum(e, axis=-1, keepdims=True),
            alpha * acc + jnp.dot(e.astype(BF16), v, preferred_element_type=F32))


def _block_expand(k0, width):
    kpos = k0 + lax.broadcasted_iota(jnp.int32, (128, width), 1)
    blk = lax.broadcasted_iota(jnp.int32, (128, width), 0)
    return jnp.where((kpos >> 6) == blk, 1.0, 0.0).astype(BF16)


def _nsa_prompt_body(q_ref, gate_ref, kc_ref, vc_ref, ks_ref, vs_ref, kw_ref, vw_ref, btile_ref, cover_ref,
                     o_ref, *, tq):
    qt = pl.program_id(2)
    q0 = pl.multiple_of(qt * tq, tq)
    rows = A_REP * tq
    q = q_ref[...]
    qs = jnp.concatenate([q[:, r * HEAD_DIM:(r + 1) * HEAD_DIM] for r in range(A_REP)], axis=0)
    qs = (qs * (HEAD_DIM ** -0.5)).astype(BF16)

    def row_pos(width):
        return q0 + (lax.broadcasted_iota(jnp.int32, (rows, width), 0) & (tq - 1))

    def col_idx(width):
        return lax.broadcasted_iota(jnp.int32, (rows, width), 1)

    n_idx = col_idx(128)
    valid_c = (n_idx * CMP_STRIDE + (CMP_LEN - 1) <= row_pos(128)) & (n_idx < 127)
    s_c = jnp.where(valid_c, _dot_nt(qs, kc_ref[...]), NEG_INF)
    m_c = jnp.max(s_c, axis=-1, keepdims=True)
    m_c = jnp.where(m_c == NEG_INF, 0.0, m_c)
    e_c = jnp.exp(s_c - m_c)
    d_c = jnp.sum(e_c, axis=-1, keepdims=True)
    p_c = (e_c / jnp.where(d_c > 0, d_c, 1.0)).astype(BF16)
    o_c = jnp.dot(p_c, vc_ref[...], preferred_element_type=F32)
    p_heads = jnp.concatenate([p_c[r * tq:(r + 1) * tq] for r in range(A_REP)], axis=1)
    imp = jnp.dot(p_heads, cover_ref[...], preferred_element_type=F32)

    s_idx = lax.broadcasted_iota(jnp.int32, (tq, 128), 1)
    qblk = (q0 + lax.broadcasted_iota(jnp.int32, (tq, 128), 0)) >> 6
    forced = (s_idx == 0) | (s_idx == qblk) | (s_idx == qblk - 1)
    allowed = s_idx <= qblk
    score = jnp.where(allowed, imp + jnp.where(forced, FORCE_BONUS, 0.0), NEG_INF)
    rank = jnp.zeros((tq, 128), F32)
    for j in range(32):
        col = score[:, j:j + 1]
        beats = (col > score) | ((col == score) & (s_idx > j))
        rank = rank + jnp.where(beats, 1.0, 0.0)
    sel = jnp.where((rank < SEL_TOPK) & allowed, 1.0, 0.0).astype(BF16)
    sel = jnp.concatenate([sel] * A_REP, axis=0)

    prev0 = pl.multiple_of(jnp.maximum(q0 - tq, 0), tq)
    near_pos = jnp.where(col_idx(NSA_NEAR) < tq, prev0, q0 - tq) + col_idx(NSA_NEAR)
    near_ok = (near_pos <= row_pos(NSA_NEAR)) & ((col_idx(NSA_NEAR) >= tq) | (qt > 0))
    btile = btile_ref[...]

    def near(k_ref, v_ref, extra_mask):
        k = jnp.concatenate([k_ref[pl.ds(prev0, tq), :], k_ref[pl.ds(q0, tq), :]], axis=0)
        v = jnp.concatenate([v_ref[pl.ds(prev0, tq), :], v_ref[pl.ds(q0, tq), :]], axis=0)
        mask = near_ok if extra_mask is None else (near_ok & extra_mask)
        return _softmax_start(jnp.where(mask, _dot_nt(qs, k) + btile, NEG_INF), v)

    expand_near = jnp.concatenate([_block_expand(prev0, tq), _block_expand(q0, tq)], axis=1)
    sel_near = jnp.dot(sel, expand_near, preferred_element_type=F32) > 0.5

    far_end = q0 - tq

    def sel_far(i, carry):
        k0 = pl.multiple_of(i * NSA_FAR_TILE, NSA_FAR_TILE)
        s = _dot_nt(qs, ks_ref[pl.ds(k0, NSA_FAR_TILE), :])
        hit = jnp.dot(sel, _block_expand(k0, NSA_FAR_TILE), preferred_element_type=F32) > 0.5
        mask = hit & (k0 + col_idx(NSA_FAR_TILE) < far_end)
        return _softmax_more(carry, jnp.where(mask, s, NEG_INF), vs_ref[pl.ds(k0, NSA_FAR_TILE), :])

    n_far = (jnp.maximum(far_end, 0) + NSA_FAR_TILE - 1) // NSA_FAR_TILE
    _, l_s, acc_s = lax.fori_loop(0, n_far, sel_far, near(ks_ref, vs_ref, sel_near))
    o_s = acc_s / l_s

    w_far = WIN_A - tq
    wf0 = pl.multiple_of(jnp.maximum(q0 - WIN_A, 0), tq)
    wpos = wf0 + col_idx(w_far)
    wmask = (row_pos(w_far) - wpos < WIN_A) & (wpos < far_end)
    s_w = jnp.where(wmask, _dot_nt(qs, kw_ref[pl.ds(wf0, w_far), :]), NEG_INF)
    _, l_w, acc_w = _softmax_more(near(kw_ref, vw_ref, None), s_w, vw_ref[pl.ds(wf0, w_far), :])
    o_w = acc_w / l_w

    gates = gate_ref[...]
    outs = []
    for r in range(A_REP):
        sl = slice(r * tq, (r + 1) * tq)
        outs.append(gates[:, 3 * r:3 * r + 1] * o_c[sl] + gates[:, 3 * r + 1:3 * r + 2] * o_s[sl]
                    + gates[:, 3 * r + 2:3 * r + 3] * o_w[sl])
    o_ref[...] = jnp.concatenate(outs, axis=1)


def _near_bucket_table(tq):
    i = np.arange(tq)[:, None]
    j = np.arange(2 * tq)[None, :]
    dist = np.where(j < tq, tq + i - j, i - (j - tq))
    n = np.maximum(dist, 0)
    exact = NUM_BUCKETS // 2
    nf = np.maximum(n, exact).astype(np.float64)
    large = exact + (np.log(nf / exact) / math.log(MAX_DISTANCE / exact) * (NUM_BUCKETS - exact)).astype(np.int64)
    return np.where(n < exact, n, np.minimum(large, NUM_BUCKETS - 1)).astype(np.int32)


def _cover_matrix(n_cmp_pad, n_sel):
    n = np.arange(n_cmp_pad)
    c_start = n * CMP_STRIDE
    c_end = c_start + CMP_LEN - 1
    s_start = np.arange(128) * SEL_BLOCK
    cover = (c_start[:, None] < s_start[None, :] + SEL_BLOCK) & (c_end[:, None] >= s_start[None, :])
    cover &= (np.arange(128)[None, :] < n_sel)
    return np.tile(cover.astype(np.float32), (A_REP, 1))


def _nsa_prompt_attend(q, gates, kc, vc, ks, vs, kw, vw, tab):
    b, t, _ = q.shape
    tq = NSA_Q_TILE
    assert t % NSA_FAR_TILE == 0 and t // SEL_BLOCK <= 32 and t >= WIN_A
    near = tab[_near_bucket_table(tq)] - tab[NUM_BUCKETS - 1]
    btile = jnp.transpose(near, (2, 3, 0, 1)).reshape(A_KV, A_REP * tq, 2 * tq)
    cover = jnp.asarray(_cover_matrix(128, t // SEL_BLOCK), BF16)
    kv_spec = lambda rows: pl.BlockSpec((None, None, rows, HEAD_DIM), lambda i, g, j: (i, g, 0, 0))
    return pl.pallas_call(
        functools.partial(_nsa_prompt_body, tq=tq),
        grid=(b, A_KV, t // tq),
        in_specs=[
            pl.BlockSpec((None, tq, A_REP * HEAD_DIM), lambda i, g, j: (i, j, g)),
            pl.BlockSpec((None, None, tq, 3 * A_REP), lambda i, g, j: (i, g, j, 0)),
            kv_spec(128), kv_spec(128), kv_spec(t), kv_spec(t), kv_spec(t), kv_spec(t),
            pl.BlockSpec((None, A_REP * tq, 2 * tq), lambda i, g, j: (g, 0, 0)),
            pl.BlockSpec((A_REP * 128, 128), lambda i, g, j: (0, 0)),
        ],
        out_specs=pl.BlockSpec((None, tq, A_REP * HEAD_DIM), lambda i, g, j: (i, j, g)),
        out_shape=jax.ShapeDtypeStruct((b, t, A_Q), F32),
        compiler_params=pltpu.CompilerParams(
            dimension_semantics=("parallel", "parallel", "arbitrary"),
            vmem_limit_bytes=V7X_VMEM_LIMIT_BYTES),
        name="nsa_prompt",
    )(q, gates, kc, vc, ks, vs, kw, vw, btile, cover)


S5_NS = B_GROUPS * B_STATE
S5_T_CHUNK = 64
S5_STRIP = 512


def _s5_body(u_ref, h0r_ref, h0i_ref, ar_ref, ai_ref, ldt_ref, wb_ref, wc_ref, d_ref, wglu_ref, bglu_ref,
             o_ref, hr_ref, hi_ref, coef_ref, st_ref, xbuf_ref, ubuf_ref, obuf_ref, *, nb, steps, interleave):
    c = pl.program_id(0)

    @pl.when(c == 0)
    def _():
        dt = jnp.exp(ldt_ref[...])
        ar, ai = ar_ref[...], ai_ref[...]
        mag = jnp.exp(ar * dt)
        abr, abi = mag * jnp.cos(ai * dt), mag * jnp.sin(ai * dt)
        den = ar * ar + ai * ai
        wr = ((abr - 1.0) * ar + abi * ai) / den
        wi = (abi * ar - (abr - 1.0) * ai) / den
        for k, val in enumerate((abr, abi, wr, wi)):
            coef_ref[k] = jnp.broadcast_to(val, (nb, S5_NS))
        st_ref[0] = h0r_ref[...]
        st_ref[1] = h0i_ref[...]

    if interleave:
        for b in range(nb):
            for j in range(B_WIDTH // 128):
                ubuf_ref.at[j][pl.ds(b, steps, stride=nb), :] = u_ref[b, :, j * 128:(j + 1) * 128]
        u = jnp.concatenate([ubuf_ref[j] for j in range(B_WIDTH // 128)], axis=1)
    else:
        u = u_ref[...]
    xbuf_ref[...] = jnp.dot(u.astype(BF16), wb_ref[...], preferred_element_type=F32)

    for lo in range(0, S5_NS, S5_STRIP):
        re = slice(lo, lo + S5_STRIP)
        im = slice(S5_NS + lo, S5_NS + lo + S5_STRIP)
        abr, abi, wr, wi = (coef_ref[k, :, re] for k in range(4))

        def step(t, carry):
            sr, si = carry
            r0 = pl.multiple_of(t * nb, nb)
            bur = xbuf_ref[pl.ds(r0, nb), re]
            bui = xbuf_ref[pl.ds(r0, nb), im]
            nsr = abr * sr - abi * si + (wr * bur - wi * bui)
            nsi = abr * si + abi * sr + (wr * bui + wi * bur)
            xbuf_ref[pl.ds(r0, nb), re] = nsr
            xbuf_ref[pl.ds(r0, nb), im] = nsi
            return nsr, nsi

        sr, si = lax.fori_loop(0, steps, step, (st_ref[0, :, re], st_ref[1, :, re]),
                               unroll=min(steps, 8))
        st_ref[0, :, re] = sr
        st_ref[1, :, re] = si

    y = jnp.dot(xbuf_ref[...].astype(BF16), wc_ref[...], preferred_element_type=F32) + d_ref[...] * u
    z = _gelu_tanh(y)
    gate = jnp.dot(z.astype(BF16), wglu_ref[...], preferred_element_type=F32) + bglu_ref[...]
    out = z * (1.0 / (1.0 + jnp.exp(-gate)))
    if interleave:
        for j in range(B_WIDTH // 128):
            obuf_ref[j] = out[:, j * 128:(j + 1) * 128]
        for b in range(nb):
            for j in range(B_WIDTH // 128):
                o_ref[b, :, j * 128:(j + 1) * 128] = obuf_ref.at[j][pl.ds(b, steps, stride=nb), :]
    else:
        o_ref[...] = out
    hr_ref[...] = st_ref[0]
    hi_ref[...] = st_ref[1]


def _s5_weights(a_re, a_im, log_dt, b_re, b_im, c_re, c_im, d, w_glu, b_glu):
    eye = jnp.eye(B_GROUPS, dtype=F32)
    blk_in = lambda w: jnp.einsum('hg,gpc->hcgp', eye, w).reshape(B_WIDTH, S5_NS)
    blk_out = lambda w: jnp.einsum('gh,gcp->gphc', eye, w).reshape(S5_NS, B_WIDTH)
    wb = jnp.concatenate([blk_in(b_re), blk_in(b_im)], axis=1).astype(BF16)
    wc = jnp.concatenate([blk_out(c_re), -blk_out(c_im)], axis=0).astype(BF16)
    flat = lambda x: x.reshape(1, S5_NS)
    return (flat(a_re), flat(a_im), flat(jnp.repeat(log_dt, B_STATE)), wb, wc, d.reshape(1, B_WIDTH),
            w_glu.astype(BF16), b_glu.reshape(1, B_WIDTH))


def _s5_mix(u, h_re, h_im, weights):
    nb, t, _ = u.shape
    interleave = t > 1
    steps = min(t, S5_T_CHUNK)
    rows = nb * steps
    body = functools.partial(_s5_body, nb=nb, steps=steps, interleave=interleave)
    if interleave:
        u_in = u
        u_spec = pl.BlockSpec((nb, steps, B_WIDTH), lambda c: (0, c, 0))
        o_shape = jax.ShapeDtypeStruct((nb, t, B_WIDTH), F32)
        scratch_rows = rows
    else:
        u_in = u.reshape(nb, B_WIDTH)
        u_spec = pl.BlockSpec((nb, B_WIDTH), lambda c: (0, 0))
        o_shape = jax.ShapeDtypeStruct((nb, B_WIDTH), F32)
        scratch_rows = 8
    o, hr, hi = pl.pallas_call(
        body,
        grid=(t // steps,),
        in_specs=[
            u_spec, _const_spec((nb, S5_NS)), _const_spec((nb, S5_NS)),
            _const_spec((1, S5_NS)), _const_spec((1, S5_NS)), _const_spec((1, S5_NS)),
            _const_spec((B_WIDTH, 2 * S5_NS)), _const_spec((2 * S5_NS, B_WIDTH)),
            _const_spec((1, B_WIDTH)), _const_spec((B_WIDTH, B_WIDTH)), _const_spec((1, B_WIDTH)),
        ],
        out_specs=[u_spec, pl.BlockSpec((nb, S5_NS), lambda c: (0, 0)), pl.BlockSpec((nb, S5_NS), lambda c: (0, 0))],
        out_shape=[o_shape, jax.ShapeDtypeStruct((nb, S5_NS), F32), jax.ShapeDtypeStruct((nb, S5_NS), F32)],
        scratch_shapes=[
            pltpu.VMEM((4, nb, S5_NS), F32),
            pltpu.VMEM((2, nb, S5_NS), F32),
            pltpu.VMEM((rows, 2 * S5_NS), F32),
            pltpu.VMEM((B_WIDTH // 128, scratch_rows, 128), F32),
            pltpu.VMEM((B_WIDTH // 128, scratch_rows, 128), F32),
        ],
        compiler_params=pltpu.CompilerParams(
            dimension_semantics=("arbitrary",),
            vmem_limit_bytes=V7X_VMEM_LIMIT_BYTES),
        name="s5_mix",
    )(u_in, h_re, h_im, *weights)
    return o.reshape(nb, t, B_WIDTH), hr, hi


def rms_norm(x, g):
    xf = x.astype(jnp.float32)
    y = xf * lax.rsqrt(jnp.mean(xf * xf, axis=-1, keepdims=True) + EPS)
    return (y * g.astype(jnp.float32)).astype(x.dtype)


def rel_bucket(dist):
    n = jnp.maximum(dist, 0)
    exact = NUM_BUCKETS // 2
    nf = jnp.maximum(n, exact).astype(jnp.float32)
    large = exact + (jnp.log(nf / exact) / math.log(MAX_DISTANCE / exact) * (NUM_BUCKETS - exact)).astype(jnp.int32)
    return jnp.where(n < exact, n, jnp.minimum(large, NUM_BUCKETS - 1))


def masked_softmax(s, mask):
    s = jnp.where(mask, s.astype(jnp.float32), -jnp.inf)
    m = jnp.max(s, axis=-1, keepdims=True)
    m = jnp.where(jnp.isfinite(m), m, 0.0)
    e = jnp.exp(s - m)
    d = jnp.sum(e, axis=-1, keepdims=True)
    return e / jnp.where(d > 0, d, 1.0)


def compress(rows, pos, w1, w2):
    b, t, g, dh = rows.shape
    r = CMP_LEN // CMP_STRIDE
    nch = t // CMP_STRIDE
    chunks = rows[:, :nch * CMP_STRIDE].reshape(b, nch, CMP_STRIDE, g, dh)
    w1b = w1.reshape(r, CMP_STRIDE, dh, w1.shape[-1])
    nc = nch - r + 1
    hid = jnp.einsum('ld,ldh->h', pos, w1.reshape(CMP_LEN, dh, w1.shape[-1]))
    for j in range(r):
        hid = hid + jnp.einsum('bnsgd,sdh->bngh', chunks[:, j:j + nc], w1b[j])
    return jnp.einsum('bngh,hd->bngd', jax.nn.gelu(hid), w2)


def compressed_kv(k_rows, v_rows, cmp_pos, cmp_w1, cmp_w2, k_gain):
    kc = rms_norm(compress(k_rows, cmp_pos[0], cmp_w1[0], cmp_w2[0]), k_gain)
    vc = compress(v_rows, cmp_pos[1], cmp_w1[1], cmp_w2[1])
    c_end = jnp.arange(kc.shape[1]) * CMP_STRIDE + (CMP_LEN - 1)
    return kc, vc, c_end


def nsa_attend(q, q_pos, kc, vc, c_end, n_sel, gather_sel, kw, vw, w_pos, gates, tab):
    scale = HEAD_DIM ** -0.5
    g_n = q.shape[2]
    s_c = jnp.einsum('bqgrd,bngd->bgrqn', q, kc) * scale
    p_c = masked_softmax(s_c, c_end[None, :] <= q_pos[:, None])
    o_c = jnp.einsum('bgrqn,bngd->bqgrd', p_c.astype(vc.dtype), vc)
    s_start = jnp.arange(n_sel) * SEL_BLOCK
    c_start = c_end - (CMP_LEN - 1)
    cover = ((c_start[:, None] < s_start[None, :] + SEL_BLOCK) & (c_end[:, None] >= s_start[None, :])).astype(jnp.float32)
    imp = jnp.einsum('bgrqn,ns->bgqs', p_c, cover)
    qblk = q_pos[:, None] // SEL_BLOCK
    sblk = jnp.arange(n_sel)[None, :]
    forced = ((sblk == 0) | (sblk == qblk) | (sblk == qblk - 1)).astype(jnp.float32)
    score = jnp.where(sblk <= qblk, imp + FORCE_BONUS * forced, -jnp.inf)
    _, idx = lax.top_k(score, min(SEL_TOPK, n_sel))
    k_sel, v_sel = gather_sel(idx)
    kpos = idx[..., None] * SEL_BLOCK + jnp.arange(SEL_BLOCK)
    dist = q_pos[:, None, None] - kpos
    tab_g = jnp.transpose(tab, (1, 0, 2))
    gi = jnp.arange(g_n)[None, :, None, None, None]
    bias = jnp.moveaxis(tab_g[gi, rel_bucket(dist)], -1, 3)
    s_s = jnp.einsum('bqgrd,bgqkjd->bgqrkj', q, k_sel) * scale + bias
    shp = s_s.shape
    mask = jnp.broadcast_to((dist >= 0)[:, :, :, None], shp)
    p_s = masked_softmax(s_s.reshape(shp[:4] + (-1,)), mask.reshape(shp[:4] + (-1,))).reshape(shp)
    o_s = jnp.einsum('bgqrkj,bgqkjd->bqgrd', p_s.astype(v_sel.dtype), v_sel)
    wd = q_pos[:, None] - w_pos[None, :]
    wmask = (wd >= 0) & (wd < WIN_A) & (w_pos[None, :] >= 0)
    wbias = jnp.transpose(tab[rel_bucket(wd)], (2, 3, 0, 1))
    s_w = jnp.einsum('bqgrd,bwgd->bgrqw', q, kw) * scale + wbias
    p_w = masked_softmax(s_w, wmask)
    o_w = jnp.einsum('bgrqw,bwgd->bqgrd', p_w.astype(vw.dtype), vw)
    return gates[..., 0:1] * o_c + gates[..., 1:2] * o_s + gates[..., 2:3] * o_w


def nsa_prompt(q, gates, kc_rows, vc_rows, ks, vs, kw, vw, cmp_pos, cmp_w1, cmp_w2, k_gain, tab):
    b, t = q.shape[:2]
    kc, vc, c_end = compressed_kv(kc_rows, vc_rows, cmp_pos, cmp_w1, cmp_w2, k_gain)
    n_sel = t // SEL_BLOCK
    ks_b = ks.reshape(b, n_sel, SEL_BLOCK, A_KV, HEAD_DIM).transpose(0, 3, 1, 2, 4)
    vs_b = vs.reshape(b, n_sel, SEL_BLOCK, A_KV, HEAD_DIM).transpose(0, 3, 1, 2, 4)
    bi = jnp.arange(b)[:, None, None, None]
    gi = jnp.arange(A_KV)[None, :, None, None]

    def gather_sel(idx):
        return ks_b[bi, gi, idx], vs_b[bi, gi, idx]

    pad = ((0, 0), (WIN_A, 0), (0, 0), (0, 0))
    kw_p, vw_p = jnp.pad(kw, pad), jnp.pad(vw, pad)

    def chunk(start):
        q_pos = start + jnp.arange(NSA_QBLK)
        qc = lax.dynamic_slice_in_dim(q, start, NSA_QBLK, axis=1)
        gc = lax.dynamic_slice_in_dim(gates, start, NSA_QBLK, axis=1)
        kwc = lax.dynamic_slice_in_dim(kw_p, start, NSA_QBLK + WIN_A, axis=1)
        vwc = lax.dynamic_slice_in_dim(vw_p, start, NSA_QBLK + WIN_A, axis=1)
        w_pos = start - WIN_A + jnp.arange(NSA_QBLK + WIN_A)
        return nsa_attend(qc, q_pos, kc, vc, c_end, n_sel, gather_sel, kwc, vwc, w_pos, gc, tab)

    out = lax.map(chunk, jnp.arange(t // NSA_QBLK) * NSA_QBLK)
    return jnp.moveaxis(out, 0, 1).reshape(b, t, A_KV, A_REP, HEAD_DIM)


def _nsa_prompt_glue(q, gates, kc_rows, vc_rows, ks, vs, kw, vw, cmp_pos, cmp_w1, cmp_w2, k_gain, tab):
    b, t = q.shape[:2]
    kc, vc, _ = compressed_kv(kc_rows, vc_rows, cmp_pos, cmp_w1, cmp_w2, k_gain)
    pad = lambda x: jnp.pad(x, ((0, 0), (0, 128 - x.shape[1]), (0, 0), (0, 0)))
    heads_first = lambda x: jnp.transpose(x, (0, 2, 1, 3)).astype(BF16)
    o = _nsa_prompt_attend(q.reshape(b, t, A_Q),
                           jnp.transpose(gates.reshape(b, t, A_KV, 3 * A_REP), (0, 2, 1, 3)),
                           heads_first(pad(kc)), heads_first(pad(vc)), heads_first(ks), heads_first(vs),
                           heads_first(kw), heads_first(vw), tab)
    return o.reshape(b, t, A_KV, A_REP, HEAD_DIM)


def nsa_sample(q, gates, new_c, new_s, win_all, past, pool_cmp, pool_sel, page_table,
               cmp_pos, cmp_w1, cmp_w2, k_gain, tab):
    db, s = q.shape[:2]
    past_c = pool_cmp[page_table].reshape(db, past, 2, A_KV, HEAD_DIM)
    rows_c = jnp.concatenate([past_c, new_c], axis=1)
    kc, vc, c_end = compressed_kv(rows_c[:, :, 0], rows_c[:, :, 1], cmp_pos, cmp_w1, cmp_w2, k_gain)
    q_pos = past + jnp.arange(s)
    n_past_blk = past // SEL_BLOCK
    n_new_blk = -(-s // SEL_BLOCK)
    n_sel = n_past_blk + n_new_blk
    bpp = PAGE_SIZE // SEL_BLOCK
    pool_b = pool_sel.reshape(pool_sel.shape[0], bpp, SEL_BLOCK, 2, A_KV, HEAD_DIM)
    new_b = jnp.pad(new_s, ((0, 0), (0, n_new_blk * SEL_BLOCK - s), (0, 0), (0, 0), (0, 0)))
    new_b = new_b.reshape(db, n_new_blk, SEL_BLOCK, 2, A_KV, HEAD_DIM)
    bi = jnp.arange(db)[:, None, None, None]
    gi = jnp.arange(A_KV)[None, :, None, None]

    def gather_sel(idx):
        pidx = jnp.clip(idx, 0, n_past_blk - 1)
        phys = page_table[bi, pidx // bpp]
        pool_f = pool_b.reshape((-1,) + pool_b.shape[2:])
        both = pool_f[phys * bpp + pidx % bpp]
        from_pool = jnp.stack([both[:, g, ..., g, :] for g in range(A_KV)], axis=1)
        from_new = new_b[bi, jnp.clip(idx - n_past_blk, 0, n_new_blk - 1), :, :, gi]
        kv = jnp.where((idx < n_past_blk)[..., None, None, None], from_pool, from_new)
        return kv[..., 0, :], kv[..., 1, :]

    w_len = win_all.shape[1]
    w_pos = past + s - w_len + jnp.arange(w_len)
    return nsa_attend(q, q_pos, kc, vc, c_end, n_sel, gather_sel, win_all[:, :, 0], win_all[:, :, 1],
                      w_pos, gates, tab)


def _ssm_combine(e1, e2):
    a1r, a1i, b1r, b1i = e1
    a2r, a2i, b2r, b2i = e2
    return (a2r * a1r - a2i * a1i, a2r * a1i + a2i * a1r,
            a2r * b1r - a2i * b1i + b2r, a2r * b1i + a2i * b1r + b2i)


def s5_mix(u, h_re, h_im, a_re, a_im, log_dt, b_re, b_im, c_re, c_im, d, w_glu, b_glu):
    bsz, t, _ = u.shape
    f32 = jnp.float32
    uf = u.astype(f32)
    ug = uf.reshape(bsz, t, B_GROUPS, B_GROUP)
    dt = jnp.exp(log_dt.astype(f32))[:, None]
    ar, ai = a_re.astype(f32), a_im.astype(f32)
    mag = jnp.exp(ar * dt)
    abr, abi = mag * jnp.cos(ai * dt), mag * jnp.sin(ai * dt)
    den = ar * ar + ai * ai
    wr = ((abr - 1.0) * ar + abi * ai) / den
    wi = (abi * ar - (abr - 1.0) * ai) / den
    bu_r = jnp.einsum('gpc,btgc->btgp', b_re.astype(f32), ug)
    bu_i = jnp.einsum('gpc,btgc->btgp', b_im.astype(f32), ug)
    x_r = wr * bu_r - wi * bu_i
    x_i = wr * bu_i + wi * bu_r
    hr, hi = h_re.astype(f32), h_im.astype(f32)
    x_r = x_r.at[:, 0].add(abr * hr - abi * hi)
    x_i = x_i.at[:, 0].add(abr * hi + abi * hr)
    a_r = jnp.broadcast_to(abr, x_r.shape)
    a_i = jnp.broadcast_to(abi, x_i.shape)
    _, _, s_r, s_i = lax.associative_scan(_ssm_combine, (a_r, a_i, x_r, x_i), axis=1)
    y = jnp.einsum('gcp,btgp->btgc', c_re.astype(f32), s_r) - jnp.einsum('gcp,btgp->btgc', c_im.astype(f32), s_i)
    y = y.reshape(bsz, t, B_WIDTH) + d.astype(f32) * uf
    z = jax.nn.gelu(y)
    out = z * jax.nn.sigmoid(z @ w_glu.astype(f32) + b_glu.astype(f32))
    return out.astype(u.dtype), s_r[:, -1], s_i[:, -1]


def nsa_s5_inputs(xn, w_in, q_gain, k_gain):
    b, t, _ = xn.shape
    sizes = [A_Q] + [A_KVW] * 6 + [A_GATE, B_WIDTH]
    cuts = [int(c) for c in np.cumsum(sizes)[:-1]]
    q, kc, vc, ks, vs, kw, vw, gl, u = jnp.split(xn @ w_in, cuts, axis=-1)
    heads = lambda z: z.reshape(b, t, A_KV, HEAD_DIM)
    q = rms_norm(q.reshape(b, t, A_KV, A_REP, HEAD_DIM), q_gain)
    gates = jax.nn.sigmoid(gl).reshape(b, t, A_KV, A_REP, 3)
    return (q, rms_norm(heads(kc), k_gain), heads(vc), rms_norm(heads(ks), k_gain), heads(vs),
            rms_norm(heads(kw), k_gain), heads(vw), gates, u)


def swa_inputs(xn, w_in, q_gain, k_gain):
    b, t, _ = xn.shape
    q, k, v = jnp.split(xn @ w_in, [C_HEADS * HEAD_DIM, (C_HEADS + C_KV) * HEAD_DIM], axis=-1)
    q = rms_norm(q.reshape(b, t, C_KV, C_REP, HEAD_DIM), q_gain)
    k = rms_norm(k.reshape(b, t, C_KV, HEAD_DIM), k_gain)
    return q, k, v.reshape(b, t, C_KV, HEAD_DIM)


def swa_attend(q, q_pos, k, v, k_pos, sinks, tab):
    dist = q_pos[:, :, None] - k_pos[:, None, :]
    mask = (dist >= 0) & (dist < WIN_C) & (k_pos[:, None, :] >= 0)
    bias = jnp.transpose(tab[rel_bucket(dist)], (3, 4, 0, 1, 2))
    s = jnp.einsum('bnqgrd,bnkgd->bgrnqk', q, k).astype(jnp.float32) * (HEAD_DIM ** -0.5) + bias
    s = jnp.where(mask, s, -jnp.inf)
    sink = sinks.astype(jnp.float32).reshape(C_KV, C_REP)[:, :, None, None, None]
    m = jnp.maximum(jnp.max(s, axis=-1, keepdims=True), sink)
    e = jnp.exp(s - m)
    p = e / (jnp.sum(e, axis=-1, keepdims=True) + jnp.exp(sink - m))
    return jnp.einsum('bgrnqk,bnkgd->bnqgrd', p.astype(v.dtype), v)


def swa_prompt_attend(q, k, v, sinks, tab):
    b, t = q.shape[:2]
    nb = t // WIN_C
    qb = q.reshape(b, nb, WIN_C, C_KV, C_REP, HEAD_DIM)

    def band(x):
        xb = x.reshape(b, nb, WIN_C, C_KV, HEAD_DIM)
        prev = jnp.pad(xb[:, :-1], ((0, 0), (1, 0), (0, 0), (0, 0), (0, 0)))
        return jnp.concatenate([prev, xb], axis=2)

    q_pos = jnp.arange(t).reshape(nb, WIN_C)
    k_pos = (jnp.arange(nb)[:, None] - 1) * WIN_C + jnp.arange(2 * WIN_C)[None, :]
    o = swa_attend(qb, q_pos, band(k), band(v), k_pos, sinks, tab)
    return o.reshape(b, t, C_HEADS * HEAD_DIM)


def kernel(x_prompt, x_sample, cache_nsa_cmp, cache_nsa_sel, cache_nsa_win, state_s5_re, state_s5_im,
           cache_swa, state_ffn_conv, page_table, rel_bias, norm_mix, norm_ffn, a_w_in, a_w_out,
           nsa_q_gain, nsa_k_gain, nsa_cmp_pos, nsa_cmp_w1, nsa_cmp_w2, s5_a_re, s5_a_im, s5_log_dt,
           s5_b_re, s5_b_im, s5_c_re, s5_c_im, s5_d, s5_w_glu, s5_b_glu, c_w_in, c_w_out, c_q_gain,
           c_k_gain, c_sinks, ffn_w_up, ffn_w_gate, ffn_conv_w, ffn_conv_b, ffn_w_down):
    bp, tp, _ = x_prompt.shape
    bs, ts, _ = x_sample.shape
    past = page_table.shape[1] * PAGE_SIZE
    tab_a = rel_bias[:, :A_HEADS].reshape(NUM_BUCKETS, A_KV, A_REP)
    tab_c = rel_bias[:, :C_HEADS].reshape(NUM_BUCKETS, C_KV, C_REP)
    hp, hs = x_prompt, x_sample
    cmp_p, cmp_s, sel_p, sel_s, win_p, win_s = [], [], [], [], [], []
    s5r_p, s5i_p, s5r_s, s5i_s = [], [], [], []
    swa_p, swa_s, conv_p, conv_s = [], [], [], []
    for layer in range(DEPTH):
        i = layer // 2
        xp = rms_norm(hp, norm_mix[layer])
        xs = rms_norm(hs, norm_mix[layer])
        if layer % 2 == 0:
            qp, kcp, vcp, ksp, vsp, kwp, vwp, gp, up = nsa_s5_inputs(xp, a_w_in[i], nsa_q_gain[i], nsa_k_gain[i])
            qs, kcs, vcs, kss, vss, kws, vws, gs, us = nsa_s5_inputs(xs, a_w_in[i], nsa_q_gain[i], nsa_k_gain[i])
            cmp_args = (nsa_cmp_pos[i], nsa_cmp_w1[i], nsa_cmp_w2[i], nsa_k_gain[i], tab_a)
            o_ap = _nsa_prompt_glue(qp, gp, kcp, vcp, ksp, vsp, kwp, vwp, *cmp_args)
            new_c = jnp.stack([kcs, vcs], axis=2)
            new_sel = jnp.stack([kss, vss], axis=2)
            win_all = jnp.concatenate([cache_nsa_win[i].astype(kws.dtype), jnp.stack([kws, vws], axis=2)], axis=1)
            o_as = nsa_sample(qs, gs, new_c, new_sel, win_all, past, cache_nsa_cmp[i], cache_nsa_sel[i],
                              page_table, *cmp_args)
            s5_args = (s5_a_re[i], s5_a_im[i], s5_log_dt[i], s5_b_re[i], s5_b_im[i], s5_c_re[i], s5_c_im[i],
                       s5_d[i], s5_w_glu[i], s5_b_glu[i])
            s5w = _s5_weights(*s5_args)
            h0 = jnp.zeros((bp, S5_NS), F32)
            o_bp, hr_p, hi_p = _s5_mix(up, h0, h0, s5w)
            o_bs, hr_s, hi_s = _s5_mix(us, state_s5_re[i].reshape(bs, S5_NS), state_s5_im[i].reshape(bs, S5_NS), s5w)
            hr_p, hi_p = (x.reshape(bp, B_GROUPS, B_STATE) for x in (hr_p, hi_p))
            hr_s, hi_s = (x.reshape(bs, B_GROUPS, B_STATE) for x in (hr_s, hi_s))
            hp = hp + jnp.concatenate([o_ap.reshape(bp, tp, A_Q), o_bp], axis=-1) @ a_w_out[i]
            hs = hs + jnp.concatenate([o_as.reshape(bs, ts, A_Q), o_bs], axis=-1) @ a_w_out[i]
            cmp_p.append(jnp.stack([kcp, vcp], axis=2))
            cmp_s.append(new_c)
            sel_p.append(jnp.stack([ksp, vsp], axis=2))
            sel_s.append(new_sel)
            win_p.append(jnp.stack([kwp, vwp], axis=2)[:, -min(WIN_A, tp):])
            win_s.append(win_all[:, -min(WIN_A, win_all.shape[1]):])
            s5r_p.append(hr_p)
            s5i_p.append(hi_p)
            s5r_s.append(hr_s)
            s5i_s.append(hi_s)
        else:
            qp, kp, vp = swa_inputs(xp, c_w_in[i], c_q_gain[i], c_k_gain[i])
            qs, k_s, v_s = swa_inputs(xs, c_w_in[i], c_q_gain[i], c_k_gain[i])
            o_cp = swa_prompt_attend(qp, kp, vp, c_sinks[i], tab_c)
            kv_all = jnp.concatenate([cache_swa[i].astype(k_s.dtype), jnp.stack([k_s, v_s], axis=2)], axis=1)
            kv_len = kv_all.shape[1]
            q_pos = (past + jnp.arange(ts))[None, :]
            k_pos = (past + ts - kv_len + jnp.arange(kv_len))[None, :]
            o_cs = swa_attend(qs[:, None], q_pos, kv_all[:, None, :, 0], kv_all[:, None, :, 1], k_pos,
                              c_sinks[i], tab_c).reshape(bs, ts, C_HEADS * HEAD_DIM)
            hp = hp + o_cp @ c_w_out[i]
            hs = hs + o_cs @ c_w_out[i]
            swa_p.append(jnp.stack([kp, vp], axis=2)[:, -min(WIN_C, tp):])
            swa_s.append(kv_all[:, -min(WIN_C, kv_len):])
        wup = ffn_w_up[layer].astype(BF16)
        wgate = ffn_w_gate[layer].astype(BF16)
        wdown = ffn_w_down[layer].astype(BF16)
        gain = norm_ffn[layer].reshape(1, D_MODEL)
        cw = ffn_conv_w[layer]
        cb = ffn_conv_b[layer].reshape(1, D_FF)
        hp, cp = _ffn_prompt(hp, gain, jnp.zeros((bp, CONV_W - 1, D_FF), F32), wup, wgate, cw, cb, wdown)
        hs2, cs = _ffn_sample(hs.reshape(bs * ts, D_MODEL), gain,
                              state_ffn_conv[layer].reshape(bs, (CONV_W - 1) * D_FF), wup, wgate, cw, cb, wdown)
        hs = hs2.reshape(bs, ts, D_MODEL)
        conv_p.append(cp)
        conv_s.append(cs.reshape(bs, CONV_W - 1, D_FF))
    y_prompt, y_sample = hp, hs
    return (y_prompt, y_sample, jnp.stack(cmp_p), jnp.stack(cmp_s), jnp.stack(sel_p), jnp.stack(sel_s),
            jnp.stack(win_p), jnp.stack(win_s), jnp.stack(s5r_p), jnp.stack(s5i_p), jnp.stack(s5r_s),
            jnp.stack(s5i_s), jnp.stack(swa_p), jnp.stack(swa_s), jnp.stack(conv_p), jnp.stack(conv_s))
```

```python
import functools
import math

import jax
import jax.numpy as jnp
import numpy as np
from jax import lax
from jax.experimental import pallas as pl
from jax.experimental.pallas import tpu as pltpu

D_MODEL = 1024
DEPTH = 2
PAGE_SIZE = 128
HEAD_DIM = 64
A_HEADS = 8
A_KV = 2
A_REP = A_HEADS // A_KV
A_Q = A_HEADS * HEAD_DIM
A_KVW = A_KV * HEAD_DIM
A_GATE = 3 * A_HEADS
CMP_LEN = 32
CMP_STRIDE = 16
SEL_BLOCK = 64
SEL_TOPK = 16
WIN_A = 512
NSA_QBLK = 64
FORCE_BONUS = 1000.0
B_WIDTH = D_MODEL // 2
B_GROUP = 16
B_GROUPS = B_WIDTH // B_GROUP
B_STATE = 64
C_HEADS = D_MODEL // HEAD_DIM
C_KV = 2
C_REP = C_HEADS // C_KV
WIN_C = 128
NUM_BUCKETS = 32
MAX_DISTANCE = 128
D_FF = 2816
CONV_W = 3
EPS = 1e-6

F32 = jnp.float32
BF16 = jnp.bfloat16

V7X_VMEM_LIMIT_BYTES = 56 * 1024 * 1024
FFN_ROW_TILE = 512
FFN_COL_CHUNK = 1408


def _gelu_tanh(x):
    return 0.5 * x * (1.0 + jnp.tanh(math.sqrt(2.0 / math.pi) * (x + 0.044715 * (x * x * x))))


def _rms_rows(x, gain):
    return x * lax.rsqrt(jnp.mean(x * x, axis=-1, keepdims=True) + EPS) * gain


def _const_spec(shape):
    zeros = (0,) * len(shape)
    return pl.BlockSpec(shape, lambda *_: zeros, pipeline_mode=pl.Buffered(1))


def _ffn_prompt_body(x_ref, gain_ref, prev_ref, wup_ref, wgate_ref, cw_ref, cb_ref, wdown_ref,
                     y_ref, cs_ref, hbuf_ref, *, tm, ffc):
    t = pl.program_id(1)
    x = x_ref[...]
    xn = _rms_rows(x, gain_ref[...]).astype(BF16)
    acc = x
    for c in range(D_FF // ffc):
        lo = c * ffc
        h = jnp.dot(xn, wup_ref[:, lo:lo + ffc], preferred_element_type=F32)
        g = jnp.dot(xn, wgate_ref[:, lo:lo + ffc], preferred_element_type=F32)

        @pl.when(t == 0)
        def _():
            hbuf_ref[c, 6:8, :] = prev_ref[:, lo:lo + ffc]

        hbuf_ref[c, 8:8 + tm, :] = h
        hm1 = hbuf_ref[c, 7:7 + tm, :]
        hm2 = hbuf_ref[c, 6:6 + tm, :]
        cw = cw_ref[:, lo:lo + ffc]
        hc = cw[0:1] * hm2 + cw[1:2] * hm1 + cw[2:3] * h + cb_ref[:, lo:lo + ffc]
        a = (_gelu_tanh(hc) * g).astype(BF16)
        acc = acc + jnp.dot(a, wdown_ref[lo:lo + ffc, :], preferred_element_type=F32)
        hbuf_ref[c, 0:8, :] = h[tm - 8:tm, :]
        cs_ref[:, lo:lo + ffc] = h[tm - 2:tm, :]
    y_ref[...] = acc


def _ffn_prompt(x, gain, prev, wup, wgate, cw, cb, wdown):
    b, t, d = x.shape
    tm, ffc = FFN_ROW_TILE, FFN_COL_CHUNK
    body = functools.partial(_ffn_prompt_body, tm=tm, ffc=ffc)
    return pl.pallas_call(
        body,
        grid=(b, t // tm),
        in_specs=[
            pl.BlockSpec((None, tm, d), lambda i, j: (i, j, 0)),
            _const_spec((1, d)),
            pl.BlockSpec((None, CONV_W - 1, D_FF), lambda i, j: (i, 0, 0)),
            _const_spec((d, D_FF)),
            _const_spec((d, D_FF)),
            _const_spec((CONV_W, D_FF)),
            _const_spec((1, D_FF)),
            _const_spec((D_FF, d)),
        ],
        out_specs=[
            pl.BlockSpec((None, tm, d), lambda i, j: (i, j, 0)),
            pl.BlockSpec((None, CONV_W - 1, D_FF), lambda i, j: (i, 0, 0)),
        ],
        out_shape=[
            jax.ShapeDtypeStruct((b, t, d), F32),
            jax.ShapeDtypeStruct((b, CONV_W - 1, D_FF), F32),
        ],
        scratch_shapes=[pltpu.VMEM((D_FF // ffc, 8 + tm, ffc), F32)],
        compiler_params=pltpu.CompilerParams(
            dimension_semantics=("parallel", "arbitrary"),
            vmem_limit_bytes=V7X_VMEM_LIMIT_BYTES),
        name="ffn_prompt",
    )(x, gain, prev, wup, wgate, cw, cb, wdown)


def _ffn_sample_body(x_ref, gain_ref, prev_ref, wup_ref, wgate_ref, cw_ref, cb_ref, wdown_ref,
                     y_ref, cs_ref, *, ffc):
    x = x_ref[...]
    xn = _rms_rows(x, gain_ref[...]).astype(BF16)
    acc = x
    for c in range(D_FF // ffc):
        lo = c * ffc
        h = jnp.dot(xn, wup_ref[:, lo:lo + ffc], preferred_element_type=F32)
        g = jnp.dot(xn, wgate_ref[:, lo:lo + ffc], preferred_element_type=F32)
        hm2 = prev_ref[:, lo:lo + ffc]
        hm1 = prev_ref[:, D_FF + lo:D_FF + lo + ffc]
        cw = cw_ref[:, lo:lo + ffc]
        hc = cw[0:1] * hm2 + cw[1:2] * hm1 + cw[2:3] * h + cb_ref[:, lo:lo + ffc]
        a = (_gelu_tanh(hc) * g).astype(BF16)
        acc = acc + jnp.dot(a, wdown_ref[lo:lo + ffc, :], preferred_element_type=F32)
        cs_ref[:, lo:lo + ffc] = hm1
        cs_ref[:, D_FF + lo:D_FF + lo + ffc] = h
    y_ref[...] = acc


def _ffn_sample(x, gain, prev, wup, wgate, cw, cb, wdown):
    n, d = x.shape
    body = functools.partial(_ffn_sample_body, ffc=FFN_COL_CHUNK)
    return pl.pallas_call(
        body,
        grid=(1,),
        in_specs=[
            _const_spec((n, d)),
            _const_spec((1, d)),
            _const_spec((n, (CONV_W - 1) * D_FF)),
            _const_spec((d, D_FF)),
            _const_spec((d, D_FF)),
            _const_spec((CONV_W, D_FF)),
            _const_spec((1, D_FF)),
            _const_spec((D_FF, d)),
        ],
        out_specs=[
            pl.BlockSpec((n, d), lambda i: (0, 0)),
            pl.BlockSpec((n, (CONV_W - 1) * D_FF), lambda i: (0, 0)),
        ],
        out_shape=[
            jax.ShapeDtypeStruct((n, d), F32),
            jax.ShapeDtypeStruct((n, (CONV_W - 1) * D_FF), F32),
        ],
        compiler_params=pltpu.CompilerParams(
            dimension_semantics=("arbitrary",),
            vmem_limit_bytes=V7X_VMEM_LIMIT_BYTES),
        name="ffn_sample",
    )(x, gain, prev, wup, wgate, cw, cb, wdown)


NSA_Q_TILE = 128
NSA_FAR_TILE = 512
NSA_NEAR = 2 * NSA_Q_TILE
NEG_INF = float("-inf")


def _dot_nt(a, b):
    return lax.dot_general(a, b, (((1,), (1,)), ((), ())), preferred_element_type=F32)


def _softmax_start(s, v):
    m = jnp.max(s, axis=-1, keepdims=True)
    e = jnp.exp(s - m)
    return m, jnp.sum(e, axis=-1, keepdims=True), jnp.dot(e.astype(BF16), v, preferred_element_type=F32)


def _softmax_more(carry, s, v):
    m, l, acc = carry
    m_new = jnp.maximum(m, jnp.max(s, axis=-1, keepdims=True))
    alpha = jnp.exp(m - m_new)
    e = jnp.exp(s - m_new)
    return (m_new, alpha * l + jnp.sum(e, axis=-1, keepdims=True),
            alpha * acc + jnp.dot(e.astype(BF16), v, preferred_element_type=F32))


def _block_expand(k0, width):
    kpos = k0 + lax.broadcasted_iota(jnp.int32, (128, width), 1)
    blk = lax.broadcasted_iota(jnp.int32, (128, width), 0)
    return jnp.where((kpos >> 6) == blk, 1.0, 0.0).astype(BF16)


def _nsa_prompt_body(q_ref, gate_ref, kc_ref, vc_ref, ks_ref, vs_ref, kw_ref, vw_ref, btile_ref, cover_ref,
                     o_ref, *, tq):
    qt = pl.program_id(2)
    q0 = pl.multiple_of(qt * tq, tq)
    rows = A_REP * tq
    q = q_ref[...]
    qs = jnp.concatenate([q[:, r * HEAD_DIM:(r + 1) * HEAD_DIM] for r in range(A_REP)], axis=0)
    qs = (qs * (HEAD_DIM ** -0.5)).astype(BF16)

    def row_pos(width):
        return q0 + (lax.broadcasted_iota(jnp.int32, (rows, width), 0) & (tq - 1))

    def col_idx(width):
        return lax.broadcasted_iota(jnp.int32, (rows, width), 1)

    n_idx = col_idx(128)
    valid_c = (n_idx * CMP_STRIDE + (CMP_LEN - 1) <= row_pos(128)) & (n_idx < 127)
    s_c = jnp.where(valid_c, _dot_nt(qs, kc_ref[...]), NEG_INF)
    m_c = jnp.max(s_c, axis=-1, keepdims=True)
    m_c = jnp.where(m_c == NEG_INF, 0.0, m_c)
    e_c = jnp.exp(s_c - m_c)
    d_c = jnp.sum(e_c, axis=-1, keepdims=True)
    p_c = (e_c / jnp.where(d_c > 0, d_c, 1.0)).astype(BF16)
    o_c = jnp.dot(p_c, vc_ref[...], preferred_element_type=F32)
    p_heads = jnp.concatenate([p_c[r * tq:(r + 1) * tq] for r in range(A_REP)], axis=1)
    imp = jnp.dot(p_heads, cover_ref[...], preferred_element_type=F32)

    s_idx = lax.broadcasted_iota(jnp.int32, (tq, 128), 1)
    qblk = (q0 + lax.broadcasted_iota(jnp.int32, (tq, 128), 0)) >> 6
    forced = (s_idx == 0) | (s_idx == qblk) | (s_idx == qblk - 1)
    allowed = s_idx <= qblk
    score = jnp.where(allowed, imp + jnp.where(forced, FORCE_BONUS, 0.0), NEG_INF)
    rank = jnp.zeros((tq, 128), F32)
    for j in range(32):
        col = score[:, j:j + 1]
        beats = (col > score) | ((col == score) & (s_idx > j))
        rank = rank + jnp.where(beats, 1.0, 0.0)
    sel = jnp.where((rank < SEL_TOPK) & allowed, 1.0, 0.0).astype(BF16)
    sel = jnp.concatenate([sel] * A_REP, axis=0)

    prev0 = pl.multiple_of(jnp.maximum(q0 - tq, 0), tq)
    near_pos = jnp.where(col_idx(NSA_NEAR) < tq, prev0, q0 - tq) + col_idx(NSA_NEAR)
    near_ok = (near_pos <= row_pos(NSA_NEAR)) & ((col_idx(NSA_NEAR) >= tq) | (qt > 0))
    btile = btile_ref[...]

    def near(k_ref, v_ref, extra_mask):
        k = jnp.concatenate([k_ref[pl.ds(prev0, tq), :], k_ref[pl.ds(q0, tq), :]], axis=0)
        v = jnp.concatenate([v_ref[pl.ds(prev0, tq), :], v_ref[pl.ds(q0, tq), :]], axis=0)
        mask = near_ok if extra_mask is None else (near_ok & extra_mask)
        return _softmax_start(jnp.where(mask, _dot_nt(qs, k) + btile, NEG_INF), v)

    expand_near = jnp.concatenate([_block_expand(prev0, tq), _block_expand(q0, tq)], axis=1)
    sel_near = jnp.dot(sel, expand_near, preferred_element_type=F32) > 0.5

    far_end = q0 - tq

    def sel_far(i, carry):
        k0 = pl.multiple_of(i * NSA_FAR_TILE, NSA_FAR_TILE)
        s = _dot_nt(qs, ks_ref[pl.ds(k0, NSA_FAR_TILE), :])
        hit = jnp.dot(sel, _block_expand(k0, NSA_FAR_TILE), preferred_element_type=F32) > 0.5
        mask = hit & (k0 + col_idx(NSA_FAR_TILE) < far_end)
        return _softmax_more(carry, jnp.where(mask, s, NEG_INF), vs_ref[pl.ds(k0, NSA_FAR_TILE), :])

    n_far = (jnp.maximum(far_end, 0) + NSA_FAR_TILE - 1) // NSA_FAR_TILE
    _, l_s, acc_s = lax.fori_loop(0, n_far, sel_far, near(ks_ref, vs_ref, sel_near))
    o_s = acc_s / l_s

    w_far = WIN_A - tq
    wf0 = pl.multiple_of(jnp.maximum(q0 - WIN_A, 0), tq)
    wpos = wf0 + col_idx(w_far)
    wmask = (row_pos(w_far) - wpos < WIN_A) & (wpos < far_end)
    s_w = jnp.where(wmask, _dot_nt(qs, kw_ref[pl.ds(wf0, w_far), :]), NEG_INF)
    _, l_w, acc_w = _softmax_more(near(kw_ref, vw_ref, None), s_w, vw_ref[pl.ds(wf0, w_far), :])
    o_w = acc_w / l_w

    gates = gate_ref[...]
    outs = []
    for r in range(A_REP):
        sl = slice(r * tq, (r + 1) * tq)
        outs.append(gates[:, 3 * r:3 * r + 1] * o_c[sl] + gates[:, 3 * r + 1:3 * r + 2] * o_s[sl]
                    + gates[:, 3 * r + 2:3 * r + 3] * o_w[sl])
    o_ref[...] = jnp.concatenate(outs, axis=1)


def _near_bucket_table(tq):
    i = np.arange(tq)[:, None]
    j = np.arange(2 * tq)[None, :]
    dist = np.where(j < tq, tq + i - j, i - (j - tq))
    n = np.maximum(dist, 0)
    exact = NUM_BUCKETS // 2
    nf = np.maximum(n, exact).astype(np.float64)
    large = exact + (np.log(nf / exact) / math.log(MAX_DISTANCE / exact) * (NUM_BUCKETS - exact)).astype(np.int64)
    return np.where(n < exact, n, np.minimum(large, NUM_BUCKETS - 1)).astype(np.int32)


def _cover_matrix(n_cmp_pad, n_sel):
    n = np.arange(n_cmp_pad)
    c_start = n * CMP_STRIDE
    c_end = c_start + CMP_LEN - 1
    s_start = np.arange(128) * SEL_BLOCK
    cover = (c_start[:, None] < s_start[None, :] + SEL_BLOCK) & (c_end[:, None] >= s_start[None, :])
    cover &= (np.arange(128)[None, :] < n_sel)
    return np.tile(cover.astype(np.float32), (A_REP, 1))


def _nsa_prompt_attend(q, gates, kc, vc, ks, vs, kw, vw, tab):
    b, t, _ = q.shape
    tq = NSA_Q_TILE
    assert t % NSA_FAR_TILE == 0 and t // SEL_BLOCK <= 32 and t >= WIN_A
    near = tab[_near_bucket_table(tq)] - tab[NUM_BUCKETS - 1]
    btile = jnp.transpose(near, (2, 3, 0, 1)).reshape(A_KV, A_REP * tq, 2 * tq)
    cover = jnp.asarray(_cover_matrix(128, t // SEL_BLOCK), BF16)
    kv_spec = lambda rows: pl.BlockSpec((None, None, rows, HEAD_DIM), lambda i, g, j: (i, g, 0, 0))
    return pl.pallas_call(
        functools.partial(_nsa_prompt_body, tq=tq),
        grid=(b, A_KV, t // tq),
        in_specs=[
            pl.BlockSpec((None, tq, A_REP * HEAD_DIM), lambda i, g, j: (i, j, g)),
            pl.BlockSpec((None, None, tq, 3 * A_REP), lambda i, g, j: (i, g, j, 0)),
            kv_spec(128), kv_spec(128), kv_spec(t), kv_spec(t), kv_spec(t), kv_spec(t),
            pl.BlockSpec((None, A_REP * tq, 2 * tq), lambda i, g, j: (g, 0, 0)),
            pl.BlockSpec((A_REP * 128, 128), lambda i, g, j: (0, 0)),
        ],
        out_specs=pl.BlockSpec((None, tq, A_REP * HEAD_DIM), lambda i, g, j: (i, j, g)),
        out_shape=jax.ShapeDtypeStruct((b, t, A_Q), F32),
        compiler_params=pltpu.CompilerParams(
            dimension_semantics=("parallel", "parallel", "arbitrary"),
            vmem_limit_bytes=V7X_VMEM_LIMIT_BYTES),
        name="nsa_prompt",
    )(q, gates, kc, vc, ks, vs, kw, vw, btile, cover)


S5_NS = B_GROUPS * B_STATE
S5_T_CHUNK = 64
S5_STRIP = 512


def _s5_body(u_ref, h0r_ref, h0i_ref, ar_ref, ai_ref, ldt_ref, wb_ref, wc_ref, d_ref, wglu_ref, bglu_ref,
             o_ref, hr_ref, hi_ref, coef_ref, st_ref, xbuf_ref, ubuf_ref, obuf_ref, *, nb, steps, interleave):
    c = pl.program_id(0)

    @pl.when(c == 0)
    def _():
        dt = jnp.exp(ldt_ref[...])
        ar, ai = ar_ref[...], ai_ref[...]
        mag = jnp.exp(ar * dt)
        abr, abi = mag * jnp.cos(ai * dt), mag * jnp.sin(ai * dt)
        den = ar * ar + ai * ai
        wr = ((abr - 1.0) * ar + abi * ai) / den
        wi = (abi * ar - (abr - 1.0) * ai) / den
        for k, val in enumerate((abr, abi, wr, wi)):
            coef_ref[k] = jnp.broadcast_to(val, (nb, S5_NS))
        st_ref[0] = h0r_ref[...]
        st_ref[1] = h0i_ref[...]

    if interleave:
        for b in range(nb):
            for j in range(B_WIDTH // 128):
                ubuf_ref.at[j][pl.ds(b, steps, stride=nb), :] = u_ref[b, :, j * 128:(j + 1) * 128]
        u = jnp.concatenate([ubuf_ref[j] for j in range(B_WIDTH // 128)], axis=1)
    else:
        u = u_ref[...]
    xbuf_ref[...] = jnp.dot(u.astype(BF16), wb_ref[...], preferred_element_type=F32)

    for lo in range(0, S5_NS, S5_STRIP):
        re = slice(lo, lo + S5_STRIP)
        im = slice(S5_NS + lo, S5_NS + lo + S5_STRIP)
        abr, abi, wr, wi = (coef_ref[k, :, re] for k in range(4))

        def step(t, carry):
            sr, si = carry
            r0 = pl.multiple_of(t * nb, nb)
            bur = xbuf_ref[pl.ds(r0, nb), re]
            bui = xbuf_ref[pl.ds(r0, nb), im]
            nsr = abr * sr - abi * si + (wr * bur - wi * bui)
            nsi = abr * si + abi * sr + (wr * bui + wi * bur)
            xbuf_ref[pl.ds(r0, nb), re] = nsr
            xbuf_ref[pl.ds(r0, nb), im] = nsi
            return nsr, nsi

        sr, si = lax.fori_loop(0, steps, step, (st_ref[0, :, re], st_ref[1, :, re]),
                               unroll=min(steps, 8))
        st_ref[0, :, re] = sr
        st_ref[1, :, re] = si

    y = jnp.dot(xbuf_ref[...].astype(BF16), wc_ref[...], preferred_element_type=F32) + d_ref[...] * u
    z = _gelu_tanh(y)
    gate = jnp.dot(z.astype(BF16), wglu_ref[...], preferred_element_type=F32) + bglu_ref[...]
    out = z * (1.0 / (1.0 + jnp.exp(-gate)))
    if interleave:
        for j in range(B_WIDTH // 128):
            obuf_ref[j] = out[:, j * 128:(j + 1) * 128]
        for b in range(nb):
            for j in range(B_WIDTH // 128):
                o_ref[b, :, j * 128:(j + 1) * 128] = obuf_ref.at[j][pl.ds(b, steps, stride=nb), :]
    else:
        o_ref[...] = out
    hr_ref[...] = st_ref[0]
    hi_ref[...] = st_ref[1]


def _s5_weights(a_re, a_im, log_dt, b_re, b_im, c_re, c_im, d, w_glu, b_glu):
    eye = jnp.eye(B_GROUPS, dtype=F32)
    blk_in = lambda w: jnp.einsum('hg,gpc->hcgp', eye, w).reshape(B_WIDTH, S5_NS)
    blk_out = lambda w: jnp.einsum('gh,gcp->gphc', eye, w).reshape(S5_NS, B_WIDTH)
    wb = jnp.concatenate([blk_in(b_re), blk_in(b_im)], axis=1).astype(BF16)
    wc = jnp.concatenate([blk_out(c_re), -blk_out(c_im)], axis=0).astype(BF16)
    flat = lambda x: x.reshape(1, S5_NS)
    return (flat(a_re), flat(a_im), flat(jnp.repeat(log_dt, B_STATE)), wb, wc, d.reshape(1, B_WIDTH),
            w_glu.astype(BF16), b_glu.reshape(1, B_WIDTH))


def _s5_mix(u, h_re, h_im, weights):
    nb, t, _ = u.shape
    interleave = t > 1
    steps = min(t, S5_T_CHUNK)
    rows = nb * steps
    body = functools.partial(_s5_body, nb=nb, steps=steps, interleave=interleave)
    if interleave:
        u_in = u
        u_spec = pl.BlockSpec((nb, steps, B_WIDTH), lambda c: (0, c, 0))
        o_shape = jax.ShapeDtypeStruct((nb, t, B_WIDTH), F32)
        scratch_rows = rows
    else:
        u_in = u.reshape(nb, B_WIDTH)
        u_spec = pl.BlockSpec((nb, B_WIDTH), lambda c: (0, 0))
        o_shape = jax.ShapeDtypeStruct((nb, B_WIDTH), F32)
        scratch_rows = 8
    o, hr, hi = pl.pallas_call(
        body,
        grid=(t // steps,),
        in_specs=[
            u_spec, _const_spec((nb, S5_NS)), _const_spec((nb, S5_NS)),
            _const_spec((1, S5_NS)), _const_spec((1, S5_NS)), _const_spec((1, S5_NS)),
            _const_spec((B_WIDTH, 2 * S5_NS)), _const_spec((2 * S5_NS, B_WIDTH)),
            _const_spec((1, B_WIDTH)), _const_spec((B_WIDTH, B_WIDTH)), _const_spec((1, B_WIDTH)),
        ],
        out_specs=[u_spec, pl.BlockSpec((nb, S5_NS), lambda c: (0, 0)), pl.BlockSpec((nb, S5_NS), lambda c: (0, 0))],
        out_shape=[o_shape, jax.ShapeDtypeStruct((nb, S5_NS), F32), jax.ShapeDtypeStruct((nb, S5_NS), F32)],
        scratch_shapes=[
            pltpu.VMEM((4, nb, S5_NS), F32),
            pltpu.VMEM((2, nb, S5_NS), F32),
            pltpu.VMEM((rows, 2 * S5_NS), F32),
            pltpu.VMEM((B_WIDTH // 128, scratch_rows, 128), F32),
            pltpu.VMEM((B_WIDTH // 128, scratch_rows, 128), F32),
        ],
        compiler_params=pltpu.CompilerParams(
            dimension_semantics=("arbitrary",),
            vmem_limit_bytes=V7X_VMEM_LIMIT_BYTES),
        name="s5_mix",
    )(u_in, h_re, h_im, *weights)
    return o.reshape(nb, t, B_WIDTH), hr, hi


CHUNK_W = CMP_STRIDE * 2 * A_KVW
CHUNKS_PER_PAGE = PAGE_SIZE // CMP_STRIDE
SEL_PAD = 256


def _bf16_round(x):
    return x.astype(BF16).astype(F32)


def _compress_tail(c, pos, w2):
    c1 = c[:, 128:]
    nxt = jnp.concatenate([c1[1:], c1[:1]], axis=0)
    hid = (pos + c[:, :128]) + nxt
    return jnp.dot(_gelu_tanh(hid).astype(BF16), w2, preferred_element_type=F32)


def _rms_heads128(x, gain):
    left = lax.broadcasted_iota(jnp.int32, x.shape, 1) < HEAD_DIM
    sq = x * x
    s0 = jnp.sum(jnp.where(left, sq, 0.0), axis=-1, keepdims=True)
    s1 = jnp.sum(jnp.where(left, 0.0, sq), axis=-1, keepdims=True)
    ms = jnp.where(left, s0, s1) * (1.0 / HEAD_DIM)
    return x * lax.rsqrt(ms + EPS) * gain


def _pad_rows8(x):
    return jnp.concatenate([x, jnp.zeros((8 - x.shape[0], x.shape[1]), x.dtype)], axis=0)


def _nsa_sample_cmp_body(pt_ref, *refs, n_pages, n_sel):
    page_refs = refs[:n_pages]
    (q_ref, posk_ref, posv_ref, wpos_k_ref, wpos_v_ref, wk_ref, wv_ref, w2k_ref, w2v_ref, kgain_ref, cover_ref,
     oc_ref, idx_ref, pos_scr) = refs[n_pages:]

    @pl.when(pl.program_id(0) == 0)
    def _():
        pos_scr[0] = jnp.dot(posk_ref[...], wpos_k_ref[...], preferred_element_type=F32)
        pos_scr[1] = jnp.dot(posv_ref[...], wpos_v_ref[...], preferred_element_type=F32)

    n_chunks = n_pages * CHUNKS_PER_PAGE
    n_cmp = n_chunks - 1
    xb = jnp.concatenate([r[...] for r in page_refs], axis=0).astype(BF16)
    row_w = 2 * A_KVW
    xk = jnp.concatenate([xb[:, s * row_w:s * row_w + A_KVW] for s in range(CMP_STRIDE)], axis=1)
    xv = jnp.concatenate([xb[:, s * row_w + A_KVW:(s + 1) * row_w] for s in range(CMP_STRIDE)], axis=1)
    kraw = _compress_tail(jnp.dot(xk, wk_ref[...], preferred_element_type=F32), pos_scr[0, 0:1], w2k_ref[...])
    vcmp = _compress_tail(jnp.dot(xv, wv_ref[...], preferred_element_type=F32), pos_scr[1, 0:1], w2v_ref[...])
    kc = _rms_heads128(kraw, kgain_ref[...]).astype(BF16)
    vc = vcmp.astype(BF16)

    col = lax.broadcasted_iota(jnp.int32, (8, n_chunks), 1)
    s_idx = lax.broadcasted_iota(jnp.int32, (1, SEL_PAD), 1)
    s_idx_f = s_idx.astype(F32)
    forced = (s_idx == 0) | (s_idx == n_sel - 1) | (s_idx == n_sel - 2)
    lane = lax.broadcasted_iota(jnp.int32, (1, 128), 1)
    for g in range(A_KV):
        hs = slice(g * HEAD_DIM, (g + 1) * HEAD_DIM)
        q8 = (_pad_rows8(q_ref[g]) * (HEAD_DIM ** -0.5)).astype(BF16)
        s = jnp.where(col < n_cmp, _dot_nt(q8, kc[:, hs]), NEG_INF)
        e = jnp.exp(s - jnp.max(s, axis=-1, keepdims=True))
        p = (e / jnp.sum(e, axis=-1, keepdims=True)).astype(BF16)
        oc_ref[g] = jnp.dot(p, vc[:, hs], preferred_element_type=F32)[0:A_REP]
        imp = jnp.sum(jnp.dot(p, cover_ref[...], preferred_element_type=F32)[0:A_REP], axis=0, keepdims=True)
        score = jnp.where(s_idx < n_sel, imp + jnp.where(forced, FORCE_BONUS, 0.0), NEG_INF)
        rank = jnp.zeros((1, SEL_PAD), F32)
        for j in range(n_sel):
            cj = score[:, j:j + 1]
            beats = (cj > score) | ((cj == score) & (s_idx > j))
            rank = rank + jnp.where(beats, 1.0, 0.0)
        picks = jnp.zeros((1, 128), F32)
        for r in range(SEL_TOPK):
            block = jnp.sum(jnp.where(rank == float(r), s_idx_f, 0.0), axis=-1, keepdims=True)
            picks = picks + jnp.where(lane == r, block, 0.0)
        idx_ref[g:g + 1, :] = picks.astype(jnp.int32)


def _nsa_sample_att_body(idx_ref, pt_ref, *refs, n_past_blk):
    n_blk = A_KV * SEL_TOPK
    blk_refs = refs[:n_blk]
    (q_ref, gate_ref, oc_ref, newsel_ref, wcache_ref, newwin_ref, selb_ref, winb_ref, newb_ref, o_ref) = refs[n_blk:]
    b = pl.program_id(0)
    w_len = wcache_ref.shape[0]
    wcol = lax.broadcasted_iota(jnp.int32, (8, w_len), 1)

    def attend(q8, s, v, new_kv, new_bias, hs, vs):
        k_new = _bf16_round(new_kv[:, hs])
        v_new = _bf16_round(new_kv[:, vs])
        s_new = jnp.sum(q8.astype(F32) * k_new, axis=-1, keepdims=True) + new_bias
        m = jnp.maximum(jnp.max(s, axis=-1, keepdims=True), s_new)
        e = jnp.exp(s - m)
        e_new = jnp.exp(s_new - m)
        acc = jnp.dot(e.astype(BF16), v, preferred_element_type=F32) + _bf16_round(e_new) * v_new
        return acc / (jnp.sum(e, axis=-1, keepdims=True) + e_new)

    for g in range(A_KV):
        hs = slice(g * HEAD_DIM, (g + 1) * HEAD_DIM)
        vs = slice(A_KVW + g * HEAD_DIM, A_KVW + (g + 1) * HEAD_DIM)
        q8 = (_pad_rows8(q_ref[g]) * (HEAD_DIM ** -0.5)).astype(BF16)
        new_bias = newb_ref[g][:, 0:1]
        k_sel = jnp.concatenate([blk_refs[g * SEL_TOPK + k][:, hs] for k in range(SEL_TOPK)], axis=0).astype(BF16)
        v_sel = jnp.concatenate([blk_refs[g * SEL_TOPK + k][:, vs] for k in range(SEL_TOPK)], axis=0).astype(BF16)
        near = selb_ref[g]
        pieces = []
        for k in range(SEL_TOPK):
            ik = idx_ref[b * n_blk + g * SEL_TOPK + k]
            piece = jnp.where(ik == n_past_blk - 2, near[:, :SEL_BLOCK],
                              jnp.where(ik == n_past_blk - 1, near[:, SEL_BLOCK:], 0.0))
            pieces.append(jnp.where(ik >= n_past_blk, NEG_INF, piece))
        s_s = _dot_nt(q8, k_sel) + jnp.concatenate(pieces, axis=1)
        o_s = attend(q8, s_s, v_sel, newsel_ref[...], new_bias, hs, vs)
        s_w = jnp.where(wcol >= 1, _dot_nt(q8, wcache_ref[:, hs].astype(BF16)) + winb_ref[g], NEG_INF)
        o_w = attend(q8, s_w, wcache_ref[:, vs].astype(BF16), newwin_ref[...], new_bias, hs, vs)
        gates = gate_ref[g]
        o_ref[g] = (gates[:, 0:1] * oc_ref[g] + gates[:, 1:2] * o_s[0:A_REP] + gates[:, 2:3] * o_w[0:A_REP])


def _bucket_np(dist):
    n = np.maximum(dist, 0)
    exact = NUM_BUCKETS // 2
    nf = np.maximum(n, exact).astype(np.float64)
    large = exact + (np.log(nf / exact) / math.log(MAX_DISTANCE / exact) * (NUM_BUCKETS - exact)).astype(np.int64)
    return np.where(n < exact, n, np.minimum(large, NUM_BUCKETS - 1)).astype(np.int32)


def _nsa_sample(q, gates, new_sel, new_win, pool_cmp, pool_sel, win_cache, page_table,
                cmp_pos, cmp_w1, cmp_w2, k_gain, tab):
    n, n_pages = page_table.shape
    past = n_pages * PAGE_SIZE
    n_past_blk = past // SEL_BLOCK
    n_sel = n_past_blk + 1
    n_chunks = n_pages * CHUNKS_PER_PAGE
    w_len = win_cache.shape[1]
    assert n_sel <= SEL_PAD and w_len == WIN_A and past >= WIN_A
    pt_flat = page_table.reshape(-1)

    eye = jnp.eye(A_KV, dtype=F32)
    w1 = cmp_w1.reshape(2, 2, CMP_STRIDE, HEAD_DIM, HEAD_DIM)
    w_big = jnp.einsum('ajsdh,gk->asgdjkh', w1, eye).reshape(2, CMP_STRIDE * A_KVW, 2 * A_KVW).astype(BF16)
    w_pos = jnp.concatenate([cmp_w1, cmp_w1], axis=-1).astype(BF16)
    pos = jnp.broadcast_to(cmp_pos.reshape(2, 1, CMP_LEN * HEAD_DIM), (2, 8, CMP_LEN * HEAD_DIM)).astype(BF16)
    w2_big = jnp.einsum('ahd,gk->aghkd', cmp_w2, eye).reshape(2, A_KVW, A_KVW).astype(BF16)
    gain2 = jnp.tile(k_gain, A_KV).reshape(1, A_KVW)
    c_idx = np.arange(n_chunks)
    s_start = np.arange(SEL_PAD) * SEL_BLOCK
    cover = ((c_idx[:, None] * CMP_STRIDE < s_start[None, :] + SEL_BLOCK)
             & (c_idx[:, None] * CMP_STRIDE + CMP_LEN - 1 >= s_start[None, :])
             & (c_idx[:, None] < n_chunks - 1) & (np.arange(SEL_PAD)[None, :] < n_sel))
    cover = jnp.asarray(cover.astype(np.float32), BF16)

    head_spec = lambda last: pl.BlockSpec((None, A_KV, A_REP, last), lambda i, *_: (i, 0, 0, 0))
    page_specs = [pl.BlockSpec((None, CHUNKS_PER_PAGE, CHUNK_W),
                               functools.partial(lambda i, pt, p: (pt[i * n_pages + p], 0, 0), p=p))
                  for p in range(n_pages)]
    full = lambda shape: pl.BlockSpec(shape, lambda i, *_: (0,) * len(shape))
    pool_c = pool_cmp.reshape(pool_cmp.shape[0], CHUNKS_PER_PAGE, CHUNK_W)
    o_c, idx = pl.pallas_call(
        functools.partial(_nsa_sample_cmp_body, n_pages=n_pages, n_sel=n_sel),
        grid_spec=pltpu.PrefetchScalarGridSpec(
            num_scalar_prefetch=1,
            grid=(n,),
            in_specs=page_specs + [
                head_spec(HEAD_DIM),
                full((8, CMP_LEN * HEAD_DIM)), full((8, CMP_LEN * HEAD_DIM)),
                full((CMP_LEN * HEAD_DIM, A_KVW)), full((CMP_LEN * HEAD_DIM, A_KVW)),
                full((CMP_STRIDE * A_KVW, 2 * A_KVW)), full((CMP_STRIDE * A_KVW, 2 * A_KVW)),
                full((A_KVW, A_KVW)), full((A_KVW, A_KVW)), full((1, A_KVW)), full((n_chunks, SEL_PAD)),
            ],
            out_specs=[head_spec(HEAD_DIM), pl.BlockSpec((None, A_KV, 128), lambda i, *_: (i, 0, 0))],
            scratch_shapes=[pltpu.VMEM((2, 8, A_KVW), F32)],
        ),
        out_shape=[jax.ShapeDtypeStruct((n, A_KV, A_REP, HEAD_DIM), F32),
                   jax.ShapeDtypeStruct((n, A_KV, 128), jnp.int32)],
        compiler_params=pltpu.CompilerParams(
            dimension_semantics=("arbitrary",), vmem_limit_bytes=V7X_VMEM_LIMIT_BYTES),
        name="nsa_sample_cmp",
    )(pt_flat, *([pool_c] * n_pages), q, pos[0], pos[1], w_pos[0], w_pos[1], w_big[0], w_big[1],
      w2_big[0], w2_big[1], gain2, cover)

    tab_h = jnp.transpose(tab, (1, 2, 0))
    rel = lambda dist: jnp.pad(tab_h[:, :, _bucket_np(dist)] - tab_h[:, :, NUM_BUCKETS - 1:],
                               ((0, 0), (0, 8 - A_REP), (0, 0)))
    sel_bias = rel(past - ((n_past_blk - 2) * SEL_BLOCK + np.arange(2 * SEL_BLOCK)))
    win_bias = rel(w_len - np.arange(w_len))
    new_bias = rel(np.zeros(128, np.int64))

    n_blk = A_KV * SEL_TOPK
    idx_flat = idx[:, :, :SEL_TOPK].reshape(-1)
    blocks_per_page = PAGE_SIZE // SEL_BLOCK

    def blk_map(i, idx_s, pt, slot):
        blk = jnp.minimum(idx_s[i * n_blk + slot], n_past_blk - 1)
        return (pt[i * n_pages + blk // blocks_per_page] * blocks_per_page + blk % blocks_per_page, 0, 0)

    blk_specs = [pl.BlockSpec((None, SEL_BLOCK, 2 * A_KVW), functools.partial(blk_map, slot=s)) for s in range(n_blk)]
    pool_s = pool_sel.reshape(pool_sel.shape[0] * blocks_per_page, SEL_BLOCK, 2 * A_KVW)
    row_spec = pl.BlockSpec((None, 1, 2 * A_KVW), lambda i, *_: (i, 0, 0))
    return pl.pallas_call(
        functools.partial(_nsa_sample_att_body, n_past_blk=n_past_blk),
        grid_spec=pltpu.PrefetchScalarGridSpec(
            num_scalar_prefetch=2,
            grid=(n,),
            in_specs=blk_specs + [
                head_spec(HEAD_DIM), head_spec(3), head_spec(HEAD_DIM), row_spec,
                pl.BlockSpec((None, w_len, 2 * A_KVW), lambda i, *_: (i, 0, 0)), row_spec,
                full((A_KV, 8, 2 * SEL_BLOCK)), full((A_KV, 8, w_len)), full((A_KV, 8, 128)),
            ],
            out_specs=head_spec(HEAD_DIM),
        ),
        out_shape=jax.ShapeDtypeStruct((n, A_KV, A_REP, HEAD_DIM), F32),
        compiler_params=pltpu.CompilerParams(
            dimension_semantics=("arbitrary",), vmem_limit_bytes=V7X_VMEM_LIMIT_BYTES),
        name="nsa_sample_att",
    )(idx_flat, pt_flat, *([pool_s] * n_blk), q, gates, o_c, new_sel, win_cache, new_win,
      sel_bias, win_bias, new_bias)


def rms_norm(x, g):
    xf = x.astype(jnp.float32)
    y = xf * lax.rsqrt(jnp.mean(xf * xf, axis=-1, keepdims=True) + EPS)
    return (y * g.astype(jnp.float32)).astype(x.dtype)


def rel_bucket(dist):
    n = jnp.maximum(dist, 0)
    exact = NUM_BUCKETS // 2
    nf = jnp.maximum(n, exact).astype(jnp.float32)
    large = exact + (jnp.log(nf / exact) / math.log(MAX_DISTANCE / exact) * (NUM_BUCKETS - exact)).astype(jnp.int32)
    return jnp.where(n < exact, n, jnp.minimum(large, NUM_BUCKETS - 1))


def masked_softmax(s, mask):
    s = jnp.where(mask, s.astype(jnp.float32), -jnp.inf)
    m = jnp.max(s, axis=-1, keepdims=True)
    m = jnp.where(jnp.isfinite(m), m, 0.0)
    e = jnp.exp(s - m)
    d = jnp.sum(e, axis=-1, keepdims=True)
    return e / jnp.where(d > 0, d, 1.0)


def compress(rows, pos, w1, w2):
    b, t, g, dh = rows.shape
    r = CMP_LEN // CMP_STRIDE
    nch = t // CMP_STRIDE
    chunks = rows[:, :nch * CMP_STRIDE].reshape(b, nch, CMP_STRIDE, g, dh)
    w1b = w1.reshape(r, CMP_STRIDE, dh, w1.shape[-1])
    nc = nch - r + 1
    hid = jnp.einsum('ld,ldh->h', pos, w1.reshape(CMP_LEN, dh, w1.shape[-1]))
    for j in range(r):
        hid = hid + jnp.einsum('bnsgd,sdh->bngh', chunks[:, j:j + nc], w1b[j])
    return jnp.einsum('bngh,hd->bngd', jax.nn.gelu(hid), w2)


def compressed_kv(k_rows, v_rows, cmp_pos, cmp_w1, cmp_w2, k_gain):
    kc = rms_norm(compress(k_rows, cmp_pos[0], cmp_w1[0], cmp_w2[0]), k_gain)
    vc = compress(v_rows, cmp_pos[1], cmp_w1[1], cmp_w2[1])
    c_end = jnp.arange(kc.shape[1]) * CMP_STRIDE + (CMP_LEN - 1)
    return kc, vc, c_end


def nsa_attend(q, q_pos, kc, vc, c_end, n_sel, gather_sel, kw, vw, w_pos, gates, tab):
    scale = HEAD_DIM ** -0.5
    g_n = q.shape[2]
    s_c = jnp.einsum('bqgrd,bngd->bgrqn', q, kc) * scale
    p_c = masked_softmax(s_c, c_end[None, :] <= q_pos[:, None])
    o_c = jnp.einsum('bgrqn,bngd->bqgrd', p_c.astype(vc.dtype), vc)
    s_start = jnp.arange(n_sel) * SEL_BLOCK
    c_start = c_end - (CMP_LEN - 1)
    cover = ((c_start[:, None] < s_start[None, :] + SEL_BLOCK) & (c_end[:, None] >= s_start[None, :])).astype(jnp.float32)
    imp = jnp.einsum('bgrqn,ns->bgqs', p_c, cover)
    qblk = q_pos[:, None] // SEL_BLOCK
    sblk = jnp.arange(n_sel)[None, :]
    forced = ((sblk == 0) | (sblk == qblk) | (sblk == qblk - 1)).astype(jnp.float32)
    score = jnp.where(sblk <= qblk, imp + FORCE_BONUS * forced, -jnp.inf)
    _, idx = lax.top_k(score, min(SEL_TOPK, n_sel))
    k_sel, v_sel = gather_sel(idx)
    kpos = idx[..., None] * SEL_BLOCK + jnp.arange(SEL_BLOCK)
    dist = q_pos[:, None, None] - kpos
    tab_g = jnp.transpose(tab, (1, 0, 2))
    gi = jnp.arange(g_n)[None, :, None, None, None]
    bias = jnp.moveaxis(tab_g[gi, rel_bucket(dist)], -1, 3)
    s_s = jnp.einsum('bqgrd,bgqkjd->bgqrkj', q, k_sel) * scale + bias
    shp = s_s.shape
    mask = jnp.broadcast_to((dist >= 0)[:, :, :, None], shp)
    p_s = masked_softmax(s_s.reshape(shp[:4] + (-1,)), mask.reshape(shp[:4] + (-1,))).reshape(shp)
    o_s = jnp.einsum('bgqrkj,bgqkjd->bqgrd', p_s.astype(v_sel.dtype), v_sel)
    wd = q_pos[:, None] - w_pos[None, :]
    wmask = (wd >= 0) & (wd < WIN_A) & (w_pos[None, :] >= 0)
    wbias = jnp.transpose(tab[rel_bucket(wd)], (2, 3, 0, 1))
    s_w = jnp.einsum('bqgrd,bwgd->bgrqw', q, kw) * scale + wbias
    p_w = masked_softmax(s_w, wmask)
    o_w = jnp.einsum('bgrqw,bwgd->bqgrd', p_w.astype(vw.dtype), vw)
    return gates[..., 0:1] * o_c + gates[..., 1:2] * o_s + gates[..., 2:3] * o_w


def nsa_prompt(q, gates, kc_rows, vc_rows, ks, vs, kw, vw, cmp_pos, cmp_w1, cmp_w2, k_gain, tab):
    b, t = q.shape[:2]
    kc, vc, c_end = compressed_kv(kc_rows, vc_rows, cmp_pos, cmp_w1, cmp_w2, k_gain)
    n_sel = t // SEL_BLOCK
    ks_b = ks.reshape(b, n_sel, SEL_BLOCK, A_KV, HEAD_DIM).transpose(0, 3, 1, 2, 4)
    vs_b = vs.reshape(b, n_sel, SEL_BLOCK, A_KV, HEAD_DIM).transpose(0, 3, 1, 2, 4)
    bi = jnp.arange(b)[:, None, None, None]
    gi = jnp.arange(A_KV)[None, :, None, None]

    def gather_sel(idx):
        return ks_b[bi, gi, idx], vs_b[bi, gi, idx]

    pad = ((0, 0), (WIN_A, 0), (0, 0), (0, 0))
    kw_p, vw_p = jnp.pad(kw, pad), jnp.pad(vw, pad)

    def chunk(start):
        q_pos = start + jnp.arange(NSA_QBLK)
        qc = lax.dynamic_slice_in_dim(q, start, NSA_QBLK, axis=1)
        gc = lax.dynamic_slice_in_dim(gates, start, NSA_QBLK, axis=1)
        kwc = lax.dynamic_slice_in_dim(kw_p, start, NSA_QBLK + WIN_A, axis=1)
        vwc = lax.dynamic_slice_in_dim(vw_p, start, NSA_QBLK + WIN_A, axis=1)
        w_pos = start - WIN_A + jnp.arange(NSA_QBLK + WIN_A)
        return nsa_attend(qc, q_pos, kc, vc, c_end, n_sel, gather_sel, kwc, vwc, w_pos, gc, tab)

    out = lax.map(chunk, jnp.arange(t // NSA_QBLK) * NSA_QBLK)
    return jnp.moveaxis(out, 0, 1).reshape(b, t, A_KV, A_REP, HEAD_DIM)


def _nsa_prompt_glue(q, gates, kc_rows, vc_rows, ks, vs, kw, vw, cmp_pos, cmp_w1, cmp_w2, k_gain, tab):
    b, t = q.shape[:2]
    kc, vc, _ = compressed_kv(kc_rows, vc_rows, cmp_pos, cmp_w1, cmp_w2, k_gain)
    pad = lambda x: jnp.pad(x, ((0, 0), (0, 128 - x.shape[1]), (0, 0), (0, 0)))
    heads_first = lambda x: jnp.transpose(x, (0, 2, 1, 3)).astype(BF16)
    o = _nsa_prompt_attend(q.reshape(b, t, A_Q),
                           jnp.transpose(gates.reshape(b, t, A_KV, 3 * A_REP), (0, 2, 1, 3)),
                           heads_first(pad(kc)), heads_first(pad(vc)), heads_first(ks), heads_first(vs),
                           heads_first(kw), heads_first(vw), tab)
    return o.reshape(b, t, A_KV, A_REP, HEAD_DIM)


def nsa_sample(q, gates, new_c, new_s, win_all, past, pool_cmp, pool_sel, page_table,
               cmp_pos, cmp_w1, cmp_w2, k_gain, tab):
    db, s = q.shape[:2]
    past_c = pool_cmp[page_table].reshape(db, past, 2, A_KV, HEAD_DIM)
    rows_c = jnp.concatenate([past_c, new_c], axis=1)
    kc, vc, c_end = compressed_kv(rows_c[:, :, 0], rows_c[:, :, 1], cmp_pos, cmp_w1, cmp_w2, k_gain)
    q_pos = past + jnp.arange(s)
    n_past_blk = past // SEL_BLOCK
    n_new_blk = -(-s // SEL_BLOCK)
    n_sel = n_past_blk + n_new_blk
    bpp = PAGE_SIZE // SEL_BLOCK
    pool_b = pool_sel.reshape(pool_sel.shape[0], bpp, SEL_BLOCK, 2, A_KV, HEAD_DIM)
    new_b = jnp.pad(new_s, ((0, 0), (0, n_new_blk * SEL_BLOCK - s), (0, 0), (0, 0), (0, 0)))
    new_b = new_b.reshape(db, n_new_blk, SEL_BLOCK, 2, A_KV, HEAD_DIM)
    bi = jnp.arange(db)[:, None, None, None]
    gi = jnp.arange(A_KV)[None, :, None, None]

    def gather_sel(idx):
        pidx = jnp.clip(idx, 0, n_past_blk - 1)
        phys = page_table[bi, pidx // bpp]
        pool_f = pool_b.reshape((-1,) + pool_b.shape[2:])
        both = pool_f[phys * bpp + pidx % bpp]
        from_pool = jnp.stack([both[:, g, ..., g, :] for g in range(A_KV)], axis=1)
        from_new = new_b[bi, jnp.clip(idx - n_past_blk, 0, n_new_blk - 1), :, :, gi]
        kv = jnp.where((idx < n_past_blk)[..., None, None, None], from_pool, from_new)
        return kv[..., 0, :], kv[..., 1, :]

    w_len = win_all.shape[1]
    w_pos = past + s - w_len + jnp.arange(w_len)
    return nsa_attend(q, q_pos, kc, vc, c_end, n_sel, gather_sel, win_all[:, :, 0], win_all[:, :, 1],
                      w_pos, gates, tab)


def _ssm_combine(e1, e2):
    a1r, a1i, b1r, b1i = e1
    a2r, a2i, b2r, b2i = e2
    return (a2r * a1r - a2i * a1i, a2r * a1i + a2i * a1r,
            a2r * b1r - a2i * b1i + b2r, a2r * b1i + a2i * b1r + b2i)


def s5_mix(u, h_re, h_im, a_re, a_im, log_dt, b_re, b_im, c_re, c_im, d, w_glu, b_glu):
    bsz, t, _ = u.shape
    f32 = jnp.float32
    uf = u.astype(f32)
    ug = uf.reshape(bsz, t, B_GROUPS, B_GROUP)
    dt = jnp.exp(log_dt.astype(f32))[:, None]
    ar, ai = a_re.astype(f32), a_im.astype(f32)
    mag = jnp.exp(ar * dt)
    abr, abi = mag * jnp.cos(ai * dt), mag * jnp.sin(ai * dt)
    den = ar * ar + ai * ai
    wr = ((abr - 1.0) * ar + abi * ai) / den
    wi = (abi * ar - (abr - 1.0) * ai) / den
    bu_r = jnp.einsum('gpc,btgc->btgp', b_re.astype(f32), ug)
    bu_i = jnp.einsum('gpc,btgc->btgp', b_im.astype(f32), ug)
    x_r = wr * bu_r - wi * bu_i
    x_i = wr * bu_i + wi * bu_r
    hr, hi = h_re.astype(f32), h_im.astype(f32)
    x_r = x_r.at[:, 0].add(abr * hr - abi * hi)
    x_i = x_i.at[:, 0].add(abr * hi + abi * hr)
    a_r = jnp.broadcast_to(abr, x_r.shape)
    a_i = jnp.broadcast_to(abi, x_i.shape)
    _, _, s_r, s_i = lax.associative_scan(_ssm_combine, (a_r, a_i, x_r, x_i), axis=1)
    y = jnp.einsum('gcp,btgp->btgc', c_re.astype(f32), s_r) - jnp.einsum('gcp,btgp->btgc', c_im.astype(f32), s_i)
    y = y.reshape(bsz, t, B_WIDTH) + d.astype(f32) * uf
    z = jax.nn.gelu(y)
    out = z * jax.nn.sigmoid(z @ w_glu.astype(f32) + b_glu.astype(f32))
    return out.astype(u.dtype), s_r[:, -1], s_i[:, -1]


def nsa_s5_inputs(xn, w_in, q_gain, k_gain):
    b, t, _ = xn.shape
    sizes = [A_Q] + [A_KVW] * 6 + [A_GATE, B_WIDTH]
    cuts = [int(c) for c in np.cumsum(sizes)[:-1]]
    q, kc, vc, ks, vs, kw, vw, gl, u = jnp.split(xn @ w_in, cuts, axis=-1)
    heads = lambda z: z.reshape(b, t, A_KV, HEAD_DIM)
    q = rms_norm(q.reshape(b, t, A_KV, A_REP, HEAD_DIM), q_gain)
    gates = jax.nn.sigmoid(gl).reshape(b, t, A_KV, A_REP, 3)
    return (q, rms_norm(heads(kc), k_gain), heads(vc), rms_norm(heads(ks), k_gain), heads(vs),
            rms_norm(heads(kw), k_gain), heads(vw), gates, u)


def swa_inputs(xn, w_in, q_gain, k_gain):
    b, t, _ = xn.shape
    q, k, v = jnp.split(xn @ w_in, [C_HEADS * HEAD_DIM, (C_HEADS + C_KV) * HEAD_DIM], axis=-1)
    q = rms_norm(q.reshape(b, t, C_KV, C_REP, HEAD_DIM), q_gain)
    k = rms_norm(k.reshape(b, t, C_KV, HEAD_DIM), k_gain)
    return q, k, v.reshape(b, t, C_KV, HEAD_DIM)


def swa_attend(q, q_pos, k, v, k_pos, sinks, tab):
    dist = q_pos[:, :, None] - k_pos[:, None, :]
    mask = (dist >= 0) & (dist < WIN_C) & (k_pos[:, None, :] >= 0)
    bias = jnp.transpose(tab[rel_bucket(dist)], (3, 4, 0, 1, 2))
    s = jnp.einsum('bnqgrd,bnkgd->bgrnqk', q, k).astype(jnp.float32) * (HEAD_DIM ** -0.5) + bias
    s = jnp.where(mask, s, -jnp.inf)
    sink = sinks.astype(jnp.float32).reshape(C_KV, C_REP)[:, :, None, None, None]
    m = jnp.maximum(jnp.max(s, axis=-1, keepdims=True), sink)
    e = jnp.exp(s - m)
    p = e / (jnp.sum(e, axis=-1, keepdims=True) + jnp.exp(sink - m))
    return jnp.einsum('bgrnqk,bnkgd->bnqgrd', p.astype(v.dtype), v)


def swa_prompt_attend(q, k, v, sinks, tab):
    b, t = q.shape[:2]
    nb = t // WIN_C
    qb = q.reshape(b, nb, WIN_C, C_KV, C_REP, HEAD_DIM)

    def band(x):
        xb = x.reshape(b, nb, WIN_C, C_KV, HEAD_DIM)
        prev = jnp.pad(xb[:, :-1], ((0, 0), (1, 0), (0, 0), (0, 0), (0, 0)))
        return jnp.concatenate([prev, xb], axis=2)

    q_pos = jnp.arange(t).reshape(nb, WIN_C)
    k_pos = (jnp.arange(nb)[:, None] - 1) * WIN_C + jnp.arange(2 * WIN_C)[None, :]
    o = swa_attend(qb, q_pos, band(k), band(v), k_pos, sinks, tab)
    return o.reshape(b, t, C_HEADS * HEAD_DIM)


def kernel(x_prompt, x_sample, cache_nsa_cmp, cache_nsa_sel, cache_nsa_win, state_s5_re, state_s5_im,
           cache_swa, state_ffn_conv, page_table, rel_bias, norm_mix, norm_ffn, a_w_in, a_w_out,
           nsa_q_gain, nsa_k_gain, nsa_cmp_pos, nsa_cmp_w1, nsa_cmp_w2, s5_a_re, s5_a_im, s5_log_dt,
           s5_b_re, s5_b_im, s5_c_re, s5_c_im, s5_d, s5_w_glu, s5_b_glu, c_w_in, c_w_out, c_q_gain,
           c_k_gain, c_sinks, ffn_w_up, ffn_w_gate, ffn_conv_w, ffn_conv_b, ffn_w_down):
    bp, tp, _ = x_prompt.shape
    bs, ts, _ = x_sample.shape
    past = page_table.shape[1] * PAGE_SIZE
    tab_a = rel_bias[:, :A_HEADS].reshape(NUM_BUCKETS, A_KV, A_REP)
    tab_c = rel_bias[:, :C_HEADS].reshape(NUM_BUCKETS, C_KV, C_REP)
    hp, hs = x_prompt, x_sample
    cmp_p, cmp_s, sel_p, sel_s, win_p, win_s = [], [], [], [], [], []
    s5r_p, s5i_p, s5r_s, s5i_s = [], [], [], []
    swa_p, swa_s, conv_p, conv_s = [], [], [], []
    for layer in range(DEPTH):
        i = layer // 2
        xp = rms_norm(hp, norm_mix[layer])
        xs = rms_norm(hs, norm_mix[layer])
        if layer % 2 == 0:
            qp, kcp, vcp, ksp, vsp, kwp, vwp, gp, up = nsa_s5_inputs(xp, a_w_in[i], nsa_q_gain[i], nsa_k_gain[i])
            qs, kcs, vcs, kss, vss, kws, vws, gs, us = nsa_s5_inputs(xs, a_w_in[i], nsa_q_gain[i], nsa_k_gain[i])
            cmp_args = (nsa_cmp_pos[i], nsa_cmp_w1[i], nsa_cmp_w2[i], nsa_k_gain[i], tab_a)
            o_ap = _nsa_prompt_glue(qp, gp, kcp, vcp, ksp, vsp, kwp, vwp, *cmp_args)
            new_c = jnp.stack([kcs, vcs], axis=2)
            new_sel = jnp.stack([kss, vss], axis=2)
            win_all = jnp.concatenate([cache_nsa_win[i].astype(kws.dtype), jnp.stack([kws, vws], axis=2)], axis=1)
            assert ts == 1
            flat_kv = lambda k, v: jnp.concatenate([k.reshape(bs, 1, A_KVW), v.reshape(bs, 1, A_KVW)], axis=-1)
            o_as = _nsa_sample(qs.reshape(bs, A_KV, A_REP, HEAD_DIM), gs.reshape(bs, A_KV, A_REP, 3),
                               flat_kv(kss, vss), flat_kv(kws, vws),
                               cache_nsa_cmp[i].reshape(-1, PAGE_SIZE, 2 * A_KVW),
                               cache_nsa_sel[i].reshape(-1, PAGE_SIZE, 2 * A_KVW),
                               cache_nsa_win[i].reshape(bs, -1, 2 * A_KVW), page_table, *cmp_args)
            s5_args = (s5_a_re[i], s5_a_im[i], s5_log_dt[i], s5_b_re[i], s5_b_im[i], s5_c_re[i], s5_c_im[i],
                       s5_d[i], s5_w_glu[i], s5_b_glu[i])
            s5w = _s5_weights(*s5_args)
            h0 = jnp.zeros((bp, S5_NS), F32)
            o_bp, hr_p, hi_p = _s5_mix(up, h0, h0, s5w)
            o_bs, hr_s, hi_s = _s5_mix(us, state_s5_re[i].reshape(bs, S5_NS), state_s5_im[i].reshape(bs, S5_NS), s5w)
            hr_p, hi_p = (x.reshape(bp, B_GROUPS, B_STATE) for x in (hr_p, hi_p))
            hr_s, hi_s = (x.reshape(bs, B_GROUPS, B_STATE) for x in (hr_s, hi_s))
            hp = hp + jnp.concatenate([o_ap.reshape(bp, tp, A_Q), o_bp], axis=-1) @ a_w_out[i]
            hs = hs + jnp.concatenate([o_as.reshape(bs, ts, A_Q), o_bs], axis=-1) @ a_w_out[i]
            cmp_p.append(jnp.stack([kcp, vcp], axis=2))
            cmp_s.append(new_c)
            sel_p.append(jnp.stack([ksp, vsp], axis=2))
            sel_s.append(new_sel)
            win_p.append(jnp.stack([kwp, vwp], axis=2)[:, -min(WIN_A, tp):])
            win_s.append(win_all[:, -min(WIN_A, win_all.shape[1]):])
            s5r_p.append(hr_p)
            s5i_p.append(hi_p)
            s5r_s.append(hr_s)
            s5i_s.append(hi_s)
        else:
            qp, kp, vp = swa_inputs(xp, c_w_in[i], c_q_gain[i], c_k_gain[i])
            qs, k_s, v_s = swa_inputs(xs, c_w_in[i], c_q_gain[i], c_k_gain[i])
            o_cp = swa_prompt_attend(qp, kp, vp, c_sinks[i], tab_c)
            kv_all = jnp.concatenate([cache_swa[i].astype(k_s.dtype), jnp.stack([k_s, v_s], axis=2)], axis=1)
            kv_len = kv_all.shape[1]
            q_pos = (past + jnp.arange(ts))[None, :]
            k_pos = (past + ts - kv_len + jnp.arange(kv_len))[None, :]
            o_cs = swa_attend(qs[:, None], q_pos, kv_all[:, None, :, 0], kv_all[:, None, :, 1], k_pos,
                              c_sinks[i], tab_c).reshape(bs, ts, C_HEADS * HEAD_DIM)
            hp = hp + o_cp @ c_w_out[i]
            hs = hs + o_cs @ c_w_out[i]
            swa_p.append(jnp.stack([kp, vp], axis=2)[:, -min(WIN_C, tp):])
            swa_s.append(kv_all[:, -min(WIN_C, kv_len):])
        wup = ffn_w_up[layer].astype(BF16)
        wgate = ffn_w_gate[layer].astype(BF16)
        wdown = ffn_w_down[layer].astype(BF16)
        gain = norm_ffn[layer].reshape(1, D_MODEL)
        cw = ffn_conv_w[layer]
        cb = ffn_conv_b[layer].reshape(1, D_FF)
        hp, cp = _ffn_prompt(hp, gain, jnp.zeros((bp, CONV_W - 1, D_FF), F32), wup, wgate, cw, cb, wdown)
        hs2, cs = _ffn_sample(hs.reshape(bs * ts, D_MODEL), gain,
                              state_ffn_conv[layer].reshape(bs, (CONV_W - 1) * D_FF), wup, wgate, cw, cb, wdown)
        hs = hs2.reshape(bs, ts, D_MODEL)
        conv_p.append(cp)
        conv_s.append(cs.reshape(bs, CONV_W - 1, D_FF))
    y_prompt, y_sample = hp, hs
    return (y_prompt, y_sample, jnp.stack(cmp_p), jnp.stack(cmp_s), jnp.stack(sel_p), jnp.stack(sel_s),
            jnp.stack(win_p), jnp.stack(win_s), jnp.stack(s5r_p), jnp.stack(s5i_p), jnp.stack(s5r_s),
            jnp.stack(s5i_s), jnp.stack(swa_p), jnp.stack(swa_s), jnp.stack(conv_p), jnp.stack(conv_s))
```

```python
import functools
import math

import jax
import jax.numpy as jnp
import numpy as np
from jax import lax
from jax.experimental import pallas as pl
from jax.experimental.pallas import tpu as pltpu

D_MODEL = 1024
DEPTH = 2
PAGE_SIZE = 128
HEAD_DIM = 64
A_HEADS = 8
A_KV = 2
A_REP = A_HEADS // A_KV
A_Q = A_HEADS * HEAD_DIM
A_KVW = A_KV * HEAD_DIM
A_GATE = 3 * A_HEADS
CMP_LEN = 32
CMP_STRIDE = 16
SEL_BLOCK = 64
SEL_TOPK = 16
WIN_A = 512
NSA_QBLK = 64
FORCE_BONUS = 1000.0
B_WIDTH = D_MODEL // 2
B_GROUP = 16
B_GROUPS = B_WIDTH // B_GROUP
B_STATE = 64
C_HEADS = D_MODEL // HEAD_DIM
C_KV = 2
C_REP = C_HEADS // C_KV
WIN_C = 128
NUM_BUCKETS = 32
MAX_DISTANCE = 128
D_FF = 2816
CONV_W = 3
EPS = 1e-6

F32 = jnp.float32
BF16 = jnp.bfloat16

V7X_VMEM_LIMIT_BYTES = 56 * 1024 * 1024
FFN_ROW_TILE = 512
FFN_COL_CHUNK = 1408


def _gelu_tanh(x):
    return 0.5 * x * (1.0 + jnp.tanh(math.sqrt(2.0 / math.pi) * (x + 0.044715 * (x * x * x))))


def _rms_rows(x, gain):
    return x * lax.rsqrt(jnp.mean(x * x, axis=-1, keepdims=True) + EPS) * gain


def _const_spec(shape):
    zeros = (0,) * len(shape)
    return pl.BlockSpec(shape, lambda *_: zeros, pipeline_mode=pl.Buffered(1))


def _ffn_prompt_body(x_ref, gain_ref, prev_ref, wup_ref, wgate_ref, cw_ref, cb_ref, wdown_ref,
                     y_ref, cs_ref, hbuf_ref, *, tm, ffc):
    t = pl.program_id(1)
    x = x_ref[...]
    xn = _rms_rows(x, gain_ref[...]).astype(BF16)
    acc = x
    for c in range(D_FF // ffc):
        lo = c * ffc
        h = jnp.dot(xn, wup_ref[:, lo:lo + ffc], preferred_element_type=F32)
        g = jnp.dot(xn, wgate_ref[:, lo:lo + ffc], preferred_element_type=F32)

        @pl.when(t == 0)
        def _():
            hbuf_ref[c, 6:8, :] = prev_ref[:, lo:lo + ffc]

        hbuf_ref[c, 8:8 + tm, :] = h
        hm1 = hbuf_ref[c, 7:7 + tm, :]
        hm2 = hbuf_ref[c, 6:6 + tm, :]
        cw = cw_ref[:, lo:lo + ffc]
        hc = cw[0:1] * hm2 + cw[1:2] * hm1 + cw[2:3] * h + cb_ref[:, lo:lo + ffc]
        a = (_gelu_tanh(hc) * g).astype(BF16)
        acc = acc + jnp.dot(a, wdown_ref[lo:lo + ffc, :], preferred_element_type=F32)
        hbuf_ref[c, 0:8, :] = h[tm - 8:tm, :]
        cs_ref[:, lo:lo + ffc] = h[tm - 2:tm, :]
    y_ref[...] = acc


def _ffn_prompt(x, gain, prev, wup, wgate, cw, cb, wdown):
    b, t, d = x.shape
    tm, ffc = FFN_ROW_TILE, FFN_COL_CHUNK
    body = functools.partial(_ffn_prompt_body, tm=tm, ffc=ffc)
    return pl.pallas_call(
        body,
        grid=(b, t // tm),
        in_specs=[
            pl.BlockSpec((None, tm, d), lambda i, j: (i, j, 0)),
            _const_spec((1, d)),
            pl.BlockSpec((None, CONV_W - 1, D_FF), lambda i, j: (i, 0, 0)),
            _const_spec((d, D_FF)),
            _const_spec((d, D_FF)),
            _const_spec((CONV_W, D_FF)),
            _const_spec((1, D_FF)),
            _const_spec((D_FF, d)),
        ],
        out_specs=[
            pl.BlockSpec((None, tm, d), lambda i, j: (i, j, 0)),
            pl.BlockSpec((None, CONV_W - 1, D_FF), lambda i, j: (i, 0, 0)),
        ],
        out_shape=[
            jax.ShapeDtypeStruct((b, t, d), F32),
            jax.ShapeDtypeStruct((b, CONV_W - 1, D_FF), F32),
        ],
        scratch_shapes=[pltpu.VMEM((D_FF // ffc, 8 + tm, ffc), F32)],
        compiler_params=pltpu.CompilerParams(
            dimension_semantics=("parallel", "arbitrary"),
            vmem_limit_bytes=V7X_VMEM_LIMIT_BYTES),
        name="ffn_prompt",
    )(x, gain, prev, wup, wgate, cw, cb, wdown)


def _ffn_sample_body(x_ref, gain_ref, prev_ref, wup_ref, wgate_ref, cw_ref, cb_ref, wdown_ref,
                     y_ref, cs_ref, *, ffc):
    x = x_ref[...]
    xn = _rms_rows(x, gain_ref[...]).astype(BF16)
    acc = x
    for c in range(D_FF // ffc):
        lo = c * ffc
        h = jnp.dot(xn, wup_ref[:, lo:lo + ffc], preferred_element_type=F32)
        g = jnp.dot(xn, wgate_ref[:, lo:lo + ffc], preferred_element_type=F32)
        hm2 = prev_ref[:, lo:lo + ffc]
        hm1 = prev_ref[:, D_FF + lo:D_FF + lo + ffc]
        cw = cw_ref[:, lo:lo + ffc]
        hc = cw[0:1] * hm2 + cw[1:2] * hm1 + cw[2:3] * h + cb_ref[:, lo:lo + ffc]
        a = (_gelu_tanh(hc) * g).astype(BF16)
        acc = acc + jnp.dot(a, wdown_ref[lo:lo + ffc, :], preferred_element_type=F32)
        cs_ref[:, lo:lo + ffc] = hm1
        cs_ref[:, D_FF + lo:D_FF + lo + ffc] = h
    y_ref[...] = acc


def _ffn_sample(x, gain, prev, wup, wgate, cw, cb, wdown):
    n, d = x.shape
    body = functools.partial(_ffn_sample_body, ffc=FFN_COL_CHUNK)
    return pl.pallas_call(
        body,
        grid=(1,),
        in_specs=[
            _const_spec((n, d)),
            _const_spec((1, d)),
            _const_spec((n, (CONV_W - 1) * D_FF)),
            _const_spec((d, D_FF)),
            _const_spec((d, D_FF)),
            _const_spec((CONV_W, D_FF)),
            _const_spec((1, D_FF)),
            _const_spec((D_FF, d)),
        ],
        out_specs=[
            pl.BlockSpec((n, d), lambda i: (0, 0)),
            pl.BlockSpec((n, (CONV_W - 1) * D_FF), lambda i: (0, 0)),
        ],
        out_shape=[
            jax.ShapeDtypeStruct((n, d), F32),
            jax.ShapeDtypeStruct((n, (CONV_W - 1) * D_FF), F32),
        ],
        compiler_params=pltpu.CompilerParams(
            dimension_semantics=("arbitrary",),
            vmem_limit_bytes=V7X_VMEM_LIMIT_BYTES),
        name="ffn_sample",
    )(x, gain, prev, wup, wgate, cw, cb, wdown)


NSA_Q_TILE = 128
NSA_FAR_TILE = 512
NSA_NEAR = 2 * NSA_Q_TILE
NEG_INF = float("-inf")


def _dot_nt(a, b):
    return lax.dot_general(a, b, (((1,), (1,)), ((), ())), preferred_element_type=F32)


def _softmax_start(s, v):
    m = jnp.max(s, axis=-1, keepdims=True)
    e = jnp.exp(s - m)
    return m, jnp.sum(e, axis=-1, keepdims=True), jnp.dot(e.astype(BF16), v, preferred_element_type=F32)


def _softmax_more(carry, s, v):
    m, l, acc = carry
    m_new = jnp.maximum(m, jnp.max(s, axis=-1, keepdims=True))
    alpha = jnp.exp(m - m_new)
    e = jnp.exp(s - m_new)
    return (m_new, alpha * l + jnp.sum(e, axis=-1, keepdims=True),
            alpha * acc + jnp.dot(e.astype(BF16), v, preferred_element_type=F32))


def _block_expand(k0, width):
    kpos = k0 + lax.broadcasted_iota(jnp.int32, (128, width), 1)
    blk = lax.broadcasted_iota(jnp.int32, (128, width), 0)
    return jnp.where((kpos >> 6) == blk, 1.0, 0.0).astype(BF16)


def _nsa_prompt_body(q_ref, gate_ref, kc_ref, vc_ref, ks_ref, vs_ref, kw_ref, vw_ref, btile_ref, cover_ref,
                     o_ref, *, tq):
    qt = pl.program_id(2)
    q0 = pl.multiple_of(qt * tq, tq)
    rows = A_REP * tq
    q = q_ref[...]
    qs = jnp.concatenate([q[:, r * HEAD_DIM:(r + 1) * HEAD_DIM] for r in range(A_REP)], axis=0)
    qs = (qs * (HEAD_DIM ** -0.5)).astype(BF16)

    def row_pos(width):
        return q0 + (lax.broadcasted_iota(jnp.int32, (rows, width), 0) & (tq - 1))

    def col_idx(width):
        return lax.broadcasted_iota(jnp.int32, (rows, width), 1)

    n_idx = col_idx(128)
    valid_c = (n_idx * CMP_STRIDE + (CMP_LEN - 1) <= row_pos(128)) & (n_idx < 127)
    s_c = jnp.where(valid_c, _dot_nt(qs, kc_ref[...]), NEG_INF)
    m_c = jnp.max(s_c, axis=-1, keepdims=True)
    m_c = jnp.where(m_c == NEG_INF, 0.0, m_c)
    e_c = jnp.exp(s_c - m_c)
    d_c = jnp.sum(e_c, axis=-1, keepdims=True)
    p_c = (e_c / jnp.where(d_c > 0, d_c, 1.0)).astype(BF16)
    o_c = jnp.dot(p_c, vc_ref[...], preferred_element_type=F32)
    p_heads = jnp.concatenate([p_c[r * tq:(r + 1) * tq] for r in range(A_REP)], axis=1)
    imp = jnp.dot(p_heads, cover_ref[...], preferred_element_type=F32)

    s_idx = lax.broadcasted_iota(jnp.int32, (tq, 128), 1)
    qblk = (q0 + lax.broadcasted_iota(jnp.int32, (tq, 128), 0)) >> 6
    forced = (s_idx == 0) | (s_idx == qblk) | (s_idx == qblk - 1)
    allowed = s_idx <= qblk
    score = jnp.where(allowed, imp + jnp.where(forced, FORCE_BONUS, 0.0), NEG_INF)
    rank = jnp.zeros((tq, 128), F32)
    for j in range(32):
        col = score[:, j:j + 1]
        beats = (col > score) | ((col == score) & (s_idx > j))
        rank = rank + jnp.where(beats, 1.0, 0.0)
    sel = jnp.where((rank < SEL_TOPK) & allowed, 1.0, 0.0).astype(BF16)
    sel = jnp.concatenate([sel] * A_REP, axis=0)

    prev0 = pl.multiple_of(jnp.maximum(q0 - tq, 0), tq)
    near_pos = jnp.where(col_idx(NSA_NEAR) < tq, prev0, q0 - tq) + col_idx(NSA_NEAR)
    near_ok = (near_pos <= row_pos(NSA_NEAR)) & ((col_idx(NSA_NEAR) >= tq) | (qt > 0))
    btile = btile_ref[...]

    def near(k_ref, v_ref, extra_mask):
        k = jnp.concatenate([k_ref[pl.ds(prev0, tq), :], k_ref[pl.ds(q0, tq), :]], axis=0)
        v = jnp.concatenate([v_ref[pl.ds(prev0, tq), :], v_ref[pl.ds(q0, tq), :]], axis=0)
        mask = near_ok if extra_mask is None else (near_ok & extra_mask)
        return _softmax_start(jnp.where(mask, _dot_nt(qs, k) + btile, NEG_INF), v)

    expand_near = jnp.concatenate([_block_expand(prev0, tq), _block_expand(q0, tq)], axis=1)
    sel_near = jnp.dot(sel, expand_near, preferred_element_type=F32) > 0.5

    far_end = q0 - tq

    def sel_far(i, carry):
        k0 = pl.multiple_of(i * NSA_FAR_TILE, NSA_FAR_TILE)
        s = _dot_nt(qs, ks_ref[pl.ds(k0, NSA_FAR_TILE), :])
        hit = jnp.dot(sel, _block_expand(k0, NSA_FAR_TILE), preferred_element_type=F32) > 0.5
        mask = hit & (k0 + col_idx(NSA_FAR_TILE) < far_end)
        return _softmax_more(carry, jnp.where(mask, s, NEG_INF), vs_ref[pl.ds(k0, NSA_FAR_TILE), :])

    n_far = (jnp.maximum(far_end, 0) + NSA_FAR_TILE - 1) // NSA_FAR_TILE
    _, l_s, acc_s = lax.fori_loop(0, n_far, sel_far, near(ks_ref, vs_ref, sel_near))
    o_s = acc_s / l_s

    w_far = WIN_A - tq
    wf0 = pl.multiple_of(jnp.maximum(q0 - WIN_A, 0), tq)
    wpos = wf0 + col_idx(w_far)
    wmask = (row_pos(w_far) - wpos < WIN_A) & (wpos < far_end)
    s_w = jnp.where(wmask, _dot_nt(qs, kw_ref[pl.ds(wf0, w_far), :]), NEG_INF)
    _, l_w, acc_w = _softmax_more(near(kw_ref, vw_ref, None), s_w, vw_ref[pl.ds(wf0, w_far), :])
    o_w = acc_w / l_w

    gates = gate_ref[...]
    outs = []
    for r in range(A_REP):
        sl = slice(r * tq, (r + 1) * tq)
        outs.append(gates[:, 3 * r:3 * r + 1] * o_c[sl] + gates[:, 3 * r + 1:3 * r + 2] * o_s[sl]
                    + gates[:, 3 * r + 2:3 * r + 3] * o_w[sl])
    o_ref[...] = jnp.concatenate(outs, axis=1)


def _near_bucket_table(tq):
    i = np.arange(tq)[:, None]
    j = np.arange(2 * tq)[None, :]
    dist = np.where(j < tq, tq + i - j, i - (j - tq))
    n = np.maximum(dist, 0)
    exact = NUM_BUCKETS // 2
    nf = np.maximum(n, exact).astype(np.float64)
    large = exact + (np.log(nf / exact) / math.log(MAX_DISTANCE / exact) * (NUM_BUCKETS - exact)).astype(np.int64)
    return np.where(n < exact, n, np.minimum(large, NUM_BUCKETS - 1)).astype(np.int32)


def _cover_matrix(n_cmp_pad, n_sel):
    n = np.arange(n_cmp_pad)
    c_start = n * CMP_STRIDE
    c_end = c_start + CMP_LEN - 1
    s_start = np.arange(128) * SEL_BLOCK
    cover = (c_start[:, None] < s_start[None, :] + SEL_BLOCK) & (c_end[:, None] >= s_start[None, :])
    cover &= (np.arange(128)[None, :] < n_sel)
    return np.tile(cover.astype(np.float32), (A_REP, 1))


def _nsa_prompt_attend(q, gates, kc, vc, ks, vs, kw, vw, tab):
    b, t, _ = q.shape
    tq = NSA_Q_TILE
    assert t % NSA_FAR_TILE == 0 and t // SEL_BLOCK <= 32 and t >= WIN_A
    near = tab[_near_bucket_table(tq)] - tab[NUM_BUCKETS - 1]
    btile = jnp.transpose(near, (2, 3, 0, 1)).reshape(A_KV, A_REP * tq, 2 * tq)
    cover = jnp.asarray(_cover_matrix(128, t // SEL_BLOCK), BF16)
    kv_spec = lambda rows: pl.BlockSpec((None, None, rows, HEAD_DIM), lambda i, g, j: (i, g, 0, 0))
    return pl.pallas_call(
        functools.partial(_nsa_prompt_body, tq=tq),
        grid=(b, A_KV, t // tq),
        in_specs=[
            pl.BlockSpec((None, tq, A_REP * HEAD_DIM), lambda i, g, j: (i, j, g)),
            pl.BlockSpec((None, None, tq, 3 * A_REP), lambda i, g, j: (i, g, j, 0)),
            kv_spec(128), kv_spec(128), kv_spec(t), kv_spec(t), kv_spec(t), kv_spec(t),
            pl.BlockSpec((None, A_REP * tq, 2 * tq), lambda i, g, j: (g, 0, 0)),
            pl.BlockSpec((A_REP * 128, 128), lambda i, g, j: (0, 0)),
        ],
        out_specs=pl.BlockSpec((None, tq, A_REP * HEAD_DIM), lambda i, g, j: (i, j, g)),
        out_shape=jax.ShapeDtypeStruct((b, t, A_Q), F32),
        compiler_params=pltpu.CompilerParams(
            dimension_semantics=("parallel", "parallel", "arbitrary"),
            vmem_limit_bytes=V7X_VMEM_LIMIT_BYTES),
        name="nsa_prompt",
    )(q, gates, kc, vc, ks, vs, kw, vw, btile, cover)


S5_NS = B_GROUPS * B_STATE
S5_T_CHUNK = 64
S5_STRIP = 512


def _s5_body(u_ref, h0r_ref, h0i_ref, ar_ref, ai_ref, ldt_ref, wb_ref, wc_ref, d_ref, wglu_ref, bglu_ref,
             o_ref, hr_ref, hi_ref, coef_ref, st_ref, xbuf_ref, ubuf_ref, obuf_ref, *, nb, steps, interleave):
    c = pl.program_id(0)

    @pl.when(c == 0)
    def _():
        dt = jnp.exp(ldt_ref[...])
        ar, ai = ar_ref[...], ai_ref[...]
        mag = jnp.exp(ar * dt)
        abr, abi = mag * jnp.cos(ai * dt), mag * jnp.sin(ai * dt)
        den = ar * ar + ai * ai
        wr = ((abr - 1.0) * ar + abi * ai) / den
        wi = (abi * ar - (abr - 1.0) * ai) / den
        for k, val in enumerate((abr, abi, wr, wi)):
            coef_ref[k] = jnp.broadcast_to(val, (nb, S5_NS))
        st_ref[0] = h0r_ref[...]
        st_ref[1] = h0i_ref[...]

    if interleave:
        for b in range(nb):
            for j in range(B_WIDTH // 128):
                ubuf_ref.at[j][pl.ds(b, steps, stride=nb), :] = u_ref[b, :, j * 128:(j + 1) * 128]
        u = jnp.concatenate([ubuf_ref[j] for j in range(B_WIDTH // 128)], axis=1)
    else:
        u = u_ref[...]
    xbuf_ref[...] = jnp.dot(u.astype(BF16), wb_ref[...], preferred_element_type=F32)

    for lo in range(0, S5_NS, S5_STRIP):
        re = slice(lo, lo + S5_STRIP)
        im = slice(S5_NS + lo, S5_NS + lo + S5_STRIP)
        abr, abi, wr, wi = (coef_ref[k, :, re] for k in range(4))

        def step(t, carry):
            sr, si = carry
            r0 = pl.multiple_of(t * nb, nb)
            bur = xbuf_ref[pl.ds(r0, nb), re]
            bui = xbuf_ref[pl.ds(r0, nb), im]
            nsr = abr * sr - abi * si + (wr * bur - wi * bui)
            nsi = abr * si + abi * sr + (wr * bui + wi * bur)
            xbuf_ref[pl.ds(r0, nb), re] = nsr
            xbuf_ref[pl.ds(r0, nb), im] = nsi
            return nsr, nsi

        sr, si = lax.fori_loop(0, steps, step, (st_ref[0, :, re], st_ref[1, :, re]),
                               unroll=min(steps, 8))
        st_ref[0, :, re] = sr
        st_ref[1, :, re] = si

    y = jnp.dot(xbuf_ref[...].astype(BF16), wc_ref[...], preferred_element_type=F32) + d_ref[...] * u
    z = _gelu_tanh(y)
    gate = jnp.dot(z.astype(BF16), wglu_ref[...], preferred_element_type=F32) + bglu_ref[...]
    out = z * (1.0 / (1.0 + jnp.exp(-gate)))
    if interleave:
        for j in range(B_WIDTH // 128):
            obuf_ref[j] = out[:, j * 128:(j + 1) * 128]
        for b in range(nb):
            for j in range(B_WIDTH // 128):
                o_ref[b, :, j * 128:(j + 1) * 128] = obuf_ref.at[j][pl.ds(b, steps, stride=nb), :]
    else:
        o_ref[...] = out
    hr_ref[...] = st_ref[0]
    hi_ref[...] = st_ref[1]


def _s5_weights(a_re, a_im, log_dt, b_re, b_im, c_re, c_im, d, w_glu, b_glu):
    eye = jnp.eye(B_GROUPS, dtype=F32)
    blk_in = lambda w: jnp.einsum('hg,gpc->hcgp', eye, w).reshape(B_WIDTH, S5_NS)
    blk_out = lambda w: jnp.einsum('gh,gcp->gphc', eye, w).reshape(S5_NS, B_WIDTH)
    wb = jnp.concatenate([blk_in(b_re), blk_in(b_im)], axis=1).astype(BF16)
    wc = jnp.concatenate([blk_out(c_re), -blk_out(c_im)], axis=0).astype(BF16)
    flat = lambda x: x.reshape(1, S5_NS)
    return (flat(a_re), flat(a_im), flat(jnp.repeat(log_dt, B_STATE)), wb, wc, d.reshape(1, B_WIDTH),
            w_glu.astype(BF16), b_glu.reshape(1, B_WIDTH))


def _s5_mix(u, h_re, h_im, weights):
    nb, t, _ = u.shape
    interleave = t > 1
    steps = min(t, S5_T_CHUNK)
    rows = nb * steps
    body = functools.partial(_s5_body, nb=nb, steps=steps, interleave=interleave)
    if interleave:
        u_in = u
        u_spec = pl.BlockSpec((nb, steps, B_WIDTH), lambda c: (0, c, 0))
        o_shape = jax.ShapeDtypeStruct((nb, t, B_WIDTH), F32)
        scratch_rows = rows
    else:
        u_in = u.reshape(nb, B_WIDTH)
        u_spec = pl.BlockSpec((nb, B_WIDTH), lambda c: (0, 0))
        o_shape = jax.ShapeDtypeStruct((nb, B_WIDTH), F32)
        scratch_rows = 8
    o, hr, hi = pl.pallas_call(
        body,
        grid=(t // steps,),
        in_specs=[
            u_spec, _const_spec((nb, S5_NS)), _const_spec((nb, S5_NS)),
            _const_spec((1, S5_NS)), _const_spec((1, S5_NS)), _const_spec((1, S5_NS)),
            _const_spec((B_WIDTH, 2 * S5_NS)), _const_spec((2 * S5_NS, B_WIDTH)),
            _const_spec((1, B_WIDTH)), _const_spec((B_WIDTH, B_WIDTH)), _const_spec((1, B_WIDTH)),
        ],
        out_specs=[u_spec, pl.BlockSpec((nb, S5_NS), lambda c: (0, 0)), pl.BlockSpec((nb, S5_NS), lambda c: (0, 0))],
        out_shape=[o_shape, jax.ShapeDtypeStruct((nb, S5_NS), F32), jax.ShapeDtypeStruct((nb, S5_NS), F32)],
        scratch_shapes=[
            pltpu.VMEM((4, nb, S5_NS), F32),
            pltpu.VMEM((2, nb, S5_NS), F32),
            pltpu.VMEM((rows, 2 * S5_NS), F32),
            pltpu.VMEM((B_WIDTH // 128, scratch_rows, 128), F32),
            pltpu.VMEM((B_WIDTH // 128, scratch_rows, 128), F32),
        ],
        compiler_params=pltpu.CompilerParams(
            dimension_semantics=("arbitrary",),
            vmem_limit_bytes=V7X_VMEM_LIMIT_BYTES),
        name="s5_mix",
    )(u_in, h_re, h_im, *weights)
    return o.reshape(nb, t, B_WIDTH), hr, hi


CHUNK_W = CMP_STRIDE * 2 * A_KVW
CHUNKS_PER_PAGE = PAGE_SIZE // CMP_STRIDE
SEL_PAD = 256


def _bf16_round(x):
    return x.astype(BF16).astype(F32)


def _compress_tail(c, pos, w2):
    c1 = c[:, 128:]
    nxt = jnp.concatenate([c1[1:], c1[:1]], axis=0)
    hid = (pos + c[:, :128]) + nxt
    return jnp.dot(_gelu_tanh(hid).astype(BF16), w2, preferred_element_type=F32)


def _rms_heads128(x, gain):
    left = lax.broadcasted_iota(jnp.int32, x.shape, 1) < HEAD_DIM
    sq = x * x
    s0 = jnp.sum(jnp.where(left, sq, 0.0), axis=-1, keepdims=True)
    s1 = jnp.sum(jnp.where(left, 0.0, sq), axis=-1, keepdims=True)
    ms = jnp.where(left, s0, s1) * (1.0 / HEAD_DIM)
    return x * lax.rsqrt(ms + EPS) * gain


def _pad_rows8(x):
    return jnp.concatenate([x, jnp.zeros((8 - x.shape[0], x.shape[1]), x.dtype)], axis=0)


def _nsa_sample_cmp_body(pt_ref, *refs, n_pages, n_sel):
    page_refs = refs[:n_pages]
    (q_ref, posk_ref, posv_ref, wpos_k_ref, wpos_v_ref, wk_ref, wv_ref, w2k_ref, w2v_ref, kgain_ref, cover_ref,
     oc_ref, idx_ref, pos_scr, rows_scr) = refs[n_pages:]

    @pl.when(pl.program_id(0) == 0)
    def _():
        pos_scr[0] = jnp.dot(posk_ref[...], wpos_k_ref[...], preferred_element_type=F32)
        pos_scr[1] = jnp.dot(posv_ref[...], wpos_v_ref[...], preferred_element_type=F32)

    n_chunks = n_pages * CHUNKS_PER_PAGE
    n_cmp = n_chunks - 1
    for kv in range(2):
        for p in range(n_pages):
            rows_scr[kv, p * PAGE_SIZE:(p + 1) * PAGE_SIZE, :] = page_refs[p][kv].reshape(A_KVW, PAGE_SIZE).T

    def compress(kv, w_ref, w2_ref):
        acc = jnp.zeros((n_chunks, 2 * A_KVW), F32)
        for sp in range(CMP_STRIDE // 2):
            x = jnp.concatenate([rows_scr.at[kv][pl.ds(2 * sp + e, n_chunks, stride=CMP_STRIDE), :] for e in range(2)],
                                axis=1).astype(BF16)
            acc = acc + jnp.dot(x, w_ref[sp], preferred_element_type=F32)
        return _compress_tail(acc, pos_scr[kv, 0:1], w2_ref[...])

    kc = _rms_heads128(compress(0, wk_ref, w2k_ref), kgain_ref[...]).astype(BF16)
    vc = compress(1, wv_ref, w2v_ref).astype(BF16)

    col = lax.broadcasted_iota(jnp.int32, (8, n_chunks), 1)
    s_idx = lax.broadcasted_iota(jnp.int32, (1, SEL_PAD), 1)
    s_idx_f = s_idx.astype(F32)
    forced = (s_idx == 0) | (s_idx == n_sel - 1) | (s_idx == n_sel - 2)
    lane = lax.broadcasted_iota(jnp.int32, (1, 128), 1)
    for g in range(A_KV):
        hs = slice(g * HEAD_DIM, (g + 1) * HEAD_DIM)
        q8 = (_pad_rows8(q_ref[g]) * (HEAD_DIM ** -0.5)).astype(BF16)
        s = jnp.where(col < n_cmp, _dot_nt(q8, kc[:, hs]), NEG_INF)
        e = jnp.exp(s - jnp.max(s, axis=-1, keepdims=True))
        p = (e / jnp.sum(e, axis=-1, keepdims=True)).astype(BF16)
        oc_ref[g] = jnp.dot(p, vc[:, hs], preferred_element_type=F32)[0:A_REP]
        imp = jnp.sum(jnp.dot(p, cover_ref[...], preferred_element_type=F32)[0:A_REP], axis=0, keepdims=True)
        score = jnp.where(s_idx < n_sel, imp + jnp.where(forced, FORCE_BONUS, 0.0), NEG_INF)
        rank = jnp.zeros((1, SEL_PAD), F32)
        for j in range(n_sel):
            cj = score[:, j:j + 1]
            beats = (cj > score) | ((cj == score) & (s_idx > j))
            rank = rank + jnp.where(beats, 1.0, 0.0)
        picks = jnp.zeros((1, 128), F32)
        for r in range(SEL_TOPK):
            block = jnp.sum(jnp.where(rank == float(r), s_idx_f, 0.0), axis=-1, keepdims=True)
            picks = picks + jnp.where(lane == r, block, 0.0)
        idx_ref[g:g + 1, :] = picks.astype(jnp.int32)


def _nsa_sample_att_body(idx_ref, pt_ref, *refs, n_past_blk):
    page_refs = refs[:SEL_TOPK]
    (q_ref, gate_ref, oc_ref, newsel_ref, wcache_ref, newwin_ref, selb_ref, winb_ref, newb_ref, o_ref) = refs[SEL_TOPK:]
    b, g = pl.program_id(0), pl.program_id(1)
    blocks_per_page = PAGE_SIZE // SEL_BLOCK
    q8 = (_pad_rows8(q_ref[...]) * (HEAD_DIM ** -0.5)).astype(BF16)
    new_bias = newb_ref[:, 0:1]

    def attend(s, v_t, new_kv):
        k_new = _bf16_round(new_kv[0:1])
        v_new = _bf16_round(new_kv[1:2])
        s_new = jnp.sum(q8.astype(F32) * k_new, axis=-1, keepdims=True) + new_bias
        m = jnp.maximum(jnp.max(s, axis=-1, keepdims=True), s_new)
        e = jnp.exp(s - m)
        e_new = jnp.exp(s_new - m)
        acc = _dot_nt(e.astype(BF16), v_t) + _bf16_round(e_new) * v_new
        return acc / (jnp.sum(e, axis=-1, keepdims=True) + e_new)

    lane = lax.broadcasted_iota(jnp.int32, (8, PAGE_SIZE), 1)
    near = selb_ref[...]
    scores = []
    for k in range(SEL_TOPK):
        ik = idx_ref[(b * A_KV + g) * SEL_TOPK + k]
        blk = jnp.minimum(ik, n_past_blk - 1)
        s_k = jnp.dot(q8, page_refs[k][0].astype(BF16), preferred_element_type=F32)
        s_k = s_k + jnp.where(blk // blocks_per_page == (n_past_blk - 1) // blocks_per_page, near, 0.0)
        keep = (lane // SEL_BLOCK == blk % blocks_per_page) & (ik < n_past_blk)
        scores.append(jnp.where(keep, s_k, NEG_INF))
    v_sel = jnp.concatenate([page_refs[k][1].astype(BF16) for k in range(SEL_TOPK)], axis=1)
    o_s = attend(jnp.concatenate(scores, axis=1), v_sel, newsel_ref[...])

    w_len = wcache_ref.shape[-1]
    wcol = lax.broadcasted_iota(jnp.int32, (8, w_len), 1)
    s_w = jnp.dot(q8, wcache_ref[0].astype(BF16), preferred_element_type=F32) + winb_ref[...]
    o_w = attend(jnp.where(wcol >= 1, s_w, NEG_INF), wcache_ref[1].astype(BF16), newwin_ref[...])
    gates = gate_ref[...]
    o_ref[...] = gates[:, 0:1] * oc_ref[...] + gates[:, 1:2] * o_s[0:A_REP] + gates[:, 2:3] * o_w[0:A_REP]


def _bucket_np(dist):
    n = np.maximum(dist, 0)
    exact = NUM_BUCKETS // 2
    nf = np.maximum(n, exact).astype(np.float64)
    large = exact + (np.log(nf / exact) / math.log(MAX_DISTANCE / exact) * (NUM_BUCKETS - exact)).astype(np.int64)
    return np.where(n < exact, n, np.minimum(large, NUM_BUCKETS - 1)).astype(np.int32)


def _rows_last(cache):
    nd = cache.ndim
    return jnp.transpose(cache, tuple(range(nd - 4)) + (nd - 3, nd - 2, nd - 1, nd - 4))


def _nsa_sample(q, gates, new_sel, new_win, pool_cmp, pool_sel, win_cache, page_table,
                cmp_pos, cmp_w1, cmp_w2, k_gain, tab):
    n, n_pages = page_table.shape
    past = n_pages * PAGE_SIZE
    n_past_blk = past // SEL_BLOCK
    n_sel = n_past_blk + 1
    n_chunks = n_pages * CHUNKS_PER_PAGE
    w_len = win_cache.shape[1]
    blocks_per_page = PAGE_SIZE // SEL_BLOCK
    assert n_sel <= SEL_PAD and w_len == WIN_A and past >= WIN_A and blocks_per_page == 2
    pt_flat = page_table.reshape(-1)

    eye = jnp.eye(A_KV, dtype=F32)
    w1 = cmp_w1.reshape(2, 2, CMP_STRIDE, HEAD_DIM, HEAD_DIM)
    w_big = jnp.einsum('ajsdh,gk->asgdjkh', w1, eye).reshape(2, CMP_STRIDE // 2, 2 * A_KVW, 2 * A_KVW).astype(BF16)
    w_pos = jnp.concatenate([cmp_w1, cmp_w1], axis=-1).astype(BF16)
    pos = jnp.broadcast_to(cmp_pos.reshape(2, 1, CMP_LEN * HEAD_DIM), (2, 8, CMP_LEN * HEAD_DIM)).astype(BF16)
    w2_big = jnp.einsum('ahd,gk->aghkd', cmp_w2, eye).reshape(2, A_KVW, A_KVW).astype(BF16)
    gain2 = jnp.tile(k_gain, A_KV).reshape(1, A_KVW)
    c_idx = np.arange(n_chunks)
    s_start = np.arange(SEL_PAD) * SEL_BLOCK
    cover = ((c_idx[:, None] * CMP_STRIDE < s_start[None, :] + SEL_BLOCK)
             & (c_idx[:, None] * CMP_STRIDE + CMP_LEN - 1 >= s_start[None, :])
             & (c_idx[:, None] < n_chunks - 1) & (np.arange(SEL_PAD)[None, :] < n_sel))
    cover = jnp.asarray(cover.astype(np.float32), BF16)

    head_spec = lambda last: pl.BlockSpec((None, A_KV, A_REP, last), lambda i, *_: (i, 0, 0, 0))
    page_specs = [pl.BlockSpec((None, 2, A_KV, HEAD_DIM, PAGE_SIZE),
                               functools.partial(lambda i, pt, p: (pt[i * n_pages + p], 0, 0, 0, 0), p=p))
                  for p in range(n_pages)]
    full = lambda shape: pl.BlockSpec(shape, lambda i, *_: (0,) * len(shape))
    pool_c = _rows_last(pool_cmp)
    mlp_w = (CMP_STRIDE // 2, 2 * A_KVW, 2 * A_KVW)
    o_c, idx = pl.pallas_call(
        functools.partial(_nsa_sample_cmp_body, n_pages=n_pages, n_sel=n_sel),
        grid_spec=pltpu.PrefetchScalarGridSpec(
            num_scalar_prefetch=1,
            grid=(n,),
            in_specs=page_specs + [
                head_spec(HEAD_DIM),
                full((8, CMP_LEN * HEAD_DIM)), full((8, CMP_LEN * HEAD_DIM)),
                full((CMP_LEN * HEAD_DIM, A_KVW)), full((CMP_LEN * HEAD_DIM, A_KVW)),
                full(mlp_w), full(mlp_w),
                full((A_KVW, A_KVW)), full((A_KVW, A_KVW)), full((1, A_KVW)), full((n_chunks, SEL_PAD)),
            ],
            out_specs=[head_spec(HEAD_DIM), pl.BlockSpec((None, A_KV, 128), lambda i, *_: (i, 0, 0))],
            scratch_shapes=[pltpu.VMEM((2, 8, A_KVW), F32), pltpu.VMEM((2, past, A_KVW), F32)],
        ),
        out_shape=[jax.ShapeDtypeStruct((n, A_KV, A_REP, HEAD_DIM), F32),
                   jax.ShapeDtypeStruct((n, A_KV, 128), jnp.int32)],
        compiler_params=pltpu.CompilerParams(
            dimension_semantics=("arbitrary",), vmem_limit_bytes=V7X_VMEM_LIMIT_BYTES),
        name="nsa_sample_cmp",
    )(pt_flat, *([pool_c] * n_pages), q, pos[0], pos[1], w_pos[0], w_pos[1], w_big[0], w_big[1],
      w2_big[0], w2_big[1], gain2, cover)

    tab_h = jnp.transpose(tab, (1, 2, 0))
    rel = lambda dist: jnp.pad(tab_h[:, :, _bucket_np(dist)] - tab_h[:, :, NUM_BUCKETS - 1:],
                               ((0, 0), (0, 8 - A_REP), (0, 0)))
    sel_bias = rel(past - (past - PAGE_SIZE + np.arange(PAGE_SIZE)))
    win_bias = rel(w_len - np.arange(w_len))
    new_bias = rel(np.zeros(128, np.int64))

    idx_flat = idx[:, :, :SEL_TOPK].reshape(-1)

    def page_map(i, g, idx_s, pt, slot):
        blk = jnp.minimum(idx_s[(i * A_KV + g) * SEL_TOPK + slot], n_past_blk - 1)
        return (pt[i * n_pages + blk // blocks_per_page], 0, g, 0, 0)

    slot_specs = [pl.BlockSpec((None, 2, None, HEAD_DIM, PAGE_SIZE), functools.partial(page_map, slot=s))
                  for s in range(SEL_TOPK)]
    per_head = lambda rows, last: pl.BlockSpec((None, None, rows, last), lambda i, g, *_: (i, g, 0, 0))
    per_group = lambda last: pl.BlockSpec((None, 8, last), lambda i, g, *_: (g, 0, 0))
    return pl.pallas_call(
        functools.partial(_nsa_sample_att_body, n_past_blk=n_past_blk),
        grid_spec=pltpu.PrefetchScalarGridSpec(
            num_scalar_prefetch=2,
            grid=(n, A_KV),
            in_specs=slot_specs + [
                per_head(A_REP, HEAD_DIM), per_head(A_REP, 3), per_head(A_REP, HEAD_DIM), per_head(2, HEAD_DIM),
                pl.BlockSpec((None, 2, None, HEAD_DIM, w_len), lambda i, g, *_: (i, 0, g, 0, 0)),
                per_head(2, HEAD_DIM), per_group(PAGE_SIZE), per_group(w_len), per_group(128),
            ],
            out_specs=per_head(A_REP, HEAD_DIM),
        ),
        out_shape=jax.ShapeDtypeStruct((n, A_KV, A_REP, HEAD_DIM), F32),
        compiler_params=pltpu.CompilerParams(
            dimension_semantics=("arbitrary", "arbitrary"), vmem_limit_bytes=V7X_VMEM_LIMIT_BYTES),
        name="nsa_sample_att",
    )(idx_flat, pt_flat, *([_rows_last(pool_sel)] * SEL_TOPK), q, gates, o_c,
      jnp.transpose(new_sel, (0, 2, 1, 3)), _rows_last(win_cache), jnp.transpose(new_win, (0, 2, 1, 3)),
      sel_bias, win_bias, new_bias)


def rms_norm(x, g):
    xf = x.astype(jnp.float32)
    y = xf * lax.rsqrt(jnp.mean(xf * xf, axis=-1, keepdims=True) + EPS)
    return (y * g.astype(jnp.float32)).astype(x.dtype)


def rel_bucket(dist):
    n = jnp.maximum(dist, 0)
    exact = NUM_BUCKETS // 2
    nf = jnp.maximum(n, exact).astype(jnp.float32)
    large = exact + (jnp.log(nf / exact) / math.log(MAX_DISTANCE / exact) * (NUM_BUCKETS - exact)).astype(jnp.int32)
    return jnp.where(n < exact, n, jnp.minimum(large, NUM_BUCKETS - 1))


def masked_softmax(s, mask):
    s = jnp.where(mask, s.astype(jnp.float32), -jnp.inf)
    m = jnp.max(s, axis=-1, keepdims=True)
    m = jnp.where(jnp.isfinite(m), m, 0.0)
    e = jnp.exp(s - m)
    d = jnp.sum(e, axis=-1, keepdims=True)
    return e / jnp.where(d > 0, d, 1.0)


def compress(rows, pos, w1, w2):
    b, t, g, dh = rows.shape
    r = CMP_LEN // CMP_STRIDE
    nch = t // CMP_STRIDE
    chunks = rows[:, :nch * CMP_STRIDE].reshape(b, nch, CMP_STRIDE, g, dh)
    w1b = w1.reshape(r, CMP_STRIDE, dh, w1.shape[-1])
    nc = nch - r + 1
    hid = jnp.einsum('ld,ldh->h', pos, w1.reshape(CMP_LEN, dh, w1.shape[-1]))
    for j in range(r):
        hid = hid + jnp.einsum('bnsgd,sdh->bngh', chunks[:, j:j + nc], w1b[j])
    return jnp.einsum('bngh,hd->bngd', jax.nn.gelu(hid), w2)


def compressed_kv(k_rows, v_rows, cmp_pos, cmp_w1, cmp_w2, k_gain):
    kc = rms_norm(compress(k_rows, cmp_pos[0], cmp_w1[0], cmp_w2[0]), k_gain)
    vc = compress(v_rows, cmp_pos[1], cmp_w1[1], cmp_w2[1])
    c_end = jnp.arange(kc.shape[1]) * CMP_STRIDE + (CMP_LEN - 1)
    return kc, vc, c_end


def nsa_attend(q, q_pos, kc, vc, c_end, n_sel, gather_sel, kw, vw, w_pos, gates, tab):
    scale = HEAD_DIM ** -0.5
    g_n = q.shape[2]
    s_c = jnp.einsum('bqgrd,bngd->bgrqn', q, kc) * scale
    p_c = masked_softmax(s_c, c_end[None, :] <= q_pos[:, None])
    o_c = jnp.einsum('bgrqn,bngd->bqgrd', p_c.astype(vc.dtype), vc)
    s_start = jnp.arange(n_sel) * SEL_BLOCK
    c_start = c_end - (CMP_LEN - 1)
    cover = ((c_start[:, None] < s_start[None, :] + SEL_BLOCK) & (c_end[:, None] >= s_start[None, :])).astype(jnp.float32)
    imp = jnp.einsum('bgrqn,ns->bgqs', p_c, cover)
    qblk = q_pos[:, None] // SEL_BLOCK
    sblk = jnp.arange(n_sel)[None, :]
    forced = ((sblk == 0) | (sblk == qblk) | (sblk == qblk - 1)).astype(jnp.float32)
    score = jnp.where(sblk <= qblk, imp + FORCE_BONUS * forced, -jnp.inf)
    _, idx = lax.top_k(score, min(SEL_TOPK, n_sel))
    k_sel, v_sel = gather_sel(idx)
    kpos = idx[..., None] * SEL_BLOCK + jnp.arange(SEL_BLOCK)
    dist = q_pos[:, None, None] - kpos
    tab_g = jnp.transpose(tab, (1, 0, 2))
    gi = jnp.arange(g_n)[None, :, None, None, None]
    bias = jnp.moveaxis(tab_g[gi, rel_bucket(dist)], -1, 3)
    s_s = jnp.einsum('bqgrd,bgqkjd->bgqrkj', q, k_sel) * scale + bias
    shp = s_s.shape
    mask = jnp.broadcast_to((dist >= 0)[:, :, :, None], shp)
    p_s = masked_softmax(s_s.reshape(shp[:4] + (-1,)), mask.reshape(shp[:4] + (-1,))).reshape(shp)
    o_s = jnp.einsum('bgqrkj,bgqkjd->bqgrd', p_s.astype(v_sel.dtype), v_sel)
    wd = q_pos[:, None] - w_pos[None, :]
    wmask = (wd >= 0) & (wd < WIN_A) & (w_pos[None, :] >= 0)
    wbias = jnp.transpose(tab[rel_bucket(wd)], (2, 3, 0, 1))
    s_w = jnp.einsum('bqgrd,bwgd->bgrqw', q, kw) * scale + wbias
    p_w = masked_softmax(s_w, wmask)
    o_w = jnp.einsum('bgrqw,bwgd->bqgrd', p_w.astype(vw.dtype), vw)
    return gates[..., 0:1] * o_c + gates[..., 1:2] * o_s + gates[..., 2:3] * o_w


def nsa_prompt(q, gates, kc_rows, vc_rows, ks, vs, kw, vw, cmp_pos, cmp_w1, cmp_w2, k_gain, tab):
    b, t = q.shape[:2]
    kc, vc, c_end = compressed_kv(kc_rows, vc_rows, cmp_pos, cmp_w1, cmp_w2, k_gain)
    n_sel = t // SEL_BLOCK
    ks_b = ks.reshape(b, n_sel, SEL_BLOCK, A_KV, HEAD_DIM).transpose(0, 3, 1, 2, 4)
    vs_b = vs.reshape(b, n_sel, SEL_BLOCK, A_KV, HEAD_DIM).transpose(0, 3, 1, 2, 4)
    bi = jnp.arange(b)[:, None, None, None]
    gi = jnp.arange(A_KV)[None, :, None, None]

    def gather_sel(idx):
        return ks_b[bi, gi, idx], vs_b[bi, gi, idx]

    pad = ((0, 0), (WIN_A, 0), (0, 0), (0, 0))
    kw_p, vw_p = jnp.pad(kw, pad), jnp.pad(vw, pad)

    def chunk(start):
        q_pos = start + jnp.arange(NSA_QBLK)
        qc = lax.dynamic_slice_in_dim(q, start, NSA_QBLK, axis=1)
        gc = lax.dynamic_slice_in_dim(gates, start, NSA_QBLK, axis=1)
        kwc = lax.dynamic_slice_in_dim(kw_p, start, NSA_QBLK + WIN_A, axis=1)
        vwc = lax.dynamic_slice_in_dim(vw_p, start, NSA_QBLK + WIN_A, axis=1)
        w_pos = start - WIN_A + jnp.arange(NSA_QBLK + WIN_A)
        return nsa_attend(qc, q_pos, kc, vc, c_end, n_sel, gather_sel, kwc, vwc, w_pos, gc, tab)

    out = lax.map(chunk, jnp.arange(t // NSA_QBLK) * NSA_QBLK)
    return jnp.moveaxis(out, 0, 1).reshape(b, t, A_KV, A_REP, HEAD_DIM)


def _nsa_prompt_glue(q, gates, kc_rows, vc_rows, ks, vs, kw, vw, cmp_pos, cmp_w1, cmp_w2, k_gain, tab):
    b, t = q.shape[:2]
    kc, vc, _ = compressed_kv(kc_rows, vc_rows, cmp_pos, cmp_w1, cmp_w2, k_gain)
    pad = lambda x: jnp.pad(x, ((0, 0), (0, 128 - x.shape[1]), (0, 0), (0, 0)))
    heads_first = lambda x: jnp.transpose(x, (0, 2, 1, 3)).astype(BF16)
    o = _nsa_prompt_attend(q.reshape(b, t, A_Q),
                           jnp.transpose(gates.reshape(b, t, A_KV, 3 * A_REP), (0, 2, 1, 3)),
                           heads_first(pad(kc)), heads_first(pad(vc)), heads_first(ks), heads_first(vs),
                           heads_first(kw), heads_first(vw), tab)
    return o.reshape(b, t, A_KV, A_REP, HEAD_DIM)


def nsa_sample(q, gates, new_c, new_s, win_all, past, pool_cmp, pool_sel, page_table,
               cmp_pos, cmp_w1, cmp_w2, k_gain, tab):
    db, s = q.shape[:2]
    past_c = pool_cmp[page_table].reshape(db, past, 2, A_KV, HEAD_DIM)
    rows_c = jnp.concatenate([past_c, new_c], axis=1)
    kc, vc, c_end = compressed_kv(rows_c[:, :, 0], rows_c[:, :, 1], cmp_pos, cmp_w1, cmp_w2, k_gain)
    q_pos = past + jnp.arange(s)
    n_past_blk = past // SEL_BLOCK
    n_new_blk = -(-s // SEL_BLOCK)
    n_sel = n_past_blk + n_new_blk
    bpp = PAGE_SIZE // SEL_BLOCK
    pool_b = pool_sel.reshape(pool_sel.shape[0], bpp, SEL_BLOCK, 2, A_KV, HEAD_DIM)
    new_b = jnp.pad(new_s, ((0, 0), (0, n_new_blk * SEL_BLOCK - s), (0, 0), (0, 0), (0, 0)))
    new_b = new_b.reshape(db, n_new_blk, SEL_BLOCK, 2, A_KV, HEAD_DIM)
    bi = jnp.arange(db)[:, None, None, None]
    gi = jnp.arange(A_KV)[None, :, None, None]

    def gather_sel(idx):
        pidx = jnp.clip(idx, 0, n_past_blk - 1)
        phys = page_table[bi, pidx // bpp]
        pool_f = pool_b.reshape((-1,) + pool_b.shape[2:])
        both = pool_f[phys * bpp + pidx % bpp]
        from_pool = jnp.stack([both[:, g, ..., g, :] for g in range(A_KV)], axis=1)
        from_new = new_b[bi, jnp.clip(idx - n_past_blk, 0, n_new_blk - 1), :, :, gi]
        kv = jnp.where((idx < n_past_blk)[..., None, None, None], from_pool, from_new)
        return kv[..., 0, :], kv[..., 1, :]

    w_len = win_all.shape[1]
    w_pos = past + s - w_len + jnp.arange(w_len)
    return nsa_attend(q, q_pos, kc, vc, c_end, n_sel, gather_sel, win_all[:, :, 0], win_all[:, :, 1],
                      w_pos, gates, tab)


def _ssm_combine(e1, e2):
    a1r, a1i, b1r, b1i = e1
    a2r, a2i, b2r, b2i = e2
    return (a2r * a1r - a2i * a1i, a2r * a1i + a2i * a1r,
            a2r * b1r - a2i * b1i + b2r, a2r * b1i + a2i * b1r + b2i)


def s5_mix(u, h_re, h_im, a_re, a_im, log_dt, b_re, b_im, c_re, c_im, d, w_glu, b_glu):
    bsz, t, _ = u.shape
    f32 = jnp.float32
    uf = u.astype(f32)
    ug = uf.reshape(bsz, t, B_GROUPS, B_GROUP)
    dt = jnp.exp(log_dt.astype(f32))[:, None]
    ar, ai = a_re.astype(f32), a_im.astype(f32)
    mag = jnp.exp(ar * dt)
    abr, abi = mag * jnp.cos(ai * dt), mag * jnp.sin(ai * dt)
    den = ar * ar + ai * ai
    wr = ((abr - 1.0) * ar + abi * ai) / den
    wi = (abi * ar - (abr - 1.0) * ai) / den
    bu_r = jnp.einsum('gpc,btgc->btgp', b_re.astype(f32), ug)
    bu_i = jnp.einsum('gpc,btgc->btgp', b_im.astype(f32), ug)
    x_r = wr * bu_r - wi * bu_i
    x_i = wr * bu_i + wi * bu_r
    hr, hi = h_re.astype(f32), h_im.astype(f32)
    x_r = x_r.at[:, 0].add(abr * hr - abi * hi)
    x_i = x_i.at[:, 0].add(abr * hi + abi * hr)
    a_r = jnp.broadcast_to(abr, x_r.shape)
    a_i = jnp.broadcast_to(abi, x_i.shape)
    _, _, s_r, s_i = lax.associative_scan(_ssm_combine, (a_r, a_i, x_r, x_i), axis=1)
    y = jnp.einsum('gcp,btgp->btgc', c_re.astype(f32), s_r) - jnp.einsum('gcp,btgp->btgc', c_im.astype(f32), s_i)
    y = y.reshape(bsz, t, B_WIDTH) + d.astype(f32) * uf
    z = jax.nn.gelu(y)
    out = z * jax.nn.sigmoid(z @ w_glu.astype(f32) + b_glu.astype(f32))
    return out.astype(u.dtype), s_r[:, -1], s_i[:, -1]


def nsa_s5_inputs(xn, w_in, q_gain, k_gain):
    b, t, _ = xn.shape
    sizes = [A_Q] + [A_KVW] * 6 + [A_GATE, B_WIDTH]
    cuts = [int(c) for c in np.cumsum(sizes)[:-1]]
    q, kc, vc, ks, vs, kw, vw, gl, u = jnp.split(xn @ w_in, cuts, axis=-1)
    heads = lambda z: z.reshape(b, t, A_KV, HEAD_DIM)
    q = rms_norm(q.reshape(b, t, A_KV, A_REP, HEAD_DIM), q_gain)
    gates = jax.nn.sigmoid(gl).reshape(b, t, A_KV, A_REP, 3)
    return (q, rms_norm(heads(kc), k_gain), heads(vc), rms_norm(heads(ks), k_gain), heads(vs),
            rms_norm(heads(kw), k_gain), heads(vw), gates, u)


def swa_inputs(xn, w_in, q_gain, k_gain):
    b, t, _ = xn.shape
    q, k, v = jnp.split(xn @ w_in, [C_HEADS * HEAD_DIM, (C_HEADS + C_KV) * HEAD_DIM], axis=-1)
    q = rms_norm(q.reshape(b, t, C_KV, C_REP, HEAD_DIM), q_gain)
    k = rms_norm(k.reshape(b, t, C_KV, HEAD_DIM), k_gain)
    return q, k, v.reshape(b, t, C_KV, HEAD_DIM)


def swa_attend(q, q_pos, k, v, k_pos, sinks, tab):
    dist = q_pos[:, :, None] - k_pos[:, None, :]
    mask = (dist >= 0) & (dist < WIN_C) & (k_pos[:, None, :] >= 0)
    bias = jnp.transpose(tab[rel_bucket(dist)], (3, 4, 0, 1, 2))
    s = jnp.einsum('bnqgrd,bnkgd->bgrnqk', q, k).astype(jnp.float32) * (HEAD_DIM ** -0.5) + bias
    s = jnp.where(mask, s, -jnp.inf)
    sink = sinks.astype(jnp.float32).reshape(C_KV, C_REP)[:, :, None, None, None]
    m = jnp.maximum(jnp.max(s, axis=-1, keepdims=True), sink)
    e = jnp.exp(s - m)
    p = e / (jnp.sum(e, axis=-1, keepdims=True) + jnp.exp(sink - m))
    return jnp.einsum('bgrnqk,bnkgd->bnqgrd', p.astype(v.dtype), v)


def swa_prompt_attend(q, k, v, sinks, tab):
    b, t = q.shape[:2]
    nb = t // WIN_C
    qb = q.reshape(b, nb, WIN_C, C_KV, C_REP, HEAD_DIM)

    def band(x):
        xb = x.reshape(b, nb, WIN_C, C_KV, HEAD_DIM)
        prev = jnp.pad(xb[:, :-1], ((0, 0), (1, 0), (0, 0), (0, 0), (0, 0)))
        return jnp.concatenate([prev, xb], axis=2)

    q_pos = jnp.arange(t).reshape(nb, WIN_C)
    k_pos = (jnp.arange(nb)[:, None] - 1) * WIN_C + jnp.arange(2 * WIN_C)[None, :]
    o = swa_attend(qb, q_pos, band(k), band(v), k_pos, sinks, tab)
    return o.reshape(b, t, C_HEADS * HEAD_DIM)


def kernel(x_prompt, x_sample, cache_nsa_cmp, cache_nsa_sel, cache_nsa_win, state_s5_re, state_s5_im,
           cache_swa, state_ffn_conv, page_table, rel_bias, norm_mix, norm_ffn, a_w_in, a_w_out,
           nsa_q_gain, nsa_k_gain, nsa_cmp_pos, nsa_cmp_w1, nsa_cmp_w2, s5_a_re, s5_a_im, s5_log_dt,
           s5_b_re, s5_b_im, s5_c_re, s5_c_im, s5_d, s5_w_glu, s5_b_glu, c_w_in, c_w_out, c_q_gain,
           c_k_gain, c_sinks, ffn_w_up, ffn_w_gate, ffn_conv_w, ffn_conv_b, ffn_w_down):
    bp, tp, _ = x_prompt.shape
    bs, ts, _ = x_sample.shape
    past = page_table.shape[1] * PAGE_SIZE
    tab_a = rel_bias[:, :A_HEADS].reshape(NUM_BUCKETS, A_KV, A_REP)
    tab_c = rel_bias[:, :C_HEADS].reshape(NUM_BUCKETS, C_KV, C_REP)
    hp, hs = x_prompt, x_sample
    cmp_p, cmp_s, sel_p, sel_s, win_p, win_s = [], [], [], [], [], []
    s5r_p, s5i_p, s5r_s, s5i_s = [], [], [], []
    swa_p, swa_s, conv_p, conv_s = [], [], [], []
    for layer in range(DEPTH):
        i = layer // 2
        xp = rms_norm(hp, norm_mix[layer])
        xs = rms_norm(hs, norm_mix[layer])
        if layer % 2 == 0:
            qp, kcp, vcp, ksp, vsp, kwp, vwp, gp, up = nsa_s5_inputs(xp, a_w_in[i], nsa_q_gain[i], nsa_k_gain[i])
            qs, kcs, vcs, kss, vss, kws, vws, gs, us = nsa_s5_inputs(xs, a_w_in[i], nsa_q_gain[i], nsa_k_gain[i])
            cmp_args = (nsa_cmp_pos[i], nsa_cmp_w1[i], nsa_cmp_w2[i], nsa_k_gain[i], tab_a)
            o_ap = _nsa_prompt_glue(qp, gp, kcp, vcp, ksp, vsp, kwp, vwp, *cmp_args)
            new_c = jnp.stack([kcs, vcs], axis=2)
            new_sel = jnp.stack([kss, vss], axis=2)
            win_all = jnp.concatenate([cache_nsa_win[i].astype(kws.dtype), jnp.stack([kws, vws], axis=2)], axis=1)
            assert ts == 1
            o_as = _nsa_sample(qs.reshape(bs, A_KV, A_REP, HEAD_DIM), gs.reshape(bs, A_KV, A_REP, 3),
                               new_sel.reshape(bs, 2, A_KV, HEAD_DIM),
                               jnp.stack([kws, vws], axis=2).reshape(bs, 2, A_KV, HEAD_DIM),
                               cache_nsa_cmp[i], cache_nsa_sel[i], cache_nsa_win[i], page_table, *cmp_args)
            s5_args = (s5_a_re[i], s5_a_im[i], s5_log_dt[i], s5_b_re[i], s5_b_im[i], s5_c_re[i], s5_c_im[i],
                       s5_d[i], s5_w_glu[i], s5_b_glu[i])
            s5w = _s5_weights(*s5_args)
            h0 = jnp.zeros((bp, S5_NS), F32)
            o_bp, hr_p, hi_p = _s5_mix(up, h0, h0, s5w)
            o_bs, hr_s, hi_s = _s5_mix(us, state_s5_re[i].reshape(bs, S5_NS), state_s5_im[i].reshape(bs, S5_NS), s5w)
            hr_p, hi_p = (x.reshape(bp, B_GROUPS, B_STATE) for x in (hr_p, hi_p))
            hr_s, hi_s = (x.reshape(bs, B_GROUPS, B_STATE) for x in (hr_s, hi_s))
            hp = hp + jnp.concatenate([o_ap.reshape(bp, tp, A_Q), o_bp], axis=-1) @ a_w_out[i]
            hs = hs + jnp.concatenate([o_as.reshape(bs, ts, A_Q), o_bs], axis=-1) @ a_w_out[i]
            cmp_p.append(jnp.stack([kcp, vcp], axis=2))
            cmp_s.append(new_c)
            sel_p.append(jnp.stack([ksp, vsp], axis=2))
            sel_s.append(new_sel)
            win_p.append(jnp.stack([kwp, vwp], axis=2)[:, -min(WIN_A, tp):])
            win_s.append(win_all[:, -min(WIN_A, win_all.shape[1]):])
            s5r_p.append(hr_p)
            s5i_p.append(hi_p)
            s5r_s.append(hr_s)
            s5i_s.append(hi_s)
        else:
            qp, kp, vp = swa_inputs(xp, c_w_in[i], c_q_gain[i], c_k_gain[i])
            qs, k_s, v_s = swa_inputs(xs, c_w_in[i], c_q_gain[i], c_k_gain[i])
            o_cp = swa_prompt_attend(qp, kp, vp, c_sinks[i], tab_c)
            kv_all = jnp.concatenate([cache_swa[i].astype(k_s.dtype), jnp.stack([k_s, v_s], axis=2)], axis=1)
            kv_len = kv_all.shape[1]
            q_pos = (past + jnp.arange(ts))[None, :]
            k_pos = (past + ts - kv_len + jnp.arange(kv_len))[None, :]
            o_cs = swa_attend(qs[:, None], q_pos, kv_all[:, None, :, 0], kv_all[:, None, :, 1], k_pos,
                              c_sinks[i], tab_c).reshape(bs, ts, C_HEADS * HEAD_DIM)
            hp = hp + o_cp @ c_w_out[i]
            hs = hs + o_cs @ c_w_out[i]
            swa_p.append(jnp.stack([kp, vp], axis=2)[:, -min(WIN_C, tp):])
            swa_s.append(kv_all[:, -min(WIN_C, kv_len):])
        wup = ffn_w_up[layer].astype(BF16)
        wgate = ffn_w_gate[layer].astype(BF16)
        wdown = ffn_w_down[layer].astype(BF16)
        gain = norm_ffn[layer].reshape(1, D_MODEL)
        cw = ffn_conv_w[layer]
        cb = ffn_conv_b[layer].reshape(1, D_FF)
        hp, cp = _ffn_prompt(hp, gain, jnp.zeros((bp, CONV_W - 1, D_FF), F32), wup, wgate, cw, cb, wdown)
        hs2, cs = _ffn_sample(hs.reshape(bs * ts, D_MODEL), gain,
                              state_ffn_conv[layer].reshape(bs, (CONV_W - 1) * D_FF), wup, wgate, cw, cb, wdown)
        hs = hs2.reshape(bs, ts, D_MODEL)
        conv_p.append(cp)
        conv_s.append(cs.reshape(bs, CONV_W - 1, D_FF))
    y_prompt, y_sample = hp, hs
    return (y_prompt, y_sample, jnp.stack(cmp_p), jnp.stack(cmp_s), jnp.stack(sel_p), jnp.stack(sel_s),
            jnp.stack(win_p), jnp.stack(win_s), jnp.stack(s5r_p), jnp.stack(s5i_p), jnp.stack(s5r_s),
            jnp.stack(s5i_s), jnp.stack(swa_p), jnp.stack(swa_s), jnp.stack(conv_p), jnp.stack(conv_s))
```

```python
import functools
import math

import jax
import jax.numpy as jnp
import numpy as np
from jax import lax
from jax.experimental import pallas as pl
from jax.experimental.pallas import tpu as pltpu

D_MODEL = 1024
DEPTH = 2
PAGE_SIZE = 128
HEAD_DIM = 64
A_HEADS = 8
A_KV = 2
A_REP = A_HEADS // A_KV
A_Q = A_HEADS * HEAD_DIM
A_KVW = A_KV * HEAD_DIM
A_GATE = 3 * A_HEADS
CMP_LEN = 32
CMP_STRIDE = 16
SEL_BLOCK = 64
SEL_TOPK = 16
WIN_A = 512
NSA_QBLK = 64
FORCE_BONUS = 1000.0
B_WIDTH = D_MODEL // 2
B_GROUP = 16
B_GROUPS = B_WIDTH // B_GROUP
B_STATE = 64
C_HEADS = D_MODEL // HEAD_DIM
C_KV = 2
C_REP = C_HEADS // C_KV
WIN_C = 128
NUM_BUCKETS = 32
MAX_DISTANCE = 128
D_FF = 2816
CONV_W = 3
EPS = 1e-6

F32 = jnp.float32
BF16 = jnp.bfloat16

V7X_VMEM_LIMIT_BYTES = 56 * 1024 * 1024


def _gelu_tanh(x):
    return 0.5 * x * (1.0 + jnp.tanh(math.sqrt(2.0 / math.pi) * (x + 0.044715 * (x * x * x))))


def _rms_rows(x, gain):
    return x * lax.rsqrt(jnp.mean(x * x, axis=-1, keepdims=True) + EPS) * gain


def _const_spec(shape):
    zeros = (0,) * len(shape)
    return pl.BlockSpec(shape, lambda *_: zeros, pipeline_mode=pl.Buffered(1))


NSA_Q_TILE = 128
NSA_FAR_TILE = 512
NSA_NEAR = 2 * NSA_Q_TILE
NEG_INF = float("-inf")


def _dot_nt(a, b):
    return lax.dot_general(a, b, (((1,), (1,)), ((), ())), preferred_element_type=F32)


def _softmax_start(s, v):
    m = jnp.max(s, axis=-1, keepdims=True)
    e = jnp.exp(s - m)
    return m, jnp.sum(e, axis=-1, keepdims=True), jnp.dot(e.astype(BF16), v, preferred_element_type=F32)


def _softmax_more(carry, s, v):
    m, l, acc = carry
    m_new = jnp.maximum(m, jnp.max(s, axis=-1, keepdims=True))
    alpha = jnp.exp(m - m_new)
    e = jnp.exp(s - m_new)
    return (m_new, alpha * l + jnp.sum(e, axis=-1, keepdims=True),
            alpha * acc + jnp.dot(e.astype(BF16), v, preferred_element_type=F32))


def _block_expand(k0, width):
    kpos = k0 + lax.broadcasted_iota(jnp.int32, (128, width), 1)
    blk = lax.broadcasted_iota(jnp.int32, (128, width), 0)
    return jnp.where((kpos >> 6) == blk, 1.0, 0.0).astype(BF16)


def _nsa_prompt_body(q_ref, gate_ref, kc_ref, vc_ref, ks_ref, vs_ref, kw_ref, vw_ref, btile_ref, cover_ref,
                     o_ref, *, tq):
    qt = pl.program_id(2)
    q0 = pl.multiple_of(qt * tq, tq)
    rows = A_REP * tq
    q = q_ref[...]
    qs = jnp.concatenate([q[:, r * HEAD_DIM:(r + 1) * HEAD_DIM] for r in range(A_REP)], axis=0)
    qs = (qs * (HEAD_DIM ** -0.5)).astype(BF16)

    def row_pos(width):
        return q0 + (lax.broadcasted_iota(jnp.int32, (rows, width), 0) & (tq - 1))

    def col_idx(width):
        return lax.broadcasted_iota(jnp.int32, (rows, width), 1)

    n_idx = col_idx(128)
    valid_c = (n_idx * CMP_STRIDE + (CMP_LEN - 1) <= row_pos(128)) & (n_idx < 127)
    s_c = jnp.where(valid_c, _dot_nt(qs, kc_ref[...]), NEG_INF)
    m_c = jnp.max(s_c, axis=-1, keepdims=True)
    m_c = jnp.where(m_c == NEG_INF, 0.0, m_c)
    e_c = jnp.exp(s_c - m_c)
    d_c = jnp.sum(e_c, axis=-1, keepdims=True)
    p_c = (e_c / jnp.where(d_c > 0, d_c, 1.0)).astype(BF16)
    o_c = jnp.dot(p_c, vc_ref[...], preferred_element_type=F32)
    p_heads = jnp.concatenate([p_c[r * tq:(r + 1) * tq] for r in range(A_REP)], axis=1)
    imp = jnp.dot(p_heads, cover_ref[...], preferred_element_type=F32)

    s_idx = lax.broadcasted_iota(jnp.int32, (tq, 128), 1)
    qblk = (q0 + lax.broadcasted_iota(jnp.int32, (tq, 128), 0)) >> 6
    forced = (s_idx == 0) | (s_idx == qblk) | (s_idx == qblk - 1)
    allowed = s_idx <= qblk
    score = jnp.where(allowed, imp + jnp.where(forced, FORCE_BONUS, 0.0), NEG_INF)
    rank = jnp.zeros((tq, 128), F32)
    for j in range(32):
        col = score[:, j:j + 1]
        beats = (col > score) | ((col == score) & (s_idx > j))
        rank = rank + jnp.where(beats, 1.0, 0.0)
    sel = jnp.where((rank < SEL_TOPK) & allowed, 1.0, 0.0).astype(BF16)
    sel = jnp.concatenate([sel] * A_REP, axis=0)

    prev0 = pl.multiple_of(jnp.maximum(q0 - tq, 0), tq)
    near_pos = jnp.where(col_idx(NSA_NEAR) < tq, prev0, q0 - tq) + col_idx(NSA_NEAR)
    near_ok = (near_pos <= row_pos(NSA_NEAR)) & ((col_idx(NSA_NEAR) >= tq) | (qt > 0))
    btile = btile_ref[...]

    def near(k_ref, v_ref, extra_mask):
        k = jnp.concatenate([k_ref[pl.ds(prev0, tq), :], k_ref[pl.ds(q0, tq), :]], axis=0)
        v = jnp.concatenate([v_ref[pl.ds(prev0, tq), :], v_ref[pl.ds(q0, tq), :]], axis=0)
        mask = near_ok if extra_mask is None else (near_ok & extra_mask)
        return _softmax_start(jnp.where(mask, _dot_nt(qs, k) + btile, NEG_INF), v)

    expand_near = jnp.concatenate([_block_expand(prev0, tq), _block_expand(q0, tq)], axis=1)
    sel_near = jnp.dot(sel, expand_near, preferred_element_type=F32) > 0.5

    far_end = q0 - tq

    def sel_far(i, carry):
        k0 = pl.multiple_of(i * NSA_FAR_TILE, NSA_FAR_TILE)
        s = _dot_nt(qs, ks_ref[pl.ds(k0, NSA_FAR_TILE), :])
        hit = jnp.dot(sel, _block_expand(k0, NSA_FAR_TILE), preferred_element_type=F32) > 0.5
        mask = hit & (k0 + col_idx(NSA_FAR_TILE) < far_end)
        return _softmax_more(carry, jnp.where(mask, s, NEG_INF), vs_ref[pl.ds(k0, NSA_FAR_TILE), :])

    n_far = (jnp.maximum(far_end, 0) + NSA_FAR_TILE - 1) // NSA_FAR_TILE
    _, l_s, acc_s = lax.fori_loop(0, n_far, sel_far, near(ks_ref, vs_ref, sel_near))
    o_s = acc_s / l_s

    w_far = WIN_A - tq
    wf0 = pl.multiple_of(jnp.maximum(q0 - WIN_A, 0), tq)
    wpos = wf0 + col_idx(w_far)
    wmask = (row_pos(w_far) - wpos < WIN_A) & (wpos < far_end)
    s_w = jnp.where(wmask, _dot_nt(qs, kw_ref[pl.ds(wf0, w_far), :]), NEG_INF)
    _, l_w, acc_w = _softmax_more(near(kw_ref, vw_ref, None), s_w, vw_ref[pl.ds(wf0, w_far), :])
    o_w = acc_w / l_w

    gates = gate_ref[...]
    first_group = pl.program_id(1) == 0
    outs = []
    for r in range(A_REP):
        sl = slice(r * tq, (r + 1) * tq)
        gc, gs, gw = (jnp.where(first_group, gates[:, 3 * r + c:3 * r + c + 1],
                                gates[:, 3 * (A_REP + r) + c:3 * (A_REP + r) + c + 1]) for c in range(3))
        outs.append(gc * o_c[sl] + gs * o_s[sl] + gw * o_w[sl])
    o_ref[...] = jnp.concatenate(outs, axis=1)


def _near_bucket_table(tq):
    i = np.arange(tq)[:, None]
    j = np.arange(2 * tq)[None, :]
    dist = np.where(j < tq, tq + i - j, i - (j - tq))
    n = np.maximum(dist, 0)
    exact = NUM_BUCKETS // 2
    nf = np.maximum(n, exact).astype(np.float64)
    large = exact + (np.log(nf / exact) / math.log(MAX_DISTANCE / exact) * (NUM_BUCKETS - exact)).astype(np.int64)
    return np.where(n < exact, n, np.minimum(large, NUM_BUCKETS - 1)).astype(np.int32)


def _cover_matrix(n_cmp_pad, n_sel):
    n = np.arange(n_cmp_pad)
    c_start = n * CMP_STRIDE
    c_end = c_start + CMP_LEN - 1
    s_start = np.arange(128) * SEL_BLOCK
    cover = (c_start[:, None] < s_start[None, :] + SEL_BLOCK) & (c_end[:, None] >= s_start[None, :])
    cover &= (np.arange(128)[None, :] < n_sel)
    return np.tile(cover.astype(np.float32), (A_REP, 1))


def _nsa_prompt_attend(q, gates, kc, vc, kvt, tab):
    b, t, _ = q.shape
    tq = NSA_Q_TILE
    assert t % NSA_FAR_TILE == 0 and t // SEL_BLOCK <= 32 and t >= WIN_A and kc.shape[2] == 128
    near = tab[_near_bucket_table(tq)] - tab[NUM_BUCKETS - 1]
    btile = jnp.transpose(near, (2, 3, 0, 1)).reshape(A_KV, A_REP * tq, 2 * tq)
    cover = jnp.asarray(_cover_matrix(128, t // SEL_BLOCK), BF16)
    cmp_spec = pl.BlockSpec((None, None, 128, HEAD_DIM), lambda i, g, j: (i, g, 0, 0))
    kvt_spec = lambda slot: pl.BlockSpec((None, None, t, HEAD_DIM), lambda i, g, j: (i, slot + g, 0, 0))
    return pl.pallas_call(
        functools.partial(_nsa_prompt_body, tq=tq),
        grid=(b, A_KV, t // tq),
        in_specs=[
            pl.BlockSpec((None, tq, A_REP * HEAD_DIM), lambda i, g, j: (i, j, g)),
            pl.BlockSpec((None, tq, 128), lambda i, g, j: (i, j, 0)),
            cmp_spec, cmp_spec, kvt_spec(0), kvt_spec(2), kvt_spec(4), kvt_spec(6),
            pl.BlockSpec((None, A_REP * tq, 2 * tq), lambda i, g, j: (g, 0, 0)),
            pl.BlockSpec((A_REP * 128, 128), lambda i, g, j: (0, 0)),
        ],
        out_specs=pl.BlockSpec((None, tq, A_REP * HEAD_DIM), lambda i, g, j: (i, j, g)),
        out_shape=jax.ShapeDtypeStruct((b, t, A_Q), F32),
        compiler_params=pltpu.CompilerParams(
            dimension_semantics=("parallel", "parallel", "arbitrary"),
            vmem_limit_bytes=V7X_VMEM_LIMIT_BYTES),
        name="nsa_prompt",
    )(q, gates, kc, vc, kvt, kvt, kvt, kvt, btile, cover)


S5_NS = B_GROUPS * B_STATE
S5_T_CHUNK = 64
S5_STRIP = 512


def _s5_body(u_ref, h0r_ref, h0i_ref, ar_ref, ai_ref, ldt_ref, wb_ref, wc_ref, d_ref, wglu_ref, bglu_ref,
             o_ref, hr_ref, hi_ref, coef_ref, st_ref, xbuf_ref, ubuf_ref, obuf_ref, *, nb, steps, interleave):
    c = pl.program_id(0)

    @pl.when(c == 0)
    def _():
        dt = jnp.exp(ldt_ref[...])
        ar, ai = ar_ref[...], ai_ref[...]
        mag = jnp.exp(ar * dt)
        abr, abi = mag * jnp.cos(ai * dt), mag * jnp.sin(ai * dt)
        den = ar * ar + ai * ai
        wr = ((abr - 1.0) * ar + abi * ai) / den
        wi = (abi * ar - (abr - 1.0) * ai) / den
        for k, val in enumerate((abr, abi, wr, wi)):
            coef_ref[k] = jnp.broadcast_to(val, (nb, S5_NS))
        st_ref[0] = h0r_ref[...]
        st_ref[1] = h0i_ref[...]

    if interleave:
        for b in range(nb):
            for j in range(B_WIDTH // 128):
                ubuf_ref.at[j][pl.ds(b, steps, stride=nb), :] = u_ref[b, :, j * 128:(j + 1) * 128]
        u = jnp.concatenate([ubuf_ref[j] for j in range(B_WIDTH // 128)], axis=1)
    else:
        u = u_ref[...]
    xbuf_ref[...] = jnp.dot(u.astype(BF16), wb_ref[...], preferred_element_type=F32)

    for lo in range(0, S5_NS, S5_STRIP):
        re = slice(lo, lo + S5_STRIP)
        im = slice(S5_NS + lo, S5_NS + lo + S5_STRIP)
        abr, abi, wr, wi = (coef_ref[k, :, re] for k in range(4))

        def step(t, carry):
            sr, si = carry
            r0 = pl.multiple_of(t * nb, nb)
            bur = xbuf_ref[pl.ds(r0, nb), re]
            bui = xbuf_ref[pl.ds(r0, nb), im]
            nsr = abr * sr - abi * si + (wr * bur - wi * bui)
            nsi = abr * si + abi * sr + (wr * bui + wi * bur)
            xbuf_ref[pl.ds(r0, nb), re] = nsr
            xbuf_ref[pl.ds(r0, nb), im] = nsi
            return nsr, nsi

        sr, si = lax.fori_loop(0, steps, step, (st_ref[0, :, re], st_ref[1, :, re]),
                               unroll=min(steps, 8))
        st_ref[0, :, re] = sr
        st_ref[1, :, re] = si

    y = jnp.dot(xbuf_ref[...].astype(BF16), wc_ref[...], preferred_element_type=F32) + d_ref[...] * u
    z = _gelu_tanh(y)
    gate = jnp.dot(z.astype(BF16), wglu_ref[...], preferred_element_type=F32) + bglu_ref[...]
    out = z * (1.0 / (1.0 + jnp.exp(-gate)))
    if interleave:
        for j in range(B_WIDTH // 128):
            obuf_ref[j] = out[:, j * 128:(j + 1) * 128]
        for b in range(nb):
            for j in range(B_WIDTH // 128):
                o_ref[b, :, j * 128:(j + 1) * 128] = obuf_ref.at[j][pl.ds(b, steps, stride=nb), :]
    else:
        o_ref[...] = out
    hr_ref[...] = st_ref[0]
    hi_ref[...] = st_ref[1]


def _s5_weights(a_re, a_im, log_dt, b_re, b_im, c_re, c_im, d, w_glu, b_glu):
    eye = jnp.eye(B_GROUPS, dtype=F32)
    blk_in = lambda w: jnp.einsum('hg,gpc->hcgp', eye, w).reshape(B_WIDTH, S5_NS)
    blk_out = lambda w: jnp.einsum('gh,gcp->gphc', eye, w).reshape(S5_NS, B_WIDTH)
    wb = jnp.concatenate([blk_in(b_re), blk_in(b_im)], axis=1).astype(BF16)
    wc = jnp.concatenate([blk_out(c_re), -blk_out(c_im)], axis=0).astype(BF16)
    flat = lambda x: x.reshape(1, S5_NS)
    return (flat(a_re), flat(a_im), flat(jnp.repeat(log_dt, B_STATE)), wb, wc, d.reshape(1, B_WIDTH),
            w_glu.astype(BF16), b_glu.reshape(1, B_WIDTH))


def _s5_mix(u, h_re, h_im, weights):
    nb, t, _ = u.shape
    interleave = t > 1
    steps = min(t, S5_T_CHUNK)
    rows = nb * steps
    body = functools.partial(_s5_body, nb=nb, steps=steps, interleave=interleave)
    if interleave:
        u_in = u
        u_spec = pl.BlockSpec((nb, steps, B_WIDTH), lambda c: (0, c, 0))
        o_shape = jax.ShapeDtypeStruct((nb, t, B_WIDTH), F32)
        scratch_rows = rows
    else:
        u_in = u.reshape(nb, B_WIDTH)
        u_spec = pl.BlockSpec((nb, B_WIDTH), lambda c: (0, 0))
        o_shape = jax.ShapeDtypeStruct((nb, B_WIDTH), F32)
        scratch_rows = 8
    o, hr, hi = pl.pallas_call(
        body,
        grid=(t // steps,),
        in_specs=[
            u_spec, _const_spec((nb, S5_NS)), _const_spec((nb, S5_NS)),
            _const_spec((1, S5_NS)), _const_spec((1, S5_NS)), _const_spec((1, S5_NS)),
            _const_spec((B_WIDTH, 2 * S5_NS)), _const_spec((2 * S5_NS, B_WIDTH)),
            _const_spec((1, B_WIDTH)), _const_spec((B_WIDTH, B_WIDTH)), _const_spec((1, B_WIDTH)),
        ],
        out_specs=[u_spec, pl.BlockSpec((nb, S5_NS), lambda c: (0, 0)), pl.BlockSpec((nb, S5_NS), lambda c: (0, 0))],
        out_shape=[o_shape, jax.ShapeDtypeStruct((nb, S5_NS), F32), jax.ShapeDtypeStruct((nb, S5_NS), F32)],
        scratch_shapes=[
            pltpu.VMEM((4, nb, S5_NS), F32),
            pltpu.VMEM((2, nb, S5_NS), F32),
            pltpu.VMEM((rows, 2 * S5_NS), F32),
            pltpu.VMEM((B_WIDTH // 128, scratch_rows, 128), F32),
            pltpu.VMEM((B_WIDTH // 128, scratch_rows, 128), F32),
        ],
        compiler_params=pltpu.CompilerParams(
            dimension_semantics=("arbitrary",),
            vmem_limit_bytes=V7X_VMEM_LIMIT_BYTES),
        name="s5_mix",
    )(u_in, h_re, h_im, *weights)
    return o.reshape(nb, t, B_WIDTH), hr, hi


CHUNK_W = CMP_STRIDE * 2 * A_KVW
CHUNKS_PER_PAGE = PAGE_SIZE // CMP_STRIDE
SEL_PAD = 256


def _bf16_round(x):
    return x.astype(BF16).astype(F32)


def _compress_tail(c, pos, w2):
    c1 = c[:, 128:]
    nxt = jnp.concatenate([c1[1:], c1[:1]], axis=0)
    hid = (pos + c[:, :128]) + nxt
    return jnp.dot(_gelu_tanh(hid).astype(BF16), w2, preferred_element_type=F32)


def _rms_heads128(x, gain):
    left = lax.broadcasted_iota(jnp.int32, x.shape, 1) < HEAD_DIM
    sq = x * x
    s0 = jnp.sum(jnp.where(left, sq, 0.0), axis=-1, keepdims=True)
    s1 = jnp.sum(jnp.where(left, 0.0, sq), axis=-1, keepdims=True)
    ms = jnp.where(left, s0, s1) * (1.0 / HEAD_DIM)
    return x * lax.rsqrt(ms + EPS) * gain


def _pad_rows8(x):
    return jnp.concatenate([x, jnp.zeros((8 - x.shape[0], x.shape[1]), x.dtype)], axis=0)


def _compress_rows(rows_scr, kv, n_chunks, w_ref, pos, w2_ref):
    acc = jnp.zeros((n_chunks, 2 * A_KVW), F32)
    for sp in range(CMP_STRIDE // 2):
        x = jnp.concatenate([rows_scr.at[kv][pl.ds(2 * sp + e, n_chunks, stride=CMP_STRIDE), :] for e in range(2)],
                            axis=1).astype(BF16)
        acc = acc + jnp.dot(x, w_ref[sp], preferred_element_type=F32)
    return _compress_tail(acc, pos, w2_ref[...])


def _compress_weights(cmp_pos, cmp_w1, cmp_w2, k_gain):
    eye = jnp.eye(A_KV, dtype=F32)
    w1 = cmp_w1.reshape(2, 2, CMP_STRIDE, HEAD_DIM, HEAD_DIM)
    w_big = jnp.einsum('ajsdh,gk->asgdjkh', w1, eye).reshape(2, CMP_STRIDE // 2, 2 * A_KVW, 2 * A_KVW).astype(BF16)
    w_pos = jnp.concatenate([cmp_w1, cmp_w1], axis=-1).astype(BF16)
    pos = jnp.broadcast_to(cmp_pos.reshape(2, 1, CMP_LEN * HEAD_DIM), (2, 8, CMP_LEN * HEAD_DIM)).astype(BF16)
    w2_big = jnp.einsum('ahd,gk->aghkd', cmp_w2, eye).reshape(2, A_KVW, A_KVW).astype(BF16)
    gain2 = jnp.tile(k_gain, A_KV).reshape(1, A_KVW)
    return (pos[0], pos[1], w_pos[0], w_pos[1], w_big[0], w_big[1], w2_big[0], w2_big[1], gain2)


def _compress_specs(full):
    mlp_w = (CMP_STRIDE // 2, 2 * A_KVW, 2 * A_KVW)
    return [full((8, CMP_LEN * HEAD_DIM)), full((8, CMP_LEN * HEAD_DIM)),
            full((CMP_LEN * HEAD_DIM, A_KVW)), full((CMP_LEN * HEAD_DIM, A_KVW)),
            full(mlp_w), full(mlp_w), full((A_KVW, A_KVW)), full((A_KVW, A_KVW)), full((1, A_KVW))]


def _pos_terms(pos_scr, posk_ref, posv_ref, wpos_k_ref, wpos_v_ref):
    pos_scr[0] = jnp.dot(posk_ref[...], wpos_k_ref[...], preferred_element_type=F32)
    pos_scr[1] = jnp.dot(posv_ref[...], wpos_v_ref[...], preferred_element_type=F32)


def _cmp_prompt_body(cmp_ref, posk_ref, posv_ref, wpos_k_ref, wpos_v_ref, wk_ref, wv_ref, w2k_ref, w2v_ref,
                     kgain_ref, kc_ref, vc_ref, pos_scr, rows_scr, *, n_chunks):
    @pl.when(pl.program_id(0) == 0)
    def _():
        _pos_terms(pos_scr, posk_ref, posv_ref, wpos_k_ref, wpos_v_ref)

    rows_scr[0] = cmp_ref[:, :A_KVW]
    rows_scr[1] = cmp_ref[:, A_KVW:]
    kc = _rms_heads128(_compress_rows(rows_scr, 0, n_chunks, wk_ref, pos_scr[0, 0:1], w2k_ref),
                       kgain_ref[...]).astype(BF16)
    vc = _compress_rows(rows_scr, 1, n_chunks, wv_ref, pos_scr[1, 0:1], w2v_ref).astype(BF16)
    for g in range(A_KV):
        kc_ref[g] = kc[:, g * HEAD_DIM:(g + 1) * HEAD_DIM]
        vc_ref[g] = vc[:, g * HEAD_DIM:(g + 1) * HEAD_DIM]


def _cmp_prompt(cmp_rows, cmp_weights):
    b, t, _ = cmp_rows.shape
    n_chunks = t // CMP_STRIDE
    full = lambda shape: pl.BlockSpec(shape, lambda i: (0,) * len(shape))
    out_spec = pl.BlockSpec((None, A_KV, n_chunks, HEAD_DIM), lambda i: (i, 0, 0, 0))
    out_shape = jax.ShapeDtypeStruct((b, A_KV, n_chunks, HEAD_DIM), BF16)
    return pl.pallas_call(
        functools.partial(_cmp_prompt_body, n_chunks=n_chunks),
        grid=(b,),
        in_specs=[pl.BlockSpec((None, t, 2 * A_KVW), lambda i: (i, 0, 0))] + _compress_specs(full),
        out_specs=[out_spec, out_spec],
        out_shape=[out_shape, out_shape],
        scratch_shapes=[pltpu.VMEM((2, 8, A_KVW), F32), pltpu.VMEM((2, t, A_KVW), F32)],
        compiler_params=pltpu.CompilerParams(
            dimension_semantics=("arbitrary",), vmem_limit_bytes=V7X_VMEM_LIMIT_BYTES),
        name="nsa_cmp_prompt",
    )(cmp_rows, *cmp_weights)


def _nsa_sample_cmp_body(pt_ref, *refs, n_pages, n_sel):
    page_refs = refs[:n_pages]
    (q_ref, posk_ref, posv_ref, wpos_k_ref, wpos_v_ref, wk_ref, wv_ref, w2k_ref, w2v_ref, kgain_ref, cover_ref,
     oc_ref, idx_ref, pos_scr, rows_scr) = refs[n_pages:]

    @pl.when(pl.program_id(0) == 0)
    def _():
        _pos_terms(pos_scr, posk_ref, posv_ref, wpos_k_ref, wpos_v_ref)

    n_chunks = n_pages * CHUNKS_PER_PAGE
    n_cmp = n_chunks - 1
    for kv in range(2):
        for p in range(n_pages):
            rows_scr[kv, p * PAGE_SIZE:(p + 1) * PAGE_SIZE, :] = page_refs[p][kv].reshape(A_KVW, PAGE_SIZE).T

    kc = _rms_heads128(_compress_rows(rows_scr, 0, n_chunks, wk_ref, pos_scr[0, 0:1], w2k_ref),
                       kgain_ref[...]).astype(BF16)
    vc = _compress_rows(rows_scr, 1, n_chunks, wv_ref, pos_scr[1, 0:1], w2v_ref).astype(BF16)

    col = lax.broadcasted_iota(jnp.int32, (8, n_chunks), 1)
    s_idx = lax.broadcasted_iota(jnp.int32, (1, SEL_PAD), 1)
    s_idx_f = s_idx.astype(F32)
    forced = (s_idx == 0) | (s_idx == n_sel - 1) | (s_idx == n_sel - 2)
    lane = lax.broadcasted_iota(jnp.int32, (1, 128), 1)
    for g in range(A_KV):
        hs = slice(g * HEAD_DIM, (g + 1) * HEAD_DIM)
        q8 = (_pad_rows8(q_ref[g]) * (HEAD_DIM ** -0.5)).astype(BF16)
        s = jnp.where(col < n_cmp, _dot_nt(q8, kc[:, hs]), NEG_INF)
        e = jnp.exp(s - jnp.max(s, axis=-1, keepdims=True))
        p = (e / jnp.sum(e, axis=-1, keepdims=True)).astype(BF16)
        oc_ref[g] = jnp.dot(p, vc[:, hs], preferred_element_type=F32)[0:A_REP]
        imp = jnp.sum(jnp.dot(p, cover_ref[...], preferred_element_type=F32)[0:A_REP], axis=0, keepdims=True)
        score = jnp.where(s_idx < n_sel, imp + jnp.where(forced, FORCE_BONUS, 0.0), NEG_INF)
        rank = jnp.zeros((1, SEL_PAD), F32)
        for j in range(n_sel):
            cj = score[:, j:j + 1]
            beats = (cj > score) | ((cj == score) & (s_idx > j))
            rank = rank + jnp.where(beats, 1.0, 0.0)
        picks = jnp.zeros((1, 128), F32)
        for r in range(SEL_TOPK):
            block = jnp.sum(jnp.where(rank == float(r), s_idx_f, 0.0), axis=-1, keepdims=True)
            picks = picks + jnp.where(lane == r, block, 0.0)
        idx_ref[g:g + 1, :] = picks.astype(jnp.int32)


def _nsa_sample_att_body(idx_ref, pt_ref, *refs, n_past_blk):
    page_refs = refs[:SEL_TOPK]
    (q_ref, gate_ref, oc_ref, newsel_ref, wcache_ref, newwin_ref, selb_ref, winb_ref, newb_ref, o_ref) = refs[SEL_TOPK:]
    b, g = pl.program_id(0), pl.program_id(1)
    blocks_per_page = PAGE_SIZE // SEL_BLOCK
    q8 = (_pad_rows8(q_ref[...]) * (HEAD_DIM ** -0.5)).astype(BF16)
    new_bias = newb_ref[:, 0:1]

    def attend(s, v_t, new_kv):
        k_new = _bf16_round(new_kv[0:1])
        v_new = _bf16_round(new_kv[1:2])
        s_new = jnp.sum(q8.astype(F32) * k_new, axis=-1, keepdims=True) + new_bias
        m = jnp.maximum(jnp.max(s, axis=-1, keepdims=True), s_new)
        e = jnp.exp(s - m)
        e_new = jnp.exp(s_new - m)
        den = jnp.sum(e, axis=-1, keepdims=True) + e_new
        return _dot_nt((e / den).astype(BF16), v_t) + _bf16_round(e_new / den) * v_new

    lane = lax.broadcasted_iota(jnp.int32, (8, PAGE_SIZE), 1)
    near = selb_ref[...]
    scores = []
    for k in range(SEL_TOPK):
        ik = idx_ref[(b * A_KV + g) * SEL_TOPK + k]
        blk = jnp.minimum(ik, n_past_blk - 1)
        s_k = jnp.dot(q8, page_refs[k][0].astype(BF16), preferred_element_type=F32)
        s_k = s_k + jnp.where(blk // blocks_per_page == (n_past_blk - 1) // blocks_per_page, near, 0.0)
        keep = (lane // SEL_BLOCK == blk % blocks_per_page) & (ik < n_past_blk)
        scores.append(jnp.where(keep, s_k, NEG_INF))
    v_sel = jnp.concatenate([page_refs[k][1].astype(BF16) for k in range(SEL_TOPK)], axis=1)
    o_s = attend(jnp.concatenate(scores, axis=1), v_sel, newsel_ref[...])

    w_len = wcache_ref.shape[-1]
    wcol = lax.broadcasted_iota(jnp.int32, (8, w_len), 1)
    s_w = jnp.dot(q8, wcache_ref[0].astype(BF16), preferred_element_type=F32) + winb_ref[...]
    o_w = attend(jnp.where(wcol >= 1, s_w, NEG_INF), wcache_ref[1].astype(BF16), newwin_ref[...])
    gates = gate_ref[...]
    o_ref[...] = gates[:, 0:1] * oc_ref[...] + gates[:, 1:2] * o_s[0:A_REP] + gates[:, 2:3] * o_w[0:A_REP]


def _bucket_np(dist):
    n = np.maximum(dist, 0)
    exact = NUM_BUCKETS // 2
    nf = np.maximum(n, exact).astype(np.float64)
    large = exact + (np.log(nf / exact) / math.log(MAX_DISTANCE / exact) * (NUM_BUCKETS - exact)).astype(np.int64)
    return np.where(n < exact, n, np.minimum(large, NUM_BUCKETS - 1)).astype(np.int32)


def _rows_last(cache):
    nd = cache.ndim
    return jnp.transpose(cache, tuple(range(nd - 4)) + (nd - 3, nd - 2, nd - 1, nd - 4))


def _nsa_sample(q, gates, new_sel, new_win, pool_cmp, pool_sel, win_cache, page_table, cmp_weights, tab):
    n, n_pages = page_table.shape
    past = n_pages * PAGE_SIZE
    n_past_blk = past // SEL_BLOCK
    n_sel = n_past_blk + 1
    n_chunks = n_pages * CHUNKS_PER_PAGE
    w_len = win_cache.shape[1]
    blocks_per_page = PAGE_SIZE // SEL_BLOCK
    assert n_sel <= SEL_PAD and w_len == WIN_A and past >= WIN_A and blocks_per_page == 2
    pt_flat = page_table.reshape(-1)

    c_idx = np.arange(n_chunks)
    s_start = np.arange(SEL_PAD) * SEL_BLOCK
    cover = ((c_idx[:, None] * CMP_STRIDE < s_start[None, :] + SEL_BLOCK)
             & (c_idx[:, None] * CMP_STRIDE + CMP_LEN - 1 >= s_start[None, :])
             & (c_idx[:, None] < n_chunks - 1) & (np.arange(SEL_PAD)[None, :] < n_sel))
    cover = jnp.asarray(cover.astype(np.float32), BF16)

    head_spec = lambda last: pl.BlockSpec((None, A_KV, A_REP, last), lambda i, *_: (i, 0, 0, 0))
    page_specs = [pl.BlockSpec((None, 2, A_KV, HEAD_DIM, PAGE_SIZE),
                               functools.partial(lambda i, pt, p: (pt[i * n_pages + p], 0, 0, 0, 0), p=p))
                  for p in range(n_pages)]
    full = lambda shape: pl.BlockSpec(shape, lambda i, *_: (0,) * len(shape))
    pool_c = _rows_last(pool_cmp)
    o_c, idx = pl.pallas_call(
        functools.partial(_nsa_sample_cmp_body, n_pages=n_pages, n_sel=n_sel),
        grid_spec=pltpu.PrefetchScalarGridSpec(
            num_scalar_prefetch=1,
            grid=(n,),
            in_specs=page_specs + [head_spec(HEAD_DIM)] + _compress_specs(full) + [full((n_chunks, SEL_PAD))],
            out_specs=[head_spec(HEAD_DIM), pl.BlockSpec((None, A_KV, 128), lambda i, *_: (i, 0, 0))],
            scratch_shapes=[pltpu.VMEM((2, 8, A_KVW), F32), pltpu.VMEM((2, past, A_KVW), F32)],
        ),
        out_shape=[jax.ShapeDtypeStruct((n, A_KV, A_REP, HEAD_DIM), F32),
                   jax.ShapeDtypeStruct((n, A_KV, 128), jnp.int32)],
        compiler_params=pltpu.CompilerParams(
            dimension_semantics=("arbitrary",), vmem_limit_bytes=V7X_VMEM_LIMIT_BYTES),
        name="nsa_sample_cmp",
    )(pt_flat, *([pool_c] * n_pages), q, *cmp_weights, cover)

    tab_h = jnp.transpose(tab, (1, 2, 0))
    rel = lambda dist: jnp.pad(tab_h[:, :, _bucket_np(dist)] - tab_h[:, :, NUM_BUCKETS - 1:],
                               ((0, 0), (0, 8 - A_REP), (0, 0)))
    sel_bias = rel(past - (past - PAGE_SIZE + np.arange(PAGE_SIZE)))
    win_bias = rel(w_len - np.arange(w_len))
    new_bias = rel(np.zeros(128, np.int64))

    idx_flat = idx[:, :, :SEL_TOPK].reshape(-1)

    def page_map(i, g, idx_s, pt, slot):
        blk = jnp.minimum(idx_s[(i * A_KV + g) * SEL_TOPK + slot], n_past_blk - 1)
        return (pt[i * n_pages + blk // blocks_per_page], 0, g, 0, 0)

    slot_specs = [pl.BlockSpec((None, 2, None, HEAD_DIM, PAGE_SIZE), functools.partial(page_map, slot=s))
                  for s in range(SEL_TOPK)]
    per_head = lambda rows, last: pl.BlockSpec((None, None, rows, last), lambda i, g, *_: (i, g, 0, 0))
    per_group = lambda last: pl.BlockSpec((None, 8, last), lambda i, g, *_: (g, 0, 0))
    return pl.pallas_call(
        functools.partial(_nsa_sample_att_body, n_past_blk=n_past_blk),
        grid_spec=pltpu.PrefetchScalarGridSpec(
            num_scalar_prefetch=2,
            grid=(n, A_KV),
            in_specs=slot_specs + [
                per_head(A_REP, HEAD_DIM), per_head(A_REP, 3), per_head(A_REP, HEAD_DIM), per_head(2, HEAD_DIM),
                pl.BlockSpec((None, 2, None, HEAD_DIM, w_len), lambda i, g, *_: (i, 0, g, 0, 0)),
                per_head(2, HEAD_DIM), per_group(PAGE_SIZE), per_group(w_len), per_group(128),
            ],
            out_specs=per_head(A_REP, HEAD_DIM),
        ),
        out_shape=jax.ShapeDtypeStruct((n, A_KV, A_REP, HEAD_DIM), F32),
        compiler_params=pltpu.CompilerParams(
            dimension_semantics=("arbitrary", "arbitrary"), vmem_limit_bytes=V7X_VMEM_LIMIT_BYTES),
        name="nsa_sample_att",
    )(idx_flat, pt_flat, *([_rows_last(pool_sel)] * SEL_TOPK), q, gates, o_c,
      jnp.transpose(new_sel, (0, 2, 1, 3)), _rows_last(win_cache), jnp.transpose(new_win, (0, 2, 1, 3)),
      sel_bias, win_bias, new_bias)


SWA_Q_TILE = WIN_C


def _swa_prompt_body(q_ref, k_ref, v_ref, btile_ref, sink_ref, o_ref, *, tq):
    qt = pl.program_id(2)
    q0 = pl.multiple_of(qt * tq, tq)
    rows = C_REP * tq
    q = q_ref[...]
    qs = jnp.concatenate([q[:, r * HEAD_DIM:(r + 1) * HEAD_DIM] for r in range(C_REP)], axis=0)
    qs = (qs * (HEAD_DIM ** -0.5)).astype(BF16)
    prev0 = pl.multiple_of(jnp.maximum(q0 - tq, 0), tq)
    k = jnp.concatenate([k_ref[pl.ds(prev0, tq), :], k_ref[pl.ds(q0, tq), :]], axis=0)
    v = jnp.concatenate([v_ref[pl.ds(prev0, tq), :], v_ref[pl.ds(q0, tq), :]], axis=0)
    col = lax.broadcasted_iota(jnp.int32, (rows, 2 * tq), 1)
    row = lax.broadcasted_iota(jnp.int32, (rows, 2 * tq), 0) & (tq - 1)
    dist = tq + row - col
    mask = (dist >= 0) & (dist < WIN_C) & ((col >= tq) | (qt > 0))
    s = jnp.where(mask, _dot_nt(qs, k) + btile_ref[...], NEG_INF)
    sinks = sink_ref[...]
    sink = jnp.concatenate([jnp.broadcast_to(sinks[r:r + 1, 0:1], (tq, 1)) for r in range(C_REP)], axis=0)
    m = jnp.maximum(jnp.max(s, axis=-1, keepdims=True), sink)
    e = jnp.exp(s - m)
    p = e / (jnp.sum(e, axis=-1, keepdims=True) + jnp.exp(sink - m))
    o = jnp.dot(p.astype(BF16), v, preferred_element_type=F32)
    o_ref[...] = jnp.concatenate([o[r * tq:(r + 1) * tq] for r in range(C_REP)], axis=1)


def _swa_prompt(q, kvt, sinks, tab):
    b, t, _ = q.shape
    tq = SWA_Q_TILE
    i = np.arange(tq)[:, None]
    j = np.arange(2 * tq)[None, :]
    btile = jnp.transpose(tab[_bucket_np(tq + i - j)], (2, 3, 0, 1)).reshape(C_KV, C_REP * tq, 2 * tq)
    kv_spec = lambda slot: pl.BlockSpec((None, None, t, HEAD_DIM), lambda i, g, j: (i, slot + g, 0, 0))
    qo_spec = pl.BlockSpec((None, tq, C_REP * HEAD_DIM), lambda i, g, j: (i, j, g))
    sink_lanes = jnp.broadcast_to(sinks.reshape(C_KV, C_REP, 1), (C_KV, C_REP, 128))
    return pl.pallas_call(
        functools.partial(_swa_prompt_body, tq=tq),
        grid=(b, C_KV, t // tq),
        in_specs=[qo_spec, kv_spec(0), kv_spec(C_KV),
                  pl.BlockSpec((None, C_REP * tq, 2 * tq), lambda i, g, j: (g, 0, 0)),
                  pl.BlockSpec((None, C_REP, 128), lambda i, g, j: (g, 0, 0))],
        out_specs=qo_spec,
        out_shape=jax.ShapeDtypeStruct((b, t, C_HEADS * HEAD_DIM), F32),
        compiler_params=pltpu.CompilerParams(
            dimension_semantics=("parallel", "parallel", "arbitrary"), vmem_limit_bytes=V7X_VMEM_LIMIT_BYTES),
        name="swa_prompt",
    )(q, kvt, kvt, btile, sink_lanes)


def _swa_sample_body(q_ref, cache_ref, new_ref, bias_ref, newb_ref, sink_ref, o_ref):
    w_len = cache_ref.shape[-1]
    wcol = lax.broadcasted_iota(jnp.int32, (C_REP, w_len), 1)
    for g in range(C_KV):
        q8 = (q_ref[g] * (HEAD_DIM ** -0.5)).astype(BF16)
        s = jnp.dot(q8, cache_ref[0, g].astype(BF16), preferred_element_type=F32) + bias_ref[g]
        s = jnp.where(wcol >= 1, s, NEG_INF)
        k_new = _bf16_round(new_ref[g, 0:1])
        v_new = _bf16_round(new_ref[g, 1:2])
        s_new = jnp.sum(q8.astype(F32) * k_new, axis=-1, keepdims=True) + newb_ref[g][:, 0:1]
        sink = sink_ref[g][:, 0:1]
        m = jnp.maximum(jnp.maximum(jnp.max(s, axis=-1, keepdims=True), s_new), sink)
        e = jnp.exp(s - m)
        e_new = jnp.exp(s_new - m)
        den = jnp.sum(e, axis=-1, keepdims=True) + e_new + jnp.exp(sink - m)
        o_ref[g] = _dot_nt((e / den).astype(BF16), cache_ref[1, g].astype(BF16)) + _bf16_round(e_new / den) * v_new


def _swa_sample(q, cache, new_kv, sinks, tab):
    n, w_len = cache.shape[:2]
    tab_h = jnp.transpose(tab, (1, 2, 0))
    bias = tab_h[:, :, _bucket_np(w_len - np.arange(w_len))]
    lanes = lambda x: jnp.broadcast_to(x[:, :, None], (C_KV, C_REP, 128))
    full = lambda shape: pl.BlockSpec(shape, lambda i: (0,) * len(shape))
    return pl.pallas_call(
        _swa_sample_body,
        grid=(n,),
        in_specs=[
            pl.BlockSpec((None, C_KV, C_REP, HEAD_DIM), lambda i: (i, 0, 0, 0)),
            pl.BlockSpec((None, 2, C_KV, HEAD_DIM, w_len), lambda i: (i, 0, 0, 0, 0)),
            pl.BlockSpec((None, C_KV, 2, HEAD_DIM), lambda i: (i, 0, 0, 0)),
            full((C_KV, C_REP, w_len)), full((C_KV, C_REP, 128)), full((C_KV, C_REP, 128)),
        ],
        out_specs=pl.BlockSpec((None, C_KV, C_REP, HEAD_DIM), lambda i: (i, 0, 0, 0)),
        out_shape=jax.ShapeDtypeStruct((n, C_KV, C_REP, HEAD_DIM), F32),
        compiler_params=pltpu.CompilerParams(
            dimension_semantics=("arbitrary",), vmem_limit_bytes=V7X_VMEM_LIMIT_BYTES),
        name="swa_sample",
    )(q, _rows_last(cache), new_kv, bias, lanes(tab_h[:, :, 0]), lanes(sinks.reshape(C_KV, C_REP)))


PROJ_ROW_TILE = 512
A_IN_PAD = 1920
A_U_COL = A_Q + 6 * A_KVW
A_GATE_COL = A_U_COL + B_WIDTH


def _heads_first(kvt_ref, slot, k, v, n_kv):
    for g in range(n_kv):
        kvt_ref[slot + g] = k[:, g * HEAD_DIM:(g + 1) * HEAD_DIM].astype(BF16)
        kvt_ref[slot + n_kv + g] = v[:, g * HEAD_DIM:(g + 1) * HEAD_DIM].astype(BF16)


def _inproj_a_body(x_ref, gain_ref, w_ref, qg_ref, kg_ref,
                   q_ref, cmp_ref, sel_ref, win_ref, gate_ref, u_ref, kvt_ref):
    xn = _rms_rows(x_ref[...], gain_ref[...]).astype(BF16)
    z = jnp.dot(xn, w_ref[...], preferred_element_type=F32)
    for j in range(A_Q // 128):
        q_ref[:, j * 128:(j + 1) * 128] = _rms_heads128(z[:, j * 128:(j + 1) * 128], qg_ref[...])
    for out_ref, off, slot in ((cmp_ref, A_Q, None), (sel_ref, A_Q + 2 * A_KVW, 0), (win_ref, A_Q + 4 * A_KVW, 4)):
        k = _rms_heads128(z[:, off:off + A_KVW], kg_ref[...])
        v = z[:, off + A_KVW:off + 2 * A_KVW]
        out_ref[:, :A_KVW] = k
        out_ref[:, A_KVW:] = v
        if slot is not None:
            _heads_first(kvt_ref, slot, k, v, A_KV)
    u_ref[...] = z[:, A_U_COL:A_GATE_COL]
    gate_ref[...] = 1.0 / (1.0 + jnp.exp(-z[:, A_GATE_COL:A_IN_PAD]))


def _inproj_a(x, gain, w_in, q_gain, k_gain):
    b, t, d = x.shape
    tm = min(t, PROJ_ROW_TILE)
    w = jnp.concatenate([w_in[:, :A_U_COL], w_in[:, A_U_COL + A_GATE:], w_in[:, A_U_COL:A_U_COL + A_GATE],
                         jnp.zeros((d, A_IN_PAD - A_GATE_COL - A_GATE), F32)], axis=1).astype(BF16)
    tile2 = lambda g: jnp.tile(g, 2).reshape(1, 128)
    rows = lambda width: pl.BlockSpec((None, tm, width), lambda i, j: (i, j, 0))
    shape = lambda width: jax.ShapeDtypeStruct((b, t, width), F32)
    return pl.pallas_call(
        _inproj_a_body,
        grid=(b, t // tm),
        in_specs=[rows(d), _const_spec((1, d)), _const_spec((d, A_IN_PAD)), _const_spec((1, 128)), _const_spec((1, 128))],
        out_specs=[rows(A_Q), rows(2 * A_KVW), rows(2 * A_KVW), rows(2 * A_KVW), rows(128), rows(B_WIDTH),
                   pl.BlockSpec((None, 8, tm, HEAD_DIM), lambda i, j: (i, 0, j, 0))],
        out_shape=[shape(A_Q), shape(2 * A_KVW), shape(2 * A_KVW), shape(2 * A_KVW), shape(128), shape(B_WIDTH),
                   jax.ShapeDtypeStruct((b, 8, t, HEAD_DIM), BF16)],
        compiler_params=pltpu.CompilerParams(
            dimension_semantics=("parallel", "parallel"), vmem_limit_bytes=V7X_VMEM_LIMIT_BYTES),
        name="inproj_nsa_s5",
    )(x, gain.reshape(1, d), w, tile2(q_gain), tile2(k_gain))


def _inproj_c_body(x_ref, gain_ref, w_ref, qg_ref, kg_ref, q_ref, kv_ref, kvt_ref):
    xn = _rms_rows(x_ref[...], gain_ref[...]).astype(BF16)
    z = jnp.dot(xn, w_ref[...], preferred_element_type=F32)
    n_q = C_HEADS * HEAD_DIM
    for j in range(n_q // 128):
        q_ref[:, j * 128:(j + 1) * 128] = _rms_heads128(z[:, j * 128:(j + 1) * 128], qg_ref[...])
    k = _rms_heads128(z[:, n_q:n_q + C_KV * HEAD_DIM], kg_ref[...])
    v = z[:, n_q + C_KV * HEAD_DIM:]
    kv_ref[:, :C_KV * HEAD_DIM] = k
    kv_ref[:, C_KV * HEAD_DIM:] = v
    _heads_first(kvt_ref, 0, k, v, C_KV)


def _inproj_c(x, gain, w_in, q_gain, k_gain):
    b, t, d = x.shape
    tm = min(t, PROJ_ROW_TILE)
    n_in = w_in.shape[1]
    tile2 = lambda g: jnp.tile(g, 2).reshape(1, 128)
    rows = lambda width: pl.BlockSpec((None, tm, width), lambda i, j: (i, j, 0))
    shape = lambda width: jax.ShapeDtypeStruct((b, t, width), F32)
    return pl.pallas_call(
        _inproj_c_body,
        grid=(b, t // tm),
        in_specs=[rows(d), _const_spec((1, d)), _const_spec((d, n_in)), _const_spec((1, 128)), _const_spec((1, 128))],
        out_specs=[rows(C_HEADS * HEAD_DIM), rows(2 * C_KV * HEAD_DIM),
                   pl.BlockSpec((None, 2 * C_KV, tm, HEAD_DIM), lambda i, j: (i, 0, j, 0))],
        out_shape=[shape(C_HEADS * HEAD_DIM), shape(2 * C_KV * HEAD_DIM),
                   jax.ShapeDtypeStruct((b, 2 * C_KV, t, HEAD_DIM), BF16)],
        compiler_params=pltpu.CompilerParams(
            dimension_semantics=("parallel", "parallel"), vmem_limit_bytes=V7X_VMEM_LIMIT_BYTES),
        name="inproj_swa",
    )(x, gain.reshape(1, d), w_in.astype(BF16), tile2(q_gain), tile2(k_gain))


FFN_ROW_TILE = 512
FFN_COL_CHUNK = 1408


def _mixer_residual(y_ref, x_ref, mix_refs, wout_refs):
    y_ref[...] = x_ref[...]
    for m_ref, w_ref in zip(mix_refs, wout_refs):
        y_ref[...] += jnp.dot(m_ref[...].astype(BF16), w_ref[...], preferred_element_type=F32)
    return y_ref[...]


def _tail_prompt_body(*refs, n_mix, tm, ffc):
    x_ref = refs[0]
    mix_refs = refs[1:1 + n_mix]
    wout_refs = refs[1 + n_mix:1 + 2 * n_mix]
    (gain_ref, prev_ref, wup_ref, wgate_ref, cw_ref, cb_ref, wdown_ref, y_ref, cs_ref, hbuf_ref) = refs[1 + 2 * n_mix:]
    t = pl.program_id(1)
    xn = _rms_rows(_mixer_residual(y_ref, x_ref, mix_refs, wout_refs), gain_ref[...]).astype(BF16)
    for c in range(D_FF // ffc):
        lo = c * ffc
        h = jnp.dot(xn, wup_ref[:, lo:lo + ffc], preferred_element_type=F32)
        g = jnp.dot(xn, wgate_ref[:, lo:lo + ffc], preferred_element_type=F32)

        @pl.when(t == 0)
        def _():
            hbuf_ref[c, 6:8, :] = prev_ref[:, lo:lo + ffc]

        hbuf_ref[c, 8:8 + tm, :] = h
        hm1 = hbuf_ref[c, 7:7 + tm, :]
        hm2 = hbuf_ref[c, 6:6 + tm, :]
        cw = cw_ref[:, lo:lo + ffc]
        hc = cw[0:1] * hm2 + cw[1:2] * hm1 + cw[2:3] * h + cb_ref[:, lo:lo + ffc]
        a = (_gelu_tanh(hc) * g).astype(BF16)
        y_ref[...] += jnp.dot(a, wdown_ref[lo:lo + ffc, :], preferred_element_type=F32)
        hbuf_ref[c, 0:8, :] = h[tm - 8:tm, :]
        cs_ref[:, lo:lo + ffc] = h[tm - 2:tm, :]


def _tail_prompt(x, mixes, wouts, gain, prev, wup, wgate, cw, cb, wdown):
    b, t, d = x.shape
    tm, ffc = FFN_ROW_TILE, FFN_COL_CHUNK
    rows = lambda width: pl.BlockSpec((None, tm, width), lambda i, j: (i, j, 0))
    state = pl.BlockSpec((None, CONV_W - 1, D_FF), lambda i, j: (i, 0, 0))
    return pl.pallas_call(
        functools.partial(_tail_prompt_body, n_mix=len(mixes), tm=tm, ffc=ffc),
        grid=(b, t // tm),
        in_specs=[rows(d)] + [rows(m.shape[-1]) for m in mixes] + [_const_spec(w.shape) for w in wouts] + [
            _const_spec((1, d)), state, _const_spec((d, D_FF)), _const_spec((d, D_FF)),
            _const_spec((CONV_W, D_FF)), _const_spec((1, D_FF)), _const_spec((D_FF, d))],
        out_specs=[rows(d), state],
        out_shape=[jax.ShapeDtypeStruct((b, t, d), F32), jax.ShapeDtypeStruct((b, CONV_W - 1, D_FF), F32)],
        scratch_shapes=[pltpu.VMEM((D_FF // ffc, 8 + tm, ffc), F32)],
        compiler_params=pltpu.CompilerParams(
            dimension_semantics=("parallel", "arbitrary"), vmem_limit_bytes=V7X_VMEM_LIMIT_BYTES),
        name="tail_prompt",
    )(x, *mixes, *wouts, gain, prev, wup, wgate, cw, cb, wdown)


def _tail_sample_body(*refs, n_mix, ffc):
    x_ref = refs[0]
    mix_refs = refs[1:1 + n_mix]
    wout_refs = refs[1 + n_mix:1 + 2 * n_mix]
    (gain_ref, prev_ref, wup_ref, wgate_ref, cw_ref, cb_ref, wdown_ref, y_ref, cs_ref) = refs[1 + 2 * n_mix:]
    xn = _rms_rows(_mixer_residual(y_ref, x_ref, mix_refs, wout_refs), gain_ref[...]).astype(BF16)
    for c in range(D_FF // ffc):
        lo = c * ffc
        h = jnp.dot(xn, wup_ref[:, lo:lo + ffc], preferred_element_type=F32)
        g = jnp.dot(xn, wgate_ref[:, lo:lo + ffc], preferred_element_type=F32)
        hm2 = prev_ref[:, lo:lo + ffc]
        hm1 = prev_ref[:, D_FF + lo:D_FF + lo + ffc]
        cw = cw_ref[:, lo:lo + ffc]
        hc = cw[0:1] * hm2 + cw[1:2] * hm1 + cw[2:3] * h + cb_ref[:, lo:lo + ffc]
        a = (_gelu_tanh(hc) * g).astype(BF16)
        y_ref[...] += jnp.dot(a, wdown_ref[lo:lo + ffc, :], preferred_element_type=F32)
        cs_ref[:, lo:lo + ffc] = hm1
        cs_ref[:, D_FF + lo:D_FF + lo + ffc] = h


def _tail_sample(x, mixes, wouts, gain, prev, wup, wgate, cw, cb, wdown):
    n, d = x.shape
    full = lambda shape: _const_spec(shape)
    return pl.pallas_call(
        functools.partial(_tail_sample_body, n_mix=len(mixes), ffc=FFN_COL_CHUNK),
        grid=(1,),
        in_specs=[full((n, d))] + [full(m.shape) for m in mixes] + [full(w.shape) for w in wouts] + [
            full((1, d)), full((n, (CONV_W - 1) * D_FF)), full((d, D_FF)), full((d, D_FF)),
            full((CONV_W, D_FF)), full((1, D_FF)), full((D_FF, d))],
        out_specs=[pl.BlockSpec((n, d), lambda i: (0, 0)), pl.BlockSpec((n, (CONV_W - 1) * D_FF), lambda i: (0, 0))],
        out_shape=[jax.ShapeDtypeStruct((n, d), F32), jax.ShapeDtypeStruct((n, (CONV_W - 1) * D_FF), F32)],
        compiler_params=pltpu.CompilerParams(
            dimension_semantics=("arbitrary",), vmem_limit_bytes=V7X_VMEM_LIMIT_BYTES),
        name="tail_sample",
    )(x, *mixes, *wouts, gain, prev, wup, wgate, cw, cb, wdown)


def kernel(x_prompt, x_sample, cache_nsa_cmp, cache_nsa_sel, cache_nsa_win, state_s5_re, state_s5_im,
           cache_swa, state_ffn_conv, page_table, rel_bias, norm_mix, norm_ffn, a_w_in, a_w_out,
           nsa_q_gain, nsa_k_gain, nsa_cmp_pos, nsa_cmp_w1, nsa_cmp_w2, s5_a_re, s5_a_im, s5_log_dt,
           s5_b_re, s5_b_im, s5_c_re, s5_c_im, s5_d, s5_w_glu, s5_b_glu, c_w_in, c_w_out, c_q_gain,
           c_k_gain, c_sinks, ffn_w_up, ffn_w_gate, ffn_conv_w, ffn_conv_b, ffn_w_down):
    bp, tp, _ = x_prompt.shape
    bs, ts, _ = x_sample.shape
    assert ts == 1 and DEPTH == 2
    tab_a = rel_bias[:, :A_HEADS].reshape(NUM_BUCKETS, A_KV, A_REP)
    tab_c = rel_bias[:, :C_HEADS].reshape(NUM_BUCKETS, C_KV, C_REP)
    kv6 = lambda x: x.reshape(x.shape[:-1] + (2, x.shape[-1] // (2 * HEAD_DIM), HEAD_DIM))
    hp, hs = x_prompt, x_sample.reshape(1, bs, D_MODEL)
    conv_p, conv_s = [], []

    def tail(layer, hp, hs, mixes_p, mixes_s, wouts):
        wouts = [w.astype(BF16) for w in wouts]
        ffn = (norm_ffn[layer].reshape(1, D_MODEL),)
        wts = (ffn_w_up[layer].astype(BF16), ffn_w_gate[layer].astype(BF16), ffn_conv_w[layer],
               ffn_conv_b[layer].reshape(1, D_FF), ffn_w_down[layer].astype(BF16))
        hp, cp = _tail_prompt(hp, mixes_p, wouts, *ffn, jnp.zeros((bp, CONV_W - 1, D_FF), F32), *wts)
        hs2, cs = _tail_sample(hs[0], [m.reshape(bs, -1) for m in mixes_s], wouts, *ffn,
                               state_ffn_conv[layer].reshape(bs, (CONV_W - 1) * D_FF), *wts)
        conv_p.append(cp)
        conv_s.append(cs.reshape(bs, CONV_W - 1, D_FF))
        return hp, hs2.reshape(1, bs, D_MODEL)

    proj = (norm_mix[0], a_w_in[0], nsa_q_gain[0], nsa_k_gain[0])
    qp, cmp_p, sel_p, win_p, gate_p, up, kvt_p = _inproj_a(hp, *proj)
    qs, cmp_s, sel_s, win_s, gate_s, us, _ = _inproj_a(hs, *proj)
    cmp_w = _compress_weights(nsa_cmp_pos[0], nsa_cmp_w1[0], nsa_cmp_w2[0], nsa_k_gain[0])
    kc, vc = _cmp_prompt(cmp_p, cmp_w)
    o_ap = _nsa_prompt_attend(qp, gate_p, kc, vc, kvt_p, tab_a)
    o_as = _nsa_sample(qs.reshape(bs, A_KV, A_REP, HEAD_DIM), gate_s[0, :, :A_GATE].reshape(bs, A_KV, A_REP, 3),
                       sel_s.reshape(bs, 2, A_KV, HEAD_DIM), win_s.reshape(bs, 2, A_KV, HEAD_DIM),
                       cache_nsa_cmp[0], cache_nsa_sel[0], cache_nsa_win[0], page_table, cmp_w, tab_a)
    s5w = _s5_weights(s5_a_re[0], s5_a_im[0], s5_log_dt[0], s5_b_re[0], s5_b_im[0], s5_c_re[0], s5_c_im[0],
                      s5_d[0], s5_w_glu[0], s5_b_glu[0])
    h0 = jnp.zeros((bp, S5_NS), F32)
    o_bp, hr_p, hi_p = _s5_mix(up, h0, h0, s5w)
    o_bs, hr_s, hi_s = _s5_mix(us.reshape(bs, 1, B_WIDTH), state_s5_re[0].reshape(bs, S5_NS),
                               state_s5_im[0].reshape(bs, S5_NS), s5w)
    hp, hs = tail(0, hp, hs, [o_ap, o_bp], [o_as, o_bs], [a_w_out[0][:A_Q], a_w_out[0][A_Q:]])

    proj = (norm_mix[1], c_w_in[0], c_q_gain[0], c_k_gain[0])
    qcp, kv_p, kvt_c = _inproj_c(hp, *proj)
    qcs, kv_s, _ = _inproj_c(hs, *proj)
    o_cp = _swa_prompt(qcp, kvt_c, c_sinks[0], tab_c)
    new_kv = kv_s.reshape(bs, 2, C_KV, HEAD_DIM)
    o_cs = _swa_sample(qcs.reshape(bs, C_KV, C_REP, HEAD_DIM), cache_swa[0], jnp.transpose(new_kv, (0, 2, 1, 3)),
                       c_sinks[0], tab_c)
    hp, hs = tail(1, hp, hs, [o_cp], [o_cs], [c_w_out[0]])

    state = lambda x, n: x.reshape(1, n, B_GROUPS, B_STATE)
    nsa_win_s = jnp.concatenate([cache_nsa_win[0][:, 1:], kv6(win_s).reshape(bs, 1, 2, A_KV, HEAD_DIM)], axis=1)
    swa_s = jnp.concatenate([cache_swa[0][:, 1:], new_kv.reshape(bs, 1, 2, C_KV, HEAD_DIM)], axis=1)
    return (hp, hs.reshape(bs, ts, D_MODEL),
            kv6(cmp_p)[None], kv6(cmp_s).reshape(1, bs, ts, 2, A_KV, HEAD_DIM),
            kv6(sel_p)[None], kv6(sel_s).reshape(1, bs, ts, 2, A_KV, HEAD_DIM),
            kv6(win_p)[None, :, -min(WIN_A, tp):], nsa_win_s[None],
            state(hr_p, bp), state(hi_p, bp), state(hr_s, bs), state(hi_s, bs),
            kv6(kv_p)[None, :, -min(WIN_C, tp):], swa_s[None],
            jnp.stack(conv_p), jnp.stack(conv_s))
```

```python
import functools
import math

import jax
import jax.numpy as jnp
import numpy as np
from jax import lax
from jax.experimental import pallas as pl
from jax.experimental.pallas import tpu as pltpu

D_MODEL = 1024
DEPTH = 2
PAGE_SIZE = 128
HEAD_DIM = 64
A_HEADS = 8
A_KV = 2
A_REP = A_HEADS // A_KV
A_Q = A_HEADS * HEAD_DIM
A_KVW = A_KV * HEAD_DIM
A_GATE = 3 * A_HEADS
CMP_LEN = 32
CMP_STRIDE = 16
SEL_BLOCK = 64
SEL_TOPK = 16
WIN_A = 512
NSA_QBLK = 64
FORCE_BONUS = 1000.0
B_WIDTH = D_MODEL // 2
B_GROUP = 16
B_GROUPS = B_WIDTH // B_GROUP
B_STATE = 64
C_HEADS = D_MODEL // HEAD_DIM
C_KV = 2
C_REP = C_HEADS // C_KV
WIN_C = 128
NUM_BUCKETS = 32
MAX_DISTANCE = 128
D_FF = 2816
CONV_W = 3
EPS = 1e-6

F32 = jnp.float32
BF16 = jnp.bfloat16

V7X_VMEM_LIMIT_BYTES = 56 * 1024 * 1024


def _gelu_tanh(x):
    return 0.5 * x * (1.0 + jnp.tanh(math.sqrt(2.0 / math.pi) * (x + 0.044715 * (x * x * x))))


def _rms_rows(x, gain):
    return x * lax.rsqrt(jnp.mean(x * x, axis=-1, keepdims=True) + EPS) * gain


def _const_spec(shape):
    zeros = (0,) * len(shape)
    return pl.BlockSpec(shape, lambda *_: zeros, pipeline_mode=pl.Buffered(1))


NSA_Q_TILE = 128
NSA_FAR_TILE = 512
NSA_NEAR = 2 * NSA_Q_TILE
NEG_INF = float("-inf")


def _dot_nt(a, b):
    return lax.dot_general(a, b, (((1,), (1,)), ((), ())), preferred_element_type=F32)


def _softmax_start(s, v):
    m = jnp.max(s, axis=-1, keepdims=True)
    e = jnp.exp(s - m)
    return m, jnp.sum(e, axis=-1, keepdims=True), jnp.dot(e.astype(BF16), v, preferred_element_type=F32)


def _softmax_more(carry, s, v):
    m, l, acc = carry
    m_new = jnp.maximum(m, jnp.max(s, axis=-1, keepdims=True))
    alpha = jnp.exp(m - m_new)
    e = jnp.exp(s - m_new)
    return (m_new, alpha * l + jnp.sum(e, axis=-1, keepdims=True),
            alpha * acc + jnp.dot(e.astype(BF16), v, preferred_element_type=F32))


def _block_expand(k0, width):
    kpos = k0 + lax.broadcasted_iota(jnp.int32, (128, width), 1)
    blk = lax.broadcasted_iota(jnp.int32, (128, width), 0)
    return jnp.where((kpos >> 6) == blk, 1.0, 0.0).astype(BF16)


def _nsa_prompt_body(q_ref, gate_ref, kc_ref, vc_ref, ks_ref, vs_ref, kw_ref, vw_ref, btile_ref, cover_ref,
                     o_ref, *, tq):
    qt = pl.program_id(2)
    q0 = pl.multiple_of(qt * tq, tq)
    rows = A_REP * tq
    q = q_ref[...]
    qs = jnp.concatenate([q[:, r * HEAD_DIM:(r + 1) * HEAD_DIM] for r in range(A_REP)], axis=0)
    qs = (qs * (HEAD_DIM ** -0.5)).astype(BF16)

    def row_pos(width):
        return q0 + (lax.broadcasted_iota(jnp.int32, (rows, width), 0) & (tq - 1))

    def col_idx(width):
        return lax.broadcasted_iota(jnp.int32, (rows, width), 1)

    n_idx = col_idx(128)
    valid_c = (n_idx * CMP_STRIDE + (CMP_LEN - 1) <= row_pos(128)) & (n_idx < 127)
    s_c = jnp.where(valid_c, _dot_nt(qs, kc_ref[...]), NEG_INF)
    m_c = jnp.max(s_c, axis=-1, keepdims=True)
    m_c = jnp.where(m_c == NEG_INF, 0.0, m_c)
    e_c = jnp.exp(s_c - m_c)
    d_c = jnp.sum(e_c, axis=-1, keepdims=True)
    p_c = (e_c / jnp.where(d_c > 0, d_c, 1.0)).astype(BF16)
    o_c = jnp.dot(p_c, vc_ref[...], preferred_element_type=F32)
    p_heads = jnp.concatenate([p_c[r * tq:(r + 1) * tq] for r in range(A_REP)], axis=1)
    imp = _dot_nt(cover_ref[...], p_heads)[0:32]
    s_idx = lax.broadcasted_iota(jnp.int32, (32, tq), 0)
    qblk = (q0 + lax.broadcasted_iota(jnp.int32, (32, tq), 1)) >> 6
    forced = (s_idx == 0) | (s_idx == qblk) | (s_idx == qblk - 1)
    allowed = s_idx <= qblk
    score = jnp.where(allowed, imp + jnp.where(forced, FORCE_BONUS, 0.0), NEG_INF)
    rank = jnp.zeros((32, tq), F32)
    for j in range(32):
        other = score[j:j + 1, :]
        beats = (other > score) | ((other == score) & (s_idx > j))
        rank = rank + jnp.where(beats, 1.0, 0.0)
    sel = jnp.where((rank < SEL_TOPK) & allowed, 1.0, 0.0)
    sel = jnp.concatenate([sel, jnp.zeros((128 - 32, tq), F32)], axis=0).T.astype(BF16)
    sel = jnp.concatenate([sel] * A_REP, axis=0)

    prev0 = pl.multiple_of(jnp.maximum(q0 - tq, 0), tq)
    btile = btile_ref[...]

    def near(k_ref, v_ref, extra_mask):
        k = jnp.concatenate([k_ref[pl.ds(prev0, tq), :], k_ref[pl.ds(q0, tq), :]], axis=0)
        v = jnp.concatenate([v_ref[pl.ds(prev0, tq), :], v_ref[pl.ds(q0, tq), :]], axis=0)
        s = _dot_nt(qs, k) + btile
        return _softmax_start(s if extra_mask is None else jnp.where(extra_mask, s, NEG_INF), v)

    expand_near = jnp.concatenate([_block_expand(prev0, tq), _block_expand(q0, tq)], axis=1)
    sel_near = jnp.dot(sel, expand_near, preferred_element_type=F32) > 0.5

    far_end = q0 - tq

    def sel_far(i, carry):
        k0 = pl.multiple_of(i * NSA_FAR_TILE, NSA_FAR_TILE)
        s = _dot_nt(qs, ks_ref[pl.ds(k0, NSA_FAR_TILE), :])
        hit = jnp.dot(sel, _block_expand(k0, NSA_FAR_TILE), preferred_element_type=F32) > 0.5
        mask = hit & (k0 + col_idx(NSA_FAR_TILE) < far_end)
        return _softmax_more(carry, jnp.where(mask, s, NEG_INF), vs_ref[pl.ds(k0, NSA_FAR_TILE), :])

    n_far = (jnp.maximum(far_end, 0) + NSA_FAR_TILE - 1) // NSA_FAR_TILE
    _, l_s, acc_s = lax.fori_loop(0, n_far, sel_far, near(ks_ref, vs_ref, sel_near))
    o_s = acc_s / l_s

    w_far = WIN_A - tq
    wf0 = pl.multiple_of(jnp.maximum(q0 - WIN_A, 0), tq)
    wpos = wf0 + col_idx(w_far)
    wmask = (row_pos(w_far) - wpos < WIN_A) & (wpos < far_end)
    s_w = jnp.where(wmask, _dot_nt(qs, kw_ref[pl.ds(wf0, w_far), :]), NEG_INF)
    _, l_w, acc_w = _softmax_more(near(kw_ref, vw_ref, None), s_w, vw_ref[pl.ds(wf0, w_far), :])
    o_w = acc_w / l_w

    gates = gate_ref[...]
    first_group = pl.program_id(1) == 0
    outs = []
    for r in range(A_REP):
        sl = slice(r * tq, (r + 1) * tq)
        gc, gs, gw = (jnp.where(first_group, gates[:, 3 * r + c:3 * r + c + 1],
                                gates[:, 3 * (A_REP + r) + c:3 * (A_REP + r) + c + 1]) for c in range(3))
        outs.append(gc * o_c[sl] + gs * o_s[sl] + gw * o_w[sl])
    o_ref[...] = jnp.concatenate(outs, axis=1)


def _bucket_np(dist):
    n = np.maximum(dist, 0)
    exact = NUM_BUCKETS // 2
    nf = np.maximum(n, exact).astype(np.float64)
    large = exact + (np.log(nf / exact) / math.log(MAX_DISTANCE / exact) * (NUM_BUCKETS - exact)).astype(np.int64)
    return np.where(n < exact, n, np.minimum(large, NUM_BUCKETS - 1)).astype(np.int32)


def _bias_lookup(tab, dist):
    bucket = _bucket_np(np.asarray(dist))
    onehot = (jnp.asarray(bucket.reshape(-1, 1)) == jnp.arange(NUM_BUCKETS)[None, :]).astype(F32)
    flat = jnp.dot(onehot, tab.reshape(NUM_BUCKETS, -1), precision=lax.Precision.HIGHEST)
    return flat.reshape(bucket.shape + tab.shape[1:])


def _near_bias_tiles(tab, tq, window):
    i = np.arange(tq)[:, None]
    j = np.arange(2 * tq)[None, :]
    dist = tq + i - j
    ok = (dist >= 0) & (dist < window)
    bias = jnp.transpose(_bias_lookup(tab, dist), (2, 3, 0, 1))
    tiles = jnp.stack([jnp.where(jnp.asarray(ok & (j >= tq)), bias, NEG_INF), jnp.where(jnp.asarray(ok), bias, NEG_INF)])
    return tiles.reshape(2, tab.shape[1], tab.shape[2] * tq, 2 * tq)


def _cover_matrix(n_cmp_pad, n_sel):
    n = np.arange(n_cmp_pad)
    c_start = n * CMP_STRIDE
    c_end = c_start + CMP_LEN - 1
    s_start = np.arange(128) * SEL_BLOCK
    cover = (c_start[:, None] < s_start[None, :] + SEL_BLOCK) & (c_end[:, None] >= s_start[None, :])
    cover &= (np.arange(128)[None, :] < n_sel)
    return np.tile(cover.astype(np.float32), (A_REP, 1))


def _nsa_prompt_attend(q, gates, kc, vc, kvt, tab):
    b, t, _ = q.shape
    tq = NSA_Q_TILE
    assert t % NSA_FAR_TILE == 0 and t // SEL_BLOCK <= 32 and t >= WIN_A and kc.shape[2] == 128
    btile = _near_bias_tiles(tab - tab[NUM_BUCKETS - 1], tq, 2 * tq)
    cover = jnp.asarray(_cover_matrix(128, t // SEL_BLOCK).T, BF16)
    cmp_spec = pl.BlockSpec((None, None, 128, HEAD_DIM), lambda i, g, j: (i, g, 0, 0))
    kvt_spec = lambda slot: pl.BlockSpec((None, None, t, HEAD_DIM), lambda i, g, j: (i, slot + g, 0, 0))
    return pl.pallas_call(
        functools.partial(_nsa_prompt_body, tq=tq),
        grid=(b, A_KV, t // tq),
        in_specs=[
            pl.BlockSpec((None, tq, A_REP * HEAD_DIM), lambda i, g, j: (i, j, g)),
            pl.BlockSpec((None, tq, 128), lambda i, g, j: (i, j, 0)),
            cmp_spec, cmp_spec, kvt_spec(0), kvt_spec(2), kvt_spec(4), kvt_spec(6),
            pl.BlockSpec((None, None, A_REP * tq, 2 * tq), lambda i, g, j: (jnp.minimum(j, 1), g, 0, 0)),
            pl.BlockSpec((128, A_REP * 128), lambda i, g, j: (0, 0)),
        ],
        out_specs=pl.BlockSpec((None, tq, A_REP * HEAD_DIM), lambda i, g, j: (i, j, g)),
        out_shape=jax.ShapeDtypeStruct((b, t, A_Q), F32),
        compiler_params=pltpu.CompilerParams(
            dimension_semantics=("parallel", "parallel", "arbitrary"),
            vmem_limit_bytes=V7X_VMEM_LIMIT_BYTES),
        name="nsa_prompt",
    )(q, gates, kc, vc, kvt, kvt, kvt, kvt, btile, cover)


S5_NS = B_GROUPS * B_STATE
S5_T_CHUNK = 64
S5_STRIP = 512


def _s5_body(u_ref, h0r_ref, h0i_ref, ar_ref, ai_ref, ldt_ref, wb_ref, wc_ref, d_ref, wglu_ref, bglu_ref,
             o_ref, hr_ref, hi_ref, coef_ref, st_ref, xbuf_ref, ubuf_ref, obuf_ref, *, nb, steps, interleave):
    c = pl.program_id(0)

    @pl.when(c == 0)
    def _():
        dt = jnp.exp(ldt_ref[...])
        ar, ai = ar_ref[...], ai_ref[...]
        mag = jnp.exp(ar * dt)
        abr, abi = mag * jnp.cos(ai * dt), mag * jnp.sin(ai * dt)
        den = ar * ar + ai * ai
        wr = ((abr - 1.0) * ar + abi * ai) / den
        wi = (abi * ar - (abr - 1.0) * ai) / den
        for k, val in enumerate((abr, abi, wr, wi)):
            coef_ref[k] = jnp.broadcast_to(val, (nb, S5_NS))
        st_ref[0] = h0r_ref[...]
        st_ref[1] = h0i_ref[...]

    if interleave:
        for b in range(nb):
            for j in range(B_WIDTH // 128):
                ubuf_ref.at[j][pl.ds(b, steps, stride=nb), :] = u_ref[b, :, j * 128:(j + 1) * 128]
        u = jnp.concatenate([ubuf_ref[j] for j in range(B_WIDTH // 128)], axis=1)
    else:
        u = u_ref[...]
    xbuf_ref[...] = jnp.dot(u.astype(BF16), wb_ref[...], preferred_element_type=F32)

    for lo in range(0, S5_NS, S5_STRIP):
        re = slice(lo, lo + S5_STRIP)
        im = slice(S5_NS + lo, S5_NS + lo + S5_STRIP)
        abr, abi, wr, wi = (coef_ref[k, :, re] for k in range(4))

        def step(t, carry):
            sr, si = carry
            r0 = pl.multiple_of(t * nb, nb)
            bur = xbuf_ref[pl.ds(r0, nb), re]
            bui = xbuf_ref[pl.ds(r0, nb), im]
            nsr = abr * sr - abi * si + (wr * bur - wi * bui)
            nsi = abr * si + abi * sr + (wr * bui + wi * bur)
            xbuf_ref[pl.ds(r0, nb), re] = nsr
            xbuf_ref[pl.ds(r0, nb), im] = nsi
            return nsr, nsi

        sr, si = lax.fori_loop(0, steps, step, (st_ref[0, :, re], st_ref[1, :, re]),
                               unroll=min(steps, 8))
        st_ref[0, :, re] = sr
        st_ref[1, :, re] = si

    y = jnp.dot(xbuf_ref[...].astype(BF16), wc_ref[...], preferred_element_type=F32) + d_ref[...] * u
    z = _gelu_tanh(y)
    gate = jnp.dot(z.astype(BF16), wglu_ref[...], preferred_element_type=F32) + bglu_ref[...]
    out = z * (1.0 / (1.0 + jnp.exp(-gate)))
    if interleave:
        for j in range(B_WIDTH // 128):
            obuf_ref[j] = out[:, j * 128:(j + 1) * 128]
        for b in range(nb):
            for j in range(B_WIDTH // 128):
                o_ref[b, :, j * 128:(j + 1) * 128] = obuf_ref.at[j][pl.ds(b, steps, stride=nb), :]
    else:
        o_ref[...] = out
    hr_ref[...] = st_ref[0]
    hi_ref[...] = st_ref[1]


def _s5_weights(a_re, a_im, log_dt, b_re, b_im, c_re, c_im, d, w_glu, b_glu):
    eye = jnp.eye(B_GROUPS, dtype=F32)
    blk_in = lambda w: jnp.einsum('hg,gpc->hcgp', eye, w).reshape(B_WIDTH, S5_NS)
    blk_out = lambda w: jnp.einsum('gh,gcp->gphc', eye, w).reshape(S5_NS, B_WIDTH)
    wb = jnp.concatenate([blk_in(b_re), blk_in(b_im)], axis=1).astype(BF16)
    wc = jnp.concatenate([blk_out(c_re), -blk_out(c_im)], axis=0).astype(BF16)
    flat = lambda x: x.reshape(1, S5_NS)
    return (flat(a_re), flat(a_im), flat(jnp.repeat(log_dt, B_STATE)), wb, wc, d.reshape(1, B_WIDTH),
            w_glu.astype(BF16), b_glu.reshape(1, B_WIDTH))


def _s5_mix(u, h_re, h_im, weights):
    nb, t, _ = u.shape
    interleave = t > 1
    steps = min(t, S5_T_CHUNK)
    rows = nb * steps
    body = functools.partial(_s5_body, nb=nb, steps=steps, interleave=interleave)
    if interleave:
        u_in = u
        u_spec = pl.BlockSpec((nb, steps, B_WIDTH), lambda c: (0, c, 0))
        o_shape = jax.ShapeDtypeStruct((nb, t, B_WIDTH), F32)
        scratch_rows = rows
    else:
        u_in = u.reshape(nb, B_WIDTH)
        u_spec = pl.BlockSpec((nb, B_WIDTH), lambda c: (0, 0))
        o_shape = jax.ShapeDtypeStruct((nb, B_WIDTH), F32)
        scratch_rows = 8
    o, hr, hi = pl.pallas_call(
        body,
        grid=(t // steps,),
        in_specs=[
            u_spec, _const_spec((nb, S5_NS)), _const_spec((nb, S5_NS)),
            _const_spec((1, S5_NS)), _const_spec((1, S5_NS)), _const_spec((1, S5_NS)),
            _const_spec((B_WIDTH, 2 * S5_NS)), _const_spec((2 * S5_NS, B_WIDTH)),
            _const_spec((1, B_WIDTH)), _const_spec((B_WIDTH, B_WIDTH)), _const_spec((1, B_WIDTH)),
        ],
        out_specs=[u_spec, pl.BlockSpec((nb, S5_NS), lambda c: (0, 0)), pl.BlockSpec((nb, S5_NS), lambda c: (0, 0))],
        out_shape=[o_shape, jax.ShapeDtypeStruct((nb, S5_NS), F32), jax.ShapeDtypeStruct((nb, S5_NS), F32)],
        scratch_shapes=[
            pltpu.VMEM((4, nb, S5_NS), F32),
            pltpu.VMEM((2, nb, S5_NS), F32),
            pltpu.VMEM((rows, 2 * S5_NS), F32),
            pltpu.VMEM((B_WIDTH // 128, scratch_rows, 128), F32),
            pltpu.VMEM((B_WIDTH // 128, scratch_rows, 128), F32),
        ],
        compiler_params=pltpu.CompilerParams(
            dimension_semantics=("arbitrary",),
            vmem_limit_bytes=V7X_VMEM_LIMIT_BYTES),
        name="s5_mix",
    )(u_in, h_re, h_im, *weights)
    return o.reshape(nb, t, B_WIDTH), hr, hi


CHUNK_W = CMP_STRIDE * 2 * A_KVW
CHUNKS_PER_PAGE = PAGE_SIZE // CMP_STRIDE
SEL_PAD = 256


def _bf16_round(x):
    return x.astype(BF16).astype(F32)


def _compress_tail(c, pos, w2):
    c1 = c[:, 128:]
    nxt = jnp.concatenate([c1[1:], c1[:1]], axis=0)
    hid = (pos + c[:, :128]) + nxt
    return jnp.dot(_gelu_tanh(hid).astype(BF16), w2, preferred_element_type=F32)


def _rms_heads128(x, gain):
    left = lax.broadcasted_iota(jnp.int32, x.shape, 1) < HEAD_DIM
    sq = x * x
    s0 = jnp.sum(jnp.where(left, sq, 0.0), axis=-1, keepdims=True)
    s1 = jnp.sum(jnp.where(left, 0.0, sq), axis=-1, keepdims=True)
    ms = jnp.where(left, s0, s1) * (1.0 / HEAD_DIM)
    return x * lax.rsqrt(ms + EPS) * gain


def _pad_rows8(x):
    return jnp.concatenate([x, jnp.zeros((8 - x.shape[0], x.shape[1]), x.dtype)], axis=0)


def _compress_rows(rows_scr, kv, n_chunks, w_ref, pos, w2_ref):
    acc = jnp.zeros((n_chunks, 2 * A_KVW), F32)
    for sp in range(CMP_STRIDE // 2):
        x = jnp.concatenate([rows_scr.at[kv][pl.ds(2 * sp + e, n_chunks, stride=CMP_STRIDE), :] for e in range(2)],
                            axis=1).astype(BF16)
        acc = acc + jnp.dot(x, w_ref[sp], preferred_element_type=F32)
    return _compress_tail(acc, pos, w2_ref[...])


def _compress_weights(cmp_pos, cmp_w1, cmp_w2, k_gain):
    eye = jnp.eye(A_KV, dtype=F32)
    w1 = cmp_w1.reshape(2, 2, CMP_STRIDE, HEAD_DIM, HEAD_DIM)
    w_big = jnp.einsum('ajsdh,gk->asgdjkh', w1, eye).reshape(2, CMP_STRIDE // 2, 2 * A_KVW, 2 * A_KVW).astype(BF16)
    w_pos = jnp.concatenate([cmp_w1, cmp_w1], axis=-1).astype(BF16)
    pos = jnp.broadcast_to(cmp_pos.reshape(2, 1, CMP_LEN * HEAD_DIM), (2, 8, CMP_LEN * HEAD_DIM)).astype(BF16)
    w2_big = jnp.einsum('ahd,gk->aghkd', cmp_w2, eye).reshape(2, A_KVW, A_KVW).astype(BF16)
    gain2 = jnp.tile(k_gain, A_KV).reshape(1, A_KVW)
    return (pos[0], pos[1], w_pos[0], w_pos[1], w_big[0], w_big[1], w2_big[0], w2_big[1], gain2)


def _compress_specs(full):
    mlp_w = (CMP_STRIDE // 2, 2 * A_KVW, 2 * A_KVW)
    return [full((8, CMP_LEN * HEAD_DIM)), full((8, CMP_LEN * HEAD_DIM)),
            full((CMP_LEN * HEAD_DIM, A_KVW)), full((CMP_LEN * HEAD_DIM, A_KVW)),
            full(mlp_w), full(mlp_w), full((A_KVW, A_KVW)), full((A_KVW, A_KVW)), full((1, A_KVW))]


def _pos_terms(pos_scr, posk_ref, posv_ref, wpos_k_ref, wpos_v_ref):
    pos_scr[0] = jnp.dot(posk_ref[...], wpos_k_ref[...], preferred_element_type=F32)
    pos_scr[1] = jnp.dot(posv_ref[...], wpos_v_ref[...], preferred_element_type=F32)


def _cmp_prompt_body(cmp_ref, posk_ref, posv_ref, wpos_k_ref, wpos_v_ref, wk_ref, wv_ref, w2k_ref, w2v_ref,
                     kgain_ref, kc_ref, vc_ref, pos_scr, rows_scr, *, n_chunks):
    @pl.when(pl.program_id(0) == 0)
    def _():
        _pos_terms(pos_scr, posk_ref, posv_ref, wpos_k_ref, wpos_v_ref)

    rows_scr[0] = cmp_ref[:, :A_KVW]
    rows_scr[1] = cmp_ref[:, A_KVW:]
    kc = _rms_heads128(_compress_rows(rows_scr, 0, n_chunks, wk_ref, pos_scr[0, 0:1], w2k_ref),
                       kgain_ref[...]).astype(BF16)
    vc = _compress_rows(rows_scr, 1, n_chunks, wv_ref, pos_scr[1, 0:1], w2v_ref).astype(BF16)
    for g in range(A_KV):
        kc_ref[g] = kc[:, g * HEAD_DIM:(g + 1) * HEAD_DIM]
        vc_ref[g] = vc[:, g * HEAD_DIM:(g + 1) * HEAD_DIM]


def _cmp_prompt(cmp_rows, cmp_weights):
    b, t, _ = cmp_rows.shape
    n_chunks = t // CMP_STRIDE
    full = lambda shape: pl.BlockSpec(shape, lambda i: (0,) * len(shape))
    out_spec = pl.BlockSpec((None, A_KV, n_chunks, HEAD_DIM), lambda i: (i, 0, 0, 0))
    out_shape = jax.ShapeDtypeStruct((b, A_KV, n_chunks, HEAD_DIM), BF16)
    return pl.pallas_call(
        functools.partial(_cmp_prompt_body, n_chunks=n_chunks),
        grid=(b,),
        in_specs=[pl.BlockSpec((None, t, 2 * A_KVW), lambda i: (i, 0, 0))] + _compress_specs(full),
        out_specs=[out_spec, out_spec],
        out_shape=[out_shape, out_shape],
        scratch_shapes=[pltpu.VMEM((2, 8, A_KVW), F32), pltpu.VMEM((2, t, A_KVW), F32)],
        compiler_params=pltpu.CompilerParams(
            dimension_semantics=("arbitrary",), vmem_limit_bytes=V7X_VMEM_LIMIT_BYTES),
        name="nsa_cmp_prompt",
    )(cmp_rows, *cmp_weights)


def _nsa_sample_cmp_body(pt_ref, *refs, n_pages, n_sel):
    page_refs = refs[:n_pages]
    (q_ref, posk_ref, posv_ref, wpos_k_ref, wpos_v_ref, wk_ref, wv_ref, w2k_ref, w2v_ref, kgain_ref, cover_ref,
     oc_ref, idx_ref, pos_scr, rows_scr) = refs[n_pages:]

    @pl.when(pl.program_id(0) == 0)
    def _():
        _pos_terms(pos_scr, posk_ref, posv_ref, wpos_k_ref, wpos_v_ref)

    n_chunks = n_pages * CHUNKS_PER_PAGE
    n_cmp = n_chunks - 1
    for kv in range(2):
        for p in range(n_pages):
            rows_scr[kv, p * PAGE_SIZE:(p + 1) * PAGE_SIZE, :] = page_refs[p][kv].reshape(A_KVW, PAGE_SIZE).T

    kc = _rms_heads128(_compress_rows(rows_scr, 0, n_chunks, wk_ref, pos_scr[0, 0:1], w2k_ref),
                       kgain_ref[...]).astype(BF16)
    vc = _compress_rows(rows_scr, 1, n_chunks, wv_ref, pos_scr[1, 0:1], w2v_ref).astype(BF16)

    col = lax.broadcasted_iota(jnp.int32, (8, n_chunks), 1)
    s_idx = lax.broadcasted_iota(jnp.int32, (1, SEL_PAD), 1)
    s_idx_f = s_idx.astype(F32)
    forced = (s_idx == 0) | (s_idx == n_sel - 1) | (s_idx == n_sel - 2)
    lane = lax.broadcasted_iota(jnp.int32, (1, 128), 1)
    for g in range(A_KV):
        hs = slice(g * HEAD_DIM, (g + 1) * HEAD_DIM)
        q8 = (_pad_rows8(q_ref[g]) * (HEAD_DIM ** -0.5)).astype(BF16)
        s = jnp.where(col < n_cmp, _dot_nt(q8, kc[:, hs]), NEG_INF)
        e = jnp.exp(s - jnp.max(s, axis=-1, keepdims=True))
        p = (e / jnp.sum(e, axis=-1, keepdims=True)).astype(BF16)
        oc_ref[g] = jnp.dot(p, vc[:, hs], preferred_element_type=F32)[0:A_REP]
        imp = jnp.sum(jnp.dot(p, cover_ref[...], preferred_element_type=F32)[0:A_REP], axis=0, keepdims=True)
        score = jnp.where(s_idx < n_sel, imp + jnp.where(forced, FORCE_BONUS, 0.0), NEG_INF)
        rank = jnp.zeros((1, SEL_PAD), F32)
        for j in range(n_sel):
            cj = score[:, j:j + 1]
            beats = (cj > score) | ((cj == score) & (s_idx > j))
            rank = rank + jnp.where(beats, 1.0, 0.0)
        picks = jnp.zeros((1, 128), F32)
        for r in range(SEL_TOPK):
            block = jnp.sum(jnp.where(rank == float(r), s_idx_f, 0.0), axis=-1, keepdims=True)
            picks = picks + jnp.where(lane == r, block, 0.0)
        idx_ref[g:g + 1, :] = picks.astype(jnp.int32)


def _nsa_sample_att_body(idx_ref, pt_ref, *refs, n_past_blk):
    page_refs = refs[:SEL_TOPK]
    (q_ref, gate_ref, oc_ref, newsel_ref, wcache_ref, newwin_ref, selb_ref, winb_ref, newb_ref, o_ref) = refs[SEL_TOPK:]
    b, g = pl.program_id(0), pl.program_id(1)
    blocks_per_page = PAGE_SIZE // SEL_BLOCK
    q8 = (_pad_rows8(q_ref[...]) * (HEAD_DIM ** -0.5)).astype(BF16)
    new_bias = newb_ref[:, 0:1]

    def attend(s, v_t, new_kv):
        k_new = _bf16_round(new_kv[0:1])
        v_new = _bf16_round(new_kv[1:2])
        s_new = jnp.sum(q8.astype(F32) * k_new, axis=-1, keepdims=True) + new_bias
        m = jnp.maximum(jnp.max(s, axis=-1, keepdims=True), s_new)
        e = jnp.exp(s - m)
        e_new = jnp.exp(s_new - m)
        den = jnp.sum(e, axis=-1, keepdims=True) + e_new
        return _dot_nt((e / den).astype(BF16), v_t) + _bf16_round(e_new / den) * v_new

    lane = lax.broadcasted_iota(jnp.int32, (8, PAGE_SIZE), 1)
    near = selb_ref[...]
    scores = []
    for k in range(SEL_TOPK):
        ik = idx_ref[(b * A_KV + g) * SEL_TOPK + k]
        blk = jnp.minimum(ik, n_past_blk - 1)
        s_k = jnp.dot(q8, page_refs[k][0].astype(BF16), preferred_element_type=F32)
        s_k = s_k + jnp.where(blk // blocks_per_page == (n_past_blk - 1) // blocks_per_page, near, 0.0)
        keep = (lane // SEL_BLOCK == blk % blocks_per_page) & (ik < n_past_blk)
        scores.append(jnp.where(keep, s_k, NEG_INF))
    v_sel = jnp.concatenate([page_refs[k][1].astype(BF16) for k in range(SEL_TOPK)], axis=1)
    o_s = attend(jnp.concatenate(scores, axis=1), v_sel, newsel_ref[...])

    w_len = wcache_ref.shape[-1]
    wcol = lax.broadcasted_iota(jnp.int32, (8, w_len), 1)
    s_w = jnp.dot(q8, wcache_ref[0].astype(BF16), preferred_element_type=F32) + winb_ref[...]
    o_w = attend(jnp.where(wcol >= 1, s_w, NEG_INF), wcache_ref[1].astype(BF16), newwin_ref[...])
    gates = gate_ref[...]
    o_ref[...] = gates[:, 0:1] * oc_ref[...] + gates[:, 1:2] * o_s[0:A_REP] + gates[:, 2:3] * o_w[0:A_REP]


def _rows_last(cache):
    nd = cache.ndim
    return jnp.transpose(cache, tuple(range(nd - 4)) + (nd - 3, nd - 2, nd - 1, nd - 4))


def _nsa_sample(q, gates, new_sel, new_win, pool_cmp, pool_sel, win_cache, page_table, cmp_weights, tab):
    n, n_pages = page_table.shape
    past = n_pages * PAGE_SIZE
    n_past_blk = past // SEL_BLOCK
    n_sel = n_past_blk + 1
    n_chunks = n_pages * CHUNKS_PER_PAGE
    w_len = win_cache.shape[1]
    blocks_per_page = PAGE_SIZE // SEL_BLOCK
    assert n_sel <= SEL_PAD and w_len == WIN_A and past >= WIN_A and blocks_per_page == 2
    pt_flat = page_table.reshape(-1)

    c_idx = np.arange(n_chunks)
    s_start = np.arange(SEL_PAD) * SEL_BLOCK
    cover = ((c_idx[:, None] * CMP_STRIDE < s_start[None, :] + SEL_BLOCK)
             & (c_idx[:, None] * CMP_STRIDE + CMP_LEN - 1 >= s_start[None, :])
             & (c_idx[:, None] < n_chunks - 1) & (np.arange(SEL_PAD)[None, :] < n_sel))
    cover = jnp.asarray(cover.astype(np.float32), BF16)

    head_spec = lambda last: pl.BlockSpec((None, A_KV, A_REP, last), lambda i, *_: (i, 0, 0, 0))
    page_specs = [pl.BlockSpec((None, 2, A_KV, HEAD_DIM, PAGE_SIZE),
                               functools.partial(lambda i, pt, p: (pt[i * n_pages + p], 0, 0, 0, 0), p=p))
                  for p in range(n_pages)]
    full = lambda shape: pl.BlockSpec(shape, lambda i, *_: (0,) * len(shape))
    pool_c = _rows_last(pool_cmp)
    o_c, idx = pl.pallas_call(
        functools.partial(_nsa_sample_cmp_body, n_pages=n_pages, n_sel=n_sel),
        grid_spec=pltpu.PrefetchScalarGridSpec(
            num_scalar_prefetch=1,
            grid=(n,),
            in_specs=page_specs + [head_spec(HEAD_DIM)] + _compress_specs(full) + [full((n_chunks, SEL_PAD))],
            out_specs=[head_spec(HEAD_DIM), pl.BlockSpec((None, A_KV, 128), lambda i, *_: (i, 0, 0))],
            scratch_shapes=[pltpu.VMEM((2, 8, A_KVW), F32), pltpu.VMEM((2, past, A_KVW), F32)],
        ),
        out_shape=[jax.ShapeDtypeStruct((n, A_KV, A_REP, HEAD_DIM), F32),
                   jax.ShapeDtypeStruct((n, A_KV, 128), jnp.int32)],
        compiler_params=pltpu.CompilerParams(
            dimension_semantics=("arbitrary",), vmem_limit_bytes=V7X_VMEM_LIMIT_BYTES),
        name="nsa_sample_cmp",
    )(pt_flat, *([pool_c] * n_pages), q, *cmp_weights, cover)

    rel = lambda dist: jnp.pad(jnp.transpose(_bias_lookup(tab - tab[NUM_BUCKETS - 1], dist), (1, 2, 0)),
                               ((0, 0), (0, 8 - A_REP), (0, 0)))
    sel_bias = rel(past - (past - PAGE_SIZE + np.arange(PAGE_SIZE)))
    win_bias = rel(w_len - np.arange(w_len))
    new_bias = rel(np.zeros(128, np.int64))

    idx_flat = idx[:, :, :SEL_TOPK].reshape(-1)

    def page_map(i, g, idx_s, pt, slot):
        blk = jnp.minimum(idx_s[(i * A_KV + g) * SEL_TOPK + slot], n_past_blk - 1)
        return (pt[i * n_pages + blk // blocks_per_page], 0, g, 0, 0)

    slot_specs = [pl.BlockSpec((None, 2, None, HEAD_DIM, PAGE_SIZE), functools.partial(page_map, slot=s))
                  for s in range(SEL_TOPK)]
    per_head = lambda rows, last: pl.BlockSpec((None, None, rows, last), lambda i, g, *_: (i, g, 0, 0))
    per_group = lambda last: pl.BlockSpec((None, 8, last), lambda i, g, *_: (g, 0, 0))
    return pl.pallas_call(
        functools.partial(_nsa_sample_att_body, n_past_blk=n_past_blk),
        grid_spec=pltpu.PrefetchScalarGridSpec(
            num_scalar_prefetch=2,
            grid=(n, A_KV),
            in_specs=slot_specs + [
                per_head(A_REP, HEAD_DIM), per_head(A_REP, 3), per_head(A_REP, HEAD_DIM), per_head(2, HEAD_DIM),
                pl.BlockSpec((None, 2, None, HEAD_DIM, w_len), lambda i, g, *_: (i, 0, g, 0, 0)),
                per_head(2, HEAD_DIM), per_group(PAGE_SIZE), per_group(w_len), per_group(128),
            ],
            out_specs=per_head(A_REP, HEAD_DIM),
        ),
        out_shape=jax.ShapeDtypeStruct((n, A_KV, A_REP, HEAD_DIM), F32),
        compiler_params=pltpu.CompilerParams(
            dimension_semantics=("arbitrary", "arbitrary"), vmem_limit_bytes=V7X_VMEM_LIMIT_BYTES),
        name="nsa_sample_att",
    )(idx_flat, pt_flat, *([_rows_last(pool_sel)] * SEL_TOPK), q, gates, o_c,
      jnp.transpose(new_sel, (0, 2, 1, 3)), _rows_last(win_cache), jnp.transpose(new_win, (0, 2, 1, 3)),
      sel_bias, win_bias, new_bias)


SWA_Q_TILE = WIN_C


def _swa_prompt_body(q_ref, k_ref, v_ref, btile_ref, sink_ref, o_ref, *, tq):
    qt = pl.program_id(2)
    q0 = pl.multiple_of(qt * tq, tq)
    rows = C_REP * tq
    q = q_ref[...]
    qs = jnp.concatenate([q[:, r * HEAD_DIM:(r + 1) * HEAD_DIM] for r in range(C_REP)], axis=0)
    qs = (qs * (HEAD_DIM ** -0.5)).astype(BF16)
    prev0 = pl.multiple_of(jnp.maximum(q0 - tq, 0), tq)
    k = jnp.concatenate([k_ref[pl.ds(prev0, tq), :], k_ref[pl.ds(q0, tq), :]], axis=0)
    v = jnp.concatenate([v_ref[pl.ds(prev0, tq), :], v_ref[pl.ds(q0, tq), :]], axis=0)
    s = _dot_nt(qs, k) + btile_ref[...]
    sinks = sink_ref[...]
    sink = jnp.concatenate([jnp.broadcast_to(sinks[r:r + 1, 0:1], (tq, 1)) for r in range(C_REP)], axis=0)
    m = jnp.maximum(jnp.max(s, axis=-1, keepdims=True), sink)
    e = jnp.exp(s - m)
    p = e / (jnp.sum(e, axis=-1, keepdims=True) + jnp.exp(sink - m))
    o = jnp.dot(p.astype(BF16), v, preferred_element_type=F32)
    o_ref[...] = jnp.concatenate([o[r * tq:(r + 1) * tq] for r in range(C_REP)], axis=1)


def _swa_prompt(q, kvt, sinks, tab):
    b, t, _ = q.shape
    tq = SWA_Q_TILE
    btile = _near_bias_tiles(tab, tq, WIN_C)
    kv_spec = lambda slot: pl.BlockSpec((None, None, t, HEAD_DIM), lambda i, g, j: (i, slot + g, 0, 0))
    qo_spec = pl.BlockSpec((None, tq, C_REP * HEAD_DIM), lambda i, g, j: (i, j, g))
    sink_lanes = jnp.broadcast_to(sinks.reshape(C_KV, C_REP, 1), (C_KV, C_REP, 128))
    return pl.pallas_call(
        functools.partial(_swa_prompt_body, tq=tq),
        grid=(b, C_KV, t // tq),
        in_specs=[qo_spec, kv_spec(0), kv_spec(C_KV),
                  pl.BlockSpec((None, None, C_REP * tq, 2 * tq), lambda i, g, j: (jnp.minimum(j, 1), g, 0, 0)),
                  pl.BlockSpec((None, C_REP, 128), lambda i, g, j: (g, 0, 0))],
        out_specs=qo_spec,
        out_shape=jax.ShapeDtypeStruct((b, t, C_HEADS * HEAD_DIM), F32),
        compiler_params=pltpu.CompilerParams(
            dimension_semantics=("parallel", "parallel", "arbitrary"), vmem_limit_bytes=V7X_VMEM_LIMIT_BYTES),
        name="swa_prompt",
    )(q, kvt, kvt, btile, sink_lanes)


def _swa_sample_body(q_ref, cache_ref, new_ref, bias_ref, newb_ref, sink_ref, o_ref):
    w_len = cache_ref.shape[-1]
    wcol = lax.broadcasted_iota(jnp.int32, (C_REP, w_len), 1)
    for g in range(C_KV):
        q8 = (q_ref[g] * (HEAD_DIM ** -0.5)).astype(BF16)
        s = jnp.dot(q8, cache_ref[0, g].astype(BF16), preferred_element_type=F32) + bias_ref[g]
        s = jnp.where(wcol >= 1, s, NEG_INF)
        k_new = _bf16_round(new_ref[g, 0:1])
        v_new = _bf16_round(new_ref[g, 1:2])
        s_new = jnp.sum(q8.astype(F32) * k_new, axis=-1, keepdims=True) + newb_ref[g][:, 0:1]
        sink = sink_ref[g][:, 0:1]
        m = jnp.maximum(jnp.maximum(jnp.max(s, axis=-1, keepdims=True), s_new), sink)
        e = jnp.exp(s - m)
        e_new = jnp.exp(s_new - m)
        den = jnp.sum(e, axis=-1, keepdims=True) + e_new + jnp.exp(sink - m)
        o_ref[g] = _dot_nt((e / den).astype(BF16), cache_ref[1, g].astype(BF16)) + _bf16_round(e_new / den) * v_new


def _swa_sample(q, cache, new_kv, sinks, tab):
    n, w_len = cache.shape[:2]
    bias = jnp.transpose(_bias_lookup(tab, w_len - np.arange(w_len)), (1, 2, 0))
    lanes = lambda x: jnp.broadcast_to(x[:, :, None], (C_KV, C_REP, 128))
    full = lambda shape: pl.BlockSpec(shape, lambda i: (0,) * len(shape))
    return pl.pallas_call(
        _swa_sample_body,
        grid=(n,),
        in_specs=[
            pl.BlockSpec((None, C_KV, C_REP, HEAD_DIM), lambda i: (i, 0, 0, 0)),
            pl.BlockSpec((None, 2, C_KV, HEAD_DIM, w_len), lambda i: (i, 0, 0, 0, 0)),
            pl.BlockSpec((None, C_KV, 2, HEAD_DIM), lambda i: (i, 0, 0, 0)),
            full((C_KV, C_REP, w_len)), full((C_KV, C_REP, 128)), full((C_KV, C_REP, 128)),
        ],
        out_specs=pl.BlockSpec((None, C_KV, C_REP, HEAD_DIM), lambda i: (i, 0, 0, 0)),
        out_shape=jax.ShapeDtypeStruct((n, C_KV, C_REP, HEAD_DIM), F32),
        compiler_params=pltpu.CompilerParams(
            dimension_semantics=("arbitrary",), vmem_limit_bytes=V7X_VMEM_LIMIT_BYTES),
        name="swa_sample",
    )(q, _rows_last(cache), new_kv, bias, lanes(tab[0]), lanes(sinks.reshape(C_KV, C_REP)))


PROJ_ROW_TILE = 512
A_IN_PAD = 1920
A_U_COL = A_Q + 6 * A_KVW
A_GATE_COL = A_U_COL + B_WIDTH


def _heads_first(kvt_ref, slot, k, v, n_kv):
    for g in range(n_kv):
        kvt_ref[slot + g] = k[:, g * HEAD_DIM:(g + 1) * HEAD_DIM].astype(BF16)
        kvt_ref[slot + n_kv + g] = v[:, g * HEAD_DIM:(g + 1) * HEAD_DIM].astype(BF16)


def _inproj_a_body(x_ref, gain_ref, w_ref, qg_ref, kg_ref,
                   q_ref, cmp_ref, sel_ref, win_ref, gate_ref, u_ref, kvt_ref):
    xn = _rms_rows(x_ref[...], gain_ref[...]).astype(BF16)
    z = jnp.dot(xn, w_ref[...], preferred_element_type=F32)
    for j in range(A_Q // 128):
        q_ref[:, j * 128:(j + 1) * 128] = _rms_heads128(z[:, j * 128:(j + 1) * 128], qg_ref[...])
    for out_ref, off, slot in ((cmp_ref, A_Q, None), (sel_ref, A_Q + 2 * A_KVW, 0), (win_ref, A_Q + 4 * A_KVW, 4)):
        k = _rms_heads128(z[:, off:off + A_KVW], kg_ref[...])
        v = z[:, off + A_KVW:off + 2 * A_KVW]
        out_ref[:, :A_KVW] = k
        out_ref[:, A_KVW:] = v
        if slot is not None:
            _heads_first(kvt_ref, slot, k, v, A_KV)
    u_ref[...] = z[:, A_U_COL:A_GATE_COL]
    gate_ref[...] = 1.0 / (1.0 + jnp.exp(-z[:, A_GATE_COL:A_IN_PAD]))


def _inproj_a(x, gain, w_in, q_gain, k_gain):
    b, t, d = x.shape
    tm = min(t, PROJ_ROW_TILE)
    w = jnp.concatenate([w_in[:, :A_U_COL], w_in[:, A_U_COL + A_GATE:], w_in[:, A_U_COL:A_U_COL + A_GATE],
                         jnp.zeros((d, A_IN_PAD - A_GATE_COL - A_GATE), F32)], axis=1).astype(BF16)
    tile2 = lambda g: jnp.tile(g, 2).reshape(1, 128)
    rows = lambda width: pl.BlockSpec((None, tm, width), lambda i, j: (i, j, 0))
    shape = lambda width: jax.ShapeDtypeStruct((b, t, width), F32)
    return pl.pallas_call(
        _inproj_a_body,
        grid=(b, t // tm),
        in_specs=[rows(d), _const_spec((1, d)), _const_spec((d, A_IN_PAD)), _const_spec((1, 128)), _const_spec((1, 128))],
        out_specs=[rows(A_Q), rows(2 * A_KVW), rows(2 * A_KVW), rows(2 * A_KVW), rows(128), rows(B_WIDTH),
                   pl.BlockSpec((None, 8, tm, HEAD_DIM), lambda i, j: (i, 0, j, 0))],
        out_shape=[shape(A_Q), shape(2 * A_KVW), shape(2 * A_KVW), shape(2 * A_KVW), shape(128), shape(B_WIDTH),
                   jax.ShapeDtypeStruct((b, 8, t, HEAD_DIM), BF16)],
        compiler_params=pltpu.CompilerParams(
            dimension_semantics=("parallel", "parallel"), vmem_limit_bytes=V7X_VMEM_LIMIT_BYTES),
        name="inproj_nsa_s5",
    )(x, gain.reshape(1, d), w, tile2(q_gain), tile2(k_gain))


def _inproj_c_body(x_ref, gain_ref, w_ref, qg_ref, kg_ref, q_ref, kv_ref, kvt_ref):
    xn = _rms_rows(x_ref[...], gain_ref[...]).astype(BF16)
    z = jnp.dot(xn, w_ref[...], preferred_element_type=F32)
    n_q = C_HEADS * HEAD_DIM
    for j in range(n_q // 128):
        q_ref[:, j * 128:(j + 1) * 128] = _rms_heads128(z[:, j * 128:(j + 1) * 128], qg_ref[...])
    k = _rms_heads128(z[:, n_q:n_q + C_KV * HEAD_DIM], kg_ref[...])
    v = z[:, n_q + C_KV * HEAD_DIM:]
    kv_ref[:, :C_KV * HEAD_DIM] = k
    kv_ref[:, C_KV * HEAD_DIM:] = v
    _heads_first(kvt_ref, 0, k, v, C_KV)


def _inproj_c(x, gain, w_in, q_gain, k_gain):
    b, t, d = x.shape
    tm = min(t, PROJ_ROW_TILE)
    n_in = w_in.shape[1]
    tile2 = lambda g: jnp.tile(g, 2).reshape(1, 128)
    rows = lambda width: pl.BlockSpec((None, tm, width), lambda i, j: (i, j, 0))
    shape = lambda width: jax.ShapeDtypeStruct((b, t, width), F32)
    return pl.pallas_call(
        _inproj_c_body,
        grid=(b, t // tm),
        in_specs=[rows(d), _const_spec((1, d)), _const_spec((d, n_in)), _const_spec((1, 128)), _const_spec((1, 128))],
        out_specs=[rows(C_HEADS * HEAD_DIM), rows(2 * C_KV * HEAD_DIM),
                   pl.BlockSpec((None, 2 * C_KV, tm, HEAD_DIM), lambda i, j: (i, 0, j, 0))],
        out_shape=[shape(C_HEADS * HEAD_DIM), shape(2 * C_KV * HEAD_DIM),
                   jax.ShapeDtypeStruct((b, 2 * C_KV, t, HEAD_DIM), BF16)],
        compiler_params=pltpu.CompilerParams(
            dimension_semantics=("parallel", "parallel"), vmem_limit_bytes=V7X_VMEM_LIMIT_BYTES),
        name="inproj_swa",
    )(x, gain.reshape(1, d), w_in.astype(BF16), tile2(q_gain), tile2(k_gain))


FFN_ROW_TILE = 512
FFN_COL_CHUNK = 1408


def _mixer_residual(y_ref, x_ref, mix_refs, wout_refs):
    y_ref[...] = x_ref[...]
    for m_ref, w_ref in zip(mix_refs, wout_refs):
        y_ref[...] += jnp.dot(m_ref[...].astype(BF16), w_ref[...], preferred_element_type=F32)
    return y_ref[...]


def _tail_prompt_body(*refs, n_mix, tm, ffc):
    x_ref = refs[0]
    mix_refs = refs[1:1 + n_mix]
    wout_refs = refs[1 + n_mix:1 + 2 * n_mix]
    (gain_ref, prev_ref, wup_ref, wgate_ref, cw_ref, cb_ref, wdown_ref, y_ref, cs_ref, hbuf_ref) = refs[1 + 2 * n_mix:]
    t = pl.program_id(1)
    xn = _rms_rows(_mixer_residual(y_ref, x_ref, mix_refs, wout_refs), gain_ref[...]).astype(BF16)
    for c in range(D_FF // ffc):
        lo = c * ffc
        h = jnp.dot(xn, wup_ref[:, lo:lo + ffc], preferred_element_type=F32)
        g = jnp.dot(xn, wgate_ref[:, lo:lo + ffc], preferred_element_type=F32)

        @pl.when(t == 0)
        def _():
            hbuf_ref[c, 6:8, :] = prev_ref[:, lo:lo + ffc]

        hbuf_ref[c, 8:8 + tm, :] = h
        hm1 = hbuf_ref[c, 7:7 + tm, :]
        hm2 = hbuf_ref[c, 6:6 + tm, :]
        cw = cw_ref[:, lo:lo + ffc]
        hc = cw[0:1] * hm2 + cw[1:2] * hm1 + cw[2:3] * h + cb_ref[:, lo:lo + ffc]
        a = (_gelu_tanh(hc) * g).astype(BF16)
        y_ref[...] += jnp.dot(a, wdown_ref[lo:lo + ffc, :], preferred_element_type=F32)
        hbuf_ref[c, 0:8, :] = h[tm - 8:tm, :]
        cs_ref[:, lo:lo + ffc] = h[tm - 2:tm, :]


def _tail_prompt(x, mixes, wouts, gain, prev, wup, wgate, cw, cb, wdown):
    b, t, d = x.shape
    tm, ffc = FFN_ROW_TILE, FFN_COL_CHUNK
    rows = lambda width: pl.BlockSpec((None, tm, width), lambda i, j: (i, j, 0))
    state = pl.BlockSpec((None, CONV_W - 1, D_FF), lambda i, j: (i, 0, 0))
    return pl.pallas_call(
        functools.partial(_tail_prompt_body, n_mix=len(mixes), tm=tm, ffc=ffc),
        grid=(b, t // tm),
        in_specs=[rows(d)] + [rows(m.shape[-1]) for m in mixes] + [_const_spec(w.shape) for w in wouts] + [
            _const_spec((1, d)), state, _const_spec((d, D_FF)), _const_spec((d, D_FF)),
            _const_spec((CONV_W, D_FF)), _const_spec((1, D_FF)), _const_spec((D_FF, d))],
        out_specs=[rows(d), state],
        out_shape=[jax.ShapeDtypeStruct((b, t, d), F32), jax.ShapeDtypeStruct((b, CONV_W - 1, D_FF), F32)],
        scratch_shapes=[pltpu.VMEM((D_FF // ffc, 8 + tm, ffc), F32)],
        compiler_params=pltpu.CompilerParams(
            dimension_semantics=("parallel", "arbitrary"), vmem_limit_bytes=V7X_VMEM_LIMIT_BYTES),
        name="tail_prompt",
    )(x, *mixes, *wouts, gain, prev, wup, wgate, cw, cb, wdown)


def _tail_sample_body(*refs, n_mix, ffc):
    x_ref = refs[0]
    mix_refs = refs[1:1 + n_mix]
    wout_refs = refs[1 + n_mix:1 + 2 * n_mix]
    (gain_ref, prev_ref, wup_ref, wgate_ref, cw_ref, cb_ref, wdown_ref, y_ref, cs_ref) = refs[1 + 2 * n_mix:]
    xn = _rms_rows(_mixer_residual(y_ref, x_ref, mix_refs, wout_refs), gain_ref[...]).astype(BF16)
    for c in range(D_FF // ffc):
        lo = c * ffc
        h = jnp.dot(xn, wup_ref[:, lo:lo + ffc], preferred_element_type=F32)
        g = jnp.dot(xn, wgate_ref[:, lo:lo + ffc], preferred_element_type=F32)
        hm2 = prev_ref[:, lo:lo + ffc]
        hm1 = prev_ref[:, D_FF + lo:D_FF + lo + ffc]
        cw = cw_ref[:, lo:lo + ffc]
        hc = cw[0:1] * hm2 + cw[1:2] * hm1 + cw[2:3] * h + cb_ref[:, lo:lo + ffc]
        a = (_gelu_tanh(hc) * g).astype(BF16)
        y_ref[...] += jnp.dot(a, wdown_ref[lo:lo + ffc, :], preferred_element_type=F32)
        cs_ref[:, lo:lo + ffc] = hm1
        cs_ref[:, D_FF + lo:D_FF + lo + ffc] = h


def _tail_sample(x, mixes, wouts, gain, prev, wup, wgate, cw, cb, wdown):
    n, d = x.shape
    full = lambda shape: _const_spec(shape)
    return pl.pallas_call(
        functools.partial(_tail_sample_body, n_mix=len(mixes), ffc=FFN_COL_CHUNK),
        grid=(1,),
        in_specs=[full((n, d))] + [full(m.shape) for m in mixes] + [full(w.shape) for w in wouts] + [
            full((1, d)), full((n, (CONV_W - 1) * D_FF)), full((d, D_FF)), full((d, D_FF)),
            full((CONV_W, D_FF)), full((1, D_FF)), full((D_FF, d))],
        out_specs=[pl.BlockSpec((n, d), lambda i: (0, 0)), pl.BlockSpec((n, (CONV_W - 1) * D_FF), lambda i: (0, 0))],
        out_shape=[jax.ShapeDtypeStruct((n, d), F32), jax.ShapeDtypeStruct((n, (CONV_W - 1) * D_FF), F32)],
        compiler_params=pltpu.CompilerParams(
            dimension_semantics=("arbitrary",), vmem_limit_bytes=V7X_VMEM_LIMIT_BYTES),
        name="tail_sample",
    )(x, *mixes, *wouts, gain, prev, wup, wgate, cw, cb, wdown)


def kernel(x_prompt, x_sample, cache_nsa_cmp, cache_nsa_sel, cache_nsa_win, state_s5_re, state_s5_im,
           cache_swa, state_ffn_conv, page_table, rel_bias, norm_mix, norm_ffn, a_w_in, a_w_out,
           nsa_q_gain, nsa_k_gain, nsa_cmp_pos, nsa_cmp_w1, nsa_cmp_w2, s5_a_re, s5_a_im, s5_log_dt,
           s5_b_re, s5_b_im, s5_c_re, s5_c_im, s5_d, s5_w_glu, s5_b_glu, c_w_in, c_w_out, c_q_gain,
           c_k_gain, c_sinks, ffn_w_up, ffn_w_gate, ffn_conv_w, ffn_conv_b, ffn_w_down):
    bp, tp, _ = x_prompt.shape
    bs, ts, _ = x_sample.shape
    assert ts == 1 and DEPTH == 2
    tab_a = rel_bias[:, :A_HEADS].reshape(NUM_BUCKETS, A_KV, A_REP)
    tab_c = rel_bias[:, :C_HEADS].reshape(NUM_BUCKETS, C_KV, C_REP)
    kv6 = lambda x: x.reshape(x.shape[:-1] + (2, x.shape[-1] // (2 * HEAD_DIM), HEAD_DIM))
    hp, hs = x_prompt, x_sample.reshape(1, bs, D_MODEL)
    conv_p, conv_s = [], []

    def tail(layer, hp, hs, mixes_p, mixes_s, wouts):
        wouts = [w.astype(BF16) for w in wouts]
        ffn = (norm_ffn[layer].reshape(1, D_MODEL),)
        wts = (ffn_w_up[layer].astype(BF16), ffn_w_gate[layer].astype(BF16), ffn_conv_w[layer],
               ffn_conv_b[layer].reshape(1, D_FF), ffn_w_down[layer].astype(BF16))
        hp, cp = _tail_prompt(hp, mixes_p, wouts, *ffn, jnp.zeros((bp, CONV_W - 1, D_FF), F32), *wts)
        hs2, cs = _tail_sample(hs[0], [m.reshape(bs, -1) for m in mixes_s], wouts, *ffn,
                               state_ffn_conv[layer].reshape(bs, (CONV_W - 1) * D_FF), *wts)
        conv_p.append(cp)
        conv_s.append(cs.reshape(bs, CONV_W - 1, D_FF))
        return hp, hs2.reshape(1, bs, D_MODEL)

    proj = (norm_mix[0], a_w_in[0], nsa_q_gain[0], nsa_k_gain[0])
    qp, cmp_p, sel_p, win_p, gate_p, up, kvt_p = _inproj_a(hp, *proj)
    qs, cmp_s, sel_s, win_s, gate_s, us, _ = _inproj_a(hs, *proj)
    cmp_w = _compress_weights(nsa_cmp_pos[0], nsa_cmp_w1[0], nsa_cmp_w2[0], nsa_k_gain[0])
    kc, vc = _cmp_prompt(cmp_p, cmp_w)
    o_ap = _nsa_prompt_attend(qp, gate_p, kc, vc, kvt_p, tab_a)
    o_as = _nsa_sample(qs.reshape(bs, A_KV, A_REP, HEAD_DIM), gate_s[0, :, :A_GATE].reshape(bs, A_KV, A_REP, 3),
                       sel_s.reshape(bs, 2, A_KV, HEAD_DIM), win_s.reshape(bs, 2, A_KV, HEAD_DIM),
                       cache_nsa_cmp[0], cache_nsa_sel[0], cache_nsa_win[0], page_table, cmp_w, tab_a)
    s5w = _s5_weights(s5_a_re[0], s5_a_im[0], s5_log_dt[0], s5_b_re[0], s5_b_im[0], s5_c_re[0], s5_c_im[0],
                      s5_d[0], s5_w_glu[0], s5_b_glu[0])
    h0 = jnp.zeros((bp, S5_NS), F32)
    o_bp, hr_p, hi_p = _s5_mix(up, h0, h0, s5w)
    o_bs, hr_s, hi_s = _s5_mix(us.reshape(bs, 1, B_WIDTH), state_s5_re[0].reshape(bs, S5_NS),
                               state_s5_im[0].reshape(bs, S5_NS), s5w)
    hp, hs = tail(0, hp, hs, [o_ap, o_bp], [o_as, o_bs], [a_w_out[0][:A_Q], a_w_out[0][A_Q:]])

    proj = (norm_mix[1], c_w_in[0], c_q_gain[0], c_k_gain[0])
    qcp, kv_p, kvt_c = _inproj_c(hp, *proj)
    qcs, kv_s, _ = _inproj_c(hs, *proj)
    o_cp = _swa_prompt(qcp, kvt_c, c_sinks[0], tab_c)
    new_kv = kv_s.reshape(bs, 2, C_KV, HEAD_DIM)
    o_cs = _swa_sample(qcs.reshape(bs, C_KV, C_REP, HEAD_DIM), cache_swa[0], jnp.transpose(new_kv, (0, 2, 1, 3)),
                       c_sinks[0], tab_c)
    hp, hs = tail(1, hp, hs, [o_cp], [o_cs], [c_w_out[0]])

    state = lambda x, n: x.reshape(1, n, B_GROUPS, B_STATE)
    nsa_win_s = jnp.concatenate([cache_nsa_win[0][:, 1:], kv6(win_s).reshape(bs, 1, 2, A_KV, HEAD_DIM)], axis=1)
    swa_s = jnp.concatenate([cache_swa[0][:, 1:], new_kv.reshape(bs, 1, 2, C_KV, HEAD_DIM)], axis=1)
    return (hp, hs.reshape(bs, ts, D_MODEL),
            kv6(cmp_p)[None], kv6(cmp_s).reshape(1, bs, ts, 2, A_KV, HEAD_DIM),
            kv6(sel_p)[None], kv6(sel_s).reshape(1, bs, ts, 2, A_KV, HEAD_DIM),
            kv6(win_p)[None, :, -min(WIN_A, tp):], nsa_win_s[None],
            state(hr_p, bp), state(hi_p, bp), state(hr_s, bs), state(hi_s, bs),
            kv6(kv_p)[None, :, -min(WIN_C, tp):], swa_s[None],
            jnp.stack(conv_p), jnp.stack(conv_s))
```

```python
import functools
import math

import jax
import jax.numpy as jnp
import numpy as np
from jax import lax
from jax.experimental import pallas as pl
from jax.experimental.pallas import tpu as pltpu

D_MODEL = 1024
DEPTH = 2
PAGE_SIZE = 128
HEAD_DIM = 64
A_HEADS = 8
A_KV = 2
A_REP = A_HEADS // A_KV
A_Q = A_HEADS * HEAD_DIM
A_KVW = A_KV * HEAD_DIM
A_GATE = 3 * A_HEADS
CMP_LEN = 32
CMP_STRIDE = 16
SEL_BLOCK = 64
SEL_TOPK = 16
WIN_A = 512
NSA_QBLK = 64
FORCE_BONUS = 1000.0
B_WIDTH = D_MODEL // 2
B_GROUP = 16
B_GROUPS = B_WIDTH // B_GROUP
B_STATE = 64
C_HEADS = D_MODEL // HEAD_DIM
C_KV = 2
C_REP = C_HEADS // C_KV
WIN_C = 128
NUM_BUCKETS = 32
MAX_DISTANCE = 128
D_FF = 2816
CONV_W = 3
EPS = 1e-6

F32 = jnp.float32
BF16 = jnp.bfloat16

V7X_VMEM_LIMIT_BYTES = 56 * 1024 * 1024


def _gelu_tanh(x):
    return 0.5 * x * (1.0 + jnp.tanh(math.sqrt(2.0 / math.pi) * (x + 0.044715 * (x * x * x))))


def _rms_rows(x, gain):
    return x * lax.rsqrt(jnp.mean(x * x, axis=-1, keepdims=True) + EPS) * gain


def _const_spec(shape):
    zeros = (0,) * len(shape)
    return pl.BlockSpec(shape, lambda *_: zeros, pipeline_mode=pl.Buffered(1))


NSA_Q_TILE = 128
NSA_FAR_TILE = 512
NSA_NEAR = 2 * NSA_Q_TILE
NEG_INF = float("-inf")


def _dot_nt(a, b):
    return lax.dot_general(a, b, (((1,), (1,)), ((), ())), preferred_element_type=F32)


def _softmax_start(s, v):
    m = jnp.max(s, axis=-1, keepdims=True)
    e = jnp.exp(s - m)
    return m, jnp.sum(e, axis=-1, keepdims=True), jnp.dot(e.astype(BF16), v, preferred_element_type=F32)


def _softmax_more(carry, s, v):
    m, l, acc = carry
    m_new = jnp.maximum(m, jnp.max(s, axis=-1, keepdims=True))
    alpha = jnp.exp(m - m_new)
    e = jnp.exp(s - m_new)
    return (m_new, alpha * l + jnp.sum(e, axis=-1, keepdims=True),
            alpha * acc + jnp.dot(e.astype(BF16), v, preferred_element_type=F32))


def _block_expand(k0, width):
    kpos = k0 + lax.broadcasted_iota(jnp.int32, (128, width), 1)
    blk = lax.broadcasted_iota(jnp.int32, (128, width), 0)
    return jnp.where((kpos >> 6) == blk, 1.0, 0.0).astype(BF16)


def _nsa_prompt_body(q_ref, gate_ref, kc_ref, vc_ref, kvt_ref, btile_ref, cover_ref, o_ref, *, tq):
    width = A_REP * HEAD_DIM
    for g in range(A_KV):
        _nsa_prompt_group(g, q_ref.at[:, g * width:(g + 1) * width], gate_ref, kc_ref.at[g], vc_ref.at[g],
                          kvt_ref.at[g], kvt_ref.at[A_KV + g], kvt_ref.at[2 * A_KV + g], kvt_ref.at[3 * A_KV + g],
                          btile_ref.at[g], cover_ref, o_ref.at[:, g * width:(g + 1) * width], tq)


def _nsa_prompt_group(g, q_ref, gate_ref, kc_ref, vc_ref, ks_ref, vs_ref, kw_ref, vw_ref, btile_ref, cover_ref,
                      o_ref, tq):
    qt = pl.program_id(1)
    q0 = pl.multiple_of(qt * tq, tq)
    rows = A_REP * tq
    q = q_ref[...]
    qs = jnp.concatenate([q[:, r * HEAD_DIM:(r + 1) * HEAD_DIM] for r in range(A_REP)], axis=0)
    qs = (qs * (HEAD_DIM ** -0.5)).astype(BF16)

    def row_pos(width):
        return q0 + (lax.broadcasted_iota(jnp.int32, (rows, width), 0) & (tq - 1))

    def col_idx(width):
        return lax.broadcasted_iota(jnp.int32, (rows, width), 1)

    n_idx = col_idx(128)
    valid_c = (n_idx * CMP_STRIDE + (CMP_LEN - 1) <= row_pos(128)) & (n_idx < 127)
    s_c = jnp.where(valid_c, _dot_nt(qs, kc_ref[...]), NEG_INF)
    m_c = jnp.max(s_c, axis=-1, keepdims=True)
    m_c = jnp.where(m_c == NEG_INF, 0.0, m_c)
    e_c = jnp.exp(s_c - m_c)
    d_c = jnp.sum(e_c, axis=-1, keepdims=True)
    p_c = (e_c / jnp.where(d_c > 0, d_c, 1.0)).astype(BF16)
    o_c = jnp.dot(p_c, vc_ref[...], preferred_element_type=F32)
    p_heads = jnp.concatenate([p_c[r * tq:(r + 1) * tq] for r in range(A_REP)], axis=1)
    imp = _dot_nt(cover_ref[...], p_heads)[0:32]
    s_idx = lax.broadcasted_iota(jnp.int32, (32, tq), 0)
    qblk = (q0 + lax.broadcasted_iota(jnp.int32, (32, tq), 1)) >> 6
    forced = (s_idx == 0) | (s_idx == qblk) | (s_idx == qblk - 1)
    allowed = s_idx <= qblk
    score = jnp.where(allowed, imp + jnp.where(forced, FORCE_BONUS, 0.0), NEG_INF)
    rank = jnp.zeros((32, tq), F32)
    for j in range(32):
        other = score[j:j + 1, :]
        beats = (other > score) | ((other == score) & (s_idx > j))
        rank = rank + jnp.where(beats, 1.0, 0.0)
    sel = jnp.where((rank < SEL_TOPK) & allowed, 1.0, 0.0)
    sel = jnp.concatenate([sel, jnp.zeros((128 - 32, tq), F32)], axis=0).T.astype(BF16)
    sel = jnp.concatenate([sel] * A_REP, axis=0)

    prev0 = pl.multiple_of(jnp.maximum(q0 - tq, 0), tq)
    btile = btile_ref[...]

    def near(k_ref, v_ref, extra_mask):
        k = jnp.concatenate([k_ref[pl.ds(prev0, tq), :], k_ref[pl.ds(q0, tq), :]], axis=0)
        v = jnp.concatenate([v_ref[pl.ds(prev0, tq), :], v_ref[pl.ds(q0, tq), :]], axis=0)
        s = _dot_nt(qs, k) + btile
        return _softmax_start(s if extra_mask is None else jnp.where(extra_mask, s, NEG_INF), v)

    expand_near = jnp.concatenate([_block_expand(prev0, tq), _block_expand(q0, tq)], axis=1)
    sel_near = jnp.dot(sel, expand_near, preferred_element_type=F32) > 0.5

    far_end = q0 - tq

    def sel_far(i, carry):
        k0 = pl.multiple_of(i * NSA_FAR_TILE, NSA_FAR_TILE)
        s = _dot_nt(qs, ks_ref[pl.ds(k0, NSA_FAR_TILE), :])
        hit = jnp.dot(sel, _block_expand(k0, NSA_FAR_TILE), preferred_element_type=F32) > 0.5
        mask = hit & (k0 + col_idx(NSA_FAR_TILE) < far_end)
        return _softmax_more(carry, jnp.where(mask, s, NEG_INF), vs_ref[pl.ds(k0, NSA_FAR_TILE), :])

    n_far = (jnp.maximum(far_end, 0) + NSA_FAR_TILE - 1) // NSA_FAR_TILE
    _, l_s, acc_s = lax.fori_loop(0, n_far, sel_far, near(ks_ref, vs_ref, sel_near))
    o_s = acc_s / l_s

    w_far = WIN_A - tq
    wf0 = pl.multiple_of(jnp.maximum(q0 - WIN_A, 0), tq)
    wpos = wf0 + col_idx(w_far)
    wmask = (row_pos(w_far) - wpos < WIN_A) & (wpos < far_end)
    s_w = jnp.where(wmask, _dot_nt(qs, kw_ref[pl.ds(wf0, w_far), :]), NEG_INF)
    _, l_w, acc_w = _softmax_more(near(kw_ref, vw_ref, None), s_w, vw_ref[pl.ds(wf0, w_far), :])
    o_w = acc_w / l_w

    gates = gate_ref[...]
    outs = []
    for r in range(A_REP):
        sl = slice(r * tq, (r + 1) * tq)
        col = 3 * (g * A_REP + r)
        outs.append(gates[:, col:col + 1] * o_c[sl] + gates[:, col + 1:col + 2] * o_s[sl]
                    + gates[:, col + 2:col + 3] * o_w[sl])
    o_ref[...] = jnp.concatenate(outs, axis=1)


def _bucket_np(dist):
    n = np.maximum(dist, 0)
    exact = NUM_BUCKETS // 2
    nf = np.maximum(n, exact).astype(np.float64)
    large = exact + (np.log(nf / exact) / math.log(MAX_DISTANCE / exact) * (NUM_BUCKETS - exact)).astype(np.int64)
    return np.where(n < exact, n, np.minimum(large, NUM_BUCKETS - 1)).astype(np.int32)


def _bias_lookup(tab, dist):
    bucket = _bucket_np(np.asarray(dist))
    onehot = (jnp.asarray(bucket.reshape(-1, 1)) == jnp.arange(NUM_BUCKETS)[None, :]).astype(F32)
    flat = jnp.dot(onehot, tab.reshape(NUM_BUCKETS, -1), precision=lax.Precision.HIGHEST)
    return flat.reshape(bucket.shape + tab.shape[1:])


def _near_bias_tiles(tab, tq, window):
    i = np.arange(tq)[:, None]
    j = np.arange(2 * tq)[None, :]
    dist = tq + i - j
    ok = (dist >= 0) & (dist < window)
    bias = jnp.transpose(_bias_lookup(tab, dist), (2, 3, 0, 1))
    tiles = jnp.stack([jnp.where(jnp.asarray(ok & (j >= tq)), bias, NEG_INF), jnp.where(jnp.asarray(ok), bias, NEG_INF)])
    return tiles.reshape(2, tab.shape[1], tab.shape[2] * tq, 2 * tq)


def _cover_matrix(n_cmp_pad, n_sel):
    n = np.arange(n_cmp_pad)
    c_start = n * CMP_STRIDE
    c_end = c_start + CMP_LEN - 1
    s_start = np.arange(128) * SEL_BLOCK
    cover = (c_start[:, None] < s_start[None, :] + SEL_BLOCK) & (c_end[:, None] >= s_start[None, :])
    cover &= (np.arange(128)[None, :] < n_sel)
    return np.tile(cover.astype(np.float32), (A_REP, 1))


def _nsa_prompt_attend(q, gates, kc, vc, kvt, tab):
    b, t, _ = q.shape
    tq = NSA_Q_TILE
    assert t % NSA_FAR_TILE == 0 and t // SEL_BLOCK <= 32 and t >= WIN_A and kc.shape[2] == 128
    btile = _near_bias_tiles(tab - tab[NUM_BUCKETS - 1], tq, 2 * tq)
    cover = jnp.asarray(_cover_matrix(128, t // SEL_BLOCK).T, BF16)
    cmp_spec = pl.BlockSpec((None, A_KV, 128, HEAD_DIM), lambda i, j: (i, 0, 0, 0))
    return pl.pallas_call(
        functools.partial(_nsa_prompt_body, tq=tq),
        grid=(b, t // tq),
        in_specs=[
            pl.BlockSpec((None, tq, A_Q), lambda i, j: (i, j, 0)),
            pl.BlockSpec((None, tq, 128), lambda i, j: (i, j, 0)),
            cmp_spec, cmp_spec,
            pl.BlockSpec((None, 4 * A_KV, t, HEAD_DIM), lambda i, j: (i, 0, 0, 0)),
            pl.BlockSpec((None, A_KV, A_REP * tq, 2 * tq), lambda i, j: (jnp.minimum(j, 1), 0, 0, 0)),
            pl.BlockSpec((128, A_REP * 128), lambda i, j: (0, 0)),
        ],
        out_specs=pl.BlockSpec((None, tq, A_Q), lambda i, j: (i, j, 0)),
        out_shape=jax.ShapeDtypeStruct((b, t, A_Q), F32),
        compiler_params=pltpu.CompilerParams(
            dimension_semantics=("parallel", "arbitrary"), vmem_limit_bytes=V7X_VMEM_LIMIT_BYTES),
        name="nsa_prompt",
    )(q, gates, kc, vc, kvt, btile, cover)


S5_NS = B_GROUPS * B_STATE
S5_T_CHUNK = 64
S5_STRIP = 512


def _s5_body(u_ref, h0r_ref, h0i_ref, ar_ref, ai_ref, ldt_ref, wb_ref, wc_ref, d_ref, wglu_ref, bglu_ref,
             o_ref, hr_ref, hi_ref, coef_ref, st_ref, xbuf_ref, ubuf_ref, obuf_ref, *, nb, steps, interleave):
    c = pl.program_id(0)

    @pl.when(c == 0)
    def _():
        dt = jnp.exp(ldt_ref[...])
        ar, ai = ar_ref[...], ai_ref[...]
        mag = jnp.exp(ar * dt)
        abr, abi = mag * jnp.cos(ai * dt), mag * jnp.sin(ai * dt)
        den = ar * ar + ai * ai
        wr = ((abr - 1.0) * ar + abi * ai) / den
        wi = (abi * ar - (abr - 1.0) * ai) / den
        for k, val in enumerate((abr, abi, wr, wi)):
            coef_ref[k] = jnp.broadcast_to(val, (nb, S5_NS))
        st_ref[0] = h0r_ref[...]
        st_ref[1] = h0i_ref[...]

    if interleave:
        for b in range(nb):
            for j in range(B_WIDTH // 128):
                ubuf_ref.at[j][pl.ds(b, steps, stride=nb), :] = u_ref[b, :, j * 128:(j + 1) * 128]
        u = jnp.concatenate([ubuf_ref[j] for j in range(B_WIDTH // 128)], axis=1)
    else:
        u = u_ref[...]
    xbuf_ref[...] = jnp.dot(u.astype(BF16), wb_ref[...], preferred_element_type=F32)

    for lo in range(0, S5_NS, S5_STRIP):
        re = slice(lo, lo + S5_STRIP)
        im = slice(S5_NS + lo, S5_NS + lo + S5_STRIP)
        abr, abi, wr, wi = (coef_ref[k, :, re] for k in range(4))

        def step(t, carry):
            sr, si = carry
            r0 = pl.multiple_of(t * nb, nb)
            bur = xbuf_ref[pl.ds(r0, nb), re]
            bui = xbuf_ref[pl.ds(r0, nb), im]
            nsr = abr * sr - abi * si + (wr * bur - wi * bui)
            nsi = abr * si + abi * sr + (wr * bui + wi * bur)
            xbuf_ref[pl.ds(r0, nb), re] = nsr
            xbuf_ref[pl.ds(r0, nb), im] = nsi
            return nsr, nsi

        sr, si = lax.fori_loop(0, steps, step, (st_ref[0, :, re], st_ref[1, :, re]),
                               unroll=min(steps, 8))
        st_ref[0, :, re] = sr
        st_ref[1, :, re] = si

    y = jnp.dot(xbuf_ref[...].astype(BF16), wc_ref[...], preferred_element_type=F32) + d_ref[...] * u
    z = _gelu_tanh(y)
    gate = jnp.dot(z.astype(BF16), wglu_ref[...], preferred_element_type=F32) + bglu_ref[...]
    out = z * (1.0 / (1.0 + jnp.exp(-gate)))
    if interleave:
        for j in range(B_WIDTH // 128):
            obuf_ref[j] = out[:, j * 128:(j + 1) * 128]
        for b in range(nb):
            for j in range(B_WIDTH // 128):
                o_ref[b, :, j * 128:(j + 1) * 128] = obuf_ref.at[j][pl.ds(b, steps, stride=nb), :]
    else:
        o_ref[...] = out
    hr_ref[...] = st_ref[0]
    hi_ref[...] = st_ref[1]


def _s5_weights(a_re, a_im, log_dt, b_re, b_im, c_re, c_im, d, w_glu, b_glu):
    eye = jnp.eye(B_GROUPS, dtype=F32)
    blk_in = lambda w: jnp.einsum('hg,gpc->hcgp', eye, w).reshape(B_WIDTH, S5_NS)
    blk_out = lambda w: jnp.einsum('gh,gcp->gphc', eye, w).reshape(S5_NS, B_WIDTH)
    wb = jnp.concatenate([blk_in(b_re), blk_in(b_im)], axis=1).astype(BF16)
    wc = jnp.concatenate([blk_out(c_re), -blk_out(c_im)], axis=0).astype(BF16)
    flat = lambda x: x.reshape(1, S5_NS)
    return (flat(a_re), flat(a_im), flat(jnp.repeat(log_dt, B_STATE)), wb, wc, d.reshape(1, B_WIDTH),
            w_glu.astype(BF16), b_glu.reshape(1, B_WIDTH))


def _s5_mix(u, h_re, h_im, weights):
    nb, t, _ = u.shape
    interleave = t > 1
    steps = min(t, S5_T_CHUNK)
    rows = nb * steps
    body = functools.partial(_s5_body, nb=nb, steps=steps, interleave=interleave)
    if interleave:
        u_in = u
        u_spec = pl.BlockSpec((nb, steps, B_WIDTH), lambda c: (0, c, 0))
        o_shape = jax.ShapeDtypeStruct((nb, t, B_WIDTH), F32)
        scratch_rows = rows
    else:
        u_in = u.reshape(nb, B_WIDTH)
        u_spec = pl.BlockSpec((nb, B_WIDTH), lambda c: (0, 0))
        o_shape = jax.ShapeDtypeStruct((nb, B_WIDTH), F32)
        scratch_rows = 8
    o, hr, hi = pl.pallas_call(
        body,
        grid=(t // steps,),
        in_specs=[
            u_spec, _const_spec((nb, S5_NS)), _const_spec((nb, S5_NS)),
            _const_spec((1, S5_NS)), _const_spec((1, S5_NS)), _const_spec((1, S5_NS)),
            _const_spec((B_WIDTH, 2 * S5_NS)), _const_spec((2 * S5_NS, B_WIDTH)),
            _const_spec((1, B_WIDTH)), _const_spec((B_WIDTH, B_WIDTH)), _const_spec((1, B_WIDTH)),
        ],
        out_specs=[u_spec, pl.BlockSpec((nb, S5_NS), lambda c: (0, 0)), pl.BlockSpec((nb, S5_NS), lambda c: (0, 0))],
        out_shape=[o_shape, jax.ShapeDtypeStruct((nb, S5_NS), F32), jax.ShapeDtypeStruct((nb, S5_NS), F32)],
        scratch_shapes=[
            pltpu.VMEM((4, nb, S5_NS), F32),
            pltpu.VMEM((2, nb, S5_NS), F32),
            pltpu.VMEM((rows, 2 * S5_NS), F32),
            pltpu.VMEM((B_WIDTH // 128, scratch_rows, 128), F32),
            pltpu.VMEM((B_WIDTH // 128, scratch_rows, 128), F32),
        ],
        compiler_params=pltpu.CompilerParams(
            dimension_semantics=("arbitrary",),
            vmem_limit_bytes=V7X_VMEM_LIMIT_BYTES),
        name="s5_mix",
    )(u_in, h_re, h_im, *weights)
    return o.reshape(nb, t, B_WIDTH), hr, hi


CHUNK_W = CMP_STRIDE * 2 * A_KVW
CHUNKS_PER_PAGE = PAGE_SIZE // CMP_STRIDE
SEL_PAD = 256


def _bf16_round(x):
    return x.astype(BF16).astype(F32)


def _compress_tail(c, pos, w2):
    c1 = c[:, 128:]
    nxt = jnp.concatenate([c1[1:], c1[:1]], axis=0)
    hid = (pos + c[:, :128]) + nxt
    return jnp.dot(_gelu_tanh(hid).astype(BF16), w2, preferred_element_type=F32)


def _rms_heads128(x, gain):
    left = lax.broadcasted_iota(jnp.int32, x.shape, 1) < HEAD_DIM
    sq = x * x
    s0 = jnp.sum(jnp.where(left, sq, 0.0), axis=-1, keepdims=True)
    s1 = jnp.sum(jnp.where(left, 0.0, sq), axis=-1, keepdims=True)
    ms = jnp.where(left, s0, s1) * (1.0 / HEAD_DIM)
    return x * lax.rsqrt(ms + EPS) * gain


def _pad_rows8(x):
    return jnp.concatenate([x, jnp.zeros((8 - x.shape[0], x.shape[1]), x.dtype)], axis=0)


def _compress_rows(rows_scr, kv, n_chunks, w_ref, pos, w2_ref):
    acc = jnp.zeros((n_chunks, 2 * A_KVW), F32)
    for sp in range(CMP_STRIDE // 2):
        x = jnp.concatenate([rows_scr.at[kv][pl.ds(2 * sp + e, n_chunks, stride=CMP_STRIDE), :] for e in range(2)],
                            axis=1).astype(BF16)
        acc = acc + jnp.dot(x, w_ref[sp], preferred_element_type=F32)
    return _compress_tail(acc, pos, w2_ref[...])


def _compress_weights(cmp_pos, cmp_w1, cmp_w2, k_gain):
    eye = jnp.eye(A_KV, dtype=F32)
    w1 = cmp_w1.reshape(2, 2, CMP_STRIDE, HEAD_DIM, HEAD_DIM)
    w_big = jnp.einsum('ajsdh,gk->asgdjkh', w1, eye).reshape(2, CMP_STRIDE // 2, 2 * A_KVW, 2 * A_KVW).astype(BF16)
    w_pos = jnp.concatenate([cmp_w1, cmp_w1], axis=-1).astype(BF16)
    pos = jnp.broadcast_to(cmp_pos.reshape(2, 1, CMP_LEN * HEAD_DIM), (2, 8, CMP_LEN * HEAD_DIM)).astype(BF16)
    w2_big = jnp.einsum('ahd,gk->aghkd', cmp_w2, eye).reshape(2, A_KVW, A_KVW).astype(BF16)
    gain2 = jnp.tile(k_gain, A_KV).reshape(1, A_KVW)
    return (pos[0], pos[1], w_pos[0], w_pos[1], w_big[0], w_big[1], w2_big[0], w2_big[1], gain2)


def _compress_specs(full):
    mlp_w = (CMP_STRIDE // 2, 2 * A_KVW, 2 * A_KVW)
    return [full((8, CMP_LEN * HEAD_DIM)), full((8, CMP_LEN * HEAD_DIM)),
            full((CMP_LEN * HEAD_DIM, A_KVW)), full((CMP_LEN * HEAD_DIM, A_KVW)),
            full(mlp_w), full(mlp_w), full((A_KVW, A_KVW)), full((A_KVW, A_KVW)), full((1, A_KVW))]


def _pos_terms(pos_scr, posk_ref, posv_ref, wpos_k_ref, wpos_v_ref):
    pos_scr[0] = jnp.dot(posk_ref[...], wpos_k_ref[...], preferred_element_type=F32)
    pos_scr[1] = jnp.dot(posv_ref[...], wpos_v_ref[...], preferred_element_type=F32)


def _cmp_prompt_body(cmp_ref, posk_ref, posv_ref, wpos_k_ref, wpos_v_ref, wk_ref, wv_ref, w2k_ref, w2v_ref,
                     kgain_ref, kc_ref, vc_ref, pos_scr, rows_scr, *, n_chunks):
    @pl.when(pl.program_id(0) == 0)
    def _():
        _pos_terms(pos_scr, posk_ref, posv_ref, wpos_k_ref, wpos_v_ref)

    rows_scr[0] = cmp_ref[:, :A_KVW]
    rows_scr[1] = cmp_ref[:, A_KVW:]
    kc = _rms_heads128(_compress_rows(rows_scr, 0, n_chunks, wk_ref, pos_scr[0, 0:1], w2k_ref),
                       kgain_ref[...]).astype(BF16)
    vc = _compress_rows(rows_scr, 1, n_chunks, wv_ref, pos_scr[1, 0:1], w2v_ref).astype(BF16)
    for g in range(A_KV):
        kc_ref[g] = kc[:, g * HEAD_DIM:(g + 1) * HEAD_DIM]
        vc_ref[g] = vc[:, g * HEAD_DIM:(g + 1) * HEAD_DIM]


def _cmp_prompt(cmp_rows, cmp_weights):
    b, t, _ = cmp_rows.shape
    n_chunks = t // CMP_STRIDE
    full = lambda shape: pl.BlockSpec(shape, lambda i: (0,) * len(shape))
    out_spec = pl.BlockSpec((None, A_KV, n_chunks, HEAD_DIM), lambda i: (i, 0, 0, 0))
    out_shape = jax.ShapeDtypeStruct((b, A_KV, n_chunks, HEAD_DIM), BF16)
    return pl.pallas_call(
        functools.partial(_cmp_prompt_body, n_chunks=n_chunks),
        grid=(b,),
        in_specs=[pl.BlockSpec((None, t, 2 * A_KVW), lambda i: (i, 0, 0))] + _compress_specs(full),
        out_specs=[out_spec, out_spec],
        out_shape=[out_shape, out_shape],
        scratch_shapes=[pltpu.VMEM((2, 8, A_KVW), F32), pltpu.VMEM((2, t, A_KVW), F32)],
        compiler_params=pltpu.CompilerParams(
            dimension_semantics=("arbitrary",), vmem_limit_bytes=V7X_VMEM_LIMIT_BYTES),
        name="nsa_cmp_prompt",
    )(cmp_rows, *cmp_weights)


def _nsa_sample_cmp_body(pt_ref, *refs, n_pages, n_sel):
    page_refs = refs[:n_pages]
    (q_ref, posk_ref, posv_ref, wpos_k_ref, wpos_v_ref, wk_ref, wv_ref, w2k_ref, w2v_ref, kgain_ref, cover_ref,
     oc_ref, idx_ref, pos_scr, rows_scr) = refs[n_pages:]

    @pl.when(pl.program_id(0) == 0)
    def _():
        _pos_terms(pos_scr, posk_ref, posv_ref, wpos_k_ref, wpos_v_ref)

    n_chunks = n_pages * CHUNKS_PER_PAGE
    n_cmp = n_chunks - 1
    for kv in range(2):
        for p in range(n_pages):
            rows_scr[kv, p * PAGE_SIZE:(p + 1) * PAGE_SIZE, :] = page_refs[p][kv].reshape(A_KVW, PAGE_SIZE).T

    kc = _rms_heads128(_compress_rows(rows_scr, 0, n_chunks, wk_ref, pos_scr[0, 0:1], w2k_ref),
                       kgain_ref[...]).astype(BF16)
    vc = _compress_rows(rows_scr, 1, n_chunks, wv_ref, pos_scr[1, 0:1], w2v_ref).astype(BF16)

    col = lax.broadcasted_iota(jnp.int32, (8, n_chunks), 1)
    s_idx = lax.broadcasted_iota(jnp.int32, (1, SEL_PAD), 1)
    s_idx_f = s_idx.astype(F32)
    forced = (s_idx == 0) | (s_idx == n_sel - 1) | (s_idx == n_sel - 2)
    lane = lax.broadcasted_iota(jnp.int32, (1, 128), 1)
    for g in range(A_KV):
        hs = slice(g * HEAD_DIM, (g + 1) * HEAD_DIM)
        q8 = (_pad_rows8(q_ref[g]) * (HEAD_DIM ** -0.5)).astype(BF16)
        s = jnp.where(col < n_cmp, _dot_nt(q8, kc[:, hs]), NEG_INF)
        e = jnp.exp(s - jnp.max(s, axis=-1, keepdims=True))
        p = (e / jnp.sum(e, axis=-1, keepdims=True)).astype(BF16)
        oc_ref[g] = jnp.dot(p, vc[:, hs], preferred_element_type=F32)[0:A_REP]
        imp = jnp.sum(jnp.dot(p, cover_ref[...], preferred_element_type=F32)[0:A_REP], axis=0, keepdims=True)
        score = jnp.where(s_idx < n_sel, imp + jnp.where(forced, FORCE_BONUS, 0.0), NEG_INF)
        rank = jnp.zeros((1, SEL_PAD), F32)
        for j in range(n_sel):
            cj = score[:, j:j + 1]
            beats = (cj > score) | ((cj == score) & (s_idx > j))
            rank = rank + jnp.where(beats, 1.0, 0.0)
        picks = jnp.zeros((1, 128), F32)
        for r in range(SEL_TOPK):
            block = jnp.sum(jnp.where(rank == float(r), s_idx_f, 0.0), axis=-1, keepdims=True)
            picks = picks + jnp.where(lane == r, block, 0.0)
        idx_ref[g:g + 1, :] = picks.astype(jnp.int32)


def _nsa_sample_att_body(idx_ref, pt_ref, *refs, n_past_blk):
    n_slots = A_KV * SEL_TOPK
    slot_refs = refs[:n_slots]
    (q_all, gate_all, oc_all, newsel_all, wcache_all, newwin_all, selb_all, winb_all, newb_all, o_all) = refs[n_slots:]
    b = pl.program_id(0)
    for g in range(A_KV):
        _nsa_sample_att_group(b, g, idx_ref, slot_refs[g * SEL_TOPK:(g + 1) * SEL_TOPK], q_all.at[g], gate_all.at[g],
                              oc_all.at[g], newsel_all.at[g], wcache_all.at[:, g], newwin_all.at[g], selb_all.at[g],
                              winb_all.at[g], newb_all.at[g], o_all.at[g], n_past_blk)


def _nsa_sample_att_group(b, g, idx_ref, page_refs, q_ref, gate_ref, oc_ref, newsel_ref, wcache_ref, newwin_ref,
                          selb_ref, winb_ref, newb_ref, o_ref, n_past_blk):
    blocks_per_page = PAGE_SIZE // SEL_BLOCK
    q8 = (_pad_rows8(q_ref[...]) * (HEAD_DIM ** -0.5)).astype(BF16)
    new_bias = newb_ref[:, 0:1]

    def attend(s, v_t, new_kv):
        k_new = _bf16_round(new_kv[0:1])
        v_new = _bf16_round(new_kv[1:2])
        s_new = jnp.sum(q8.astype(F32) * k_new, axis=-1, keepdims=True) + new_bias
        m = jnp.maximum(jnp.max(s, axis=-1, keepdims=True), s_new)
        e = jnp.exp(s - m)
        e_new = jnp.exp(s_new - m)
        den = jnp.sum(e, axis=-1, keepdims=True) + e_new
        return _dot_nt((e / den).astype(BF16), v_t) + _bf16_round(e_new / den) * v_new

    lane = lax.broadcasted_iota(jnp.int32, (8, PAGE_SIZE), 1)
    near = selb_ref[...]
    scores = []
    for k in range(SEL_TOPK):
        ik = idx_ref[(b * A_KV + g) * SEL_TOPK + k]
        blk = jnp.minimum(ik, n_past_blk - 1)
        s_k = jnp.dot(q8, page_refs[k][0].astype(BF16), preferred_element_type=F32)
        s_k = s_k + jnp.where(blk // blocks_per_page == (n_past_blk - 1) // blocks_per_page, near, 0.0)
        keep = (lane // SEL_BLOCK == blk % blocks_per_page) & (ik < n_past_blk)
        scores.append(jnp.where(keep, s_k, NEG_INF))
    v_sel = jnp.concatenate([page_refs[k][1].astype(BF16) for k in range(SEL_TOPK)], axis=1)
    o_s = attend(jnp.concatenate(scores, axis=1), v_sel, newsel_ref[...])

    w_len = wcache_ref.shape[-1]
    wcol = lax.broadcasted_iota(jnp.int32, (8, w_len), 1)
    s_w = jnp.dot(q8, wcache_ref[0].astype(BF16), preferred_element_type=F32) + winb_ref[...]
    o_w = attend(jnp.where(wcol >= 1, s_w, NEG_INF), wcache_ref[1].astype(BF16), newwin_ref[...])
    gates = gate_ref[...]
    o_ref[...] = gates[:, 0:1] * oc_ref[...] + gates[:, 1:2] * o_s[0:A_REP] + gates[:, 2:3] * o_w[0:A_REP]


def _rows_last(cache):
    nd = cache.ndim
    return jnp.transpose(cache, tuple(range(nd - 4)) + (nd - 3, nd - 2, nd - 1, nd - 4))


def _nsa_sample(q, gates, new_sel, new_win, pool_cmp, pool_sel, win_cache, page_table, cmp_weights, tab):
    n, n_pages = page_table.shape
    past = n_pages * PAGE_SIZE
    n_past_blk = past // SEL_BLOCK
    n_sel = n_past_blk + 1
    n_chunks = n_pages * CHUNKS_PER_PAGE
    w_len = win_cache.shape[1]
    blocks_per_page = PAGE_SIZE // SEL_BLOCK
    assert n_sel <= SEL_PAD and w_len == WIN_A and past >= WIN_A and blocks_per_page == 2
    pt_flat = page_table.reshape(-1)

    c_idx = np.arange(n_chunks)
    s_start = np.arange(SEL_PAD) * SEL_BLOCK
    cover = ((c_idx[:, None] * CMP_STRIDE < s_start[None, :] + SEL_BLOCK)
             & (c_idx[:, None] * CMP_STRIDE + CMP_LEN - 1 >= s_start[None, :])
             & (c_idx[:, None] < n_chunks - 1) & (np.arange(SEL_PAD)[None, :] < n_sel))
    cover = jnp.asarray(cover.astype(np.float32), BF16)

    head_spec = lambda last: pl.BlockSpec((None, A_KV, A_REP, last), lambda i, *_: (i, 0, 0, 0))
    page_specs = [pl.BlockSpec((None, 2, A_KV, HEAD_DIM, PAGE_SIZE),
                               functools.partial(lambda i, pt, p: (pt[i * n_pages + p], 0, 0, 0, 0), p=p))
                  for p in range(n_pages)]
    full = lambda shape: pl.BlockSpec(shape, lambda i, *_: (0,) * len(shape))
    pool_c = _rows_last(pool_cmp)
    o_c, idx = pl.pallas_call(
        functools.partial(_nsa_sample_cmp_body, n_pages=n_pages, n_sel=n_sel),
        grid_spec=pltpu.PrefetchScalarGridSpec(
            num_scalar_prefetch=1,
            grid=(n,),
            in_specs=page_specs + [head_spec(HEAD_DIM)] + _compress_specs(full) + [full((n_chunks, SEL_PAD))],
            out_specs=[head_spec(HEAD_DIM), pl.BlockSpec((None, A_KV, 128), lambda i, *_: (i, 0, 0))],
            scratch_shapes=[pltpu.VMEM((2, 8, A_KVW), F32), pltpu.VMEM((2, past, A_KVW), F32)],
        ),
        out_shape=[jax.ShapeDtypeStruct((n, A_KV, A_REP, HEAD_DIM), F32),
                   jax.ShapeDtypeStruct((n, A_KV, 128), jnp.int32)],
        compiler_params=pltpu.CompilerParams(
            dimension_semantics=("arbitrary",), vmem_limit_bytes=V7X_VMEM_LIMIT_BYTES),
        name="nsa_sample_cmp",
    )(pt_flat, *([pool_c] * n_pages), q, *cmp_weights, cover)

    rel = lambda dist: jnp.pad(jnp.transpose(_bias_lookup(tab - tab[NUM_BUCKETS - 1], dist), (1, 2, 0)),
                               ((0, 0), (0, 8 - A_REP), (0, 0)))
    sel_bias = rel(past - (past - PAGE_SIZE + np.arange(PAGE_SIZE)))
    win_bias = rel(w_len - np.arange(w_len))
    new_bias = rel(np.zeros(128, np.int64))

    idx_flat = idx[:, :, :SEL_TOPK].reshape(-1)

    def page_map(i, idx_s, pt, g, slot):
        blk = jnp.minimum(idx_s[(i * A_KV + g) * SEL_TOPK + slot], n_past_blk - 1)
        return (pt[i * n_pages + blk // blocks_per_page], 0, g, 0, 0)

    slot_specs = [pl.BlockSpec((None, 2, None, HEAD_DIM, PAGE_SIZE), functools.partial(page_map, g=g, slot=s))
                  for g in range(A_KV) for s in range(SEL_TOPK)]
    per_head = lambda rows, last: pl.BlockSpec((None, A_KV, rows, last), lambda i, *_: (i, 0, 0, 0))
    per_group = lambda last: pl.BlockSpec((A_KV, 8, last), lambda i, *_: (0, 0, 0))
    return pl.pallas_call(
        functools.partial(_nsa_sample_att_body, n_past_blk=n_past_blk),
        grid_spec=pltpu.PrefetchScalarGridSpec(
            num_scalar_prefetch=2,
            grid=(n,),
            in_specs=slot_specs + [
                per_head(A_REP, HEAD_DIM), per_head(A_REP, 3), per_head(A_REP, HEAD_DIM), per_head(2, HEAD_DIM),
                pl.BlockSpec((None, 2, A_KV, HEAD_DIM, w_len), lambda i, *_: (i, 0, 0, 0, 0)),
                per_head(2, HEAD_DIM), per_group(PAGE_SIZE), per_group(w_len), per_group(128),
            ],
            out_specs=per_head(A_REP, HEAD_DIM),
        ),
        out_shape=jax.ShapeDtypeStruct((n, A_KV, A_REP, HEAD_DIM), F32),
        compiler_params=pltpu.CompilerParams(
            dimension_semantics=("arbitrary",), vmem_limit_bytes=V7X_VMEM_LIMIT_BYTES),
        name="nsa_sample_att",
    )(idx_flat, pt_flat, *([_rows_last(pool_sel)] * (A_KV * SEL_TOPK)), q, gates, o_c,
      jnp.transpose(new_sel, (0, 2, 1, 3)), _rows_last(win_cache), jnp.transpose(new_win, (0, 2, 1, 3)),
      sel_bias, win_bias, new_bias)


SWA_Q_TILE = WIN_C


def _swa_prompt_body(q_ref, k_ref, v_ref, btile_ref, sink_ref, o_ref, *, tq):
    qt = pl.program_id(2)
    q0 = pl.multiple_of(qt * tq, tq)
    q = q_ref[...]
    qs = jnp.concatenate([q[:, r * HEAD_DIM:(r + 1) * HEAD_DIM] for r in range(C_REP)], axis=0)
    qs = (qs * (HEAD_DIM ** -0.5)).astype(BF16)
    prev0 = pl.multiple_of(jnp.maximum(q0 - tq, 0), tq)
    k = jnp.concatenate([k_ref[pl.ds(prev0, tq), :], k_ref[pl.ds(q0, tq), :]], axis=0)
    v = jnp.concatenate([v_ref[pl.ds(prev0, tq), :], v_ref[pl.ds(q0, tq), :]], axis=0)
    s = _dot_nt(k, qs) + btile_ref[...]
    sinks = sink_ref[...]
    sink = jnp.concatenate([sinks[r:r + 1, :] for r in range(C_REP)], axis=1)
    m = jnp.maximum(jnp.max(s, axis=0, keepdims=True), sink)
    e = jnp.exp(s - m)
    p = e / (jnp.sum(e, axis=0, keepdims=True) + jnp.exp(sink - m))
    o = lax.dot_general(v, p.astype(BF16), (((0,), (0,)), ((), ())), preferred_element_type=F32)
    o_ref[...] = jnp.concatenate([o[:, r * tq:(r + 1) * tq].T for r in range(C_REP)], axis=1)


def _swa_prompt(q, kvt, sinks, tab):
    b, t, _ = q.shape
    tq = SWA_Q_TILE
    btile = jnp.swapaxes(_near_bias_tiles(tab, tq, WIN_C), -1, -2)
    kv_spec = lambda slot: pl.BlockSpec((None, None, t, HEAD_DIM), lambda i, g, j: (i, slot + g, 0, 0))
    qo_spec = pl.BlockSpec((None, tq, C_REP * HEAD_DIM), lambda i, g, j: (i, j, g))
    sink_lanes = jnp.broadcast_to(sinks.reshape(C_KV, C_REP, 1), (C_KV, C_REP, 128))
    return pl.pallas_call(
        functools.partial(_swa_prompt_body, tq=tq),
        grid=(b, C_KV, t // tq),
        in_specs=[qo_spec, kv_spec(0), kv_spec(C_KV),
                  pl.BlockSpec((None, None, 2 * tq, C_REP * tq), lambda i, g, j: (jnp.minimum(j, 1), g, 0, 0)),
                  pl.BlockSpec((None, C_REP, 128), lambda i, g, j: (g, 0, 0))],
        out_specs=qo_spec,
        out_shape=jax.ShapeDtypeStruct((b, t, C_HEADS * HEAD_DIM), F32),
        compiler_params=pltpu.CompilerParams(
            dimension_semantics=("parallel", "parallel", "arbitrary"), vmem_limit_bytes=V7X_VMEM_LIMIT_BYTES),
        name="swa_prompt",
    )(q, kvt, kvt, btile, sink_lanes)


def _swa_sample_body(q_ref, cache_ref, new_ref, bias_ref, newb_ref, sink_ref, o_ref):
    w_len = cache_ref.shape[-1]
    wcol = lax.broadcasted_iota(jnp.int32, (C_REP, w_len), 1)
    for g in range(C_KV):
        q8 = (q_ref[g] * (HEAD_DIM ** -0.5)).astype(BF16)
        s = jnp.dot(q8, cache_ref[0, g].astype(BF16), preferred_element_type=F32) + bias_ref[g]
        s = jnp.where(wcol >= 1, s, NEG_INF)
        k_new = _bf16_round(new_ref[g, 0:1])
        v_new = _bf16_round(new_ref[g, 1:2])
        s_new = jnp.sum(q8.astype(F32) * k_new, axis=-1, keepdims=True) + newb_ref[g][:, 0:1]
        sink = sink_ref[g][:, 0:1]
        m = jnp.maximum(jnp.maximum(jnp.max(s, axis=-1, keepdims=True), s_new), sink)
        e = jnp.exp(s - m)
        e_new = jnp.exp(s_new - m)
        den = jnp.sum(e, axis=-1, keepdims=True) + e_new + jnp.exp(sink - m)
        o_ref[g] = _dot_nt((e / den).astype(BF16), cache_ref[1, g].astype(BF16)) + _bf16_round(e_new / den) * v_new


def _swa_sample(q, cache, new_kv, sinks, tab):
    n, w_len = cache.shape[:2]
    bias = jnp.transpose(_bias_lookup(tab, w_len - np.arange(w_len)), (1, 2, 0))
    lanes = lambda x: jnp.broadcast_to(x[:, :, None], (C_KV, C_REP, 128))
    full = lambda shape: pl.BlockSpec(shape, lambda i: (0,) * len(shape))
    return pl.pallas_call(
        _swa_sample_body,
        grid=(n,),
        in_specs=[
            pl.BlockSpec((None, C_KV, C_REP, HEAD_DIM), lambda i: (i, 0, 0, 0)),
            pl.BlockSpec((None, 2, C_KV, HEAD_DIM, w_len), lambda i: (i, 0, 0, 0, 0)),
            pl.BlockSpec((None, C_KV, 2, HEAD_DIM), lambda i: (i, 0, 0, 0)),
            full((C_KV, C_REP, w_len)), full((C_KV, C_REP, 128)), full((C_KV, C_REP, 128)),
        ],
        out_specs=pl.BlockSpec((None, C_KV, C_REP, HEAD_DIM), lambda i: (i, 0, 0, 0)),
        out_shape=jax.ShapeDtypeStruct((n, C_KV, C_REP, HEAD_DIM), F32),
        compiler_params=pltpu.CompilerParams(
            dimension_semantics=("arbitrary",), vmem_limit_bytes=V7X_VMEM_LIMIT_BYTES),
        name="swa_sample",
    )(q, _rows_last(cache), new_kv, bias, lanes(tab[0]), lanes(sinks.reshape(C_KV, C_REP)))


PROJ_ROW_TILE = 512
A_IN_PAD = 1920
A_U_COL = A_Q + 6 * A_KVW
A_GATE_COL = A_U_COL + B_WIDTH


def _heads_first(kvt_ref, slot, k, v, n_kv):
    for g in range(n_kv):
        kvt_ref[slot + g] = k[:, g * HEAD_DIM:(g + 1) * HEAD_DIM].astype(BF16)
        kvt_ref[slot + n_kv + g] = v[:, g * HEAD_DIM:(g + 1) * HEAD_DIM].astype(BF16)


def _inproj_a_body(x_ref, gain_ref, w_ref, qg_ref, kg_ref,
                   q_ref, cmp_ref, sel_ref, win_ref, gate_ref, u_ref, kvt_ref):
    xn = _rms_rows(x_ref[...], gain_ref[...]).astype(BF16)
    z = jnp.dot(xn, w_ref[...], preferred_element_type=F32)
    for j in range(A_Q // 128):
        q_ref[:, j * 128:(j + 1) * 128] = _rms_heads128(z[:, j * 128:(j + 1) * 128], qg_ref[...])
    for out_ref, off, slot in ((cmp_ref, A_Q, None), (sel_ref, A_Q + 2 * A_KVW, 0), (win_ref, A_Q + 4 * A_KVW, 4)):
        k = _rms_heads128(z[:, off:off + A_KVW], kg_ref[...])
        v = z[:, off + A_KVW:off + 2 * A_KVW]
        out_ref[:, :A_KVW] = k
        out_ref[:, A_KVW:] = v
        if slot is not None:
            _heads_first(kvt_ref, slot, k, v, A_KV)
    u_ref[...] = z[:, A_U_COL:A_GATE_COL]
    gate_ref[...] = 1.0 / (1.0 + jnp.exp(-z[:, A_GATE_COL:A_IN_PAD]))


def _inproj_a(x, gain, w_in, q_gain, k_gain):
    b, t, d = x.shape
    tm = min(t, PROJ_ROW_TILE)
    w = jnp.concatenate([w_in[:, :A_U_COL], w_in[:, A_U_COL + A_GATE:], w_in[:, A_U_COL:A_U_COL + A_GATE],
                         jnp.zeros((d, A_IN_PAD - A_GATE_COL - A_GATE), F32)], axis=1).astype(BF16)
    tile2 = lambda g: jnp.tile(g, 2).reshape(1, 128)
    rows = lambda width: pl.BlockSpec((None, tm, width), lambda i, j: (i, j, 0))
    shape = lambda width: jax.ShapeDtypeStruct((b, t, width), F32)
    return pl.pallas_call(
        _inproj_a_body,
        grid=(b, t // tm),
        in_specs=[rows(d), _const_spec((1, d)), _const_spec((d, A_IN_PAD)), _const_spec((1, 128)), _const_spec((1, 128))],
        out_specs=[rows(A_Q), rows(2 * A_KVW), rows(2 * A_KVW), rows(2 * A_KVW), rows(128), rows(B_WIDTH),
                   pl.BlockSpec((None, 8, tm, HEAD_DIM), lambda i, j: (i, 0, j, 0))],
        out_shape=[shape(A_Q), shape(2 * A_KVW), shape(2 * A_KVW), shape(2 * A_KVW), shape(128), shape(B_WIDTH),
                   jax.ShapeDtypeStruct((b, 8, t, HEAD_DIM), BF16)],
        compiler_params=pltpu.CompilerParams(
            dimension_semantics=("parallel", "parallel"), vmem_limit_bytes=V7X_VMEM_LIMIT_BYTES),
        name="inproj_nsa_s5",
    )(x, gain.reshape(1, d), w, tile2(q_gain), tile2(k_gain))


def _inproj_c_body(x_ref, gain_ref, w_ref, qg_ref, kg_ref, q_ref, kv_ref, kvt_ref):
    xn = _rms_rows(x_ref[...], gain_ref[...]).astype(BF16)
    z = jnp.dot(xn, w_ref[...], preferred_element_type=F32)
    n_q = C_HEADS * HEAD_DIM
    for j in range(n_q // 128):
        q_ref[:, j * 128:(j + 1) * 128] = _rms_heads128(z[:, j * 128:(j + 1) * 128], qg_ref[...])
    k = _rms_heads128(z[:, n_q:n_q + C_KV * HEAD_DIM], kg_ref[...])
    v = z[:, n_q + C_KV * HEAD_DIM:]
    kv_ref[:, :C_KV * HEAD_DIM] = k
    kv_ref[:, C_KV * HEAD_DIM:] = v
    _heads_first(kvt_ref, 0, k, v, C_KV)


def _inproj_c(x, gain, w_in, q_gain, k_gain):
    b, t, d = x.shape
    tm = min(t, PROJ_ROW_TILE)
    n_in = w_in.shape[1]
    tile2 = lambda g: jnp.tile(g, 2).reshape(1, 128)
    rows = lambda width: pl.BlockSpec((None, tm, width), lambda i, j: (i, j, 0))
    shape = lambda width: jax.ShapeDtypeStruct((b, t, width), F32)
    return pl.pallas_call(
        _inproj_c_body,
        grid=(b, t // tm),
        in_specs=[rows(d), _const_spec((1, d)), _const_spec((d, n_in)), _const_spec((1, 128)), _const_spec((1, 128))],
        out_specs=[rows(C_HEADS * HEAD_DIM), rows(2 * C_KV * HEAD_DIM),
                   pl.BlockSpec((None, 2 * C_KV, tm, HEAD_DIM), lambda i, j: (i, 0, j, 0))],
        out_shape=[shape(C_HEADS * HEAD_DIM), shape(2 * C_KV * HEAD_DIM),
                   jax.ShapeDtypeStruct((b, 2 * C_KV, t, HEAD_DIM), BF16)],
        compiler_params=pltpu.CompilerParams(
            dimension_semantics=("parallel", "parallel"), vmem_limit_bytes=V7X_VMEM_LIMIT_BYTES),
        name="inproj_swa",
    )(x, gain.reshape(1, d), w_in.astype(BF16), tile2(q_gain), tile2(k_gain))


FFN_ROW_TILE = 512
FFN_COL_CHUNK = 1408


def _mixer_residual(y_ref, x_ref, mix_refs, wout_refs):
    y_ref[...] = x_ref[...]
    for m_ref, w_ref in zip(mix_refs, wout_refs):
        y_ref[...] += jnp.dot(m_ref[...].astype(BF16), w_ref[...], preferred_element_type=F32)
    return y_ref[...]


def _tail_prompt_body(*refs, n_mix, tm, ffc):
    x_ref = refs[0]
    mix_refs = refs[1:1 + n_mix]
    wout_refs = refs[1 + n_mix:1 + 2 * n_mix]
    (gain_ref, prev_ref, wup_ref, wgate_ref, cw_ref, cb_ref, wdown_ref, y_ref, cs_ref, hbuf_ref) = refs[1 + 2 * n_mix:]
    t = pl.program_id(1)
    xn = _rms_rows(_mixer_residual(y_ref, x_ref, mix_refs, wout_refs), gain_ref[...]).astype(BF16)
    for c in range(D_FF // ffc):
        lo = c * ffc
        h = jnp.dot(xn, wup_ref[:, lo:lo + ffc], preferred_element_type=F32)
        g = jnp.dot(xn, wgate_ref[:, lo:lo + ffc], preferred_element_type=F32)

        @pl.when(t == 0)
        def _():
            hbuf_ref[c, 6:8, :] = prev_ref[:, lo:lo + ffc]

        hbuf_ref[c, 8:8 + tm, :] = h
        hm1 = hbuf_ref[c, 7:7 + tm, :]
        hm2 = hbuf_ref[c, 6:6 + tm, :]
        cw = cw_ref[:, lo:lo + ffc]
        hc = cw[0:1] * hm2 + cw[1:2] * hm1 + cw[2:3] * h + cb_ref[:, lo:lo + ffc]
        a = (_gelu_tanh(hc) * g).astype(BF16)
        y_ref[...] += jnp.dot(a, wdown_ref[lo:lo + ffc, :], preferred_element_type=F32)
        hbuf_ref[c, 0:8, :] = h[tm - 8:tm, :]
        cs_ref[:, lo:lo + ffc] = h[tm - 2:tm, :]


def _tail_prompt(x, mixes, wouts, gain, prev, wup, wgate, cw, cb, wdown):
    b, t, d = x.shape
    tm, ffc = FFN_ROW_TILE, FFN_COL_CHUNK
    rows = lambda width: pl.BlockSpec((None, tm, width), lambda i, j: (i, j, 0))
    state = pl.BlockSpec((None, CONV_W - 1, D_FF), lambda i, j: (i, 0, 0))
    return pl.pallas_call(
        functools.partial(_tail_prompt_body, n_mix=len(mixes), tm=tm, ffc=ffc),
        grid=(b, t // tm),
        in_specs=[rows(d)] + [rows(m.shape[-1]) for m in mixes] + [_const_spec(w.shape) for w in wouts] + [
            _const_spec((1, d)), state, _const_spec((d, D_FF)), _const_spec((d, D_FF)),
            _const_spec((CONV_W, D_FF)), _const_spec((1, D_FF)), _const_spec((D_FF, d))],
        out_specs=[rows(d), state],
        out_shape=[jax.ShapeDtypeStruct((b, t, d), F32), jax.ShapeDtypeStruct((b, CONV_W - 1, D_FF), F32)],
        scratch_shapes=[pltpu.VMEM((D_FF // ffc, 8 + tm, ffc), F32)],
        compiler_params=pltpu.CompilerParams(
            dimension_semantics=("parallel", "arbitrary"), vmem_limit_bytes=V7X_VMEM_LIMIT_BYTES),
        name="tail_prompt",
    )(x, *mixes, *wouts, gain, prev, wup, wgate, cw, cb, wdown)


def _tail_sample_body(*refs, n_mix, ffc):
    x_ref = refs[0]
    mix_refs = refs[1:1 + n_mix]
    wout_refs = refs[1 + n_mix:1 + 2 * n_mix]
    (gain_ref, prev_ref, wup_ref, wgate_ref, cw_ref, cb_ref, wdown_ref, y_ref, cs_ref) = refs[1 + 2 * n_mix:]
    xn = _rms_rows(_mixer_residual(y_ref, x_ref, mix_refs, wout_refs), gain_ref[...]).astype(BF16)
    for c in range(D_FF // ffc):
        lo = c * ffc
        h = jnp.dot(xn, wup_ref[:, lo:lo + ffc], preferred_element_type=F32)
        g = jnp.dot(xn, wgate_ref[:, lo:lo + ffc], preferred_element_type=F32)
        hm2 = prev_ref[:, lo:lo + ffc]
        hm1 = prev_ref[:, D_FF + lo:D_FF + lo + ffc]
        cw = cw_ref[:, lo:lo + ffc]
        hc = cw[0:1] * hm2 + cw[1:2] * hm1 + cw[2:3] * h + cb_ref[:, lo:lo + ffc]
        a = (_gelu_tanh(hc) * g).astype(BF16)
        y_ref[...] += jnp.dot(a, wdown_ref[lo:lo + ffc, :], preferred_element_type=F32)
        cs_ref[:, lo:lo + ffc] = hm1
        cs_ref[:, D_FF + lo:D_FF + lo + ffc] = h


def _tail_sample(x, mixes, wouts, gain, prev, wup, wgate, cw, cb, wdown):
    n, d = x.shape
    full = lambda shape: _const_spec(shape)
    return pl.pallas_call(
        functools.partial(_tail_sample_body, n_mix=len(mixes), ffc=FFN_COL_CHUNK),
        grid=(1,),
        in_specs=[full((n, d))] + [full(m.shape) for m in mixes] + [full(w.shape) for w in wouts] + [
            full((1, d)), full((n, (CONV_W - 1) * D_FF)), full((d, D_FF)), full((d, D_FF)),
            full((CONV_W, D_FF)), full((1, D_FF)), full((D_FF, d))],
        out_specs=[pl.BlockSpec((n, d), lambda i: (0, 0)), pl.BlockSpec((n, (CONV_W - 1) * D_FF), lambda i: (0, 0))],
        out_shape=[jax.ShapeDtypeStruct((n, d), F32), jax.ShapeDtypeStruct((n, (CONV_W - 1) * D_FF), F32)],
        compiler_params=pltpu.CompilerParams(
            dimension_semantics=("arbitrary",), vmem_limit_bytes=V7X_VMEM_LIMIT_BYTES),
        name="tail_sample",
    )(x, *mixes, *wouts, gain, prev, wup, wgate, cw, cb, wdown)


def kernel(x_prompt, x_sample, cache_nsa_cmp, cache_nsa_sel, cache_nsa_win, state_s5_re, state_s5_im,
           cache_swa, state_ffn_conv, page_table, rel_bias, norm_mix, norm_ffn, a_w_in, a_w_out,
           nsa_q_gain, nsa_k_gain, nsa_cmp_pos, nsa_cmp_w1, nsa_cmp_w2, s5_a_re, s5_a_im, s5_log_dt,
           s5_b_re, s5_b_im, s5_c_re, s5_c_im, s5_d, s5_w_glu, s5_b_glu, c_w_in, c_w_out, c_q_gain,
           c_k_gain, c_sinks, ffn_w_up, ffn_w_gate, ffn_conv_w, ffn_conv_b, ffn_w_down):
    bp, tp, _ = x_prompt.shape
    bs, ts, _ = x_sample.shape
    assert ts == 1 and DEPTH == 2
    tab_a = rel_bias[:, :A_HEADS].reshape(NUM_BUCKETS, A_KV, A_REP)
    tab_c = rel_bias[:, :C_HEADS].reshape(NUM_BUCKETS, C_KV, C_REP)
    kv6 = lambda x: x.reshape(x.shape[:-1] + (2, x.shape[-1] // (2 * HEAD_DIM), HEAD_DIM))
    hp, hs = x_prompt, x_sample.reshape(1, bs, D_MODEL)
    conv_p, conv_s = [], []

    def tail(layer, hp, hs, mixes_p, mixes_s, wouts):
        wouts = [w.astype(BF16) for w in wouts]
        ffn = (norm_ffn[layer].reshape(1, D_MODEL),)
        wts = (ffn_w_up[layer].astype(BF16), ffn_w_gate[layer].astype(BF16), ffn_conv_w[layer],
               ffn_conv_b[layer].reshape(1, D_FF), ffn_w_down[layer].astype(BF16))
        hp, cp = _tail_prompt(hp, mixes_p, wouts, *ffn, jnp.zeros((bp, CONV_W - 1, D_FF), F32), *wts)
        hs2, cs = _tail_sample(hs[0], [m.reshape(bs, -1) for m in mixes_s], wouts, *ffn,
                               state_ffn_conv[layer].reshape(bs, (CONV_W - 1) * D_FF), *wts)
        conv_p.append(cp)
        conv_s.append(cs.reshape(bs, CONV_W - 1, D_FF))
        return hp, hs2.reshape(1, bs, D_MODEL)

    proj = (norm_mix[0], a_w_in[0], nsa_q_gain[0], nsa_k_gain[0])
    qp, cmp_p, sel_p, win_p, gate_p, up, kvt_p = _inproj_a(hp, *proj)
    qs, cmp_s, sel_s, win_s, gate_s, us, _ = _inproj_a(hs, *proj)
    cmp_w = _compress_weights(nsa_cmp_pos[0], nsa_cmp_w1[0], nsa_cmp_w2[0], nsa_k_gain[0])
    kc, vc = _cmp_prompt(cmp_p, cmp_w)
    o_ap = _nsa_prompt_attend(qp, gate_p, kc, vc, kvt_p, tab_a)
    o_as = _nsa_sample(qs.reshape(bs, A_KV, A_REP, HEAD_DIM), gate_s[0, :, :A_GATE].reshape(bs, A_KV, A_REP, 3),
                       sel_s.reshape(bs, 2, A_KV, HEAD_DIM), win_s.reshape(bs, 2, A_KV, HEAD_DIM),
                       cache_nsa_cmp[0], cache_nsa_sel[0], cache_nsa_win[0], page_table, cmp_w, tab_a)
    s5w = _s5_weights(s5_a_re[0], s5_a_im[0], s5_log_dt[0], s5_b_re[0], s5_b_im[0], s5_c_re[0], s5_c_im[0],
                      s5_d[0], s5_w_glu[0], s5_b_glu[0])
    h0 = jnp.zeros((bp, S5_NS), F32)
    o_bp, hr_p, hi_p = _s5_mix(up, h0, h0, s5w)
    o_bs, hr_s, hi_s = _s5_mix(us.reshape(bs, 1, B_WIDTH), state_s5_re[0].reshape(bs, S5_NS),
                               state_s5_im[0].reshape(bs, S5_NS), s5w)
    hp, hs = tail(0, hp, hs, [o_ap, o_bp], [o_as, o_bs], [a_w_out[0][:A_Q], a_w_out[0][A_Q:]])

    proj = (norm_mix[1], c_w_in[0], c_q_gain[0], c_k_gain[0])
    qcp, kv_p, kvt_c = _inproj_c(hp, *proj)
    qcs, kv_s, _ = _inproj_c(hs, *proj)
    o_cp = _swa_prompt(qcp, kvt_c, c_sinks[0], tab_c)
    new_kv = kv_s.reshape(bs, 2, C_KV, HEAD_DIM)
    o_cs = _swa_sample(qcs.reshape(bs, C_KV, C_REP, HEAD_DIM), cache_swa[0], jnp.transpose(new_kv, (0, 2, 1, 3)),
                       c_sinks[0], tab_c)
    hp, hs = tail(1, hp, hs, [o_cp], [o_cs], [c_w_out[0]])

    state = lambda x, n: x.reshape(1, n, B_GROUPS, B_STATE)
    nsa_win_s = jnp.concatenate([cache_nsa_win[0][:, 1:], kv6(win_s).reshape(bs, 1, 2, A_KV, HEAD_DIM)], axis=1)
    swa_s = jnp.concatenate([cache_swa[0][:, 1:], new_kv.reshape(bs, 1, 2, C_KV, HEAD_DIM)], axis=1)
    return (hp, hs.reshape(bs, ts, D_MODEL),
            kv6(cmp_p)[None], kv6(cmp_s).reshape(1, bs, ts, 2, A_KV, HEAD_DIM),
            kv6(sel_p)[None], kv6(sel_s).reshape(1, bs, ts, 2, A_KV, HEAD_DIM),
            kv6(win_p)[None, :, -min(WIN_A, tp):], nsa_win_s[None],
            state(hr_p, bp), state(hi_p, bp), state(hr_s, bs), state(hi_s, bs),
            kv6(kv_p)[None, :, -min(WIN_C, tp):], swa_s[None],
            jnp.stack(conv_p), jnp.stack(conv_s))
```

```python
import functools
import math

import jax
import jax.numpy as jnp
import numpy as np
from jax import lax
from jax.experimental import pallas as pl
from jax.experimental.pallas import tpu as pltpu

D_MODEL = 1024
DEPTH = 2
PAGE_SIZE = 128
HEAD_DIM = 64
A_HEADS = 8
A_KV = 2
A_REP = A_HEADS // A_KV
A_Q = A_HEADS * HEAD_DIM
A_KVW = A_KV * HEAD_DIM
A_GATE = 3 * A_HEADS
CMP_LEN = 32
CMP_STRIDE = 16
SEL_BLOCK = 64
SEL_TOPK = 16
WIN_A = 512
NSA_QBLK = 64
FORCE_BONUS = 1000.0
B_WIDTH = D_MODEL // 2
B_GROUP = 16
B_GROUPS = B_WIDTH // B_GROUP
B_STATE = 64
C_HEADS = D_MODEL // HEAD_DIM
C_KV = 2
C_REP = C_HEADS // C_KV
WIN_C = 128
NUM_BUCKETS = 32
MAX_DISTANCE = 128
D_FF = 2816
CONV_W = 3
EPS = 1e-6

F32 = jnp.float32
BF16 = jnp.bfloat16

V7X_VMEM_LIMIT_BYTES = 56 * 1024 * 1024


def _gelu_tanh(x):
    return 0.5 * x * (1.0 + jnp.tanh(math.sqrt(2.0 / math.pi) * (x + 0.044715 * (x * x * x))))


def _rms_rows(x, gain):
    return x * lax.rsqrt(jnp.mean(x * x, axis=-1, keepdims=True) + EPS) * gain


def _const_spec(shape):
    zeros = (0,) * len(shape)
    return pl.BlockSpec(shape, lambda *_: zeros, pipeline_mode=pl.Buffered(1))


NSA_Q_TILE = 128
NSA_FAR_TILE = 512
NSA_NEAR = 2 * NSA_Q_TILE
NEG_INF = float("-inf")


def _dot_nt(a, b):
    return lax.dot_general(a, b, (((1,), (1,)), ((), ())), preferred_element_type=F32)


def _dot_tn(a, b):
    return lax.dot_general(a, b, (((0,), (0,)), ((), ())), preferred_element_type=F32)


def _softmax_start(s, v):
    m = jnp.max(s, axis=0, keepdims=True)
    e = jnp.exp(s - m)
    return m, jnp.sum(e, axis=0, keepdims=True), _dot_tn(v, e.astype(BF16))


def _softmax_more(carry, s, v):
    m, l, acc = carry
    m_new = jnp.maximum(m, jnp.max(s, axis=0, keepdims=True))
    alpha = jnp.exp(m - m_new)
    e = jnp.exp(s - m_new)
    return m_new, alpha * l + jnp.sum(e, axis=0, keepdims=True), alpha * acc + _dot_tn(v, e.astype(BF16))


def _nsa_prompt_body(q_ref, gate_ref, kc_ref, vc_ref, kvt_ref, btile_ref, cover_ref, o_ref, sel_scr, *, tq):
    width = A_REP * HEAD_DIM
    gates_t = gate_ref[...].T
    for g in range(A_KV):
        _nsa_prompt_group(g, q_ref.at[:, g * width:(g + 1) * width], gates_t, kc_ref.at[g], vc_ref.at[g],
                          kvt_ref.at[g], kvt_ref.at[A_KV + g], kvt_ref.at[2 * A_KV + g], kvt_ref.at[3 * A_KV + g],
                          btile_ref.at[g], cover_ref, o_ref.at[:, g * width:(g + 1) * width], sel_scr.at[g], tq)


def _nsa_prompt_group(g, q_ref, gates_t, kc_ref, vc_ref, ks_ref, vs_ref, kw_ref, vw_ref, btile_ref, cover_ref,
                      o_ref, sel_ref, tq):
    qt = pl.program_id(1)
    q0 = pl.multiple_of(qt * tq, tq)
    cols = A_REP * tq
    q = q_ref[...]
    qs = jnp.concatenate([q[:, r * HEAD_DIM:(r + 1) * HEAD_DIM] for r in range(A_REP)], axis=0)
    qs = (qs * (HEAD_DIM ** -0.5)).astype(BF16)

    def q_pos(height):
        return q0 + (lax.broadcasted_iota(jnp.int32, (height, cols), 1) & (tq - 1))

    def key_idx(height):
        return lax.broadcasted_iota(jnp.int32, (height, cols), 0)

    n_idx = key_idx(128)
    valid_c = (n_idx * CMP_STRIDE + (CMP_LEN - 1) <= q_pos(128)) & (n_idx < 127)
    s_c = jnp.where(valid_c, _dot_nt(kc_ref[...], qs), NEG_INF)
    m_c = jnp.max(s_c, axis=0, keepdims=True)
    m_c = jnp.where(m_c == NEG_INF, 0.0, m_c)
    e_c = jnp.exp(s_c - m_c)
    d_c = jnp.sum(e_c, axis=0, keepdims=True)
    p_c = (e_c / jnp.where(d_c > 0, d_c, 1.0)).astype(BF16)
    o_c = _dot_tn(vc_ref[...], p_c)
    imp_heads = jnp.dot(cover_ref[...], p_c, preferred_element_type=F32)
    imp = sum(imp_heads[0:32, r * tq:(r + 1) * tq] for r in range(A_REP))
    s_idx = lax.broadcasted_iota(jnp.int32, (32, tq), 0)
    qblk = (q0 + lax.broadcasted_iota(jnp.int32, (32, tq), 1)) >> 6
    forced = (s_idx == 0) | (s_idx == qblk) | (s_idx == qblk - 1)
    allowed = s_idx <= qblk
    score = jnp.where(allowed, imp + jnp.where(forced, FORCE_BONUS, 0.0), NEG_INF)
    rank = jnp.zeros((32, tq), F32)
    for j in range(32):
        other = score[j:j + 1, :]
        beats = (other > score) | ((other == score) & (s_idx > j))
        rank = rank + jnp.where(beats, 1.0, 0.0)
    sel = jnp.where((rank < SEL_TOPK) & allowed, 1.0, 0.0)
    sel_ref[...] = jnp.concatenate([sel] * A_REP, axis=1)

    def block_mask(k0, n_blocks):
        first = k0 // SEL_BLOCK
        return jnp.concatenate([jnp.broadcast_to(sel_ref[pl.ds(first + j, 1), :], (SEL_BLOCK, cols))
                                for j in range(n_blocks)], axis=0) > 0.5

    prev0 = pl.multiple_of(jnp.maximum(q0 - tq, 0), tq)
    btile = btile_ref[...]

    def near(k_ref, v_ref, extra_mask):
        k = jnp.concatenate([k_ref[pl.ds(prev0, tq), :], k_ref[pl.ds(q0, tq), :]], axis=0)
        v = jnp.concatenate([v_ref[pl.ds(prev0, tq), :], v_ref[pl.ds(q0, tq), :]], axis=0)
        s = _dot_nt(k, qs) + btile
        return _softmax_start(s if extra_mask is None else jnp.where(extra_mask, s, NEG_INF), v)

    sel_near = jnp.concatenate([block_mask(prev0, tq // SEL_BLOCK), block_mask(q0, tq // SEL_BLOCK)], axis=0)

    far_end = q0 - tq

    def sel_far(i, carry):
        k0 = pl.multiple_of(i * NSA_FAR_TILE, NSA_FAR_TILE)
        s = _dot_nt(ks_ref[pl.ds(k0, NSA_FAR_TILE), :], qs)
        mask = block_mask(k0, NSA_FAR_TILE // SEL_BLOCK) & (k0 + key_idx(NSA_FAR_TILE) < far_end)
        return _softmax_more(carry, jnp.where(mask, s, NEG_INF), vs_ref[pl.ds(k0, NSA_FAR_TILE), :])

    n_far = (jnp.maximum(far_end, 0) + NSA_FAR_TILE - 1) // NSA_FAR_TILE
    _, l_s, acc_s = lax.fori_loop(0, n_far, sel_far, near(ks_ref, vs_ref, sel_near))
    o_s = acc_s / l_s

    w_far = WIN_A - tq
    wf0 = pl.multiple_of(jnp.maximum(q0 - WIN_A, 0), tq)
    wpos = wf0 + key_idx(w_far)
    wmask = (q_pos(w_far) - wpos < WIN_A) & (wpos < far_end)
    s_w = jnp.where(wmask, _dot_nt(kw_ref[pl.ds(wf0, w_far), :], qs), NEG_INF)
    _, l_w, acc_w = _softmax_more(near(kw_ref, vw_ref, None), s_w, vw_ref[pl.ds(wf0, w_far), :])
    o_w = acc_w / l_w

    outs = []
    for r in range(A_REP):
        sl = slice(r * tq, (r + 1) * tq)
        row = 3 * (g * A_REP + r)
        out_t = (gates_t[row:row + 1] * o_c[:, sl] + gates_t[row + 1:row + 2] * o_s[:, sl]
                 + gates_t[row + 2:row + 3] * o_w[:, sl])
        outs.append(out_t.T)
    o_ref[...] = jnp.concatenate(outs, axis=1)


def _bucket_np(dist):
    n = np.maximum(dist, 0)
    exact = NUM_BUCKETS // 2
    nf = np.maximum(n, exact).astype(np.float64)
    large = exact + (np.log(nf / exact) / math.log(MAX_DISTANCE / exact) * (NUM_BUCKETS - exact)).astype(np.int64)
    return np.where(n < exact, n, np.minimum(large, NUM_BUCKETS - 1)).astype(np.int32)


def _bias_lookup(tab, dist):
    bucket = _bucket_np(np.asarray(dist))
    onehot = (jnp.asarray(bucket.reshape(-1, 1)) == jnp.arange(NUM_BUCKETS)[None, :]).astype(F32)
    flat = jnp.dot(onehot, tab.reshape(NUM_BUCKETS, -1), precision=lax.Precision.HIGHEST)
    return flat.reshape(bucket.shape + tab.shape[1:])


def _near_bias_tiles(tab, tq, window):
    i = np.arange(tq)[:, None]
    j = np.arange(2 * tq)[None, :]
    dist = tq + i - j
    ok = (dist >= 0) & (dist < window)
    bias = jnp.transpose(_bias_lookup(tab, dist), (2, 3, 0, 1))
    tiles = jnp.stack([jnp.where(jnp.asarray(ok & (j >= tq)), bias, NEG_INF), jnp.where(jnp.asarray(ok), bias, NEG_INF)])
    return tiles.reshape(2, tab.shape[1], tab.shape[2] * tq, 2 * tq)


def _cover_matrix(n_cmp_pad, n_sel):
    n = np.arange(n_cmp_pad)
    c_start = n * CMP_STRIDE
    c_end = c_start + CMP_LEN - 1
    s_start = np.arange(128) * SEL_BLOCK
    cover = (c_start[:, None] < s_start[None, :] + SEL_BLOCK) & (c_end[:, None] >= s_start[None, :])
    cover &= (np.arange(128)[None, :] < n_sel)
    return np.tile(cover.astype(np.float32), (A_REP, 1))


def _nsa_prompt_attend(q, gates, kc, vc, kvt, tab):
    b, t, _ = q.shape
    tq = NSA_Q_TILE
    assert t % NSA_FAR_TILE == 0 and t // SEL_BLOCK <= 32 and t >= WIN_A and kc.shape[2] == 128
    btile = jnp.swapaxes(_near_bias_tiles(tab - tab[NUM_BUCKETS - 1], tq, 2 * tq), -1, -2)
    cover = jnp.asarray(_cover_matrix(128, t // SEL_BLOCK)[:128].T, BF16)
    cmp_spec = pl.BlockSpec((None, A_KV, 128, HEAD_DIM), lambda i, j: (i, 0, 0, 0))
    return pl.pallas_call(
        functools.partial(_nsa_prompt_body, tq=tq),
        grid=(b, t // tq),
        in_specs=[
            pl.BlockSpec((None, tq, A_Q), lambda i, j: (i, j, 0)),
            pl.BlockSpec((None, tq, 128), lambda i, j: (i, j, 0)),
            cmp_spec, cmp_spec,
            pl.BlockSpec((None, 4 * A_KV, t, HEAD_DIM), lambda i, j: (i, 0, 0, 0)),
            pl.BlockSpec((None, A_KV, 2 * tq, A_REP * tq), lambda i, j: (jnp.minimum(j, 1), 0, 0, 0)),
            pl.BlockSpec((128, 128), lambda i, j: (0, 0)),
        ],
        out_specs=pl.BlockSpec((None, tq, A_Q), lambda i, j: (i, j, 0)),
        out_shape=jax.ShapeDtypeStruct((b, t, A_Q), F32),
        scratch_shapes=[pltpu.VMEM((A_KV, 32, A_REP * tq), F32)],
        compiler_params=pltpu.CompilerParams(
            dimension_semantics=("parallel", "arbitrary"), vmem_limit_bytes=V7X_VMEM_LIMIT_BYTES),
        name="nsa_prompt",
    )(q, gates, kc, vc, kvt, btile, cover)


S5_NS = B_GROUPS * B_STATE
S5_T_CHUNK = 64
S5_STRIP = 512


def _s5_body(u_ref, h0r_ref, h0i_ref, ar_ref, ai_ref, ldt_ref, wb_ref, wc_ref, d_ref, wglu_ref, bglu_ref,
             o_ref, hr_ref, hi_ref, coef_ref, st_ref, xbuf_ref, ubuf_ref, obuf_ref, *, nb, steps, interleave):
    c = pl.program_id(0)

    @pl.when(c == 0)
    def _():
        dt = jnp.exp(ldt_ref[...])
        ar, ai = ar_ref[...], ai_ref[...]
        mag = jnp.exp(ar * dt)
        abr, abi = mag * jnp.cos(ai * dt), mag * jnp.sin(ai * dt)
        den = ar * ar + ai * ai
        wr = ((abr - 1.0) * ar + abi * ai) / den
        wi = (abi * ar - (abr - 1.0) * ai) / den
        for k, val in enumerate((abr, abi, wr, wi)):
            coef_ref[k] = jnp.broadcast_to(val, (nb, S5_NS))
        st_ref[0] = h0r_ref[...]
        st_ref[1] = h0i_ref[...]

    if interleave:
        for b in range(nb):
            for j in range(B_WIDTH // 128):
                ubuf_ref.at[j][pl.ds(b, steps, stride=nb), :] = u_ref[b, :, j * 128:(j + 1) * 128]
        u = jnp.concatenate([ubuf_ref[j] for j in range(B_WIDTH // 128)], axis=1)
    else:
        u = u_ref[...]
    xbuf_ref[...] = jnp.dot(u.astype(BF16), wb_ref[...], preferred_element_type=F32)

    for lo in range(0, S5_NS, S5_STRIP):
        re = slice(lo, lo + S5_STRIP)
        im = slice(S5_NS + lo, S5_NS + lo + S5_STRIP)
        abr, abi, wr, wi = (coef_ref[k, :, re] for k in range(4))

        def step(t, carry):
            sr, si = carry
            r0 = pl.multiple_of(t * nb, nb)
            bur = xbuf_ref[pl.ds(r0, nb), re]
            bui = xbuf_ref[pl.ds(r0, nb), im]
            nsr = abr * sr - abi * si + (wr * bur - wi * bui)
            nsi = abr * si + abi * sr + (wr * bui + wi * bur)
            xbuf_ref[pl.ds(r0, nb), re] = nsr
            xbuf_ref[pl.ds(r0, nb), im] = nsi
            return nsr, nsi

        sr, si = lax.fori_loop(0, steps, step, (st_ref[0, :, re], st_ref[1, :, re]),
                               unroll=min(steps, 8))
        st_ref[0, :, re] = sr
        st_ref[1, :, re] = si

    y = jnp.dot(xbuf_ref[...].astype(BF16), wc_ref[...], preferred_element_type=F32) + d_ref[...] * u
    z = _gelu_tanh(y)
    gate = jnp.dot(z.astype(BF16), wglu_ref[...], preferred_element_type=F32) + bglu_ref[...]
    out = z * (1.0 / (1.0 + jnp.exp(-gate)))
    if interleave:
        for j in range(B_WIDTH // 128):
            obuf_ref[j] = out[:, j * 128:(j + 1) * 128]
        for b in range(nb):
            for j in range(B_WIDTH // 128):
                o_ref[b, :, j * 128:(j + 1) * 128] = obuf_ref.at[j][pl.ds(b, steps, stride=nb), :]
    else:
        o_ref[...] = out
    hr_ref[...] = st_ref[0]
    hi_ref[...] = st_ref[1]


def _s5_weights(a_re, a_im, log_dt, b_re, b_im, c_re, c_im, d, w_glu, b_glu):
    eye = jnp.eye(B_GROUPS, dtype=F32)
    blk_in = lambda w: jnp.einsum('hg,gpc->hcgp', eye, w).reshape(B_WIDTH, S5_NS)
    blk_out = lambda w: jnp.einsum('gh,gcp->gphc', eye, w).reshape(S5_NS, B_WIDTH)
    wb = jnp.concatenate([blk_in(b_re), blk_in(b_im)], axis=1).astype(BF16)
    wc = jnp.concatenate([blk_out(c_re), -blk_out(c_im)], axis=0).astype(BF16)
    flat = lambda x: x.reshape(1, S5_NS)
    return (flat(a_re), flat(a_im), flat(jnp.repeat(log_dt, B_STATE)), wb, wc, d.reshape(1, B_WIDTH),
            w_glu.astype(BF16), b_glu.reshape(1, B_WIDTH))


def _s5_mix(u, h_re, h_im, weights):
    nb, t, _ = u.shape
    interleave = t > 1
    steps = min(t, S5_T_CHUNK)
    rows = nb * steps
    body = functools.partial(_s5_body, nb=nb, steps=steps, interleave=interleave)
    if interleave:
        u_in = u
        u_spec = pl.BlockSpec((nb, steps, B_WIDTH), lambda c: (0, c, 0))
        o_shape = jax.ShapeDtypeStruct((nb, t, B_WIDTH), F32)
        scratch_rows = rows
    else:
        u_in = u.reshape(nb, B_WIDTH)
        u_spec = pl.BlockSpec((nb, B_WIDTH), lambda c: (0, 0))
        o_shape = jax.ShapeDtypeStruct((nb, B_WIDTH), F32)
        scratch_rows = 8
    o, hr, hi = pl.pallas_call(
        body,
        grid=(t // steps,),
        in_specs=[
            u_spec, _const_spec((nb, S5_NS)), _const_spec((nb, S5_NS)),
            _const_spec((1, S5_NS)), _const_spec((1, S5_NS)), _const_spec((1, S5_NS)),
            _const_spec((B_WIDTH, 2 * S5_NS)), _const_spec((2 * S5_NS, B_WIDTH)),
            _const_spec((1, B_WIDTH)), _const_spec((B_WIDTH, B_WIDTH)), _const_spec((1, B_WIDTH)),
        ],
        out_specs=[u_spec, pl.BlockSpec((nb, S5_NS), lambda c: (0, 0)), pl.BlockSpec((nb, S5_NS), lambda c: (0, 0))],
        out_shape=[o_shape, jax.ShapeDtypeStruct((nb, S5_NS), F32), jax.ShapeDtypeStruct((nb, S5_NS), F32)],
        scratch_shapes=[
            pltpu.VMEM((4, nb, S5_NS), F32),
            pltpu.VMEM((2, nb, S5_NS), F32),
            pltpu.VMEM((rows, 2 * S5_NS), F32),
            pltpu.VMEM((B_WIDTH // 128, scratch_rows, 128), F32),
            pltpu.VMEM((B_WIDTH // 128, scratch_rows, 128), F32),
        ],
        compiler_params=pltpu.CompilerParams(
            dimension_semantics=("arbitrary",),
            vmem_limit_bytes=V7X_VMEM_LIMIT_BYTES),
        name="s5_mix",
    )(u_in, h_re, h_im, *weights)
    return o.reshape(nb, t, B_WIDTH), hr, hi


CHUNK_W = CMP_STRIDE * 2 * A_KVW
CHUNKS_PER_PAGE = PAGE_SIZE // CMP_STRIDE
SEL_PAD = 256


def _bf16_round(x):
    return x.astype(BF16).astype(F32)


def _compress_tail(c, pos, w2):
    c1 = c[:, 128:]
    nxt = jnp.concatenate([c1[1:], c1[:1]], axis=0)
    hid = (pos + c[:, :128]) + nxt
    return jnp.dot(_gelu_tanh(hid).astype(BF16), w2, preferred_element_type=F32)


def _rms_heads128(x, gain):
    left = lax.broadcasted_iota(jnp.int32, x.shape, 1) < HEAD_DIM
    sq = x * x
    s0 = jnp.sum(jnp.where(left, sq, 0.0), axis=-1, keepdims=True)
    s1 = jnp.sum(jnp.where(left, 0.0, sq), axis=-1, keepdims=True)
    ms = jnp.where(left, s0, s1) * (1.0 / HEAD_DIM)
    return x * lax.rsqrt(ms + EPS) * gain


def _pad_rows8(x):
    return jnp.concatenate([x, jnp.zeros((8 - x.shape[0], x.shape[1]), x.dtype)], axis=0)


def _compress_rows(rows_scr, kv, n_chunks, w_ref, pos, w2_ref):
    acc = jnp.zeros((n_chunks, 2 * A_KVW), F32)
    for sp in range(CMP_STRIDE // 2):
        x = jnp.concatenate([rows_scr.at[kv][pl.ds(2 * sp + e, n_chunks, stride=CMP_STRIDE), :] for e in range(2)],
                            axis=1).astype(BF16)
        acc = acc + jnp.dot(x, w_ref[sp], preferred_element_type=F32)
    return _compress_tail(acc, pos, w2_ref[...])


def _compress_weights(cmp_pos, cmp_w1, cmp_w2, k_gain):
    eye = jnp.eye(A_KV, dtype=F32)
    w1 = cmp_w1.reshape(2, 2, CMP_STRIDE, HEAD_DIM, HEAD_DIM)
    w_big = jnp.einsum('ajsdh,gk->asgdjkh', w1, eye).reshape(2, CMP_STRIDE // 2, 2 * A_KVW, 2 * A_KVW).astype(BF16)
    w_pos = jnp.concatenate([cmp_w1, cmp_w1], axis=-1).astype(BF16)
    pos = jnp.broadcast_to(cmp_pos.reshape(2, 1, CMP_LEN * HEAD_DIM), (2, 8, CMP_LEN * HEAD_DIM)).astype(BF16)
    w2_big = jnp.einsum('ahd,gk->aghkd', cmp_w2, eye).reshape(2, A_KVW, A_KVW).astype(BF16)
    gain2 = jnp.tile(k_gain, A_KV).reshape(1, A_KVW)
    return (pos[0], pos[1], w_pos[0], w_pos[1], w_big[0], w_big[1], w2_big[0], w2_big[1], gain2)


def _compress_specs(full):
    mlp_w = (CMP_STRIDE // 2, 2 * A_KVW, 2 * A_KVW)
    return [full((8, CMP_LEN * HEAD_DIM)), full((8, CMP_LEN * HEAD_DIM)),
            full((CMP_LEN * HEAD_DIM, A_KVW)), full((CMP_LEN * HEAD_DIM, A_KVW)),
            full(mlp_w), full(mlp_w), full((A_KVW, A_KVW)), full((A_KVW, A_KVW)), full((1, A_KVW))]


def _pos_terms(pos_scr, posk_ref, posv_ref, wpos_k_ref, wpos_v_ref):
    pos_scr[0] = jnp.dot(posk_ref[...], wpos_k_ref[...], preferred_element_type=F32)
    pos_scr[1] = jnp.dot(posv_ref[...], wpos_v_ref[...], preferred_element_type=F32)


def _cmp_prompt_body(cmp_ref, posk_ref, posv_ref, wpos_k_ref, wpos_v_ref, wk_ref, wv_ref, w2k_ref, w2v_ref,
                     kgain_ref, kc_ref, vc_ref, pos_scr, rows_scr, *, n_chunks):
    @pl.when(pl.program_id(0) == 0)
    def _():
        _pos_terms(pos_scr, posk_ref, posv_ref, wpos_k_ref, wpos_v_ref)

    rows_scr[0] = cmp_ref[:, :A_KVW]
    rows_scr[1] = cmp_ref[:, A_KVW:]
    kc = _rms_heads128(_compress_rows(rows_scr, 0, n_chunks, wk_ref, pos_scr[0, 0:1], w2k_ref),
                       kgain_ref[...]).astype(BF16)
    vc = _compress_rows(rows_scr, 1, n_chunks, wv_ref, pos_scr[1, 0:1], w2v_ref).astype(BF16)
    for g in range(A_KV):
        kc_ref[g] = kc[:, g * HEAD_DIM:(g + 1) * HEAD_DIM]
        vc_ref[g] = vc[:, g * HEAD_DIM:(g + 1) * HEAD_DIM]


def _cmp_prompt(cmp_rows, cmp_weights):
    b, t, _ = cmp_rows.shape
    n_chunks = t // CMP_STRIDE
    full = lambda shape: pl.BlockSpec(shape, lambda i: (0,) * len(shape))
    out_spec = pl.BlockSpec((None, A_KV, n_chunks, HEAD_DIM), lambda i: (i, 0, 0, 0))
    out_shape = jax.ShapeDtypeStruct((b, A_KV, n_chunks, HEAD_DIM), BF16)
    return pl.pallas_call(
        functools.partial(_cmp_prompt_body, n_chunks=n_chunks),
        grid=(b,),
        in_specs=[pl.BlockSpec((None, t, 2 * A_KVW), lambda i: (i, 0, 0))] + _compress_specs(full),
        out_specs=[out_spec, out_spec],
        out_shape=[out_shape, out_shape],
        scratch_shapes=[pltpu.VMEM((2, 8, A_KVW), F32), pltpu.VMEM((2, t, A_KVW), F32)],
        compiler_params=pltpu.CompilerParams(
            dimension_semantics=("arbitrary",), vmem_limit_bytes=V7X_VMEM_LIMIT_BYTES),
        name="nsa_cmp_prompt",
    )(cmp_rows, *cmp_weights)


def _nsa_sample_cmp_body(pt_ref, *refs, n_pages, n_sel):
    page_refs = refs[:n_pages]
    (q_ref, posk_ref, posv_ref, wpos_k_ref, wpos_v_ref, wk_ref, wv_ref, w2k_ref, w2v_ref, kgain_ref, cover_ref,
     oc_ref, idx_ref, pos_scr, rows_scr) = refs[n_pages:]

    @pl.when(pl.program_id(0) == 0)
    def _():
        _pos_terms(pos_scr, posk_ref, posv_ref, wpos_k_ref, wpos_v_ref)

    n_chunks = n_pages * CHUNKS_PER_PAGE
    n_cmp = n_chunks - 1
    for kv in range(2):
        for p in range(n_pages):
            rows_scr[kv, p * PAGE_SIZE:(p + 1) * PAGE_SIZE, :] = page_refs[p][kv].reshape(A_KVW, PAGE_SIZE).T

    kc = _rms_heads128(_compress_rows(rows_scr, 0, n_chunks, wk_ref, pos_scr[0, 0:1], w2k_ref),
                       kgain_ref[...]).astype(BF16)
    vc = _compress_rows(rows_scr, 1, n_chunks, wv_ref, pos_scr[1, 0:1], w2v_ref).astype(BF16)

    col = lax.broadcasted_iota(jnp.int32, (8, n_chunks), 1)
    s_idx = lax.broadcasted_iota(jnp.int32, (1, SEL_PAD), 1)
    s_idx_f = s_idx.astype(F32)
    forced = (s_idx == 0) | (s_idx == n_sel - 1) | (s_idx == n_sel - 2)
    lane = lax.broadcasted_iota(jnp.int32, (1, 128), 1)
    for g in range(A_KV):
        hs = slice(g * HEAD_DIM, (g + 1) * HEAD_DIM)
        q8 = (_pad_rows8(q_ref[g]) * (HEAD_DIM ** -0.5)).astype(BF16)
        s = jnp.where(col < n_cmp, _dot_nt(q8, kc[:, hs]), NEG_INF)
        e = jnp.exp(s - jnp.max(s, axis=-1, keepdims=True))
        p = (e / jnp.sum(e, axis=-1, keepdims=True)).astype(BF16)
        oc_ref[g] = jnp.dot(p, vc[:, hs], preferred_element_type=F32)[0:A_REP]
        imp = jnp.sum(jnp.dot(p, cover_ref[...], preferred_element_type=F32)[0:A_REP], axis=0, keepdims=True)
        score = jnp.where(s_idx < n_sel, imp + jnp.where(forced, FORCE_BONUS, 0.0), NEG_INF)
        rank = jnp.zeros((1, SEL_PAD), F32)
        for j in range(n_sel):
            cj = score[:, j:j + 1]
            beats = (cj > score) | ((cj == score) & (s_idx > j))
            rank = rank + jnp.where(beats, 1.0, 0.0)
        picks = jnp.zeros((1, 128), F32)
        for r in range(SEL_TOPK):
            block = jnp.sum(jnp.where(rank == float(r), s_idx_f, 0.0), axis=-1, keepdims=True)
            picks = picks + jnp.where(lane == r, block, 0.0)
        idx_ref[g:g + 1, :] = picks.astype(jnp.int32)


def _nsa_sample_att_body(idx_ref, pt_ref, *refs, n_past_blk):
    n_slots = A_KV * SEL_TOPK
    slot_refs = refs[:n_slots]
    (q_all, gate_all, oc_all, newsel_all, wcache_all, newwin_all, selb_all, winb_all, newb_all, o_all) = refs[n_slots:]
    b = pl.program_id(0)
    for g in range(A_KV):
        _nsa_sample_att_group(b, g, idx_ref, slot_refs[g * SEL_TOPK:(g + 1) * SEL_TOPK], q_all.at[g], gate_all.at[g],
                              oc_all.at[g], newsel_all.at[g], wcache_all.at[:, g], newwin_all.at[g], selb_all.at[g],
                              winb_all.at[g], newb_all.at[g], o_all.at[g], n_past_blk)


def _nsa_sample_att_group(b, g, idx_ref, page_refs, q_ref, gate_ref, oc_ref, newsel_ref, wcache_ref, newwin_ref,
                          selb_ref, winb_ref, newb_ref, o_ref, n_past_blk):
    blocks_per_page = PAGE_SIZE // SEL_BLOCK
    q8 = (_pad_rows8(q_ref[...]) * (HEAD_DIM ** -0.5)).astype(BF16)
    new_bias = newb_ref[:, 0:1]

    def attend(s, v_t, new_kv):
        k_new = _bf16_round(new_kv[0:1])
        v_new = _bf16_round(new_kv[1:2])
        s_new = jnp.sum(q8.astype(F32) * k_new, axis=-1, keepdims=True) + new_bias
        m = jnp.maximum(jnp.max(s, axis=-1, keepdims=True), s_new)
        e = jnp.exp(s - m)
        e_new = jnp.exp(s_new - m)
        den = jnp.sum(e, axis=-1, keepdims=True) + e_new
        return _dot_nt((e / den).astype(BF16), v_t) + _bf16_round(e_new / den) * v_new

    lane = lax.broadcasted_iota(jnp.int32, (8, PAGE_SIZE), 1)
    near = selb_ref[...]
    scores = []
    for k in range(SEL_TOPK):
        ik = idx_ref[(b * A_KV + g) * SEL_TOPK + k]
        blk = jnp.minimum(ik, n_past_blk - 1)
        s_k = jnp.dot(q8, page_refs[k][0].astype(BF16), preferred_element_type=F32)
        s_k = s_k + jnp.where(blk // blocks_per_page == (n_past_blk - 1) // blocks_per_page, near, 0.0)
        keep = (lane // SEL_BLOCK == blk % blocks_per_page) & (ik < n_past_blk)
        scores.append(jnp.where(keep, s_k, NEG_INF))
    v_sel = jnp.concatenate([page_refs[k][1].astype(BF16) for k in range(SEL_TOPK)], axis=1)
    o_s = attend(jnp.concatenate(scores, axis=1), v_sel, newsel_ref[...])

    w_len = wcache_ref.shape[-1]
    wcol = lax.broadcasted_iota(jnp.int32, (8, w_len), 1)
    s_w = jnp.dot(q8, wcache_ref[0].astype(BF16), preferred_element_type=F32) + winb_ref[...]
    o_w = attend(jnp.where(wcol >= 1, s_w, NEG_INF), wcache_ref[1].astype(BF16), newwin_ref[...])
    gates = gate_ref[...]
    o_ref[...] = gates[:, 0:1] * oc_ref[...] + gates[:, 1:2] * o_s[0:A_REP] + gates[:, 2:3] * o_w[0:A_REP]


def _rows_last(cache):
    nd = cache.ndim
    return jnp.transpose(cache, tuple(range(nd - 4)) + (nd - 3, nd - 2, nd - 1, nd - 4))


def _nsa_sample(q, gates, new_sel, new_win, pool_cmp, pool_sel, win_cache, page_table, cmp_weights, tab):
    n, n_pages = page_table.shape
    past = n_pages * PAGE_SIZE
    n_past_blk = past // SEL_BLOCK
    n_sel = n_past_blk + 1
    n_chunks = n_pages * CHUNKS_PER_PAGE
    w_len = win_cache.shape[1]
    blocks_per_page = PAGE_SIZE // SEL_BLOCK
    assert n_sel <= SEL_PAD and w_len == WIN_A and past >= WIN_A and blocks_per_page == 2
    pt_flat = page_table.reshape(-1)

    c_idx = np.arange(n_chunks)
    s_start = np.arange(SEL_PAD) * SEL_BLOCK
    cover = ((c_idx[:, None] * CMP_STRIDE < s_start[None, :] + SEL_BLOCK)
             & (c_idx[:, None] * CMP_STRIDE + CMP_LEN - 1 >= s_start[None, :])
             & (c_idx[:, None] < n_chunks - 1) & (np.arange(SEL_PAD)[None, :] < n_sel))
    cover = jnp.asarray(cover.astype(np.float32), BF16)

    head_spec = lambda last: pl.BlockSpec((None, A_KV, A_REP, last), lambda i, *_: (i, 0, 0, 0))
    page_specs = [pl.BlockSpec((None, 2, A_KV, HEAD_DIM, PAGE_SIZE),
                               functools.partial(lambda i, pt, p: (pt[i * n_pages + p], 0, 0, 0, 0), p=p))
                  for p in range(n_pages)]
    full = lambda shape: pl.BlockSpec(shape, lambda i, *_: (0,) * len(shape))
    pool_c = _rows_last(pool_cmp)
    o_c, idx = pl.pallas_call(
        functools.partial(_nsa_sample_cmp_body, n_pages=n_pages, n_sel=n_sel),
        grid_spec=pltpu.PrefetchScalarGridSpec(
            num_scalar_prefetch=1,
            grid=(n,),
            in_specs=page_specs + [head_spec(HEAD_DIM)] + _compress_specs(full) + [full((n_chunks, SEL_PAD))],
            out_specs=[head_spec(HEAD_DIM), pl.BlockSpec((None, A_KV, 128), lambda i, *_: (i, 0, 0))],
            scratch_shapes=[pltpu.VMEM((2, 8, A_KVW), F32), pltpu.VMEM((2, past, A_KVW), F32)],
        ),
        out_shape=[jax.ShapeDtypeStruct((n, A_KV, A_REP, HEAD_DIM), F32),
                   jax.ShapeDtypeStruct((n, A_KV, 128), jnp.int32)],
        compiler_params=pltpu.CompilerParams(
            dimension_semantics=("arbitrary",), vmem_limit_bytes=V7X_VMEM_LIMIT_BYTES),
        name="nsa_sample_cmp",
    )(pt_flat, *([pool_c] * n_pages), q, *cmp_weights, cover)

    rel = lambda dist: jnp.pad(jnp.transpose(_bias_lookup(tab - tab[NUM_BUCKETS - 1], dist), (1, 2, 0)),
                               ((0, 0), (0, 8 - A_REP), (0, 0)))
    sel_bias = rel(past - (past - PAGE_SIZE + np.arange(PAGE_SIZE)))
    win_bias = rel(w_len - np.arange(w_len))
    new_bias = rel(np.zeros(128, np.int64))

    idx_flat = idx[:, :, :SEL_TOPK].reshape(-1)

    def page_map(i, idx_s, pt, g, slot):
        blk = jnp.minimum(idx_s[(i * A_KV + g) * SEL_TOPK + slot], n_past_blk - 1)
        return (pt[i * n_pages + blk // blocks_per_page], 0, g, 0, 0)

    slot_specs = [pl.BlockSpec((None, 2, None, HEAD_DIM, PAGE_SIZE), functools.partial(page_map, g=g, slot=s))
                  for g in range(A_KV) for s in range(SEL_TOPK)]
    per_head = lambda rows, last: pl.BlockSpec((None, A_KV, rows, last), lambda i, *_: (i, 0, 0, 0))
    per_group = lambda last: pl.BlockSpec((A_KV, 8, last), lambda i, *_: (0, 0, 0))
    return pl.pallas_call(
        functools.partial(_nsa_sample_att_body, n_past_blk=n_past_blk),
        grid_spec=pltpu.PrefetchScalarGridSpec(
            num_scalar_prefetch=2,
            grid=(n,),
            in_specs=slot_specs + [
                per_head(A_REP, HEAD_DIM), per_head(A_REP, 3), per_head(A_REP, HEAD_DIM), per_head(2, HEAD_DIM),
                pl.BlockSpec((None, 2, A_KV, HEAD_DIM, w_len), lambda i, *_: (i, 0, 0, 0, 0)),
                per_head(2, HEAD_DIM), per_group(PAGE_SIZE), per_group(w_len), per_group(128),
            ],
            out_specs=per_head(A_REP, HEAD_DIM),
        ),
        out_shape=jax.ShapeDtypeStruct((n, A_KV, A_REP, HEAD_DIM), F32),
        compiler_params=pltpu.CompilerParams(
            dimension_semantics=("arbitrary",), vmem_limit_bytes=V7X_VMEM_LIMIT_BYTES),
        name="nsa_sample_att",
    )(idx_flat, pt_flat, *([_rows_last(pool_sel)] * (A_KV * SEL_TOPK)), q, gates, o_c,
      jnp.transpose(new_sel, (0, 2, 1, 3)), _rows_last(win_cache), jnp.transpose(new_win, (0, 2, 1, 3)),
      sel_bias, win_bias, new_bias)


SWA_Q_TILE = WIN_C


def _swa_prompt_body(q_ref, k_ref, v_ref, btile_ref, sink_ref, o_ref, *, tq):
    qt = pl.program_id(2)
    q0 = pl.multiple_of(qt * tq, tq)
    q = q_ref[...]
    qs = jnp.concatenate([q[:, r * HEAD_DIM:(r + 1) * HEAD_DIM] for r in range(C_REP)], axis=0)
    qs = (qs * (HEAD_DIM ** -0.5)).astype(BF16)
    prev0 = pl.multiple_of(jnp.maximum(q0 - tq, 0), tq)
    k = jnp.concatenate([k_ref[pl.ds(prev0, tq), :], k_ref[pl.ds(q0, tq), :]], axis=0)
    v = jnp.concatenate([v_ref[pl.ds(prev0, tq), :], v_ref[pl.ds(q0, tq), :]], axis=0)
    s = _dot_nt(k, qs) + btile_ref[...]
    sinks = sink_ref[...]
    sink = jnp.concatenate([sinks[r:r + 1, :] for r in range(C_REP)], axis=1)
    m = jnp.maximum(jnp.max(s, axis=0, keepdims=True), sink)
    e = jnp.exp(s - m)
    p = e / (jnp.sum(e, axis=0, keepdims=True) + jnp.exp(sink - m))
    o = lax.dot_general(v, p.astype(BF16), (((0,), (0,)), ((), ())), preferred_element_type=F32)
    o_ref[...] = jnp.concatenate([o[:, r * tq:(r + 1) * tq].T for r in range(C_REP)], axis=1)


def _swa_prompt(q, kvt, sinks, tab):
    b, t, _ = q.shape
    tq = SWA_Q_TILE
    btile = jnp.swapaxes(_near_bias_tiles(tab, tq, WIN_C), -1, -2)
    kv_spec = lambda slot: pl.BlockSpec((None, None, t, HEAD_DIM), lambda i, g, j: (i, slot + g, 0, 0))
    qo_spec = pl.BlockSpec((None, tq, C_REP * HEAD_DIM), lambda i, g, j: (i, j, g))
    sink_lanes = jnp.broadcast_to(sinks.reshape(C_KV, C_REP, 1), (C_KV, C_REP, 128))
    return pl.pallas_call(
        functools.partial(_swa_prompt_body, tq=tq),
        grid=(b, C_KV, t // tq),
        in_specs=[qo_spec, kv_spec(0), kv_spec(C_KV),
                  pl.BlockSpec((None, None, 2 * tq, C_REP * tq), lambda i, g, j: (jnp.minimum(j, 1), g, 0, 0)),
                  pl.BlockSpec((None, C_REP, 128), lambda i, g, j: (g, 0, 0))],
        out_specs=qo_spec,
        out_shape=jax.ShapeDtypeStruct((b, t, C_HEADS * HEAD_DIM), F32),
        compiler_params=pltpu.CompilerParams(
            dimension_semantics=("parallel", "parallel", "arbitrary"), vmem_limit_bytes=V7X_VMEM_LIMIT_BYTES),
        name="swa_prompt",
    )(q, kvt, kvt, btile, sink_lanes)


def _swa_sample_body(q_ref, cache_ref, new_ref, bias_ref, newb_ref, sink_ref, o_ref):
    w_len = cache_ref.shape[-1]
    wcol = lax.broadcasted_iota(jnp.int32, (C_REP, w_len), 1)
    for g in range(C_KV):
        q8 = (q_ref[g] * (HEAD_DIM ** -0.5)).astype(BF16)
        s = jnp.dot(q8, cache_ref[0, g].astype(BF16), preferred_element_type=F32) + bias_ref[g]
        s = jnp.where(wcol >= 1, s, NEG_INF)
        k_new = _bf16_round(new_ref[g, 0:1])
        v_new = _bf16_round(new_ref[g, 1:2])
        s_new = jnp.sum(q8.astype(F32) * k_new, axis=-1, keepdims=True) + newb_ref[g][:, 0:1]
        sink = sink_ref[g][:, 0:1]
        m = jnp.maximum(jnp.maximum(jnp.max(s, axis=-1, keepdims=True), s_new), sink)
        e = jnp.exp(s - m)
        e_new = jnp.exp(s_new - m)
        den = jnp.sum(e, axis=-1, keepdims=True) + e_new + jnp.exp(sink - m)
        o_ref[g] = _dot_nt((e / den).astype(BF16), cache_ref[1, g].astype(BF16)) + _bf16_round(e_new / den) * v_new


def _swa_sample(q, cache, new_kv, sinks, tab):
    n, w_len = cache.shape[:2]
    bias = jnp.transpose(_bias_lookup(tab, w_len - np.arange(w_len)), (1, 2, 0))
    lanes = lambda x: jnp.broadcast_to(x[:, :, None], (C_KV, C_REP, 128))
    full = lambda shape: pl.BlockSpec(shape, lambda i: (0,) * len(shape))
    return pl.pallas_call(
        _swa_sample_body,
        grid=(n,),
        in_specs=[
            pl.BlockSpec((None, C_KV, C_REP, HEAD_DIM), lambda i: (i, 0, 0, 0)),
            pl.BlockSpec((None, 2, C_KV, HEAD_DIM, w_len), lambda i: (i, 0, 0, 0, 0)),
            pl.BlockSpec((None, C_KV, 2, HEAD_DIM), lambda i: (i, 0, 0, 0)),
            full((C_KV, C_REP, w_len)), full((C_KV, C_REP, 128)), full((C_KV, C_REP, 128)),
        ],
        out_specs=pl.BlockSpec((None, C_KV, C_REP, HEAD_DIM), lambda i: (i, 0, 0, 0)),
        out_shape=jax.ShapeDtypeStruct((n, C_KV, C_REP, HEAD_DIM), F32),
        compiler_params=pltpu.CompilerParams(
            dimension_semantics=("arbitrary",), vmem_limit_bytes=V7X_VMEM_LIMIT_BYTES),
        name="swa_sample",
    )(q, _rows_last(cache), new_kv, bias, lanes(tab[0]), lanes(sinks.reshape(C_KV, C_REP)))


PROJ_ROW_TILE = 512
A_IN_PAD = 1920
A_U_COL = A_Q + 6 * A_KVW
A_GATE_COL = A_U_COL + B_WIDTH


def _heads_first(kvt_ref, slot, k, v, n_kv):
    for g in range(n_kv):
        kvt_ref[slot + g] = k[:, g * HEAD_DIM:(g + 1) * HEAD_DIM].astype(BF16)
        kvt_ref[slot + n_kv + g] = v[:, g * HEAD_DIM:(g + 1) * HEAD_DIM].astype(BF16)


def _inproj_a_body(x_ref, gain_ref, w_ref, qg_ref, kg_ref,
                   q_ref, cmp_ref, sel_ref, win_ref, gate_ref, u_ref, kvt_ref):
    xn = _rms_rows(x_ref[...], gain_ref[...]).astype(BF16)
    z = jnp.dot(xn, w_ref[...], preferred_element_type=F32)
    for j in range(A_Q // 128):
        q_ref[:, j * 128:(j + 1) * 128] = _rms_heads128(z[:, j * 128:(j + 1) * 128], qg_ref[...])
    for out_ref, off, slot in ((cmp_ref, A_Q, None), (sel_ref, A_Q + 2 * A_KVW, 0), (win_ref, A_Q + 4 * A_KVW, 4)):
        k = _rms_heads128(z[:, off:off + A_KVW], kg_ref[...])
        v = z[:, off + A_KVW:off + 2 * A_KVW]
        out_ref[:, :A_KVW] = k
        out_ref[:, A_KVW:] = v
        if slot is not None:
            _heads_first(kvt_ref, slot, k, v, A_KV)
    u_ref[...] = z[:, A_U_COL:A_GATE_COL]
    gate_ref[...] = 1.0 / (1.0 + jnp.exp(-z[:, A_GATE_COL:A_IN_PAD]))


def _inproj_a(x, gain, w_in, q_gain, k_gain):
    b, t, d = x.shape
    tm = min(t, PROJ_ROW_TILE)
    w = jnp.concatenate([w_in[:, :A_U_COL], w_in[:, A_U_COL + A_GATE:], w_in[:, A_U_COL:A_U_COL + A_GATE],
                         jnp.zeros((d, A_IN_PAD - A_GATE_COL - A_GATE), F32)], axis=1).astype(BF16)
    tile2 = lambda g: jnp.tile(g, 2).reshape(1, 128)
    rows = lambda width: pl.BlockSpec((None, tm, width), lambda i, j: (i, j, 0))
    shape = lambda width: jax.ShapeDtypeStruct((b, t, width), F32)
    return pl.pallas_call(
        _inproj_a_body,
        grid=(b, t // tm),
        in_specs=[rows(d), _const_spec((1, d)), _const_spec((d, A_IN_PAD)), _const_spec((1, 128)), _const_spec((1, 128))],
        out_specs=[rows(A_Q), rows(2 * A_KVW), rows(2 * A_KVW), rows(2 * A_KVW), rows(128), rows(B_WIDTH),
                   pl.BlockSpec((None, 8, tm, HEAD_DIM), lambda i, j: (i, 0, j, 0))],
        out_shape=[shape(A_Q), shape(2 * A_KVW), shape(2 * A_KVW), shape(2 * A_KVW), shape(128), shape(B_WIDTH),
                   jax.ShapeDtypeStruct((b, 8, t, HEAD_DIM), BF16)],
        compiler_params=pltpu.CompilerParams(
            dimension_semantics=("parallel", "parallel"), vmem_limit_bytes=V7X_VMEM_LIMIT_BYTES),
        name="inproj_nsa_s5",
    )(x, gain.reshape(1, d), w, tile2(q_gain), tile2(k_gain))


def _inproj_c_body(x_ref, gain_ref, w_ref, qg_ref, kg_ref, q_ref, kv_ref, kvt_ref):
    xn = _rms_rows(x_ref[...], gain_ref[...]).astype(BF16)
    z = jnp.dot(xn, w_ref[...], preferred_element_type=F32)
    n_q = C_HEADS * HEAD_DIM
    for j in range(n_q // 128):
        q_ref[:, j * 128:(j + 1) * 128] = _rms_heads128(z[:, j * 128:(j + 1) * 128], qg_ref[...])
    k = _rms_heads128(z[:, n_q:n_q + C_KV * HEAD_DIM], kg_ref[...])
    v = z[:, n_q + C_KV * HEAD_DIM:]
    kv_ref[:, :C_KV * HEAD_DIM] = k
    kv_ref[:, C_KV * HEAD_DIM:] = v
    _heads_first(kvt_ref, 0, k, v, C_KV)


def _inproj_c(x, gain, w_in, q_gain, k_gain):
    b, t, d = x.shape
    tm = min(t, PROJ_ROW_TILE)
    n_in = w_in.shape[1]
    tile2 = lambda g: jnp.tile(g, 2).reshape(1, 128)
    rows = lambda width: pl.BlockSpec((None, tm, width), lambda i, j: (i, j, 0))
    shape = lambda width: jax.ShapeDtypeStruct((b, t, width), F32)
    return pl.pallas_call(
        _inproj_c_body,
        grid=(b, t // tm),
        in_specs=[rows(d), _const_spec((1, d)), _const_spec((d, n_in)), _const_spec((1, 128)), _const_spec((1, 128))],
        out_specs=[rows(C_HEADS * HEAD_DIM), rows(2 * C_KV * HEAD_DIM),
                   pl.BlockSpec((None, 2 * C_KV, tm, HEAD_DIM), lambda i, j: (i, 0, j, 0))],
        out_shape=[shape(C_HEADS * HEAD_DIM), shape(2 * C_KV * HEAD_DIM),
                   jax.ShapeDtypeStruct((b, 2 * C_KV, t, HEAD_DIM), BF16)],
        compiler_params=pltpu.CompilerParams(
            dimension_semantics=("parallel", "parallel"), vmem_limit_bytes=V7X_VMEM_LIMIT_BYTES),
        name="inproj_swa",
    )(x, gain.reshape(1, d), w_in.astype(BF16), tile2(q_gain), tile2(k_gain))


FFN_ROW_TILE = 512
FFN_COL_CHUNK = 1408


def _mixer_residual(y_ref, x_ref, mix_refs, wout_refs):
    y_ref[...] = x_ref[...]
    for m_ref, w_ref in zip(mix_refs, wout_refs):
        y_ref[...] += jnp.dot(m_ref[...].astype(BF16), w_ref[...], preferred_element_type=F32)
    return y_ref[...]


def _tail_prompt_body(*refs, n_mix, tm, ffc):
    x_ref = refs[0]
    mix_refs = refs[1:1 + n_mix]
    wout_refs = refs[1 + n_mix:1 + 2 * n_mix]
    (gain_ref, prev_ref, wup_ref, wgate_ref, cw_ref, cb_ref, wdown_ref, y_ref, cs_ref, hbuf_ref) = refs[1 + 2 * n_mix:]
    t = pl.program_id(1)
    xn = _rms_rows(_mixer_residual(y_ref, x_ref, mix_refs, wout_refs), gain_ref[...]).astype(BF16)
    for c in range(D_FF // ffc):
        lo = c * ffc
        h = jnp.dot(xn, wup_ref[:, lo:lo + ffc], preferred_element_type=F32)
        g = jnp.dot(xn, wgate_ref[:, lo:lo + ffc], preferred_element_type=F32)

        @pl.when(t == 0)
        def _():
            hbuf_ref[c, 6:8, :] = prev_ref[:, lo:lo + ffc]

        hbuf_ref[c, 8:8 + tm, :] = h
        hm1 = hbuf_ref[c, 7:7 + tm, :]
        hm2 = hbuf_ref[c, 6:6 + tm, :]
        cw = cw_ref[:, lo:lo + ffc]
        hc = cw[0:1] * hm2 + cw[1:2] * hm1 + cw[2:3] * h + cb_ref[:, lo:lo + ffc]
        a = (_gelu_tanh(hc) * g).astype(BF16)
        y_ref[...] += jnp.dot(a, wdown_ref[lo:lo + ffc, :], preferred_element_type=F32)
        hbuf_ref[c, 0:8, :] = h[tm - 8:tm, :]
        cs_ref[:, lo:lo + ffc] = h[tm - 2:tm, :]


def _tail_prompt(x, mixes, wouts, gain, prev, wup, wgate, cw, cb, wdown):
    b, t, d = x.shape
    tm, ffc = FFN_ROW_TILE, FFN_COL_CHUNK
    rows = lambda width: pl.BlockSpec((None, tm, width), lambda i, j: (i, j, 0))
    state = pl.BlockSpec((None, CONV_W - 1, D_FF), lambda i, j: (i, 0, 0))
    return pl.pallas_call(
        functools.partial(_tail_prompt_body, n_mix=len(mixes), tm=tm, ffc=ffc),
        grid=(b, t // tm),
        in_specs=[rows(d)] + [rows(m.shape[-1]) for m in mixes] + [_const_spec(w.shape) for w in wouts] + [
            _const_spec((1, d)), state, _const_spec((d, D_FF)), _const_spec((d, D_FF)),
            _const_spec((CONV_W, D_FF)), _const_spec((1, D_FF)), _const_spec((D_FF, d))],
        out_specs=[rows(d), state],
        out_shape=[jax.ShapeDtypeStruct((b, t, d), F32), jax.ShapeDtypeStruct((b, CONV_W - 1, D_FF), F32)],
        scratch_shapes=[pltpu.VMEM((D_FF // ffc, 8 + tm, ffc), F32)],
        compiler_params=pltpu.CompilerParams(
            dimension_semantics=("parallel", "arbitrary"), vmem_limit_bytes=V7X_VMEM_LIMIT_BYTES),
        name="tail_prompt",
    )(x, *mixes, *wouts, gain, prev, wup, wgate, cw, cb, wdown)


def _tail_sample_body(*refs, n_mix, ffc):
    x_ref = refs[0]
    mix_refs = refs[1:1 + n_mix]
    wout_refs = refs[1 + n_mix:1 + 2 * n_mix]
    (gain_ref, prev_ref, wup_ref, wgate_ref, cw_ref, cb_ref, wdown_ref, y_ref, cs_ref) = refs[1 + 2 * n_mix:]
    xn = _rms_rows(_mixer_residual(y_ref, x_ref, mix_refs, wout_refs), gain_ref[...]).astype(BF16)
    for c in range(D_FF // ffc):
        lo = c * ffc
        h = jnp.dot(xn, wup_ref[:, lo:lo + ffc], preferred_element_type=F32)
        g = jnp.dot(xn, wgate_ref[:, lo:lo + ffc], preferred_element_type=F32)
        hm2 = prev_ref[:, lo:lo + ffc]
        hm1 = prev_ref[:, D_FF + lo:D_FF + lo + ffc]
        cw = cw_ref[:, lo:lo + ffc]
        hc = cw[0:1] * hm2 + cw[1:2] * hm1 + cw[2:3] * h + cb_ref[:, lo:lo + ffc]
        a = (_gelu_tanh(hc) * g).astype(BF16)
        y_ref[...] += jnp.dot(a, wdown_ref[lo:lo + ffc, :], preferred_element_type=F32)
        cs_ref[:, lo:lo + ffc] = hm1
        cs_ref[:, D_FF + lo:D_FF + lo + ffc] = h


def _tail_sample(x, mixes, wouts, gain, prev, wup, wgate, cw, cb, wdown):
    n, d = x.shape
    full = lambda shape: _const_spec(shape)
    return pl.pallas_call(
        functools.partial(_tail_sample_body, n_mix=len(mixes), ffc=FFN_COL_CHUNK),
        grid=(1,),
        in_specs=[full((n, d))] + [full(m.shape) for m in mixes] + [full(w.shape) for w in wouts] + [
            full((1, d)), full((n, (CONV_W - 1) * D_FF)), full((d, D_FF)), full((d, D_FF)),
            full((CONV_W, D_FF)), full((1, D_FF)), full((D_FF, d))],
        out_specs=[pl.BlockSpec((n, d), lambda i: (0, 0)), pl.BlockSpec((n, (CONV_W - 1) * D_FF), lambda i: (0, 0))],
        out_shape=[jax.ShapeDtypeStruct((n, d), F32), jax.ShapeDtypeStruct((n, (CONV_W - 1) * D_FF), F32)],
        compiler_params=pltpu.CompilerParams(
            dimension_semantics=("arbitrary",), vmem_limit_bytes=V7X_VMEM_LIMIT_BYTES),
        name="tail_sample",
    )(x, *mixes, *wouts, gain, prev, wup, wgate, cw, cb, wdown)


def kernel(x_prompt, x_sample, cache_nsa_cmp, cache_nsa_sel, cache_nsa_win, state_s5_re, state_s5_im,
           cache_swa, state_ffn_conv, page_table, rel_bias, norm_mix, norm_ffn, a_w_in, a_w_out,
           nsa_q_gain, nsa_k_gain, nsa_cmp_pos, nsa_cmp_w1, nsa_cmp_w2, s5_a_re, s5_a_im, s5_log_dt,
           s5_b_re, s5_b_im, s5_c_re, s5_c_im, s5_d, s5_w_glu, s5_b_glu, c_w_in, c_w_out, c_q_gain,
           c_k_gain, c_sinks, ffn_w_up, ffn_w_gate, ffn_conv_w, ffn_conv_b, ffn_w_down):
    bp, tp, _ = x_prompt.shape
    bs, ts, _ = x_sample.shape
    assert ts == 1 and DEPTH == 2
    tab_a = rel_bias[:, :A_HEADS].reshape(NUM_BUCKETS, A_KV, A_REP)
    tab_c = rel_bias[:, :C_HEADS].reshape(NUM_BUCKETS, C_KV, C_REP)
    kv6 = lambda x: x.reshape(x.shape[:-1] + (2, x.shape[-1] // (2 * HEAD_DIM), HEAD_DIM))
    hp, hs = x_prompt, x_sample.reshape(1, bs, D_MODEL)
    conv_p, conv_s = [], []

    def tail(layer, hp, hs, mixes_p, mixes_s, wouts):
        wouts = [w.astype(BF16) for w in wouts]
        ffn = (norm_ffn[layer].reshape(1, D_MODEL),)
        wts = (ffn_w_up[layer].astype(BF16), ffn_w_gate[layer].astype(BF16), ffn_conv_w[layer],
               ffn_conv_b[layer].reshape(1, D_FF), ffn_w_down[layer].astype(BF16))
        hp, cp = _tail_prompt(hp, mixes_p, wouts, *ffn, jnp.zeros((bp, CONV_W - 1, D_FF), F32), *wts)
        hs2, cs = _tail_sample(hs[0], [m.reshape(bs, -1) for m in mixes_s], wouts, *ffn,
                               state_ffn_conv[layer].reshape(bs, (CONV_W - 1) * D_FF), *wts)
        conv_p.append(cp)
        conv_s.append(cs.reshape(bs, CONV_W - 1, D_FF))
        return hp, hs2.reshape(1, bs, D_MODEL)

    proj = (norm_mix[0], a_w_in[0], nsa_q_gain[0], nsa_k_gain[0])
    qp, cmp_p, sel_p, win_p, gate_p, up, kvt_p = _inproj_a(hp, *proj)
    qs, cmp_s, sel_s, win_s, gate_s, us, _ = _inproj_a(hs, *proj)
    cmp_w = _compress_weights(nsa_cmp_pos[0], nsa_cmp_w1[0], nsa_cmp_w2[0], nsa_k_gain[0])
    kc, vc = _cmp_prompt(cmp_p, cmp_w)
    o_ap = _nsa_prompt_attend(qp, gate_p, kc, vc, kvt_p, tab_a)
    o_as = _nsa_sample(qs.reshape(bs, A_KV, A_REP, HEAD_DIM), gate_s[0, :, :A_GATE].reshape(bs, A_KV, A_REP, 3),
                       sel_s.reshape(bs, 2, A_KV, HEAD_DIM), win_s.reshape(bs, 2, A_KV, HEAD_DIM),
                       cache_nsa_cmp[0], cache_nsa_sel[0], cache_nsa_win[0], page_table, cmp_w, tab_a)
    s5w = _s5_weights(s5_a_re[0], s5_a_im[0], s5_log_dt[0], s5_b_re[0], s5_b_im[0], s5_c_re[0], s5_c_im[0],
                      s5_d[0], s5_w_glu[0], s5_b_glu[0])
    h0 = jnp.zeros((bp, S5_NS), F32)
    o_bp, hr_p, hi_p = _s5_mix(up, h0, h0, s5w)
    o_bs, hr_s, hi_s = _s5_mix(us.reshape(bs, 1, B_WIDTH), state_s5_re[0].reshape(bs, S5_NS),
                               state_s5_im[0].reshape(bs, S5_NS), s5w)
    hp, hs = tail(0, hp, hs, [o_ap, o_bp], [o_as, o_bs], [a_w_out[0][:A_Q], a_w_out[0][A_Q:]])

    proj = (norm_mix[1], c_w_in[0], c_q_gain[0], c_k_gain[0])
    qcp, kv_p, kvt_c = _inproj_c(hp, *proj)
    qcs, kv_s, _ = _inproj_c(hs, *proj)
    o_cp = _swa_prompt(qcp, kvt_c, c_sinks[0], tab_c)
    new_kv = kv_s.reshape(bs, 2, C_KV, HEAD_DIM)
    o_cs = _swa_sample(qcs.reshape(bs, C_KV, C_REP, HEAD_DIM), cache_swa[0], jnp.transpose(new_kv, (0, 2, 1, 3)),
                       c_sinks[0], tab_c)
    hp, hs = tail(1, hp, hs, [o_cp], [o_cs], [c_w_out[0]])

    state = lambda x, n: x.reshape(1, n, B_GROUPS, B_STATE)
    nsa_win_s = jnp.concatenate([cache_nsa_win[0][:, 1:], kv6(win_s).reshape(bs, 1, 2, A_KV, HEAD_DIM)], axis=1)
    swa_s = jnp.concatenate([cache_swa[0][:, 1:], new_kv.reshape(bs, 1, 2, C_KV, HEAD_DIM)], axis=1)
    return (hp, hs.reshape(bs, ts, D_MODEL),
            kv6(cmp_p)[None], kv6(cmp_s).reshape(1, bs, ts, 2, A_KV, HEAD_DIM),
            kv6(sel_p)[None], kv6(sel_s).reshape(1, bs, ts, 2, A_KV, HEAD_DIM),
            kv6(win_p)[None, :, -min(WIN_A, tp):], nsa_win_s[None],
            state(hr_p, bp), state(hi_p, bp), state(hr_s, bs), state(hi_s, bs),
            kv6(kv_p)[None, :, -min(WIN_C, tp):], swa_s[None],
            jnp.stack(conv_p), jnp.stack(conv_s))
```

```python
import functools
import math

import jax
import jax.numpy as jnp
import numpy as np
from jax import lax
from jax.experimental import pallas as pl
from jax.experimental.pallas import tpu as pltpu

D_MODEL = 1024
DEPTH = 2
PAGE_SIZE = 128
HEAD_DIM = 64
A_HEADS = 8
A_KV = 2
A_REP = A_HEADS // A_KV
A_Q = A_HEADS * HEAD_DIM
A_KVW = A_KV * HEAD_DIM
A_GATE = 3 * A_HEADS
CMP_LEN = 32
CMP_STRIDE = 16
SEL_BLOCK = 64
SEL_TOPK = 16
WIN_A = 512
NSA_QBLK = 64
FORCE_BONUS = 1000.0
B_WIDTH = D_MODEL // 2
B_GROUP = 16
B_GROUPS = B_WIDTH // B_GROUP
B_STATE = 64
C_HEADS = D_MODEL // HEAD_DIM
C_KV = 2
C_REP = C_HEADS // C_KV
WIN_C = 128
NUM_BUCKETS = 32
MAX_DISTANCE = 128
D_FF = 2816
CONV_W = 3
EPS = 1e-6

F32 = jnp.float32
BF16 = jnp.bfloat16

V7X_VMEM_LIMIT_BYTES = 56 * 1024 * 1024


def _gelu_tanh(x):
    return 0.5 * x * (1.0 + jnp.tanh(math.sqrt(2.0 / math.pi) * (x + 0.044715 * (x * x * x))))


def _rms_rows(x, gain):
    return x * lax.rsqrt(jnp.mean(x * x, axis=-1, keepdims=True) + EPS) * gain


def _const_spec(shape):
    zeros = (0,) * len(shape)
    return pl.BlockSpec(shape, lambda *_: zeros, pipeline_mode=pl.Buffered(1))


NSA_Q_TILE = 128
NSA_FAR_TILE = 512
NSA_NEAR = 2 * NSA_Q_TILE
NEG_INF = float("-inf")


def _dot_nt(a, b):
    return lax.dot_general(a, b, (((1,), (1,)), ((), ())), preferred_element_type=F32)


def _dot_tn(a, b):
    return lax.dot_general(a, b, (((0,), (0,)), ((), ())), preferred_element_type=F32)


def _softmax_start(s, v):
    m = jnp.max(s, axis=0, keepdims=True)
    e = jnp.exp(s - m)
    return m, jnp.sum(e, axis=0, keepdims=True), _dot_tn(v, e.astype(BF16))


def _softmax_more(carry, s, v):
    m, l, acc = carry
    m_new = jnp.maximum(m, jnp.max(s, axis=0, keepdims=True))
    alpha = jnp.exp(m - m_new)
    e = jnp.exp(s - m_new)
    return m_new, alpha * l + jnp.sum(e, axis=0, keepdims=True), alpha * acc + _dot_tn(v, e.astype(BF16))


def _nsa_prompt_body(q_ref, gate_ref, kc_ref, vc_ref, kvt_ref, btile_ref, cover_ref, o_ref, sel_scr, *, tq):
    width = A_REP * HEAD_DIM
    gates_t = gate_ref[...].T
    for g in range(A_KV):
        _nsa_prompt_group(g, q_ref.at[:, g * width:(g + 1) * width], gates_t, kc_ref.at[g], vc_ref.at[g],
                          kvt_ref.at[g], kvt_ref.at[A_KV + g], kvt_ref.at[2 * A_KV + g], kvt_ref.at[3 * A_KV + g],
                          btile_ref.at[g], cover_ref, o_ref.at[:, g * width:(g + 1) * width], sel_scr.at[g], tq)


def _nsa_prompt_group(g, q_ref, gates_t, kc_ref, vc_ref, ks_ref, vs_ref, kw_ref, vw_ref, btile_ref, cover_ref,
                      o_ref, sel_ref, tq):
    qt = pl.program_id(1)
    q0 = pl.multiple_of(qt * tq, tq)
    cols = A_REP * tq
    q = q_ref[...]
    qs = jnp.concatenate([q[:, r * HEAD_DIM:(r + 1) * HEAD_DIM] for r in range(A_REP)], axis=0)
    qs = (qs * (HEAD_DIM ** -0.5)).astype(BF16)

    def q_pos(height):
        return q0 + (lax.broadcasted_iota(jnp.int32, (height, cols), 1) & (tq - 1))

    def key_idx(height):
        return lax.broadcasted_iota(jnp.int32, (height, cols), 0)

    n_idx = key_idx(128)
    valid_c = (n_idx * CMP_STRIDE + (CMP_LEN - 1) <= q_pos(128)) & (n_idx < 127)
    s_c = jnp.where(valid_c, _dot_nt(kc_ref[...], qs), NEG_INF)
    m_c = jnp.max(s_c, axis=0, keepdims=True)
    m_c = jnp.where(m_c == NEG_INF, 0.0, m_c)
    e_c = jnp.exp(s_c - m_c)
    d_c = jnp.sum(e_c, axis=0, keepdims=True)
    p_c = (e_c / jnp.where(d_c > 0, d_c, 1.0)).astype(BF16)
    o_c = _dot_tn(vc_ref[...], p_c)
    imp_heads = jnp.dot(cover_ref[...], p_c, preferred_element_type=F32)
    imp = sum(imp_heads[0:32, r * tq:(r + 1) * tq] for r in range(A_REP))
    s_idx = lax.broadcasted_iota(jnp.int32, (32, tq), 0)
    qblk = (q0 + lax.broadcasted_iota(jnp.int32, (32, tq), 1)) >> 6
    forced = (s_idx == 0) | (s_idx == qblk) | (s_idx == qblk - 1)
    allowed = s_idx <= qblk
    score = jnp.where(allowed, imp + jnp.where(forced, FORCE_BONUS, 0.0), NEG_INF)
    rank = jnp.zeros((32, tq), F32)
    for j in range(32):
        other = score[j:j + 1, :]
        beats = (other > score) | ((other == score) & (s_idx > j))
        rank = rank + jnp.where(beats, 1.0, 0.0)
    sel = jnp.where((rank < SEL_TOPK) & allowed, 1.0, 0.0)
    sel_ref[...] = jnp.concatenate([sel] * A_REP, axis=1)

    def block_mask(k0, n_blocks):
        first = k0 // SEL_BLOCK
        return jnp.concatenate([jnp.broadcast_to(sel_ref[pl.ds(first + j, 1), :], (SEL_BLOCK, cols))
                                for j in range(n_blocks)], axis=0) > 0.5

    prev0 = pl.multiple_of(jnp.maximum(q0 - tq, 0), tq)
    btile = btile_ref[...]

    def near(k_ref, v_ref, extra_mask):
        k = jnp.concatenate([k_ref[pl.ds(prev0, tq), :], k_ref[pl.ds(q0, tq), :]], axis=0)
        v = jnp.concatenate([v_ref[pl.ds(prev0, tq), :], v_ref[pl.ds(q0, tq), :]], axis=0)
        s = _dot_nt(k, qs) + btile
        return _softmax_start(s if extra_mask is None else jnp.where(extra_mask, s, NEG_INF), v)

    sel_near = jnp.concatenate([block_mask(prev0, tq // SEL_BLOCK), block_mask(q0, tq // SEL_BLOCK)], axis=0)

    far_end = q0 - tq

    def sel_far(i, carry):
        k0 = pl.multiple_of(i * NSA_FAR_TILE, NSA_FAR_TILE)
        s = _dot_nt(ks_ref[pl.ds(k0, NSA_FAR_TILE), :], qs)
        mask = block_mask(k0, NSA_FAR_TILE // SEL_BLOCK) & (k0 + key_idx(NSA_FAR_TILE) < far_end)
        return _softmax_more(carry, jnp.where(mask, s, NEG_INF), vs_ref[pl.ds(k0, NSA_FAR_TILE), :])

    n_far = (jnp.maximum(far_end, 0) + NSA_FAR_TILE - 1) // NSA_FAR_TILE
    _, l_s, acc_s = lax.fori_loop(0, n_far, sel_far, near(ks_ref, vs_ref, sel_near))
    o_s = acc_s / l_s

    w_far = WIN_A - tq
    wf0 = pl.multiple_of(jnp.maximum(q0 - WIN_A, 0), tq)
    wpos = wf0 + key_idx(w_far)
    wmask = (q_pos(w_far) - wpos < WIN_A) & (wpos < far_end)
    s_w = jnp.where(wmask, _dot_nt(kw_ref[pl.ds(wf0, w_far), :], qs), NEG_INF)
    _, l_w, acc_w = _softmax_more(near(kw_ref, vw_ref, None), s_w, vw_ref[pl.ds(wf0, w_far), :])
    o_w = acc_w / l_w

    outs = []
    for r in range(A_REP):
        sl = slice(r * tq, (r + 1) * tq)
        row = 3 * (g * A_REP + r)
        out_t = (gates_t[row:row + 1] * o_c[:, sl] + gates_t[row + 1:row + 2] * o_s[:, sl]
                 + gates_t[row + 2:row + 3] * o_w[:, sl])
        outs.append(out_t.T)
    o_ref[...] = jnp.concatenate(outs, axis=1)


def _bucket_np(dist):
    n = np.maximum(dist, 0)
    exact = NUM_BUCKETS // 2
    nf = np.maximum(n, exact).astype(np.float64)
    large = exact + (np.log(nf / exact) / math.log(MAX_DISTANCE / exact) * (NUM_BUCKETS - exact)).astype(np.int64)
    return np.where(n < exact, n, np.minimum(large, NUM_BUCKETS - 1)).astype(np.int32)


def _bias_lookup(tab, dist):
    bucket = _bucket_np(np.asarray(dist))
    onehot = (jnp.asarray(bucket.reshape(-1, 1)) == jnp.arange(NUM_BUCKETS)[None, :]).astype(F32)
    flat = jnp.dot(onehot, tab.reshape(NUM_BUCKETS, -1), precision=lax.Precision.HIGHEST)
    return flat.reshape(bucket.shape + tab.shape[1:])


def _near_bias_tiles(tab, tq, window):
    i = np.arange(tq)[:, None]
    j = np.arange(2 * tq)[None, :]
    dist = tq + i - j
    ok = (dist >= 0) & (dist < window)
    bias = jnp.transpose(_bias_lookup(tab, dist), (2, 3, 0, 1))
    tiles = jnp.stack([jnp.where(jnp.asarray(ok & (j >= tq)), bias, NEG_INF), jnp.where(jnp.asarray(ok), bias, NEG_INF)])
    return tiles.reshape(2, tab.shape[1], tab.shape[2] * tq, 2 * tq)


def _cover_matrix(n_cmp_pad, n_sel):
    n = np.arange(n_cmp_pad)
    c_start = n * CMP_STRIDE
    c_end = c_start + CMP_LEN - 1
    s_start = np.arange(128) * SEL_BLOCK
    cover = (c_start[:, None] < s_start[None, :] + SEL_BLOCK) & (c_end[:, None] >= s_start[None, :])
    cover &= (np.arange(128)[None, :] < n_sel)
    return np.tile(cover.astype(np.float32), (A_REP, 1))


def _nsa_prompt_attend(q, gates, kc, vc, kvt, tab):
    b, t, _ = q.shape
    tq = NSA_Q_TILE
    assert t % NSA_FAR_TILE == 0 and t // SEL_BLOCK <= 32 and t >= WIN_A and kc.shape[2] == 128
    btile = jnp.swapaxes(_near_bias_tiles(tab - tab[NUM_BUCKETS - 1], tq, 2 * tq), -1, -2)
    cover = jnp.asarray(_cover_matrix(128, t // SEL_BLOCK)[:128].T, BF16)
    cmp_spec = pl.BlockSpec((None, A_KV, 128, HEAD_DIM), lambda i, j: (i, 0, 0, 0))
    return pl.pallas_call(
        functools.partial(_nsa_prompt_body, tq=tq),
        grid=(b, t // tq),
        in_specs=[
            pl.BlockSpec((None, tq, A_Q), lambda i, j: (i, j, 0)),
            pl.BlockSpec((None, tq, 128), lambda i, j: (i, j, 0)),
            cmp_spec, cmp_spec,
            pl.BlockSpec((None, 4 * A_KV, t, HEAD_DIM), lambda i, j: (i, 0, 0, 0)),
            pl.BlockSpec((None, A_KV, 2 * tq, A_REP * tq), lambda i, j: (jnp.minimum(j, 1), 0, 0, 0)),
            pl.BlockSpec((128, 128), lambda i, j: (0, 0)),
        ],
        out_specs=pl.BlockSpec((None, tq, A_Q), lambda i, j: (i, j, 0)),
        out_shape=jax.ShapeDtypeStruct((b, t, A_Q), F32),
        scratch_shapes=[pltpu.VMEM((A_KV, 32, A_REP * tq), F32)],
        compiler_params=pltpu.CompilerParams(
            dimension_semantics=("parallel", "arbitrary"), vmem_limit_bytes=V7X_VMEM_LIMIT_BYTES),
        name="nsa_prompt",
    )(q, gates, kc, vc, kvt, btile, cover)


S5_NS = B_GROUPS * B_STATE
S5_T_CHUNK = 64
S5_STRIP = 512
S5_HALVES = 2


def _s5_body(u_ref, h0r_ref, h0i_ref, ar_ref, ai_ref, ldt_ref, wb_ref, wc_ref, d_ref, wglu_ref, bglu_ref,
             o_ref, hr_ref, hi_ref, coef_ref, st_ref, xbuf_ref, ubuf_ref, obuf_ref, *, nb, steps, interleave):
    c = pl.program_id(0)

    @pl.when(c == 0)
    def _():
        dt = jnp.exp(ldt_ref[...])
        ar, ai = ar_ref[...], ai_ref[...]
        mag = jnp.exp(ar * dt)
        abr, abi = mag * jnp.cos(ai * dt), mag * jnp.sin(ai * dt)
        den = ar * ar + ai * ai
        wr = ((abr - 1.0) * ar + abi * ai) / den
        wi = (abi * ar - (abr - 1.0) * ai) / den
        for k, val in enumerate((abr, abi, wr, wi)):
            coef_ref[k] = jnp.broadcast_to(val, (nb, S5_NS))
        st_ref[0] = h0r_ref[...]
        st_ref[1] = h0i_ref[...]

    if interleave:
        for b in range(nb):
            for j in range(B_WIDTH // 128):
                ubuf_ref.at[j][pl.ds(b, steps, stride=nb), :] = u_ref[b, :, j * 128:(j + 1) * 128]
        u = jnp.concatenate([ubuf_ref[j] for j in range(B_WIDTH // 128)], axis=1)
    else:
        u = u_ref[...]
    hc, hs = B_WIDTH // S5_HALVES, S5_NS // S5_HALVES
    ub = u.astype(BF16)
    for h in range(S5_HALVES):
        bu = jnp.dot(ub[:, h * hc:(h + 1) * hc], wb_ref[h], preferred_element_type=F32)
        xbuf_ref[:, h * hs:(h + 1) * hs] = bu[:, :hs]
        xbuf_ref[:, S5_NS + h * hs:S5_NS + (h + 1) * hs] = bu[:, hs:]

    for lo in range(0, S5_NS, S5_STRIP):
        re = slice(lo, lo + S5_STRIP)
        im = slice(S5_NS + lo, S5_NS + lo + S5_STRIP)
        abr, abi, wr, wi = (coef_ref[k, :, re] for k in range(4))

        def step(t, carry):
            sr, si = carry
            r0 = pl.multiple_of(t * nb, nb)
            bur = xbuf_ref[pl.ds(r0, nb), re]
            bui = xbuf_ref[pl.ds(r0, nb), im]
            nsr = abr * sr - abi * si + (wr * bur - wi * bui)
            nsi = abr * si + abi * sr + (wr * bui + wi * bur)
            xbuf_ref[pl.ds(r0, nb), re] = nsr
            xbuf_ref[pl.ds(r0, nb), im] = nsi
            return nsr, nsi

        sr, si = lax.fori_loop(0, steps, step, (st_ref[0, :, re], st_ref[1, :, re]),
                               unroll=min(steps, 8))
        st_ref[0, :, re] = sr
        st_ref[1, :, re] = si

    y = jnp.concatenate(
        [jnp.dot(jnp.concatenate([xbuf_ref[:, h * hs:(h + 1) * hs], xbuf_ref[:, S5_NS + h * hs:S5_NS + (h + 1) * hs]],
                                 axis=1).astype(BF16), wc_ref[h], preferred_element_type=F32)
         for h in range(S5_HALVES)], axis=1) + d_ref[...] * u
    z = _gelu_tanh(y)
    gate = jnp.dot(z.astype(BF16), wglu_ref[...], preferred_element_type=F32) + bglu_ref[...]
    out = z * (1.0 / (1.0 + jnp.exp(-gate)))
    if interleave:
        for j in range(B_WIDTH // 128):
            obuf_ref[j] = out[:, j * 128:(j + 1) * 128]
        for b in range(nb):
            for j in range(B_WIDTH // 128):
                o_ref[b, :, j * 128:(j + 1) * 128] = obuf_ref.at[j][pl.ds(b, steps, stride=nb), :]
    else:
        o_ref[...] = out
    hr_ref[...] = st_ref[0]
    hi_ref[...] = st_ref[1]


def _s5_weights(a_re, a_im, log_dt, b_re, b_im, c_re, c_im, d, w_glu, b_glu):
    eye = jnp.eye(B_GROUPS, dtype=F32)
    blk_in = lambda w: jnp.einsum('hg,gpc->hcgp', eye, w).reshape(B_WIDTH, S5_NS)
    blk_out = lambda w: jnp.einsum('gh,gcp->gphc', eye, w).reshape(S5_NS, B_WIDTH)
    hc, hs = B_WIDTH // S5_HALVES, S5_NS // S5_HALVES
    diag = lambda w, h: w[h * hc:(h + 1) * hc, h * hs:(h + 1) * hs]
    wb = jnp.stack([jnp.concatenate([diag(blk_in(b_re), h), diag(blk_in(b_im), h)], axis=1)
                    for h in range(S5_HALVES)]).astype(BF16)
    diag_t = lambda w, h: w[h * hs:(h + 1) * hs, h * hc:(h + 1) * hc]
    wc = jnp.stack([jnp.concatenate([diag_t(blk_out(c_re), h), -diag_t(blk_out(c_im), h)], axis=0)
                    for h in range(S5_HALVES)]).astype(BF16)
    flat = lambda x: x.reshape(1, S5_NS)
    return (flat(a_re), flat(a_im), flat(jnp.repeat(log_dt, B_STATE)), wb, wc, d.reshape(1, B_WIDTH),
            w_glu.astype(BF16), b_glu.reshape(1, B_WIDTH))


def _s5_mix(u, h_re, h_im, weights):
    nb, t, _ = u.shape
    interleave = t > 1
    steps = min(t, S5_T_CHUNK)
    rows = nb * steps
    body = functools.partial(_s5_body, nb=nb, steps=steps, interleave=interleave)
    if interleave:
        u_in = u
        u_spec = pl.BlockSpec((nb, steps, B_WIDTH), lambda c: (0, c, 0))
        o_shape = jax.ShapeDtypeStruct((nb, t, B_WIDTH), F32)
        scratch_rows = rows
    else:
        u_in = u.reshape(nb, B_WIDTH)
        u_spec = pl.BlockSpec((nb, B_WIDTH), lambda c: (0, 0))
        o_shape = jax.ShapeDtypeStruct((nb, B_WIDTH), F32)
        scratch_rows = 8
    o, hr, hi = pl.pallas_call(
        body,
        grid=(t // steps,),
        in_specs=[
            u_spec, _const_spec((nb, S5_NS)), _const_spec((nb, S5_NS)),
            _const_spec((1, S5_NS)), _const_spec((1, S5_NS)), _const_spec((1, S5_NS)),
            _const_spec((S5_HALVES, B_WIDTH // S5_HALVES, 2 * S5_NS // S5_HALVES)),
            _const_spec((S5_HALVES, 2 * S5_NS // S5_HALVES, B_WIDTH // S5_HALVES)),
            _const_spec((1, B_WIDTH)), _const_spec((B_WIDTH, B_WIDTH)), _const_spec((1, B_WIDTH)),
        ],
        out_specs=[u_spec, pl.BlockSpec((nb, S5_NS), lambda c: (0, 0)), pl.BlockSpec((nb, S5_NS), lambda c: (0, 0))],
        out_shape=[o_shape, jax.ShapeDtypeStruct((nb, S5_NS), F32), jax.ShapeDtypeStruct((nb, S5_NS), F32)],
        scratch_shapes=[
            pltpu.VMEM((4, nb, S5_NS), F32),
            pltpu.VMEM((2, nb, S5_NS), F32),
            pltpu.VMEM((rows, 2 * S5_NS), F32),
            pltpu.VMEM((B_WIDTH // 128, scratch_rows, 128), F32),
            pltpu.VMEM((B_WIDTH // 128, scratch_rows, 128), F32),
        ],
        compiler_params=pltpu.CompilerParams(
            dimension_semantics=("arbitrary",),
            vmem_limit_bytes=V7X_VMEM_LIMIT_BYTES),
        name="s5_mix",
    )(u_in, h_re, h_im, *weights)
    return o.reshape(nb, t, B_WIDTH), hr, hi


CHUNK_W = CMP_STRIDE * 2 * A_KVW
CHUNKS_PER_PAGE = PAGE_SIZE // CMP_STRIDE
SEL_PAD = 256


def _bf16_round(x):
    return x.astype(BF16).astype(F32)


def _compress_tail(c, pos, w2):
    c1 = c[:, 128:]
    nxt = jnp.concatenate([c1[1:], c1[:1]], axis=0)
    hid = (pos + c[:, :128]) + nxt
    return jnp.dot(_gelu_tanh(hid).astype(BF16), w2, preferred_element_type=F32)


def _rms_heads128(x, gain):
    left = lax.broadcasted_iota(jnp.int32, x.shape, 1) < HEAD_DIM
    sq = x * x
    s0 = jnp.sum(jnp.where(left, sq, 0.0), axis=-1, keepdims=True)
    s1 = jnp.sum(jnp.where(left, 0.0, sq), axis=-1, keepdims=True)
    ms = jnp.where(left, s0, s1) * (1.0 / HEAD_DIM)
    return x * lax.rsqrt(ms + EPS) * gain


def _pad_rows8(x):
    return jnp.concatenate([x, jnp.zeros((8 - x.shape[0], x.shape[1]), x.dtype)], axis=0)


def _compress_rows(rows_scr, kv, n_chunks, w_ref, pos, w2_ref):
    acc = jnp.zeros((n_chunks, 2 * A_KVW), F32)
    for sp in range(CMP_STRIDE // 2):
        x = jnp.concatenate([rows_scr.at[kv][pl.ds(2 * sp + e, n_chunks, stride=CMP_STRIDE), :] for e in range(2)],
                            axis=1).astype(BF16)
        acc = acc + jnp.dot(x, w_ref[sp], preferred_element_type=F32)
    return _compress_tail(acc, pos, w2_ref[...])


def _compress_weights(cmp_pos, cmp_w1, cmp_w2, k_gain):
    eye = jnp.eye(A_KV, dtype=F32)
    w1 = cmp_w1.reshape(2, 2, CMP_STRIDE, HEAD_DIM, HEAD_DIM)
    w_big = jnp.einsum('ajsdh,gk->asgdjkh', w1, eye).reshape(2, CMP_STRIDE // 2, 2 * A_KVW, 2 * A_KVW).astype(BF16)
    w_pos = jnp.concatenate([cmp_w1, cmp_w1], axis=-1).astype(BF16)
    pos = jnp.broadcast_to(cmp_pos.reshape(2, 1, CMP_LEN * HEAD_DIM), (2, 8, CMP_LEN * HEAD_DIM)).astype(BF16)
    w2_big = jnp.einsum('ahd,gk->aghkd', cmp_w2, eye).reshape(2, A_KVW, A_KVW).astype(BF16)
    gain2 = jnp.tile(k_gain, A_KV).reshape(1, A_KVW)
    return (pos[0], pos[1], w_pos[0], w_pos[1], w_big[0], w_big[1], w2_big[0], w2_big[1], gain2)


def _compress_specs(full):
    mlp_w = (CMP_STRIDE // 2, 2 * A_KVW, 2 * A_KVW)
    return [full((8, CMP_LEN * HEAD_DIM)), full((8, CMP_LEN * HEAD_DIM)),
            full((CMP_LEN * HEAD_DIM, A_KVW)), full((CMP_LEN * HEAD_DIM, A_KVW)),
            full(mlp_w), full(mlp_w), full((A_KVW, A_KVW)), full((A_KVW, A_KVW)), full((1, A_KVW))]


def _pos_terms(pos_scr, posk_ref, posv_ref, wpos_k_ref, wpos_v_ref):
    pos_scr[0] = jnp.dot(posk_ref[...], wpos_k_ref[...], preferred_element_type=F32)
    pos_scr[1] = jnp.dot(posv_ref[...], wpos_v_ref[...], preferred_element_type=F32)


def _cmp_prompt_body(cmp_ref, posk_ref, posv_ref, wpos_k_ref, wpos_v_ref, wk_ref, wv_ref, w2k_ref, w2v_ref,
                     kgain_ref, kc_ref, vc_ref, pos_scr, rows_scr, *, n_chunks):
    @pl.when(pl.program_id(0) == 0)
    def _():
        _pos_terms(pos_scr, posk_ref, posv_ref, wpos_k_ref, wpos_v_ref)

    rows_scr[0] = cmp_ref[:, :A_KVW]
    rows_scr[1] = cmp_ref[:, A_KVW:]
    kc = _rms_heads128(_compress_rows(rows_scr, 0, n_chunks, wk_ref, pos_scr[0, 0:1], w2k_ref),
                       kgain_ref[...]).astype(BF16)
    vc = _compress_rows(rows_scr, 1, n_chunks, wv_ref, pos_scr[1, 0:1], w2v_ref).astype(BF16)
    for g in range(A_KV):
        kc_ref[g] = kc[:, g * HEAD_DIM:(g + 1) * HEAD_DIM]
        vc_ref[g] = vc[:, g * HEAD_DIM:(g + 1) * HEAD_DIM]


def _cmp_prompt(cmp_rows, cmp_weights):
    b, t, _ = cmp_rows.shape
    n_chunks = t // CMP_STRIDE
    full = lambda shape: pl.BlockSpec(shape, lambda i: (0,) * len(shape))
    out_spec = pl.BlockSpec((None, A_KV, n_chunks, HEAD_DIM), lambda i: (i, 0, 0, 0))
    out_shape = jax.ShapeDtypeStruct((b, A_KV, n_chunks, HEAD_DIM), BF16)
    return pl.pallas_call(
        functools.partial(_cmp_prompt_body, n_chunks=n_chunks),
        grid=(b,),
        in_specs=[pl.BlockSpec((None, t, 2 * A_KVW), lambda i: (i, 0, 0))] + _compress_specs(full),
        out_specs=[out_spec, out_spec],
        out_shape=[out_shape, out_shape],
        scratch_shapes=[pltpu.VMEM((2, 8, A_KVW), F32), pltpu.VMEM((2, t, A_KVW), F32)],
        compiler_params=pltpu.CompilerParams(
            dimension_semantics=("arbitrary",), vmem_limit_bytes=V7X_VMEM_LIMIT_BYTES),
        name="nsa_cmp_prompt",
    )(cmp_rows, *cmp_weights)


def _nsa_sample_cmp_body(pt_ref, *refs, n_pages, n_sel):
    page_refs = refs[:n_pages]
    (q_ref, posk_ref, posv_ref, wpos_k_ref, wpos_v_ref, wk_ref, wv_ref, w2k_ref, w2v_ref, kgain_ref, cover_ref,
     oc_ref, idx_ref, pos_scr, rows_scr) = refs[n_pages:]

    @pl.when(pl.program_id(0) == 0)
    def _():
        _pos_terms(pos_scr, posk_ref, posv_ref, wpos_k_ref, wpos_v_ref)

    n_chunks = n_pages * CHUNKS_PER_PAGE
    n_cmp = n_chunks - 1
    for kv in range(2):
        for p in range(n_pages):
            rows_scr[kv, p * PAGE_SIZE:(p + 1) * PAGE_SIZE, :] = page_refs[p][kv].reshape(A_KVW, PAGE_SIZE).T

    kc = _rms_heads128(_compress_rows(rows_scr, 0, n_chunks, wk_ref, pos_scr[0, 0:1], w2k_ref),
                       kgain_ref[...]).astype(BF16)
    vc = _compress_rows(rows_scr, 1, n_chunks, wv_ref, pos_scr[1, 0:1], w2v_ref).astype(BF16)

    col = lax.broadcasted_iota(jnp.int32, (8, n_chunks), 1)
    s_idx = lax.broadcasted_iota(jnp.int32, (1, SEL_PAD), 1)
    s_idx_f = s_idx.astype(F32)
    forced = (s_idx == 0) | (s_idx == n_sel - 1) | (s_idx == n_sel - 2)
    lane = lax.broadcasted_iota(jnp.int32, (1, 128), 1)
    for g in range(A_KV):
        hs = slice(g * HEAD_DIM, (g + 1) * HEAD_DIM)
        q8 = (_pad_rows8(q_ref[g]) * (HEAD_DIM ** -0.5)).astype(BF16)
        s = jnp.where(col < n_cmp, _dot_nt(q8, kc[:, hs]), NEG_INF)
        e = jnp.exp(s - jnp.max(s, axis=-1, keepdims=True))
        p = (e / jnp.sum(e, axis=-1, keepdims=True)).astype(BF16)
        oc_ref[g] = jnp.dot(p, vc[:, hs], preferred_element_type=F32)[0:A_REP]
        imp = jnp.sum(jnp.dot(p, cover_ref[...], preferred_element_type=F32)[0:A_REP], axis=0, keepdims=True)
        score = jnp.where(s_idx < n_sel, imp + jnp.where(forced, FORCE_BONUS, 0.0), NEG_INF)
        rank = jnp.zeros((1, SEL_PAD), F32)
        for j in range(n_sel):
            cj = score[:, j:j + 1]
            beats = (cj > score) | ((cj == score) & (s_idx > j))
            rank = rank + jnp.where(beats, 1.0, 0.0)
        picks = jnp.zeros((1, 128), F32)
        for r in range(SEL_TOPK):
            block = jnp.sum(jnp.where(rank == float(r), s_idx_f, 0.0), axis=-1, keepdims=True)
            picks = picks + jnp.where(lane == r, block, 0.0)
        idx_ref[g:g + 1, :] = picks.astype(jnp.int32)


def _nsa_sample_att_body(idx_ref, pt_ref, *refs, n_past_blk):
    n_slots = A_KV * SEL_TOPK
    slot_refs = refs[:n_slots]
    (q_all, gate_all, oc_all, newsel_all, wcache_all, newwin_all, selb_all, winb_all, newb_all, o_all) = refs[n_slots:]
    b = pl.program_id(0)
    for g in range(A_KV):
        _nsa_sample_att_group(b, g, idx_ref, slot_refs[g * SEL_TOPK:(g + 1) * SEL_TOPK], q_all.at[g], gate_all.at[g],
                              oc_all.at[g], newsel_all.at[g], wcache_all.at[:, g], newwin_all.at[g], selb_all.at[g],
                              winb_all.at[g], newb_all.at[g], o_all.at[g], n_past_blk)


def _nsa_sample_att_group(b, g, idx_ref, page_refs, q_ref, gate_ref, oc_ref, newsel_ref, wcache_ref, newwin_ref,
                          selb_ref, winb_ref, newb_ref, o_ref, n_past_blk):
    blocks_per_page = PAGE_SIZE // SEL_BLOCK
    q8 = (_pad_rows8(q_ref[...]) * (HEAD_DIM ** -0.5)).astype(BF16)
    new_bias = newb_ref[:, 0:1]

    def attend(s, v_t, new_kv):
        k_new = _bf16_round(new_kv[0:1])
        v_new = _bf16_round(new_kv[1:2])
        s_new = jnp.sum(q8.astype(F32) * k_new, axis=-1, keepdims=True) + new_bias
        m = jnp.maximum(jnp.max(s, axis=-1, keepdims=True), s_new)
        e = jnp.exp(s - m)
        e_new = jnp.exp(s_new - m)
        den = jnp.sum(e, axis=-1, keepdims=True) + e_new
        return _dot_nt((e / den).astype(BF16), v_t) + _bf16_round(e_new / den) * v_new

    lane = lax.broadcasted_iota(jnp.int32, (8, PAGE_SIZE), 1)
    near = selb_ref[...]
    bias = []
    for k in range(SEL_TOPK):
        ik = idx_ref[(b * A_KV + g) * SEL_TOPK + k]
        blk = jnp.minimum(ik, n_past_blk - 1)
        near_k = jnp.where(blk // blocks_per_page == (n_past_blk - 1) // blocks_per_page, near, 0.0)
        keep = (lane // SEL_BLOCK == blk % blocks_per_page) & (ik < n_past_blk)
        bias.append(jnp.where(keep, near_k, NEG_INF))
    k_sel = jnp.concatenate([page_refs[k][0].astype(BF16) for k in range(SEL_TOPK)], axis=1)
    v_sel = jnp.concatenate([page_refs[k][1].astype(BF16) for k in range(SEL_TOPK)], axis=1)
    s_sel = jnp.dot(q8, k_sel, preferred_element_type=F32) + jnp.concatenate(bias, axis=1)
    o_s = attend(s_sel, v_sel, newsel_ref[...])

    w_len = wcache_ref.shape[-1]
    wcol = lax.broadcasted_iota(jnp.int32, (8, w_len), 1)
    s_w = jnp.dot(q8, wcache_ref[0].astype(BF16), preferred_element_type=F32) + winb_ref[...]
    o_w = attend(jnp.where(wcol >= 1, s_w, NEG_INF), wcache_ref[1].astype(BF16), newwin_ref[...])
    gates = gate_ref[...]
    o_ref[...] = gates[:, 0:1] * oc_ref[...] + gates[:, 1:2] * o_s[0:A_REP] + gates[:, 2:3] * o_w[0:A_REP]


def _rows_last(cache):
    nd = cache.ndim
    return jnp.transpose(cache, tuple(range(nd - 4)) + (nd - 3, nd - 2, nd - 1, nd - 4))


def _nsa_sample(q, gates, new_sel, new_win, pool_cmp, pool_sel, win_cache, page_table, cmp_weights, tab):
    n, n_pages = page_table.shape
    past = n_pages * PAGE_SIZE
    n_past_blk = past // SEL_BLOCK
    n_sel = n_past_blk + 1
    n_chunks = n_pages * CHUNKS_PER_PAGE
    w_len = win_cache.shape[1]
    blocks_per_page = PAGE_SIZE // SEL_BLOCK
    assert n_sel <= SEL_PAD and w_len == WIN_A and past >= WIN_A and blocks_per_page == 2
    pt_flat = page_table.reshape(-1)

    c_idx = np.arange(n_chunks)
    s_start = np.arange(SEL_PAD) * SEL_BLOCK
    cover = ((c_idx[:, None] * CMP_STRIDE < s_start[None, :] + SEL_BLOCK)
             & (c_idx[:, None] * CMP_STRIDE + CMP_LEN - 1 >= s_start[None, :])
             & (c_idx[:, None] < n_chunks - 1) & (np.arange(SEL_PAD)[None, :] < n_sel))
    cover = jnp.asarray(cover.astype(np.float32), BF16)

    head_spec = lambda last: pl.BlockSpec((None, A_KV, A_REP, last), lambda i, *_: (i, 0, 0, 0))
    page_specs = [pl.BlockSpec((None, 2, A_KV, HEAD_DIM, PAGE_SIZE),
                               functools.partial(lambda i, pt, p: (pt[i * n_pages + p], 0, 0, 0, 0), p=p))
                  for p in range(n_pages)]
    full = lambda shape: pl.BlockSpec(shape, lambda i, *_: (0,) * len(shape))
    pool_c = _rows_last(pool_cmp)
    o_c, idx = pl.pallas_call(
        functools.partial(_nsa_sample_cmp_body, n_pages=n_pages, n_sel=n_sel),
        grid_spec=pltpu.PrefetchScalarGridSpec(
            num_scalar_prefetch=1,
            grid=(n,),
            in_specs=page_specs + [head_spec(HEAD_DIM)] + _compress_specs(full) + [full((n_chunks, SEL_PAD))],
            out_specs=[head_spec(HEAD_DIM), pl.BlockSpec((None, A_KV, 128), lambda i, *_: (i, 0, 0))],
            scratch_shapes=[pltpu.VMEM((2, 8, A_KVW), F32), pltpu.VMEM((2, past, A_KVW), F32)],
        ),
        out_shape=[jax.ShapeDtypeStruct((n, A_KV, A_REP, HEAD_DIM), F32),
                   jax.ShapeDtypeStruct((n, A_KV, 128), jnp.int32)],
        compiler_params=pltpu.CompilerParams(
            dimension_semantics=("arbitrary",), vmem_limit_bytes=V7X_VMEM_LIMIT_BYTES),
        name="nsa_sample_cmp",
    )(pt_flat, *([pool_c] * n_pages), q, *cmp_weights, cover)

    rel = lambda dist: jnp.pad(jnp.transpose(_bias_lookup(tab - tab[NUM_BUCKETS - 1], dist), (1, 2, 0)),
                               ((0, 0), (0, 8 - A_REP), (0, 0)))
    sel_bias = rel(past - (past - PAGE_SIZE + np.arange(PAGE_SIZE)))
    win_bias = rel(w_len - np.arange(w_len))
    new_bias = rel(np.zeros(128, np.int64))

    idx_flat = idx[:, :, :SEL_TOPK].reshape(-1)

    def page_map(i, idx_s, pt, g, slot):
        blk = jnp.minimum(idx_s[(i * A_KV + g) * SEL_TOPK + slot], n_past_blk - 1)
        return (pt[i * n_pages + blk // blocks_per_page], 0, g, 0, 0)

    slot_specs = [pl.BlockSpec((None, 2, None, HEAD_DIM, PAGE_SIZE), functools.partial(page_map, g=g, slot=s))
                  for g in range(A_KV) for s in range(SEL_TOPK)]
    per_head = lambda rows, last: pl.BlockSpec((None, A_KV, rows, last), lambda i, *_: (i, 0, 0, 0))
    per_group = lambda last: pl.BlockSpec((A_KV, 8, last), lambda i, *_: (0, 0, 0))
    return pl.pallas_call(
        functools.partial(_nsa_sample_att_body, n_past_blk=n_past_blk),
        grid_spec=pltpu.PrefetchScalarGridSpec(
            num_scalar_prefetch=2,
            grid=(n,),
            in_specs=slot_specs + [
                per_head(A_REP, HEAD_DIM), per_head(A_REP, 3), per_head(A_REP, HEAD_DIM), per_head(2, HEAD_DIM),
                pl.BlockSpec((None, 2, A_KV, HEAD_DIM, w_len), lambda i, *_: (i, 0, 0, 0, 0)),
                per_head(2, HEAD_DIM), per_group(PAGE_SIZE), per_group(w_len), per_group(128),
            ],
            out_specs=per_head(A_REP, HEAD_DIM),
        ),
        out_shape=jax.ShapeDtypeStruct((n, A_KV, A_REP, HEAD_DIM), F32),
        compiler_params=pltpu.CompilerParams(
            dimension_semantics=("arbitrary",), vmem_limit_bytes=V7X_VMEM_LIMIT_BYTES),
        name="nsa_sample_att",
    )(idx_flat, pt_flat, *([_rows_last(pool_sel)] * (A_KV * SEL_TOPK)), q, gates, o_c,
      jnp.transpose(new_sel, (0, 2, 1, 3)), _rows_last(win_cache), jnp.transpose(new_win, (0, 2, 1, 3)),
      sel_bias, win_bias, new_bias)


SWA_Q_TILE = WIN_C


def _swa_prompt_body(q_ref, k_ref, v_ref, btile_ref, sink_ref, o_ref, *, tq):
    qt = pl.program_id(2)
    q0 = pl.multiple_of(qt * tq, tq)
    q = q_ref[...]
    qs = jnp.concatenate([q[:, r * HEAD_DIM:(r + 1) * HEAD_DIM] for r in range(C_REP)], axis=0)
    qs = (qs * (HEAD_DIM ** -0.5)).astype(BF16)
    prev0 = pl.multiple_of(jnp.maximum(q0 - tq, 0), tq)
    k = jnp.concatenate([k_ref[pl.ds(prev0, tq), :], k_ref[pl.ds(q0, tq), :]], axis=0)
    v = jnp.concatenate([v_ref[pl.ds(prev0, tq), :], v_ref[pl.ds(q0, tq), :]], axis=0)
    s = _dot_nt(k, qs) + btile_ref[...]
    sinks = sink_ref[...]
    sink = jnp.concatenate([sinks[r:r + 1, :] for r in range(C_REP)], axis=1)
    m = jnp.maximum(jnp.max(s, axis=0, keepdims=True), sink)
    e = jnp.exp(s - m)
    p = e / (jnp.sum(e, axis=0, keepdims=True) + jnp.exp(sink - m))
    o = lax.dot_general(v, p.astype(BF16), (((0,), (0,)), ((), ())), preferred_element_type=F32)
    o_ref[...] = jnp.concatenate([o[:, r * tq:(r + 1) * tq].T for r in range(C_REP)], axis=1)


def _swa_prompt(q, kvt, sinks, tab):
    b, t, _ = q.shape
    tq = SWA_Q_TILE
    btile = jnp.swapaxes(_near_bias_tiles(tab, tq, WIN_C), -1, -2)
    kv_spec = lambda slot: pl.BlockSpec((None, None, t, HEAD_DIM), lambda i, g, j: (i, slot + g, 0, 0))
    qo_spec = pl.BlockSpec((None, tq, C_REP * HEAD_DIM), lambda i, g, j: (i, j, g))
    sink_lanes = jnp.broadcast_to(sinks.reshape(C_KV, C_REP, 1), (C_KV, C_REP, 128))
    return pl.pallas_call(
        functools.partial(_swa_prompt_body, tq=tq),
        grid=(b, C_KV, t // tq),
        in_specs=[qo_spec, kv_spec(0), kv_spec(C_KV),
                  pl.BlockSpec((None, None, 2 * tq, C_REP * tq), lambda i, g, j: (jnp.minimum(j, 1), g, 0, 0)),
                  pl.BlockSpec((None, C_REP, 128), lambda i, g, j: (g, 0, 0))],
        out_specs=qo_spec,
        out_shape=jax.ShapeDtypeStruct((b, t, C_HEADS * HEAD_DIM), F32),
        compiler_params=pltpu.CompilerParams(
            dimension_semantics=("parallel", "parallel", "arbitrary"), vmem_limit_bytes=V7X_VMEM_LIMIT_BYTES),
        name="swa_prompt",
    )(q, kvt, kvt, btile, sink_lanes)


def _swa_sample_body(q_ref, cache_ref, new_ref, bias_ref, newb_ref, sink_ref, o_ref):
    w_len = cache_ref.shape[-1]
    wcol = lax.broadcasted_iota(jnp.int32, (C_REP, w_len), 1)
    for g in range(C_KV):
        q8 = (q_ref[g] * (HEAD_DIM ** -0.5)).astype(BF16)
        s = jnp.dot(q8, cache_ref[0, g].astype(BF16), preferred_element_type=F32) + bias_ref[g]
        s = jnp.where(wcol >= 1, s, NEG_INF)
        k_new = _bf16_round(new_ref[g, 0:1])
        v_new = _bf16_round(new_ref[g, 1:2])
        s_new = jnp.sum(q8.astype(F32) * k_new, axis=-1, keepdims=True) + newb_ref[g][:, 0:1]
        sink = sink_ref[g][:, 0:1]
        m = jnp.maximum(jnp.maximum(jnp.max(s, axis=-1, keepdims=True), s_new), sink)
        e = jnp.exp(s - m)
        e_new = jnp.exp(s_new - m)
        den = jnp.sum(e, axis=-1, keepdims=True) + e_new + jnp.exp(sink - m)
        o_ref[g] = _dot_nt((e / den).astype(BF16), cache_ref[1, g].astype(BF16)) + _bf16_round(e_new / den) * v_new


def _swa_sample(q, cache, new_kv, sinks, tab):
    n, w_len = cache.shape[:2]
    bias = jnp.transpose(_bias_lookup(tab, w_len - np.arange(w_len)), (1, 2, 0))
    lanes = lambda x: jnp.broadcast_to(x[:, :, None], (C_KV, C_REP, 128))
    full = lambda shape: pl.BlockSpec(shape, lambda i: (0,) * len(shape))
    return pl.pallas_call(
        _swa_sample_body,
        grid=(n,),
        in_specs=[
            pl.BlockSpec((None, C_KV, C_REP, HEAD_DIM), lambda i: (i, 0, 0, 0)),
            pl.BlockSpec((None, 2, C_KV, HEAD_DIM, w_len), lambda i: (i, 0, 0, 0, 0)),
            pl.BlockSpec((None, C_KV, 2, HEAD_DIM), lambda i: (i, 0, 0, 0)),
            full((C_KV, C_REP, w_len)), full((C_KV, C_REP, 128)), full((C_KV, C_REP, 128)),
        ],
        out_specs=pl.BlockSpec((None, C_KV, C_REP, HEAD_DIM), lambda i: (i, 0, 0, 0)),
        out_shape=jax.ShapeDtypeStruct((n, C_KV, C_REP, HEAD_DIM), F32),
        compiler_params=pltpu.CompilerParams(
            dimension_semantics=("arbitrary",), vmem_limit_bytes=V7X_VMEM_LIMIT_BYTES),
        name="swa_sample",
    )(q, _rows_last(cache), new_kv, bias, lanes(tab[0]), lanes(sinks.reshape(C_KV, C_REP)))


PROJ_ROW_TILE = 512
A_IN_PAD = 1920
A_U_COL = A_Q + 6 * A_KVW
A_GATE_COL = A_U_COL + B_WIDTH


def _heads_first(kvt_ref, slot, k, v, n_kv):
    for g in range(n_kv):
        kvt_ref[slot + g] = k[:, g * HEAD_DIM:(g + 1) * HEAD_DIM].astype(BF16)
        kvt_ref[slot + n_kv + g] = v[:, g * HEAD_DIM:(g + 1) * HEAD_DIM].astype(BF16)


def _inproj_a_body(x_ref, gain_ref, w_ref, qg_ref, kg_ref,
                   q_ref, cmp_ref, sel_ref, win_ref, gate_ref, u_ref, kvt_ref):
    xn = _rms_rows(x_ref[...], gain_ref[...]).astype(BF16)
    z = jnp.dot(xn, w_ref[...], preferred_element_type=F32)
    for j in range(A_Q // 128):
        q_ref[:, j * 128:(j + 1) * 128] = _rms_heads128(z[:, j * 128:(j + 1) * 128], qg_ref[...])
    for out_ref, off, slot in ((cmp_ref, A_Q, None), (sel_ref, A_Q + 2 * A_KVW, 0), (win_ref, A_Q + 4 * A_KVW, 4)):
        k = _rms_heads128(z[:, off:off + A_KVW], kg_ref[...])
        v = z[:, off + A_KVW:off + 2 * A_KVW]
        out_ref[:, :A_KVW] = k
        out_ref[:, A_KVW:] = v
        if slot is not None:
            _heads_first(kvt_ref, slot, k, v, A_KV)
    u_ref[...] = z[:, A_U_COL:A_GATE_COL]
    gate_ref[...] = 1.0 / (1.0 + jnp.exp(-z[:, A_GATE_COL:A_IN_PAD]))


def _inproj_a(x, gain, w_in, q_gain, k_gain):
    b, t, d = x.shape
    tm = min(t, PROJ_ROW_TILE)
    w = jnp.concatenate([w_in[:, :A_U_COL], w_in[:, A_U_COL + A_GATE:], w_in[:, A_U_COL:A_U_COL + A_GATE],
                         jnp.zeros((d, A_IN_PAD - A_GATE_COL - A_GATE), F32)], axis=1).astype(BF16)
    tile2 = lambda g: jnp.tile(g, 2).reshape(1, 128)
    rows = lambda width: pl.BlockSpec((None, tm, width), lambda i, j: (i, j, 0))
    shape = lambda width: jax.ShapeDtypeStruct((b, t, width), F32)
    return pl.pallas_call(
        _inproj_a_body,
        grid=(b, t // tm),
        in_specs=[rows(d), _const_spec((1, d)), _const_spec((d, A_IN_PAD)), _const_spec((1, 128)), _const_spec((1, 128))],
        out_specs=[rows(A_Q), rows(2 * A_KVW), rows(2 * A_KVW), rows(2 * A_KVW), rows(128), rows(B_WIDTH),
                   pl.BlockSpec((None, 8, tm, HEAD_DIM), lambda i, j: (i, 0, j, 0))],
        out_shape=[shape(A_Q), shape(2 * A_KVW), shape(2 * A_KVW), shape(2 * A_KVW), shape(128), shape(B_WIDTH),
                   jax.ShapeDtypeStruct((b, 8, t, HEAD_DIM), BF16)],
        compiler_params=pltpu.CompilerParams(
            dimension_semantics=("parallel", "parallel"), vmem_limit_bytes=V7X_VMEM_LIMIT_BYTES),
        name="inproj_nsa_s5",
    )(x, gain.reshape(1, d), w, tile2(q_gain), tile2(k_gain))


def _inproj_c_body(x_ref, gain_ref, w_ref, qg_ref, kg_ref, q_ref, kv_ref, kvt_ref):
    xn = _rms_rows(x_ref[...], gain_ref[...]).astype(BF16)
    z = jnp.dot(xn, w_ref[...], preferred_element_type=F32)
    n_q = C_HEADS * HEAD_DIM
    for j in range(n_q // 128):
        q_ref[:, j * 128:(j + 1) * 128] = _rms_heads128(z[:, j * 128:(j + 1) * 128], qg_ref[...])
    k = _rms_heads128(z[:, n_q:n_q + C_KV * HEAD_DIM], kg_ref[...])
    v = z[:, n_q + C_KV * HEAD_DIM:]
    kv_ref[:, :C_KV * HEAD_DIM] = k
    kv_ref[:, C_KV * HEAD_DIM:] = v
    _heads_first(kvt_ref, 0, k, v, C_KV)


def _inproj_c(x, gain, w_in, q_gain, k_gain):
    b, t, d = x.shape
    tm = min(t, PROJ_ROW_TILE)
    n_in = w_in.shape[1]
    tile2 = lambda g: jnp.tile(g, 2).reshape(1, 128)
    rows = lambda width: pl.BlockSpec((None, tm, width), lambda i, j: (i, j, 0))
    shape = lambda width: jax.ShapeDtypeStruct((b, t, width), F32)
    return pl.pallas_call(
        _inproj_c_body,
        grid=(b, t // tm),
        in_specs=[rows(d), _const_spec((1, d)), _const_spec((d, n_in)), _const_spec((1, 128)), _const_spec((1, 128))],
        out_specs=[rows(C_HEADS * HEAD_DIM), rows(2 * C_KV * HEAD_DIM),
                   pl.BlockSpec((None, 2 * C_KV, tm, HEAD_DIM), lambda i, j: (i, 0, j, 0))],
        out_shape=[shape(C_HEADS * HEAD_DIM), shape(2 * C_KV * HEAD_DIM),
                   jax.ShapeDtypeStruct((b, 2 * C_KV, t, HEAD_DIM), BF16)],
        compiler_params=pltpu.CompilerParams(
            dimension_semantics=("parallel", "parallel"), vmem_limit_bytes=V7X_VMEM_LIMIT_BYTES),
        name="inproj_swa",
    )(x, gain.reshape(1, d), w_in.astype(BF16), tile2(q_gain), tile2(k_gain))


FFN_ROW_TILE = 512
FFN_COL_CHUNK = 1408


def _mixer_residual(y_ref, x_ref, mix_refs, wout_refs):
    y_ref[...] = x_ref[...]
    for m_ref, w_ref in zip(mix_refs, wout_refs):
        y_ref[...] += jnp.dot(m_ref[...].astype(BF16), w_ref[...], preferred_element_type=F32)
    return y_ref[...]


def _tail_prompt_body(*refs, n_mix, tm, ffc):
    x_ref = refs[0]
    mix_refs = refs[1:1 + n_mix]
    wout_refs = refs[1 + n_mix:1 + 2 * n_mix]
    (gain_ref, prev_ref, wup_ref, wgate_ref, cw_ref, cb_ref, wdown_ref, y_ref, cs_ref, hbuf_ref) = refs[1 + 2 * n_mix:]
    t = pl.program_id(1)
    xn = _rms_rows(_mixer_residual(y_ref, x_ref, mix_refs, wout_refs), gain_ref[...]).astype(BF16)
    for c in range(D_FF // ffc):
        lo = c * ffc
        h = jnp.dot(xn, wup_ref[:, lo:lo + ffc], preferred_element_type=F32)
        g = jnp.dot(xn, wgate_ref[:, lo:lo + ffc], preferred_element_type=F32)

        @pl.when(t == 0)
        def _():
            hbuf_ref[c, 6:8, :] = prev_ref[:, lo:lo + ffc]

        hbuf_ref[c, 8:8 + tm, :] = h
        hm1 = hbuf_ref[c, 7:7 + tm, :]
        hm2 = hbuf_ref[c, 6:6 + tm, :]
        cw = cw_ref[:, lo:lo + ffc]
        hc = cw[0:1] * hm2 + cw[1:2] * hm1 + cw[2:3] * h + cb_ref[:, lo:lo + ffc]
        a = (_gelu_tanh(hc) * g).astype(BF16)
        y_ref[...] += jnp.dot(a, wdown_ref[lo:lo + ffc, :], preferred_element_type=F32)
        hbuf_ref[c, 0:8, :] = h[tm - 8:tm, :]
        cs_ref[:, lo:lo + ffc] = h[tm - 2:tm, :]


def _tail_prompt(x, mixes, wouts, gain, prev, wup, wgate, cw, cb, wdown):
    b, t, d = x.shape
    tm, ffc = FFN_ROW_TILE, FFN_COL_CHUNK
    rows = lambda width: pl.BlockSpec((None, tm, width), lambda i, j: (i, j, 0))
    state = pl.BlockSpec((None, CONV_W - 1, D_FF), lambda i, j: (i, 0, 0))
    return pl.pallas_call(
        functools.partial(_tail_prompt_body, n_mix=len(mixes), tm=tm, ffc=ffc),
        grid=(b, t // tm),
        in_specs=[rows(d)] + [rows(m.shape[-1]) for m in mixes] + [_const_spec(w.shape) for w in wouts] + [
            _const_spec((1, d)), state, _const_spec((d, D_FF)), _const_spec((d, D_FF)),
            _const_spec((CONV_W, D_FF)), _const_spec((1, D_FF)), _const_spec((D_FF, d))],
        out_specs=[rows(d), state],
        out_shape=[jax.ShapeDtypeStruct((b, t, d), F32), jax.ShapeDtypeStruct((b, CONV_W - 1, D_FF), F32)],
        scratch_shapes=[pltpu.VMEM((D_FF // ffc, 8 + tm, ffc), F32)],
        compiler_params=pltpu.CompilerParams(
            dimension_semantics=("parallel", "arbitrary"), vmem_limit_bytes=V7X_VMEM_LIMIT_BYTES),
        name="tail_prompt",
    )(x, *mixes, *wouts, gain, prev, wup, wgate, cw, cb, wdown)


def _tail_sample_body(*refs, n_mix, ffc):
    x_ref = refs[0]
    mix_refs = refs[1:1 + n_mix]
    wout_refs = refs[1 + n_mix:1 + 2 * n_mix]
    (gain_ref, prev_ref, wup_ref, wgate_ref, cw_ref, cb_ref, wdown_ref, y_ref, cs_ref) = refs[1 + 2 * n_mix:]
    xn = _rms_rows(_mixer_residual(y_ref, x_ref, mix_refs, wout_refs), gain_ref[...]).astype(BF16)
    for c in range(D_FF // ffc):
        lo = c * ffc
        h = jnp.dot(xn, wup_ref[:, lo:lo + ffc], preferred_element_type=F32)
        g = jnp.dot(xn, wgate_ref[:, lo:lo + ffc], preferred_element_type=F32)
        hm2 = prev_ref[:, lo:lo + ffc]
        hm1 = prev_ref[:, D_FF + lo:D_FF + lo + ffc]
        cw = cw_ref[:, lo:lo + ffc]
        hc = cw[0:1] * hm2 + cw[1:2] * hm1 + cw[2:3] * h + cb_ref[:, lo:lo + ffc]
        a = (_gelu_tanh(hc) * g).astype(BF16)
        y_ref[...] += jnp.dot(a, wdown_ref[lo:lo + ffc, :], preferred_element_type=F32)
        cs_ref[:, lo:lo + ffc] = hm1
        cs_ref[:, D_FF + lo:D_FF + lo + ffc] = h


def _tail_sample(x, mixes, wouts, gain, prev, wup, wgate, cw, cb, wdown):
    n, d = x.shape
    full = lambda shape: _const_spec(shape)
    return pl.pallas_call(
        functools.partial(_tail_sample_body, n_mix=len(mixes), ffc=FFN_COL_CHUNK),
        grid=(1,),
        in_specs=[full((n, d))] + [full(m.shape) for m in mixes] + [full(w.shape) for w in wouts] + [
            full((1, d)), full((n, (CONV_W - 1) * D_FF)), full((d, D_FF)), full((d, D_FF)),
            full((CONV_W, D_FF)), full((1, D_FF)), full((D_FF, d))],
        out_specs=[pl.BlockSpec((n, d), lambda i: (0, 0)), pl.BlockSpec((n, (CONV_W - 1) * D_FF), lambda i: (0, 0))],
        out_shape=[jax.ShapeDtypeStruct((n, d), F32), jax.ShapeDtypeStruct((n, (CONV_W - 1) * D_FF), F32)],
        compiler_params=pltpu.CompilerParams(
            dimension_semantics=("arbitrary",), vmem_limit_bytes=V7X_VMEM_LIMIT_BYTES),
        name="tail_sample",
    )(x, *mixes, *wouts, gain, prev, wup, wgate, cw, cb, wdown)


def kernel(x_prompt, x_sample, cache_nsa_cmp, cache_nsa_sel, cache_nsa_win, state_s5_re, state_s5_im,
           cache_swa, state_ffn_conv, page_table, rel_bias, norm_mix, norm_ffn, a_w_in, a_w_out,
           nsa_q_gain, nsa_k_gain, nsa_cmp_pos, nsa_cmp_w1, nsa_cmp_w2, s5_a_re, s5_a_im, s5_log_dt,
           s5_b_re, s5_b_im, s5_c_re, s5_c_im, s5_d, s5_w_glu, s5_b_glu, c_w_in, c_w_out, c_q_gain,
           c_k_gain, c_sinks, ffn_w_up, ffn_w_gate, ffn_conv_w, ffn_conv_b, ffn_w_down):
    bp, tp, _ = x_prompt.shape
    bs, ts, _ = x_sample.shape
    assert ts == 1 and DEPTH == 2
    tab_a = rel_bias[:, :A_HEADS].reshape(NUM_BUCKETS, A_KV, A_REP)
    tab_c = rel_bias[:, :C_HEADS].reshape(NUM_BUCKETS, C_KV, C_REP)
    kv6 = lambda x: x.reshape(x.shape[:-1] + (2, x.shape[-1] // (2 * HEAD_DIM), HEAD_DIM))
    hp, hs = x_prompt, x_sample.reshape(1, bs, D_MODEL)
    conv_p, conv_s = [], []

    def tail(layer, hp, hs, mixes_p, mixes_s, wouts):
        wouts = [w.astype(BF16) for w in wouts]
        ffn = (norm_ffn[layer].reshape(1, D_MODEL),)
        wts = (ffn_w_up[layer].astype(BF16), ffn_w_gate[layer].astype(BF16), ffn_conv_w[layer],
               ffn_conv_b[layer].reshape(1, D_FF), ffn_w_down[layer].astype(BF16))
        hp, cp = _tail_prompt(hp, mixes_p, wouts, *ffn, jnp.zeros((bp, CONV_W - 1, D_FF), F32), *wts)
        hs2, cs = _tail_sample(hs[0], [m.reshape(bs, -1) for m in mixes_s], wouts, *ffn,
                               state_ffn_conv[layer].reshape(bs, (CONV_W - 1) * D_FF), *wts)
        conv_p.append(cp)
        conv_s.append(cs.reshape(bs, CONV_W - 1, D_FF))
        return hp, hs2.reshape(1, bs, D_MODEL)

    proj = (norm_mix[0], a_w_in[0], nsa_q_gain[0], nsa_k_gain[0])
    qp, cmp_p, sel_p, win_p, gate_p, up, kvt_p = _inproj_a(hp, *proj)
    qs, cmp_s, sel_s, win_s, gate_s, us, _ = _inproj_a(hs, *proj)
    cmp_w = _compress_weights(nsa_cmp_pos[0], nsa_cmp_w1[0], nsa_cmp_w2[0], nsa_k_gain[0])
    kc, vc = _cmp_prompt(cmp_p, cmp_w)
    o_ap = _nsa_prompt_attend(qp, gate_p, kc, vc, kvt_p, tab_a)
    o_as = _nsa_sample(qs.reshape(bs, A_KV, A_REP, HEAD_DIM), gate_s[0, :, :A_GATE].reshape(bs, A_KV, A_REP, 3),
                       sel_s.reshape(bs, 2, A_KV, HEAD_DIM), win_s.reshape(bs, 2, A_KV, HEAD_DIM),
                       cache_nsa_cmp[0], cache_nsa_sel[0], cache_nsa_win[0], page_table, cmp_w, tab_a)
    s5w = _s5_weights(s5_a_re[0], s5_a_im[0], s5_log_dt[0], s5_b_re[0], s5_b_im[0], s5_c_re[0], s5_c_im[0],
                      s5_d[0], s5_w_glu[0], s5_b_glu[0])
    h0 = jnp.zeros((bp, S5_NS), F32)
    o_bp, hr_p, hi_p = _s5_mix(up, h0, h0, s5w)
    o_bs, hr_s, hi_s = _s5_mix(us.reshape(bs, 1, B_WIDTH), state_s5_re[0].reshape(bs, S5_NS),
                               state_s5_im[0].reshape(bs, S5_NS), s5w)
    hp, hs = tail(0, hp, hs, [o_ap, o_bp], [o_as, o_bs], [a_w_out[0][:A_Q], a_w_out[0][A_Q:]])

    proj = (norm_mix[1], c_w_in[0], c_q_gain[0], c_k_gain[0])
    qcp, kv_p, kvt_c = _inproj_c(hp, *proj)
    qcs, kv_s, _ = _inproj_c(hs, *proj)
    o_cp = _swa_prompt(qcp, kvt_c, c_sinks[0], tab_c)
    new_kv = kv_s.reshape(bs, 2, C_KV, HEAD_DIM)
    o_cs = _swa_sample(qcs.reshape(bs, C_KV, C_REP, HEAD_DIM), cache_swa[0], jnp.transpose(new_kv, (0, 2, 1, 3)),
                       c_sinks[0], tab_c)
    hp, hs = tail(1, hp, hs, [o_cp], [o_cs], [c_w_out[0]])

    state = lambda x, n: x.reshape(1, n, B_GROUPS, B_STATE)
    nsa_win_s = jnp.concatenate([cache_nsa_win[0][:, 1:], kv6(win_s).reshape(bs, 1, 2, A_KV, HEAD_DIM)], axis=1)
    swa_s = jnp.concatenate([cache_swa[0][:, 1:], new_kv.reshape(bs, 1, 2, C_KV, HEAD_DIM)], axis=1)
    return (hp, hs.reshape(bs, ts, D_MODEL),
            kv6(cmp_p)[None], kv6(cmp_s).reshape(1, bs, ts, 2, A_KV, HEAD_DIM),
            kv6(sel_p)[None], kv6(sel_s).reshape(1, bs, ts, 2, A_KV, HEAD_DIM),
            kv6(win_p)[None, :, -min(WIN_A, tp):], nsa_win_s[None],
            state(hr_p, bp), state(hi_p, bp), state(hr_s, bs), state(hi_s, bs),
            kv6(kv_p)[None, :, -min(WIN_C, tp):], swa_s[None],
            jnp.stack(conv_p), jnp.stack(conv_s))
```

```python
import functools
import math

import jax
import jax.numpy as jnp
import numpy as np
from jax import lax
from jax.experimental import pallas as pl
from jax.experimental.pallas import tpu as pltpu

D_MODEL = 1024
DEPTH = 2
PAGE_SIZE = 128
HEAD_DIM = 64
A_HEADS = 8
A_KV = 2
A_REP = A_HEADS // A_KV
A_Q = A_HEADS * HEAD_DIM
A_KVW = A_KV * HEAD_DIM
A_GATE = 3 * A_HEADS
CMP_LEN = 32
CMP_STRIDE = 16
SEL_BLOCK = 64
SEL_TOPK = 16
WIN_A = 512
NSA_QBLK = 64
FORCE_BONUS = 1000.0
B_WIDTH = D_MODEL // 2
B_GROUP = 16
B_GROUPS = B_WIDTH // B_GROUP
B_STATE = 64
C_HEADS = D_MODEL // HEAD_DIM
C_KV = 2
C_REP = C_HEADS // C_KV
WIN_C = 128
NUM_BUCKETS = 32
MAX_DISTANCE = 128
D_FF = 2816
CONV_W = 3
EPS = 1e-6

F32 = jnp.float32
BF16 = jnp.bfloat16

V7X_VMEM_LIMIT_BYTES = 56 * 1024 * 1024


def _gelu_tanh(x):
    return 0.5 * x * (1.0 + jnp.tanh(math.sqrt(2.0 / math.pi) * (x + 0.044715 * (x * x * x))))


def _rms_rows(x, gain):
    return x * lax.rsqrt(jnp.mean(x * x, axis=-1, keepdims=True) + EPS) * gain


def _const_spec(shape):
    zeros = (0,) * len(shape)
    return pl.BlockSpec(shape, lambda *_: zeros, pipeline_mode=pl.Buffered(1))


NSA_Q_TILE = 128
NSA_FAR_TILE = 512
NSA_NEAR = 2 * NSA_Q_TILE
NEG_INF = float("-inf")


def _dot_nt(a, b):
    return lax.dot_general(a, b, (((1,), (1,)), ((), ())), preferred_element_type=F32)


def _dot_tn(a, b):
    return lax.dot_general(a, b, (((0,), (0,)), ((), ())), preferred_element_type=F32)


def _softmax_start(s, v):
    m = jnp.max(s, axis=0, keepdims=True)
    e = jnp.exp(s - m)
    return m, jnp.sum(e, axis=0, keepdims=True), _dot_tn(v, e.astype(BF16))


def _softmax_more(carry, s, v):
    m, l, acc = carry
    m_new = jnp.maximum(m, jnp.max(s, axis=0, keepdims=True))
    alpha = jnp.exp(m - m_new)
    e = jnp.exp(s - m_new)
    return m_new, alpha * l + jnp.sum(e, axis=0, keepdims=True), alpha * acc + _dot_tn(v, e.astype(BF16))


def _nsa_prompt_body(q_ref, gate_ref, kc_ref, vc_ref, kvt_ref, btile_ref, cover_ref, o_ref, sel_scr, *, tq):
    width = A_REP * HEAD_DIM
    gates_t = gate_ref[...].T
    for g in range(A_KV):
        _nsa_prompt_group(g, q_ref.at[:, g * width:(g + 1) * width], gates_t, kc_ref.at[g], vc_ref.at[g],
                          kvt_ref.at[g], kvt_ref.at[A_KV + g], kvt_ref.at[2 * A_KV + g], kvt_ref.at[3 * A_KV + g],
                          btile_ref.at[g], cover_ref, o_ref.at[:, g * width:(g + 1) * width], sel_scr.at[g], tq)


def _nsa_prompt_group(g, q_ref, gates_t, kc_ref, vc_ref, ks_ref, vs_ref, kw_ref, vw_ref, btile_ref, cover_ref,
                      o_ref, sel_ref, tq):
    qt = pl.program_id(1)
    q0 = pl.multiple_of(qt * tq, tq)
    cols = A_REP * tq
    q = q_ref[...]
    qs = jnp.concatenate([q[:, r * HEAD_DIM:(r + 1) * HEAD_DIM] for r in range(A_REP)], axis=0)
    qs = (qs * (HEAD_DIM ** -0.5)).astype(BF16)

    def q_pos(height):
        return q0 + (lax.broadcasted_iota(jnp.int32, (height, cols), 1) & (tq - 1))

    def key_idx(height):
        return lax.broadcasted_iota(jnp.int32, (height, cols), 0)

    n_idx = key_idx(128)
    valid_c = (n_idx * CMP_STRIDE + (CMP_LEN - 1) <= q_pos(128)) & (n_idx < 127)
    s_c = jnp.where(valid_c, _dot_nt(kc_ref[...], qs), NEG_INF)
    m_c = jnp.max(s_c, axis=0, keepdims=True)
    m_c = jnp.where(m_c == NEG_INF, 0.0, m_c)
    e_c = jnp.exp(s_c - m_c)
    d_c = jnp.sum(e_c, axis=0, keepdims=True)
    p_c = (e_c / jnp.where(d_c > 0, d_c, 1.0)).astype(BF16)
    o_c = _dot_tn(vc_ref[...], p_c)
    imp_heads = jnp.dot(cover_ref[...], p_c, preferred_element_type=F32)
    imp = sum(imp_heads[0:32, r * tq:(r + 1) * tq] for r in range(A_REP))
    s_idx = lax.broadcasted_iota(jnp.int32, (32, tq), 0)
    qblk = (q0 + lax.broadcasted_iota(jnp.int32, (32, tq), 1)) >> 6
    forced = (s_idx == 0) | (s_idx == qblk) | (s_idx == qblk - 1)
    allowed = s_idx <= qblk
    score = jnp.where(allowed, imp + jnp.where(forced, FORCE_BONUS, 0.0), NEG_INF)
    rank = jnp.zeros((32, tq), F32)
    for j in range(32):
        other = score[j:j + 1, :]
        beats = (other > score) | ((other == score) & (s_idx > j))
        rank = rank + jnp.where(beats, 1.0, 0.0)
    sel = jnp.where((rank < SEL_TOPK) & allowed, 1.0, 0.0)
    sel_ref[...] = jnp.concatenate([sel] * A_REP, axis=1)

    def block_mask(k0, n_blocks):
        first = k0 // SEL_BLOCK
        return jnp.concatenate([jnp.broadcast_to(sel_ref[pl.ds(first + j, 1), :], (SEL_BLOCK, cols))
                                for j in range(n_blocks)], axis=0) > 0.5

    prev0 = pl.multiple_of(jnp.maximum(q0 - tq, 0), tq)
    btile = btile_ref[...]

    def near(k_ref, v_ref, extra_mask):
        k = jnp.concatenate([k_ref[pl.ds(prev0, tq), :], k_ref[pl.ds(q0, tq), :]], axis=0)
        v = jnp.concatenate([v_ref[pl.ds(prev0, tq), :], v_ref[pl.ds(q0, tq), :]], axis=0)
        s = _dot_nt(k, qs) + btile
        return _softmax_start(s if extra_mask is None else jnp.where(extra_mask, s, NEG_INF), v)

    sel_near = jnp.concatenate([block_mask(prev0, tq // SEL_BLOCK), block_mask(q0, tq // SEL_BLOCK)], axis=0)

    far_end = q0 - tq

    def sel_far(i, carry):
        k0 = pl.multiple_of(i * NSA_FAR_TILE, NSA_FAR_TILE)
        s = _dot_nt(ks_ref[pl.ds(k0, NSA_FAR_TILE), :], qs)
        mask = block_mask(k0, NSA_FAR_TILE // SEL_BLOCK) & (k0 + key_idx(NSA_FAR_TILE) < far_end)
        return _softmax_more(carry, jnp.where(mask, s, NEG_INF), vs_ref[pl.ds(k0, NSA_FAR_TILE), :])

    n_far = (jnp.maximum(far_end, 0) + NSA_FAR_TILE - 1) // NSA_FAR_TILE
    _, l_s, acc_s = lax.fori_loop(0, n_far, sel_far, near(ks_ref, vs_ref, sel_near))
    o_s = acc_s / l_s

    w_far = WIN_A - tq
    wf0 = pl.multiple_of(jnp.maximum(q0 - WIN_A, 0), tq)
    wpos = wf0 + key_idx(w_far)
    wmask = (q_pos(w_far) - wpos < WIN_A) & (wpos < far_end)
    s_w = jnp.where(wmask, _dot_nt(kw_ref[pl.ds(wf0, w_far), :], qs), NEG_INF)
    _, l_w, acc_w = _softmax_more(near(kw_ref, vw_ref, None), s_w, vw_ref[pl.ds(wf0, w_far), :])
    o_w = acc_w / l_w

    outs = []
    for r in range(A_REP):
        sl = slice(r * tq, (r + 1) * tq)
        row = 3 * (g * A_REP + r)
        out_t = (gates_t[row:row + 1] * o_c[:, sl] + gates_t[row + 1:row + 2] * o_s[:, sl]
                 + gates_t[row + 2:row + 3] * o_w[:, sl])
        outs.append(out_t.T)
    o_ref[...] = jnp.concatenate(outs, axis=1)


def _bucket_np(dist):
    n = np.maximum(dist, 0)
    exact = NUM_BUCKETS // 2
    nf = np.maximum(n, exact).astype(np.float64)
    large = exact + (np.log(nf / exact) / math.log(MAX_DISTANCE / exact) * (NUM_BUCKETS - exact)).astype(np.int64)
    return np.where(n < exact, n, np.minimum(large, NUM_BUCKETS - 1)).astype(np.int32)


def _bias_lookup(tab, dist):
    bucket = _bucket_np(np.asarray(dist))
    onehot = (jnp.asarray(bucket.reshape(-1, 1)) == jnp.arange(NUM_BUCKETS)[None, :]).astype(F32)
    flat = jnp.dot(onehot, tab.reshape(NUM_BUCKETS, -1), precision=lax.Precision.HIGHEST)
    return flat.reshape(bucket.shape + tab.shape[1:])


def _near_bias_tiles(tab, tq, window):
    i = np.arange(tq)[:, None]
    j = np.arange(2 * tq)[None, :]
    dist = tq + i - j
    ok = (dist >= 0) & (dist < window)
    bias = jnp.transpose(_bias_lookup(tab, dist), (2, 3, 0, 1))
    tiles = jnp.stack([jnp.where(jnp.asarray(ok & (j >= tq)), bias, NEG_INF), jnp.where(jnp.asarray(ok), bias, NEG_INF)])
    return tiles.reshape(2, tab.shape[1], tab.shape[2] * tq, 2 * tq)


def _cover_matrix(n_cmp_pad, n_sel):
    n = np.arange(n_cmp_pad)
    c_start = n * CMP_STRIDE
    c_end = c_start + CMP_LEN - 1
    s_start = np.arange(128) * SEL_BLOCK
    cover = (c_start[:, None] < s_start[None, :] + SEL_BLOCK) & (c_end[:, None] >= s_start[None, :])
    cover &= (np.arange(128)[None, :] < n_sel)
    return np.tile(cover.astype(np.float32), (A_REP, 1))


def _nsa_prompt_attend(q, gates, kc, vc, kvt, tab):
    b, t, _ = q.shape
    tq = NSA_Q_TILE
    assert t % NSA_FAR_TILE == 0 and t // SEL_BLOCK <= 32 and t >= WIN_A and kc.shape[2] == 128
    btile = jnp.swapaxes(_near_bias_tiles(tab - tab[NUM_BUCKETS - 1], tq, 2 * tq), -1, -2)
    cover = jnp.asarray(_cover_matrix(128, t // SEL_BLOCK)[:128].T, BF16)
    cmp_spec = pl.BlockSpec((None, A_KV, 128, HEAD_DIM), lambda i, j: (i, 0, 0, 0))
    return pl.pallas_call(
        functools.partial(_nsa_prompt_body, tq=tq),
        grid=(b, t // tq),
        in_specs=[
            pl.BlockSpec((None, tq, A_Q), lambda i, j: (i, j, 0)),
            pl.BlockSpec((None, tq, 128), lambda i, j: (i, j, 0)),
            cmp_spec, cmp_spec,
            pl.BlockSpec((None, 4 * A_KV, t, HEAD_DIM), lambda i, j: (i, 0, 0, 0)),
            pl.BlockSpec((None, A_KV, 2 * tq, A_REP * tq), lambda i, j: (jnp.minimum(j, 1), 0, 0, 0)),
            pl.BlockSpec((128, 128), lambda i, j: (0, 0)),
        ],
        out_specs=pl.BlockSpec((None, tq, A_Q), lambda i, j: (i, j, 0)),
        out_shape=jax.ShapeDtypeStruct((b, t, A_Q), F32),
        scratch_shapes=[pltpu.VMEM((A_KV, 32, A_REP * tq), F32)],
        compiler_params=pltpu.CompilerParams(
            dimension_semantics=("parallel", "arbitrary"), vmem_limit_bytes=V7X_VMEM_LIMIT_BYTES),
        name="nsa_prompt",
    )(q, gates, kc, vc, kvt, btile, cover)


S5_NS = B_GROUPS * B_STATE
S5_T_CHUNK = 64
S5_STRIP = 512
S5_HALVES = 2


def _s5_body(u_ref, h0r_ref, h0i_ref, ar_ref, ai_ref, ldt_ref, wb_ref, wc_ref, d_ref, wglu_ref, bglu_ref,
             o_ref, hr_ref, hi_ref, coef_ref, st_ref, xbuf_ref, ubuf_ref, obuf_ref, *, nb, steps, interleave):
    c = pl.program_id(0)

    @pl.when(c == 0)
    def _():
        dt = jnp.exp(ldt_ref[...])
        ar, ai = ar_ref[...], ai_ref[...]
        mag = jnp.exp(ar * dt)
        abr, abi = mag * jnp.cos(ai * dt), mag * jnp.sin(ai * dt)
        den = ar * ar + ai * ai
        wr = ((abr - 1.0) * ar + abi * ai) / den
        wi = (abi * ar - (abr - 1.0) * ai) / den
        for k, val in enumerate((abr, abi, wr, wi)):
            coef_ref[k] = jnp.broadcast_to(val, (nb, S5_NS))
        st_ref[0] = h0r_ref[...]
        st_ref[1] = h0i_ref[...]

    if interleave:
        for b in range(nb):
            for j in range(B_WIDTH // 128):
                ubuf_ref.at[j][pl.ds(b, steps, stride=nb), :] = u_ref[b, :, j * 128:(j + 1) * 128]
        u = jnp.concatenate([ubuf_ref[j] for j in range(B_WIDTH // 128)], axis=1)
    else:
        u = u_ref[...]
    hc, hs = B_WIDTH // S5_HALVES, S5_NS // S5_HALVES
    ub = u.astype(BF16)
    for h in range(S5_HALVES):
        bu = jnp.dot(ub[:, h * hc:(h + 1) * hc], wb_ref[h], preferred_element_type=F32)
        xbuf_ref[:, h * hs:(h + 1) * hs] = bu[:, :hs]
        xbuf_ref[:, S5_NS + h * hs:S5_NS + (h + 1) * hs] = bu[:, hs:]

    for lo in range(0, S5_NS, S5_STRIP):
        re = slice(lo, lo + S5_STRIP)
        im = slice(S5_NS + lo, S5_NS + lo + S5_STRIP)
        abr, abi, wr, wi = (coef_ref[k, :, re] for k in range(4))

        def step(t, carry):
            sr, si = carry
            r0 = pl.multiple_of(t * nb, nb)
            bur = xbuf_ref[pl.ds(r0, nb), re]
            bui = xbuf_ref[pl.ds(r0, nb), im]
            nsr = abr * sr - abi * si + (wr * bur - wi * bui)
            nsi = abr * si + abi * sr + (wr * bui + wi * bur)
            xbuf_ref[pl.ds(r0, nb), re] = nsr
            xbuf_ref[pl.ds(r0, nb), im] = nsi
            return nsr, nsi

        sr, si = lax.fori_loop(0, steps, step, (st_ref[0, :, re], st_ref[1, :, re]),
                               unroll=min(steps, 8))
        st_ref[0, :, re] = sr
        st_ref[1, :, re] = si

    y = jnp.concatenate(
        [jnp.dot(jnp.concatenate([xbuf_ref[:, h * hs:(h + 1) * hs], xbuf_ref[:, S5_NS + h * hs:S5_NS + (h + 1) * hs]],
                                 axis=1).astype(BF16), wc_ref[h], preferred_element_type=F32)
         for h in range(S5_HALVES)], axis=1) + d_ref[...] * u
    z = _gelu_tanh(y)
    gate = jnp.dot(z.astype(BF16), wglu_ref[...], preferred_element_type=F32) + bglu_ref[...]
    out = z * (1.0 / (1.0 + jnp.exp(-gate)))
    if interleave:
        for j in range(B_WIDTH // 128):
            obuf_ref[j] = out[:, j * 128:(j + 1) * 128]
        for b in range(nb):
            for j in range(B_WIDTH // 128):
                o_ref[b, :, j * 128:(j + 1) * 128] = obuf_ref.at[j][pl.ds(b, steps, stride=nb), :]
    else:
        o_ref[...] = out
    hr_ref[...] = st_ref[0]
    hi_ref[...] = st_ref[1]


def _s5_weights(a_re, a_im, log_dt, b_re, b_im, c_re, c_im, d, w_glu, b_glu):
    eye = jnp.eye(B_GROUPS, dtype=F32)
    blk_in = lambda w: jnp.einsum('hg,gpc->hcgp', eye, w).reshape(B_WIDTH, S5_NS)
    blk_out = lambda w: jnp.einsum('gh,gcp->gphc', eye, w).reshape(S5_NS, B_WIDTH)
    hc, hs = B_WIDTH // S5_HALVES, S5_NS // S5_HALVES
    diag = lambda w, h: w[h * hc:(h + 1) * hc, h * hs:(h + 1) * hs]
    wb = jnp.stack([jnp.concatenate([diag(blk_in(b_re), h), diag(blk_in(b_im), h)], axis=1)
                    for h in range(S5_HALVES)]).astype(BF16)
    diag_t = lambda w, h: w[h * hs:(h + 1) * hs, h * hc:(h + 1) * hc]
    wc = jnp.stack([jnp.concatenate([diag_t(blk_out(c_re), h), -diag_t(blk_out(c_im), h)], axis=0)
                    for h in range(S5_HALVES)]).astype(BF16)
    flat = lambda x: x.reshape(1, S5_NS)
    return (flat(a_re), flat(a_im), flat(jnp.repeat(log_dt, B_STATE)), wb, wc, d.reshape(1, B_WIDTH),
            w_glu.astype(BF16), b_glu.reshape(1, B_WIDTH))


def _s5_mix(u, h_re, h_im, weights):
    nb, t, _ = u.shape
    interleave = t > 1
    steps = min(t, S5_T_CHUNK)
    rows = nb * steps
    body = functools.partial(_s5_body, nb=nb, steps=steps, interleave=interleave)
    if interleave:
        u_in = u
        u_spec = pl.BlockSpec((nb, steps, B_WIDTH), lambda c: (0, c, 0))
        o_shape = jax.ShapeDtypeStruct((nb, t, B_WIDTH), F32)
        scratch_rows = rows
    else:
        u_in = u.reshape(nb, B_WIDTH)
        u_spec = pl.BlockSpec((nb, B_WIDTH), lambda c: (0, 0))
        o_shape = jax.ShapeDtypeStruct((nb, B_WIDTH), F32)
        scratch_rows = 8
    o, hr, hi = pl.pallas_call(
        body,
        grid=(t // steps,),
        in_specs=[
            u_spec, _const_spec((nb, S5_NS)), _const_spec((nb, S5_NS)),
            _const_spec((1, S5_NS)), _const_spec((1, S5_NS)), _const_spec((1, S5_NS)),
            _const_spec((S5_HALVES, B_WIDTH // S5_HALVES, 2 * S5_NS // S5_HALVES)),
            _const_spec((S5_HALVES, 2 * S5_NS // S5_HALVES, B_WIDTH // S5_HALVES)),
            _const_spec((1, B_WIDTH)), _const_spec((B_WIDTH, B_WIDTH)), _const_spec((1, B_WIDTH)),
        ],
        out_specs=[u_spec, pl.BlockSpec((nb, S5_NS), lambda c: (0, 0)), pl.BlockSpec((nb, S5_NS), lambda c: (0, 0))],
        out_shape=[o_shape, jax.ShapeDtypeStruct((nb, S5_NS), F32), jax.ShapeDtypeStruct((nb, S5_NS), F32)],
        scratch_shapes=[
            pltpu.VMEM((4, nb, S5_NS), F32),
            pltpu.VMEM((2, nb, S5_NS), F32),
            pltpu.VMEM((rows, 2 * S5_NS), F32),
            pltpu.VMEM((B_WIDTH // 128, scratch_rows, 128), F32),
            pltpu.VMEM((B_WIDTH // 128, scratch_rows, 128), F32),
        ],
        compiler_params=pltpu.CompilerParams(
            dimension_semantics=("arbitrary",),
            vmem_limit_bytes=V7X_VMEM_LIMIT_BYTES),
        name="s5_mix",
    )(u_in, h_re, h_im, *weights)
    return o.reshape(nb, t, B_WIDTH), hr, hi


CHUNK_W = CMP_STRIDE * 2 * A_KVW
CHUNKS_PER_PAGE = PAGE_SIZE // CMP_STRIDE
SEL_PAD = 256


def _bf16_round(x):
    return x.astype(BF16).astype(F32)


def _compress_tail(c, pos, w2):
    c1 = c[:, 128:]
    nxt = jnp.concatenate([c1[1:], c1[:1]], axis=0)
    hid = (pos + c[:, :128]) + nxt
    return jnp.dot(_gelu_tanh(hid).astype(BF16), w2, preferred_element_type=F32)


def _rms_heads128(x, gain):
    left = lax.broadcasted_iota(jnp.int32, x.shape, 1) < HEAD_DIM
    sq = x * x
    s0 = jnp.sum(jnp.where(left, sq, 0.0), axis=-1, keepdims=True)
    s1 = jnp.sum(jnp.where(left, 0.0, sq), axis=-1, keepdims=True)
    ms = jnp.where(left, s0, s1) * (1.0 / HEAD_DIM)
    return x * lax.rsqrt(ms + EPS) * gain


def _pad_rows8(x):
    return jnp.concatenate([x, jnp.zeros((8 - x.shape[0], x.shape[1]), x.dtype)], axis=0)


def _compress_rows(rows_scr, kv, n_chunks, w_ref, pos, w2_ref):
    acc = jnp.zeros((n_chunks, 2 * A_KVW), F32)
    for sp in range(CMP_STRIDE // 2):
        x = jnp.concatenate([rows_scr.at[kv][pl.ds(2 * sp + e, n_chunks, stride=CMP_STRIDE), :] for e in range(2)],
                            axis=1).astype(BF16)
        acc = acc + jnp.dot(x, w_ref[sp], preferred_element_type=F32)
    return _compress_tail(acc, pos, w2_ref[...])


def _compress_weights(cmp_pos, cmp_w1, cmp_w2, k_gain):
    eye = jnp.eye(A_KV, dtype=F32)
    w1 = cmp_w1.reshape(2, 2, CMP_STRIDE, HEAD_DIM, HEAD_DIM)
    w_big = jnp.einsum('ajsdh,gk->asgdjkh', w1, eye).reshape(2, CMP_STRIDE // 2, 2 * A_KVW, 2 * A_KVW).astype(BF16)
    w_pos = jnp.concatenate([cmp_w1, cmp_w1], axis=-1).astype(BF16)
    pos = jnp.broadcast_to(cmp_pos.reshape(2, 1, CMP_LEN * HEAD_DIM), (2, 8, CMP_LEN * HEAD_DIM)).astype(BF16)
    w2_big = jnp.einsum('ahd,gk->aghkd', cmp_w2, eye).reshape(2, A_KVW, A_KVW).astype(BF16)
    gain2 = jnp.tile(k_gain, A_KV).reshape(1, A_KVW)
    return (pos[0], pos[1], w_pos[0], w_pos[1], w_big[0], w_big[1], w2_big[0], w2_big[1], gain2)


def _compress_specs(full):
    mlp_w = (CMP_STRIDE // 2, 2 * A_KVW, 2 * A_KVW)
    return [full((8, CMP_LEN * HEAD_DIM)), full((8, CMP_LEN * HEAD_DIM)),
            full((CMP_LEN * HEAD_DIM, A_KVW)), full((CMP_LEN * HEAD_DIM, A_KVW)),
            full(mlp_w), full(mlp_w), full((A_KVW, A_KVW)), full((A_KVW, A_KVW)), full((1, A_KVW))]


def _pos_terms(pos_scr, posk_ref, posv_ref, wpos_k_ref, wpos_v_ref):
    pos_scr[0] = jnp.dot(posk_ref[...], wpos_k_ref[...], preferred_element_type=F32)
    pos_scr[1] = jnp.dot(posv_ref[...], wpos_v_ref[...], preferred_element_type=F32)


def _cmp_prompt_body(cmp_ref, posk_ref, posv_ref, wpos_k_ref, wpos_v_ref, wk_ref, wv_ref, w2k_ref, w2v_ref,
                     kgain_ref, kc_ref, vc_ref, pos_scr, rows_scr, *, n_chunks):
    @pl.when(pl.program_id(0) == 0)
    def _():
        _pos_terms(pos_scr, posk_ref, posv_ref, wpos_k_ref, wpos_v_ref)

    rows_scr[0] = cmp_ref[:, :A_KVW]
    rows_scr[1] = cmp_ref[:, A_KVW:]
    kc = _rms_heads128(_compress_rows(rows_scr, 0, n_chunks, wk_ref, pos_scr[0, 0:1], w2k_ref),
                       kgain_ref[...]).astype(BF16)
    vc = _compress_rows(rows_scr, 1, n_chunks, wv_ref, pos_scr[1, 0:1], w2v_ref).astype(BF16)
    for g in range(A_KV):
        kc_ref[g] = kc[:, g * HEAD_DIM:(g + 1) * HEAD_DIM]
        vc_ref[g] = vc[:, g * HEAD_DIM:(g + 1) * HEAD_DIM]


def _cmp_prompt(cmp_rows, cmp_weights):
    b, t, _ = cmp_rows.shape
    n_chunks = t // CMP_STRIDE
    full = lambda shape: pl.BlockSpec(shape, lambda i: (0,) * len(shape))
    out_spec = pl.BlockSpec((None, A_KV, n_chunks, HEAD_DIM), lambda i: (i, 0, 0, 0))
    out_shape = jax.ShapeDtypeStruct((b, A_KV, n_chunks, HEAD_DIM), BF16)
    return pl.pallas_call(
        functools.partial(_cmp_prompt_body, n_chunks=n_chunks),
        grid=(b,),
        in_specs=[pl.BlockSpec((None, t, 2 * A_KVW), lambda i: (i, 0, 0))] + _compress_specs(full),
        out_specs=[out_spec, out_spec],
        out_shape=[out_shape, out_shape],
        scratch_shapes=[pltpu.VMEM((2, 8, A_KVW), F32), pltpu.VMEM((2, t, A_KVW), F32)],
        compiler_params=pltpu.CompilerParams(
            dimension_semantics=("arbitrary",), vmem_limit_bytes=V7X_VMEM_LIMIT_BYTES),
        name="nsa_cmp_prompt",
    )(cmp_rows, *cmp_weights)


def _nsa_sample_cmp_body(pt_ref, *refs, n_pages, n_sel):
    page_refs = refs[:n_pages]
    (q_ref, posk_ref, posv_ref, wpos_k_ref, wpos_v_ref, wk_ref, wv_ref, w2k_ref, w2v_ref, kgain_ref, cover_ref,
     oc_ref, idx_ref, pos_scr, rows_scr) = refs[n_pages:]

    @pl.when(pl.program_id(0) == 0)
    def _():
        _pos_terms(pos_scr, posk_ref, posv_ref, wpos_k_ref, wpos_v_ref)

    n_chunks = n_pages * CHUNKS_PER_PAGE
    n_cmp = n_chunks - 1
    for kv in range(2):
        for p in range(n_pages):
            rows_scr[kv, p * PAGE_SIZE:(p + 1) * PAGE_SIZE, :] = page_refs[p][kv].reshape(A_KVW, PAGE_SIZE).T

    kc = _rms_heads128(_compress_rows(rows_scr, 0, n_chunks, wk_ref, pos_scr[0, 0:1], w2k_ref),
                       kgain_ref[...]).astype(BF16)
    vc = _compress_rows(rows_scr, 1, n_chunks, wv_ref, pos_scr[1, 0:1], w2v_ref).astype(BF16)

    col = lax.broadcasted_iota(jnp.int32, (8, n_chunks), 1)
    s_idx = lax.broadcasted_iota(jnp.int32, (1, SEL_PAD), 1)
    s_idx_f = s_idx.astype(F32)
    forced = (s_idx == 0) | (s_idx == n_sel - 1) | (s_idx == n_sel - 2)
    lane = lax.broadcasted_iota(jnp.int32, (1, 128), 1)
    for g in range(A_KV):
        hs = slice(g * HEAD_DIM, (g + 1) * HEAD_DIM)
        q8 = (_pad_rows8(q_ref[g]) * (HEAD_DIM ** -0.5)).astype(BF16)
        s = jnp.where(col < n_cmp, _dot_nt(q8, kc[:, hs]), NEG_INF)
        e = jnp.exp(s - jnp.max(s, axis=-1, keepdims=True))
        p = (e / jnp.sum(e, axis=-1, keepdims=True)).astype(BF16)
        oc_ref[g] = jnp.dot(p, vc[:, hs], preferred_element_type=F32)[0:A_REP]
        imp = jnp.sum(jnp.dot(p, cover_ref[...], preferred_element_type=F32)[0:A_REP], axis=0, keepdims=True)
        score = jnp.where(s_idx < n_sel, imp + jnp.where(forced, FORCE_BONUS, 0.0), NEG_INF)
        rank = jnp.zeros((1, SEL_PAD), F32)
        for j in range(n_sel):
            cj = score[:, j:j + 1]
            beats = (cj > score) | ((cj == score) & (s_idx > j))
            rank = rank + jnp.where(beats, 1.0, 0.0)
        picks = jnp.zeros((1, 128), F32)
        for r in range(SEL_TOPK):
            block = jnp.sum(jnp.where(rank == float(r), s_idx_f, 0.0), axis=-1, keepdims=True)
            picks = picks + jnp.where(lane == r, block, 0.0)
        idx_ref[g:g + 1, :] = picks.astype(jnp.int32)


def _nsa_sample_att_body(idx_ref, pt_ref, *refs, n_past_blk):
    n_slots = A_KV * SEL_TOPK
    slot_refs = refs[:n_slots]
    (q_all, gate_all, oc_all, newsel_all, wcache_all, newwin_all, newcol_all, selb_all, winb_all, newb_all,
     o_all, wout_all) = refs[n_slots:]
    b = pl.program_id(0)
    for g in range(A_KV):
        _nsa_sample_att_group(b, g, idx_ref, slot_refs[g * SEL_TOPK:(g + 1) * SEL_TOPK], q_all.at[g], gate_all.at[g],
                              oc_all.at[g], newsel_all.at[g], wcache_all.at[:, g], newwin_all.at[g], selb_all.at[g],
                              winb_all.at[g], newb_all.at[g], o_all.at[g], n_past_blk)
        for kv in range(2):
            wout_all[kv, g] = _shift_in(wcache_all[kv, g], newcol_all[g][:, kv:kv + 1])


def _shift_in(rows_last, new_col):
    w_len = rows_last.shape[-1]
    lane = lax.broadcasted_iota(jnp.int32, rows_last.shape, 1)
    return jnp.where(lane == w_len - 1, new_col, pltpu.roll(rows_last, w_len - 1, axis=1))


def _nsa_sample_att_group(b, g, idx_ref, page_refs, q_ref, gate_ref, oc_ref, newsel_ref, wcache_ref, newwin_ref,
                          selb_ref, winb_ref, newb_ref, o_ref, n_past_blk):
    blocks_per_page = PAGE_SIZE // SEL_BLOCK
    q8 = (_pad_rows8(q_ref[...]) * (HEAD_DIM ** -0.5)).astype(BF16)
    new_bias = newb_ref[:, 0:1]

    def attend(s, v_t, new_kv):
        k_new = _bf16_round(new_kv[0:1])
        v_new = _bf16_round(new_kv[1:2])
        s_new = jnp.sum(q8.astype(F32) * k_new, axis=-1, keepdims=True) + new_bias
        m = jnp.maximum(jnp.max(s, axis=-1, keepdims=True), s_new)
        e = jnp.exp(s - m)
        e_new = jnp.exp(s_new - m)
        den = jnp.sum(e, axis=-1, keepdims=True) + e_new
        return _dot_nt((e / den).astype(BF16), v_t) + _bf16_round(e_new / den) * v_new

    lane = lax.broadcasted_iota(jnp.int32, (8, PAGE_SIZE), 1)
    near = selb_ref[...]
    bias = []
    for k in range(SEL_TOPK):
        ik = idx_ref[(b * A_KV + g) * SEL_TOPK + k]
        blk = jnp.minimum(ik, n_past_blk - 1)
        near_k = jnp.where(blk // blocks_per_page == (n_past_blk - 1) // blocks_per_page, near, 0.0)
        keep = (lane // SEL_BLOCK == blk % blocks_per_page) & (ik < n_past_blk)
        bias.append(jnp.where(keep, near_k, NEG_INF))
    k_sel = jnp.concatenate([page_refs[k][0].astype(BF16) for k in range(SEL_TOPK)], axis=1)
    v_sel = jnp.concatenate([page_refs[k][1].astype(BF16) for k in range(SEL_TOPK)], axis=1)
    s_sel = jnp.dot(q8, k_sel, preferred_element_type=F32) + jnp.concatenate(bias, axis=1)
    o_s = attend(s_sel, v_sel, newsel_ref[...])

    w_len = wcache_ref.shape[-1]
    wcol = lax.broadcasted_iota(jnp.int32, (8, w_len), 1)
    s_w = jnp.dot(q8, wcache_ref[0].astype(BF16), preferred_element_type=F32) + winb_ref[...]
    o_w = attend(jnp.where(wcol >= 1, s_w, NEG_INF), wcache_ref[1].astype(BF16), newwin_ref[...])
    gates = gate_ref[...]
    o_ref[...] = gates[:, 0:1] * oc_ref[...] + gates[:, 1:2] * o_s[0:A_REP] + gates[:, 2:3] * o_w[0:A_REP]


def _rows_last(cache):
    nd = cache.ndim
    return jnp.transpose(cache, tuple(range(nd - 4)) + (nd - 3, nd - 2, nd - 1, nd - 4))


def _rows_first(cache):
    nd = cache.ndim
    return jnp.transpose(cache, tuple(range(nd - 4)) + (nd - 1, nd - 4, nd - 3, nd - 2))


def _nsa_sample(q, gates, new_sel, new_win, pool_cmp, pool_sel, win_cache, page_table, cmp_weights, tab):
    n, n_pages = page_table.shape
    past = n_pages * PAGE_SIZE
    n_past_blk = past // SEL_BLOCK
    n_sel = n_past_blk + 1
    n_chunks = n_pages * CHUNKS_PER_PAGE
    w_len = win_cache.shape[1]
    blocks_per_page = PAGE_SIZE // SEL_BLOCK
    assert n_sel <= SEL_PAD and w_len == WIN_A and past >= WIN_A and blocks_per_page == 2
    pt_flat = page_table.reshape(-1)

    c_idx = np.arange(n_chunks)
    s_start = np.arange(SEL_PAD) * SEL_BLOCK
    cover = ((c_idx[:, None] * CMP_STRIDE < s_start[None, :] + SEL_BLOCK)
             & (c_idx[:, None] * CMP_STRIDE + CMP_LEN - 1 >= s_start[None, :])
             & (c_idx[:, None] < n_chunks - 1) & (np.arange(SEL_PAD)[None, :] < n_sel))
    cover = jnp.asarray(cover.astype(np.float32), BF16)

    head_spec = lambda last: pl.BlockSpec((None, A_KV, A_REP, last), lambda i, *_: (i, 0, 0, 0))
    page_specs = [pl.BlockSpec((None, 2, A_KV, HEAD_DIM, PAGE_SIZE),
                               functools.partial(lambda i, pt, p: (pt[i * n_pages + p], 0, 0, 0, 0), p=p))
                  for p in range(n_pages)]
    full = lambda shape: pl.BlockSpec(shape, lambda i, *_: (0,) * len(shape))
    pool_c = _rows_last(pool_cmp)
    o_c, idx = pl.pallas_call(
        functools.partial(_nsa_sample_cmp_body, n_pages=n_pages, n_sel=n_sel),
        grid_spec=pltpu.PrefetchScalarGridSpec(
            num_scalar_prefetch=1,
            grid=(n,),
            in_specs=page_specs + [head_spec(HEAD_DIM)] + _compress_specs(full) + [full((n_chunks, SEL_PAD))],
            out_specs=[head_spec(HEAD_DIM), pl.BlockSpec((None, A_KV, 128), lambda i, *_: (i, 0, 0))],
            scratch_shapes=[pltpu.VMEM((2, 8, A_KVW), F32), pltpu.VMEM((2, past, A_KVW), F32)],
        ),
        out_shape=[jax.ShapeDtypeStruct((n, A_KV, A_REP, HEAD_DIM), F32),
                   jax.ShapeDtypeStruct((n, A_KV, 128), jnp.int32)],
        compiler_params=pltpu.CompilerParams(
            dimension_semantics=("arbitrary",), vmem_limit_bytes=V7X_VMEM_LIMIT_BYTES),
        name="nsa_sample_cmp",
    )(pt_flat, *([pool_c] * n_pages), q, *cmp_weights, cover)

    rel = lambda dist: jnp.pad(jnp.transpose(_bias_lookup(tab - tab[NUM_BUCKETS - 1], dist), (1, 2, 0)),
                               ((0, 0), (0, 8 - A_REP), (0, 0)))
    sel_bias = rel(past - (past - PAGE_SIZE + np.arange(PAGE_SIZE)))
    win_bias = rel(w_len - np.arange(w_len))
    new_bias = rel(np.zeros(128, np.int64))

    idx_flat = idx[:, :, :SEL_TOPK].reshape(-1)

    def page_map(i, idx_s, pt, g, slot):
        blk = jnp.minimum(idx_s[(i * A_KV + g) * SEL_TOPK + slot], n_past_blk - 1)
        return (pt[i * n_pages + blk // blocks_per_page], 0, g, 0, 0)

    slot_specs = [pl.BlockSpec((None, 2, None, HEAD_DIM, PAGE_SIZE), functools.partial(page_map, g=g, slot=s))
                  for g in range(A_KV) for s in range(SEL_TOPK)]
    per_head = lambda rows, last: pl.BlockSpec((None, A_KV, rows, last), lambda i, *_: (i, 0, 0, 0))
    per_group = lambda last: pl.BlockSpec((A_KV, 8, last), lambda i, *_: (0, 0, 0))
    win_spec = pl.BlockSpec((None, 2, A_KV, HEAD_DIM, w_len), lambda i, *_: (i, 0, 0, 0, 0))
    o, win_next = pl.pallas_call(
        functools.partial(_nsa_sample_att_body, n_past_blk=n_past_blk),
        grid_spec=pltpu.PrefetchScalarGridSpec(
            num_scalar_prefetch=2,
            grid=(n,),
            in_specs=slot_specs + [
                per_head(A_REP, HEAD_DIM), per_head(A_REP, 3), per_head(A_REP, HEAD_DIM), per_head(2, HEAD_DIM),
                win_spec, per_head(2, HEAD_DIM), per_head(HEAD_DIM, 2),
                per_group(PAGE_SIZE), per_group(w_len), per_group(128),
            ],
            out_specs=[per_head(A_REP, HEAD_DIM), win_spec],
        ),
        out_shape=[jax.ShapeDtypeStruct((n, A_KV, A_REP, HEAD_DIM), F32),
                   jax.ShapeDtypeStruct((n, 2, A_KV, HEAD_DIM, w_len), F32)],
        compiler_params=pltpu.CompilerParams(
            dimension_semantics=("arbitrary",), vmem_limit_bytes=V7X_VMEM_LIMIT_BYTES),
        name="nsa_sample_att",
    )(idx_flat, pt_flat, *([_rows_last(pool_sel)] * (A_KV * SEL_TOPK)), q, gates, o_c,
      jnp.transpose(new_sel, (0, 2, 1, 3)), _rows_last(win_cache), jnp.transpose(new_win, (0, 2, 1, 3)),
      jnp.transpose(new_win, (0, 2, 3, 1)), sel_bias, win_bias, new_bias)
    return o, _rows_first(win_next)


SWA_Q_TILE = WIN_C


def _swa_prompt_body(q_ref, k_ref, v_ref, btile_ref, sink_ref, o_ref, *, tq):
    qt = pl.program_id(2)
    q0 = pl.multiple_of(qt * tq, tq)
    q = q_ref[...]
    qs = jnp.concatenate([q[:, r * HEAD_DIM:(r + 1) * HEAD_DIM] for r in range(C_REP)], axis=0)
    qs = (qs * (HEAD_DIM ** -0.5)).astype(BF16)
    prev0 = pl.multiple_of(jnp.maximum(q0 - tq, 0), tq)
    k = jnp.concatenate([k_ref[pl.ds(prev0, tq), :], k_ref[pl.ds(q0, tq), :]], axis=0)
    v = jnp.concatenate([v_ref[pl.ds(prev0, tq), :], v_ref[pl.ds(q0, tq), :]], axis=0)
    s = _dot_nt(k, qs) + btile_ref[...]
    sinks = sink_ref[...]
    sink = jnp.concatenate([sinks[r:r + 1, :] for r in range(C_REP)], axis=1)
    m = jnp.maximum(jnp.max(s, axis=0, keepdims=True), sink)
    e = jnp.exp(s - m)
    p = e / (jnp.sum(e, axis=0, keepdims=True) + jnp.exp(sink - m))
    o = lax.dot_general(v, p.astype(BF16), (((0,), (0,)), ((), ())), preferred_element_type=F32)
    o_ref[...] = jnp.concatenate([o[:, r * tq:(r + 1) * tq].T for r in range(C_REP)], axis=1)


def _swa_prompt(q, kvt, sinks, tab):
    b, t, _ = q.shape
    tq = SWA_Q_TILE
    btile = jnp.swapaxes(_near_bias_tiles(tab, tq, WIN_C), -1, -2)
    kv_spec = lambda slot: pl.BlockSpec((None, None, t, HEAD_DIM), lambda i, g, j: (i, slot + g, 0, 0))
    qo_spec = pl.BlockSpec((None, tq, C_REP * HEAD_DIM), lambda i, g, j: (i, j, g))
    sink_lanes = jnp.broadcast_to(sinks.reshape(C_KV, C_REP, 1), (C_KV, C_REP, 128))
    return pl.pallas_call(
        functools.partial(_swa_prompt_body, tq=tq),
        grid=(b, C_KV, t // tq),
        in_specs=[qo_spec, kv_spec(0), kv_spec(C_KV),
                  pl.BlockSpec((None, None, 2 * tq, C_REP * tq), lambda i, g, j: (jnp.minimum(j, 1), g, 0, 0)),
                  pl.BlockSpec((None, C_REP, 128), lambda i, g, j: (g, 0, 0))],
        out_specs=qo_spec,
        out_shape=jax.ShapeDtypeStruct((b, t, C_HEADS * HEAD_DIM), F32),
        compiler_params=pltpu.CompilerParams(
            dimension_semantics=("parallel", "parallel", "arbitrary"), vmem_limit_bytes=V7X_VMEM_LIMIT_BYTES),
        name="swa_prompt",
    )(q, kvt, kvt, btile, sink_lanes)


def _swa_sample_body(q_ref, cache_ref, new_ref, newcol_ref, bias_ref, newb_ref, sink_ref, o_ref, next_ref):
    w_len = cache_ref.shape[-1]
    wcol = lax.broadcasted_iota(jnp.int32, (C_REP, w_len), 1)
    for g in range(C_KV):
        q8 = (q_ref[g] * (HEAD_DIM ** -0.5)).astype(BF16)
        s = jnp.dot(q8, cache_ref[0, g].astype(BF16), preferred_element_type=F32) + bias_ref[g]
        s = jnp.where(wcol >= 1, s, NEG_INF)
        k_new = _bf16_round(new_ref[g, 0:1])
        v_new = _bf16_round(new_ref[g, 1:2])
        s_new = jnp.sum(q8.astype(F32) * k_new, axis=-1, keepdims=True) + newb_ref[g][:, 0:1]
        sink = sink_ref[g][:, 0:1]
        m = jnp.maximum(jnp.maximum(jnp.max(s, axis=-1, keepdims=True), s_new), sink)
        e = jnp.exp(s - m)
        e_new = jnp.exp(s_new - m)
        den = jnp.sum(e, axis=-1, keepdims=True) + e_new + jnp.exp(sink - m)
        o_ref[g] = _dot_nt((e / den).astype(BF16), cache_ref[1, g].astype(BF16)) + _bf16_round(e_new / den) * v_new
        for kv in range(2):
            next_ref[kv, g] = _shift_in(cache_ref[kv, g], newcol_ref[g][:, kv:kv + 1])


def _swa_sample(q, cache, new_kv, sinks, tab):
    n, w_len = cache.shape[:2]
    bias = jnp.transpose(_bias_lookup(tab, w_len - np.arange(w_len)), (1, 2, 0))
    lanes = lambda x: jnp.broadcast_to(x[:, :, None], (C_KV, C_REP, 128))
    full = lambda shape: pl.BlockSpec(shape, lambda i: (0,) * len(shape))
    cache_spec = pl.BlockSpec((None, 2, C_KV, HEAD_DIM, w_len), lambda i: (i, 0, 0, 0, 0))
    o_spec = pl.BlockSpec((None, C_KV, C_REP, HEAD_DIM), lambda i: (i, 0, 0, 0))
    o, cache_next = pl.pallas_call(
        _swa_sample_body,
        grid=(n,),
        in_specs=[
            o_spec, cache_spec,
            pl.BlockSpec((None, C_KV, 2, HEAD_DIM), lambda i: (i, 0, 0, 0)),
            pl.BlockSpec((None, C_KV, HEAD_DIM, 2), lambda i: (i, 0, 0, 0)),
            full((C_KV, C_REP, w_len)), full((C_KV, C_REP, 128)), full((C_KV, C_REP, 128)),
        ],
        out_specs=[o_spec, cache_spec],
        out_shape=[jax.ShapeDtypeStruct((n, C_KV, C_REP, HEAD_DIM), F32),
                   jax.ShapeDtypeStruct((n, 2, C_KV, HEAD_DIM, w_len), F32)],
        compiler_params=pltpu.CompilerParams(
            dimension_semantics=("arbitrary",), vmem_limit_bytes=V7X_VMEM_LIMIT_BYTES),
        name="swa_sample",
    )(q, _rows_last(cache), new_kv, jnp.transpose(new_kv, (0, 1, 3, 2)), bias, lanes(tab[0]),
      lanes(sinks.reshape(C_KV, C_REP)))
    return o, _rows_first(cache_next)


PROJ_ROW_TILE = 512
A_IN_PAD = 1920
A_U_COL = A_Q + 6 * A_KVW
A_GATE_COL = A_U_COL + B_WIDTH


def _heads_first(kvt_ref, slot, k, v, n_kv):
    for g in range(n_kv):
        kvt_ref[slot + g] = k[:, g * HEAD_DIM:(g + 1) * HEAD_DIM].astype(BF16)
        kvt_ref[slot + n_kv + g] = v[:, g * HEAD_DIM:(g + 1) * HEAD_DIM].astype(BF16)


def _inproj_a_body(x_ref, gain_ref, w_ref, qg_ref, kg_ref,
                   q_ref, cmp_ref, sel_ref, win_ref, gate_ref, u_ref, kvt_ref):
    xn = _rms_rows(x_ref[...], gain_ref[...]).astype(BF16)
    z = jnp.dot(xn, w_ref[...], preferred_element_type=F32)
    for j in range(A_Q // 128):
        q_ref[:, j * 128:(j + 1) * 128] = _rms_heads128(z[:, j * 128:(j + 1) * 128], qg_ref[...])
    for out_ref, off, slot in ((cmp_ref, A_Q, None), (sel_ref, A_Q + 2 * A_KVW, 0), (win_ref, A_Q + 4 * A_KVW, 4)):
        k = _rms_heads128(z[:, off:off + A_KVW], kg_ref[...])
        v = z[:, off + A_KVW:off + 2 * A_KVW]
        out_ref[:, :A_KVW] = k
        out_ref[:, A_KVW:] = v
        if slot is not None:
            _heads_first(kvt_ref, slot, k, v, A_KV)
    u_ref[...] = z[:, A_U_COL:A_GATE_COL]
    gate_ref[...] = 1.0 / (1.0 + jnp.exp(-z[:, A_GATE_COL:A_IN_PAD]))


def _inproj_a(x, gain, w_in, q_gain, k_gain):
    b, t, d = x.shape
    tm = min(t, PROJ_ROW_TILE)
    w = jnp.concatenate([w_in[:, :A_U_COL], w_in[:, A_U_COL + A_GATE:], w_in[:, A_U_COL:A_U_COL + A_GATE],
                         jnp.zeros((d, A_IN_PAD - A_GATE_COL - A_GATE), F32)], axis=1).astype(BF16)
    tile2 = lambda g: jnp.tile(g, 2).reshape(1, 128)
    rows = lambda width: pl.BlockSpec((None, tm, width), lambda i, j: (i, j, 0))
    shape = lambda width: jax.ShapeDtypeStruct((b, t, width), F32)
    return pl.pallas_call(
        _inproj_a_body,
        grid=(b, t // tm),
        in_specs=[rows(d), _const_spec((1, d)), _const_spec((d, A_IN_PAD)), _const_spec((1, 128)), _const_spec((1, 128))],
        out_specs=[rows(A_Q), rows(2 * A_KVW), rows(2 * A_KVW), rows(2 * A_KVW), rows(128), rows(B_WIDTH),
                   pl.BlockSpec((None, 8, tm, HEAD_DIM), lambda i, j: (i, 0, j, 0))],
        out_shape=[shape(A_Q), shape(2 * A_KVW), shape(2 * A_KVW), shape(2 * A_KVW), shape(128), shape(B_WIDTH),
                   jax.ShapeDtypeStruct((b, 8, t, HEAD_DIM), BF16)],
        compiler_params=pltpu.CompilerParams(
            dimension_semantics=("parallel", "parallel"), vmem_limit_bytes=V7X_VMEM_LIMIT_BYTES),
        name="inproj_nsa_s5",
    )(x, gain.reshape(1, d), w, tile2(q_gain), tile2(k_gain))


def _inproj_c_body(x_ref, gain_ref, w_ref, qg_ref, kg_ref, q_ref, kv_ref, kvt_ref):
    xn = _rms_rows(x_ref[...], gain_ref[...]).astype(BF16)
    z = jnp.dot(xn, w_ref[...], preferred_element_type=F32)
    n_q = C_HEADS * HEAD_DIM
    for j in range(n_q // 128):
        q_ref[:, j * 128:(j + 1) * 128] = _rms_heads128(z[:, j * 128:(j + 1) * 128], qg_ref[...])
    k = _rms_heads128(z[:, n_q:n_q + C_KV * HEAD_DIM], kg_ref[...])
    v = z[:, n_q + C_KV * HEAD_DIM:]
    kv_ref[:, :C_KV * HEAD_DIM] = k
    kv_ref[:, C_KV * HEAD_DIM:] = v
    _heads_first(kvt_ref, 0, k, v, C_KV)


def _inproj_c(x, gain, w_in, q_gain, k_gain):
    b, t, d = x.shape
    tm = min(t, PROJ_ROW_TILE)
    n_in = w_in.shape[1]
    tile2 = lambda g: jnp.tile(g, 2).reshape(1, 128)
    rows = lambda width: pl.BlockSpec((None, tm, width), lambda i, j: (i, j, 0))
    shape = lambda width: jax.ShapeDtypeStruct((b, t, width), F32)
    return pl.pallas_call(
        _inproj_c_body,
        grid=(b, t // tm),
        in_specs=[rows(d), _const_spec((1, d)), _const_spec((d, n_in)), _const_spec((1, 128)), _const_spec((1, 128))],
        out_specs=[rows(C_HEADS * HEAD_DIM), rows(2 * C_KV * HEAD_DIM),
                   pl.BlockSpec((None, 2 * C_KV, tm, HEAD_DIM), lambda i, j: (i, 0, j, 0))],
        out_shape=[shape(C_HEADS * HEAD_DIM), shape(2 * C_KV * HEAD_DIM),
                   jax.ShapeDtypeStruct((b, 2 * C_KV, t, HEAD_DIM), BF16)],
        compiler_params=pltpu.CompilerParams(
            dimension_semantics=("parallel", "parallel"), vmem_limit_bytes=V7X_VMEM_LIMIT_BYTES),
        name="inproj_swa",
    )(x, gain.reshape(1, d), w_in.astype(BF16), tile2(q_gain), tile2(k_gain))


FFN_ROW_TILE = 512
FFN_COL_CHUNK = 1408


def _mixer_residual(y_ref, x_ref, mix_refs, wout_refs):
    y_ref[...] = x_ref[...]
    for m_ref, w_ref in zip(mix_refs, wout_refs):
        y_ref[...] += jnp.dot(m_ref[...].astype(BF16), w_ref[...], preferred_element_type=F32)
    return y_ref[...]


def _tail_prompt_body(*refs, n_mix, tm, ffc):
    x_ref = refs[0]
    mix_refs = refs[1:1 + n_mix]
    wout_refs = refs[1 + n_mix:1 + 2 * n_mix]
    (gain_ref, prev_ref, wup_ref, wgate_ref, cw_ref, cb_ref, wdown_ref, y_ref, cs_ref, hbuf_ref) = refs[1 + 2 * n_mix:]
    t = pl.program_id(1)
    xn = _rms_rows(_mixer_residual(y_ref, x_ref, mix_refs, wout_refs), gain_ref[...]).astype(BF16)
    for c in range(D_FF // ffc):
        lo = c * ffc
        h = jnp.dot(xn, wup_ref[:, lo:lo + ffc], preferred_element_type=F32)
        g = jnp.dot(xn, wgate_ref[:, lo:lo + ffc], preferred_element_type=F32)

        @pl.when(t == 0)
        def _():
            hbuf_ref[c, 6:8, :] = prev_ref[:, lo:lo + ffc]

        hbuf_ref[c, 8:8 + tm, :] = h
        hm1 = hbuf_ref[c, 7:7 + tm, :]
        hm2 = hbuf_ref[c, 6:6 + tm, :]
        cw = cw_ref[:, lo:lo + ffc]
        hc = cw[0:1] * hm2 + cw[1:2] * hm1 + cw[2:3] * h + cb_ref[:, lo:lo + ffc]
        a = (_gelu_tanh(hc) * g).astype(BF16)
        y_ref[...] += jnp.dot(a, wdown_ref[lo:lo + ffc, :], preferred_element_type=F32)
        hbuf_ref[c, 0:8, :] = h[tm - 8:tm, :]
        cs_ref[:, lo:lo + ffc] = h[tm - 2:tm, :]


def _tail_prompt(x, mixes, wouts, gain, prev, wup, wgate, cw, cb, wdown):
    b, t, d = x.shape
    tm, ffc = FFN_ROW_TILE, FFN_COL_CHUNK
    rows = lambda width: pl.BlockSpec((None, tm, width), lambda i, j: (i, j, 0))
    state = pl.BlockSpec((None, CONV_W - 1, D_FF), lambda i, j: (i, 0, 0))
    return pl.pallas_call(
        functools.partial(_tail_prompt_body, n_mix=len(mixes), tm=tm, ffc=ffc),
        grid=(b, t // tm),
        in_specs=[rows(d)] + [rows(m.shape[-1]) for m in mixes] + [_const_spec(w.shape) for w in wouts] + [
            _const_spec((1, d)), state, _const_spec((d, D_FF)), _const_spec((d, D_FF)),
            _const_spec((CONV_W, D_FF)), _const_spec((1, D_FF)), _const_spec((D_FF, d))],
        out_specs=[rows(d), state],
        out_shape=[jax.ShapeDtypeStruct((b, t, d), F32), jax.ShapeDtypeStruct((b, CONV_W - 1, D_FF), F32)],
        scratch_shapes=[pltpu.VMEM((D_FF // ffc, 8 + tm, ffc), F32)],
        compiler_params=pltpu.CompilerParams(
            dimension_semantics=("parallel", "arbitrary"), vmem_limit_bytes=V7X_VMEM_LIMIT_BYTES),
        name="tail_prompt",
    )(x, *mixes, *wouts, gain, prev, wup, wgate, cw, cb, wdown)


def _tail_sample_body(*refs, n_mix, ffc):
    x_ref = refs[0]
    mix_refs = refs[1:1 + n_mix]
    wout_refs = refs[1 + n_mix:1 + 2 * n_mix]
    (gain_ref, prev_ref, wup_ref, wgate_ref, cw_ref, cb_ref, wdown_ref, y_ref, cs_ref) = refs[1 + 2 * n_mix:]
    xn = _rms_rows(_mixer_residual(y_ref, x_ref, mix_refs, wout_refs), gain_ref[...]).astype(BF16)
    for c in range(D_FF // ffc):
        lo = c * ffc
        h = jnp.dot(xn, wup_ref[:, lo:lo + ffc], preferred_element_type=F32)
        g = jnp.dot(xn, wgate_ref[:, lo:lo + ffc], preferred_element_type=F32)
        hm2 = prev_ref[:, lo:lo + ffc]
        hm1 = prev_ref[:, D_FF + lo:D_FF + lo + ffc]
        cw = cw_ref[:, lo:lo + ffc]
        hc = cw[0:1] * hm2 + cw[1:2] * hm1 + cw[2:3] * h + cb_ref[:, lo:lo + ffc]
        a = (_gelu_tanh(hc) * g).astype(BF16)
        y_ref[...] += jnp.dot(a, wdown_ref[lo:lo + ffc, :], preferred_element_type=F32)
        cs_ref[:, lo:lo + ffc] = hm1
        cs_ref[:, D_FF + lo:D_FF + lo + ffc] = h


def _tail_sample(x, mixes, wouts, gain, prev, wup, wgate, cw, cb, wdown):
    n, d = x.shape
    full = lambda shape: _const_spec(shape)
    return pl.pallas_call(
        functools.partial(_tail_sample_body, n_mix=len(mixes), ffc=FFN_COL_CHUNK),
        grid=(1,),
        in_specs=[full((n, d))] + [full(m.shape) for m in mixes] + [full(w.shape) for w in wouts] + [
            full((1, d)), full((n, (CONV_W - 1) * D_FF)), full((d, D_FF)), full((d, D_FF)),
            full((CONV_W, D_FF)), full((1, D_FF)), full((D_FF, d))],
        out_specs=[pl.BlockSpec((n, d), lambda i: (0, 0)), pl.BlockSpec((n, (CONV_W - 1) * D_FF), lambda i: (0, 0))],
        out_shape=[jax.ShapeDtypeStruct((n, d), F32), jax.ShapeDtypeStruct((n, (CONV_W - 1) * D_FF), F32)],
        compiler_params=pltpu.CompilerParams(
            dimension_semantics=("arbitrary",), vmem_limit_bytes=V7X_VMEM_LIMIT_BYTES),
        name="tail_sample",
    )(x, *mixes, *wouts, gain, prev, wup, wgate, cw, cb, wdown)


def kernel(x_prompt, x_sample, cache_nsa_cmp, cache_nsa_sel, cache_nsa_win, state_s5_re, state_s5_im,
           cache_swa, state_ffn_conv, page_table, rel_bias, norm_mix, norm_ffn, a_w_in, a_w_out,
           nsa_q_gain, nsa_k_gain, nsa_cmp_pos, nsa_cmp_w1, nsa_cmp_w2, s5_a_re, s5_a_im, s5_log_dt,
           s5_b_re, s5_b_im, s5_c_re, s5_c_im, s5_d, s5_w_glu, s5_b_glu, c_w_in, c_w_out, c_q_gain,
           c_k_gain, c_sinks, ffn_w_up, ffn_w_gate, ffn_conv_w, ffn_conv_b, ffn_w_down):
    bp, tp, _ = x_prompt.shape
    bs, ts, _ = x_sample.shape
    assert ts == 1 and DEPTH == 2
    tab_a = rel_bias[:, :A_HEADS].reshape(NUM_BUCKETS, A_KV, A_REP)
    tab_c = rel_bias[:, :C_HEADS].reshape(NUM_BUCKETS, C_KV, C_REP)
    kv6 = lambda x: x.reshape(x.shape[:-1] + (2, x.shape[-1] // (2 * HEAD_DIM), HEAD_DIM))
    hp, hs = x_prompt, x_sample.reshape(1, bs, D_MODEL)
    conv_p, conv_s = [], []

    def tail(layer, hp, hs, mixes_p, mixes_s, wouts):
        wouts = [w.astype(BF16) for w in wouts]
        ffn = (norm_ffn[layer].reshape(1, D_MODEL),)
        wts = (ffn_w_up[layer].astype(BF16), ffn_w_gate[layer].astype(BF16), ffn_conv_w[layer],
               ffn_conv_b[layer].reshape(1, D_FF), ffn_w_down[layer].astype(BF16))
        hp, cp = _tail_prompt(hp, mixes_p, wouts, *ffn, jnp.zeros((bp, CONV_W - 1, D_FF), F32), *wts)
        hs2, cs = _tail_sample(hs[0], [m.reshape(bs, -1) for m in mixes_s], wouts, *ffn,
                               state_ffn_conv[layer].reshape(bs, (CONV_W - 1) * D_FF), *wts)
        conv_p.append(cp)
        conv_s.append(cs.reshape(bs, CONV_W - 1, D_FF))
        return hp, hs2.reshape(1, bs, D_MODEL)

    proj = (norm_mix[0], a_w_in[0], nsa_q_gain[0], nsa_k_gain[0])
    qp, cmp_p, sel_p, win_p, gate_p, up, kvt_p = _inproj_a(hp, *proj)
    qs, cmp_s, sel_s, win_s, gate_s, us, _ = _inproj_a(hs, *proj)
    cmp_w = _compress_weights(nsa_cmp_pos[0], nsa_cmp_w1[0], nsa_cmp_w2[0], nsa_k_gain[0])
    kc, vc = _cmp_prompt(cmp_p, cmp_w)
    o_ap = _nsa_prompt_attend(qp, gate_p, kc, vc, kvt_p, tab_a)
    o_as, nsa_win_s = _nsa_sample(qs.reshape(bs, A_KV, A_REP, HEAD_DIM),
                                  gate_s[0, :, :A_GATE].reshape(bs, A_KV, A_REP, 3),
                                  sel_s.reshape(bs, 2, A_KV, HEAD_DIM), win_s.reshape(bs, 2, A_KV, HEAD_DIM),
                                  cache_nsa_cmp[0], cache_nsa_sel[0], cache_nsa_win[0], page_table, cmp_w, tab_a)
    s5w = _s5_weights(s5_a_re[0], s5_a_im[0], s5_log_dt[0], s5_b_re[0], s5_b_im[0], s5_c_re[0], s5_c_im[0],
                      s5_d[0], s5_w_glu[0], s5_b_glu[0])
    h0 = jnp.zeros((bp, S5_NS), F32)
    o_bp, hr_p, hi_p = _s5_mix(up, h0, h0, s5w)
    o_bs, hr_s, hi_s = _s5_mix(us.reshape(bs, 1, B_WIDTH), state_s5_re[0].reshape(bs, S5_NS),
                               state_s5_im[0].reshape(bs, S5_NS), s5w)
    hp, hs = tail(0, hp, hs, [o_ap, o_bp], [o_as, o_bs], [a_w_out[0][:A_Q], a_w_out[0][A_Q:]])

    proj = (norm_mix[1], c_w_in[0], c_q_gain[0], c_k_gain[0])
    qcp, kv_p, kvt_c = _inproj_c(hp, *proj)
    qcs, kv_s, _ = _inproj_c(hs, *proj)
    o_cp = _swa_prompt(qcp, kvt_c, c_sinks[0], tab_c)
    new_kv = kv_s.reshape(bs, 2, C_KV, HEAD_DIM)
    o_cs, swa_s = _swa_sample(qcs.reshape(bs, C_KV, C_REP, HEAD_DIM), cache_swa[0],
                              jnp.transpose(new_kv, (0, 2, 1, 3)), c_sinks[0], tab_c)
    hp, hs = tail(1, hp, hs, [o_cp], [o_cs], [c_w_out[0]])

    state = lambda x, n: x.reshape(1, n, B_GROUPS, B_STATE)
    return (hp, hs.reshape(bs, ts, D_MODEL),
            kv6(cmp_p)[None], kv6(cmp_s).reshape(1, bs, ts, 2, A_KV, HEAD_DIM),
            kv6(sel_p)[None], kv6(sel_s).reshape(1, bs, ts, 2, A_KV, HEAD_DIM),
            kv6(win_p)[None, :, -min(WIN_A, tp):], nsa_win_s[None],
            state(hr_p, bp), state(hi_p, bp), state(hr_s, bs), state(hi_s, bs),
            kv6(kv_p)[None, :, -min(WIN_C, tp):], swa_s[None],
            jnp.stack(conv_p), jnp.stack(conv_s))
```

```python
import functools
import math

import jax
import jax.numpy as jnp
import numpy as np
from jax import lax
from jax.experimental import pallas as pl
from jax.experimental.pallas import tpu as pltpu

D_MODEL = 1024
DEPTH = 2
PAGE_SIZE = 128
HEAD_DIM = 64
A_HEADS = 8
A_KV = 2
A_REP = A_HEADS // A_KV
A_Q = A_HEADS * HEAD_DIM
A_KVW = A_KV * HEAD_DIM
A_GATE = 3 * A_HEADS
CMP_LEN = 32
CMP_STRIDE = 16
SEL_BLOCK = 64
SEL_TOPK = 16
WIN_A = 512
NSA_QBLK = 64
FORCE_BONUS = 1000.0
B_WIDTH = D_MODEL // 2
B_GROUP = 16
B_GROUPS = B_WIDTH // B_GROUP
B_STATE = 64
C_HEADS = D_MODEL // HEAD_DIM
C_KV = 2
C_REP = C_HEADS // C_KV
WIN_C = 128
NUM_BUCKETS = 32
MAX_DISTANCE = 128
D_FF = 2816
CONV_W = 3
EPS = 1e-6

F32 = jnp.float32
BF16 = jnp.bfloat16

V7X_VMEM_LIMIT_BYTES = 56 * 1024 * 1024


def _gelu_tanh(x):
    return 0.5 * x * (1.0 + jnp.tanh(math.sqrt(2.0 / math.pi) * (x + 0.044715 * (x * x * x))))


def _rms_rows(x, gain):
    return x * lax.rsqrt(jnp.mean(x * x, axis=-1, keepdims=True) + EPS) * gain


def _const_spec(shape):
    zeros = (0,) * len(shape)
    return pl.BlockSpec(shape, lambda *_: zeros, pipeline_mode=pl.Buffered(1))


NSA_Q_TILE = 128
NSA_FAR_TILE = 512
NSA_HEADS_PER_PASS = 4
NEG_INF = float("-inf")


def _dot_nt(a, b):
    return lax.dot_general(a, b, (((1,), (1,)), ((), ())), preferred_element_type=F32)


def _dot_tn(a, b):
    return lax.dot_general(a, b, (((0,), (0,)), ((), ())), preferred_element_type=F32)


def _softmax_start(s, v):
    m = jnp.max(s, axis=0, keepdims=True)
    e = jnp.exp(s - m)
    return m, jnp.sum(e, axis=0, keepdims=True), _dot_tn(v, e.astype(BF16))


def _softmax_more(carry, s, v):
    m, l, acc = carry
    m_new = jnp.maximum(m, jnp.max(s, axis=0, keepdims=True))
    alpha = jnp.exp(m - m_new)
    e = jnp.exp(s - m_new)
    return m_new, alpha * l + jnp.sum(e, axis=0, keepdims=True), alpha * acc + _dot_tn(v, e.astype(BF16))


def _nsa_prompt_body(q_ref, gate_ref, kc_ref, vc_ref, kvt_ref, btile_ref, cover_ref, o_ref, sel_scr, *, tq):
    width = A_REP * HEAD_DIM
    gates_t = gate_ref[...].T
    for g in range(A_KV):
        _nsa_prompt_group(g, q_ref.at[:, g * width:(g + 1) * width], gates_t, kc_ref.at[g], vc_ref.at[g],
                          kvt_ref.at[g], kvt_ref.at[A_KV + g], kvt_ref.at[2 * A_KV + g], kvt_ref.at[3 * A_KV + g],
                          btile_ref.at[g], cover_ref, o_ref.at[:, g * width:(g + 1) * width], sel_scr.at[g], tq)


def _nsa_prompt_group(g, q_ref, gates_t, kc_ref, vc_ref, ks_ref, vs_ref, kw_ref, vw_ref, btile_ref, cover_ref,
                      o_ref, sel_ref, tq):
    qt = pl.program_id(1)
    q0 = pl.multiple_of(qt * tq, tq)
    cols = A_REP * tq
    q = q_ref[...]
    qs = jnp.concatenate([q[:, r * HEAD_DIM:(r + 1) * HEAD_DIM] for r in range(A_REP)], axis=0)
    qs = (qs * (HEAD_DIM ** -0.5)).astype(BF16)

    def q_pos(height):
        return q0 + (lax.broadcasted_iota(jnp.int32, (height, cols), 1) & (tq - 1))

    def key_idx(height):
        return lax.broadcasted_iota(jnp.int32, (height, cols), 0)

    n_idx = key_idx(128)
    valid_c = (n_idx * CMP_STRIDE + (CMP_LEN - 1) <= q_pos(128)) & (n_idx < 127)
    s_c = jnp.where(valid_c, _dot_nt(kc_ref[...], qs), NEG_INF)
    m_c = jnp.max(s_c, axis=0, keepdims=True)
    m_c = jnp.where(m_c == NEG_INF, 0.0, m_c)
    e_c = jnp.exp(s_c - m_c)
    d_c = jnp.sum(e_c, axis=0, keepdims=True)
    p_c = (e_c / jnp.where(d_c > 0, d_c, 1.0)).astype(BF16)
    o_c = _dot_tn(vc_ref[...], p_c)
    imp_heads = jnp.dot(cover_ref[...], p_c, preferred_element_type=F32)
    imp = sum(imp_heads[0:32, r * tq:(r + 1) * tq] for r in range(A_REP))
    s_idx = lax.broadcasted_iota(jnp.int32, (32, tq), 0)
    qblk = (q0 + lax.broadcasted_iota(jnp.int32, (32, tq), 1)) >> 6
    forced = (s_idx == 0) | (s_idx == qblk) | (s_idx == qblk - 1)
    allowed = s_idx <= qblk
    score = jnp.where(allowed, imp + jnp.where(forced, FORCE_BONUS, 0.0), NEG_INF)
    rank = jnp.zeros((32, tq), F32)
    for j in range(32):
        other = score[j:j + 1, :]
        beats = (other > score) | ((other == score) & (s_idx > j))
        rank = rank + jnp.where(beats, 1.0, 0.0)
    sel = jnp.where((rank < SEL_TOPK) & allowed, 1.0, 0.0)
    hcols = NSA_HEADS_PER_PASS * tq
    far_end = q0 - tq
    sel_far_only = jnp.where(s_idx < far_end // SEL_BLOCK, sel, 0.0)
    sel_ref[...] = jnp.concatenate([jnp.concatenate([sel] * NSA_HEADS_PER_PASS, axis=1),
                                    jnp.concatenate([sel_far_only] * NSA_HEADS_PER_PASS, axis=1)], axis=0)

    def block_mask(k0, n_blocks, base=0):
        first = base + k0 // SEL_BLOCK
        return jnp.concatenate([jnp.broadcast_to(sel_ref[pl.ds(first + j, 1), :], (SEL_BLOCK, hcols))
                                for j in range(n_blocks)], axis=0) > 0.5

    def hq_pos(height):
        return q0 + (lax.broadcasted_iota(jnp.int32, (height, hcols), 1) & (tq - 1))

    def hkey_idx(height):
        return lax.broadcasted_iota(jnp.int32, (height, hcols), 0)

    prev0 = pl.multiple_of(jnp.maximum(q0 - tq, 0), tq)
    sel_near = jnp.concatenate([block_mask(prev0, tq // SEL_BLOCK), block_mask(q0, tq // SEL_BLOCK)], axis=0)
    n_far = (jnp.maximum(far_end, 0) + NSA_FAR_TILE - 1) // NSA_FAR_TILE
    w_far = WIN_A - tq
    wf0 = pl.multiple_of(jnp.maximum(q0 - WIN_A, 0), tq)
    wpos = wf0 + hkey_idx(w_far)
    wmask = (hq_pos(w_far) - wpos < WIN_A) & (wpos < far_end)

    outs = []
    for h0 in range(0, A_REP, NSA_HEADS_PER_PASS):
        csl = slice(h0 * tq, h0 * tq + hcols)
        qh = qs[h0 * tq:h0 * tq + hcols]
        btile = btile_ref[:, csl]

        def near(k_ref, v_ref, extra_mask):
            k = jnp.concatenate([k_ref[pl.ds(prev0, tq), :], k_ref[pl.ds(q0, tq), :]], axis=0)
            v = jnp.concatenate([v_ref[pl.ds(prev0, tq), :], v_ref[pl.ds(q0, tq), :]], axis=0)
            s = _dot_nt(k, qh) + btile
            return _softmax_start(s if extra_mask is None else jnp.where(extra_mask, s, NEG_INF), v)

        def sel_far(i, carry):
            k0 = pl.multiple_of(i * NSA_FAR_TILE, NSA_FAR_TILE)
            s = _dot_nt(ks_ref[pl.ds(k0, NSA_FAR_TILE), :], qh)
            mask = block_mask(k0, NSA_FAR_TILE // SEL_BLOCK, base=32)
            return _softmax_more(carry, jnp.where(mask, s, NEG_INF), vs_ref[pl.ds(k0, NSA_FAR_TILE), :])

        _, l_s, acc_s = lax.fori_loop(0, n_far, sel_far, near(ks_ref, vs_ref, sel_near))
        o_s = acc_s / l_s

        s_w = jnp.where(wmask, _dot_nt(kw_ref[pl.ds(wf0, w_far), :], qh), NEG_INF)
        _, l_w, acc_w = _softmax_more(near(kw_ref, vw_ref, None), s_w, vw_ref[pl.ds(wf0, w_far), :])
        o_w = acc_w / l_w

        for i in range(NSA_HEADS_PER_PASS):
            r = h0 + i
            sl = slice(i * tq, (i + 1) * tq)
            row = 3 * (g * A_REP + r)
            out_t = (gates_t[row:row + 1] * o_c[:, r * tq:(r + 1) * tq] + gates_t[row + 1:row + 2] * o_s[:, sl]
                     + gates_t[row + 2:row + 3] * o_w[:, sl])
            outs.append(out_t.T)
    o_ref[...] = jnp.concatenate(outs, axis=1)


def _bucket_np(dist):
    n = np.maximum(dist, 0)
    exact = NUM_BUCKETS // 2
    nf = np.maximum(n, exact).astype(np.float64)
    large = exact + (np.log(nf / exact) / math.log(MAX_DISTANCE / exact) * (NUM_BUCKETS - exact)).astype(np.int64)
    return np.where(n < exact, n, np.minimum(large, NUM_BUCKETS - 1)).astype(np.int32)


def _bias_lookup(tab, dist):
    bucket = _bucket_np(np.asarray(dist))
    onehot = (jnp.asarray(bucket.reshape(-1, 1)) == jnp.arange(NUM_BUCKETS)[None, :]).astype(F32)
    flat = jnp.dot(onehot, tab.reshape(NUM_BUCKETS, -1), precision=lax.Precision.HIGHEST)
    return flat.reshape(bucket.shape + tab.shape[1:])


def _near_bias_tiles(tab, tq, window):
    i = np.arange(tq)[:, None]
    j = np.arange(2 * tq)[None, :]
    dist = tq + i - j
    ok = (dist >= 0) & (dist < window)
    bias = jnp.transpose(_bias_lookup(tab, dist), (2, 3, 0, 1))
    tiles = jnp.stack([jnp.where(jnp.asarray(ok & (j >= tq)), bias, NEG_INF), jnp.where(jnp.asarray(ok), bias, NEG_INF)])
    return tiles.reshape(2, tab.shape[1], tab.shape[2] * tq, 2 * tq)


def _cover_matrix(n_cmp_pad, n_sel):
    n = np.arange(n_cmp_pad)
    c_start = n * CMP_STRIDE
    c_end = c_start + CMP_LEN - 1
    s_start = np.arange(128) * SEL_BLOCK
    cover = (c_start[:, None] < s_start[None, :] + SEL_BLOCK) & (c_end[:, None] >= s_start[None, :])
    cover &= (np.arange(128)[None, :] < n_sel)
    return np.tile(cover.astype(np.float32), (A_REP, 1))


def _nsa_prompt_attend(q, gates, kc, vc, kvt, tab):
    b, t, _ = q.shape
    tq = NSA_Q_TILE
    assert t % NSA_FAR_TILE == 0 and t // SEL_BLOCK <= 32 and t >= WIN_A and kc.shape[2] == 128
    btile = jnp.swapaxes(_near_bias_tiles(tab - tab[NUM_BUCKETS - 1], tq, 2 * tq), -1, -2)
    cover = jnp.asarray(_cover_matrix(128, t // SEL_BLOCK)[:128].T, BF16)
    cmp_spec = pl.BlockSpec((None, A_KV, 128, HEAD_DIM), lambda i, j: (i, 0, 0, 0))
    return pl.pallas_call(
        functools.partial(_nsa_prompt_body, tq=tq),
        grid=(b, t // tq),
        in_specs=[
            pl.BlockSpec((None, tq, A_Q), lambda i, j: (i, j, 0)),
            pl.BlockSpec((None, tq, 128), lambda i, j: (i, j, 0)),
            cmp_spec, cmp_spec,
            pl.BlockSpec((None, 4 * A_KV, t, HEAD_DIM), lambda i, j: (i, 0, 0, 0)),
            pl.BlockSpec((None, A_KV, 2 * tq, A_REP * tq), lambda i, j: (jnp.minimum(j, 1), 0, 0, 0)),
            pl.BlockSpec((128, 128), lambda i, j: (0, 0)),
        ],
        out_specs=pl.BlockSpec((None, tq, A_Q), lambda i, j: (i, j, 0)),
        out_shape=jax.ShapeDtypeStruct((b, t, A_Q), F32),
        scratch_shapes=[pltpu.VMEM((A_KV, 2 * 32, NSA_HEADS_PER_PASS * tq), F32)],
        compiler_params=pltpu.CompilerParams(
            dimension_semantics=("parallel", "arbitrary"), vmem_limit_bytes=V7X_VMEM_LIMIT_BYTES),
        name="nsa_prompt",
    )(q, gates, kc, vc, kvt, btile, cover)


S5_NS = B_GROUPS * B_STATE
S5_T_CHUNK = 64
S5_STRIP = 512
S5_HALVES = 2


def _s5_body(u_ref, h0r_ref, h0i_ref, ar_ref, ai_ref, ldt_ref, wb_ref, wc_ref, d_ref, wglu_ref, bglu_ref,
             o_ref, hr_ref, hi_ref, coef_ref, st_ref, xbuf_ref, ubuf_ref, obuf_ref, *, nb, steps, interleave):
    c = pl.program_id(0)

    @pl.when(c == 0)
    def _():
        dt = jnp.exp(ldt_ref[...])
        ar, ai = ar_ref[...], ai_ref[...]
        mag = jnp.exp(ar * dt)
        abr, abi = mag * jnp.cos(ai * dt), mag * jnp.sin(ai * dt)
        den = ar * ar + ai * ai
        wr = ((abr - 1.0) * ar + abi * ai) / den
        wi = (abi * ar - (abr - 1.0) * ai) / den
        for k, val in enumerate((abr, abi, wr, wi)):
            coef_ref[k] = jnp.broadcast_to(val, (nb, S5_NS))
        st_ref[0] = h0r_ref[...]
        st_ref[1] = h0i_ref[...]

    if interleave:
        for b in range(nb):
            for j in range(B_WIDTH // 128):
                ubuf_ref.at[j][pl.ds(b, steps, stride=nb), :] = u_ref[b, :, j * 128:(j + 1) * 128]
        u = jnp.concatenate([ubuf_ref[j] for j in range(B_WIDTH // 128)], axis=1)
    else:
        u = u_ref[...]
    hc, hs = B_WIDTH // S5_HALVES, S5_NS // S5_HALVES
    ub = u.astype(BF16)
    for h in range(S5_HALVES):
        bu = jnp.dot(ub[:, h * hc:(h + 1) * hc], wb_ref[h], preferred_element_type=F32)
        xbuf_ref[:, h * hs:(h + 1) * hs] = bu[:, :hs]
        xbuf_ref[:, S5_NS + h * hs:S5_NS + (h + 1) * hs] = bu[:, hs:]

    for lo in range(0, S5_NS, S5_STRIP):
        re = slice(lo, lo + S5_STRIP)
        im = slice(S5_NS + lo, S5_NS + lo + S5_STRIP)
        abr, abi, wr, wi = (coef_ref[k, :, re] for k in range(4))

        def step(t, carry):
            sr, si = carry
            r0 = pl.multiple_of(t * nb, nb)
            bur = xbuf_ref[pl.ds(r0, nb), re]
            bui = xbuf_ref[pl.ds(r0, nb), im]
            nsr = abr * sr - abi * si + (wr * bur - wi * bui)
            nsi = abr * si + abi * sr + (wr * bui + wi * bur)
            xbuf_ref[pl.ds(r0, nb), re] = nsr
            xbuf_ref[pl.ds(r0, nb), im] = nsi
            return nsr, nsi

        sr, si = lax.fori_loop(0, steps, step, (st_ref[0, :, re], st_ref[1, :, re]),
                               unroll=min(steps, 8))
        st_ref[0, :, re] = sr
        st_ref[1, :, re] = si

    y = jnp.concatenate(
        [jnp.dot(jnp.concatenate([xbuf_ref[:, h * hs:(h + 1) * hs], xbuf_ref[:, S5_NS + h * hs:S5_NS + (h + 1) * hs]],
                                 axis=1).astype(BF16), wc_ref[h], preferred_element_type=F32)
         for h in range(S5_HALVES)], axis=1) + d_ref[...] * u
    z = _gelu_tanh(y)
    gate = jnp.dot(z.astype(BF16), wglu_ref[...], preferred_element_type=F32) + bglu_ref[...]
    out = z * (1.0 / (1.0 + jnp.exp(-gate)))
    if interleave:
        for j in range(B_WIDTH // 128):
            obuf_ref[j] = out[:, j * 128:(j + 1) * 128]
        for b in range(nb):
            for j in range(B_WIDTH // 128):
                o_ref[b, :, j * 128:(j + 1) * 128] = obuf_ref.at[j][pl.ds(b, steps, stride=nb), :]
    else:
        o_ref[...] = out
    hr_ref[...] = st_ref[0]
    hi_ref[...] = st_ref[1]


def _s5_weights(a_re, a_im, log_dt, b_re, b_im, c_re, c_im, d, w_glu, b_glu):
    eye = jnp.eye(B_GROUPS, dtype=F32)
    blk_in = lambda w: jnp.einsum('hg,gpc->hcgp', eye, w).reshape(B_WIDTH, S5_NS)
    blk_out = lambda w: jnp.einsum('gh,gcp->gphc', eye, w).reshape(S5_NS, B_WIDTH)
    hc, hs = B_WIDTH // S5_HALVES, S5_NS // S5_HALVES
    diag = lambda w, h: w[h * hc:(h + 1) * hc, h * hs:(h + 1) * hs]
    wb = jnp.stack([jnp.concatenate([diag(blk_in(b_re), h), diag(blk_in(b_im), h)], axis=1)
                    for h in range(S5_HALVES)]).astype(BF16)
    diag_t = lambda w, h: w[h * hs:(h + 1) * hs, h * hc:(h + 1) * hc]
    wc = jnp.stack([jnp.concatenate([diag_t(blk_out(c_re), h), -diag_t(blk_out(c_im), h)], axis=0)
                    for h in range(S5_HALVES)]).astype(BF16)
    flat = lambda x: x.reshape(1, S5_NS)
    return (flat(a_re), flat(a_im), flat(jnp.repeat(log_dt, B_STATE)), wb, wc, d.reshape(1, B_WIDTH),
            w_glu.astype(BF16), b_glu.reshape(1, B_WIDTH))


def _s5_mix(u, h_re, h_im, weights):
    nb, t, _ = u.shape
    interleave = t > 1
    steps = min(t, S5_T_CHUNK)
    rows = nb * steps
    body = functools.partial(_s5_body, nb=nb, steps=steps, interleave=interleave)
    if interleave:
        u_in = u
        u_spec = pl.BlockSpec((nb, steps, B_WIDTH), lambda c: (0, c, 0))
        o_shape = jax.ShapeDtypeStruct((nb, t, B_WIDTH), F32)
        scratch_rows = rows
    else:
        u_in = u.reshape(nb, B_WIDTH)
        u_spec = pl.BlockSpec((nb, B_WIDTH), lambda c: (0, 0))
        o_shape = jax.ShapeDtypeStruct((nb, B_WIDTH), F32)
        scratch_rows = 8
    o, hr, hi = pl.pallas_call(
        body,
        grid=(t // steps,),
        in_specs=[
            u_spec, _const_spec((nb, S5_NS)), _const_spec((nb, S5_NS)),
            _const_spec((1, S5_NS)), _const_spec((1, S5_NS)), _const_spec((1, S5_NS)),
            _const_spec((S5_HALVES, B_WIDTH // S5_HALVES, 2 * S5_NS // S5_HALVES)),
            _const_spec((S5_HALVES, 2 * S5_NS // S5_HALVES, B_WIDTH // S5_HALVES)),
            _const_spec((1, B_WIDTH)), _const_spec((B_WIDTH, B_WIDTH)), _const_spec((1, B_WIDTH)),
        ],
        out_specs=[u_spec, pl.BlockSpec((nb, S5_NS), lambda c: (0, 0)), pl.BlockSpec((nb, S5_NS), lambda c: (0, 0))],
        out_shape=[o_shape, jax.ShapeDtypeStruct((nb, S5_NS), F32), jax.ShapeDtypeStruct((nb, S5_NS), F32)],
        scratch_shapes=[
            pltpu.VMEM((4, nb, S5_NS), F32),
            pltpu.VMEM((2, nb, S5_NS), F32),
            pltpu.VMEM((rows, 2 * S5_NS), F32),
            pltpu.VMEM((B_WIDTH // 128, scratch_rows, 128), F32),
            pltpu.VMEM((B_WIDTH // 128, scratch_rows, 128), F32),
        ],
        compiler_params=pltpu.CompilerParams(
            dimension_semantics=("arbitrary",),
            vmem_limit_bytes=V7X_VMEM_LIMIT_BYTES),
        name="s5_mix",
    )(u_in, h_re, h_im, *weights)
    return o.reshape(nb, t, B_WIDTH), hr, hi


CHUNK_W = CMP_STRIDE * 2 * A_KVW
CHUNKS_PER_PAGE = PAGE_SIZE // CMP_STRIDE
SEL_PAD = 256


def _bf16_round(x):
    return x.astype(BF16).astype(F32)


def _compress_tail(c, pos, w2):
    c1 = c[:, 128:]
    nxt = jnp.concatenate([c1[1:], c1[:1]], axis=0)
    hid = (pos + c[:, :128]) + nxt
    return jnp.dot(_gelu_tanh(hid).astype(BF16), w2, preferred_element_type=F32)


def _rms_heads128(x, gain):
    left = lax.broadcasted_iota(jnp.int32, x.shape, 1) < HEAD_DIM
    sq = x * x
    s0 = jnp.sum(jnp.where(left, sq, 0.0), axis=-1, keepdims=True)
    s1 = jnp.sum(jnp.where(left, 0.0, sq), axis=-1, keepdims=True)
    ms = jnp.where(left, s0, s1) * (1.0 / HEAD_DIM)
    return x * lax.rsqrt(ms + EPS) * gain


def _pad_rows8(x):
    return jnp.concatenate([x, jnp.zeros((8 - x.shape[0], x.shape[1]), x.dtype)], axis=0)


def _compress_mlp(row_of_chunks, n_chunks, w_ref, pos, w2_ref):
    acc = jnp.zeros((n_chunks, 2 * A_KVW), F32)
    for sp in range(CMP_STRIDE // 2):
        x = jnp.concatenate([row_of_chunks(2 * sp + e) for e in range(2)], axis=1).astype(BF16)
        acc = acc + jnp.dot(x, w_ref[sp], preferred_element_type=F32)
    return _compress_tail(acc, pos, w2_ref[...])


def _compress_rows(rows_scr, kv, n_chunks, w_ref, pos, w2_ref):
    return _compress_mlp(lambda s: rows_scr.at[kv][pl.ds(s, n_chunks, stride=CMP_STRIDE), :],
                         n_chunks, w_ref, pos, w2_ref)


def _compress_weights(cmp_pos, cmp_w1, cmp_w2, k_gain):
    eye = jnp.eye(A_KV, dtype=F32)
    w1 = cmp_w1.reshape(2, 2, CMP_STRIDE, HEAD_DIM, HEAD_DIM)
    w_big = jnp.einsum('ajsdh,gk->asgdjkh', w1, eye).reshape(2, CMP_STRIDE // 2, 2 * A_KVW, 2 * A_KVW).astype(BF16)
    w_pos = jnp.concatenate([cmp_w1, cmp_w1], axis=-1).astype(BF16)
    pos = jnp.broadcast_to(cmp_pos.reshape(2, 1, CMP_LEN * HEAD_DIM), (2, 8, CMP_LEN * HEAD_DIM)).astype(BF16)
    w2_big = jnp.einsum('ahd,gk->aghkd', cmp_w2, eye).reshape(2, A_KVW, A_KVW).astype(BF16)
    gain2 = jnp.tile(k_gain, A_KV).reshape(1, A_KVW)
    return (pos[0], pos[1], w_pos[0], w_pos[1], w_big[0], w_big[1], w2_big[0], w2_big[1], gain2)


def _compress_specs(full):
    mlp_w = (CMP_STRIDE // 2, 2 * A_KVW, 2 * A_KVW)
    return [full((8, CMP_LEN * HEAD_DIM)), full((8, CMP_LEN * HEAD_DIM)),
            full((CMP_LEN * HEAD_DIM, A_KVW)), full((CMP_LEN * HEAD_DIM, A_KVW)),
            full(mlp_w), full(mlp_w), full((A_KVW, A_KVW)), full((A_KVW, A_KVW)), full((1, A_KVW))]


def _pos_terms(pos_scr, posk_ref, posv_ref, wpos_k_ref, wpos_v_ref):
    pos_scr[0] = jnp.dot(posk_ref[...], wpos_k_ref[...], preferred_element_type=F32)
    pos_scr[1] = jnp.dot(posv_ref[...], wpos_v_ref[...], preferred_element_type=F32)


def _cmp_prompt_body(cmp_ref, posk_ref, posv_ref, wpos_k_ref, wpos_v_ref, wk_ref, wv_ref, w2k_ref, w2v_ref,
                     kgain_ref, kc_ref, vc_ref, pos_scr, rows_scr, *, n_chunks):
    @pl.when(pl.program_id(0) == 0)
    def _():
        _pos_terms(pos_scr, posk_ref, posv_ref, wpos_k_ref, wpos_v_ref)

    rows_scr[0] = cmp_ref[:, :A_KVW]
    rows_scr[1] = cmp_ref[:, A_KVW:]
    kc = _rms_heads128(_compress_rows(rows_scr, 0, n_chunks, wk_ref, pos_scr[0, 0:1], w2k_ref),
                       kgain_ref[...]).astype(BF16)
    vc = _compress_rows(rows_scr, 1, n_chunks, wv_ref, pos_scr[1, 0:1], w2v_ref).astype(BF16)
    for g in range(A_KV):
        kc_ref[g] = kc[:, g * HEAD_DIM:(g + 1) * HEAD_DIM]
        vc_ref[g] = vc[:, g * HEAD_DIM:(g + 1) * HEAD_DIM]


def _cmp_prompt(cmp_rows, cmp_weights):
    b, t, _ = cmp_rows.shape
    n_chunks = t // CMP_STRIDE
    full = lambda shape: pl.BlockSpec(shape, lambda i: (0,) * len(shape))
    out_spec = pl.BlockSpec((None, A_KV, n_chunks, HEAD_DIM), lambda i: (i, 0, 0, 0))
    out_shape = jax.ShapeDtypeStruct((b, A_KV, n_chunks, HEAD_DIM), BF16)
    return pl.pallas_call(
        functools.partial(_cmp_prompt_body, n_chunks=n_chunks),
        grid=(b,),
        in_specs=[pl.BlockSpec((None, t, 2 * A_KVW), lambda i: (i, 0, 0))] + _compress_specs(full),
        out_specs=[out_spec, out_spec],
        out_shape=[out_shape, out_shape],
        scratch_shapes=[pltpu.VMEM((2, 8, A_KVW), F32), pltpu.VMEM((2, t, A_KVW), F32)],
        compiler_params=pltpu.CompilerParams(
            dimension_semantics=("arbitrary",), vmem_limit_bytes=V7X_VMEM_LIMIT_BYTES),
        name="nsa_cmp_prompt",
    )(cmp_rows, *cmp_weights)


def _nsa_sample_cmp_body(pt_ref, *refs, n_pages):
    page_refs = refs[:n_pages]
    (posk_ref, posv_ref, wpos_k_ref, wpos_v_ref, wk_ref, wv_ref, w2k_ref, w2v_ref, kgain_ref,
     kc_ref, vc_ref, pos_scr, rows_scr) = refs[n_pages:]

    @pl.when(pl.program_id(0) == 0)
    def _():
        _pos_terms(pos_scr, posk_ref, posv_ref, wpos_k_ref, wpos_v_ref)

    n_chunks = n_pages * CHUNKS_PER_PAGE
    for kv in range(2):
        for p in range(n_pages):
            rows = page_refs[p][kv].reshape(A_KVW, PAGE_SIZE).T
            by_s = pltpu.einshape("csl->scl", rows.reshape(CHUNKS_PER_PAGE, CMP_STRIDE, A_KVW))
            for s in range(CMP_STRIDE):
                rows_scr[kv, s, p * CHUNKS_PER_PAGE:(p + 1) * CHUNKS_PER_PAGE, :] = by_s[s]

    compress = lambda kv, w_ref, w2_ref: _compress_mlp(lambda s: rows_scr[kv, s], n_chunks, w_ref,
                                                       pos_scr[kv, 0:1], w2_ref)
    kc_ref[...] = _rms_heads128(compress(0, wk_ref, w2k_ref), kgain_ref[...]).astype(BF16)
    vc_ref[...] = compress(1, wv_ref, w2v_ref).astype(BF16)


NSA_PICK_BATCH = 8


def _nsa_sample_pick_body(q_ref, kc_ref, vc_ref, cover_ref, oc_ref, idx_ref, *, n_cmp, n_sel):
    nb, n_chunks = kc_ref.shape[0], kc_ref.shape[1]
    col = lax.broadcasted_iota(jnp.int32, (8, n_chunks), 1)
    imps = []
    for i in range(nb):
        for g in range(A_KV):
            hs = slice(g * HEAD_DIM, (g + 1) * HEAD_DIM)
            q8 = (_pad_rows8(q_ref[i, g]) * (HEAD_DIM ** -0.5)).astype(BF16)
            s = jnp.where(col < n_cmp, _dot_nt(q8, kc_ref[i, :, hs]), NEG_INF)
            e = jnp.exp(s - jnp.max(s, axis=-1, keepdims=True))
            p = (e / jnp.sum(e, axis=-1, keepdims=True)).astype(BF16)
            oc_ref[i, g] = jnp.dot(p, vc_ref[i, :, hs], preferred_element_type=F32)[0:A_REP]
            imps.append(jnp.sum(jnp.dot(p, cover_ref[...], preferred_element_type=F32)[0:A_REP], axis=0, keepdims=True))
    imp = jnp.concatenate(imps, axis=0)
    rows = nb * A_KV
    s_idx = lax.broadcasted_iota(jnp.int32, (rows, SEL_PAD), 1)
    s_idx_f = s_idx.astype(F32)
    forced = (s_idx == 0) | (s_idx == n_sel - 1) | (s_idx == n_sel - 2)
    score = jnp.where(s_idx < n_sel, imp + jnp.where(forced, FORCE_BONUS, 0.0), NEG_INF)
    rank = jnp.zeros((rows, SEL_PAD), F32)
    for j in range(n_sel):
        cj = score[:, j:j + 1]
        beats = (cj > score) | ((cj == score) & (s_idx > j))
        rank = rank + jnp.where(beats, 1.0, 0.0)
    lane = lax.broadcasted_iota(jnp.int32, (rows, 128), 1)
    picks = jnp.zeros((rows, 128), F32)
    for r in range(SEL_TOPK):
        block = jnp.sum(jnp.where(rank == float(r), s_idx_f, 0.0), axis=-1, keepdims=True)
        picks = picks + jnp.where(lane == r, block, 0.0)
    idx_ref[...] = picks.astype(jnp.int32)


def _nsa_sample_att_body(idx_ref, pt_ref, *refs, n_past_blk):
    n_slots = A_KV * SEL_TOPK
    slot_refs = refs[:n_slots]
    (q_all, gate_all, oc_all, newsel_all, wcache_all, newwin_all, newcol_all, selb_all, winb_all, newb_all,
     o_all, wout_all) = refs[n_slots:]
    b = pl.program_id(0)
    for g in range(A_KV):
        _nsa_sample_att_group(b, g, idx_ref, slot_refs[g * SEL_TOPK:(g + 1) * SEL_TOPK], q_all.at[g], gate_all.at[g],
                              oc_all.at[g], newsel_all.at[g], wcache_all.at[:, g], newwin_all.at[g], selb_all.at[g],
                              winb_all.at[g], newb_all.at[g], o_all.at[g], n_past_blk)
        for kv in range(2):
            wout_all[kv, g] = _shift_in(wcache_all[kv, g], newcol_all[g][:, kv:kv + 1])


def _shift_in(rows_last, new_col):
    w_len = rows_last.shape[-1]
    lane = lax.broadcasted_iota(jnp.int32, rows_last.shape, 1)
    return jnp.where(lane == w_len - 1, new_col, pltpu.roll(rows_last, w_len - 1, axis=1))


def _nsa_sample_att_group(b, g, idx_ref, page_refs, q_ref, gate_ref, oc_ref, newsel_ref, wcache_ref, newwin_ref,
                          selb_ref, winb_ref, newb_ref, o_ref, n_past_blk):
    blocks_per_page = PAGE_SIZE // SEL_BLOCK
    q8 = (_pad_rows8(q_ref[...]) * (HEAD_DIM ** -0.5)).astype(BF16)
    new_bias = newb_ref[:, 0:1]

    def attend(s, v_t, new_kv):
        k_new = _bf16_round(new_kv[0:1])
        v_new = _bf16_round(new_kv[1:2])
        s_new = jnp.sum(q8.astype(F32) * k_new, axis=-1, keepdims=True) + new_bias
        m = jnp.maximum(jnp.max(s, axis=-1, keepdims=True), s_new)
        e = jnp.exp(s - m)
        e_new = jnp.exp(s_new - m)
        den = jnp.sum(e, axis=-1, keepdims=True) + e_new
        return _dot_nt((e / den).astype(BF16), v_t) + _bf16_round(e_new / den) * v_new

    lane = lax.broadcasted_iota(jnp.int32, (8, PAGE_SIZE), 1)
    near = selb_ref[...]
    bias = []
    for k in range(SEL_TOPK):
        ik = idx_ref[(b * A_KV + g) * SEL_TOPK + k]
        blk = jnp.minimum(ik, n_past_blk - 1)
        near_k = jnp.where(blk // blocks_per_page == (n_past_blk - 1) // blocks_per_page, near, 0.0)
        keep = (lane // SEL_BLOCK == blk % blocks_per_page) & (ik < n_past_blk)
        bias.append(jnp.where(keep, near_k, NEG_INF))
    k_sel = jnp.concatenate([page_refs[k][0].astype(BF16) for k in range(SEL_TOPK)], axis=1)
    v_sel = jnp.concatenate([page_refs[k][1].astype(BF16) for k in range(SEL_TOPK)], axis=1)
    s_sel = jnp.dot(q8, k_sel, preferred_element_type=F32) + jnp.concatenate(bias, axis=1)
    o_s = attend(s_sel, v_sel, newsel_ref[...])

    w_len = wcache_ref.shape[-1]
    wcol = lax.broadcasted_iota(jnp.int32, (8, w_len), 1)
    s_w = jnp.dot(q8, wcache_ref[0].astype(BF16), preferred_element_type=F32) + winb_ref[...]
    o_w = attend(jnp.where(wcol >= 1, s_w, NEG_INF), wcache_ref[1].astype(BF16), newwin_ref[...])
    gates = gate_ref[...]
    o_ref[...] = gates[:, 0:1] * oc_ref[...] + gates[:, 1:2] * o_s[0:A_REP] + gates[:, 2:3] * o_w[0:A_REP]


def _rows_last(cache):
    nd = cache.ndim
    return jnp.transpose(cache, tuple(range(nd - 4)) + (nd - 3, nd - 2, nd - 1, nd - 4))


def _rows_first(cache):
    nd = cache.ndim
    return jnp.transpose(cache, tuple(range(nd - 4)) + (nd - 1, nd - 4, nd - 3, nd - 2))


def _nsa_sample(q, gates, new_sel, new_win, pool_cmp, pool_sel, win_cache, page_table, cmp_weights, tab):
    n, n_pages = page_table.shape
    past = n_pages * PAGE_SIZE
    n_past_blk = past // SEL_BLOCK
    n_sel = n_past_blk + 1
    n_chunks = n_pages * CHUNKS_PER_PAGE
    w_len = win_cache.shape[1]
    blocks_per_page = PAGE_SIZE // SEL_BLOCK
    assert n_sel <= SEL_PAD and w_len == WIN_A and past >= WIN_A and blocks_per_page == 2
    pt_flat = page_table.reshape(-1)

    c_idx = np.arange(n_chunks)
    s_start = np.arange(SEL_PAD) * SEL_BLOCK
    cover = ((c_idx[:, None] * CMP_STRIDE < s_start[None, :] + SEL_BLOCK)
             & (c_idx[:, None] * CMP_STRIDE + CMP_LEN - 1 >= s_start[None, :])
             & (c_idx[:, None] < n_chunks - 1) & (np.arange(SEL_PAD)[None, :] < n_sel))
    cover = jnp.asarray(cover.astype(np.float32), BF16)

    head_spec = lambda last: pl.BlockSpec((None, A_KV, A_REP, last), lambda i, *_: (i, 0, 0, 0))
    page_specs = [pl.BlockSpec((None, 2, A_KV, HEAD_DIM, PAGE_SIZE),
                               functools.partial(lambda i, pt, p: (pt[i * n_pages + p], 0, 0, 0, 0), p=p))
                  for p in range(n_pages)]
    full = lambda shape: pl.BlockSpec(shape, lambda i, *_: (0,) * len(shape))
    pool_c = _rows_last(pool_cmp)
    cmp_spec = pl.BlockSpec((None, n_chunks, A_KVW), lambda i, *_: (i, 0, 0))
    cmp_shape = jax.ShapeDtypeStruct((n, n_chunks, A_KVW), BF16)
    kc, vc = pl.pallas_call(
        functools.partial(_nsa_sample_cmp_body, n_pages=n_pages),
        grid_spec=pltpu.PrefetchScalarGridSpec(
            num_scalar_prefetch=1,
            grid=(n,),
            in_specs=page_specs + _compress_specs(full),
            out_specs=[cmp_spec, cmp_spec],
            scratch_shapes=[pltpu.VMEM((2, 8, A_KVW), F32), pltpu.VMEM((2, CMP_STRIDE, n_chunks, A_KVW), F32)],
        ),
        out_shape=[cmp_shape, cmp_shape],
        compiler_params=pltpu.CompilerParams(
            dimension_semantics=("arbitrary",), vmem_limit_bytes=V7X_VMEM_LIMIT_BYTES),
        name="nsa_sample_cmp",
    )(pt_flat, *([pool_c] * n_pages), *cmp_weights)

    nb = NSA_PICK_BATCH
    assert n % nb == 0
    batch = lambda *tail: pl.BlockSpec((nb,) + tail, lambda i: (i,) + (0,) * len(tail))
    o_c, idx = pl.pallas_call(
        functools.partial(_nsa_sample_pick_body, n_cmp=n_chunks - 1, n_sel=n_sel),
        grid=(n // nb,),
        in_specs=[batch(A_KV, A_REP, HEAD_DIM), batch(n_chunks, A_KVW), batch(n_chunks, A_KVW),
                  pl.BlockSpec((n_chunks, SEL_PAD), lambda i: (0, 0))],
        out_specs=[batch(A_KV, A_REP, HEAD_DIM), pl.BlockSpec((nb * A_KV, 128), lambda i: (i, 0))],
        out_shape=[jax.ShapeDtypeStruct((n, A_KV, A_REP, HEAD_DIM), F32),
                   jax.ShapeDtypeStruct((n * A_KV, 128), jnp.int32)],
        compiler_params=pltpu.CompilerParams(
            dimension_semantics=("arbitrary",), vmem_limit_bytes=V7X_VMEM_LIMIT_BYTES),
        name="nsa_sample_pick",
    )(q, kc, vc, cover)

    rel = lambda dist: jnp.pad(jnp.transpose(_bias_lookup(tab - tab[NUM_BUCKETS - 1], dist), (1, 2, 0)),
                               ((0, 0), (0, 8 - A_REP), (0, 0)))
    sel_bias = rel(past - (past - PAGE_SIZE + np.arange(PAGE_SIZE)))
    win_bias = rel(w_len - np.arange(w_len))
    new_bias = rel(np.zeros(128, np.int64))

    idx_flat = idx[:, :SEL_TOPK].reshape(-1)

    def page_map(i, idx_s, pt, g, slot):
        blk = jnp.minimum(idx_s[(i * A_KV + g) * SEL_TOPK + slot], n_past_blk - 1)
        return (pt[i * n_pages + blk // blocks_per_page], 0, g, 0, 0)

    slot_specs = [pl.BlockSpec((None, 2, None, HEAD_DIM, PAGE_SIZE), functools.partial(page_map, g=g, slot=s))
                  for g in range(A_KV) for s in range(SEL_TOPK)]
    per_head = lambda rows, last: pl.BlockSpec((None, A_KV, rows, last), lambda i, *_: (i, 0, 0, 0))
    per_group = lambda last: pl.BlockSpec((A_KV, 8, last), lambda i, *_: (0, 0, 0))
    win_spec = pl.BlockSpec((None, 2, A_KV, HEAD_DIM, w_len), lambda i, *_: (i, 0, 0, 0, 0))
    o, win_next = pl.pallas_call(
        functools.partial(_nsa_sample_att_body, n_past_blk=n_past_blk),
        grid_spec=pltpu.PrefetchScalarGridSpec(
            num_scalar_prefetch=2,
            grid=(n,),
            in_specs=slot_specs + [
                per_head(A_REP, HEAD_DIM), per_head(A_REP, 3), per_head(A_REP, HEAD_DIM), per_head(2, HEAD_DIM),
                win_spec, per_head(2, HEAD_DIM), per_head(HEAD_DIM, 2),
                per_group(PAGE_SIZE), per_group(w_len), per_group(128),
            ],
            out_specs=[per_head(A_REP, HEAD_DIM), win_spec],
        ),
        out_shape=[jax.ShapeDtypeStruct((n, A_KV, A_REP, HEAD_DIM), F32),
                   jax.ShapeDtypeStruct((n, 2, A_KV, HEAD_DIM, w_len), F32)],
        compiler_params=pltpu.CompilerParams(
            dimension_semantics=("arbitrary",), vmem_limit_bytes=V7X_VMEM_LIMIT_BYTES),
        name="nsa_sample_att",
    )(idx_flat, pt_flat, *([_rows_last(pool_sel)] * (A_KV * SEL_TOPK)), q, gates, o_c,
      jnp.transpose(new_sel, (0, 2, 1, 3)), _rows_last(win_cache), jnp.transpose(new_win, (0, 2, 1, 3)),
      jnp.transpose(new_win, (0, 2, 3, 1)), sel_bias, win_bias, new_bias)
    return o, _rows_first(win_next)


SWA_Q_TILE = WIN_C


def _swa_prompt_body(q_ref, k_ref, v_ref, btile_ref, sink_ref, o_ref, *, tq):
    qt = pl.program_id(2)
    q0 = pl.multiple_of(qt * tq, tq)
    q = q_ref[...]
    qs = jnp.concatenate([q[:, r * HEAD_DIM:(r + 1) * HEAD_DIM] for r in range(C_REP)], axis=0)
    qs = (qs * (HEAD_DIM ** -0.5)).astype(BF16)
    prev0 = pl.multiple_of(jnp.maximum(q0 - tq, 0), tq)
    k = jnp.concatenate([k_ref[pl.ds(prev0, tq), :], k_ref[pl.ds(q0, tq), :]], axis=0)
    v = jnp.concatenate([v_ref[pl.ds(prev0, tq), :], v_ref[pl.ds(q0, tq), :]], axis=0)
    s = _dot_nt(k, qs) + btile_ref[...]
    sinks = sink_ref[...]
    sink = jnp.concatenate([sinks[r:r + 1, :] for r in range(C_REP)], axis=1)
    m = jnp.maximum(jnp.max(s, axis=0, keepdims=True), sink)
    e = jnp.exp(s - m)
    p = e / (jnp.sum(e, axis=0, keepdims=True) + jnp.exp(sink - m))
    o = lax.dot_general(v, p.astype(BF16), (((0,), (0,)), ((), ())), preferred_element_type=F32)
    o_ref[...] = jnp.concatenate([o[:, r * tq:(r + 1) * tq].T for r in range(C_REP)], axis=1)


def _swa_prompt(q, kvt, sinks, tab):
    b, t, _ = q.shape
    tq = SWA_Q_TILE
    btile = jnp.swapaxes(_near_bias_tiles(tab, tq, WIN_C), -1, -2)
    kv_spec = lambda slot: pl.BlockSpec((None, None, t, HEAD_DIM), lambda i, g, j: (i, slot + g, 0, 0))
    qo_spec = pl.BlockSpec((None, tq, C_REP * HEAD_DIM), lambda i, g, j: (i, j, g))
    sink_lanes = jnp.broadcast_to(sinks.reshape(C_KV, C_REP, 1), (C_KV, C_REP, 128))
    return pl.pallas_call(
        functools.partial(_swa_prompt_body, tq=tq),
        grid=(b, C_KV, t // tq),
        in_specs=[qo_spec, kv_spec(0), kv_spec(C_KV),
                  pl.BlockSpec((None, None, 2 * tq, C_REP * tq), lambda i, g, j: (jnp.minimum(j, 1), g, 0, 0)),
                  pl.BlockSpec((None, C_REP, 128), lambda i, g, j: (g, 0, 0))],
        out_specs=qo_spec,
        out_shape=jax.ShapeDtypeStruct((b, t, C_HEADS * HEAD_DIM), F32),
        compiler_params=pltpu.CompilerParams(
            dimension_semantics=("parallel", "parallel", "arbitrary"), vmem_limit_bytes=V7X_VMEM_LIMIT_BYTES),
        name="swa_prompt",
    )(q, kvt, kvt, btile, sink_lanes)


def _swa_sample_body(q_ref, cache_ref, new_ref, newcol_ref, bias_ref, newb_ref, sink_ref, o_ref, next_ref):
    w_len = cache_ref.shape[-1]
    wcol = lax.broadcasted_iota(jnp.int32, (C_REP, w_len), 1)
    for g in range(C_KV):
        q8 = (q_ref[g] * (HEAD_DIM ** -0.5)).astype(BF16)
        s = jnp.dot(q8, cache_ref[0, g].astype(BF16), preferred_element_type=F32) + bias_ref[g]
        s = jnp.where(wcol >= 1, s, NEG_INF)
        k_new = _bf16_round(new_ref[g, 0:1])
        v_new = _bf16_round(new_ref[g, 1:2])
        s_new = jnp.sum(q8.astype(F32) * k_new, axis=-1, keepdims=True) + newb_ref[g][:, 0:1]
        sink = sink_ref[g][:, 0:1]
        m = jnp.maximum(jnp.maximum(jnp.max(s, axis=-1, keepdims=True), s_new), sink)
        e = jnp.exp(s - m)
        e_new = jnp.exp(s_new - m)
        den = jnp.sum(e, axis=-1, keepdims=True) + e_new + jnp.exp(sink - m)
        o_ref[g] = _dot_nt((e / den).astype(BF16), cache_ref[1, g].astype(BF16)) + _bf16_round(e_new / den) * v_new
        for kv in range(2):
            next_ref[kv, g] = _shift_in(cache_ref[kv, g], newcol_ref[g][:, kv:kv + 1])


def _swa_sample(q, cache, new_kv, sinks, tab):
    n, w_len = cache.shape[:2]
    bias = jnp.transpose(_bias_lookup(tab, w_len - np.arange(w_len)), (1, 2, 0))
    lanes = lambda x: jnp.broadcast_to(x[:, :, None], (C_KV, C_REP, 128))
    full = lambda shape: pl.BlockSpec(shape, lambda i: (0,) * len(shape))
    cache_spec = pl.BlockSpec((None, 2, C_KV, HEAD_DIM, w_len), lambda i: (i, 0, 0, 0, 0))
    o_spec = pl.BlockSpec((None, C_KV, C_REP, HEAD_DIM), lambda i: (i, 0, 0, 0))
    o, cache_next = pl.pallas_call(
        _swa_sample_body,
        grid=(n,),
        in_specs=[
            o_spec, cache_spec,
            pl.BlockSpec((None, C_KV, 2, HEAD_DIM), lambda i: (i, 0, 0, 0)),
            pl.BlockSpec((None, C_KV, HEAD_DIM, 2), lambda i: (i, 0, 0, 0)),
            full((C_KV, C_REP, w_len)), full((C_KV, C_REP, 128)), full((C_KV, C_REP, 128)),
        ],
        out_specs=[o_spec, cache_spec],
        out_shape=[jax.ShapeDtypeStruct((n, C_KV, C_REP, HEAD_DIM), F32),
                   jax.ShapeDtypeStruct((n, 2, C_KV, HEAD_DIM, w_len), F32)],
        compiler_params=pltpu.CompilerParams(
            dimension_semantics=("arbitrary",), vmem_limit_bytes=V7X_VMEM_LIMIT_BYTES),
        name="swa_sample",
    )(q, _rows_last(cache), new_kv, jnp.transpose(new_kv, (0, 1, 3, 2)), bias, lanes(tab[0]),
      lanes(sinks.reshape(C_KV, C_REP)))
    return o, _rows_first(cache_next)


PROJ_ROW_TILE = 512
A_IN_PAD = 1920
A_U_COL = A_Q + 6 * A_KVW
A_GATE_COL = A_U_COL + B_WIDTH


def _heads_first(kvt_ref, slot, k, v, n_kv):
    for g in range(n_kv):
        kvt_ref[slot + g] = k[:, g * HEAD_DIM:(g + 1) * HEAD_DIM].astype(BF16)
        kvt_ref[slot + n_kv + g] = v[:, g * HEAD_DIM:(g + 1) * HEAD_DIM].astype(BF16)


def _inproj_a_body(x_ref, gain_ref, w_ref, qg_ref, kg_ref,
                   q_ref, cmp_ref, sel_ref, win_ref, gate_ref, u_ref, kvt_ref):
    xn = _rms_rows(x_ref[...], gain_ref[...]).astype(BF16)
    z = jnp.dot(xn, w_ref[...], preferred_element_type=F32)
    for j in range(A_Q // 128):
        q_ref[:, j * 128:(j + 1) * 128] = _rms_heads128(z[:, j * 128:(j + 1) * 128], qg_ref[...])
    for out_ref, off, slot in ((cmp_ref, A_Q, None), (sel_ref, A_Q + 2 * A_KVW, 0), (win_ref, A_Q + 4 * A_KVW, 4)):
        k = _rms_heads128(z[:, off:off + A_KVW], kg_ref[...])
        v = z[:, off + A_KVW:off + 2 * A_KVW]
        out_ref[:, :A_KVW] = k
        out_ref[:, A_KVW:] = v
        if slot is not None:
            _heads_first(kvt_ref, slot, k, v, A_KV)
    u_ref[...] = z[:, A_U_COL:A_GATE_COL]
    gate_ref[...] = 1.0 / (1.0 + jnp.exp(-z[:, A_GATE_COL:A_IN_PAD]))


def _inproj_a(x, gain, w_in, q_gain, k_gain):
    b, t, d = x.shape
    tm = min(t, PROJ_ROW_TILE)
    w = jnp.concatenate([w_in[:, :A_U_COL], w_in[:, A_U_COL + A_GATE:], w_in[:, A_U_COL:A_U_COL + A_GATE],
                         jnp.zeros((d, A_IN_PAD - A_GATE_COL - A_GATE), F32)], axis=1).astype(BF16)
    tile2 = lambda g: jnp.tile(g, 2).reshape(1, 128)
    rows = lambda width: pl.BlockSpec((None, tm, width), lambda i, j: (i, j, 0))
    shape = lambda width: jax.ShapeDtypeStruct((b, t, width), F32)
    return pl.pallas_call(
        _inproj_a_body,
        grid=(b, t // tm),
        in_specs=[rows(d), _const_spec((1, d)), _const_spec((d, A_IN_PAD)), _const_spec((1, 128)), _const_spec((1, 128))],
        out_specs=[rows(A_Q), rows(2 * A_KVW), rows(2 * A_KVW), rows(2 * A_KVW), rows(128), rows(B_WIDTH),
                   pl.BlockSpec((None, 8, tm, HEAD_DIM), lambda i, j: (i, 0, j, 0))],
        out_shape=[shape(A_Q), shape(2 * A_KVW), shape(2 * A_KVW), shape(2 * A_KVW), shape(128), shape(B_WIDTH),
                   jax.ShapeDtypeStruct((b, 8, t, HEAD_DIM), BF16)],
        compiler_params=pltpu.CompilerParams(
            dimension_semantics=("parallel", "parallel"), vmem_limit_bytes=V7X_VMEM_LIMIT_BYTES),
        name="inproj_nsa_s5",
    )(x, gain.reshape(1, d), w, tile2(q_gain), tile2(k_gain))


def _inproj_c_body(x_ref, gain_ref, w_ref, qg_ref, kg_ref, q_ref, kv_ref, kvt_ref):
    xn = _rms_rows(x_ref[...], gain_ref[...]).astype(BF16)
    z = jnp.dot(xn, w_ref[...], preferred_element_type=F32)
    n_q = C_HEADS * HEAD_DIM
    for j in range(n_q // 128):
        q_ref[:, j * 128:(j + 1) * 128] = _rms_heads128(z[:, j * 128:(j + 1) * 128], qg_ref[...])
    k = _rms_heads128(z[:, n_q:n_q + C_KV * HEAD_DIM], kg_ref[...])
    v = z[:, n_q + C_KV * HEAD_DIM:]
    kv_ref[:, :C_KV * HEAD_DIM] = k
    kv_ref[:, C_KV * HEAD_DIM:] = v
    _heads_first(kvt_ref, 0, k, v, C_KV)


def _inproj_c(x, gain, w_in, q_gain, k_gain):
    b, t, d = x.shape
    tm = min(t, PROJ_ROW_TILE)
    n_in = w_in.shape[1]
    tile2 = lambda g: jnp.tile(g, 2).reshape(1, 128)
    rows = lambda width: pl.BlockSpec((None, tm, width), lambda i, j: (i, j, 0))
    shape = lambda width: jax.ShapeDtypeStruct((b, t, width), F32)
    return pl.pallas_call(
        _inproj_c_body,
        grid=(b, t // tm),
        in_specs=[rows(d), _const_spec((1, d)), _const_spec((d, n_in)), _const_spec((1, 128)), _const_spec((1, 128))],
        out_specs=[rows(C_HEADS * HEAD_DIM), rows(2 * C_KV * HEAD_DIM),
                   pl.BlockSpec((None, 2 * C_KV, tm, HEAD_DIM), lambda i, j: (i, 0, j, 0))],
        out_shape=[shape(C_HEADS * HEAD_DIM), shape(2 * C_KV * HEAD_DIM),
                   jax.ShapeDtypeStruct((b, 2 * C_KV, t, HEAD_DIM), BF16)],
        compiler_params=pltpu.CompilerParams(
            dimension_semantics=("parallel", "parallel"), vmem_limit_bytes=V7X_VMEM_LIMIT_BYTES),
        name="inproj_swa",
    )(x, gain.reshape(1, d), w_in.astype(BF16), tile2(q_gain), tile2(k_gain))


FFN_ROW_TILE = 512
FFN_COL_CHUNK = 1408


def _mixer_residual(y_ref, x_ref, mix_refs, wout_refs):
    y_ref[...] = x_ref[...]
    for m_ref, w_ref in zip(mix_refs, wout_refs):
        y_ref[...] += jnp.dot(m_ref[...].astype(BF16), w_ref[...], preferred_element_type=F32)
    return y_ref[...]


def _tail_prompt_body(*refs, n_mix, tm, ffc):
    x_ref = refs[0]
    mix_refs = refs[1:1 + n_mix]
    wout_refs = refs[1 + n_mix:1 + 2 * n_mix]
    (gain_ref, prev_ref, wup_ref, wgate_ref, cw_ref, cb_ref, wdown_ref, y_ref, cs_ref, hbuf_ref) = refs[1 + 2 * n_mix:]
    t = pl.program_id(1)
    xn = _rms_rows(_mixer_residual(y_ref, x_ref, mix_refs, wout_refs), gain_ref[...]).astype(BF16)
    for c in range(D_FF // ffc):
        lo = c * ffc
        h = jnp.dot(xn, wup_ref[:, lo:lo + ffc], preferred_element_type=F32)
        g = jnp.dot(xn, wgate_ref[:, lo:lo + ffc], preferred_element_type=F32)

        @pl.when(t == 0)
        def _():
            hbuf_ref[c, 6:8, :] = prev_ref[:, lo:lo + ffc]

        hbuf_ref[c, 8:8 + tm, :] = h
        hm1 = hbuf_ref[c, 7:7 + tm, :]
        hm2 = hbuf_ref[c, 6:6 + tm, :]
        cw = cw_ref[:, lo:lo + ffc]
        hc = cw[0:1] * hm2 + cw[1:2] * hm1 + cw[2:3] * h + cb_ref[:, lo:lo + ffc]
        a = (_gelu_tanh(hc) * g).astype(BF16)
        y_ref[...] += jnp.dot(a, wdown_ref[lo:lo + ffc, :], preferred_element_type=F32)
        hbuf_ref[c, 0:8, :] = h[tm - 8:tm, :]
        cs_ref[:, lo:lo + ffc] = h[tm - 2:tm, :]


def _tail_prompt(x, mixes, wouts, gain, prev, wup, wgate, cw, cb, wdown):
    b, t, d = x.shape
    tm, ffc = FFN_ROW_TILE, FFN_COL_CHUNK
    rows = lambda width: pl.BlockSpec((None, tm, width), lambda i, j: (i, j, 0))
    state = pl.BlockSpec((None, CONV_W - 1, D_FF), lambda i, j: (i, 0, 0))
    return pl.pallas_call(
        functools.partial(_tail_prompt_body, n_mix=len(mixes), tm=tm, ffc=ffc),
        grid=(b, t // tm),
        in_specs=[rows(d)] + [rows(m.shape[-1]) for m in mixes] + [_const_spec(w.shape) for w in wouts] + [
            _const_spec((1, d)), state, _const_spec((d, D_FF)), _const_spec((d, D_FF)),
            _const_spec((CONV_W, D_FF)), _const_spec((1, D_FF)), _const_spec((D_FF, d))],
        out_specs=[rows(d), state],
        out_shape=[jax.ShapeDtypeStruct((b, t, d), F32), jax.ShapeDtypeStruct((b, CONV_W - 1, D_FF), F32)],
        scratch_shapes=[pltpu.VMEM((D_FF // ffc, 8 + tm, ffc), F32)],
        compiler_params=pltpu.CompilerParams(
            dimension_semantics=("parallel", "arbitrary"), vmem_limit_bytes=V7X_VMEM_LIMIT_BYTES),
        name="tail_prompt",
    )(x, *mixes, *wouts, gain, prev, wup, wgate, cw, cb, wdown)


def _tail_sample_body(*refs, n_mix, ffc):
    x_ref = refs[0]
    mix_refs = refs[1:1 + n_mix]
    wout_refs = refs[1 + n_mix:1 + 2 * n_mix]
    (gain_ref, prev_ref, wup_ref, wgate_ref, cw_ref, cb_ref, wdown_ref, y_ref, cs_ref) = refs[1 + 2 * n_mix:]
    xn = _rms_rows(_mixer_residual(y_ref, x_ref, mix_refs, wout_refs), gain_ref[...]).astype(BF16)
    for c in range(D_FF // ffc):
        lo = c * ffc
        h = jnp.dot(xn, wup_ref[:, lo:lo + ffc], preferred_element_type=F32)
        g = jnp.dot(xn, wgate_ref[:, lo:lo + ffc], preferred_element_type=F32)
        hm2 = prev_ref[:, lo:lo + ffc]
        hm1 = prev_ref[:, D_FF + lo:D_FF + lo + ffc]
        cw = cw_ref[:, lo:lo + ffc]
        hc = cw[0:1] * hm2 + cw[1:2] * hm1 + cw[2:3] * h + cb_ref[:, lo:lo + ffc]
        a = (_gelu_tanh(hc) * g).astype(BF16)
        y_ref[...] += jnp.dot(a, wdown_ref[lo:lo + ffc, :], preferred_element_type=F32)
        cs_ref[:, lo:lo + ffc] = hm1
        cs_ref[:, D_FF + lo:D_FF + lo + ffc] = h


def _tail_sample(x, mixes, wouts, gain, prev, wup, wgate, cw, cb, wdown):
    n, d = x.shape
    full = lambda shape: _const_spec(shape)
    return pl.pallas_call(
        functools.partial(_tail_sample_body, n_mix=len(mixes), ffc=FFN_COL_CHUNK),
        grid=(1,),
        in_specs=[full((n, d))] + [full(m.shape) for m in mixes] + [full(w.shape) for w in wouts] + [
            full((1, d)), full((n, (CONV_W - 1) * D_FF)), full((d, D_FF)), full((d, D_FF)),
            full((CONV_W, D_FF)), full((1, D_FF)), full((D_FF, d))],
        out_specs=[pl.BlockSpec((n, d), lambda i: (0, 0)), pl.BlockSpec((n, (CONV_W - 1) * D_FF), lambda i: (0, 0))],
        out_shape=[jax.ShapeDtypeStruct((n, d), F32), jax.ShapeDtypeStruct((n, (CONV_W - 1) * D_FF), F32)],
        compiler_params=pltpu.CompilerParams(
            dimension_semantics=("arbitrary",), vmem_limit_bytes=V7X_VMEM_LIMIT_BYTES),
        name="tail_sample",
    )(x, *mixes, *wouts, gain, prev, wup, wgate, cw, cb, wdown)


def kernel(x_prompt, x_sample, cache_nsa_cmp, cache_nsa_sel, cache_nsa_win, state_s5_re, state_s5_im,
           cache_swa, state_ffn_conv, page_table, rel_bias, norm_mix, norm_ffn, a_w_in, a_w_out,
           nsa_q_gain, nsa_k_gain, nsa_cmp_pos, nsa_cmp_w1, nsa_cmp_w2, s5_a_re, s5_a_im, s5_log_dt,
           s5_b_re, s5_b_im, s5_c_re, s5_c_im, s5_d, s5_w_glu, s5_b_glu, c_w_in, c_w_out, c_q_gain,
           c_k_gain, c_sinks, ffn_w_up, ffn_w_gate, ffn_conv_w, ffn_conv_b, ffn_w_down):
    bp, tp, _ = x_prompt.shape
    bs, ts, _ = x_sample.shape
    assert ts == 1 and DEPTH == 2
    tab_a = rel_bias[:, :A_HEADS].reshape(NUM_BUCKETS, A_KV, A_REP)
    tab_c = rel_bias[:, :C_HEADS].reshape(NUM_BUCKETS, C_KV, C_REP)
    kv6 = lambda x: x.reshape(x.shape[:-1] + (2, x.shape[-1] // (2 * HEAD_DIM), HEAD_DIM))
    hp, hs = x_prompt, x_sample.reshape(1, bs, D_MODEL)
    conv_p, conv_s = [], []

    def tail(layer, hp, hs, mixes_p, mixes_s, wouts):
        wouts = [w.astype(BF16) for w in wouts]
        ffn = (norm_ffn[layer].reshape(1, D_MODEL),)
        wts = (ffn_w_up[layer].astype(BF16), ffn_w_gate[layer].astype(BF16), ffn_conv_w[layer],
               ffn_conv_b[layer].reshape(1, D_FF), ffn_w_down[layer].astype(BF16))
        hp, cp = _tail_prompt(hp, mixes_p, wouts, *ffn, jnp.zeros((bp, CONV_W - 1, D_FF), F32), *wts)
        hs2, cs = _tail_sample(hs[0], [m.reshape(bs, -1) for m in mixes_s], wouts, *ffn,
                               state_ffn_conv[layer].reshape(bs, (CONV_W - 1) * D_FF), *wts)
        conv_p.append(cp)
        conv_s.append(cs.reshape(bs, CONV_W - 1, D_FF))
        return hp, hs2.reshape(1, bs, D_MODEL)

    proj = (norm_mix[0], a_w_in[0], nsa_q_gain[0], nsa_k_gain[0])
    qp, cmp_p, sel_p, win_p, gate_p, up, kvt_p = _inproj_a(hp, *proj)
    qs, cmp_s, sel_s, win_s, gate_s, us, _ = _inproj_a(hs, *proj)
    cmp_w = _compress_weights(nsa_cmp_pos[0], nsa_cmp_w1[0], nsa_cmp_w2[0], nsa_k_gain[0])
    kc, vc = _cmp_prompt(cmp_p, cmp_w)
    o_ap = _nsa_prompt_attend(qp, gate_p, kc, vc, kvt_p, tab_a)
    o_as, nsa_win_s = _nsa_sample(qs.reshape(bs, A_KV, A_REP, HEAD_DIM),
                                  gate_s[0, :, :A_GATE].reshape(bs, A_KV, A_REP, 3),
                                  sel_s.reshape(bs, 2, A_KV, HEAD_DIM), win_s.reshape(bs, 2, A_KV, HEAD_DIM),
                                  cache_nsa_cmp[0], cache_nsa_sel[0], cache_nsa_win[0], page_table, cmp_w, tab_a)
    s5w = _s5_weights(s5_a_re[0], s5_a_im[0], s5_log_dt[0], s5_b_re[0], s5_b_im[0], s5_c_re[0], s5_c_im[0],
                      s5_d[0], s5_w_glu[0], s5_b_glu[0])
    h0 = jnp.zeros((bp, S5_NS), F32)
    o_bp, hr_p, hi_p = _s5_mix(up, h0, h0, s5w)
    o_bs, hr_s, hi_s = _s5_mix(us.reshape(bs, 1, B_WIDTH), state_s5_re[0].reshape(bs, S5_NS),
                               state_s5_im[0].reshape(bs, S5_NS), s5w)
    hp, hs = tail(0, hp, hs, [o_ap, o_bp], [o_as, o_bs], [a_w_out[0][:A_Q], a_w_out[0][A_Q:]])

    proj = (norm_mix[1], c_w_in[0], c_q_gain[0], c_k_gain[0])
    qcp, kv_p, kvt_c = _inproj_c(hp, *proj)
    qcs, kv_s, _ = _inproj_c(hs, *proj)
    o_cp = _swa_prompt(qcp, kvt_c, c_sinks[0], tab_c)
    new_kv = kv_s.reshape(bs, 2, C_KV, HEAD_DIM)
    o_cs, swa_s = _swa_sample(qcs.reshape(bs, C_KV, C_REP, HEAD_DIM), cache_swa[0],
                              jnp.transpose(new_kv, (0, 2, 1, 3)), c_sinks[0], tab_c)
    hp, hs = tail(1, hp, hs, [o_cp], [o_cs], [c_w_out[0]])

    state = lambda x, n: x.reshape(1, n, B_GROUPS, B_STATE)
    return (hp, hs.reshape(bs, ts, D_MODEL),
            kv6(cmp_p)[None], kv6(cmp_s).reshape(1, bs, ts, 2, A_KV, HEAD_DIM),
            kv6(sel_p)[None], kv6(sel_s).reshape(1, bs, ts, 2, A_KV, HEAD_DIM),
            kv6(win_p)[None, :, -min(WIN_A, tp):], nsa_win_s[None],
            state(hr_p, bp), state(hi_p, bp), state(hr_s, bs), state(hi_s, bs),
            kv6(kv_p)[None, :, -min(WIN_C, tp):], swa_s[None],
            jnp.stack(conv_p), jnp.stack(conv_s))
```

```python
import functools
import math

import jax
import jax.numpy as jnp
import numpy as np
from jax import lax
from jax.experimental import pallas as pl
from jax.experimental.pallas import tpu as pltpu

D_MODEL = 1024
DEPTH = 2
PAGE_SIZE = 128
HEAD_DIM = 64
A_HEADS = 8
A_KV = 2
A_REP = A_HEADS // A_KV
A_Q = A_HEADS * HEAD_DIM
A_KVW = A_KV * HEAD_DIM
A_GATE = 3 * A_HEADS
CMP_LEN = 32
CMP_STRIDE = 16
SEL_BLOCK = 64
SEL_TOPK = 16
WIN_A = 512
NSA_QBLK = 64
FORCE_BONUS = 1000.0
B_WIDTH = D_MODEL // 2
B_GROUP = 16
B_GROUPS = B_WIDTH // B_GROUP
B_STATE = 64
C_HEADS = D_MODEL // HEAD_DIM
C_KV = 2
C_REP = C_HEADS // C_KV
WIN_C = 128
NUM_BUCKETS = 32
MAX_DISTANCE = 128
D_FF = 2816
CONV_W = 3
EPS = 1e-6

F32 = jnp.float32
BF16 = jnp.bfloat16

V7X_VMEM_LIMIT_BYTES = 56 * 1024 * 1024


def _gelu_tanh(x):
    return 0.5 * x * (1.0 + jnp.tanh(math.sqrt(2.0 / math.pi) * (x + 0.044715 * (x * x * x))))


def _rms_rows(x, gain):
    return x * lax.rsqrt(jnp.mean(x * x, axis=-1, keepdims=True) + EPS) * gain


def _const_spec(shape):
    zeros = (0,) * len(shape)
    return pl.BlockSpec(shape, lambda *_: zeros, pipeline_mode=pl.Buffered(1))


NSA_Q_TILE = 128
NSA_FAR_TILE = 512
NSA_HEADS_PER_PASS = 4
NEG_INF = float("-inf")


def _dot_nt(a, b):
    return lax.dot_general(a, b, (((1,), (1,)), ((), ())), preferred_element_type=F32)


def _dot_tn(a, b):
    return lax.dot_general(a, b, (((0,), (0,)), ((), ())), preferred_element_type=F32)


def _softmax_start(s, v):
    m = jnp.max(s, axis=0, keepdims=True)
    e = jnp.exp(s - m)
    return m, jnp.sum(e, axis=0, keepdims=True), _dot_tn(v, e.astype(BF16))


def _softmax_more(carry, s, v):
    m, l, acc = carry
    m_new = jnp.maximum(m, jnp.max(s, axis=0, keepdims=True))
    alpha = jnp.exp(m - m_new)
    e = jnp.exp(s - m_new)
    return m_new, alpha * l + jnp.sum(e, axis=0, keepdims=True), alpha * acc + _dot_tn(v, e.astype(BF16))


def _nsa_prompt_body(q_ref, gate_ref, kc_ref, vc_ref, kvt_ref, btile_ref, cover_ref, o_ref, sel_scr, *, tq):
    width = A_REP * HEAD_DIM
    gates_t = gate_ref[...].T
    for g in range(A_KV):
        _nsa_prompt_group(g, q_ref.at[:, g * width:(g + 1) * width], gates_t, kc_ref.at[g], vc_ref.at[g],
                          kvt_ref.at[g], kvt_ref.at[A_KV + g], kvt_ref.at[2 * A_KV + g], kvt_ref.at[3 * A_KV + g],
                          btile_ref.at[g], cover_ref, o_ref.at[:, g * width:(g + 1) * width], sel_scr.at[g], tq)


def _nsa_prompt_group(g, q_ref, gates_t, kc_ref, vc_ref, ks_ref, vs_ref, kw_ref, vw_ref, btile_ref, cover_ref,
                      o_ref, sel_ref, tq):
    qt = pl.program_id(1)
    q0 = pl.multiple_of(qt * tq, tq)
    cols = A_REP * tq
    q = q_ref[...]
    qs = jnp.concatenate([q[:, r * HEAD_DIM:(r + 1) * HEAD_DIM] for r in range(A_REP)], axis=0)
    qs = (qs * (HEAD_DIM ** -0.5)).astype(BF16)

    def q_pos(height):
        return q0 + (lax.broadcasted_iota(jnp.int32, (height, cols), 1) & (tq - 1))

    def key_idx(height):
        return lax.broadcasted_iota(jnp.int32, (height, cols), 0)

    n_idx = key_idx(128)
    valid_c = (n_idx * CMP_STRIDE + (CMP_LEN - 1) <= q_pos(128)) & (n_idx < 127)
    s_c = jnp.where(valid_c, _dot_nt(kc_ref[...], qs), NEG_INF)
    m_c = jnp.max(s_c, axis=0, keepdims=True)
    m_c = jnp.where(m_c == NEG_INF, 0.0, m_c)
    e_c = jnp.exp(s_c - m_c)
    d_c = jnp.sum(e_c, axis=0, keepdims=True)
    p_c = (e_c / jnp.where(d_c > 0, d_c, 1.0)).astype(BF16)
    o_c = _dot_tn(vc_ref[...], p_c)
    imp_heads = jnp.dot(cover_ref[...], p_c, preferred_element_type=F32)
    imp = sum(imp_heads[0:32, r * tq:(r + 1) * tq] for r in range(A_REP))
    s_idx = lax.broadcasted_iota(jnp.int32, (32, tq), 0)
    qblk = (q0 + lax.broadcasted_iota(jnp.int32, (32, tq), 1)) >> 6
    forced = (s_idx == 0) | (s_idx == qblk) | (s_idx == qblk - 1)
    allowed = s_idx <= qblk
    score = jnp.where(allowed, imp + jnp.where(forced, FORCE_BONUS, 0.0), NEG_INF)
    rank = jnp.zeros((32, tq), F32)
    for j in range(32):
        other = score[j:j + 1, :]
        beats = (other > score) | ((other == score) & (s_idx > j))
        rank = rank + jnp.where(beats, 1.0, 0.0)
    sel = jnp.where((rank < SEL_TOPK) & allowed, 1.0, 0.0)
    hcols = NSA_HEADS_PER_PASS * tq
    far_end = q0 - tq
    sel_far_only = jnp.where(s_idx < far_end // SEL_BLOCK, sel, 0.0)
    sel_ref[...] = jnp.concatenate([jnp.concatenate([sel] * NSA_HEADS_PER_PASS, axis=1),
                                    jnp.concatenate([sel_far_only] * NSA_HEADS_PER_PASS, axis=1)], axis=0)

    def block_mask(k0, n_blocks, base=0):
        first = base + k0 // SEL_BLOCK
        return jnp.concatenate([jnp.broadcast_to(sel_ref[pl.ds(first + j, 1), :], (SEL_BLOCK, hcols))
                                for j in range(n_blocks)], axis=0) > 0.5

    def hq_pos(height):
        return q0 + (lax.broadcasted_iota(jnp.int32, (height, hcols), 1) & (tq - 1))

    def hkey_idx(height):
        return lax.broadcasted_iota(jnp.int32, (height, hcols), 0)

    prev0 = pl.multiple_of(jnp.maximum(q0 - tq, 0), tq)
    sel_near = jnp.concatenate([block_mask(prev0, tq // SEL_BLOCK), block_mask(q0, tq // SEL_BLOCK)], axis=0)
    n_far = (jnp.maximum(far_end, 0) + NSA_FAR_TILE - 1) // NSA_FAR_TILE
    w_far = WIN_A - tq
    wf0 = pl.multiple_of(jnp.maximum(q0 - WIN_A, 0), tq)
    wpos = wf0 + hkey_idx(w_far)
    wmask = (hq_pos(w_far) - wpos < WIN_A) & (wpos < far_end)

    outs = []
    for h0 in range(0, A_REP, NSA_HEADS_PER_PASS):
        csl = slice(h0 * tq, h0 * tq + hcols)
        qh = qs[h0 * tq:h0 * tq + hcols]
        btile = btile_ref[:, csl]

        def near(k_ref, v_ref, extra_mask):
            k = jnp.concatenate([k_ref[pl.ds(prev0, tq), :], k_ref[pl.ds(q0, tq), :]], axis=0)
            v = jnp.concatenate([v_ref[pl.ds(prev0, tq), :], v_ref[pl.ds(q0, tq), :]], axis=0)
            s = _dot_nt(k, qh) + btile
            return _softmax_start(s if extra_mask is None else jnp.where(extra_mask, s, NEG_INF), v)

        def sel_far(i, carry):
            k0 = pl.multiple_of(i * NSA_FAR_TILE, NSA_FAR_TILE)
            s = _dot_nt(ks_ref[pl.ds(k0, NSA_FAR_TILE), :], qh)
            mask = block_mask(k0, NSA_FAR_TILE // SEL_BLOCK, base=32)
            return _softmax_more(carry, jnp.where(mask, s, NEG_INF), vs_ref[pl.ds(k0, NSA_FAR_TILE), :])

        _, l_s, acc_s = lax.fori_loop(0, n_far, sel_far, near(ks_ref, vs_ref, sel_near))
        o_s = acc_s / l_s

        s_w = jnp.where(wmask, _dot_nt(kw_ref[pl.ds(wf0, w_far), :], qh), NEG_INF)
        _, l_w, acc_w = _softmax_more(near(kw_ref, vw_ref, None), s_w, vw_ref[pl.ds(wf0, w_far), :])
        o_w = acc_w / l_w

        for i in range(NSA_HEADS_PER_PASS):
            r = h0 + i
            sl = slice(i * tq, (i + 1) * tq)
            row = 3 * (g * A_REP + r)
            out_t = (gates_t[row:row + 1] * o_c[:, r * tq:(r + 1) * tq] + gates_t[row + 1:row + 2] * o_s[:, sl]
                     + gates_t[row + 2:row + 3] * o_w[:, sl])
            outs.append(out_t.T)
    o_ref[...] = jnp.concatenate(outs, axis=1)


def _bucket_np(dist):
    n = np.maximum(dist, 0)
    exact = NUM_BUCKETS // 2
    nf = np.maximum(n, exact).astype(np.float64)
    large = exact + (np.log(nf / exact) / math.log(MAX_DISTANCE / exact) * (NUM_BUCKETS - exact)).astype(np.int64)
    return np.where(n < exact, n, np.minimum(large, NUM_BUCKETS - 1)).astype(np.int32)


def _bias_lookup(tab, dist):
    bucket = _bucket_np(np.asarray(dist))
    onehot = (jnp.asarray(bucket.reshape(-1, 1)) == jnp.arange(NUM_BUCKETS)[None, :]).astype(F32)
    flat = jnp.dot(onehot, tab.reshape(NUM_BUCKETS, -1), precision=lax.Precision.HIGHEST)
    return flat.reshape(bucket.shape + tab.shape[1:])


def _near_bias_tiles(tab, tq, window):
    i = np.arange(tq)[:, None]
    j = np.arange(2 * tq)[None, :]
    dist = tq + i - j
    ok = (dist >= 0) & (dist < window)
    bias = jnp.transpose(_bias_lookup(tab, dist), (2, 3, 0, 1))
    tiles = jnp.stack([jnp.where(jnp.asarray(ok & (j >= tq)), bias, NEG_INF), jnp.where(jnp.asarray(ok), bias, NEG_INF)])
    return tiles.reshape(2, tab.shape[1], tab.shape[2] * tq, 2 * tq)


def _cover_matrix(n_cmp_pad, n_sel):
    n = np.arange(n_cmp_pad)
    c_start = n * CMP_STRIDE
    c_end = c_start + CMP_LEN - 1
    s_start = np.arange(128) * SEL_BLOCK
    cover = (c_start[:, None] < s_start[None, :] + SEL_BLOCK) & (c_end[:, None] >= s_start[None, :])
    cover &= (np.arange(128)[None, :] < n_sel)
    return np.tile(cover.astype(np.float32), (A_REP, 1))


def _nsa_prompt_attend(q, gates, kc, vc, kvt, tab):
    b, t, _ = q.shape
    tq = NSA_Q_TILE
    assert t % NSA_FAR_TILE == 0 and t // SEL_BLOCK <= 32 and t >= WIN_A and kc.shape[2] == 128
    btile = jnp.swapaxes(_near_bias_tiles(tab - tab[NUM_BUCKETS - 1], tq, 2 * tq), -1, -2)
    cover = jnp.asarray(_cover_matrix(128, t // SEL_BLOCK)[:128].T, BF16)
    cmp_spec = pl.BlockSpec((None, A_KV, 128, HEAD_DIM), lambda i, j: (i, 0, 0, 0))
    return pl.pallas_call(
        functools.partial(_nsa_prompt_body, tq=tq),
        grid=(b, t // tq),
        in_specs=[
            pl.BlockSpec((None, tq, A_Q), lambda i, j: (i, j, 0)),
            pl.BlockSpec((None, tq, 128), lambda i, j: (i, j, 0)),
            cmp_spec, cmp_spec,
            pl.BlockSpec((None, 4 * A_KV, t, HEAD_DIM), lambda i, j: (i, 0, 0, 0)),
            pl.BlockSpec((None, A_KV, 2 * tq, A_REP * tq), lambda i, j: (jnp.minimum(j, 1), 0, 0, 0)),
            pl.BlockSpec((128, 128), lambda i, j: (0, 0)),
        ],
        out_specs=pl.BlockSpec((None, tq, A_Q), lambda i, j: (i, j, 0)),
        out_shape=jax.ShapeDtypeStruct((b, t, A_Q), F32),
        scratch_shapes=[pltpu.VMEM((A_KV, 2 * 32, NSA_HEADS_PER_PASS * tq), F32)],
        compiler_params=pltpu.CompilerParams(
            dimension_semantics=("parallel", "arbitrary"), vmem_limit_bytes=V7X_VMEM_LIMIT_BYTES),
        name="nsa_prompt",
    )(q, gates, kc, vc, kvt, btile, cover)


S5_NS = B_GROUPS * B_STATE
S5_T_CHUNK = 64
S5_STRIP = 512
S5_HALVES = 2


def _s5_body(u_ref, h0r_ref, h0i_ref, ar_ref, ai_ref, ldt_ref, wb_ref, wc_ref, d_ref, wglu_ref, bglu_ref,
             o_ref, hr_ref, hi_ref, coef_ref, st_ref, xbuf_ref, ubuf_ref, obuf_ref, *, nb, steps, interleave):
    c = pl.program_id(0)

    @pl.when(c == 0)
    def _():
        dt = jnp.exp(ldt_ref[...])
        ar, ai = ar_ref[...], ai_ref[...]
        mag = jnp.exp(ar * dt)
        abr, abi = mag * jnp.cos(ai * dt), mag * jnp.sin(ai * dt)
        den = ar * ar + ai * ai
        wr = ((abr - 1.0) * ar + abi * ai) / den
        wi = (abi * ar - (abr - 1.0) * ai) / den
        for k, val in enumerate((abr, abi, wr, wi)):
            coef_ref[k] = jnp.broadcast_to(val, (nb, S5_NS))
        st_ref[0] = h0r_ref[...]
        st_ref[1] = h0i_ref[...]

    if interleave:
        for b in range(nb):
            for j in range(B_WIDTH // 128):
                ubuf_ref.at[j][pl.ds(b, steps, stride=nb), :] = u_ref[b, :, j * 128:(j + 1) * 128]
        u = jnp.concatenate([ubuf_ref[j] for j in range(B_WIDTH // 128)], axis=1)
    else:
        u = u_ref[...]
    hc, hs = B_WIDTH // S5_HALVES, S5_NS // S5_HALVES
    ub = u.astype(BF16)
    for h in range(S5_HALVES):
        bu = jnp.dot(ub[:, h * hc:(h + 1) * hc], wb_ref[h], preferred_element_type=F32)
        xbuf_ref[:, h * hs:(h + 1) * hs] = bu[:, :hs]
        xbuf_ref[:, S5_NS + h * hs:S5_NS + (h + 1) * hs] = bu[:, hs:]

    for lo in range(0, S5_NS, S5_STRIP):
        re = slice(lo, lo + S5_STRIP)
        im = slice(S5_NS + lo, S5_NS + lo + S5_STRIP)
        abr, abi, wr, wi = (coef_ref[k, :, re] for k in range(4))

        def step(t, carry):
            sr, si = carry
            r0 = pl.multiple_of(t * nb, nb)
            bur = xbuf_ref[pl.ds(r0, nb), re]
            bui = xbuf_ref[pl.ds(r0, nb), im]
            nsr = abr * sr - abi * si + (wr * bur - wi * bui)
            nsi = abr * si + abi * sr + (wr * bui + wi * bur)
            xbuf_ref[pl.ds(r0, nb), re] = nsr
            xbuf_ref[pl.ds(r0, nb), im] = nsi
            return nsr, nsi

        sr, si = lax.fori_loop(0, steps, step, (st_ref[0, :, re], st_ref[1, :, re]),
                               unroll=min(steps, 8))
        st_ref[0, :, re] = sr
        st_ref[1, :, re] = si

    y = jnp.concatenate(
        [jnp.dot(jnp.concatenate([xbuf_ref[:, h * hs:(h + 1) * hs], xbuf_ref[:, S5_NS + h * hs:S5_NS + (h + 1) * hs]],
                                 axis=1).astype(BF16), wc_ref[h], preferred_element_type=F32)
         for h in range(S5_HALVES)], axis=1) + d_ref[...] * u
    z = _gelu_tanh(y)
    gate = jnp.dot(z.astype(BF16), wglu_ref[...], preferred_element_type=F32) + bglu_ref[...]
    out = z * (1.0 / (1.0 + jnp.exp(-gate)))
    if interleave:
        for j in range(B_WIDTH // 128):
            obuf_ref[j] = out[:, j * 128:(j + 1) * 128]
        for b in range(nb):
            for j in range(B_WIDTH // 128):
                o_ref[b, :, j * 128:(j + 1) * 128] = obuf_ref.at[j][pl.ds(b, steps, stride=nb), :]
    else:
        o_ref[...] = out
    hr_ref[...] = st_ref[0]
    hi_ref[...] = st_ref[1]


def _s5_weights(a_re, a_im, log_dt, b_re, b_im, c_re, c_im, d, w_glu, b_glu):
    eye = jnp.eye(B_GROUPS, dtype=F32)
    blk_in = lambda w: jnp.einsum('hg,gpc->hcgp', eye, w).reshape(B_WIDTH, S5_NS)
    blk_out = lambda w: jnp.einsum('gh,gcp->gphc', eye, w).reshape(S5_NS, B_WIDTH)
    hc, hs = B_WIDTH // S5_HALVES, S5_NS // S5_HALVES
    diag = lambda w, h: w[h * hc:(h + 1) * hc, h * hs:(h + 1) * hs]
    wb = jnp.stack([jnp.concatenate([diag(blk_in(b_re), h), diag(blk_in(b_im), h)], axis=1)
                    for h in range(S5_HALVES)]).astype(BF16)
    diag_t = lambda w, h: w[h * hs:(h + 1) * hs, h * hc:(h + 1) * hc]
    wc = jnp.stack([jnp.concatenate([diag_t(blk_out(c_re), h), -diag_t(blk_out(c_im), h)], axis=0)
                    for h in range(S5_HALVES)]).astype(BF16)
    flat = lambda x: x.reshape(1, S5_NS)
    return (flat(a_re), flat(a_im), flat(jnp.repeat(log_dt, B_STATE)), wb, wc, d.reshape(1, B_WIDTH),
            w_glu.astype(BF16), b_glu.reshape(1, B_WIDTH))


def _s5_mix(u, h_re, h_im, weights):
    nb, t, _ = u.shape
    interleave = t > 1
    steps = min(t, S5_T_CHUNK)
    rows = nb * steps
    body = functools.partial(_s5_body, nb=nb, steps=steps, interleave=interleave)
    if interleave:
        u_in = u
        u_spec = pl.BlockSpec((nb, steps, B_WIDTH), lambda c: (0, c, 0))
        o_shape = jax.ShapeDtypeStruct((nb, t, B_WIDTH), F32)
        scratch_rows = rows
    else:
        u_in = u.reshape(nb, B_WIDTH)
        u_spec = pl.BlockSpec((nb, B_WIDTH), lambda c: (0, 0))
        o_shape = jax.ShapeDtypeStruct((nb, B_WIDTH), F32)
        scratch_rows = 8
    o, hr, hi = pl.pallas_call(
        body,
        grid=(t // steps,),
        in_specs=[
            u_spec, _const_spec((nb, S5_NS)), _const_spec((nb, S5_NS)),
            _const_spec((1, S5_NS)), _const_spec((1, S5_NS)), _const_spec((1, S5_NS)),
            _const_spec((S5_HALVES, B_WIDTH // S5_HALVES, 2 * S5_NS // S5_HALVES)),
            _const_spec((S5_HALVES, 2 * S5_NS // S5_HALVES, B_WIDTH // S5_HALVES)),
            _const_spec((1, B_WIDTH)), _const_spec((B_WIDTH, B_WIDTH)), _const_spec((1, B_WIDTH)),
        ],
        out_specs=[u_spec, pl.BlockSpec((nb, S5_NS), lambda c: (0, 0)), pl.BlockSpec((nb, S5_NS), lambda c: (0, 0))],
        out_shape=[o_shape, jax.ShapeDtypeStruct((nb, S5_NS), F32), jax.ShapeDtypeStruct((nb, S5_NS), F32)],
        scratch_shapes=[
            pltpu.VMEM((4, nb, S5_NS), F32),
            pltpu.VMEM((2, nb, S5_NS), F32),
            pltpu.VMEM((rows, 2 * S5_NS), F32),
            pltpu.VMEM((B_WIDTH // 128, scratch_rows, 128), F32),
            pltpu.VMEM((B_WIDTH // 128, scratch_rows, 128), F32),
        ],
        compiler_params=pltpu.CompilerParams(
            dimension_semantics=("arbitrary",),
            vmem_limit_bytes=V7X_VMEM_LIMIT_BYTES),
        name="s5_mix",
    )(u_in, h_re, h_im, *weights)
    return o.reshape(nb, t, B_WIDTH), hr, hi


CHUNK_W = CMP_STRIDE * 2 * A_KVW
CHUNKS_PER_PAGE = PAGE_SIZE // CMP_STRIDE
SEL_PAD = 256


def _bf16_round(x):
    return x.astype(BF16).astype(F32)


def _compress_tail(c, pos, w2):
    c1 = c[:, 128:]
    nxt = jnp.concatenate([c1[1:], c1[:1]], axis=0)
    hid = (pos + c[:, :128]) + nxt
    return jnp.dot(_gelu_tanh(hid).astype(BF16), w2, preferred_element_type=F32)


def _rms_heads128(x, gain):
    left = lax.broadcasted_iota(jnp.int32, x.shape, 1) < HEAD_DIM
    sq = x * x
    s0 = jnp.sum(jnp.where(left, sq, 0.0), axis=-1, keepdims=True)
    s1 = jnp.sum(jnp.where(left, 0.0, sq), axis=-1, keepdims=True)
    ms = jnp.where(left, s0, s1) * (1.0 / HEAD_DIM)
    return x * lax.rsqrt(ms + EPS) * gain


def _pad_rows8(x):
    return jnp.concatenate([x, jnp.zeros((8 - x.shape[0], x.shape[1]), x.dtype)], axis=0)


def _compress_mlp(row_of_chunks, n_chunks, w_ref, pos, w2_ref):
    acc = jnp.zeros((n_chunks, 2 * A_KVW), F32)
    for sp in range(CMP_STRIDE // 2):
        x = jnp.concatenate([row_of_chunks(2 * sp + e) for e in range(2)], axis=1).astype(BF16)
        acc = acc + jnp.dot(x, w_ref[sp], preferred_element_type=F32)
    return _compress_tail(acc, pos, w2_ref[...])


def _compress_rows(rows_scr, kv, n_chunks, w_ref, pos, w2_ref):
    return _compress_mlp(lambda s: rows_scr.at[kv][pl.ds(s, n_chunks, stride=CMP_STRIDE), :],
                         n_chunks, w_ref, pos, w2_ref)


def _compress_weights(cmp_pos, cmp_w1, cmp_w2, k_gain):
    eye = jnp.eye(A_KV, dtype=F32)
    w1 = cmp_w1.reshape(2, 2, CMP_STRIDE, HEAD_DIM, HEAD_DIM)
    w_big = jnp.einsum('ajsdh,gk->asgdjkh', w1, eye).reshape(2, CMP_STRIDE // 2, 2 * A_KVW, 2 * A_KVW).astype(BF16)
    w_pos = jnp.concatenate([cmp_w1, cmp_w1], axis=-1).astype(BF16)
    pos = jnp.broadcast_to(cmp_pos.reshape(2, 1, CMP_LEN * HEAD_DIM), (2, 8, CMP_LEN * HEAD_DIM)).astype(BF16)
    w2_big = jnp.einsum('ahd,gk->aghkd', cmp_w2, eye).reshape(2, A_KVW, A_KVW).astype(BF16)
    gain2 = jnp.tile(k_gain, A_KV).reshape(1, A_KVW)
    return (pos[0], pos[1], w_pos[0], w_pos[1], w_big[0], w_big[1], w2_big[0], w2_big[1], gain2)


def _compress_specs(full):
    mlp_w = (CMP_STRIDE // 2, 2 * A_KVW, 2 * A_KVW)
    return [full((8, CMP_LEN * HEAD_DIM)), full((8, CMP_LEN * HEAD_DIM)),
            full((CMP_LEN * HEAD_DIM, A_KVW)), full((CMP_LEN * HEAD_DIM, A_KVW)),
            full(mlp_w), full(mlp_w), full((A_KVW, A_KVW)), full((A_KVW, A_KVW)), full((1, A_KVW))]


def _pos_terms(pos_scr, posk_ref, posv_ref, wpos_k_ref, wpos_v_ref):
    pos_scr[0] = jnp.dot(posk_ref[...], wpos_k_ref[...], preferred_element_type=F32)
    pos_scr[1] = jnp.dot(posv_ref[...], wpos_v_ref[...], preferred_element_type=F32)


def _cmp_prompt_body(cmp_ref, posk_ref, posv_ref, wpos_k_ref, wpos_v_ref, wk_ref, wv_ref, w2k_ref, w2v_ref,
                     kgain_ref, kc_ref, vc_ref, pos_scr, rows_scr, *, n_chunks):
    @pl.when(pl.program_id(0) == 0)
    def _():
        _pos_terms(pos_scr, posk_ref, posv_ref, wpos_k_ref, wpos_v_ref)

    rows_scr[0] = cmp_ref[:, :A_KVW]
    rows_scr[1] = cmp_ref[:, A_KVW:]
    kc = _rms_heads128(_compress_rows(rows_scr, 0, n_chunks, wk_ref, pos_scr[0, 0:1], w2k_ref),
                       kgain_ref[...]).astype(BF16)
    vc = _compress_rows(rows_scr, 1, n_chunks, wv_ref, pos_scr[1, 0:1], w2v_ref).astype(BF16)
    for g in range(A_KV):
        kc_ref[g] = kc[:, g * HEAD_DIM:(g + 1) * HEAD_DIM]
        vc_ref[g] = vc[:, g * HEAD_DIM:(g + 1) * HEAD_DIM]


def _cmp_prompt(cmp_rows, cmp_weights):
    b, t, _ = cmp_rows.shape
    n_chunks = t // CMP_STRIDE
    full = lambda shape: pl.BlockSpec(shape, lambda i: (0,) * len(shape))
    out_spec = pl.BlockSpec((None, A_KV, n_chunks, HEAD_DIM), lambda i: (i, 0, 0, 0))
    out_shape = jax.ShapeDtypeStruct((b, A_KV, n_chunks, HEAD_DIM), BF16)
    return pl.pallas_call(
        functools.partial(_cmp_prompt_body, n_chunks=n_chunks),
        grid=(b,),
        in_specs=[pl.BlockSpec((None, t, 2 * A_KVW), lambda i: (i, 0, 0))] + _compress_specs(full),
        out_specs=[out_spec, out_spec],
        out_shape=[out_shape, out_shape],
        scratch_shapes=[pltpu.VMEM((2, 8, A_KVW), F32), pltpu.VMEM((2, t, A_KVW), F32)],
        compiler_params=pltpu.CompilerParams(
            dimension_semantics=("arbitrary",), vmem_limit_bytes=V7X_VMEM_LIMIT_BYTES),
        name="nsa_cmp_prompt",
    )(cmp_rows, *cmp_weights)


def _nsa_sample_cmp_body(pt_ref, *refs, n_pages):
    page_refs = refs[:n_pages]
    (posk_ref, posv_ref, wpos_k_ref, wpos_v_ref, wk_ref, wv_ref, w2k_ref, w2v_ref, kgain_ref,
     kc_ref, vc_ref, pos_scr, rows_scr) = refs[n_pages:]

    @pl.when(pl.program_id(0) == 0)
    def _():
        _pos_terms(pos_scr, posk_ref, posv_ref, wpos_k_ref, wpos_v_ref)

    n_chunks = n_pages * CHUNKS_PER_PAGE
    for kv in range(2):
        for p in range(n_pages):
            rows = page_refs[p][kv].reshape(A_KVW, PAGE_SIZE).T
            by_s = pltpu.einshape("csl->scl", rows.reshape(CHUNKS_PER_PAGE, CMP_STRIDE, A_KVW))
            for s in range(CMP_STRIDE):
                rows_scr[kv, s, p * CHUNKS_PER_PAGE:(p + 1) * CHUNKS_PER_PAGE, :] = by_s[s]

    compress = lambda kv, w_ref, w2_ref: _compress_mlp(lambda s: rows_scr[kv, s], n_chunks, w_ref,
                                                       pos_scr[kv, 0:1], w2_ref)
    kc_ref[...] = _rms_heads128(compress(0, wk_ref, w2k_ref), kgain_ref[...]).astype(BF16)
    vc_ref[...] = compress(1, wv_ref, w2v_ref).astype(BF16)


NSA_PICK_BATCH = 8


def _nsa_sample_pick_body(q_ref, kc_ref, vc_ref, cover_ref, oc_ref, idx_ref, *, n_cmp, n_sel):
    nb, n_chunks = kc_ref.shape[0], kc_ref.shape[1]
    col = lax.broadcasted_iota(jnp.int32, (8, n_chunks), 1)
    imps = []
    for i in range(nb):
        for g in range(A_KV):
            hs = slice(g * HEAD_DIM, (g + 1) * HEAD_DIM)
            q8 = (_pad_rows8(q_ref[i, g]) * (HEAD_DIM ** -0.5)).astype(BF16)
            s = jnp.where(col < n_cmp, _dot_nt(q8, kc_ref[i, :, hs]), NEG_INF)
            e = jnp.exp(s - jnp.max(s, axis=-1, keepdims=True))
            p = (e / jnp.sum(e, axis=-1, keepdims=True)).astype(BF16)
            oc_ref[i, g] = jnp.dot(p, vc_ref[i, :, hs], preferred_element_type=F32)[0:A_REP]
            imps.append(jnp.sum(jnp.dot(p, cover_ref[...], preferred_element_type=F32)[0:A_REP], axis=0, keepdims=True))
    imp = jnp.concatenate(imps, axis=0)
    rows = nb * A_KV
    s_idx = lax.broadcasted_iota(jnp.int32, (rows, SEL_PAD), 1)
    s_idx_f = s_idx.astype(F32)
    forced = (s_idx == 0) | (s_idx == n_sel - 1) | (s_idx == n_sel - 2)
    score = jnp.where(s_idx < n_sel, imp + jnp.where(forced, FORCE_BONUS, 0.0), NEG_INF)
    rank = jnp.zeros((rows, SEL_PAD), F32)
    for j in range(n_sel):
        cj = score[:, j:j + 1]
        beats = (cj > score) | ((cj == score) & (s_idx > j))
        rank = rank + jnp.where(beats, 1.0, 0.0)
    lane = lax.broadcasted_iota(jnp.int32, (rows, 128), 1)
    picks = jnp.zeros((rows, 128), F32)
    for r in range(SEL_TOPK):
        block = jnp.sum(jnp.where(rank == float(r), s_idx_f, 0.0), axis=-1, keepdims=True)
        picks = picks + jnp.where(lane == r, block, 0.0)
    idx_ref[...] = picks.astype(jnp.int32)


NSA_ATT_BATCH = 2


def _nsa_sample_att_body(idx_ref, pt_ref, *refs, n_past_blk):
    n_slots = NSA_ATT_BATCH * A_KV * SEL_TOPK
    slot_refs = refs[:n_slots]
    (q_all, gate_all, oc_all, newsel_all, wcache_all, newwin_all, newcol_all, selb_all, winb_all, newb_all,
     o_all, wout_all) = refs[n_slots:]
    for i in range(NSA_ATT_BATCH):
        b = pl.program_id(0) * NSA_ATT_BATCH + i
        for g in range(A_KV):
            first = (i * A_KV + g) * SEL_TOPK
            _nsa_sample_att_group(b, g, idx_ref, slot_refs[first:first + SEL_TOPK], q_all.at[i, g], gate_all.at[i, g],
                                  oc_all.at[i, g], newsel_all.at[i, g], wcache_all.at[i, :, g], newwin_all.at[i, g],
                                  selb_all.at[g], winb_all.at[g], newb_all.at[g], o_all.at[i, g], n_past_blk)
            for kv in range(2):
                wout_all[i, kv, g] = _shift_in(wcache_all[i, kv, g], newcol_all[i, g][:, kv:kv + 1])


def _shift_in(rows_last, new_col):
    w_len = rows_last.shape[-1]
    lane = lax.broadcasted_iota(jnp.int32, rows_last.shape, 1)
    return jnp.where(lane == w_len - 1, new_col, pltpu.roll(rows_last, w_len - 1, axis=1))


def _nsa_sample_att_group(b, g, idx_ref, page_refs, q_ref, gate_ref, oc_ref, newsel_ref, wcache_ref, newwin_ref,
                          selb_ref, winb_ref, newb_ref, o_ref, n_past_blk):
    blocks_per_page = PAGE_SIZE // SEL_BLOCK
    q8 = (_pad_rows8(q_ref[...]) * (HEAD_DIM ** -0.5)).astype(BF16)
    new_bias = newb_ref[:, 0:1]

    def attend(s, v_t, new_kv):
        k_new = _bf16_round(new_kv[0:1])
        v_new = _bf16_round(new_kv[1:2])
        s_new = jnp.sum(q8.astype(F32) * k_new, axis=-1, keepdims=True) + new_bias
        m = jnp.maximum(jnp.max(s, axis=-1, keepdims=True), s_new)
        e = jnp.exp(s - m)
        e_new = jnp.exp(s_new - m)
        den = jnp.sum(e, axis=-1, keepdims=True) + e_new
        return _dot_nt((e / den).astype(BF16), v_t) + _bf16_round(e_new / den) * v_new

    lane = lax.broadcasted_iota(jnp.int32, (8, PAGE_SIZE), 1)
    near = selb_ref[...]
    bias = []
    for k in range(SEL_TOPK):
        ik = idx_ref[(b * A_KV + g) * SEL_TOPK + k]
        blk = jnp.minimum(ik, n_past_blk - 1)
        near_k = jnp.where(blk // blocks_per_page == (n_past_blk - 1) // blocks_per_page, near, 0.0)
        keep = (lane // SEL_BLOCK == blk % blocks_per_page) & (ik < n_past_blk)
        bias.append(jnp.where(keep, near_k, NEG_INF))
    k_sel = jnp.concatenate([page_refs[k][0].astype(BF16) for k in range(SEL_TOPK)], axis=1)
    v_sel = jnp.concatenate([page_refs[k][1].astype(BF16) for k in range(SEL_TOPK)], axis=1)
    s_sel = jnp.dot(q8, k_sel, preferred_element_type=F32) + jnp.concatenate(bias, axis=1)
    o_s = attend(s_sel, v_sel, newsel_ref[...])

    w_len = wcache_ref.shape[-1]
    wcol = lax.broadcasted_iota(jnp.int32, (8, w_len), 1)
    s_w = jnp.dot(q8, wcache_ref[0].astype(BF16), preferred_element_type=F32) + winb_ref[...]
    o_w = attend(jnp.where(wcol >= 1, s_w, NEG_INF), wcache_ref[1].astype(BF16), newwin_ref[...])
    gates = gate_ref[...]
    o_ref[...] = gates[:, 0:1] * oc_ref[...] + gates[:, 1:2] * o_s[0:A_REP] + gates[:, 2:3] * o_w[0:A_REP]


def _rows_last(cache):
    nd = cache.ndim
    return jnp.transpose(cache, tuple(range(nd - 4)) + (nd - 3, nd - 2, nd - 1, nd - 4))


def _rows_first(cache):
    nd = cache.ndim
    return jnp.transpose(cache, tuple(range(nd - 4)) + (nd - 1, nd - 4, nd - 3, nd - 2))


def _nsa_sample(q, gates, new_sel, new_win, pool_cmp, pool_sel, win_cache, page_table, cmp_weights, tab):
    n, n_pages = page_table.shape
    past = n_pages * PAGE_SIZE
    n_past_blk = past // SEL_BLOCK
    n_sel = n_past_blk + 1
    n_chunks = n_pages * CHUNKS_PER_PAGE
    w_len = win_cache.shape[1]
    blocks_per_page = PAGE_SIZE // SEL_BLOCK
    assert n_sel <= SEL_PAD and w_len == WIN_A and past >= WIN_A and blocks_per_page == 2
    pt_flat = page_table.reshape(-1)

    c_idx = np.arange(n_chunks)
    s_start = np.arange(SEL_PAD) * SEL_BLOCK
    cover = ((c_idx[:, None] * CMP_STRIDE < s_start[None, :] + SEL_BLOCK)
             & (c_idx[:, None] * CMP_STRIDE + CMP_LEN - 1 >= s_start[None, :])
             & (c_idx[:, None] < n_chunks - 1) & (np.arange(SEL_PAD)[None, :] < n_sel))
    cover = jnp.asarray(cover.astype(np.float32), BF16)

    head_spec = lambda last: pl.BlockSpec((None, A_KV, A_REP, last), lambda i, *_: (i, 0, 0, 0))
    page_specs = [pl.BlockSpec((None, 2, A_KV, HEAD_DIM, PAGE_SIZE),
                               functools.partial(lambda i, pt, p: (pt[i * n_pages + p], 0, 0, 0, 0), p=p))
                  for p in range(n_pages)]
    full = lambda shape: pl.BlockSpec(shape, lambda i, *_: (0,) * len(shape))
    pool_c = _rows_last(pool_cmp)
    cmp_spec = pl.BlockSpec((None, n_chunks, A_KVW), lambda i, *_: (i, 0, 0))
    cmp_shape = jax.ShapeDtypeStruct((n, n_chunks, A_KVW), BF16)
    kc, vc = pl.pallas_call(
        functools.partial(_nsa_sample_cmp_body, n_pages=n_pages),
        grid_spec=pltpu.PrefetchScalarGridSpec(
            num_scalar_prefetch=1,
            grid=(n,),
            in_specs=page_specs + _compress_specs(full),
            out_specs=[cmp_spec, cmp_spec],
            scratch_shapes=[pltpu.VMEM((2, 8, A_KVW), F32), pltpu.VMEM((2, CMP_STRIDE, n_chunks, A_KVW), F32)],
        ),
        out_shape=[cmp_shape, cmp_shape],
        compiler_params=pltpu.CompilerParams(
            dimension_semantics=("arbitrary",), vmem_limit_bytes=V7X_VMEM_LIMIT_BYTES),
        name="nsa_sample_cmp",
    )(pt_flat, *([pool_c] * n_pages), *cmp_weights)

    nb = NSA_PICK_BATCH
    assert n % nb == 0
    batch = lambda *tail: pl.BlockSpec((nb,) + tail, lambda i: (i,) + (0,) * len(tail))
    o_c, idx = pl.pallas_call(
        functools.partial(_nsa_sample_pick_body, n_cmp=n_chunks - 1, n_sel=n_sel),
        grid=(n // nb,),
        in_specs=[batch(A_KV, A_REP, HEAD_DIM), batch(n_chunks, A_KVW), batch(n_chunks, A_KVW),
                  pl.BlockSpec((n_chunks, SEL_PAD), lambda i: (0, 0))],
        out_specs=[batch(A_KV, A_REP, HEAD_DIM), pl.BlockSpec((nb * A_KV, 128), lambda i: (i, 0))],
        out_shape=[jax.ShapeDtypeStruct((n, A_KV, A_REP, HEAD_DIM), F32),
                   jax.ShapeDtypeStruct((n * A_KV, 128), jnp.int32)],
        compiler_params=pltpu.CompilerParams(
            dimension_semantics=("arbitrary",), vmem_limit_bytes=V7X_VMEM_LIMIT_BYTES),
        name="nsa_sample_pick",
    )(q, kc, vc, cover)

    rel = lambda dist: jnp.pad(jnp.transpose(_bias_lookup(tab - tab[NUM_BUCKETS - 1], dist), (1, 2, 0)),
                               ((0, 0), (0, 8 - A_REP), (0, 0)))
    sel_bias = rel(past - (past - PAGE_SIZE + np.arange(PAGE_SIZE)))
    win_bias = rel(w_len - np.arange(w_len))
    new_bias = rel(np.zeros(128, np.int64))

    idx_flat = idx[:, :SEL_TOPK].reshape(-1)

    nba = NSA_ATT_BATCH
    assert n % nba == 0

    def page_map(i, idx_s, pt, seq, g, slot):
        b = i * nba + seq
        blk = jnp.minimum(idx_s[(b * A_KV + g) * SEL_TOPK + slot], n_past_blk - 1)
        return (pt[b * n_pages + blk // blocks_per_page], 0, g, 0, 0)

    slot_specs = [pl.BlockSpec((None, 2, None, HEAD_DIM, PAGE_SIZE), functools.partial(page_map, seq=e, g=g, slot=s))
                  for e in range(nba) for g in range(A_KV) for s in range(SEL_TOPK)]
    per_head = lambda rows, last: pl.BlockSpec((nba, A_KV, rows, last), lambda i, *_: (i, 0, 0, 0))
    per_group = lambda last: pl.BlockSpec((A_KV, 8, last), lambda i, *_: (0, 0, 0))
    win_spec = pl.BlockSpec((nba, 2, A_KV, HEAD_DIM, w_len), lambda i, *_: (i, 0, 0, 0, 0))
    o, win_next = pl.pallas_call(
        functools.partial(_nsa_sample_att_body, n_past_blk=n_past_blk),
        grid_spec=pltpu.PrefetchScalarGridSpec(
            num_scalar_prefetch=2,
            grid=(n // nba,),
            in_specs=slot_specs + [
                per_head(A_REP, HEAD_DIM), per_head(A_REP, 3), per_head(A_REP, HEAD_DIM), per_head(2, HEAD_DIM),
                win_spec, per_head(2, HEAD_DIM), per_head(HEAD_DIM, 2),
                per_group(PAGE_SIZE), per_group(w_len), per_group(128),
            ],
            out_specs=[per_head(A_REP, HEAD_DIM), win_spec],
        ),
        out_shape=[jax.ShapeDtypeStruct((n, A_KV, A_REP, HEAD_DIM), F32),
                   jax.ShapeDtypeStruct((n, 2, A_KV, HEAD_DIM, w_len), F32)],
        compiler_params=pltpu.CompilerParams(
            dimension_semantics=("arbitrary",), vmem_limit_bytes=V7X_VMEM_LIMIT_BYTES),
        name="nsa_sample_att",
    )(idx_flat, pt_flat, *([_rows_last(pool_sel)] * (nba * A_KV * SEL_TOPK)), q, gates, o_c,
      jnp.transpose(new_sel, (0, 2, 1, 3)), _rows_last(win_cache), jnp.transpose(new_win, (0, 2, 1, 3)),
      jnp.transpose(new_win, (0, 2, 3, 1)), sel_bias, win_bias, new_bias)
    return o, _rows_first(win_next)


SWA_Q_TILE = WIN_C


def _swa_prompt_body(q_ref, kv_ref, btile_ref, sink_ref, o_ref, *, tq):
    qt = pl.program_id(1)
    q0 = pl.multiple_of(qt * tq, tq)
    prev0 = pl.multiple_of(jnp.maximum(q0 - tq, 0), tq)
    width = C_REP * HEAD_DIM
    for g in range(C_KV):
        q = q_ref[:, g * width:(g + 1) * width]
        qs = jnp.concatenate([q[:, r * HEAD_DIM:(r + 1) * HEAD_DIM] for r in range(C_REP)], axis=0)
        qs = (qs * (HEAD_DIM ** -0.5)).astype(BF16)
        k_ref, v_ref = kv_ref.at[g], kv_ref.at[C_KV + g]
        k = jnp.concatenate([k_ref[pl.ds(prev0, tq), :], k_ref[pl.ds(q0, tq), :]], axis=0)
        v = jnp.concatenate([v_ref[pl.ds(prev0, tq), :], v_ref[pl.ds(q0, tq), :]], axis=0)
        s = _dot_nt(k, qs) + btile_ref[g]
        sinks = sink_ref[g]
        sink = jnp.concatenate([sinks[r:r + 1, :] for r in range(C_REP)], axis=1)
        m = jnp.maximum(jnp.max(s, axis=0, keepdims=True), sink)
        e = jnp.exp(s - m)
        p = e / (jnp.sum(e, axis=0, keepdims=True) + jnp.exp(sink - m))
        o = _dot_tn(v, p.astype(BF16))
        o_ref[:, g * width:(g + 1) * width] = jnp.concatenate([o[:, r * tq:(r + 1) * tq].T for r in range(C_REP)],
                                                              axis=1)


def _swa_prompt(q, kvt, sinks, tab):
    b, t, _ = q.shape
    tq = SWA_Q_TILE
    btile = jnp.swapaxes(_near_bias_tiles(tab, tq, WIN_C), -1, -2)
    qo_spec = pl.BlockSpec((None, tq, C_HEADS * HEAD_DIM), lambda i, j: (i, j, 0))
    sink_lanes = jnp.broadcast_to(sinks.reshape(C_KV, C_REP, 1), (C_KV, C_REP, 128))
    return pl.pallas_call(
        functools.partial(_swa_prompt_body, tq=tq),
        grid=(b, t // tq),
        in_specs=[qo_spec,
                  pl.BlockSpec((None, 2 * C_KV, t, HEAD_DIM), lambda i, j: (i, 0, 0, 0)),
                  pl.BlockSpec((None, C_KV, 2 * tq, C_REP * tq), lambda i, j: (jnp.minimum(j, 1), 0, 0, 0)),
                  pl.BlockSpec((C_KV, C_REP, 128), lambda i, j: (0, 0, 0))],
        out_specs=qo_spec,
        out_shape=jax.ShapeDtypeStruct((b, t, C_HEADS * HEAD_DIM), F32),
        compiler_params=pltpu.CompilerParams(
            dimension_semantics=("parallel", "arbitrary"), vmem_limit_bytes=V7X_VMEM_LIMIT_BYTES),
        name="swa_prompt",
    )(q, kvt, btile, sink_lanes)


def _swa_sample_body(q_ref, cache_ref, new_ref, newcol_ref, bias_ref, newb_ref, sink_ref, o_ref, next_ref):
    w_len = cache_ref.shape[-1]
    wcol = lax.broadcasted_iota(jnp.int32, (C_REP, w_len), 1)
    for g in range(C_KV):
        q8 = (q_ref[g] * (HEAD_DIM ** -0.5)).astype(BF16)
        s = jnp.dot(q8, cache_ref[0, g].astype(BF16), preferred_element_type=F32) + bias_ref[g]
        s = jnp.where(wcol >= 1, s, NEG_INF)
        k_new = _bf16_round(new_ref[g, 0:1])
        v_new = _bf16_round(new_ref[g, 1:2])
        s_new = jnp.sum(q8.astype(F32) * k_new, axis=-1, keepdims=True) + newb_ref[g][:, 0:1]
        sink = sink_ref[g][:, 0:1]
        m = jnp.maximum(jnp.maximum(jnp.max(s, axis=-1, keepdims=True), s_new), sink)
        e = jnp.exp(s - m)
        e_new = jnp.exp(s_new - m)
        den = jnp.sum(e, axis=-1, keepdims=True) + e_new + jnp.exp(sink - m)
        o_ref[g] = _dot_nt((e / den).astype(BF16), cache_ref[1, g].astype(BF16)) + _bf16_round(e_new / den) * v_new
        for kv in range(2):
            next_ref[kv, g] = _shift_in(cache_ref[kv, g], newcol_ref[g][:, kv:kv + 1])


def _swa_sample(q, cache, new_kv, sinks, tab):
    n, w_len = cache.shape[:2]
    bias = jnp.transpose(_bias_lookup(tab, w_len - np.arange(w_len)), (1, 2, 0))
    lanes = lambda x: jnp.broadcast_to(x[:, :, None], (C_KV, C_REP, 128))
    full = lambda shape: pl.BlockSpec(shape, lambda i: (0,) * len(shape))
    cache_spec = pl.BlockSpec((None, 2, C_KV, HEAD_DIM, w_len), lambda i: (i, 0, 0, 0, 0))
    o_spec = pl.BlockSpec((None, C_KV, C_REP, HEAD_DIM), lambda i: (i, 0, 0, 0))
    o, cache_next = pl.pallas_call(
        _swa_sample_body,
        grid=(n,),
        in_specs=[
            o_spec, cache_spec,
            pl.BlockSpec((None, C_KV, 2, HEAD_DIM), lambda i: (i, 0, 0, 0)),
            pl.BlockSpec((None, C_KV, HEAD_DIM, 2), lambda i: (i, 0, 0, 0)),
            full((C_KV, C_REP, w_len)), full((C_KV, C_REP, 128)), full((C_KV, C_REP, 128)),
        ],
        out_specs=[o_spec, cache_spec],
        out_shape=[jax.ShapeDtypeStruct((n, C_KV, C_REP, HEAD_DIM), F32),
                   jax.ShapeDtypeStruct((n, 2, C_KV, HEAD_DIM, w_len), F32)],
        compiler_params=pltpu.CompilerParams(
            dimension_semantics=("arbitrary",), vmem_limit_bytes=V7X_VMEM_LIMIT_BYTES),
        name="swa_sample",
    )(q, _rows_last(cache), new_kv, jnp.transpose(new_kv, (0, 1, 3, 2)), bias, lanes(tab[0]),
      lanes(sinks.reshape(C_KV, C_REP)))
    return o, _rows_first(cache_next)


PROJ_ROW_TILE = 512
A_IN_PAD = 1920
A_U_COL = A_Q + 6 * A_KVW
A_GATE_COL = A_U_COL + B_WIDTH


def _heads_first(kvt_ref, slot, k, v, n_kv):
    for g in range(n_kv):
        kvt_ref[slot + g] = k[:, g * HEAD_DIM:(g + 1) * HEAD_DIM].astype(BF16)
        kvt_ref[slot + n_kv + g] = v[:, g * HEAD_DIM:(g + 1) * HEAD_DIM].astype(BF16)


def _inproj_a_body(x_ref, gain_ref, w_ref, qg_ref, kg_ref,
                   q_ref, cmp_ref, sel_ref, win_ref, gate_ref, u_ref, kvt_ref):
    xn = _rms_rows(x_ref[...], gain_ref[...]).astype(BF16)
    z = jnp.dot(xn, w_ref[...], preferred_element_type=F32)
    for j in range(A_Q // 128):
        q_ref[:, j * 128:(j + 1) * 128] = _rms_heads128(z[:, j * 128:(j + 1) * 128], qg_ref[...])
    for out_ref, off, slot in ((cmp_ref, A_Q, None), (sel_ref, A_Q + 2 * A_KVW, 0), (win_ref, A_Q + 4 * A_KVW, 4)):
        k = _rms_heads128(z[:, off:off + A_KVW], kg_ref[...])
        v = z[:, off + A_KVW:off + 2 * A_KVW]
        out_ref[:, :A_KVW] = k
        out_ref[:, A_KVW:] = v
        if slot is not None:
            _heads_first(kvt_ref, slot, k, v, A_KV)
    u_ref[...] = z[:, A_U_COL:A_GATE_COL]
    gate_ref[...] = 1.0 / (1.0 + jnp.exp(-z[:, A_GATE_COL:A_IN_PAD]))


def _inproj_a(x, gain, w_in, q_gain, k_gain):
    b, t, d = x.shape
    tm = min(t, PROJ_ROW_TILE)
    w = jnp.concatenate([w_in[:, :A_U_COL], w_in[:, A_U_COL + A_GATE:], w_in[:, A_U_COL:A_U_COL + A_GATE],
                         jnp.zeros((d, A_IN_PAD - A_GATE_COL - A_GATE), F32)], axis=1).astype(BF16)
    tile2 = lambda g: jnp.tile(g, 2).reshape(1, 128)
    rows = lambda width: pl.BlockSpec((None, tm, width), lambda i, j: (i, j, 0))
    shape = lambda width: jax.ShapeDtypeStruct((b, t, width), F32)
    return pl.pallas_call(
        _inproj_a_body,
        grid=(b, t // tm),
        in_specs=[rows(d), _const_spec((1, d)), _const_spec((d, A_IN_PAD)), _const_spec((1, 128)), _const_spec((1, 128))],
        out_specs=[rows(A_Q), rows(2 * A_KVW), rows(2 * A_KVW), rows(2 * A_KVW), rows(128), rows(B_WIDTH),
                   pl.BlockSpec((None, 8, tm, HEAD_DIM), lambda i, j: (i, 0, j, 0))],
        out_shape=[shape(A_Q), shape(2 * A_KVW), shape(2 * A_KVW), shape(2 * A_KVW), shape(128), shape(B_WIDTH),
                   jax.ShapeDtypeStruct((b, 8, t, HEAD_DIM), BF16)],
        compiler_params=pltpu.CompilerParams(
            dimension_semantics=("parallel", "parallel"), vmem_limit_bytes=V7X_VMEM_LIMIT_BYTES),
        name="inproj_nsa_s5",
    )(x, gain.reshape(1, d), w, tile2(q_gain), tile2(k_gain))


def _inproj_c_body(x_ref, gain_ref, w_ref, qg_ref, kg_ref, q_ref, kv_ref, kvt_ref):
    xn = _rms_rows(x_ref[...], gain_ref[...]).astype(BF16)
    z = jnp.dot(xn, w_ref[...], preferred_element_type=F32)
    n_q = C_HEADS * HEAD_DIM
    for j in range(n_q // 128):
        q_ref[:, j * 128:(j + 1) * 128] = _rms_heads128(z[:, j * 128:(j + 1) * 128], qg_ref[...])
    k = _rms_heads128(z[:, n_q:n_q + C_KV * HEAD_DIM], kg_ref[...])
    v = z[:, n_q + C_KV * HEAD_DIM:]
    kv_ref[:, :C_KV * HEAD_DIM] = k
    kv_ref[:, C_KV * HEAD_DIM:] = v
    _heads_first(kvt_ref, 0, k, v, C_KV)


def _inproj_c(x, gain, w_in, q_gain, k_gain):
    b, t, d = x.shape
    tm = min(t, PROJ_ROW_TILE)
    n_in = w_in.shape[1]
    tile2 = lambda g: jnp.tile(g, 2).reshape(1, 128)
    rows = lambda width: pl.BlockSpec((None, tm, width), lambda i, j: (i, j, 0))
    shape = lambda width: jax.ShapeDtypeStruct((b, t, width), F32)
    return pl.pallas_call(
        _inproj_c_body,
        grid=(b, t // tm),
        in_specs=[rows(d), _const_spec((1, d)), _const_spec((d, n_in)), _const_spec((1, 128)), _const_spec((1, 128))],
        out_specs=[rows(C_HEADS * HEAD_DIM), rows(2 * C_KV * HEAD_DIM),
                   pl.BlockSpec((None, 2 * C_KV, tm, HEAD_DIM), lambda i, j: (i, 0, j, 0))],
        out_shape=[shape(C_HEADS * HEAD_DIM), shape(2 * C_KV * HEAD_DIM),
                   jax.ShapeDtypeStruct((b, 2 * C_KV, t, HEAD_DIM), BF16)],
        compiler_params=pltpu.CompilerParams(
            dimension_semantics=("parallel", "parallel"), vmem_limit_bytes=V7X_VMEM_LIMIT_BYTES),
        name="inproj_swa",
    )(x, gain.reshape(1, d), w_in.astype(BF16), tile2(q_gain), tile2(k_gain))


FFN_ROW_TILE = 512
FFN_COL_CHUNK = 1408


def _mixer_residual(y_ref, x_ref, mix_refs, wout_refs):
    y_ref[...] = x_ref[...]
    for m_ref, w_ref in zip(mix_refs, wout_refs):
        y_ref[...] += jnp.dot(m_ref[...].astype(BF16), w_ref[...], preferred_element_type=F32)
    return y_ref[...]


def _tail_prompt_body(*refs, n_mix, tm, ffc):
    x_ref = refs[0]
    mix_refs = refs[1:1 + n_mix]
    wout_refs = refs[1 + n_mix:1 + 2 * n_mix]
    (gain_ref, prev_ref, wup_ref, wgate_ref, cw_ref, cb_ref, wdown_ref, y_ref, cs_ref, hbuf_ref) = refs[1 + 2 * n_mix:]
    t = pl.program_id(1)
    xn = _rms_rows(_mixer_residual(y_ref, x_ref, mix_refs, wout_refs), gain_ref[...]).astype(BF16)
    for c in range(D_FF // ffc):
        lo = c * ffc
        h = jnp.dot(xn, wup_ref[:, lo:lo + ffc], preferred_element_type=F32)
        g = jnp.dot(xn, wgate_ref[:, lo:lo + ffc], preferred_element_type=F32)

        @pl.when(t == 0)
        def _():
            hbuf_ref[c, 6:8, :] = prev_ref[:, lo:lo + ffc]

        hbuf_ref[c, 8:8 + tm, :] = h
        hm1 = hbuf_ref[c, 7:7 + tm, :]
        hm2 = hbuf_ref[c, 6:6 + tm, :]
        cw = cw_ref[:, lo:lo + ffc]
        hc = cw[0:1] * hm2 + cw[1:2] * hm1 + cw[2:3] * h + cb_ref[:, lo:lo + ffc]
        a = (_gelu_tanh(hc) * g).astype(BF16)
        y_ref[...] += jnp.dot(a, wdown_ref[lo:lo + ffc, :], preferred_element_type=F32)
        hbuf_ref[c, 0:8, :] = h[tm - 8:tm, :]
        cs_ref[:, lo:lo + ffc] = h[tm - 2:tm, :]


def _tail_prompt(x, mixes, wouts, gain, prev, wup, wgate, cw, cb, wdown):
    b, t, d = x.shape
    tm, ffc = FFN_ROW_TILE, FFN_COL_CHUNK
    rows = lambda width: pl.BlockSpec((None, tm, width), lambda i, j: (i, j, 0))
    state = pl.BlockSpec((None, CONV_W - 1, D_FF), lambda i, j: (i, 0, 0))
    return pl.pallas_call(
        functools.partial(_tail_prompt_body, n_mix=len(mixes), tm=tm, ffc=ffc),
        grid=(b, t // tm),
        in_specs=[rows(d)] + [rows(m.shape[-1]) for m in mixes] + [_const_spec(w.shape) for w in wouts] + [
            _const_spec((1, d)), state, _const_spec((d, D_FF)), _const_spec((d, D_FF)),
            _const_spec((CONV_W, D_FF)), _const_spec((1, D_FF)), _const_spec((D_FF, d))],
        out_specs=[rows(d), state],
        out_shape=[jax.ShapeDtypeStruct((b, t, d), F32), jax.ShapeDtypeStruct((b, CONV_W - 1, D_FF), F32)],
        scratch_shapes=[pltpu.VMEM((D_FF // ffc, 8 + tm, ffc), F32)],
        compiler_params=pltpu.CompilerParams(
            dimension_semantics=("parallel", "arbitrary"), vmem_limit_bytes=V7X_VMEM_LIMIT_BYTES),
        name="tail_prompt",
    )(x, *mixes, *wouts, gain, prev, wup, wgate, cw, cb, wdown)


def _tail_sample_body(*refs, n_mix, ffc):
    x_ref = refs[0]
    mix_refs = refs[1:1 + n_mix]
    wout_refs = refs[1 + n_mix:1 + 2 * n_mix]
    (gain_ref, prev_ref, wup_ref, wgate_ref, cw_ref, cb_ref, wdown_ref, y_ref, cs_ref) = refs[1 + 2 * n_mix:]
    xn = _rms_rows(_mixer_residual(y_ref, x_ref, mix_refs, wout_refs), gain_ref[...]).astype(BF16)
    for c in range(D_FF // ffc):
        lo = c * ffc
        h = jnp.dot(xn, wup_ref[:, lo:lo + ffc], preferred_element_type=F32)
        g = jnp.dot(xn, wgate_ref[:, lo:lo + ffc], preferred_element_type=F32)
        hm2 = prev_ref[:, lo:lo + ffc]
        hm1 = prev_ref[:, D_FF + lo:D_FF + lo + ffc]
        cw = cw_ref[:, lo:lo + ffc]
        hc = cw[0:1] * hm2 + cw[1:2] * hm1 + cw[2:3] * h + cb_ref[:, lo:lo + ffc]
        a = (_gelu_tanh(hc) * g).astype(BF16)
        y_ref[...] += jnp.dot(a, wdown_ref[lo:lo + ffc, :], preferred_element_type=F32)
        cs_ref[:, lo:lo + ffc] = hm1
        cs_ref[:, D_FF + lo:D_FF + lo + ffc] = h


def _tail_sample(x, mixes, wouts, gain, prev, wup, wgate, cw, cb, wdown):
    n, d = x.shape
    full = lambda shape: _const_spec(shape)
    return pl.pallas_call(
        functools.partial(_tail_sample_body, n_mix=len(mixes), ffc=FFN_COL_CHUNK),
        grid=(1,),
        in_specs=[full((n, d))] + [full(m.shape) for m in mixes] + [full(w.shape) for w in wouts] + [
            full((1, d)), full((n, (CONV_W - 1) * D_FF)), full((d, D_FF)), full((d, D_FF)),
            full((CONV_W, D_FF)), full((1, D_FF)), full((D_FF, d))],
        out_specs=[pl.BlockSpec((n, d), lambda i: (0, 0)), pl.BlockSpec((n, (CONV_W - 1) * D_FF), lambda i: (0, 0))],
        out_shape=[jax.ShapeDtypeStruct((n, d), F32), jax.ShapeDtypeStruct((n, (CONV_W - 1) * D_FF), F32)],
        compiler_params=pltpu.CompilerParams(
            dimension_semantics=("arbitrary",), vmem_limit_bytes=V7X_VMEM_LIMIT_BYTES),
        name="tail_sample",
    )(x, *mixes, *wouts, gain, prev, wup, wgate, cw, cb, wdown)


def kernel(x_prompt, x_sample, cache_nsa_cmp, cache_nsa_sel, cache_nsa_win, state_s5_re, state_s5_im,
           cache_swa, state_ffn_conv, page_table, rel_bias, norm_mix, norm_ffn, a_w_in, a_w_out,
           nsa_q_gain, nsa_k_gain, nsa_cmp_pos, nsa_cmp_w1, nsa_cmp_w2, s5_a_re, s5_a_im, s5_log_dt,
           s5_b_re, s5_b_im, s5_c_re, s5_c_im, s5_d, s5_w_glu, s5_b_glu, c_w_in, c_w_out, c_q_gain,
           c_k_gain, c_sinks, ffn_w_up, ffn_w_gate, ffn_conv_w, ffn_conv_b, ffn_w_down):
    bp, tp, _ = x_prompt.shape
    bs, ts, _ = x_sample.shape
    assert ts == 1 and DEPTH == 2
    tab_a = rel_bias[:, :A_HEADS].reshape(NUM_BUCKETS, A_KV, A_REP)
    tab_c = rel_bias[:, :C_HEADS].reshape(NUM_BUCKETS, C_KV, C_REP)
    kv6 = lambda x: x.reshape(x.shape[:-1] + (2, x.shape[-1] // (2 * HEAD_DIM), HEAD_DIM))
    hp, hs = x_prompt, x_sample.reshape(1, bs, D_MODEL)
    conv_p, conv_s = [], []

    def tail(layer, hp, hs, mixes_p, mixes_s, wouts):
        wouts = [w.astype(BF16) for w in wouts]
        ffn = (norm_ffn[layer].reshape(1, D_MODEL),)
        wts = (ffn_w_up[layer].astype(BF16), ffn_w_gate[layer].astype(BF16), ffn_conv_w[layer],
               ffn_conv_b[layer].reshape(1, D_FF), ffn_w_down[layer].astype(BF16))
        hp, cp = _tail_prompt(hp, mixes_p, wouts, *ffn, jnp.zeros((bp, CONV_W - 1, D_FF), F32), *wts)
        hs2, cs = _tail_sample(hs[0], [m.reshape(bs, -1) for m in mixes_s], wouts, *ffn,
                               state_ffn_conv[layer].reshape(bs, (CONV_W - 1) * D_FF), *wts)
        conv_p.append(cp)
        conv_s.append(cs.reshape(bs, CONV_W - 1, D_FF))
        return hp, hs2.reshape(1, bs, D_MODEL)

    proj = (norm_mix[0], a_w_in[0], nsa_q_gain[0], nsa_k_gain[0])
    qp, cmp_p, sel_p, win_p, gate_p, up, kvt_p = _inproj_a(hp, *proj)
    qs, cmp_s, sel_s, win_s, gate_s, us, _ = _inproj_a(hs, *proj)
    cmp_w = _compress_weights(nsa_cmp_pos[0], nsa_cmp_w1[0], nsa_cmp_w2[0], nsa_k_gain[0])
    kc, vc = _cmp_prompt(cmp_p, cmp_w)
    o_ap = _nsa_prompt_attend(qp, gate_p, kc, vc, kvt_p, tab_a)
    o_as, nsa_win_s = _nsa_sample(qs.reshape(bs, A_KV, A_REP, HEAD_DIM),
                                  gate_s[0, :, :A_GATE].reshape(bs, A_KV, A_REP, 3),
                                  sel_s.reshape(bs, 2, A_KV, HEAD_DIM), win_s.reshape(bs, 2, A_KV, HEAD_DIM),
                                  cache_nsa_cmp[0], cache_nsa_sel[0], cache_nsa_win[0], page_table, cmp_w, tab_a)
    s5w = _s5_weights(s5_a_re[0], s5_a_im[0], s5_log_dt[0], s5_b_re[0], s5_b_im[0], s5_c_re[0], s5_c_im[0],
                      s5_d[0], s5_w_glu[0], s5_b_glu[0])
    h0 = jnp.zeros((bp, S5_NS), F32)
    o_bp, hr_p, hi_p = _s5_mix(up, h0, h0, s5w)
    o_bs, hr_s, hi_s = _s5_mix(us.reshape(bs, 1, B_WIDTH), state_s5_re[0].reshape(bs, S5_NS),
                               state_s5_im[0].reshape(bs, S5_NS), s5w)
    hp, hs = tail(0, hp, hs, [o_ap, o_bp], [o_as, o_bs], [a_w_out[0][:A_Q], a_w_out[0][A_Q:]])

    proj = (norm_mix[1], c_w_in[0], c_q_gain[0], c_k_gain[0])
    qcp, kv_p, kvt_c = _inproj_c(hp, *proj)
    qcs, kv_s, _ = _inproj_c(hs, *proj)
    o_cp = _swa_prompt(qcp, kvt_c, c_sinks[0], tab_c)
    new_kv = kv_s.reshape(bs, 2, C_KV, HEAD_DIM)
    o_cs, swa_s = _swa_sample(qcs.reshape(bs, C_KV, C_REP, HEAD_DIM), cache_swa[0],
                              jnp.transpose(new_kv, (0, 2, 1, 3)), c_sinks[0], tab_c)
    hp, hs = tail(1, hp, hs, [o_cp], [o_cs], [c_w_out[0]])

    state = lambda x, n: x.reshape(1, n, B_GROUPS, B_STATE)
    return (hp, hs.reshape(bs, ts, D_MODEL),
            kv6(cmp_p)[None], kv6(cmp_s).reshape(1, bs, ts, 2, A_KV, HEAD_DIM),
            kv6(sel_p)[None], kv6(sel_s).reshape(1, bs, ts, 2, A_KV, HEAD_DIM),
            kv6(win_p)[None, :, -min(WIN_A, tp):], nsa_win_s[None],
            state(hr_p, bp), state(hi_p, bp), state(hr_s, bs), state(hi_s, bs),
            kv6(kv_p)[None, :, -min(WIN_C, tp):], swa_s[None],
            jnp.stack(conv_p), jnp.stack(conv_s))
```

```python
import functools
import math

import jax
import jax.numpy as jnp
import numpy as np
from jax import lax
from jax.experimental import pallas as pl
from jax.experimental.pallas import tpu as pltpu

D_MODEL = 1024
DEPTH = 2
PAGE_SIZE = 128
HEAD_DIM = 64
A_HEADS = 8
A_KV = 2
A_REP = A_HEADS // A_KV
A_Q = A_HEADS * HEAD_DIM
A_KVW = A_KV * HEAD_DIM
A_GATE = 3 * A_HEADS
CMP_LEN = 32
CMP_STRIDE = 16
SEL_BLOCK = 64
SEL_TOPK = 16
WIN_A = 512
NSA_QBLK = 64
FORCE_BONUS = 1000.0
B_WIDTH = D_MODEL // 2
B_GROUP = 16
B_GROUPS = B_WIDTH // B_GROUP
B_STATE = 64
C_HEADS = D_MODEL // HEAD_DIM
C_KV = 2
C_REP = C_HEADS // C_KV
WIN_C = 128
NUM_BUCKETS = 32
MAX_DISTANCE = 128
D_FF = 2816
CONV_W = 3
EPS = 1e-6

F32 = jnp.float32
BF16 = jnp.bfloat16

V7X_VMEM_LIMIT_BYTES = 56 * 1024 * 1024


def _gelu_tanh(x):
    return 0.5 * x * (1.0 + jnp.tanh(math.sqrt(2.0 / math.pi) * (x + 0.044715 * (x * x * x))))


def _rms_rows(x, gain):
    return x * lax.rsqrt(jnp.mean(x * x, axis=-1, keepdims=True) + EPS) * gain


def _const_spec(shape):
    zeros = (0,) * len(shape)
    return pl.BlockSpec(shape, lambda *_: zeros, pipeline_mode=pl.Buffered(1))


NSA_Q_TILE = 128
NSA_FAR_TILE = 512
NSA_HEADS_PER_PASS = 4
NEG_INF = float("-inf")


def _dot_nt(a, b):
    return lax.dot_general(a, b, (((1,), (1,)), ((), ())), preferred_element_type=F32)


def _dot_tn(a, b):
    return lax.dot_general(a, b, (((0,), (0,)), ((), ())), preferred_element_type=F32)


def _softmax_start(s, v):
    m = jnp.max(s, axis=0, keepdims=True)
    e = jnp.exp(s - m)
    return m, jnp.sum(e, axis=0, keepdims=True), _dot_tn(v, e.astype(BF16))


def _softmax_more(carry, s, v):
    m, l, acc = carry
    m_new = jnp.maximum(m, jnp.max(s, axis=0, keepdims=True))
    alpha = jnp.exp(m - m_new)
    e = jnp.exp(s - m_new)
    return m_new, alpha * l + jnp.sum(e, axis=0, keepdims=True), alpha * acc + _dot_tn(v, e.astype(BF16))


def _nsa_prompt_body(q_ref, gate_ref, kc_ref, vc_ref, kvt_ref, btile_ref, cover_ref, o_ref, sel_scr, *, tq):
    width = A_REP * HEAD_DIM
    gates_t = gate_ref[...].T
    for g in range(A_KV):
        _nsa_prompt_group(g, q_ref.at[:, g * width:(g + 1) * width], gates_t, kc_ref.at[g], vc_ref.at[g],
                          kvt_ref.at[g], kvt_ref.at[A_KV + g], kvt_ref.at[2 * A_KV + g], kvt_ref.at[3 * A_KV + g],
                          btile_ref.at[g], cover_ref, o_ref.at[:, g * width:(g + 1) * width], sel_scr.at[g], tq)


def _nsa_prompt_group(g, q_ref, gates_t, kc_ref, vc_ref, ks_ref, vs_ref, kw_ref, vw_ref, btile_ref, cover_ref,
                      o_ref, sel_ref, tq):
    qt = pl.program_id(1)
    q0 = pl.multiple_of(qt * tq, tq)
    cols = A_REP * tq
    q = q_ref[...]
    qs = jnp.concatenate([q[:, r * HEAD_DIM:(r + 1) * HEAD_DIM] for r in range(A_REP)], axis=0)
    qs = (qs * (HEAD_DIM ** -0.5)).astype(BF16)

    def q_pos(height):
        return q0 + (lax.broadcasted_iota(jnp.int32, (height, cols), 1) & (tq - 1))

    def key_idx(height):
        return lax.broadcasted_iota(jnp.int32, (height, cols), 0)

    n_idx = key_idx(128)
    valid_c = (n_idx * CMP_STRIDE + (CMP_LEN - 1) <= q_pos(128)) & (n_idx < 127)
    s_c = jnp.where(valid_c, _dot_nt(kc_ref[...], qs), NEG_INF)
    m_c = jnp.max(s_c, axis=0, keepdims=True)
    m_c = jnp.where(m_c == NEG_INF, 0.0, m_c)
    e_c = jnp.exp(s_c - m_c)
    d_c = jnp.sum(e_c, axis=0, keepdims=True)
    p_c = (e_c / jnp.where(d_c > 0, d_c, 1.0)).astype(BF16)
    o_c = _dot_tn(vc_ref[...], p_c)
    imp_heads = jnp.dot(cover_ref[...], p_c, preferred_element_type=F32)
    imp = sum(imp_heads[0:32, r * tq:(r + 1) * tq] for r in range(A_REP))
    s_idx = lax.broadcasted_iota(jnp.int32, (32, tq), 0)
    qblk = (q0 + lax.broadcasted_iota(jnp.int32, (32, tq), 1)) >> 6
    forced = (s_idx == 0) | (s_idx == qblk) | (s_idx == qblk - 1)
    allowed = s_idx <= qblk
    score = jnp.where(allowed, imp + jnp.where(forced, FORCE_BONUS, 0.0), NEG_INF)
    rank = jnp.zeros((32, tq), F32)
    for j in range(32):
        other = score[j:j + 1, :]
        beats = (other > score) | ((other == score) & (s_idx > j))
        rank = rank + jnp.where(beats, 1.0, 0.0)
    sel = jnp.where((rank < SEL_TOPK) & allowed, 1.0, 0.0)
    hcols = NSA_HEADS_PER_PASS * tq
    far_end = q0 - tq
    sel_far_only = jnp.where(s_idx < far_end // SEL_BLOCK, sel, 0.0)
    sel_ref[...] = jnp.concatenate([jnp.concatenate([sel] * NSA_HEADS_PER_PASS, axis=1),
                                    jnp.concatenate([sel_far_only] * NSA_HEADS_PER_PASS, axis=1)], axis=0)

    def block_mask(k0, n_blocks, base=0):
        first = base + k0 // SEL_BLOCK
        return jnp.concatenate([jnp.broadcast_to(sel_ref[pl.ds(first + j, 1), :], (SEL_BLOCK, hcols))
                                for j in range(n_blocks)], axis=0) > 0.5

    def hq_pos(height):
        return q0 + (lax.broadcasted_iota(jnp.int32, (height, hcols), 1) & (tq - 1))

    def hkey_idx(height):
        return lax.broadcasted_iota(jnp.int32, (height, hcols), 0)

    prev0 = pl.multiple_of(jnp.maximum(q0 - tq, 0), tq)
    sel_near = jnp.concatenate([block_mask(prev0, tq // SEL_BLOCK), block_mask(q0, tq // SEL_BLOCK)], axis=0)
    n_far = (jnp.maximum(far_end, 0) + NSA_FAR_TILE - 1) // NSA_FAR_TILE
    w_far = WIN_A - tq
    wf0 = pl.multiple_of(jnp.maximum(q0 - WIN_A, 0), tq)
    wpos = wf0 + hkey_idx(w_far)
    wmask = (hq_pos(w_far) - wpos < WIN_A) & (wpos < far_end)

    outs = []
    for h0 in range(0, A_REP, NSA_HEADS_PER_PASS):
        csl = slice(h0 * tq, h0 * tq + hcols)
        qh = qs[h0 * tq:h0 * tq + hcols]
        btile = btile_ref[:, csl]

        def near(k_ref, v_ref, extra_mask):
            k = jnp.concatenate([k_ref[pl.ds(prev0, tq), :], k_ref[pl.ds(q0, tq), :]], axis=0)
            v = jnp.concatenate([v_ref[pl.ds(prev0, tq), :], v_ref[pl.ds(q0, tq), :]], axis=0)
            s = _dot_nt(k, qh) + btile
            return _softmax_start(s if extra_mask is None else jnp.where(extra_mask, s, NEG_INF), v)

        def sel_far(i, carry):
            k0 = pl.multiple_of(i * NSA_FAR_TILE, NSA_FAR_TILE)
            s = _dot_nt(ks_ref[pl.ds(k0, NSA_FAR_TILE), :], qh)
            mask = block_mask(k0, NSA_FAR_TILE // SEL_BLOCK, base=32)
            return _softmax_more(carry, jnp.where(mask, s, NEG_INF), vs_ref[pl.ds(k0, NSA_FAR_TILE), :])

        _, l_s, acc_s = lax.fori_loop(0, n_far, sel_far, near(ks_ref, vs_ref, sel_near))
        o_s = acc_s / l_s

        s_w = jnp.where(wmask, _dot_nt(kw_ref[pl.ds(wf0, w_far), :], qh), NEG_INF)
        _, l_w, acc_w = _softmax_more(near(kw_ref, vw_ref, None), s_w, vw_ref[pl.ds(wf0, w_far), :])
        o_w = acc_w / l_w

        for i in range(NSA_HEADS_PER_PASS):
            r = h0 + i
            sl = slice(i * tq, (i + 1) * tq)
            row = 3 * (g * A_REP + r)
            out_t = (gates_t[row:row + 1] * o_c[:, r * tq:(r + 1) * tq] + gates_t[row + 1:row + 2] * o_s[:, sl]
                     + gates_t[row + 2:row + 3] * o_w[:, sl])
            outs.append(out_t.T)
    o_ref[...] = jnp.concatenate(outs, axis=1)


def _bucket_np(dist):
    n = np.maximum(dist, 0)
    exact = NUM_BUCKETS // 2
    nf = np.maximum(n, exact).astype(np.float64)
    large = exact + (np.log(nf / exact) / math.log(MAX_DISTANCE / exact) * (NUM_BUCKETS - exact)).astype(np.int64)
    return np.where(n < exact, n, np.minimum(large, NUM_BUCKETS - 1)).astype(np.int32)


def _bias_lookup(tab, dist):
    bucket = _bucket_np(np.asarray(dist))
    onehot = (jnp.asarray(bucket.reshape(-1, 1)) == jnp.arange(NUM_BUCKETS)[None, :]).astype(F32)
    flat = jnp.dot(onehot, tab.reshape(NUM_BUCKETS, -1), precision=lax.Precision.HIGHEST)
    return flat.reshape(bucket.shape + tab.shape[1:])


def _near_bias_tiles(tab, tq, window):
    i = np.arange(tq)[:, None]
    j = np.arange(2 * tq)[None, :]
    dist = tq + i - j
    ok = (dist >= 0) & (dist < window)
    bias = jnp.transpose(_bias_lookup(tab, dist), (2, 3, 0, 1))
    tiles = jnp.stack([jnp.where(jnp.asarray(ok & (j >= tq)), bias, NEG_INF), jnp.where(jnp.asarray(ok), bias, NEG_INF)])
    return tiles.reshape(2, tab.shape[1], tab.shape[2] * tq, 2 * tq)


def _cover_matrix(n_cmp_pad, n_sel):
    n = np.arange(n_cmp_pad)
    c_start = n * CMP_STRIDE
    c_end = c_start + CMP_LEN - 1
    s_start = np.arange(128) * SEL_BLOCK
    cover = (c_start[:, None] < s_start[None, :] + SEL_BLOCK) & (c_end[:, None] >= s_start[None, :])
    cover &= (np.arange(128)[None, :] < n_sel)
    return np.tile(cover.astype(np.float32), (A_REP, 1))


def _nsa_prompt_attend(q, gates, kc, vc, kvt, tab):
    b, t, _ = q.shape
    tq = NSA_Q_TILE
    assert t % NSA_FAR_TILE == 0 and t // SEL_BLOCK <= 32 and t >= WIN_A and kc.shape[2] == 128
    btile = jnp.swapaxes(_near_bias_tiles(tab - tab[NUM_BUCKETS - 1], tq, 2 * tq), -1, -2)
    cover = jnp.asarray(_cover_matrix(128, t // SEL_BLOCK)[:128].T, BF16)
    cmp_spec = pl.BlockSpec((None, A_KV, 128, HEAD_DIM), lambda i, j: (i, 0, 0, 0))
    return pl.pallas_call(
        functools.partial(_nsa_prompt_body, tq=tq),
        grid=(b, t // tq),
        in_specs=[
            pl.BlockSpec((None, tq, A_Q), lambda i, j: (i, j, 0)),
            pl.BlockSpec((None, tq, 128), lambda i, j: (i, j, 0)),
            cmp_spec, cmp_spec,
            pl.BlockSpec((None, 4 * A_KV, t, HEAD_DIM), lambda i, j: (i, 0, 0, 0)),
            pl.BlockSpec((None, A_KV, 2 * tq, A_REP * tq), lambda i, j: (jnp.minimum(j, 1), 0, 0, 0)),
            pl.BlockSpec((128, 128), lambda i, j: (0, 0)),
        ],
        out_specs=pl.BlockSpec((None, tq, A_Q), lambda i, j: (i, j, 0)),
        out_shape=jax.ShapeDtypeStruct((b, t, A_Q), F32),
        scratch_shapes=[pltpu.VMEM((A_KV, 2 * 32, NSA_HEADS_PER_PASS * tq), F32)],
        compiler_params=pltpu.CompilerParams(
            dimension_semantics=("parallel", "arbitrary"), vmem_limit_bytes=V7X_VMEM_LIMIT_BYTES),
        name="nsa_prompt",
    )(q, gates, kc, vc, kvt, btile, cover)


S5_NS = B_GROUPS * B_STATE
S5_T_CHUNK = 64
S5_STRIP = 512
S5_HALVES = 2


def _s5_body(u_ref, h0r_ref, h0i_ref, ar_ref, ai_ref, ldt_ref, wb_ref, wc_ref, d_ref, wglu_ref, bglu_ref,
             o_ref, hr_ref, hi_ref, coef_ref, st_ref, xbuf_ref, ubuf_ref, obuf_ref, *, nb, steps, interleave):
    c = pl.program_id(0)

    @pl.when(c == 0)
    def _():
        dt = jnp.exp(ldt_ref[...])
        ar, ai = ar_ref[...], ai_ref[...]
        mag = jnp.exp(ar * dt)
        abr, abi = mag * jnp.cos(ai * dt), mag * jnp.sin(ai * dt)
        den = ar * ar + ai * ai
        wr = ((abr - 1.0) * ar + abi * ai) / den
        wi = (abi * ar - (abr - 1.0) * ai) / den
        for k, val in enumerate((abr, abi, wr, wi)):
            coef_ref[k] = jnp.broadcast_to(val, (nb, S5_NS))
        st_ref[0] = h0r_ref[...]
        st_ref[1] = h0i_ref[...]

    if interleave:
        for b in range(nb):
            for j in range(B_WIDTH // 128):
                ubuf_ref.at[j][pl.ds(b, steps, stride=nb), :] = u_ref[b, :, j * 128:(j + 1) * 128]
        u = jnp.concatenate([ubuf_ref[j] for j in range(B_WIDTH // 128)], axis=1)
    else:
        u = u_ref[...]
    hc, hs = B_WIDTH // S5_HALVES, S5_NS // S5_HALVES
    ub = u.astype(BF16)
    for h in range(S5_HALVES):
        bu = jnp.dot(ub[:, h * hc:(h + 1) * hc], wb_ref[h], preferred_element_type=F32)
        xbuf_ref[:, h * hs:(h + 1) * hs] = bu[:, :hs]
        xbuf_ref[:, S5_NS + h * hs:S5_NS + (h + 1) * hs] = bu[:, hs:]

    for lo in range(0, S5_NS, S5_STRIP):
        re = slice(lo, lo + S5_STRIP)
        im = slice(S5_NS + lo, S5_NS + lo + S5_STRIP)
        abr, abi, wr, wi = (coef_ref[k, :, re] for k in range(4))

        def step(t, carry):
            sr, si = carry
            r0 = pl.multiple_of(t * nb, nb)
            bur = xbuf_ref[pl.ds(r0, nb), re]
            bui = xbuf_ref[pl.ds(r0, nb), im]
            nsr = abr * sr - abi * si + (wr * bur - wi * bui)
            nsi = abr * si + abi * sr + (wr * bui + wi * bur)
            xbuf_ref[pl.ds(r0, nb), re] = nsr
            xbuf_ref[pl.ds(r0, nb), im] = nsi
            return nsr, nsi

        sr, si = lax.fori_loop(0, steps, step, (st_ref[0, :, re], st_ref[1, :, re]),
                               unroll=min(steps, 8))
        st_ref[0, :, re] = sr
        st_ref[1, :, re] = si

    y = jnp.concatenate(
        [jnp.dot(jnp.concatenate([xbuf_ref[:, h * hs:(h + 1) * hs], xbuf_ref[:, S5_NS + h * hs:S5_NS + (h + 1) * hs]],
                                 axis=1).astype(BF16), wc_ref[h], preferred_element_type=F32)
         for h in range(S5_HALVES)], axis=1) + d_ref[...] * u
    z = _gelu_tanh(y)
    gate = jnp.dot(z.astype(BF16), wglu_ref[...], preferred_element_type=F32) + bglu_ref[...]
    out = z * (1.0 / (1.0 + jnp.exp(-gate)))
    if interleave:
        for j in range(B_WIDTH // 128):
            obuf_ref[j] = out[:, j * 128:(j + 1) * 128]
        for b in range(nb):
            for j in range(B_WIDTH // 128):
                o_ref[b, :, j * 128:(j + 1) * 128] = obuf_ref.at[j][pl.ds(b, steps, stride=nb), :]
    else:
        o_ref[...] = out
    hr_ref[...] = st_ref[0]
    hi_ref[...] = st_ref[1]


def _s5_weights(a_re, a_im, log_dt, b_re, b_im, c_re, c_im, d, w_glu, b_glu):
    eye = jnp.eye(B_GROUPS, dtype=F32)
    blk_in = lambda w: jnp.einsum('hg,gpc->hcgp', eye, w).reshape(B_WIDTH, S5_NS)
    blk_out = lambda w: jnp.einsum('gh,gcp->gphc', eye, w).reshape(S5_NS, B_WIDTH)
    hc, hs = B_WIDTH // S5_HALVES, S5_NS // S5_HALVES
    diag = lambda w, h: w[h * hc:(h + 1) * hc, h * hs:(h + 1) * hs]
    wb = jnp.stack([jnp.concatenate([diag(blk_in(b_re), h), diag(blk_in(b_im), h)], axis=1)
                    for h in range(S5_HALVES)]).astype(BF16)
    diag_t = lambda w, h: w[h * hs:(h + 1) * hs, h * hc:(h + 1) * hc]
    wc = jnp.stack([jnp.concatenate([diag_t(blk_out(c_re), h), -diag_t(blk_out(c_im), h)], axis=0)
                    for h in range(S5_HALVES)]).astype(BF16)
    flat = lambda x: x.reshape(1, S5_NS)
    return (flat(a_re), flat(a_im), flat(jnp.repeat(log_dt, B_STATE)), wb, wc, d.reshape(1, B_WIDTH),
            w_glu.astype(BF16), b_glu.reshape(1, B_WIDTH))


def _s5_mix(u, h_re, h_im, weights):
    nb, t, _ = u.shape
    interleave = t > 1
    steps = min(t, S5_T_CHUNK)
    rows = nb * steps
    body = functools.partial(_s5_body, nb=nb, steps=steps, interleave=interleave)
    if interleave:
        u_in = u
        u_spec = pl.BlockSpec((nb, steps, B_WIDTH), lambda c: (0, c, 0))
        o_shape = jax.ShapeDtypeStruct((nb, t, B_WIDTH), F32)
        scratch_rows = rows
    else:
        u_in = u.reshape(nb, B_WIDTH)
        u_spec = pl.BlockSpec((nb, B_WIDTH), lambda c: (0, 0))
        o_shape = jax.ShapeDtypeStruct((nb, B_WIDTH), F32)
        scratch_rows = 8
    o, hr, hi = pl.pallas_call(
        body,
        grid=(t // steps,),
        in_specs=[
            u_spec, _const_spec((nb, S5_NS)), _const_spec((nb, S5_NS)),
            _const_spec((1, S5_NS)), _const_spec((1, S5_NS)), _const_spec((1, S5_NS)),
            _const_spec((S5_HALVES, B_WIDTH // S5_HALVES, 2 * S5_NS // S5_HALVES)),
            _const_spec((S5_HALVES, 2 * S5_NS // S5_HALVES, B_WIDTH // S5_HALVES)),
            _const_spec((1, B_WIDTH)), _const_spec((B_WIDTH, B_WIDTH)), _const_spec((1, B_WIDTH)),
        ],
        out_specs=[u_spec, pl.BlockSpec((nb, S5_NS), lambda c: (0, 0)), pl.BlockSpec((nb, S5_NS), lambda c: (0, 0))],
        out_shape=[o_shape, jax.ShapeDtypeStruct((nb, S5_NS), F32), jax.ShapeDtypeStruct((nb, S5_NS), F32)],
        scratch_shapes=[
            pltpu.VMEM((4, nb, S5_NS), F32),
            pltpu.VMEM((2, nb, S5_NS), F32),
            pltpu.VMEM((rows, 2 * S5_NS), F32),
            pltpu.VMEM((B_WIDTH // 128, scratch_rows, 128), F32),
            pltpu.VMEM((B_WIDTH // 128, scratch_rows, 128), F32),
        ],
        compiler_params=pltpu.CompilerParams(
            dimension_semantics=("arbitrary",),
            vmem_limit_bytes=V7X_VMEM_LIMIT_BYTES),
        name="s5_mix",
    )(u_in, h_re, h_im, *weights)
    return o.reshape(nb, t, B_WIDTH), hr, hi


CHUNK_W = CMP_STRIDE * 2 * A_KVW
CHUNKS_PER_PAGE = PAGE_SIZE // CMP_STRIDE
SEL_PAD = 256


def _bf16_round(x):
    return x.astype(BF16).astype(F32)


def _compress_tail(c, pos, w2):
    c1 = c[:, 128:]
    nxt = jnp.concatenate([c1[1:], c1[:1]], axis=0)
    hid = (pos + c[:, :128]) + nxt
    return jnp.dot(_gelu_tanh(hid).astype(BF16), w2, preferred_element_type=F32)


def _rms_heads128(x, gain):
    left = lax.broadcasted_iota(jnp.int32, x.shape, 1) < HEAD_DIM
    sq = x * x
    s0 = jnp.sum(jnp.where(left, sq, 0.0), axis=-1, keepdims=True)
    s1 = jnp.sum(jnp.where(left, 0.0, sq), axis=-1, keepdims=True)
    ms = jnp.where(left, s0, s1) * (1.0 / HEAD_DIM)
    return x * lax.rsqrt(ms + EPS) * gain


def _pad_rows8(x):
    return jnp.concatenate([x, jnp.zeros((8 - x.shape[0], x.shape[1]), x.dtype)], axis=0)


def _compress_mlp(row_of_chunks, n_chunks, w_ref, pos, w2_ref):
    acc = jnp.zeros((n_chunks, 2 * A_KVW), F32)
    for sp in range(CMP_STRIDE // 2):
        x = jnp.concatenate([row_of_chunks(2 * sp + e) for e in range(2)], axis=1).astype(BF16)
        acc = acc + jnp.dot(x, w_ref[sp], preferred_element_type=F32)
    return _compress_tail(acc, pos, w2_ref[...])


def _compress_rows(rows_scr, kv, n_chunks, w_ref, pos, w2_ref):
    return _compress_mlp(lambda s: rows_scr.at[kv][pl.ds(s, n_chunks, stride=CMP_STRIDE), :],
                         n_chunks, w_ref, pos, w2_ref)


def _compress_weights(cmp_pos, cmp_w1, cmp_w2, k_gain):
    eye = jnp.eye(A_KV, dtype=F32)
    w1 = cmp_w1.reshape(2, 2, CMP_STRIDE, HEAD_DIM, HEAD_DIM)
    w_big = jnp.einsum('ajsdh,gk->asgdjkh', w1, eye).reshape(2, CMP_STRIDE // 2, 2 * A_KVW, 2 * A_KVW).astype(BF16)
    w_pos = jnp.concatenate([cmp_w1, cmp_w1], axis=-1).astype(BF16)
    pos = jnp.broadcast_to(cmp_pos.reshape(2, 1, CMP_LEN * HEAD_DIM), (2, 8, CMP_LEN * HEAD_DIM)).astype(BF16)
    w2_big = jnp.einsum('ahd,gk->aghkd', cmp_w2, eye).reshape(2, A_KVW, A_KVW).astype(BF16)
    gain2 = jnp.tile(k_gain, A_KV).reshape(1, A_KVW)
    return (pos[0], pos[1], w_pos[0], w_pos[1], w_big[0], w_big[1], w2_big[0], w2_big[1], gain2)


def _compress_specs(full):
    mlp_w = (CMP_STRIDE // 2, 2 * A_KVW, 2 * A_KVW)
    return [full((8, CMP_LEN * HEAD_DIM)), full((8, CMP_LEN * HEAD_DIM)),
            full((CMP_LEN * HEAD_DIM, A_KVW)), full((CMP_LEN * HEAD_DIM, A_KVW)),
            full(mlp_w), full(mlp_w), full((A_KVW, A_KVW)), full((A_KVW, A_KVW)), full((1, A_KVW))]


def _pos_terms(pos_scr, posk_ref, posv_ref, wpos_k_ref, wpos_v_ref):
    pos_scr[0] = jnp.dot(posk_ref[...], wpos_k_ref[...], preferred_element_type=F32)
    pos_scr[1] = jnp.dot(posv_ref[...], wpos_v_ref[...], preferred_element_type=F32)


def _cmp_prompt_body(cmp_ref, posk_ref, posv_ref, wpos_k_ref, wpos_v_ref, wk_ref, wv_ref, w2k_ref, w2v_ref,
                     kgain_ref, kc_ref, vc_ref, pos_scr, rows_scr, *, n_chunks):
    @pl.when(pl.program_id(0) == 0)
    def _():
        _pos_terms(pos_scr, posk_ref, posv_ref, wpos_k_ref, wpos_v_ref)

    rows_scr[0] = cmp_ref[:, :A_KVW]
    rows_scr[1] = cmp_ref[:, A_KVW:]
    kc = _rms_heads128(_compress_rows(rows_scr, 0, n_chunks, wk_ref, pos_scr[0, 0:1], w2k_ref),
                       kgain_ref[...]).astype(BF16)
    vc = _compress_rows(rows_scr, 1, n_chunks, wv_ref, pos_scr[1, 0:1], w2v_ref).astype(BF16)
    for g in range(A_KV):
        kc_ref[g] = kc[:, g * HEAD_DIM:(g + 1) * HEAD_DIM]
        vc_ref[g] = vc[:, g * HEAD_DIM:(g + 1) * HEAD_DIM]


def _cmp_prompt(cmp_rows, cmp_weights):
    b, t, _ = cmp_rows.shape
    n_chunks = t // CMP_STRIDE
    full = lambda shape: pl.BlockSpec(shape, lambda i: (0,) * len(shape))
    out_spec = pl.BlockSpec((None, A_KV, n_chunks, HEAD_DIM), lambda i: (i, 0, 0, 0))
    out_shape = jax.ShapeDtypeStruct((b, A_KV, n_chunks, HEAD_DIM), BF16)
    return pl.pallas_call(
        functools.partial(_cmp_prompt_body, n_chunks=n_chunks),
        grid=(b,),
        in_specs=[pl.BlockSpec((None, t, 2 * A_KVW), lambda i: (i, 0, 0))] + _compress_specs(full),
        out_specs=[out_spec, out_spec],
        out_shape=[out_shape, out_shape],
        scratch_shapes=[pltpu.VMEM((2, 8, A_KVW), F32), pltpu.VMEM((2, t, A_KVW), F32)],
        compiler_params=pltpu.CompilerParams(
            dimension_semantics=("arbitrary",), vmem_limit_bytes=V7X_VMEM_LIMIT_BYTES),
        name="nsa_cmp_prompt",
    )(cmp_rows, *cmp_weights)


def _nsa_sample_cmp_body(pt_ref, *refs, n_pages):
    page_refs = refs[:n_pages]
    (posk_ref, posv_ref, wpos_k_ref, wpos_v_ref, wk_ref, wv_ref, w2k_ref, w2v_ref, kgain_ref,
     kc_ref, vc_ref, pos_scr, rows_scr) = refs[n_pages:]

    @pl.when(pl.program_id(0) == 0)
    def _():
        _pos_terms(pos_scr, posk_ref, posv_ref, wpos_k_ref, wpos_v_ref)

    n_chunks = n_pages * CHUNKS_PER_PAGE
    for kv in range(2):
        for p in range(n_pages):
            rows = page_refs[p][kv].reshape(A_KVW, PAGE_SIZE).T
            by_s = pltpu.einshape("csl->scl", rows.reshape(CHUNKS_PER_PAGE, CMP_STRIDE, A_KVW))
            for s in range(CMP_STRIDE):
                rows_scr[kv, s, p * CHUNKS_PER_PAGE:(p + 1) * CHUNKS_PER_PAGE, :] = by_s[s]

    compress = lambda kv, w_ref, w2_ref: _compress_mlp(lambda s: rows_scr[kv, s], n_chunks, w_ref,
                                                       pos_scr[kv, 0:1], w2_ref)
    kc_ref[...] = _rms_heads128(compress(0, wk_ref, w2k_ref), kgain_ref[...]).astype(BF16)
    vc_ref[...] = compress(1, wv_ref, w2v_ref).astype(BF16)


NSA_PICK_BATCH = 16


def _nsa_sample_pick_body(q_ref, kc_ref, vc_ref, cover_ref, oc_ref, idx_ref, *, n_cmp, n_sel):
    nb, n_chunks = kc_ref.shape[0], kc_ref.shape[1]
    col = lax.broadcasted_iota(jnp.int32, (8, n_chunks), 1)
    imps = []
    for i in range(nb):
        for g in range(A_KV):
            hs = slice(g * HEAD_DIM, (g + 1) * HEAD_DIM)
            q8 = (_pad_rows8(q_ref[i, g]) * (HEAD_DIM ** -0.5)).astype(BF16)
            s = jnp.where(col < n_cmp, _dot_nt(q8, kc_ref[i, :, hs]), NEG_INF)
            e = jnp.exp(s - jnp.max(s, axis=-1, keepdims=True))
            p = (e / jnp.sum(e, axis=-1, keepdims=True)).astype(BF16)
            oc_ref[i, g] = jnp.dot(p, vc_ref[i, :, hs], preferred_element_type=F32)[0:A_REP]
            imps.append(jnp.sum(jnp.dot(p, cover_ref[...], preferred_element_type=F32)[0:A_REP], axis=0, keepdims=True))
    imp = jnp.concatenate(imps, axis=0)
    rows = nb * A_KV
    s_idx = lax.broadcasted_iota(jnp.int32, (rows, SEL_PAD), 1)
    s_idx_f = s_idx.astype(F32)
    forced = (s_idx == 0) | (s_idx == n_sel - 1) | (s_idx == n_sel - 2)
    score = jnp.where(s_idx < n_sel, imp + jnp.where(forced, FORCE_BONUS, 0.0), NEG_INF)
    rank = jnp.zeros((rows, SEL_PAD), F32)
    for j in range(n_sel):
        cj = score[:, j:j + 1]
        beats = (cj > score) | ((cj == score) & (s_idx > j))
        rank = rank + jnp.where(beats, 1.0, 0.0)
    lane = lax.broadcasted_iota(jnp.int32, (rows, 128), 1)
    picks = jnp.zeros((rows, 128), F32)
    for r in range(SEL_TOPK):
        block = jnp.sum(jnp.where(rank == float(r), s_idx_f, 0.0), axis=-1, keepdims=True)
        picks = picks + jnp.where(lane == r, block, 0.0)
    idx_ref[...] = picks.astype(jnp.int32)


NSA_ATT_BATCH = 2


def _nsa_sample_att_body(idx_ref, pt_ref, *refs, n_past_blk):
    n_slots = NSA_ATT_BATCH * A_KV * SEL_TOPK
    slot_refs = refs[:n_slots]
    (q_all, gate_all, oc_all, newsel_all, wcache_all, newwin_all, newcol_all, selb_all, winb_all, newb_all,
     o_all, wout_all) = refs[n_slots:]
    for i in range(NSA_ATT_BATCH):
        b = pl.program_id(0) * NSA_ATT_BATCH + i
        for g in range(A_KV):
            first = (i * A_KV + g) * SEL_TOPK
            _nsa_sample_att_group(b, g, idx_ref, slot_refs[first:first + SEL_TOPK], q_all.at[i, g], gate_all.at[i, g],
                                  oc_all.at[i, g], newsel_all.at[i, g], wcache_all.at[i, :, g], newwin_all.at[i, g],
                                  selb_all.at[g], winb_all.at[g], newb_all.at[g], o_all.at[i, g], n_past_blk)
            for kv in range(2):
                wout_all[i, kv, g] = _shift_in(wcache_all[i, kv, g], newcol_all[i, g][:, kv:kv + 1])


def _shift_in(rows_last, new_col):
    w_len = rows_last.shape[-1]
    lane = lax.broadcasted_iota(jnp.int32, rows_last.shape, 1)
    return jnp.where(lane == w_len - 1, new_col, pltpu.roll(rows_last, w_len - 1, axis=1))


def _nsa_sample_att_group(b, g, idx_ref, page_refs, q_ref, gate_ref, oc_ref, newsel_ref, wcache_ref, newwin_ref,
                          selb_ref, winb_ref, newb_ref, o_ref, n_past_blk):
    blocks_per_page = PAGE_SIZE // SEL_BLOCK
    q8 = (_pad_rows8(q_ref[...]) * (HEAD_DIM ** -0.5)).astype(BF16)
    new_bias = newb_ref[:, 0:1]

    def attend(s, v_t, new_kv):
        k_new = _bf16_round(new_kv[0:1])
        v_new = _bf16_round(new_kv[1:2])
        s_new = jnp.sum(q8.astype(F32) * k_new, axis=-1, keepdims=True) + new_bias
        m = jnp.maximum(jnp.max(s, axis=-1, keepdims=True), s_new)
        e = jnp.exp(s - m)
        e_new = jnp.exp(s_new - m)
        den = jnp.sum(e, axis=-1, keepdims=True) + e_new
        return _dot_nt((e / den).astype(BF16), v_t) + _bf16_round(e_new / den) * v_new

    lane = lax.broadcasted_iota(jnp.int32, (8, PAGE_SIZE), 1)
    near = selb_ref[...]
    bias = []
    for k in range(SEL_TOPK):
        ik = idx_ref[(b * A_KV + g) * SEL_TOPK + k]
        blk = jnp.minimum(ik, n_past_blk - 1)
        near_k = jnp.where(blk // blocks_per_page == (n_past_blk - 1) // blocks_per_page, near, 0.0)
        keep = (lane // SEL_BLOCK == blk % blocks_per_page) & (ik < n_past_blk)
        bias.append(jnp.where(keep, near_k, NEG_INF))
    k_sel = jnp.concatenate([page_refs[k][0].astype(BF16) for k in range(SEL_TOPK)], axis=1)
    v_sel = jnp.concatenate([page_refs[k][1].astype(BF16) for k in range(SEL_TOPK)], axis=1)
    s_sel = jnp.dot(q8, k_sel, preferred_element_type=F32) + jnp.concatenate(bias, axis=1)
    o_s = attend(s_sel, v_sel, newsel_ref[...])

    w_len = wcache_ref.shape[-1]
    wcol = lax.broadcasted_iota(jnp.int32, (8, w_len), 1)
    s_w = jnp.dot(q8, wcache_ref[0].astype(BF16), preferred_element_type=F32) + winb_ref[...]
    o_w = attend(jnp.where(wcol >= 1, s_w, NEG_INF), wcache_ref[1].astype(BF16), newwin_ref[...])
    gates = gate_ref[...]
    o_ref[...] = gates[:, 0:1] * oc_ref[...] + gates[:, 1:2] * o_s[0:A_REP] + gates[:, 2:3] * o_w[0:A_REP]


def _rows_last(cache):
    nd = cache.ndim
    return jnp.transpose(cache, tuple(range(nd - 4)) + (nd - 3, nd - 2, nd - 1, nd - 4))


def _rows_first(cache):
    nd = cache.ndim
    return jnp.transpose(cache, tuple(range(nd - 4)) + (nd - 1, nd - 4, nd - 3, nd - 2))


def _nsa_sample(q, gates, new_sel, new_win, pool_cmp, pool_sel, win_cache, page_table, cmp_weights, tab):
    n, n_pages = page_table.shape
    past = n_pages * PAGE_SIZE
    n_past_blk = past // SEL_BLOCK
    n_sel = n_past_blk + 1
    n_chunks = n_pages * CHUNKS_PER_PAGE
    w_len = win_cache.shape[1]
    blocks_per_page = PAGE_SIZE // SEL_BLOCK
    assert n_sel <= SEL_PAD and w_len == WIN_A and past >= WIN_A and blocks_per_page == 2
    pt_flat = page_table.reshape(-1)

    c_idx = np.arange(n_chunks)
    s_start = np.arange(SEL_PAD) * SEL_BLOCK
    cover = ((c_idx[:, None] * CMP_STRIDE < s_start[None, :] + SEL_BLOCK)
             & (c_idx[:, None] * CMP_STRIDE + CMP_LEN - 1 >= s_start[None, :])
             & (c_idx[:, None] < n_chunks - 1) & (np.arange(SEL_PAD)[None, :] < n_sel))
    cover = jnp.asarray(cover.astype(np.float32), BF16)

    head_spec = lambda last: pl.BlockSpec((None, A_KV, A_REP, last), lambda i, *_: (i, 0, 0, 0))
    page_specs = [pl.BlockSpec((None, 2, A_KV, HEAD_DIM, PAGE_SIZE),
                               functools.partial(lambda i, pt, p: (pt[i * n_pages + p], 0, 0, 0, 0), p=p))
                  for p in range(n_pages)]
    full = lambda shape: pl.BlockSpec(shape, lambda i, *_: (0,) * len(shape))
    pool_c = _rows_last(pool_cmp)
    cmp_spec = pl.BlockSpec((None, n_chunks, A_KVW), lambda i, *_: (i, 0, 0))
    cmp_shape = jax.ShapeDtypeStruct((n, n_chunks, A_KVW), BF16)
    kc, vc = pl.pallas_call(
        functools.partial(_nsa_sample_cmp_body, n_pages=n_pages),
        grid_spec=pltpu.PrefetchScalarGridSpec(
            num_scalar_prefetch=1,
            grid=(n,),
            in_specs=page_specs + _compress_specs(full),
            out_specs=[cmp_spec, cmp_spec],
            scratch_shapes=[pltpu.VMEM((2, 8, A_KVW), F32), pltpu.VMEM((2, CMP_STRIDE, n_chunks, A_KVW), F32)],
        ),
        out_shape=[cmp_shape, cmp_shape],
        compiler_params=pltpu.CompilerParams(
            dimension_semantics=("arbitrary",), vmem_limit_bytes=V7X_VMEM_LIMIT_BYTES),
        name="nsa_sample_cmp",
    )(pt_flat, *([pool_c] * n_pages), *cmp_weights)

    nb = NSA_PICK_BATCH
    assert n % nb == 0
    batch = lambda *tail: pl.BlockSpec((nb,) + tail, lambda i: (i,) + (0,) * len(tail))
    o_c, idx = pl.pallas_call(
        functools.partial(_nsa_sample_pick_body, n_cmp=n_chunks - 1, n_sel=n_sel),
        grid=(n // nb,),
        in_specs=[batch(A_KV, A_REP, HEAD_DIM), batch(n_chunks, A_KVW), batch(n_chunks, A_KVW),
                  pl.BlockSpec((n_chunks, SEL_PAD), lambda i: (0, 0))],
        out_specs=[batch(A_KV, A_REP, HEAD_DIM), pl.BlockSpec((nb * A_KV, 128), lambda i: (i, 0))],
        out_shape=[jax.ShapeDtypeStruct((n, A_KV, A_REP, HEAD_DIM), F32),
                   jax.ShapeDtypeStruct((n * A_KV, 128), jnp.int32)],
        compiler_params=pltpu.CompilerParams(
            dimension_semantics=("arbitrary",), vmem_limit_bytes=V7X_VMEM_LIMIT_BYTES),
        name="nsa_sample_pick",
    )(q, kc, vc, cover)

    rel = lambda dist: jnp.pad(jnp.transpose(_bias_lookup(tab - tab[NUM_BUCKETS - 1], dist), (1, 2, 0)),
                               ((0, 0), (0, 8 - A_REP), (0, 0)))
    sel_bias = rel(past - (past - PAGE_SIZE + np.arange(PAGE_SIZE)))
    win_bias = rel(w_len - np.arange(w_len))
    new_bias = rel(np.zeros(128, np.int64))

    idx_flat = idx[:, :SEL_TOPK].reshape(-1)

    nba = NSA_ATT_BATCH
    assert n % nba == 0

    def page_map(i, idx_s, pt, seq, g, slot):
        b = i * nba + seq
        blk = jnp.minimum(idx_s[(b * A_KV + g) * SEL_TOPK + slot], n_past_blk - 1)
        return (pt[b * n_pages + blk // blocks_per_page], 0, g, 0, 0)

    slot_specs = [pl.BlockSpec((None, 2, None, HEAD_DIM, PAGE_SIZE), functools.partial(page_map, seq=e, g=g, slot=s))
                  for e in range(nba) for g in range(A_KV) for s in range(SEL_TOPK)]
    per_head = lambda rows, last: pl.BlockSpec((nba, A_KV, rows, last), lambda i, *_: (i, 0, 0, 0))
    per_group = lambda last: pl.BlockSpec((A_KV, 8, last), lambda i, *_: (0, 0, 0))
    win_spec = pl.BlockSpec((nba, 2, A_KV, HEAD_DIM, w_len), lambda i, *_: (i, 0, 0, 0, 0))
    o, win_next = pl.pallas_call(
        functools.partial(_nsa_sample_att_body, n_past_blk=n_past_blk),
        grid_spec=pltpu.PrefetchScalarGridSpec(
            num_scalar_prefetch=2,
            grid=(n // nba,),
            in_specs=slot_specs + [
                per_head(A_REP, HEAD_DIM), per_head(A_REP, 3), per_head(A_REP, HEAD_DIM), per_head(2, HEAD_DIM),
                win_spec, per_head(2, HEAD_DIM), per_head(HEAD_DIM, 2),
                per_group(PAGE_SIZE), per_group(w_len), per_group(128),
            ],
            out_specs=[per_head(A_REP, HEAD_DIM), win_spec],
        ),
        out_shape=[jax.ShapeDtypeStruct((n, A_KV, A_REP, HEAD_DIM), F32),
                   jax.ShapeDtypeStruct((n, 2, A_KV, HEAD_DIM, w_len), F32)],
        compiler_params=pltpu.CompilerParams(
            dimension_semantics=("arbitrary",), vmem_limit_bytes=V7X_VMEM_LIMIT_BYTES),
        name="nsa_sample_att",
    )(idx_flat, pt_flat, *([_rows_last(pool_sel)] * (nba * A_KV * SEL_TOPK)), q, gates, o_c,
      jnp.transpose(new_sel, (0, 2, 1, 3)), _rows_last(win_cache), jnp.transpose(new_win, (0, 2, 1, 3)),
      jnp.transpose(new_win, (0, 2, 3, 1)), sel_bias, win_bias, new_bias)
    return o, _rows_first(win_next)


SWA_Q_TILE = WIN_C


def _swa_prompt_body(q_ref, kv_ref, btile_ref, sink_ref, o_ref, *, tq):
    qt = pl.program_id(1)
    q0 = pl.multiple_of(qt * tq, tq)
    prev0 = pl.multiple_of(jnp.maximum(q0 - tq, 0), tq)
    width = C_REP * HEAD_DIM
    for g in range(C_KV):
        q = q_ref[:, g * width:(g + 1) * width]
        qs = jnp.concatenate([q[:, r * HEAD_DIM:(r + 1) * HEAD_DIM] for r in range(C_REP)], axis=0)
        qs = (qs * (HEAD_DIM ** -0.5)).astype(BF16)
        k_ref, v_ref = kv_ref.at[g], kv_ref.at[C_KV + g]
        k = jnp.concatenate([k_ref[pl.ds(prev0, tq), :], k_ref[pl.ds(q0, tq), :]], axis=0)
        v = jnp.concatenate([v_ref[pl.ds(prev0, tq), :], v_ref[pl.ds(q0, tq), :]], axis=0)
        s = _dot_nt(k, qs) + btile_ref[g]
        sinks = sink_ref[g]
        sink = jnp.concatenate([sinks[r:r + 1, :] for r in range(C_REP)], axis=1)
        m = jnp.maximum(jnp.max(s, axis=0, keepdims=True), sink)
        e = jnp.exp(s - m)
        p = e / (jnp.sum(e, axis=0, keepdims=True) + jnp.exp(sink - m))
        o = _dot_tn(v, p.astype(BF16))
        o_ref[:, g * width:(g + 1) * width] = jnp.concatenate([o[:, r * tq:(r + 1) * tq].T for r in range(C_REP)],
                                                              axis=1)


def _swa_prompt(q, kvt, sinks, tab):
    b, t, _ = q.shape
    tq = SWA_Q_TILE
    btile = jnp.swapaxes(_near_bias_tiles(tab, tq, WIN_C), -1, -2)
    qo_spec = pl.BlockSpec((None, tq, C_HEADS * HEAD_DIM), lambda i, j: (i, j, 0))
    sink_lanes = jnp.broadcast_to(sinks.reshape(C_KV, C_REP, 1), (C_KV, C_REP, 128))
    return pl.pallas_call(
        functools.partial(_swa_prompt_body, tq=tq),
        grid=(b, t // tq),
        in_specs=[qo_spec,
                  pl.BlockSpec((None, 2 * C_KV, t, HEAD_DIM), lambda i, j: (i, 0, 0, 0)),
                  pl.BlockSpec((None, C_KV, 2 * tq, C_REP * tq), lambda i, j: (jnp.minimum(j, 1), 0, 0, 0)),
                  pl.BlockSpec((C_KV, C_REP, 128), lambda i, j: (0, 0, 0))],
        out_specs=qo_spec,
        out_shape=jax.ShapeDtypeStruct((b, t, C_HEADS * HEAD_DIM), F32),
        compiler_params=pltpu.CompilerParams(
            dimension_semantics=("parallel", "arbitrary"), vmem_limit_bytes=V7X_VMEM_LIMIT_BYTES),
        name="swa_prompt",
    )(q, kvt, btile, sink_lanes)


SWA_DECODE_BATCH = 4


def _swa_sample_body(q_ref, cache_ref, new_ref, newcol_ref, bias_ref, newb_ref, sink_ref, o_ref, next_ref):
    w_len = cache_ref.shape[-1]
    wcol = lax.broadcasted_iota(jnp.int32, (C_REP, w_len), 1)
    for i in range(q_ref.shape[0]):
        for g in range(C_KV):
            q8 = (q_ref[i, g] * (HEAD_DIM ** -0.5)).astype(BF16)
            s = jnp.dot(q8, cache_ref[i, 0, g].astype(BF16), preferred_element_type=F32) + bias_ref[g]
            s = jnp.where(wcol >= 1, s, NEG_INF)
            k_new = _bf16_round(new_ref[i, g, 0:1])
            v_new = _bf16_round(new_ref[i, g, 1:2])
            s_new = jnp.sum(q8.astype(F32) * k_new, axis=-1, keepdims=True) + newb_ref[g][:, 0:1]
            sink = sink_ref[g][:, 0:1]
            m = jnp.maximum(jnp.maximum(jnp.max(s, axis=-1, keepdims=True), s_new), sink)
            e = jnp.exp(s - m)
            e_new = jnp.exp(s_new - m)
            den = jnp.sum(e, axis=-1, keepdims=True) + e_new + jnp.exp(sink - m)
            o_ref[i, g] = (_dot_nt((e / den).astype(BF16), cache_ref[i, 1, g].astype(BF16))
                           + _bf16_round(e_new / den) * v_new)
            for kv in range(2):
                next_ref[i, kv, g] = _shift_in(cache_ref[i, kv, g], newcol_ref[i, g][:, kv:kv + 1])


def _swa_sample(q, cache, new_kv, sinks, tab):
    n, w_len = cache.shape[:2]
    bias = jnp.transpose(_bias_lookup(tab, w_len - np.arange(w_len)), (1, 2, 0))
    lanes = lambda x: jnp.broadcast_to(x[:, :, None], (C_KV, C_REP, 128))
    full = lambda shape: pl.BlockSpec(shape, lambda i: (0,) * len(shape))
    nb = SWA_DECODE_BATCH
    assert n % nb == 0
    cache_spec = pl.BlockSpec((nb, 2, C_KV, HEAD_DIM, w_len), lambda i: (i, 0, 0, 0, 0))
    o_spec = pl.BlockSpec((nb, C_KV, C_REP, HEAD_DIM), lambda i: (i, 0, 0, 0))
    o, cache_next = pl.pallas_call(
        _swa_sample_body,
        grid=(n // nb,),
        in_specs=[
            o_spec, cache_spec,
            pl.BlockSpec((nb, C_KV, 2, HEAD_DIM), lambda i: (i, 0, 0, 0)),
            pl.BlockSpec((nb, C_KV, HEAD_DIM, 2), lambda i: (i, 0, 0, 0)),
            full((C_KV, C_REP, w_len)), full((C_KV, C_REP, 128)), full((C_KV, C_REP, 128)),
        ],
        out_specs=[o_spec, cache_spec],
        out_shape=[jax.ShapeDtypeStruct((n, C_KV, C_REP, HEAD_DIM), F32),
                   jax.ShapeDtypeStruct((n, 2, C_KV, HEAD_DIM, w_len), F32)],
        compiler_params=pltpu.CompilerParams(
            dimension_semantics=("arbitrary",), vmem_limit_bytes=V7X_VMEM_LIMIT_BYTES),
        name="swa_sample",
    )(q, _rows_last(cache), new_kv, jnp.transpose(new_kv, (0, 1, 3, 2)), bias, lanes(tab[0]),
      lanes(sinks.reshape(C_KV, C_REP)))
    return o, _rows_first(cache_next)


PROJ_ROW_TILE = 512
A_IN_PAD = 1920
A_U_COL = A_Q + 6 * A_KVW
A_GATE_COL = A_U_COL + B_WIDTH


def _heads_first(kvt_ref, slot, k, v, n_kv):
    for g in range(n_kv):
        kvt_ref[slot + g] = k[:, g * HEAD_DIM:(g + 1) * HEAD_DIM].astype(BF16)
        kvt_ref[slot + n_kv + g] = v[:, g * HEAD_DIM:(g + 1) * HEAD_DIM].astype(BF16)


def _inproj_a_body(x_ref, gain_ref, w_ref, qg_ref, kg_ref,
                   q_ref, cmp_ref, sel_ref, win_ref, gate_ref, u_ref, kvt_ref):
    xn = _rms_rows(x_ref[...], gain_ref[...]).astype(BF16)
    z = jnp.dot(xn, w_ref[...], preferred_element_type=F32)
    for j in range(A_Q // 128):
        q_ref[:, j * 128:(j + 1) * 128] = _rms_heads128(z[:, j * 128:(j + 1) * 128], qg_ref[...])
    for out_ref, off, slot in ((cmp_ref, A_Q, None), (sel_ref, A_Q + 2 * A_KVW, 0), (win_ref, A_Q + 4 * A_KVW, 4)):
        k = _rms_heads128(z[:, off:off + A_KVW], kg_ref[...])
        v = z[:, off + A_KVW:off + 2 * A_KVW]
        out_ref[:, :A_KVW] = k
        out_ref[:, A_KVW:] = v
        if slot is not None:
            _heads_first(kvt_ref, slot, k, v, A_KV)
    u_ref[...] = z[:, A_U_COL:A_GATE_COL]
    gate_ref[...] = 1.0 / (1.0 + jnp.exp(-z[:, A_GATE_COL:A_IN_PAD]))


def _inproj_a(x, gain, w_in, q_gain, k_gain):
    b, t, d = x.shape
    tm = min(t, PROJ_ROW_TILE)
    w = jnp.concatenate([w_in[:, :A_U_COL], w_in[:, A_U_COL + A_GATE:], w_in[:, A_U_COL:A_U_COL + A_GATE],
                         jnp.zeros((d, A_IN_PAD - A_GATE_COL - A_GATE), F32)], axis=1).astype(BF16)
    tile2 = lambda g: jnp.tile(g, 2).reshape(1, 128)
    rows = lambda width: pl.BlockSpec((None, tm, width), lambda i, j: (i, j, 0))
    shape = lambda width: jax.ShapeDtypeStruct((b, t, width), F32)
    return pl.pallas_call(
        _inproj_a_body,
        grid=(b, t // tm),
        in_specs=[rows(d), _const_spec((1, d)), _const_spec((d, A_IN_PAD)), _const_spec((1, 128)), _const_spec((1, 128))],
        out_specs=[rows(A_Q), rows(2 * A_KVW), rows(2 * A_KVW), rows(2 * A_KVW), rows(128), rows(B_WIDTH),
                   pl.BlockSpec((None, 8, tm, HEAD_DIM), lambda i, j: (i, 0, j, 0))],
        out_shape=[shape(A_Q), shape(2 * A_KVW), shape(2 * A_KVW), shape(2 * A_KVW), shape(128), shape(B_WIDTH),
                   jax.ShapeDtypeStruct((b, 8, t, HEAD_DIM), BF16)],
        compiler_params=pltpu.CompilerParams(
            dimension_semantics=("parallel", "parallel"), vmem_limit_bytes=V7X_VMEM_LIMIT_BYTES),
        name="inproj_nsa_s5",
    )(x, gain.reshape(1, d), w, tile2(q_gain), tile2(k_gain))


def _inproj_c_body(x_ref, gain_ref, w_ref, qg_ref, kg_ref, q_ref, kv_ref, kvt_ref):
    xn = _rms_rows(x_ref[...], gain_ref[...]).astype(BF16)
    z = jnp.dot(xn, w_ref[...], preferred_element_type=F32)
    n_q = C_HEADS * HEAD_DIM
    for j in range(n_q // 128):
        q_ref[:, j * 128:(j + 1) * 128] = _rms_heads128(z[:, j * 128:(j + 1) * 128], qg_ref[...])
    k = _rms_heads128(z[:, n_q:n_q + C_KV * HEAD_DIM], kg_ref[...])
    v = z[:, n_q + C_KV * HEAD_DIM:]
    kv_ref[:, :C_KV * HEAD_DIM] = k
    kv_ref[:, C_KV * HEAD_DIM:] = v
    _heads_first(kvt_ref, 0, k, v, C_KV)


def _inproj_c(x, gain, w_in, q_gain, k_gain):
    b, t, d = x.shape
    tm = min(t, PROJ_ROW_TILE)
    n_in = w_in.shape[1]
    tile2 = lambda g: jnp.tile(g, 2).reshape(1, 128)
    rows = lambda width: pl.BlockSpec((None, tm, width), lambda i, j: (i, j, 0))
    shape = lambda width: jax.ShapeDtypeStruct((b, t, width), F32)
    return pl.pallas_call(
        _inproj_c_body,
        grid=(b, t // tm),
        in_specs=[rows(d), _const_spec((1, d)), _const_spec((d, n_in)), _const_spec((1, 128)), _const_spec((1, 128))],
        out_specs=[rows(C_HEADS * HEAD_DIM), rows(2 * C_KV * HEAD_DIM),
                   pl.BlockSpec((None, 2 * C_KV, tm, HEAD_DIM), lambda i, j: (i, 0, j, 0))],
        out_shape=[shape(C_HEADS * HEAD_DIM), shape(2 * C_KV * HEAD_DIM),
                   jax.ShapeDtypeStruct((b, 2 * C_KV, t, HEAD_DIM), BF16)],
        compiler_params=pltpu.CompilerParams(
            dimension_semantics=("parallel", "parallel"), vmem_limit_bytes=V7X_VMEM_LIMIT_BYTES),
        name="inproj_swa",
    )(x, gain.reshape(1, d), w_in.astype(BF16), tile2(q_gain), tile2(k_gain))


FFN_ROW_TILE = 512
FFN_COL_CHUNK = 1408


def _mixer_residual(y_ref, x_ref, mix_refs, wout_refs):
    y_ref[...] = x_ref[...]
    for m_ref, w_ref in zip(mix_refs, wout_refs):
        y_ref[...] += jnp.dot(m_ref[...].astype(BF16), w_ref[...], preferred_element_type=F32)
    return y_ref[...]


def _tail_prompt_body(*refs, n_mix, tm, ffc):
    x_ref = refs[0]
    mix_refs = refs[1:1 + n_mix]
    wout_refs = refs[1 + n_mix:1 + 2 * n_mix]
    (gain_ref, prev_ref, wup_ref, wgate_ref, cw_ref, cb_ref, wdown_ref, y_ref, cs_ref, hbuf_ref) = refs[1 + 2 * n_mix:]
    t = pl.program_id(1)
    xn = _rms_rows(_mixer_residual(y_ref, x_ref, mix_refs, wout_refs), gain_ref[...]).astype(BF16)
    for c in range(D_FF // ffc):
        lo = c * ffc
        h = jnp.dot(xn, wup_ref[:, lo:lo + ffc], preferred_element_type=F32)
        g = jnp.dot(xn, wgate_ref[:, lo:lo + ffc], preferred_element_type=F32)

        @pl.when(t == 0)
        def _():
            hbuf_ref[c, 6:8, :] = prev_ref[:, lo:lo + ffc]

        hbuf_ref[c, 8:8 + tm, :] = h
        hm1 = hbuf_ref[c, 7:7 + tm, :]
        hm2 = hbuf_ref[c, 6:6 + tm, :]
        cw = cw_ref[:, lo:lo + ffc]
        hc = cw[0:1] * hm2 + cw[1:2] * hm1 + cw[2:3] * h + cb_ref[:, lo:lo + ffc]
        a = (_gelu_tanh(hc) * g).astype(BF16)
        y_ref[...] += jnp.dot(a, wdown_ref[lo:lo + ffc, :], preferred_element_type=F32)
        hbuf_ref[c, 0:8, :] = h[tm - 8:tm, :]
        cs_ref[:, lo:lo + ffc] = h[tm - 2:tm, :]


def _tail_prompt(x, mixes, wouts, gain, prev, wup, wgate, cw, cb, wdown):
    b, t, d = x.shape
    tm, ffc = FFN_ROW_TILE, FFN_COL_CHUNK
    rows = lambda width: pl.BlockSpec((None, tm, width), lambda i, j: (i, j, 0))
    state = pl.BlockSpec((None, CONV_W - 1, D_FF), lambda i, j: (i, 0, 0))
    return pl.pallas_call(
        functools.partial(_tail_prompt_body, n_mix=len(mixes), tm=tm, ffc=ffc),
        grid=(b, t // tm),
        in_specs=[rows(d)] + [rows(m.shape[-1]) for m in mixes] + [_const_spec(w.shape) for w in wouts] + [
            _const_spec((1, d)), state, _const_spec((d, D_FF)), _const_spec((d, D_FF)),
            _const_spec((CONV_W, D_FF)), _const_spec((1, D_FF)), _const_spec((D_FF, d))],
        out_specs=[rows(d), state],
        out_shape=[jax.ShapeDtypeStruct((b, t, d), F32), jax.ShapeDtypeStruct((b, CONV_W - 1, D_FF), F32)],
        scratch_shapes=[pltpu.VMEM((D_FF // ffc, 8 + tm, ffc), F32)],
        compiler_params=pltpu.CompilerParams(
            dimension_semantics=("parallel", "arbitrary"), vmem_limit_bytes=V7X_VMEM_LIMIT_BYTES),
        name="tail_prompt",
    )(x, *mixes, *wouts, gain, prev, wup, wgate, cw, cb, wdown)


def _tail_sample_body(*refs, n_mix, ffc):
    x_ref = refs[0]
    mix_refs = refs[1:1 + n_mix]
    wout_refs = refs[1 + n_mix:1 + 2 * n_mix]
    (gain_ref, prev_ref, wup_ref, wgate_ref, cw_ref, cb_ref, wdown_ref, y_ref, cs_ref) = refs[1 + 2 * n_mix:]
    xn = _rms_rows(_mixer_residual(y_ref, x_ref, mix_refs, wout_refs), gain_ref[...]).astype(BF16)
    for c in range(D_FF // ffc):
        lo = c * ffc
        h = jnp.dot(xn, wup_ref[:, lo:lo + ffc], preferred_element_type=F32)
        g = jnp.dot(xn, wgate_ref[:, lo:lo + ffc], preferred_element_type=F32)
        hm2 = prev_ref[:, lo:lo + ffc]
        hm1 = prev_ref[:, D_FF + lo:D_FF + lo + ffc]
        cw = cw_ref[:, lo:lo + ffc]
        hc = cw[0:1] * hm2 + cw[1:2] * hm1 + cw[2:3] * h + cb_ref[:, lo:lo + ffc]
        a = (_gelu_tanh(hc) * g).astype(BF16)
        y_ref[...] += jnp.dot(a, wdown_ref[lo:lo + ffc, :], preferred_element_type=F32)
        cs_ref[:, lo:lo + ffc] = hm1
        cs_ref[:, D_FF + lo:D_FF + lo + ffc] = h


def _tail_sample(x, mixes, wouts, gain, prev, wup, wgate, cw, cb, wdown):
    n, d = x.shape
    full = lambda shape: _const_spec(shape)
    return pl.pallas_call(
        functools.partial(_tail_sample_body, n_mix=len(mixes), ffc=FFN_COL_CHUNK),
        grid=(1,),
        in_specs=[full((n, d))] + [full(m.shape) for m in mixes] + [full(w.shape) for w in wouts] + [
            full((1, d)), full((n, (CONV_W - 1) * D_FF)), full((d, D_FF)), full((d, D_FF)),
            full((CONV_W, D_FF)), full((1, D_FF)), full((D_FF, d))],
        out_specs=[pl.BlockSpec((n, d), lambda i: (0, 0)), pl.BlockSpec((n, (CONV_W - 1) * D_FF), lambda i: (0, 0))],
        out_shape=[jax.ShapeDtypeStruct((n, d), F32), jax.ShapeDtypeStruct((n, (CONV_W - 1) * D_FF), F32)],
        compiler_params=pltpu.CompilerParams(
            dimension_semantics=("arbitrary",), vmem_limit_bytes=V7X_VMEM_LIMIT_BYTES),
        name="tail_sample",
    )(x, *mixes, *wouts, gain, prev, wup, wgate, cw, cb, wdown)


def kernel(x_prompt, x_sample, cache_nsa_cmp, cache_nsa_sel, cache_nsa_win, state_s5_re, state_s5_im,
           cache_swa, state_ffn_conv, page_table, rel_bias, norm_mix, norm_ffn, a_w_in, a_w_out,
           nsa_q_gain, nsa_k_gain, nsa_cmp_pos, nsa_cmp_w1, nsa_cmp_w2, s5_a_re, s5_a_im, s5_log_dt,
           s5_b_re, s5_b_im, s5_c_re, s5_c_im, s5_d, s5_w_glu, s5_b_glu, c_w_in, c_w_out, c_q_gain,
           c_k_gain, c_sinks, ffn_w_up, ffn_w_gate, ffn_conv_w, ffn_conv_b, ffn_w_down):
    bp, tp, _ = x_prompt.shape
    bs, ts, _ = x_sample.shape
    assert ts == 1 and DEPTH == 2
    tab_a = rel_bias[:, :A_HEADS].reshape(NUM_BUCKETS, A_KV, A_REP)
    tab_c = rel_bias[:, :C_HEADS].reshape(NUM_BUCKETS, C_KV, C_REP)
    kv6 = lambda x: x.reshape(x.shape[:-1] + (2, x.shape[-1] // (2 * HEAD_DIM), HEAD_DIM))
    hp, hs = x_prompt, x_sample.reshape(1, bs, D_MODEL)
    conv_p, conv_s = [], []

    def tail(layer, hp, hs, mixes_p, mixes_s, wouts):
        wouts = [w.astype(BF16) for w in wouts]
        ffn = (norm_ffn[layer].reshape(1, D_MODEL),)
        wts = (ffn_w_up[layer].astype(BF16), ffn_w_gate[layer].astype(BF16), ffn_conv_w[layer],
               ffn_conv_b[layer].reshape(1, D_FF), ffn_w_down[layer].astype(BF16))
        hp, cp = _tail_prompt(hp, mixes_p, wouts, *ffn, jnp.zeros((bp, CONV_W - 1, D_FF), F32), *wts)
        hs2, cs = _tail_sample(hs[0], [m.reshape(bs, -1) for m in mixes_s], wouts, *ffn,
                               state_ffn_conv[layer].reshape(bs, (CONV_W - 1) * D_FF), *wts)
        conv_p.append(cp)
        conv_s.append(cs.reshape(bs, CONV_W - 1, D_FF))
        return hp, hs2.reshape(1, bs, D_MODEL)

    proj = (norm_mix[0], a_w_in[0], nsa_q_gain[0], nsa_k_gain[0])
    qp, cmp_p, sel_p, win_p, gate_p, up, kvt_p = _inproj_a(hp, *proj)
    qs, cmp_s, sel_s, win_s, gate_s, us, _ = _inproj_a(hs, *proj)
    cmp_w = _compress_weights(nsa_cmp_pos[0], nsa_cmp_w1[0], nsa_cmp_w2[0], nsa_k_gain[0])
    kc, vc = _cmp_prompt(cmp_p, cmp_w)
    o_ap = _nsa_prompt_attend(qp, gate_p, kc, vc, kvt_p, tab_a)
    o_as, nsa_win_s = _nsa_sample(qs.reshape(bs, A_KV, A_REP, HEAD_DIM),
                                  gate_s[0, :, :A_GATE].reshape(bs, A_KV, A_REP, 3),
                                  sel_s.reshape(bs, 2, A_KV, HEAD_DIM), win_s.reshape(bs, 2, A_KV, HEAD_DIM),
                                  cache_nsa_cmp[0], cache_nsa_sel[0], cache_nsa_win[0], page_table, cmp_w, tab_a)
    s5w = _s5_weights(s5_a_re[0], s5_a_im[0], s5_log_dt[0], s5_b_re[0], s5_b_im[0], s5_c_re[0], s5_c_im[0],
                      s5_d[0], s5_w_glu[0], s5_b_glu[0])
    h0 = jnp.zeros((bp, S5_NS), F32)
    o_bp, hr_p, hi_p = _s5_mix(up, h0, h0, s5w)
    o_bs, hr_s, hi_s = _s5_mix(us.reshape(bs, 1, B_WIDTH), state_s5_re[0].reshape(bs, S5_NS),
                               state_s5_im[0].reshape(bs, S5_NS), s5w)
    hp, hs = tail(0, hp, hs, [o_ap, o_bp], [o_as, o_bs], [a_w_out[0][:A_Q], a_w_out[0][A_Q:]])

    proj = (norm_mix[1], c_w_in[0], c_q_gain[0], c_k_gain[0])
    qcp, kv_p, kvt_c = _inproj_c(hp, *proj)
    qcs, kv_s, _ = _inproj_c(hs, *proj)
    o_cp = _swa_prompt(qcp, kvt_c, c_sinks[0], tab_c)
    new_kv = kv_s.reshape(bs, 2, C_KV, HEAD_DIM)
    o_cs, swa_s = _swa_sample(qcs.reshape(bs, C_KV, C_REP, HEAD_DIM), cache_swa[0],
                              jnp.transpose(new_kv, (0, 2, 1, 3)), c_sinks[0], tab_c)
    hp, hs = tail(1, hp, hs, [o_cp], [o_cs], [c_w_out[0]])

    state = lambda x, n: x.reshape(1, n, B_GROUPS, B_STATE)
    return (hp, hs.reshape(bs, ts, D_MODEL),
            kv6(cmp_p)[None], kv6(cmp_s).reshape(1, bs, ts, 2, A_KV, HEAD_DIM),
            kv6(sel_p)[None], kv6(sel_s).reshape(1, bs, ts, 2, A_KV, HEAD_DIM),
            kv6(win_p)[None, :, -min(WIN_A, tp):], nsa_win_s[None],
            state(hr_p, bp), state(hi_p, bp), state(hr_s, bs), state(hi_s, bs),
            kv6(kv_p)[None, :, -min(WIN_C, tp):], swa_s[None],
            jnp.stack(conv_p), jnp.stack(conv_s))
```

```python
import functools
import math

import jax
import jax.numpy as jnp
import numpy as np
from jax import lax
from jax.experimental import pallas as pl
from jax.experimental.pallas import tpu as pltpu

D_MODEL = 1024
DEPTH = 2
PAGE_SIZE = 128
HEAD_DIM = 64
A_HEADS = 8
A_KV = 2
A_REP = A_HEADS // A_KV
A_Q = A_HEADS * HEAD_DIM
A_KVW = A_KV * HEAD_DIM
A_GATE = 3 * A_HEADS
CMP_LEN = 32
CMP_STRIDE = 16
SEL_BLOCK = 64
SEL_TOPK = 16
WIN_A = 512
FORCE_BONUS = 1000.0
B_WIDTH = D_MODEL // 2
B_GROUP = 16
B_GROUPS = B_WIDTH // B_GROUP
B_STATE = 64
C_HEADS = D_MODEL // HEAD_DIM
C_KV = 2
C_REP = C_HEADS // C_KV
WIN_C = 128
NUM_BUCKETS = 32
MAX_DISTANCE = 128
D_FF = 2816
CONV_W = 3
EPS = 1e-6

F32 = jnp.float32
BF16 = jnp.bfloat16

V7X_VMEM_LIMIT_BYTES = 56 * 1024 * 1024


def _gelu_tanh(x):
    c = math.sqrt(2.0 / math.pi)
    return 0.5 * x * (1.0 + jnp.tanh(x * (c + (c * 0.044715) * (x * x))))


def _rms_rows(x, gain):
    return x * lax.rsqrt(jnp.mean(x * x, axis=-1, keepdims=True) + EPS) * gain


def _const_spec(shape):
    zeros = (0,) * len(shape)
    return pl.BlockSpec(shape, lambda *_: zeros, pipeline_mode=pl.Buffered(1))


NSA_Q_TILE = 128
NSA_FAR_TILE = 512
NSA_HEADS_PER_PASS = 4
NEG_INF = float("-inf")


def _dot_nt(a, b):
    return lax.dot_general(a, b, (((1,), (1,)), ((), ())), preferred_element_type=F32)


def _dot_tn(a, b):
    return lax.dot_general(a, b, (((0,), (0,)), ((), ())), preferred_element_type=F32)


def _softmax_start(s, v):
    m = jnp.max(s, axis=0, keepdims=True)
    e = jnp.exp(s - m)
    return m, jnp.sum(e, axis=0, keepdims=True), _dot_tn(v, e.astype(BF16))


def _softmax_more(carry, s, v):
    m, l, acc = carry
    m_new = jnp.maximum(m, jnp.max(s, axis=0, keepdims=True))
    alpha = jnp.exp(m - m_new)
    e = jnp.exp(s - m_new)
    return m_new, alpha * l + jnp.sum(e, axis=0, keepdims=True), alpha * acc + _dot_tn(v, e.astype(BF16))


def _nsa_prompt_body(q_ref, gate_ref, kc_ref, vc_ref, kvt_ref, btile_ref, cover_ref, o_ref, sel_scr, *, tq):
    width = A_REP * HEAD_DIM
    gates_t = gate_ref[...].T
    for g in range(A_KV):
        _nsa_prompt_group(g, q_ref.at[:, g * width:(g + 1) * width], gates_t, kc_ref.at[g], vc_ref.at[g],
                          kvt_ref.at[g], kvt_ref.at[A_KV + g], kvt_ref.at[2 * A_KV + g], kvt_ref.at[3 * A_KV + g],
                          btile_ref.at[g], cover_ref, o_ref.at[:, g * width:(g + 1) * width], sel_scr.at[g], tq)


def _nsa_prompt_group(g, q_ref, gates_t, kc_ref, vc_ref, ks_ref, vs_ref, kw_ref, vw_ref, btile_ref, cover_ref,
                      o_ref, sel_ref, tq):
    qt = pl.program_id(1)
    q0 = pl.multiple_of(qt * tq, tq)
    cols = A_REP * tq
    q = q_ref[...]
    qs = jnp.concatenate([q[:, r * HEAD_DIM:(r + 1) * HEAD_DIM] for r in range(A_REP)], axis=0)
    qs = (qs * (HEAD_DIM ** -0.5)).astype(BF16)

    def q_pos(height):
        return q0 + (lax.broadcasted_iota(jnp.int32, (height, cols), 1) & (tq - 1))

    def key_idx(height):
        return lax.broadcasted_iota(jnp.int32, (height, cols), 0)

    n_idx = key_idx(128)
    valid_c = (n_idx * CMP_STRIDE + (CMP_LEN - 1) <= q_pos(128)) & (n_idx < 127)
    s_c = jnp.where(valid_c, _dot_nt(kc_ref[...], qs), NEG_INF)
    m_c = jnp.max(s_c, axis=0, keepdims=True)
    m_c = jnp.where(m_c == NEG_INF, 0.0, m_c)
    e_c = jnp.exp(s_c - m_c)
    d_c = jnp.sum(e_c, axis=0, keepdims=True)
    p_c = (e_c / jnp.where(d_c > 0, d_c, 1.0)).astype(BF16)
    o_c = _dot_tn(vc_ref[...], p_c)
    imp_heads = jnp.dot(cover_ref[...], p_c, preferred_element_type=F32)
    imp = sum(imp_heads[0:32, r * tq:(r + 1) * tq] for r in range(A_REP))
    s_idx = lax.broadcasted_iota(jnp.int32, (32, tq), 0)
    qblk = (q0 + lax.broadcasted_iota(jnp.int32, (32, tq), 1)) >> 6
    forced = (s_idx == 0) | (s_idx == qblk) | (s_idx == qblk - 1)
    allowed = s_idx <= qblk
    score = jnp.where(allowed, imp + jnp.where(forced, FORCE_BONUS, 0.0), NEG_INF)
    rank = jnp.zeros((32, tq), F32)
    for j in range(32):
        other = score[j:j + 1, :]
        beats = (other > score) | ((other == score) & (s_idx > j))
        rank = rank + jnp.where(beats, 1.0, 0.0)
    sel = jnp.where((rank < SEL_TOPK) & allowed, 1.0, 0.0)
    hcols = NSA_HEADS_PER_PASS * tq
    far_end = q0 - tq
    sel_far_only = jnp.where(s_idx < far_end // SEL_BLOCK, sel, 0.0)
    sel_ref[...] = jnp.concatenate([jnp.concatenate([sel] * NSA_HEADS_PER_PASS, axis=1),
                                    jnp.concatenate([sel_far_only] * NSA_HEADS_PER_PASS, axis=1)], axis=0)

    def block_mask(k0, n_blocks, base=0):
        first = base + k0 // SEL_BLOCK
        return jnp.concatenate([jnp.broadcast_to(sel_ref[pl.ds(first + j, 1), :], (SEL_BLOCK, hcols))
                                for j in range(n_blocks)], axis=0) > 0.5

    def hq_pos(height):
        return q0 + (lax.broadcasted_iota(jnp.int32, (height, hcols), 1) & (tq - 1))

    def hkey_idx(height):
        return lax.broadcasted_iota(jnp.int32, (height, hcols), 0)

    prev0 = pl.multiple_of(jnp.maximum(q0 - tq, 0), tq)
    sel_near = jnp.concatenate([block_mask(prev0, tq // SEL_BLOCK), block_mask(q0, tq // SEL_BLOCK)], axis=0)
    n_far = (jnp.maximum(far_end, 0) + NSA_FAR_TILE - 1) // NSA_FAR_TILE
    w_far = WIN_A - tq
    wf0 = pl.multiple_of(jnp.maximum(q0 - WIN_A, 0), tq)
    wpos = wf0 + hkey_idx(w_far)
    wmask = (hq_pos(w_far) - wpos < WIN_A) & (wpos < far_end)

    outs = []
    for h0 in range(0, A_REP, NSA_HEADS_PER_PASS):
        csl = slice(h0 * tq, h0 * tq + hcols)
        qh = qs[h0 * tq:h0 * tq + hcols]
        btile = btile_ref[:, csl]

        def near(k_ref, v_ref, extra_mask):
            k = jnp.concatenate([k_ref[pl.ds(prev0, tq), :], k_ref[pl.ds(q0, tq), :]], axis=0)
            v = jnp.concatenate([v_ref[pl.ds(prev0, tq), :], v_ref[pl.ds(q0, tq), :]], axis=0)
            s = _dot_nt(k, qh) + btile
            return _softmax_start(s if extra_mask is None else jnp.where(extra_mask, s, NEG_INF), v)

        def sel_far(i, carry):
            k0 = pl.multiple_of(i * NSA_FAR_TILE, NSA_FAR_TILE)
            s = _dot_nt(ks_ref[pl.ds(k0, NSA_FAR_TILE), :], qh)
            mask = block_mask(k0, NSA_FAR_TILE // SEL_BLOCK, base=32)
            return _softmax_more(carry, jnp.where(mask, s, NEG_INF), vs_ref[pl.ds(k0, NSA_FAR_TILE), :])

        _, l_s, acc_s = lax.fori_loop(0, n_far, sel_far, near(ks_ref, vs_ref, sel_near))
        o_s = acc_s / l_s

        s_w = jnp.where(wmask, _dot_nt(kw_ref[pl.ds(wf0, w_far), :], qh), NEG_INF)
        _, l_w, acc_w = _softmax_more(near(kw_ref, vw_ref, None), s_w, vw_ref[pl.ds(wf0, w_far), :])
        o_w = acc_w / l_w

        for i in range(NSA_HEADS_PER_PASS):
            r = h0 + i
            sl = slice(i * tq, (i + 1) * tq)
            row = 3 * (g * A_REP + r)
            out_t = (gates_t[row:row + 1] * o_c[:, r * tq:(r + 1) * tq] + gates_t[row + 1:row + 2] * o_s[:, sl]
                     + gates_t[row + 2:row + 3] * o_w[:, sl])
            outs.append(out_t.T)
    o_ref[...] = jnp.concatenate(outs, axis=1)


def _bucket_np(dist):
    n = np.maximum(dist, 0)
    exact = NUM_BUCKETS // 2
    nf = np.maximum(n, exact).astype(np.float64)
    large = exact + (np.log(nf / exact) / math.log(MAX_DISTANCE / exact) * (NUM_BUCKETS - exact)).astype(np.int64)
    return np.where(n < exact, n, np.minimum(large, NUM_BUCKETS - 1)).astype(np.int32)


def _bias_lookup(tab, dist):
    bucket = _bucket_np(np.asarray(dist))
    onehot = (jnp.asarray(bucket.reshape(-1, 1)) == jnp.arange(NUM_BUCKETS)[None, :]).astype(F32)
    flat = jnp.dot(onehot, tab.reshape(NUM_BUCKETS, -1), precision=lax.Precision.HIGHEST)
    return flat.reshape(bucket.shape + tab.shape[1:])


def _near_bias_tiles(tab, tq, window):
    i = np.arange(tq)[:, None]
    j = np.arange(2 * tq)[None, :]
    dist = tq + i - j
    ok = (dist >= 0) & (dist < window)
    bias = jnp.transpose(_bias_lookup(tab, dist), (2, 3, 0, 1))
    tiles = jnp.stack([jnp.where(jnp.asarray(ok & (j >= tq)), bias, NEG_INF), jnp.where(jnp.asarray(ok), bias, NEG_INF)])
    return tiles.reshape(2, tab.shape[1], tab.shape[2] * tq, 2 * tq)


def _cover_matrix(n_cmp_pad, n_sel):
    n = np.arange(n_cmp_pad)
    c_start = n * CMP_STRIDE
    c_end = c_start + CMP_LEN - 1
    s_start = np.arange(128) * SEL_BLOCK
    cover = (c_start[:, None] < s_start[None, :] + SEL_BLOCK) & (c_end[:, None] >= s_start[None, :])
    cover &= (np.arange(128)[None, :] < n_sel)
    return np.tile(cover.astype(np.float32), (A_REP, 1))


def _nsa_prompt_attend(q, gates, kc, vc, kvt, tab):
    b, t, _ = q.shape
    tq = NSA_Q_TILE
    assert t % NSA_FAR_TILE == 0 and t // SEL_BLOCK <= 32 and t >= WIN_A and kc.shape[2] == 128
    btile = jnp.swapaxes(_near_bias_tiles(tab - tab[NUM_BUCKETS - 1], tq, 2 * tq), -1, -2)
    cover = jnp.asarray(_cover_matrix(128, t // SEL_BLOCK)[:128].T, BF16)
    cmp_spec = pl.BlockSpec((None, A_KV, 128, HEAD_DIM), lambda i, j: (i, 0, 0, 0))
    return pl.pallas_call(
        functools.partial(_nsa_prompt_body, tq=tq),
        grid=(b, t // tq),
        in_specs=[
            pl.BlockSpec((None, tq, A_Q), lambda i, j: (i, j, 0)),
            pl.BlockSpec((None, tq, 128), lambda i, j: (i, j, 0)),
            cmp_spec, cmp_spec,
            pl.BlockSpec((None, 4 * A_KV, t, HEAD_DIM), lambda i, j: (i, 0, 0, 0)),
            pl.BlockSpec((None, A_KV, 2 * tq, A_REP * tq), lambda i, j: (jnp.minimum(j, 1), 0, 0, 0)),
            pl.BlockSpec((128, 128), lambda i, j: (0, 0)),
        ],
        out_specs=pl.BlockSpec((None, tq, A_Q), lambda i, j: (i, j, 0)),
        out_shape=jax.ShapeDtypeStruct((b, t, A_Q), F32),
        scratch_shapes=[pltpu.VMEM((A_KV, 2 * 32, NSA_HEADS_PER_PASS * tq), F32)],
        compiler_params=pltpu.CompilerParams(
            dimension_semantics=("parallel", "arbitrary"), vmem_limit_bytes=V7X_VMEM_LIMIT_BYTES),
        name="nsa_prompt",
    )(q, gates, kc, vc, kvt, btile, cover)


S5_NS = B_GROUPS * B_STATE
S5_T_CHUNK = 64
S5_STRIP = 512
S5_HALVES = 2


def _s5_body(u_ref, h0r_ref, h0i_ref, ar_ref, ai_ref, ldt_ref, wb_ref, wc_ref, d_ref, wglu_ref, bglu_ref,
             o_ref, hr_ref, hi_ref, coef_ref, st_ref, xbuf_ref, ubuf_ref, obuf_ref, *, nb, steps, interleave):
    c = pl.program_id(0)

    @pl.when(c == 0)
    def _():
        dt = jnp.exp(ldt_ref[...])
        ar, ai = ar_ref[...], ai_ref[...]
        mag = jnp.exp(ar * dt)
        abr, abi = mag * jnp.cos(ai * dt), mag * jnp.sin(ai * dt)
        den = ar * ar + ai * ai
        wr = ((abr - 1.0) * ar + abi * ai) / den
        wi = (abi * ar - (abr - 1.0) * ai) / den
        for k, val in enumerate((abr, abi, wr, wi)):
            coef_ref[k] = jnp.broadcast_to(val, (nb, S5_NS))
        st_ref[0] = h0r_ref[...]
        st_ref[1] = h0i_ref[...]

    if interleave:
        for b in range(nb):
            for j in range(B_WIDTH // 128):
                ubuf_ref.at[j][pl.ds(b, steps, stride=nb), :] = u_ref[b, :, j * 128:(j + 1) * 128]
        u = jnp.concatenate([ubuf_ref[j] for j in range(B_WIDTH // 128)], axis=1)
    else:
        u = u_ref[...]
    hc, hs = B_WIDTH // S5_HALVES, S5_NS // S5_HALVES
    ub = u.astype(BF16)
    for h in range(S5_HALVES):
        bu = jnp.dot(ub[:, h * hc:(h + 1) * hc], wb_ref[h], preferred_element_type=F32)
        xbuf_ref[:, h * hs:(h + 1) * hs] = bu[:, :hs]
        xbuf_ref[:, S5_NS + h * hs:S5_NS + (h + 1) * hs] = bu[:, hs:]

    for lo in range(0, S5_NS, S5_STRIP):
        re = slice(lo, lo + S5_STRIP)
        im = slice(S5_NS + lo, S5_NS + lo + S5_STRIP)
        abr, abi, wr, wi = (coef_ref[k, :, re] for k in range(4))

        def step(t, carry):
            sr, si = carry
            r0 = pl.multiple_of(t * nb, nb)
            bur = xbuf_ref[pl.ds(r0, nb), re]
            bui = xbuf_ref[pl.ds(r0, nb), im]
            nsr = abr * sr - abi * si + (wr * bur - wi * bui)
            nsi = abr * si + abi * sr + (wr * bui + wi * bur)
            xbuf_ref[pl.ds(r0, nb), re] = nsr
            xbuf_ref[pl.ds(r0, nb), im] = nsi
            return nsr, nsi

        sr, si = lax.fori_loop(0, steps, step, (st_ref[0, :, re], st_ref[1, :, re]),
                               unroll=min(steps, 8))
        st_ref[0, :, re] = sr
        st_ref[1, :, re] = si

    y = jnp.concatenate(
        [jnp.dot(jnp.concatenate([xbuf_ref[:, h * hs:(h + 1) * hs], xbuf_ref[:, S5_NS + h * hs:S5_NS + (h + 1) * hs]],
                                 axis=1).astype(BF16), wc_ref[h], preferred_element_type=F32)
         for h in range(S5_HALVES)], axis=1) + d_ref[...] * u
    z = _gelu_tanh(y)
    gate = jnp.dot(z.astype(BF16), wglu_ref[...], preferred_element_type=F32) + bglu_ref[...]
    out = z * (1.0 / (1.0 + jnp.exp(-gate)))
    if interleave:
        for j in range(B_WIDTH // 128):
            obuf_ref[j] = out[:, j * 128:(j + 1) * 128]
        for b in range(nb):
            for j in range(B_WIDTH // 128):
                o_ref[b, :, j * 128:(j + 1) * 128] = obuf_ref.at[j][pl.ds(b, steps, stride=nb), :]
    else:
        o_ref[...] = out
    hr_ref[...] = st_ref[0]
    hi_ref[...] = st_ref[1]


def _s5_weights(a_re, a_im, log_dt, b_re, b_im, c_re, c_im, d, w_glu, b_glu):
    eye = jnp.eye(B_GROUPS, dtype=F32)
    blk_in = lambda w: jnp.einsum('hg,gpc->hcgp', eye, w).reshape(B_WIDTH, S5_NS)
    blk_out = lambda w: jnp.einsum('gh,gcp->gphc', eye, w).reshape(S5_NS, B_WIDTH)
    hc, hs = B_WIDTH // S5_HALVES, S5_NS // S5_HALVES
    diag = lambda w, h: w[h * hc:(h + 1) * hc, h * hs:(h + 1) * hs]
    wb = jnp.stack([jnp.concatenate([diag(blk_in(b_re), h), diag(blk_in(b_im), h)], axis=1)
                    for h in range(S5_HALVES)]).astype(BF16)
    diag_t = lambda w, h: w[h * hs:(h + 1) * hs, h * hc:(h + 1) * hc]
    wc = jnp.stack([jnp.concatenate([diag_t(blk_out(c_re), h), -diag_t(blk_out(c_im), h)], axis=0)
                    for h in range(S5_HALVES)]).astype(BF16)
    flat = lambda x: x.reshape(1, S5_NS)
    return (flat(a_re), flat(a_im), flat(jnp.repeat(log_dt, B_STATE)), wb, wc, d.reshape(1, B_WIDTH),
            w_glu.astype(BF16), b_glu.reshape(1, B_WIDTH))


def _s5_mix(u, h_re, h_im, weights):
    nb, t, _ = u.shape
    interleave = t > 1
    steps = min(t, S5_T_CHUNK)
    rows = nb * steps
    body = functools.partial(_s5_body, nb=nb, steps=steps, interleave=interleave)
    if interleave:
        u_in = u
        u_spec = pl.BlockSpec((nb, steps, B_WIDTH), lambda c: (0, c, 0))
        o_shape = jax.ShapeDtypeStruct((nb, t, B_WIDTH), F32)
        scratch_rows = rows
    else:
        u_in = u.reshape(nb, B_WIDTH)
        u_spec = pl.BlockSpec((nb, B_WIDTH), lambda c: (0, 0))
        o_shape = jax.ShapeDtypeStruct((nb, B_WIDTH), F32)
        scratch_rows = 8
    o, hr, hi = pl.pallas_call(
        body,
        grid=(t // steps,),
        in_specs=[
            u_spec, _const_spec((nb, S5_NS)), _const_spec((nb, S5_NS)),
            _const_spec((1, S5_NS)), _const_spec((1, S5_NS)), _const_spec((1, S5_NS)),
            _const_spec((S5_HALVES, B_WIDTH // S5_HALVES, 2 * S5_NS // S5_HALVES)),
            _const_spec((S5_HALVES, 2 * S5_NS // S5_HALVES, B_WIDTH // S5_HALVES)),
            _const_spec((1, B_WIDTH)), _const_spec((B_WIDTH, B_WIDTH)), _const_spec((1, B_WIDTH)),
        ],
        out_specs=[u_spec, pl.BlockSpec((nb, S5_NS), lambda c: (0, 0)), pl.BlockSpec((nb, S5_NS), lambda c: (0, 0))],
        out_shape=[o_shape, jax.ShapeDtypeStruct((nb, S5_NS), F32), jax.ShapeDtypeStruct((nb, S5_NS), F32)],
        scratch_shapes=[
            pltpu.VMEM((4, nb, S5_NS), F32),
            pltpu.VMEM((2, nb, S5_NS), F32),
            pltpu.VMEM((rows, 2 * S5_NS), F32),
            pltpu.VMEM((B_WIDTH // 128, scratch_rows, 128), F32),
            pltpu.VMEM((B_WIDTH // 128, scratch_rows, 128), F32),
        ],
        compiler_params=pltpu.CompilerParams(
            dimension_semantics=("arbitrary",),
            vmem_limit_bytes=V7X_VMEM_LIMIT_BYTES),
        name="s5_mix",
    )(u_in, h_re, h_im, *weights)
    return o.reshape(nb, t, B_WIDTH), hr, hi


CHUNKS_PER_PAGE = PAGE_SIZE // CMP_STRIDE
SEL_PAD = 256


def _bf16_round(x):
    return x.astype(BF16).astype(F32)


def _compress_tail(c, pos, w2):
    c1 = c[:, 128:]
    nxt = jnp.concatenate([c1[1:], c1[:1]], axis=0)
    hid = (pos + c[:, :128]) + nxt
    return jnp.dot(_gelu_tanh(hid).astype(BF16), w2, preferred_element_type=F32)


def _rms_heads128(x, gain):
    left = lax.broadcasted_iota(jnp.int32, x.shape, 1) < HEAD_DIM
    sq = x * x
    s0 = jnp.sum(jnp.where(left, sq, 0.0), axis=-1, keepdims=True)
    s1 = jnp.sum(jnp.where(left, 0.0, sq), axis=-1, keepdims=True)
    ms = jnp.where(left, s0, s1) * (1.0 / HEAD_DIM)
    return x * lax.rsqrt(ms + EPS) * gain


def _pad_rows8(x):
    return jnp.concatenate([x, jnp.zeros((8 - x.shape[0], x.shape[1]), x.dtype)], axis=0)


def _compress_mlp(row_of_chunks, n_chunks, w_ref, pos, w2_ref):
    acc = jnp.zeros((n_chunks, 2 * A_KVW), F32)
    for sp in range(CMP_STRIDE // 2):
        x = jnp.concatenate([row_of_chunks(2 * sp + e) for e in range(2)], axis=1).astype(BF16)
        acc = acc + jnp.dot(x, w_ref[sp], preferred_element_type=F32)
    return _compress_tail(acc, pos, w2_ref[...])


def _compress_rows(rows_scr, kv, n_chunks, w_ref, pos, w2_ref):
    return _compress_mlp(lambda s: rows_scr.at[kv][pl.ds(s, n_chunks, stride=CMP_STRIDE), :],
                         n_chunks, w_ref, pos, w2_ref)


def _compress_weights(cmp_pos, cmp_w1, cmp_w2, k_gain):
    eye = jnp.eye(A_KV, dtype=F32)
    w1 = cmp_w1.reshape(2, 2, CMP_STRIDE, HEAD_DIM, HEAD_DIM)
    w_big = jnp.einsum('ajsdh,gk->asgdjkh', w1, eye).reshape(2, CMP_STRIDE // 2, 2 * A_KVW, 2 * A_KVW).astype(BF16)
    w_pos = jnp.concatenate([cmp_w1, cmp_w1], axis=-1).astype(BF16)
    pos = jnp.broadcast_to(cmp_pos.reshape(2, 1, CMP_LEN * HEAD_DIM), (2, 8, CMP_LEN * HEAD_DIM)).astype(BF16)
    w2_big = jnp.einsum('ahd,gk->aghkd', cmp_w2, eye).reshape(2, A_KVW, A_KVW).astype(BF16)
    gain2 = jnp.tile(k_gain, A_KV).reshape(1, A_KVW)
    return (pos[0], pos[1], w_pos[0], w_pos[1], w_big[0], w_big[1], w2_big[0], w2_big[1], gain2)


def _compress_specs(full):
    mlp_w = (CMP_STRIDE // 2, 2 * A_KVW, 2 * A_KVW)
    return [full((8, CMP_LEN * HEAD_DIM)), full((8, CMP_LEN * HEAD_DIM)),
            full((CMP_LEN * HEAD_DIM, A_KVW)), full((CMP_LEN * HEAD_DIM, A_KVW)),
            full(mlp_w), full(mlp_w), full((A_KVW, A_KVW)), full((A_KVW, A_KVW)), full((1, A_KVW))]


def _pos_terms(pos_scr, posk_ref, posv_ref, wpos_k_ref, wpos_v_ref):
    pos_scr[0] = jnp.dot(posk_ref[...], wpos_k_ref[...], preferred_element_type=F32)
    pos_scr[1] = jnp.dot(posv_ref[...], wpos_v_ref[...], preferred_element_type=F32)


def _cmp_prompt_body(cmp_ref, posk_ref, posv_ref, wpos_k_ref, wpos_v_ref, wk_ref, wv_ref, w2k_ref, w2v_ref,
                     kgain_ref, kc_ref, vc_ref, pos_scr, rows_scr, *, n_chunks):
    @pl.when(pl.program_id(0) == 0)
    def _():
        _pos_terms(pos_scr, posk_ref, posv_ref, wpos_k_ref, wpos_v_ref)

    rows_scr[0] = cmp_ref[:, :A_KVW]
    rows_scr[1] = cmp_ref[:, A_KVW:]
    kc = _rms_heads128(_compress_rows(rows_scr, 0, n_chunks, wk_ref, pos_scr[0, 0:1], w2k_ref),
                       kgain_ref[...]).astype(BF16)
    vc = _compress_rows(rows_scr, 1, n_chunks, wv_ref, pos_scr[1, 0:1], w2v_ref).astype(BF16)
    for g in range(A_KV):
        kc_ref[g] = kc[:, g * HEAD_DIM:(g + 1) * HEAD_DIM]
        vc_ref[g] = vc[:, g * HEAD_DIM:(g + 1) * HEAD_DIM]


def _cmp_prompt(cmp_rows, cmp_weights):
    b, t, _ = cmp_rows.shape
    n_chunks = t // CMP_STRIDE
    full = lambda shape: pl.BlockSpec(shape, lambda i: (0,) * len(shape))
    out_spec = pl.BlockSpec((None, A_KV, n_chunks, HEAD_DIM), lambda i: (i, 0, 0, 0))
    out_shape = jax.ShapeDtypeStruct((b, A_KV, n_chunks, HEAD_DIM), BF16)
    return pl.pallas_call(
        functools.partial(_cmp_prompt_body, n_chunks=n_chunks),
        grid=(b,),
        in_specs=[pl.BlockSpec((None, t, 2 * A_KVW), lambda i: (i, 0, 0))] + _compress_specs(full),
        out_specs=[out_spec, out_spec],
        out_shape=[out_shape, out_shape],
        scratch_shapes=[pltpu.VMEM((2, 8, A_KVW), F32), pltpu.VMEM((2, t, A_KVW), F32)],
        compiler_params=pltpu.CompilerParams(
            dimension_semantics=("arbitrary",), vmem_limit_bytes=V7X_VMEM_LIMIT_BYTES),
        name="nsa_cmp_prompt",
    )(cmp_rows, *cmp_weights)


def _nsa_sample_cmp_body(pt_ref, *refs, n_pages):
    page_refs = refs[:n_pages]
    (posk_ref, posv_ref, wpos_k_ref, wpos_v_ref, wk_ref, wv_ref, w2k_ref, w2v_ref, kgain_ref,
     kc_ref, vc_ref, pos_scr, rows_scr) = refs[n_pages:]

    @pl.when(pl.program_id(0) == 0)
    def _():
        _pos_terms(pos_scr, posk_ref, posv_ref, wpos_k_ref, wpos_v_ref)

    n_chunks = n_pages * CHUNKS_PER_PAGE
    for kv in range(2):
        for p in range(n_pages):
            rows = page_refs[p][kv].reshape(A_KVW, PAGE_SIZE).T
            by_s = pltpu.einshape("csl->scl", rows.reshape(CHUNKS_PER_PAGE, CMP_STRIDE, A_KVW))
            for s in range(CMP_STRIDE):
                rows_scr[kv, s, p * CHUNKS_PER_PAGE:(p + 1) * CHUNKS_PER_PAGE, :] = by_s[s]

    compress = lambda kv, w_ref, w2_ref: _compress_mlp(lambda s: rows_scr[kv, s], n_chunks, w_ref,
                                                       pos_scr[kv, 0:1], w2_ref)
    kc_ref[...] = _rms_heads128(compress(0, wk_ref, w2k_ref), kgain_ref[...]).astype(BF16)
    vc_ref[...] = compress(1, wv_ref, w2v_ref).astype(BF16)


NSA_PICK_BATCH = 16


def _nsa_sample_pick_body(q_ref, kc_ref, vc_ref, cover_ref, oc_ref, idx_ref, *, n_cmp, n_sel):
    nb, n_chunks = kc_ref.shape[0], kc_ref.shape[1]
    col = lax.broadcasted_iota(jnp.int32, (8, n_chunks), 1)
    imps = []
    for i in range(nb):
        for g in range(A_KV):
            hs = slice(g * HEAD_DIM, (g + 1) * HEAD_DIM)
            q8 = (_pad_rows8(q_ref[i, g]) * (HEAD_DIM ** -0.5)).astype(BF16)
            s = jnp.where(col < n_cmp, _dot_nt(q8, kc_ref[i, :, hs]), NEG_INF)
            e = jnp.exp(s - jnp.max(s, axis=-1, keepdims=True))
            p = (e / jnp.sum(e, axis=-1, keepdims=True)).astype(BF16)
            oc_ref[i, g] = jnp.dot(p, vc_ref[i, :, hs], preferred_element_type=F32)[0:A_REP]
            imps.append(jnp.sum(jnp.dot(p, cover_ref[...], preferred_element_type=F32)[0:A_REP], axis=0, keepdims=True))
    imp = jnp.concatenate(imps, axis=0)
    rows = nb * A_KV
    s_idx = lax.broadcasted_iota(jnp.int32, (rows, SEL_PAD), 1)
    s_idx_f = s_idx.astype(F32)
    forced = (s_idx == 0) | (s_idx == n_sel - 1) | (s_idx == n_sel - 2)
    score = jnp.where(s_idx < n_sel, imp + jnp.where(forced, FORCE_BONUS, 0.0), NEG_INF)
    rank = jnp.zeros((rows, SEL_PAD), F32)
    for j in range(n_sel):
        cj = score[:, j:j + 1]
        beats = (cj > score) | ((cj == score) & (s_idx > j))
        rank = rank + jnp.where(beats, 1.0, 0.0)
    lane = lax.broadcasted_iota(jnp.int32, (rows, 128), 1)
    picks = jnp.zeros((rows, 128), F32)
    for r in range(SEL_TOPK):
        block = jnp.sum(jnp.where(rank == float(r), s_idx_f, 0.0), axis=-1, keepdims=True)
        picks = picks + jnp.where(lane == r, block, 0.0)
    idx_ref[...] = picks.astype(jnp.int32)


NSA_ATT_BATCH = 2


def _nsa_sample_att_body(idx_ref, pt_ref, *refs, n_past_blk):
    n_slots = NSA_ATT_BATCH * A_KV * SEL_TOPK
    slot_refs = refs[:n_slots]
    (q_all, gate_all, oc_all, newsel_all, wcache_all, newwin_all, newcol_all, selb_all, winb_all, newb_all,
     o_all, wout_all) = refs[n_slots:]
    for i in range(NSA_ATT_BATCH):
        b = pl.program_id(0) * NSA_ATT_BATCH + i
        for g in range(A_KV):
            first = (i * A_KV + g) * SEL_TOPK
            _nsa_sample_att_group(b, g, idx_ref, slot_refs[first:first + SEL_TOPK], q_all.at[i, g], gate_all.at[i, g],
                                  oc_all.at[i, g], newsel_all.at[i, g], wcache_all.at[i, :, g], newwin_all.at[i, g],
                                  selb_all.at[g], winb_all.at[g], newb_all.at[g], o_all.at[i, g], n_past_blk)
            for kv in range(2):
                wout_all[i, kv, g] = _shift_in(wcache_all[i, kv, g], newcol_all[i, g][:, kv:kv + 1])


def _shift_in(rows_last, new_col):
    w_len = rows_last.shape[-1]
    lane = lax.broadcasted_iota(jnp.int32, rows_last.shape, 1)
    return jnp.where(lane == w_len - 1, new_col, pltpu.roll(rows_last, w_len - 1, axis=1))


def _nsa_sample_att_group(b, g, idx_ref, page_refs, q_ref, gate_ref, oc_ref, newsel_ref, wcache_ref, newwin_ref,
                          selb_ref, winb_ref, newb_ref, o_ref, n_past_blk):
    blocks_per_page = PAGE_SIZE // SEL_BLOCK
    q8 = (_pad_rows8(q_ref[...]) * (HEAD_DIM ** -0.5)).astype(BF16)
    new_bias = newb_ref[:, 0:1]

    def attend(s, v_t, new_kv):
        k_new = _bf16_round(new_kv[0:1])
        v_new = _bf16_round(new_kv[1:2])
        s_new = jnp.sum(q8.astype(F32) * k_new, axis=-1, keepdims=True) + new_bias
        m = jnp.maximum(jnp.max(s, axis=-1, keepdims=True), s_new)
        e = jnp.exp(s - m)
        e_new = jnp.exp(s_new - m)
        den = jnp.sum(e, axis=-1, keepdims=True) + e_new
        return _dot_nt((e / den).astype(BF16), v_t) + _bf16_round(e_new / den) * v_new

    lane = lax.broadcasted_iota(jnp.int32, (8, PAGE_SIZE), 1)
    near = selb_ref[...]
    bias = []
    for k in range(SEL_TOPK):
        ik = idx_ref[(b * A_KV + g) * SEL_TOPK + k]
        blk = jnp.minimum(ik, n_past_blk - 1)
        near_k = jnp.where(blk // blocks_per_page == (n_past_blk - 1) // blocks_per_page, near, 0.0)
        keep = (lane // SEL_BLOCK == blk % blocks_per_page) & (ik < n_past_blk)
        bias.append(jnp.where(keep, near_k, NEG_INF))
    k_sel = jnp.concatenate([page_refs[k][0].astype(BF16) for k in range(SEL_TOPK)], axis=1)
    v_sel = jnp.concatenate([page_refs[k][1].astype(BF16) for k in range(SEL_TOPK)], axis=1)
    s_sel = jnp.dot(q8, k_sel, preferred_element_type=F32) + jnp.concatenate(bias, axis=1)
    o_s = attend(s_sel, v_sel, newsel_ref[...])

    w_len = wcache_ref.shape[-1]
    wcol = lax.broadcasted_iota(jnp.int32, (8, w_len), 1)
    s_w = jnp.dot(q8, wcache_ref[0].astype(BF16), preferred_element_type=F32) + winb_ref[...]
    o_w = attend(jnp.where(wcol >= 1, s_w, NEG_INF), wcache_ref[1].astype(BF16), newwin_ref[...])
    gates = gate_ref[...]
    o_ref[...] = gates[:, 0:1] * oc_ref[...] + gates[:, 1:2] * o_s[0:A_REP] + gates[:, 2:3] * o_w[0:A_REP]


def _rows_last(cache):
    nd = cache.ndim
    return jnp.transpose(cache, tuple(range(nd - 4)) + (nd - 3, nd - 2, nd - 1, nd - 4))


def _rows_first(cache):
    nd = cache.ndim
    return jnp.transpose(cache, tuple(range(nd - 4)) + (nd - 1, nd - 4, nd - 3, nd - 2))


def _nsa_sample(q, gates, new_sel, new_win, pool_cmp, pool_sel, win_cache, page_table, cmp_weights, tab):
    n, n_pages = page_table.shape
    past = n_pages * PAGE_SIZE
    n_past_blk = past // SEL_BLOCK
    n_sel = n_past_blk + 1
    n_chunks = n_pages * CHUNKS_PER_PAGE
    w_len = win_cache.shape[1]
    blocks_per_page = PAGE_SIZE // SEL_BLOCK
    assert n_sel <= SEL_PAD and w_len == WIN_A and past >= WIN_A and blocks_per_page == 2
    pt_flat = page_table.reshape(-1)

    c_idx = np.arange(n_chunks)
    s_start = np.arange(SEL_PAD) * SEL_BLOCK
    cover = ((c_idx[:, None] * CMP_STRIDE < s_start[None, :] + SEL_BLOCK)
             & (c_idx[:, None] * CMP_STRIDE + CMP_LEN - 1 >= s_start[None, :])
             & (c_idx[:, None] < n_chunks - 1) & (np.arange(SEL_PAD)[None, :] < n_sel))
    cover = jnp.asarray(cover.astype(np.float32), BF16)

    head_spec = lambda last: pl.BlockSpec((None, A_KV, A_REP, last), lambda i, *_: (i, 0, 0, 0))
    page_specs = [pl.BlockSpec((None, 2, A_KV, HEAD_DIM, PAGE_SIZE),
                               functools.partial(lambda i, pt, p: (pt[i * n_pages + p], 0, 0, 0, 0), p=p))
                  for p in range(n_pages)]
    full = lambda shape: pl.BlockSpec(shape, lambda i, *_: (0,) * len(shape))
    pool_c = _rows_last(pool_cmp)
    cmp_spec = pl.BlockSpec((None, n_chunks, A_KVW), lambda i, *_: (i, 0, 0))
    cmp_shape = jax.ShapeDtypeStruct((n, n_chunks, A_KVW), BF16)
    kc, vc = pl.pallas_call(
        functools.partial(_nsa_sample_cmp_body, n_pages=n_pages),
        grid_spec=pltpu.PrefetchScalarGridSpec(
            num_scalar_prefetch=1,
            grid=(n,),
            in_specs=page_specs + _compress_specs(full),
            out_specs=[cmp_spec, cmp_spec],
            scratch_shapes=[pltpu.VMEM((2, 8, A_KVW), F32), pltpu.VMEM((2, CMP_STRIDE, n_chunks, A_KVW), F32)],
        ),
        out_shape=[cmp_shape, cmp_shape],
        compiler_params=pltpu.CompilerParams(
            dimension_semantics=("arbitrary",), vmem_limit_bytes=V7X_VMEM_LIMIT_BYTES),
        name="nsa_sample_cmp",
    )(pt_flat, *([pool_c] * n_pages), *cmp_weights)

    nb = NSA_PICK_BATCH
    assert n % nb == 0
    batch = lambda *tail: pl.BlockSpec((nb,) + tail, lambda i: (i,) + (0,) * len(tail))
    o_c, idx = pl.pallas_call(
        functools.partial(_nsa_sample_pick_body, n_cmp=n_chunks - 1, n_sel=n_sel),
        grid=(n // nb,),
        in_specs=[batch(A_KV, A_REP, HEAD_DIM), batch(n_chunks, A_KVW), batch(n_chunks, A_KVW),
                  pl.BlockSpec((n_chunks, SEL_PAD), lambda i: (0, 0))],
        out_specs=[batch(A_KV, A_REP, HEAD_DIM), pl.BlockSpec((nb * A_KV, 128), lambda i: (i, 0))],
        out_shape=[jax.ShapeDtypeStruct((n, A_KV, A_REP, HEAD_DIM), F32),
                   jax.ShapeDtypeStruct((n * A_KV, 128), jnp.int32)],
        compiler_params=pltpu.CompilerParams(
            dimension_semantics=("arbitrary",), vmem_limit_bytes=V7X_VMEM_LIMIT_BYTES),
        name="nsa_sample_pick",
    )(q, kc, vc, cover)

    rel = lambda dist: jnp.pad(jnp.transpose(_bias_lookup(tab - tab[NUM_BUCKETS - 1], dist), (1, 2, 0)),
                               ((0, 0), (0, 8 - A_REP), (0, 0)))
    sel_bias = rel(past - (past - PAGE_SIZE + np.arange(PAGE_SIZE)))
    win_bias = rel(w_len - np.arange(w_len))
    new_bias = rel(np.zeros(128, np.int64))

    idx_flat = idx[:, :SEL_TOPK].reshape(-1)

    nba = NSA_ATT_BATCH
    assert n % nba == 0

    def page_map(i, idx_s, pt, seq, g, slot):
        b = i * nba + seq
        blk = jnp.minimum(idx_s[(b * A_KV + g) * SEL_TOPK + slot], n_past_blk - 1)
        return (pt[b * n_pages + blk // blocks_per_page], 0, g, 0, 0)

    slot_specs = [pl.BlockSpec((None, 2, None, HEAD_DIM, PAGE_SIZE), functools.partial(page_map, seq=e, g=g, slot=s))
                  for e in range(nba) for g in range(A_KV) for s in range(SEL_TOPK)]
    per_head = lambda rows, last: pl.BlockSpec((nba, A_KV, rows, last), lambda i, *_: (i, 0, 0, 0))
    per_group = lambda last: pl.BlockSpec((A_KV, 8, last), lambda i, *_: (0, 0, 0))
    win_spec = pl.BlockSpec((nba, 2, A_KV, HEAD_DIM, w_len), lambda i, *_: (i, 0, 0, 0, 0))
    o, win_next = pl.pallas_call(
        functools.partial(_nsa_sample_att_body, n_past_blk=n_past_blk),
        grid_spec=pltpu.PrefetchScalarGridSpec(
            num_scalar_prefetch=2,
            grid=(n // nba,),
            in_specs=slot_specs + [
                per_head(A_REP, HEAD_DIM), per_head(A_REP, 3), per_head(A_REP, HEAD_DIM), per_head(2, HEAD_DIM),
                win_spec, per_head(2, HEAD_DIM), per_head(HEAD_DIM, 2),
                per_group(PAGE_SIZE), per_group(w_len), per_group(128),
            ],
            out_specs=[per_head(A_REP, HEAD_DIM), win_spec],
        ),
        out_shape=[jax.ShapeDtypeStruct((n, A_KV, A_REP, HEAD_DIM), F32),
                   jax.ShapeDtypeStruct((n, 2, A_KV, HEAD_DIM, w_len), F32)],
        compiler_params=pltpu.CompilerParams(
            dimension_semantics=("arbitrary",), vmem_limit_bytes=V7X_VMEM_LIMIT_BYTES),
        name="nsa_sample_att",
    )(idx_flat, pt_flat, *([_rows_last(pool_sel)] * (nba * A_KV * SEL_TOPK)), q, gates, o_c,
      jnp.transpose(new_sel, (0, 2, 1, 3)), _rows_last(win_cache), jnp.transpose(new_win, (0, 2, 1, 3)),
      jnp.transpose(new_win, (0, 2, 3, 1)), sel_bias, win_bias, new_bias)
    return o, _rows_first(win_next)


SWA_Q_TILE = WIN_C


def _swa_prompt_body(q_ref, kv_ref, btile_ref, sink_ref, o_ref, *, tq):
    qt = pl.program_id(1)
    q0 = pl.multiple_of(qt * tq, tq)
    prev0 = pl.multiple_of(jnp.maximum(q0 - tq, 0), tq)
    width = C_REP * HEAD_DIM
    for g in range(C_KV):
        q = q_ref[:, g * width:(g + 1) * width]
        qs = jnp.concatenate([q[:, r * HEAD_DIM:(r + 1) * HEAD_DIM] for r in range(C_REP)], axis=0)
        qs = (qs * (HEAD_DIM ** -0.5)).astype(BF16)
        k_ref, v_ref = kv_ref.at[g], kv_ref.at[C_KV + g]
        k = jnp.concatenate([k_ref[pl.ds(prev0, tq), :], k_ref[pl.ds(q0, tq), :]], axis=0)
        v = jnp.concatenate([v_ref[pl.ds(prev0, tq), :], v_ref[pl.ds(q0, tq), :]], axis=0)
        s = _dot_nt(k, qs) + btile_ref[g]
        sinks = sink_ref[g]
        sink = jnp.concatenate([sinks[r:r + 1, :] for r in range(C_REP)], axis=1)
        m = jnp.maximum(jnp.max(s, axis=0, keepdims=True), sink)
        e = jnp.exp(s - m)
        p = e / (jnp.sum(e, axis=0, keepdims=True) + jnp.exp(sink - m))
        o = _dot_tn(v, p.astype(BF16))
        o_ref[:, g * width:(g + 1) * width] = jnp.concatenate([o[:, r * tq:(r + 1) * tq].T for r in range(C_REP)],
                                                              axis=1)


def _swa_prompt(q, kvt, sinks, tab):
    b, t, _ = q.shape
    tq = SWA_Q_TILE
    btile = jnp.swapaxes(_near_bias_tiles(tab, tq, WIN_C), -1, -2)
    qo_spec = pl.BlockSpec((None, tq, C_HEADS * HEAD_DIM), lambda i, j: (i, j, 0))
    sink_lanes = jnp.broadcast_to(sinks.reshape(C_KV, C_REP, 1), (C_KV, C_REP, 128))
    return pl.pallas_call(
        functools.partial(_swa_prompt_body, tq=tq),
        grid=(b, t // tq),
        in_specs=[qo_spec,
                  pl.BlockSpec((None, 2 * C_KV, t, HEAD_DIM), lambda i, j: (i, 0, 0, 0)),
                  pl.BlockSpec((None, C_KV, 2 * tq, C_REP * tq), lambda i, j: (jnp.minimum(j, 1), 0, 0, 0)),
                  pl.BlockSpec((C_KV, C_REP, 128), lambda i, j: (0, 0, 0))],
        out_specs=qo_spec,
        out_shape=jax.ShapeDtypeStruct((b, t, C_HEADS * HEAD_DIM), F32),
        compiler_params=pltpu.CompilerParams(
            dimension_semantics=("parallel", "arbitrary"), vmem_limit_bytes=V7X_VMEM_LIMIT_BYTES),
        name="swa_prompt",
    )(q, kvt, btile, sink_lanes)


SWA_DECODE_BATCH = 4


def _swa_sample_body(q_ref, cache_ref, new_ref, newcol_ref, bias_ref, newb_ref, sink_ref, o_ref, next_ref):
    w_len = cache_ref.shape[-1]
    wcol = lax.broadcasted_iota(jnp.int32, (C_REP, w_len), 1)
    for i in range(q_ref.shape[0]):
        for g in range(C_KV):
            q8 = (q_ref[i, g] * (HEAD_DIM ** -0.5)).astype(BF16)
            s = jnp.dot(q8, cache_ref[i, 0, g].astype(BF16), preferred_element_type=F32) + bias_ref[g]
            s = jnp.where(wcol >= 1, s, NEG_INF)
            k_new = _bf16_round(new_ref[i, g, 0:1])
            v_new = _bf16_round(new_ref[i, g, 1:2])
            s_new = jnp.sum(q8.astype(F32) * k_new, axis=-1, keepdims=True) + newb_ref[g][:, 0:1]
            sink = sink_ref[g][:, 0:1]
            m = jnp.maximum(jnp.maximum(jnp.max(s, axis=-1, keepdims=True), s_new), sink)
            e = jnp.exp(s - m)
            e_new = jnp.exp(s_new - m)
            den = jnp.sum(e, axis=-1, keepdims=True) + e_new + jnp.exp(sink - m)
            o_ref[i, g] = (_dot_nt((e / den).astype(BF16), cache_ref[i, 1, g].astype(BF16))
                           + _bf16_round(e_new / den) * v_new)
            for kv in range(2):
                next_ref[i, kv, g] = _shift_in(cache_ref[i, kv, g], newcol_ref[i, g][:, kv:kv + 1])


def _swa_sample(q, cache, new_kv, sinks, tab):
    n, w_len = cache.shape[:2]
    bias = jnp.transpose(_bias_lookup(tab, w_len - np.arange(w_len)), (1, 2, 0))
    lanes = lambda x: jnp.broadcast_to(x[:, :, None], (C_KV, C_REP, 128))
    full = lambda shape: pl.BlockSpec(shape, lambda i: (0,) * len(shape))
    nb = SWA_DECODE_BATCH
    assert n % nb == 0
    cache_spec = pl.BlockSpec((nb, 2, C_KV, HEAD_DIM, w_len), lambda i: (i, 0, 0, 0, 0))
    o_spec = pl.BlockSpec((nb, C_KV, C_REP, HEAD_DIM), lambda i: (i, 0, 0, 0))
    o, cache_next = pl.pallas_call(
        _swa_sample_body,
        grid=(n // nb,),
        in_specs=[
            o_spec, cache_spec,
            pl.BlockSpec((nb, C_KV, 2, HEAD_DIM), lambda i: (i, 0, 0, 0)),
            pl.BlockSpec((nb, C_KV, HEAD_DIM, 2), lambda i: (i, 0, 0, 0)),
            full((C_KV, C_REP, w_len)), full((C_KV, C_REP, 128)), full((C_KV, C_REP, 128)),
        ],
        out_specs=[o_spec, cache_spec],
        out_shape=[jax.ShapeDtypeStruct((n, C_KV, C_REP, HEAD_DIM), F32),
                   jax.ShapeDtypeStruct((n, 2, C_KV, HEAD_DIM, w_len), F32)],
        compiler_params=pltpu.CompilerParams(
            dimension_semantics=("arbitrary",), vmem_limit_bytes=V7X_VMEM_LIMIT_BYTES),
        name="swa_sample",
    )(q, _rows_last(cache), new_kv, jnp.transpose(new_kv, (0, 1, 3, 2)), bias, lanes(tab[0]),
      lanes(sinks.reshape(C_KV, C_REP)))
    return o, _rows_first(cache_next)


PROJ_ROW_TILE = 512
A_IN_PAD = 1920
A_U_COL = A_Q + 6 * A_KVW
A_GATE_COL = A_U_COL + B_WIDTH


def _heads_first(kvt_ref, slot, k, v, n_kv):
    for g in range(n_kv):
        kvt_ref[slot + g] = k[:, g * HEAD_DIM:(g + 1) * HEAD_DIM].astype(BF16)
        kvt_ref[slot + n_kv + g] = v[:, g * HEAD_DIM:(g + 1) * HEAD_DIM].astype(BF16)


def _inproj_a_body(x_ref, gain_ref, w_ref, qg_ref, kg_ref,
                   q_ref, cmp_ref, sel_ref, win_ref, gate_ref, u_ref, kvt_ref):
    xn = _rms_rows(x_ref[...], gain_ref[...]).astype(BF16)
    z = jnp.dot(xn, w_ref[...], preferred_element_type=F32)
    for j in range(A_Q // 128):
        q_ref[:, j * 128:(j + 1) * 128] = _rms_heads128(z[:, j * 128:(j + 1) * 128], qg_ref[...])
    for out_ref, off, slot in ((cmp_ref, A_Q, None), (sel_ref, A_Q + 2 * A_KVW, 0), (win_ref, A_Q + 4 * A_KVW, 4)):
        k = _rms_heads128(z[:, off:off + A_KVW], kg_ref[...])
        v = z[:, off + A_KVW:off + 2 * A_KVW]
        out_ref[:, :A_KVW] = k
        out_ref[:, A_KVW:] = v
        if slot is not None:
            _heads_first(kvt_ref, slot, k, v, A_KV)
    u_ref[...] = z[:, A_U_COL:A_GATE_COL]
    gate_ref[...] = 1.0 / (1.0 + jnp.exp(-z[:, A_GATE_COL:A_IN_PAD]))


def _inproj_a(x, gain, w_in, q_gain, k_gain):
    b, t, d = x.shape
    tm = min(t, PROJ_ROW_TILE)
    w = jnp.concatenate([w_in[:, :A_U_COL], w_in[:, A_U_COL + A_GATE:], w_in[:, A_U_COL:A_U_COL + A_GATE],
                         jnp.zeros((d, A_IN_PAD - A_GATE_COL - A_GATE), F32)], axis=1).astype(BF16)
    tile2 = lambda g: jnp.tile(g, 2).reshape(1, 128)
    rows = lambda width: pl.BlockSpec((None, tm, width), lambda i, j: (i, j, 0))
    shape = lambda width: jax.ShapeDtypeStruct((b, t, width), F32)
    return pl.pallas_call(
        _inproj_a_body,
        grid=(b, t // tm),
        in_specs=[rows(d), _const_spec((1, d)), _const_spec((d, A_IN_PAD)), _const_spec((1, 128)), _const_spec((1, 128))],
        out_specs=[rows(A_Q), rows(2 * A_KVW), rows(2 * A_KVW), rows(2 * A_KVW), rows(128), rows(B_WIDTH),
                   pl.BlockSpec((None, 8, tm, HEAD_DIM), lambda i, j: (i, 0, j, 0))],
        out_shape=[shape(A_Q), shape(2 * A_KVW), shape(2 * A_KVW), shape(2 * A_KVW), shape(128), shape(B_WIDTH),
                   jax.ShapeDtypeStruct((b, 8, t, HEAD_DIM), BF16)],
        compiler_params=pltpu.CompilerParams(
            dimension_semantics=("parallel", "parallel"), vmem_limit_bytes=V7X_VMEM_LIMIT_BYTES),
        name="inproj_nsa_s5",
    )(x, gain.reshape(1, d), w, tile2(q_gain), tile2(k_gain))


def _inproj_c_body(x_ref, gain_ref, w_ref, qg_ref, kg_ref, q_ref, kv_ref, kvt_ref):
    xn = _rms_rows(x_ref[...], gain_ref[...]).astype(BF16)
    z = jnp.dot(xn, w_ref[...], preferred_element_type=F32)
    n_q = C_HEADS * HEAD_DIM
    for j in range(n_q // 128):
        q_ref[:, j * 128:(j + 1) * 128] = _rms_heads128(z[:, j * 128:(j + 1) * 128], qg_ref[...])
    k = _rms_heads128(z[:, n_q:n_q + C_KV * HEAD_DIM], kg_ref[...])
    v = z[:, n_q + C_KV * HEAD_DIM:]
    kv_ref[:, :C_KV * HEAD_DIM] = k
    kv_ref[:, C_KV * HEAD_DIM:] = v
    _heads_first(kvt_ref, 0, k, v, C_KV)


def _inproj_c(x, gain, w_in, q_gain, k_gain):
    b, t, d = x.shape
    tm = min(t, PROJ_ROW_TILE)
    n_in = w_in.shape[1]
    tile2 = lambda g: jnp.tile(g, 2).reshape(1, 128)
    rows = lambda width: pl.BlockSpec((None, tm, width), lambda i, j: (i, j, 0))
    shape = lambda width: jax.ShapeDtypeStruct((b, t, width), F32)
    return pl.pallas_call(
        _inproj_c_body,
        grid=(b, t // tm),
        in_specs=[rows(d), _const_spec((1, d)), _const_spec((d, n_in)), _const_spec((1, 128)), _const_spec((1, 128))],
        out_specs=[rows(C_HEADS * HEAD_DIM), rows(2 * C_KV * HEAD_DIM),
                   pl.BlockSpec((None, 2 * C_KV, tm, HEAD_DIM), lambda i, j: (i, 0, j, 0))],
        out_shape=[shape(C_HEADS * HEAD_DIM), shape(2 * C_KV * HEAD_DIM),
                   jax.ShapeDtypeStruct((b, 2 * C_KV, t, HEAD_DIM), BF16)],
        compiler_params=pltpu.CompilerParams(
            dimension_semantics=("parallel", "parallel"), vmem_limit_bytes=V7X_VMEM_LIMIT_BYTES),
        name="inproj_swa",
    )(x, gain.reshape(1, d), w_in.astype(BF16), tile2(q_gain), tile2(k_gain))


FFN_ROW_TILE = 512
FFN_COL_CHUNK = 2816


def _mixer_residual(y_ref, x_ref, mix_refs, wout_refs):
    y_ref[...] = x_ref[...]
    for m_ref, w_ref in zip(mix_refs, wout_refs):
        y_ref[...] += jnp.dot(m_ref[...].astype(BF16), w_ref[...], preferred_element_type=F32)
    return y_ref[...]


def _tail_prompt_body(*refs, n_mix, tm, ffc):
    x_ref = refs[0]
    mix_refs = refs[1:1 + n_mix]
    wout_refs = refs[1 + n_mix:1 + 2 * n_mix]
    (gain_ref, prev_ref, wup_ref, wgate_ref, cw_ref, cb_ref, wdown_ref, y_ref, cs_ref, hbuf_ref) = refs[1 + 2 * n_mix:]
    t = pl.program_id(1)
    xn = _rms_rows(_mixer_residual(y_ref, x_ref, mix_refs, wout_refs), gain_ref[...]).astype(BF16)
    for c in range(D_FF // ffc):
        lo = c * ffc
        h = jnp.dot(xn, wup_ref[:, lo:lo + ffc], preferred_element_type=F32)
        g = jnp.dot(xn, wgate_ref[:, lo:lo + ffc], preferred_element_type=F32)

        @pl.when(t == 0)
        def _():
            hbuf_ref[c, 6:8, :] = prev_ref[:, lo:lo + ffc]

        hbuf_ref[c, 8:8 + tm, :] = h
        hm1 = hbuf_ref[c, 7:7 + tm, :]
        hm2 = hbuf_ref[c, 6:6 + tm, :]
        cw = cw_ref[:, lo:lo + ffc]
        hc = cw[0:1] * hm2 + cw[1:2] * hm1 + cw[2:3] * h + cb_ref[:, lo:lo + ffc]
        a = (_gelu_tanh(hc) * g).astype(BF16)
        y_ref[...] += jnp.dot(a, wdown_ref[lo:lo + ffc, :], preferred_element_type=F32)
        hbuf_ref[c, 0:8, :] = h[tm - 8:tm, :]
        cs_ref[:, lo:lo + ffc] = h[tm - 2:tm, :]


def _tail_prompt(x, mixes, wouts, gain, prev, wup, wgate, cw, cb, wdown):
    b, t, d = x.shape
    tm, ffc = FFN_ROW_TILE, FFN_COL_CHUNK
    rows = lambda width: pl.BlockSpec((None, tm, width), lambda i, j: (i, j, 0))
    state = pl.BlockSpec((None, CONV_W - 1, D_FF), lambda i, j: (i, 0, 0))
    return pl.pallas_call(
        functools.partial(_tail_prompt_body, n_mix=len(mixes), tm=tm, ffc=ffc),
        grid=(b, t // tm),
        in_specs=[rows(d)] + [rows(m.shape[-1]) for m in mixes] + [_const_spec(w.shape) for w in wouts] + [
            _const_spec((1, d)), state, _const_spec((d, D_FF)), _const_spec((d, D_FF)),
            _const_spec((CONV_W, D_FF)), _const_spec((1, D_FF)), _const_spec((D_FF, d))],
        out_specs=[rows(d), state],
        out_shape=[jax.ShapeDtypeStruct((b, t, d), F32), jax.ShapeDtypeStruct((b, CONV_W - 1, D_FF), F32)],
        scratch_shapes=[pltpu.VMEM((D_FF // ffc, 8 + tm, ffc), F32)],
        compiler_params=pltpu.CompilerParams(
            dimension_semantics=("parallel", "arbitrary"), vmem_limit_bytes=V7X_VMEM_LIMIT_BYTES),
        name="tail_prompt",
    )(x, *mixes, *wouts, gain, prev, wup, wgate, cw, cb, wdown)


def _tail_sample_body(*refs, n_mix, ffc):
    x_ref = refs[0]
    mix_refs = refs[1:1 + n_mix]
    wout_refs = refs[1 + n_mix:1 + 2 * n_mix]
    (gain_ref, prev_ref, wup_ref, wgate_ref, cw_ref, cb_ref, wdown_ref, y_ref, cs_ref) = refs[1 + 2 * n_mix:]
    xn = _rms_rows(_mixer_residual(y_ref, x_ref, mix_refs, wout_refs), gain_ref[...]).astype(BF16)
    for c in range(D_FF // ffc):
        lo = c * ffc
        h = jnp.dot(xn, wup_ref[:, lo:lo + ffc], preferred_element_type=F32)
        g = jnp.dot(xn, wgate_ref[:, lo:lo + ffc], preferred_element_type=F32)
        hm2 = prev_ref[:, lo:lo + ffc]
        hm1 = prev_ref[:, D_FF + lo:D_FF + lo + ffc]
        cw = cw_ref[:, lo:lo + ffc]
        hc = cw[0:1] * hm2 + cw[1:2] * hm1 + cw[2:3] * h + cb_ref[:, lo:lo + ffc]
        a = (_gelu_tanh(hc) * g).astype(BF16)
        y_ref[...] += jnp.dot(a, wdown_ref[lo:lo + ffc, :], preferred_element_type=F32)
        cs_ref[:, lo:lo + ffc] = hm1
        cs_ref[:, D_FF + lo:D_FF + lo + ffc] = h


def _tail_sample(x, mixes, wouts, gain, prev, wup, wgate, cw, cb, wdown):
    n, d = x.shape
    full = lambda shape: _const_spec(shape)
    return pl.pallas_call(
        functools.partial(_tail_sample_body, n_mix=len(mixes), ffc=FFN_COL_CHUNK),
        grid=(1,),
        in_specs=[full((n, d))] + [full(m.shape) for m in mixes] + [full(w.shape) for w in wouts] + [
            full((1, d)), full((n, (CONV_W - 1) * D_FF)), full((d, D_FF)), full((d, D_FF)),
            full((CONV_W, D_FF)), full((1, D_FF)), full((D_FF, d))],
        out_specs=[pl.BlockSpec((n, d), lambda i: (0, 0)), pl.BlockSpec((n, (CONV_W - 1) * D_FF), lambda i: (0, 0))],
        out_shape=[jax.ShapeDtypeStruct((n, d), F32), jax.ShapeDtypeStruct((n, (CONV_W - 1) * D_FF), F32)],
        compiler_params=pltpu.CompilerParams(
            dimension_semantics=("arbitrary",), vmem_limit_bytes=V7X_VMEM_LIMIT_BYTES),
        name="tail_sample",
    )(x, *mixes, *wouts, gain, prev, wup, wgate, cw, cb, wdown)


def kernel(x_prompt, x_sample, cache_nsa_cmp, cache_nsa_sel, cache_nsa_win, state_s5_re, state_s5_im,
           cache_swa, state_ffn_conv, page_table, rel_bias, norm_mix, norm_ffn, a_w_in, a_w_out,
           nsa_q_gain, nsa_k_gain, nsa_cmp_pos, nsa_cmp_w1, nsa_cmp_w2, s5_a_re, s5_a_im, s5_log_dt,
           s5_b_re, s5_b_im, s5_c_re, s5_c_im, s5_d, s5_w_glu, s5_b_glu, c_w_in, c_w_out, c_q_gain,
           c_k_gain, c_sinks, ffn_w_up, ffn_w_gate, ffn_conv_w, ffn_conv_b, ffn_w_down):
    bp, tp, _ = x_prompt.shape
    bs, ts, _ = x_sample.shape
    assert ts == 1 and DEPTH == 2
    tab_a = rel_bias[:, :A_HEADS].reshape(NUM_BUCKETS, A_KV, A_REP)
    tab_c = rel_bias[:, :C_HEADS].reshape(NUM_BUCKETS, C_KV, C_REP)
    kv6 = lambda x: x.reshape(x.shape[:-1] + (2, x.shape[-1] // (2 * HEAD_DIM), HEAD_DIM))
    hp, hs = x_prompt, x_sample.reshape(1, bs, D_MODEL)
    conv_p, conv_s = [], []

    def tail(layer, hp, hs, mixes_p, mixes_s, wouts):
        wouts = [w.astype(BF16) for w in wouts]
        ffn = (norm_ffn[layer].reshape(1, D_MODEL),)
        wts = (ffn_w_up[layer].astype(BF16), ffn_w_gate[layer].astype(BF16), ffn_conv_w[layer],
               ffn_conv_b[layer].reshape(1, D_FF), ffn_w_down[layer].astype(BF16))
        hp, cp = _tail_prompt(hp, mixes_p, wouts, *ffn, jnp.zeros((bp, CONV_W - 1, D_FF), F32), *wts)
        hs2, cs = _tail_sample(hs[0], [m.reshape(bs, -1) for m in mixes_s], wouts, *ffn,
                               state_ffn_conv[layer].reshape(bs, (CONV_W - 1) * D_FF), *wts)
        conv_p.append(cp)
        conv_s.append(cs.reshape(bs, CONV_W - 1, D_FF))
        return hp, hs2.reshape(1, bs, D_MODEL)

    proj = (norm_mix[0], a_w_in[0], nsa_q_gain[0], nsa_k_gain[0])
    qp, cmp_p, sel_p, win_p, gate_p, up, kvt_p = _inproj_a(hp, *proj)
    qs, cmp_s, sel_s, win_s, gate_s, us, _ = _inproj_a(hs, *proj)
    cmp_w = _compress_weights(nsa_cmp_pos[0], nsa_cmp_w1[0], nsa_cmp_w2[0], nsa_k_gain[0])
    kc, vc = _cmp_prompt(cmp_p, cmp_w)
    o_ap = _nsa_prompt_attend(qp, gate_p, kc, vc, kvt_p, tab_a)
    o_as, nsa_win_s = _nsa_sample(qs.reshape(bs, A_KV, A_REP, HEAD_DIM),
                                  gate_s[0, :, :A_GATE].reshape(bs, A_KV, A_REP, 3),
                                  sel_s.reshape(bs, 2, A_KV, HEAD_DIM), win_s.reshape(bs, 2, A_KV, HEAD_DIM),
                                  cache_nsa_cmp[0], cache_nsa_sel[0], cache_nsa_win[0], page_table, cmp_w, tab_a)
    s5w = _s5_weights(s5_a_re[0], s5_a_im[0], s5_log_dt[0], s5_b_re[0], s5_b_im[0], s5_c_re[0], s5_c_im[0],
                      s5_d[0], s5_w_glu[0], s5_b_glu[0])
    h0 = jnp.zeros((bp, S5_NS), F32)
    o_bp, hr_p, hi_p = _s5_mix(up, h0, h0, s5w)
    o_bs, hr_s, hi_s = _s5_mix(us.reshape(bs, 1, B_WIDTH), state_s5_re[0].reshape(bs, S5_NS),
                               state_s5_im[0].reshape(bs, S5_NS), s5w)
    hp, hs = tail(0, hp, hs, [o_ap, o_bp], [o_as, o_bs], [a_w_out[0][:A_Q], a_w_out[0][A_Q:]])

    proj = (norm_mix[1], c_w_in[0], c_q_gain[0], c_k_gain[0])
    qcp, kv_p, kvt_c = _inproj_c(hp, *proj)
    qcs, kv_s, _ = _inproj_c(hs, *proj)
    o_cp = _swa_prompt(qcp, kvt_c, c_sinks[0], tab_c)
    new_kv = kv_s.reshape(bs, 2, C_KV, HEAD_DIM)
    o_cs, swa_s = _swa_sample(qcs.reshape(bs, C_KV, C_REP, HEAD_DIM), cache_swa[0],
                              jnp.transpose(new_kv, (0, 2, 1, 3)), c_sinks[0], tab_c)
    hp, hs = tail(1, hp, hs, [o_cp], [o_cs], [c_w_out[0]])

    state = lambda x, n: x.reshape(1, n, B_GROUPS, B_STATE)
    return (hp, hs.reshape(bs, ts, D_MODEL),
            kv6(cmp_p)[None], kv6(cmp_s).reshape(1, bs, ts, 2, A_KV, HEAD_DIM),
            kv6(sel_p)[None], kv6(sel_s).reshape(1, bs, ts, 2, A_KV, HEAD_DIM),
            kv6(win_p)[None, :, -min(WIN_A, tp):], nsa_win_s[None],
            state(hr_p, bp), state(hi_p, bp), state(hr_s, bs), state(hi_s, bs),
            kv6(kv_p)[None, :, -min(WIN_C, tp):], swa_s[None],
            jnp.stack(conv_p), jnp.stack(conv_s))
```

```python
import functools
import math

import jax
import jax.numpy as jnp
import numpy as np
from jax import lax
from jax.experimental import pallas as pl
from jax.experimental.pallas import tpu as pltpu

D_MODEL = 1024
DEPTH = 2
PAGE_SIZE = 128
HEAD_DIM = 64
A_HEADS = 8
A_KV = 2
A_REP = A_HEADS // A_KV
A_Q = A_HEADS * HEAD_DIM
A_KVW = A_KV * HEAD_DIM
A_GATE = 3 * A_HEADS
CMP_LEN = 32
CMP_STRIDE = 16
SEL_BLOCK = 64
SEL_TOPK = 16
WIN_A = 512
FORCE_BONUS = 1000.0
B_WIDTH = D_MODEL // 2
B_GROUP = 16
B_GROUPS = B_WIDTH // B_GROUP
B_STATE = 64
C_HEADS = D_MODEL // HEAD_DIM
C_KV = 2
C_REP = C_HEADS // C_KV
WIN_C = 128
NUM_BUCKETS = 32
MAX_DISTANCE = 128
D_FF = 2816
CONV_W = 3
EPS = 1e-6

F32 = jnp.float32
BF16 = jnp.bfloat16

V7X_VMEM_LIMIT_BYTES = 56 * 1024 * 1024


def _gelu_tanh(x):
    c = math.sqrt(2.0 / math.pi)
    return 0.5 * x * (1.0 + jnp.tanh(x * (c + (c * 0.044715) * (x * x))))


def _rms_rows(x, gain):
    return x * lax.rsqrt(jnp.mean(x * x, axis=-1, keepdims=True) + EPS) * gain


def _const_spec(shape):
    zeros = (0,) * len(shape)
    return pl.BlockSpec(shape, lambda *_: zeros, pipeline_mode=pl.Buffered(1))


NSA_Q_TILE = 256
NSA_FAR_TILE = 512
NSA_HEADS_PER_PASS = 4
NEG_INF = float("-inf")


def _dot_nt(a, b):
    return lax.dot_general(a, b, (((1,), (1,)), ((), ())), preferred_element_type=F32)


def _dot_tn(a, b):
    return lax.dot_general(a, b, (((0,), (0,)), ((), ())), preferred_element_type=F32)


def _softmax_start(s, v):
    m = jnp.max(s, axis=0, keepdims=True)
    e = jnp.exp(s - m)
    return m, jnp.sum(e, axis=0, keepdims=True), _dot_tn(v, e.astype(BF16))


def _softmax_more(carry, s, v):
    m, l, acc = carry
    m_new = jnp.maximum(m, jnp.max(s, axis=0, keepdims=True))
    alpha = jnp.exp(m - m_new)
    e = jnp.exp(s - m_new)
    return m_new, alpha * l + jnp.sum(e, axis=0, keepdims=True), alpha * acc + _dot_tn(v, e.astype(BF16))


def _nsa_prompt_body(q_ref, gate_ref, kc_ref, vc_ref, kvt_ref, btile_ref, cover_ref, o_ref, sel_scr, *, tq):
    width = A_REP * HEAD_DIM
    gates_t = gate_ref[...].T
    for g in range(A_KV):
        _nsa_prompt_group(g, q_ref.at[:, g * width:(g + 1) * width], gates_t, kc_ref.at[g], vc_ref.at[g],
                          kvt_ref.at[g], kvt_ref.at[A_KV + g], kvt_ref.at[2 * A_KV + g], kvt_ref.at[3 * A_KV + g],
                          btile_ref.at[g], cover_ref, o_ref.at[:, g * width:(g + 1) * width], sel_scr.at[g], tq)


def _nsa_prompt_group(g, q_ref, gates_t, kc_ref, vc_ref, ks_ref, vs_ref, kw_ref, vw_ref, btile_ref, cover_ref,
                      o_ref, sel_ref, tq):
    qt = pl.program_id(1)
    q0 = pl.multiple_of(qt * tq, tq)
    cols = A_REP * tq
    q = q_ref[...]
    qs = jnp.concatenate([q[:, r * HEAD_DIM:(r + 1) * HEAD_DIM] for r in range(A_REP)], axis=0)
    qs = (qs * (HEAD_DIM ** -0.5)).astype(BF16)

    def q_pos(height):
        return q0 + (lax.broadcasted_iota(jnp.int32, (height, cols), 1) & (tq - 1))

    def key_idx(height):
        return lax.broadcasted_iota(jnp.int32, (height, cols), 0)

    n_idx = key_idx(128)
    valid_c = (n_idx * CMP_STRIDE + (CMP_LEN - 1) <= q_pos(128)) & (n_idx < 127)
    s_c = jnp.where(valid_c, _dot_nt(kc_ref[...], qs), NEG_INF)
    m_c = jnp.max(s_c, axis=0, keepdims=True)
    m_c = jnp.where(m_c == NEG_INF, 0.0, m_c)
    e_c = jnp.exp(s_c - m_c)
    d_c = jnp.sum(e_c, axis=0, keepdims=True)
    p_c = (e_c / jnp.where(d_c > 0, d_c, 1.0)).astype(BF16)
    o_c = _dot_tn(vc_ref[...], p_c)
    imp_heads = jnp.dot(cover_ref[...], p_c, preferred_element_type=F32)
    imp = sum(imp_heads[0:32, r * tq:(r + 1) * tq] for r in range(A_REP))
    s_idx = lax.broadcasted_iota(jnp.int32, (32, tq), 0)
    qblk = (q0 + lax.broadcasted_iota(jnp.int32, (32, tq), 1)) >> 6
    forced = (s_idx == 0) | (s_idx == qblk) | (s_idx == qblk - 1)
    allowed = s_idx <= qblk
    score = jnp.where(allowed, imp + jnp.where(forced, FORCE_BONUS, 0.0), NEG_INF)
    rank = jnp.zeros((32, tq), F32)
    for j in range(32):
        other = score[j:j + 1, :]
        beats = (other > score) | ((other == score) & (s_idx > j))
        rank = rank + jnp.where(beats, 1.0, 0.0)
    sel = jnp.where((rank < SEL_TOPK) & allowed, 1.0, 0.0)
    hcols = NSA_HEADS_PER_PASS * tq
    far_end = q0 - tq
    sel_far_only = jnp.where(s_idx < far_end // SEL_BLOCK, sel, 0.0)
    sel_ref[...] = jnp.concatenate([jnp.concatenate([sel] * NSA_HEADS_PER_PASS, axis=1),
                                    jnp.concatenate([sel_far_only] * NSA_HEADS_PER_PASS, axis=1)], axis=0)

    def block_mask(k0, n_blocks, base=0):
        first = base + k0 // SEL_BLOCK
        return jnp.concatenate([jnp.broadcast_to(sel_ref[pl.ds(first + j, 1), :], (SEL_BLOCK, hcols))
                                for j in range(n_blocks)], axis=0) > 0.5

    def hq_pos(height):
        return q0 + (lax.broadcasted_iota(jnp.int32, (height, hcols), 1) & (tq - 1))

    def hkey_idx(height):
        return lax.broadcasted_iota(jnp.int32, (height, hcols), 0)

    prev0 = pl.multiple_of(jnp.maximum(q0 - tq, 0), tq)
    sel_near = jnp.concatenate([block_mask(prev0, tq // SEL_BLOCK), block_mask(q0, tq // SEL_BLOCK)], axis=0)
    n_far = (jnp.maximum(far_end, 0) + NSA_FAR_TILE - 1) // NSA_FAR_TILE
    w_far = WIN_A - tq
    wf0 = pl.multiple_of(jnp.maximum(q0 - WIN_A, 0), tq)
    wpos = wf0 + hkey_idx(w_far)
    wmask = (hq_pos(w_far) - wpos < WIN_A) & (wpos < far_end)

    outs = []
    for h0 in range(0, A_REP, NSA_HEADS_PER_PASS):
        csl = slice(h0 * tq, h0 * tq + hcols)
        qh = qs[h0 * tq:h0 * tq + hcols]
        btile = btile_ref[:, csl]

        def near(k_ref, v_ref, extra_mask):
            k = jnp.concatenate([k_ref[pl.ds(prev0, tq), :], k_ref[pl.ds(q0, tq), :]], axis=0)
            v = jnp.concatenate([v_ref[pl.ds(prev0, tq), :], v_ref[pl.ds(q0, tq), :]], axis=0)
            s = _dot_nt(k, qh) + btile
            return _softmax_start(s if extra_mask is None else jnp.where(extra_mask, s, NEG_INF), v)

        def sel_far(i, carry):
            k0 = pl.multiple_of(i * NSA_FAR_TILE, NSA_FAR_TILE)
            s = _dot_nt(ks_ref[pl.ds(k0, NSA_FAR_TILE), :], qh)
            mask = block_mask(k0, NSA_FAR_TILE // SEL_BLOCK, base=32)
            return _softmax_more(carry, jnp.where(mask, s, NEG_INF), vs_ref[pl.ds(k0, NSA_FAR_TILE), :])

        _, l_s, acc_s = lax.fori_loop(0, n_far, sel_far, near(ks_ref, vs_ref, sel_near))
        o_s = acc_s / l_s

        s_w = jnp.where(wmask, _dot_nt(kw_ref[pl.ds(wf0, w_far), :], qh), NEG_INF)
        _, l_w, acc_w = _softmax_more(near(kw_ref, vw_ref, None), s_w, vw_ref[pl.ds(wf0, w_far), :])
        o_w = acc_w / l_w

        for i in range(NSA_HEADS_PER_PASS):
            r = h0 + i
            sl = slice(i * tq, (i + 1) * tq)
            row = 3 * (g * A_REP + r)
            out_t = (gates_t[row:row + 1] * o_c[:, r * tq:(r + 1) * tq] + gates_t[row + 1:row + 2] * o_s[:, sl]
                     + gates_t[row + 2:row + 3] * o_w[:, sl])
            outs.append(out_t.T)
    o_ref[...] = jnp.concatenate(outs, axis=1)


def _bucket_np(dist):
    n = np.maximum(dist, 0)
    exact = NUM_BUCKETS // 2
    nf = np.maximum(n, exact).astype(np.float64)
    large = exact + (np.log(nf / exact) / math.log(MAX_DISTANCE / exact) * (NUM_BUCKETS - exact)).astype(np.int64)
    return np.where(n < exact, n, np.minimum(large, NUM_BUCKETS - 1)).astype(np.int32)


def _bias_lookup(tab, dist):
    bucket = _bucket_np(np.asarray(dist))
    onehot = (jnp.asarray(bucket.reshape(-1, 1)) == jnp.arange(NUM_BUCKETS)[None, :]).astype(F32)
    flat = jnp.dot(onehot, tab.reshape(NUM_BUCKETS, -1), precision=lax.Precision.HIGHEST)
    return flat.reshape(bucket.shape + tab.shape[1:])


def _near_bias_tiles(tab, tq, window):
    i = np.arange(tq)[:, None]
    j = np.arange(2 * tq)[None, :]
    dist = tq + i - j
    ok = (dist >= 0) & (dist < window)
    bias = jnp.transpose(_bias_lookup(tab, dist), (2, 3, 0, 1))
    tiles = jnp.stack([jnp.where(jnp.asarray(ok & (j >= tq)), bias, NEG_INF), jnp.where(jnp.asarray(ok), bias, NEG_INF)])
    return tiles.reshape(2, tab.shape[1], tab.shape[2] * tq, 2 * tq)


def _cover_matrix(n_cmp_pad, n_sel):
    n = np.arange(n_cmp_pad)
    c_start = n * CMP_STRIDE
    c_end = c_start + CMP_LEN - 1
    s_start = np.arange(128) * SEL_BLOCK
    cover = (c_start[:, None] < s_start[None, :] + SEL_BLOCK) & (c_end[:, None] >= s_start[None, :])
    cover &= (np.arange(128)[None, :] < n_sel)
    return np.tile(cover.astype(np.float32), (A_REP, 1))


def _nsa_prompt_attend(q, gates, kc, vc, kvt, tab):
    b, t, _ = q.shape
    tq = NSA_Q_TILE
    assert t % NSA_FAR_TILE == 0 and t // SEL_BLOCK <= 32 and t >= WIN_A and kc.shape[2] == 128
    btile = jnp.swapaxes(_near_bias_tiles(tab - tab[NUM_BUCKETS - 1], tq, 2 * tq), -1, -2)
    cover = jnp.asarray(_cover_matrix(128, t // SEL_BLOCK)[:128].T, BF16)
    cmp_spec = pl.BlockSpec((None, A_KV, 128, HEAD_DIM), lambda i, j: (i, 0, 0, 0))
    return pl.pallas_call(
        functools.partial(_nsa_prompt_body, tq=tq),
        grid=(b, t // tq),
        in_specs=[
            pl.BlockSpec((None, tq, A_Q), lambda i, j: (i, j, 0)),
            pl.BlockSpec((None, tq, 128), lambda i, j: (i, j, 0)),
            cmp_spec, cmp_spec,
            pl.BlockSpec((None, 4 * A_KV, t, HEAD_DIM), lambda i, j: (i, 0, 0, 0)),
            pl.BlockSpec((None, A_KV, 2 * tq, A_REP * tq), lambda i, j: (jnp.minimum(j, 1), 0, 0, 0)),
            pl.BlockSpec((128, 128), lambda i, j: (0, 0)),
        ],
        out_specs=pl.BlockSpec((None, tq, A_Q), lambda i, j: (i, j, 0)),
        out_shape=jax.ShapeDtypeStruct((b, t, A_Q), F32),
        scratch_shapes=[pltpu.VMEM((A_KV, 2 * 32, NSA_HEADS_PER_PASS * tq), F32)],
        compiler_params=pltpu.CompilerParams(
            dimension_semantics=("parallel", "arbitrary"), vmem_limit_bytes=V7X_VMEM_LIMIT_BYTES),
        name="nsa_prompt",
    )(q, gates, kc, vc, kvt, btile, cover)


S5_NS = B_GROUPS * B_STATE
S5_T_CHUNK = 64
S5_STRIP = 512
S5_HALVES = 2


def _s5_body(u_ref, h0r_ref, h0i_ref, ar_ref, ai_ref, ldt_ref, wb_ref, wc_ref, d_ref, wglu_ref, bglu_ref,
             o_ref, hr_ref, hi_ref, coef_ref, st_ref, xbuf_ref, ubuf_ref, obuf_ref, *, nb, steps, interleave):
    c = pl.program_id(0)

    @pl.when(c == 0)
    def _():
        dt = jnp.exp(ldt_ref[...])
        ar, ai = ar_ref[...], ai_ref[...]
        mag = jnp.exp(ar * dt)
        abr, abi = mag * jnp.cos(ai * dt), mag * jnp.sin(ai * dt)
        den = ar * ar + ai * ai
        wr = ((abr - 1.0) * ar + abi * ai) / den
        wi = (abi * ar - (abr - 1.0) * ai) / den
        for k, val in enumerate((abr, abi, wr, wi)):
            coef_ref[k] = jnp.broadcast_to(val, (nb, S5_NS))
        st_ref[0] = h0r_ref[...]
        st_ref[1] = h0i_ref[...]

    if interleave:
        for b in range(nb):
            for j in range(B_WIDTH // 128):
                ubuf_ref.at[j][pl.ds(b, steps, stride=nb), :] = u_ref[b, :, j * 128:(j + 1) * 128]
        u = jnp.concatenate([ubuf_ref[j] for j in range(B_WIDTH // 128)], axis=1)
    else:
        u = u_ref[...]
    hc, hs = B_WIDTH // S5_HALVES, S5_NS // S5_HALVES
    ub = u.astype(BF16)
    for h in range(S5_HALVES):
        bu = jnp.dot(ub[:, h * hc:(h + 1) * hc], wb_ref[h], preferred_element_type=F32)
        xbuf_ref[:, h * hs:(h + 1) * hs] = bu[:, :hs]
        xbuf_ref[:, S5_NS + h * hs:S5_NS + (h + 1) * hs] = bu[:, hs:]

    for lo in range(0, S5_NS, S5_STRIP):
        re = slice(lo, lo + S5_STRIP)
        im = slice(S5_NS + lo, S5_NS + lo + S5_STRIP)
        abr, abi, wr, wi = (coef_ref[k, :, re] for k in range(4))

        def step(t, carry):
            sr, si = carry
            r0 = pl.multiple_of(t * nb, nb)
            bur = xbuf_ref[pl.ds(r0, nb), re]
            bui = xbuf_ref[pl.ds(r0, nb), im]
            nsr = abr * sr - abi * si + (wr * bur - wi * bui)
            nsi = abr * si + abi * sr + (wr * bui + wi * bur)
            xbuf_ref[pl.ds(r0, nb), re] = nsr
            xbuf_ref[pl.ds(r0, nb), im] = nsi
            return nsr, nsi

        sr, si = lax.fori_loop(0, steps, step, (st_ref[0, :, re], st_ref[1, :, re]),
                               unroll=min(steps, 8))
        st_ref[0, :, re] = sr
        st_ref[1, :, re] = si

    y = jnp.concatenate(
        [jnp.dot(jnp.concatenate([xbuf_ref[:, h * hs:(h + 1) * hs], xbuf_ref[:, S5_NS + h * hs:S5_NS + (h + 1) * hs]],
                                 axis=1).astype(BF16), wc_ref[h], preferred_element_type=F32)
         for h in range(S5_HALVES)], axis=1) + d_ref[...] * u
    z = _gelu_tanh(y)
    gate = jnp.dot(z.astype(BF16), wglu_ref[...], preferred_element_type=F32) + bglu_ref[...]
    out = z * (1.0 / (1.0 + jnp.exp(-gate)))
    if interleave:
        for j in range(B_WIDTH // 128):
            obuf_ref[j] = out[:, j * 128:(j + 1) * 128]
        for b in range(nb):
            for j in range(B_WIDTH // 128):
                o_ref[b, :, j * 128:(j + 1) * 128] = obuf_ref.at[j][pl.ds(b, steps, stride=nb), :]
    else:
        o_ref[...] = out
    hr_ref[...] = st_ref[0]
    hi_ref[...] = st_ref[1]


def _s5_weights(a_re, a_im, log_dt, b_re, b_im, c_re, c_im, d, w_glu, b_glu):
    eye = jnp.eye(B_GROUPS, dtype=F32)
    blk_in = lambda w: jnp.einsum('hg,gpc->hcgp', eye, w).reshape(B_WIDTH, S5_NS)
    blk_out = lambda w: jnp.einsum('gh,gcp->gphc', eye, w).reshape(S5_NS, B_WIDTH)
    hc, hs = B_WIDTH // S5_HALVES, S5_NS // S5_HALVES
    diag = lambda w, h: w[h * hc:(h + 1) * hc, h * hs:(h + 1) * hs]
    wb = jnp.stack([jnp.concatenate([diag(blk_in(b_re), h), diag(blk_in(b_im), h)], axis=1)
                    for h in range(S5_HALVES)]).astype(BF16)
    diag_t = lambda w, h: w[h * hs:(h + 1) * hs, h * hc:(h + 1) * hc]
    wc = jnp.stack([jnp.concatenate([diag_t(blk_out(c_re), h), -diag_t(blk_out(c_im), h)], axis=0)
                    for h in range(S5_HALVES)]).astype(BF16)
    flat = lambda x: x.reshape(1, S5_NS)
    return (flat(a_re), flat(a_im), flat(jnp.repeat(log_dt, B_STATE)), wb, wc, d.reshape(1, B_WIDTH),
            w_glu.astype(BF16), b_glu.reshape(1, B_WIDTH))


def _s5_mix(u, h_re, h_im, weights):
    nb, t, _ = u.shape
    interleave = t > 1
    steps = min(t, S5_T_CHUNK)
    rows = nb * steps
    body = functools.partial(_s5_body, nb=nb, steps=steps, interleave=interleave)
    if interleave:
        u_in = u
        u_spec = pl.BlockSpec((nb, steps, B_WIDTH), lambda c: (0, c, 0))
        o_shape = jax.ShapeDtypeStruct((nb, t, B_WIDTH), F32)
        scratch_rows = rows
    else:
        u_in = u.reshape(nb, B_WIDTH)
        u_spec = pl.BlockSpec((nb, B_WIDTH), lambda c: (0, 0))
        o_shape = jax.ShapeDtypeStruct((nb, B_WIDTH), F32)
        scratch_rows = 8
    o, hr, hi = pl.pallas_call(
        body,
        grid=(t // steps,),
        in_specs=[
            u_spec, _const_spec((nb, S5_NS)), _const_spec((nb, S5_NS)),
            _const_spec((1, S5_NS)), _const_spec((1, S5_NS)), _const_spec((1, S5_NS)),
            _const_spec((S5_HALVES, B_WIDTH // S5_HALVES, 2 * S5_NS // S5_HALVES)),
            _const_spec((S5_HALVES, 2 * S5_NS // S5_HALVES, B_WIDTH // S5_HALVES)),
            _const_spec((1, B_WIDTH)), _const_spec((B_WIDTH, B_WIDTH)), _const_spec((1, B_WIDTH)),
        ],
        out_specs=[u_spec, pl.BlockSpec((nb, S5_NS), lambda c: (0, 0)), pl.BlockSpec((nb, S5_NS), lambda c: (0, 0))],
        out_shape=[o_shape, jax.ShapeDtypeStruct((nb, S5_NS), F32), jax.ShapeDtypeStruct((nb, S5_NS), F32)],
        scratch_shapes=[
            pltpu.VMEM((4, nb, S5_NS), F32),
            pltpu.VMEM((2, nb, S5_NS), F32),
            pltpu.VMEM((rows, 2 * S5_NS), F32),
            pltpu.VMEM((B_WIDTH // 128, scratch_rows, 128), F32),
            pltpu.VMEM((B_WIDTH // 128, scratch_rows, 128), F32),
        ],
        compiler_params=pltpu.CompilerParams(
            dimension_semantics=("arbitrary",),
            vmem_limit_bytes=V7X_VMEM_LIMIT_BYTES),
        name="s5_mix",
    )(u_in, h_re, h_im, *weights)
    return o.reshape(nb, t, B_WIDTH), hr, hi


CHUNKS_PER_PAGE = PAGE_SIZE // CMP_STRIDE
SEL_PAD = 256


def _bf16_round(x):
    return x.astype(BF16).astype(F32)


def _compress_tail(c, pos, w2):
    c1 = c[:, 128:]
    nxt = jnp.concatenate([c1[1:], c1[:1]], axis=0)
    hid = (pos + c[:, :128]) + nxt
    return jnp.dot(_gelu_tanh(hid).astype(BF16), w2, preferred_element_type=F32)


def _rms_heads128(x, gain):
    left = lax.broadcasted_iota(jnp.int32, x.shape, 1) < HEAD_DIM
    sq = x * x
    s0 = jnp.sum(jnp.where(left, sq, 0.0), axis=-1, keepdims=True)
    s1 = jnp.sum(jnp.where(left, 0.0, sq), axis=-1, keepdims=True)
    ms = jnp.where(left, s0, s1) * (1.0 / HEAD_DIM)
    return x * lax.rsqrt(ms + EPS) * gain


def _pad_rows8(x):
    return jnp.concatenate([x, jnp.zeros((8 - x.shape[0], x.shape[1]), x.dtype)], axis=0)


def _compress_mlp(row_of_chunks, n_chunks, w_ref, pos, w2_ref):
    acc = jnp.zeros((n_chunks, 2 * A_KVW), F32)
    for sp in range(CMP_STRIDE // 2):
        x = jnp.concatenate([row_of_chunks(2 * sp + e) for e in range(2)], axis=1).astype(BF16)
        acc = acc + jnp.dot(x, w_ref[sp], preferred_element_type=F32)
    return _compress_tail(acc, pos, w2_ref[...])


def _compress_rows(rows_scr, kv, n_chunks, w_ref, pos, w2_ref):
    return _compress_mlp(lambda s: rows_scr.at[kv][pl.ds(s, n_chunks, stride=CMP_STRIDE), :],
                         n_chunks, w_ref, pos, w2_ref)


def _compress_weights(cmp_pos, cmp_w1, cmp_w2, k_gain):
    eye = jnp.eye(A_KV, dtype=F32)
    w1 = cmp_w1.reshape(2, 2, CMP_STRIDE, HEAD_DIM, HEAD_DIM)
    w_big = jnp.einsum('ajsdh,gk->asgdjkh', w1, eye).reshape(2, CMP_STRIDE // 2, 2 * A_KVW, 2 * A_KVW).astype(BF16)
    w_pos = jnp.concatenate([cmp_w1, cmp_w1], axis=-1).astype(BF16)
    pos = jnp.broadcast_to(cmp_pos.reshape(2, 1, CMP_LEN * HEAD_DIM), (2, 8, CMP_LEN * HEAD_DIM)).astype(BF16)
    w2_big = jnp.einsum('ahd,gk->aghkd', cmp_w2, eye).reshape(2, A_KVW, A_KVW).astype(BF16)
    gain2 = jnp.tile(k_gain, A_KV).reshape(1, A_KVW)
    return (pos[0], pos[1], w_pos[0], w_pos[1], w_big[0], w_big[1], w2_big[0], w2_big[1], gain2)


def _compress_specs(full):
    mlp_w = (CMP_STRIDE // 2, 2 * A_KVW, 2 * A_KVW)
    return [full((8, CMP_LEN * HEAD_DIM)), full((8, CMP_LEN * HEAD_DIM)),
            full((CMP_LEN * HEAD_DIM, A_KVW)), full((CMP_LEN * HEAD_DIM, A_KVW)),
            full(mlp_w), full(mlp_w), full((A_KVW, A_KVW)), full((A_KVW, A_KVW)), full((1, A_KVW))]


def _pos_terms(pos_scr, posk_ref, posv_ref, wpos_k_ref, wpos_v_ref):
    pos_scr[0] = jnp.dot(posk_ref[...], wpos_k_ref[...], preferred_element_type=F32)
    pos_scr[1] = jnp.dot(posv_ref[...], wpos_v_ref[...], preferred_element_type=F32)


def _cmp_prompt_body(cmp_ref, posk_ref, posv_ref, wpos_k_ref, wpos_v_ref, wk_ref, wv_ref, w2k_ref, w2v_ref,
                     kgain_ref, kc_ref, vc_ref, pos_scr, rows_scr, *, n_chunks):
    @pl.when(pl.program_id(0) == 0)
    def _():
        _pos_terms(pos_scr, posk_ref, posv_ref, wpos_k_ref, wpos_v_ref)

    rows_scr[0] = cmp_ref[:, :A_KVW]
    rows_scr[1] = cmp_ref[:, A_KVW:]
    kc = _rms_heads128(_compress_rows(rows_scr, 0, n_chunks, wk_ref, pos_scr[0, 0:1], w2k_ref),
                       kgain_ref[...]).astype(BF16)
    vc = _compress_rows(rows_scr, 1, n_chunks, wv_ref, pos_scr[1, 0:1], w2v_ref).astype(BF16)
    for g in range(A_KV):
        kc_ref[g] = kc[:, g * HEAD_DIM:(g + 1) * HEAD_DIM]
        vc_ref[g] = vc[:, g * HEAD_DIM:(g + 1) * HEAD_DIM]


def _cmp_prompt(cmp_rows, cmp_weights):
    b, t, _ = cmp_rows.shape
    n_chunks = t // CMP_STRIDE
    full = lambda shape: pl.BlockSpec(shape, lambda i: (0,) * len(shape))
    out_spec = pl.BlockSpec((None, A_KV, n_chunks, HEAD_DIM), lambda i: (i, 0, 0, 0))
    out_shape = jax.ShapeDtypeStruct((b, A_KV, n_chunks, HEAD_DIM), BF16)
    return pl.pallas_call(
        functools.partial(_cmp_prompt_body, n_chunks=n_chunks),
        grid=(b,),
        in_specs=[pl.BlockSpec((None, t, 2 * A_KVW), lambda i: (i, 0, 0))] + _compress_specs(full),
        out_specs=[out_spec, out_spec],
        out_shape=[out_shape, out_shape],
        scratch_shapes=[pltpu.VMEM((2, 8, A_KVW), F32), pltpu.VMEM((2, t, A_KVW), F32)],
        compiler_params=pltpu.CompilerParams(
            dimension_semantics=("arbitrary",), vmem_limit_bytes=V7X_VMEM_LIMIT_BYTES),
        name="nsa_cmp_prompt",
    )(cmp_rows, *cmp_weights)


def _nsa_sample_cmp_body(pt_ref, *refs, n_pages):
    page_refs = refs[:n_pages]
    (posk_ref, posv_ref, wpos_k_ref, wpos_v_ref, wk_ref, wv_ref, w2k_ref, w2v_ref, kgain_ref,
     kc_ref, vc_ref, pos_scr, rows_scr) = refs[n_pages:]

    @pl.when(pl.program_id(0) == 0)
    def _():
        _pos_terms(pos_scr, posk_ref, posv_ref, wpos_k_ref, wpos_v_ref)

    n_chunks = n_pages * CHUNKS_PER_PAGE
    for kv in range(2):
        for p in range(n_pages):
            rows = page_refs[p][kv].reshape(A_KVW, PAGE_SIZE).T
            by_s = pltpu.einshape("csl->scl", rows.reshape(CHUNKS_PER_PAGE, CMP_STRIDE, A_KVW))
            for s in range(CMP_STRIDE):
                rows_scr[kv, s, p * CHUNKS_PER_PAGE:(p + 1) * CHUNKS_PER_PAGE, :] = by_s[s]

    compress = lambda kv, w_ref, w2_ref: _compress_mlp(lambda s: rows_scr[kv, s], n_chunks, w_ref,
                                                       pos_scr[kv, 0:1], w2_ref)
    kc_ref[...] = _rms_heads128(compress(0, wk_ref, w2k_ref), kgain_ref[...]).astype(BF16)
    vc_ref[...] = compress(1, wv_ref, w2v_ref).astype(BF16)


NSA_PICK_BATCH = 16


def _nsa_sample_pick_body(q_ref, kc_ref, vc_ref, cover_ref, oc_ref, idx_ref, *, n_cmp, n_sel):
    nb, n_chunks = kc_ref.shape[0], kc_ref.shape[1]
    col = lax.broadcasted_iota(jnp.int32, (8, n_chunks), 1)
    imps = []
    for i in range(nb):
        for g in range(A_KV):
            hs = slice(g * HEAD_DIM, (g + 1) * HEAD_DIM)
            q8 = (_pad_rows8(q_ref[i, g]) * (HEAD_DIM ** -0.5)).astype(BF16)
            s = jnp.where(col < n_cmp, _dot_nt(q8, kc_ref[i, :, hs]), NEG_INF)
            e = jnp.exp(s - jnp.max(s, axis=-1, keepdims=True))
            p = (e / jnp.sum(e, axis=-1, keepdims=True)).astype(BF16)
            oc_ref[i, g] = jnp.dot(p, vc_ref[i, :, hs], preferred_element_type=F32)[0:A_REP]
            imps.append(jnp.sum(jnp.dot(p, cover_ref[...], preferred_element_type=F32)[0:A_REP], axis=0, keepdims=True))
    imp = jnp.concatenate(imps, axis=0)
    rows = nb * A_KV
    s_idx = lax.broadcasted_iota(jnp.int32, (rows, SEL_PAD), 1)
    s_idx_f = s_idx.astype(F32)
    forced = (s_idx == 0) | (s_idx == n_sel - 1) | (s_idx == n_sel - 2)
    score = jnp.where(s_idx < n_sel, imp + jnp.where(forced, FORCE_BONUS, 0.0), NEG_INF)
    rank = jnp.zeros((rows, SEL_PAD), F32)
    for j in range(n_sel):
        cj = score[:, j:j + 1]
        beats = (cj > score) | ((cj == score) & (s_idx > j))
        rank = rank + jnp.where(beats, 1.0, 0.0)
    lane = lax.broadcasted_iota(jnp.int32, (rows, 128), 1)
    picks = jnp.zeros((rows, 128), F32)
    for r in range(SEL_TOPK):
        block = jnp.sum(jnp.where(rank == float(r), s_idx_f, 0.0), axis=-1, keepdims=True)
        picks = picks + jnp.where(lane == r, block, 0.0)
    idx_ref[...] = picks.astype(jnp.int32)


NSA_ATT_BATCH = 2


def _nsa_sample_att_body(idx_ref, pt_ref, *refs, n_past_blk):
    n_slots = NSA_ATT_BATCH * A_KV * SEL_TOPK
    slot_refs = refs[:n_slots]
    (q_all, gate_all, oc_all, newsel_all, wcache_all, newwin_all, newcol_all, selb_all, winb_all, newb_all,
     o_all, wout_all) = refs[n_slots:]
    for i in range(NSA_ATT_BATCH):
        b = pl.program_id(0) * NSA_ATT_BATCH + i
        for g in range(A_KV):
            first = (i * A_KV + g) * SEL_TOPK
            _nsa_sample_att_group(b, g, idx_ref, slot_refs[first:first + SEL_TOPK], q_all.at[i, g], gate_all.at[i, g],
                                  oc_all.at[i, g], newsel_all.at[i, g], wcache_all.at[i, :, g], newwin_all.at[i, g],
                                  selb_all.at[g], winb_all.at[g], newb_all.at[g], o_all.at[i, g], n_past_blk)
            for kv in range(2):
                wout_all[i, kv, g] = _shift_in(wcache_all[i, kv, g], newcol_all[i, g][:, kv:kv + 1])


def _shift_in(rows_last, new_col):
    w_len = rows_last.shape[-1]
    lane = lax.broadcasted_iota(jnp.int32, rows_last.shape, 1)
    return jnp.where(lane == w_len - 1, new_col, pltpu.roll(rows_last, w_len - 1, axis=1))


def _nsa_sample_att_group(b, g, idx_ref, page_refs, q_ref, gate_ref, oc_ref, newsel_ref, wcache_ref, newwin_ref,
                          selb_ref, winb_ref, newb_ref, o_ref, n_past_blk):
    blocks_per_page = PAGE_SIZE // SEL_BLOCK
    q8 = (_pad_rows8(q_ref[...]) * (HEAD_DIM ** -0.5)).astype(BF16)
    new_bias = newb_ref[:, 0:1]

    def attend(s, v_t, new_kv):
        k_new = _bf16_round(new_kv[0:1])
        v_new = _bf16_round(new_kv[1:2])
        s_new = jnp.sum(q8.astype(F32) * k_new, axis=-1, keepdims=True) + new_bias
        m = jnp.maximum(jnp.max(s, axis=-1, keepdims=True), s_new)
        e = jnp.exp(s - m)
        e_new = jnp.exp(s_new - m)
        den = jnp.sum(e, axis=-1, keepdims=True) + e_new
        return _dot_nt((e / den).astype(BF16), v_t) + _bf16_round(e_new / den) * v_new

    lane = lax.broadcasted_iota(jnp.int32, (8, PAGE_SIZE), 1)
    near = selb_ref[...]
    bias = []
    for k in range(SEL_TOPK):
        ik = idx_ref[(b * A_KV + g) * SEL_TOPK + k]
        blk = jnp.minimum(ik, n_past_blk - 1)
        near_k = jnp.where(blk // blocks_per_page == (n_past_blk - 1) // blocks_per_page, near, 0.0)
        keep = (lane // SEL_BLOCK == blk % blocks_per_page) & (ik < n_past_blk)
        bias.append(jnp.where(keep, near_k, NEG_INF))
    k_sel = jnp.concatenate([page_refs[k][0].astype(BF16) for k in range(SEL_TOPK)], axis=1)
    v_sel = jnp.concatenate([page_refs[k][1].astype(BF16) for k in range(SEL_TOPK)], axis=1)
    s_sel = jnp.dot(q8, k_sel, preferred_element_type=F32) + jnp.concatenate(bias, axis=1)
    o_s = attend(s_sel, v_sel, newsel_ref[...])

    w_len = wcache_ref.shape[-1]
    wcol = lax.broadcasted_iota(jnp.int32, (8, w_len), 1)
    s_w = jnp.dot(q8, wcache_ref[0].astype(BF16), preferred_element_type=F32) + winb_ref[...]
    o_w = attend(jnp.where(wcol >= 1, s_w, NEG_INF), wcache_ref[1].astype(BF16), newwin_ref[...])
    gates = gate_ref[...]
    o_ref[...] = gates[:, 0:1] * oc_ref[...] + gates[:, 1:2] * o_s[0:A_REP] + gates[:, 2:3] * o_w[0:A_REP]


def _rows_last(cache):
    nd = cache.ndim
    return jnp.transpose(cache, tuple(range(nd - 4)) + (nd - 3, nd - 2, nd - 1, nd - 4))


def _rows_first(cache):
    nd = cache.ndim
    return jnp.transpose(cache, tuple(range(nd - 4)) + (nd - 1, nd - 4, nd - 3, nd - 2))


def _nsa_sample(q, gates, new_sel, new_win, pool_cmp, pool_sel, win_cache, page_table, cmp_weights, tab):
    n, n_pages = page_table.shape
    past = n_pages * PAGE_SIZE
    n_past_blk = past // SEL_BLOCK
    n_sel = n_past_blk + 1
    n_chunks = n_pages * CHUNKS_PER_PAGE
    w_len = win_cache.shape[1]
    blocks_per_page = PAGE_SIZE // SEL_BLOCK
    assert n_sel <= SEL_PAD and w_len == WIN_A and past >= WIN_A and blocks_per_page == 2
    pt_flat = page_table.reshape(-1)

    c_idx = np.arange(n_chunks)
    s_start = np.arange(SEL_PAD) * SEL_BLOCK
    cover = ((c_idx[:, None] * CMP_STRIDE < s_start[None, :] + SEL_BLOCK)
             & (c_idx[:, None] * CMP_STRIDE + CMP_LEN - 1 >= s_start[None, :])
             & (c_idx[:, None] < n_chunks - 1) & (np.arange(SEL_PAD)[None, :] < n_sel))
    cover = jnp.asarray(cover.astype(np.float32), BF16)

    head_spec = lambda last: pl.BlockSpec((None, A_KV, A_REP, last), lambda i, *_: (i, 0, 0, 0))
    page_specs = [pl.BlockSpec((None, 2, A_KV, HEAD_DIM, PAGE_SIZE),
                               functools.partial(lambda i, pt, p: (pt[i * n_pages + p], 0, 0, 0, 0), p=p))
                  for p in range(n_pages)]
    full = lambda shape: pl.BlockSpec(shape, lambda i, *_: (0,) * len(shape))
    pool_c = _rows_last(pool_cmp)
    cmp_spec = pl.BlockSpec((None, n_chunks, A_KVW), lambda i, *_: (i, 0, 0))
    cmp_shape = jax.ShapeDtypeStruct((n, n_chunks, A_KVW), BF16)
    kc, vc = pl.pallas_call(
        functools.partial(_nsa_sample_cmp_body, n_pages=n_pages),
        grid_spec=pltpu.PrefetchScalarGridSpec(
            num_scalar_prefetch=1,
            grid=(n,),
            in_specs=page_specs + _compress_specs(full),
            out_specs=[cmp_spec, cmp_spec],
            scratch_shapes=[pltpu.VMEM((2, 8, A_KVW), F32), pltpu.VMEM((2, CMP_STRIDE, n_chunks, A_KVW), F32)],
        ),
        out_shape=[cmp_shape, cmp_shape],
        compiler_params=pltpu.CompilerParams(
            dimension_semantics=("arbitrary",), vmem_limit_bytes=V7X_VMEM_LIMIT_BYTES),
        name="nsa_sample_cmp",
    )(pt_flat, *([pool_c] * n_pages), *cmp_weights)

    nb = NSA_PICK_BATCH
    assert n % nb == 0
    batch = lambda *tail: pl.BlockSpec((nb,) + tail, lambda i: (i,) + (0,) * len(tail))
    o_c, idx = pl.pallas_call(
        functools.partial(_nsa_sample_pick_body, n_cmp=n_chunks - 1, n_sel=n_sel),
        grid=(n // nb,),
        in_specs=[batch(A_KV, A_REP, HEAD_DIM), batch(n_chunks, A_KVW), batch(n_chunks, A_KVW),
                  pl.BlockSpec((n_chunks, SEL_PAD), lambda i: (0, 0))],
        out_specs=[batch(A_KV, A_REP, HEAD_DIM), pl.BlockSpec((nb * A_KV, 128), lambda i: (i, 0))],
        out_shape=[jax.ShapeDtypeStruct((n, A_KV, A_REP, HEAD_DIM), F32),
                   jax.ShapeDtypeStruct((n * A_KV, 128), jnp.int32)],
        compiler_params=pltpu.CompilerParams(
            dimension_semantics=("arbitrary",), vmem_limit_bytes=V7X_VMEM_LIMIT_BYTES),
        name="nsa_sample_pick",
    )(q, kc, vc, cover)

    rel = lambda dist: jnp.pad(jnp.transpose(_bias_lookup(tab - tab[NUM_BUCKETS - 1], dist), (1, 2, 0)),
                               ((0, 0), (0, 8 - A_REP), (0, 0)))
    sel_bias = rel(past - (past - PAGE_SIZE + np.arange(PAGE_SIZE)))
    win_bias = rel(w_len - np.arange(w_len))
    new_bias = rel(np.zeros(128, np.int64))

    idx_flat = idx[:, :SEL_TOPK].reshape(-1)

    nba = NSA_ATT_BATCH
    assert n % nba == 0

    def page_map(i, idx_s, pt, seq, g, slot):
        b = i * nba + seq
        blk = jnp.minimum(idx_s[(b * A_KV + g) * SEL_TOPK + slot], n_past_blk - 1)
        return (pt[b * n_pages + blk // blocks_per_page], 0, g, 0, 0)

    slot_specs = [pl.BlockSpec((None, 2, None, HEAD_DIM, PAGE_SIZE), functools.partial(page_map, seq=e, g=g, slot=s))
                  for e in range(nba) for g in range(A_KV) for s in range(SEL_TOPK)]
    per_head = lambda rows, last: pl.BlockSpec((nba, A_KV, rows, last), lambda i, *_: (i, 0, 0, 0))
    per_group = lambda last: pl.BlockSpec((A_KV, 8, last), lambda i, *_: (0, 0, 0))
    win_spec = pl.BlockSpec((nba, 2, A_KV, HEAD_DIM, w_len), lambda i, *_: (i, 0, 0, 0, 0))
    o, win_next = pl.pallas_call(
        functools.partial(_nsa_sample_att_body, n_past_blk=n_past_blk),
        grid_spec=pltpu.PrefetchScalarGridSpec(
            num_scalar_prefetch=2,
            grid=(n // nba,),
            in_specs=slot_specs + [
                per_head(A_REP, HEAD_DIM), per_head(A_REP, 3), per_head(A_REP, HEAD_DIM), per_head(2, HEAD_DIM),
                win_spec, per_head(2, HEAD_DIM), per_head(HEAD_DIM, 2),
                per_group(PAGE_SIZE), per_group(w_len), per_group(128),
            ],
            out_specs=[per_head(A_REP, HEAD_DIM), win_spec],
        ),
        out_shape=[jax.ShapeDtypeStruct((n, A_KV, A_REP, HEAD_DIM), F32),
                   jax.ShapeDtypeStruct((n, 2, A_KV, HEAD_DIM, w_len), F32)],
        compiler_params=pltpu.CompilerParams(
            dimension_semantics=("arbitrary",), vmem_limit_bytes=V7X_VMEM_LIMIT_BYTES),
        name="nsa_sample_att",
    )(idx_flat, pt_flat, *([_rows_last(pool_sel)] * (nba * A_KV * SEL_TOPK)), q, gates, o_c,
      jnp.transpose(new_sel, (0, 2, 1, 3)), _rows_last(win_cache), jnp.transpose(new_win, (0, 2, 1, 3)),
      jnp.transpose(new_win, (0, 2, 3, 1)), sel_bias, win_bias, new_bias)
    return o, _rows_first(win_next)


SWA_Q_TILE = WIN_C


def _swa_prompt_body(q_ref, kv_ref, btile_ref, sink_ref, o_ref, *, tq):
    qt = pl.program_id(1)
    q0 = pl.multiple_of(qt * tq, tq)
    prev0 = pl.multiple_of(jnp.maximum(q0 - tq, 0), tq)
    width = C_REP * HEAD_DIM
    for g in range(C_KV):
        q = q_ref[:, g * width:(g + 1) * width]
        qs = jnp.concatenate([q[:, r * HEAD_DIM:(r + 1) * HEAD_DIM] for r in range(C_REP)], axis=0)
        qs = (qs * (HEAD_DIM ** -0.5)).astype(BF16)
        k_ref, v_ref = kv_ref.at[g], kv_ref.at[C_KV + g]
        k = jnp.concatenate([k_ref[pl.ds(prev0, tq), :], k_ref[pl.ds(q0, tq), :]], axis=0)
        v = jnp.concatenate([v_ref[pl.ds(prev0, tq), :], v_ref[pl.ds(q0, tq), :]], axis=0)
        s = _dot_nt(k, qs) + btile_ref[g]
        sinks = sink_ref[g]
        sink = jnp.concatenate([sinks[r:r + 1, :] for r in range(C_REP)], axis=1)
        m = jnp.maximum(jnp.max(s, axis=0, keepdims=True), sink)
        e = jnp.exp(s - m)
        p = e / (jnp.sum(e, axis=0, keepdims=True) + jnp.exp(sink - m))
        o = _dot_tn(v, p.astype(BF16))
        o_ref[:, g * width:(g + 1) * width] = jnp.concatenate([o[:, r * tq:(r + 1) * tq].T for r in range(C_REP)],
                                                              axis=1)


def _swa_prompt(q, kvt, sinks, tab):
    b, t, _ = q.shape
    tq = SWA_Q_TILE
    btile = jnp.swapaxes(_near_bias_tiles(tab, tq, WIN_C), -1, -2)
    qo_spec = pl.BlockSpec((None, tq, C_HEADS * HEAD_DIM), lambda i, j: (i, j, 0))
    sink_lanes = jnp.broadcast_to(sinks.reshape(C_KV, C_REP, 1), (C_KV, C_REP, 128))
    return pl.pallas_call(
        functools.partial(_swa_prompt_body, tq=tq),
        grid=(b, t // tq),
        in_specs=[qo_spec,
                  pl.BlockSpec((None, 2 * C_KV, t, HEAD_DIM), lambda i, j: (i, 0, 0, 0)),
                  pl.BlockSpec((None, C_KV, 2 * tq, C_REP * tq), lambda i, j: (jnp.minimum(j, 1), 0, 0, 0)),
                  pl.BlockSpec((C_KV, C_REP, 128), lambda i, j: (0, 0, 0))],
        out_specs=qo_spec,
        out_shape=jax.ShapeDtypeStruct((b, t, C_HEADS * HEAD_DIM), F32),
        compiler_params=pltpu.CompilerParams(
            dimension_semantics=("parallel", "arbitrary"), vmem_limit_bytes=V7X_VMEM_LIMIT_BYTES),
        name="swa_prompt",
    )(q, kvt, btile, sink_lanes)


SWA_DECODE_BATCH = 4


def _swa_sample_body(q_ref, cache_ref, new_ref, newcol_ref, bias_ref, newb_ref, sink_ref, o_ref, next_ref):
    w_len = cache_ref.shape[-1]
    wcol = lax.broadcasted_iota(jnp.int32, (C_REP, w_len), 1)
    for i in range(q_ref.shape[0]):
        for g in range(C_KV):
            q8 = (q_ref[i, g] * (HEAD_DIM ** -0.5)).astype(BF16)
            s = jnp.dot(q8, cache_ref[i, 0, g].astype(BF16), preferred_element_type=F32) + bias_ref[g]
            s = jnp.where(wcol >= 1, s, NEG_INF)
            k_new = _bf16_round(new_ref[i, g, 0:1])
            v_new = _bf16_round(new_ref[i, g, 1:2])
            s_new = jnp.sum(q8.astype(F32) * k_new, axis=-1, keepdims=True) + newb_ref[g][:, 0:1]
            sink = sink_ref[g][:, 0:1]
            m = jnp.maximum(jnp.maximum(jnp.max(s, axis=-1, keepdims=True), s_new), sink)
            e = jnp.exp(s - m)
            e_new = jnp.exp(s_new - m)
            den = jnp.sum(e, axis=-1, keepdims=True) + e_new + jnp.exp(sink - m)
            o_ref[i, g] = (_dot_nt((e / den).astype(BF16), cache_ref[i, 1, g].astype(BF16))
                           + _bf16_round(e_new / den) * v_new)
            for kv in range(2):
                next_ref[i, kv, g] = _shift_in(cache_ref[i, kv, g], newcol_ref[i, g][:, kv:kv + 1])


def _swa_sample(q, cache, new_kv, sinks, tab):
    n, w_len = cache.shape[:2]
    bias = jnp.transpose(_bias_lookup(tab, w_len - np.arange(w_len)), (1, 2, 0))
    lanes = lambda x: jnp.broadcast_to(x[:, :, None], (C_KV, C_REP, 128))
    full = lambda shape: pl.BlockSpec(shape, lambda i: (0,) * len(shape))
    nb = SWA_DECODE_BATCH
    assert n % nb == 0
    cache_spec = pl.BlockSpec((nb, 2, C_KV, HEAD_DIM, w_len), lambda i: (i, 0, 0, 0, 0))
    o_spec = pl.BlockSpec((nb, C_KV, C_REP, HEAD_DIM), lambda i: (i, 0, 0, 0))
    o, cache_next = pl.pallas_call(
        _swa_sample_body,
        grid=(n // nb,),
        in_specs=[
            o_spec, cache_spec,
            pl.BlockSpec((nb, C_KV, 2, HEAD_DIM), lambda i: (i, 0, 0, 0)),
            pl.BlockSpec((nb, C_KV, HEAD_DIM, 2), lambda i: (i, 0, 0, 0)),
            full((C_KV, C_REP, w_len)), full((C_KV, C_REP, 128)), full((C_KV, C_REP, 128)),
        ],
        out_specs=[o_spec, cache_spec],
        out_shape=[jax.ShapeDtypeStruct((n, C_KV, C_REP, HEAD_DIM), F32),
                   jax.ShapeDtypeStruct((n, 2, C_KV, HEAD_DIM, w_len), F32)],
        compiler_params=pltpu.CompilerParams(
            dimension_semantics=("arbitrary",), vmem_limit_bytes=V7X_VMEM_LIMIT_BYTES),
        name="swa_sample",
    )(q, _rows_last(cache), new_kv, jnp.transpose(new_kv, (0, 1, 3, 2)), bias, lanes(tab[0]),
      lanes(sinks.reshape(C_KV, C_REP)))
    return o, _rows_first(cache_next)


PROJ_ROW_TILE = 512
A_IN_PAD = 1920
A_U_COL = A_Q + 6 * A_KVW
A_GATE_COL = A_U_COL + B_WIDTH


def _heads_first(kvt_ref, slot, k, v, n_kv):
    for g in range(n_kv):
        kvt_ref[slot + g] = k[:, g * HEAD_DIM:(g + 1) * HEAD_DIM].astype(BF16)
        kvt_ref[slot + n_kv + g] = v[:, g * HEAD_DIM:(g + 1) * HEAD_DIM].astype(BF16)


def _inproj_a_body(x_ref, gain_ref, w_ref, qg_ref, kg_ref,
                   q_ref, cmp_ref, sel_ref, win_ref, gate_ref, u_ref, kvt_ref):
    xn = _rms_rows(x_ref[...], gain_ref[...]).astype(BF16)
    z = jnp.dot(xn, w_ref[...], preferred_element_type=F32)
    for j in range(A_Q // 128):
        q_ref[:, j * 128:(j + 1) * 128] = _rms_heads128(z[:, j * 128:(j + 1) * 128], qg_ref[...])
    for out_ref, off, slot in ((cmp_ref, A_Q, None), (sel_ref, A_Q + 2 * A_KVW, 0), (win_ref, A_Q + 4 * A_KVW, 4)):
        k = _rms_heads128(z[:, off:off + A_KVW], kg_ref[...])
        v = z[:, off + A_KVW:off + 2 * A_KVW]
        out_ref[:, :A_KVW] = k
        out_ref[:, A_KVW:] = v
        if slot is not None:
            _heads_first(kvt_ref, slot, k, v, A_KV)
    u_ref[...] = z[:, A_U_COL:A_GATE_COL]
    gate_ref[...] = 1.0 / (1.0 + jnp.exp(-z[:, A_GATE_COL:A_IN_PAD]))


def _inproj_a(x, gain, w_in, q_gain, k_gain):
    b, t, d = x.shape
    tm = min(t, PROJ_ROW_TILE)
    w = jnp.concatenate([w_in[:, :A_U_COL], w_in[:, A_U_COL + A_GATE:], w_in[:, A_U_COL:A_U_COL + A_GATE],
                         jnp.zeros((d, A_IN_PAD - A_GATE_COL - A_GATE), F32)], axis=1).astype(BF16)
    tile2 = lambda g: jnp.tile(g, 2).reshape(1, 128)
    rows = lambda width: pl.BlockSpec((None, tm, width), lambda i, j: (i, j, 0))
    shape = lambda width: jax.ShapeDtypeStruct((b, t, width), F32)
    return pl.pallas_call(
        _inproj_a_body,
        grid=(b, t // tm),
        in_specs=[rows(d), _const_spec((1, d)), _const_spec((d, A_IN_PAD)), _const_spec((1, 128)), _const_spec((1, 128))],
        out_specs=[rows(A_Q), rows(2 * A_KVW), rows(2 * A_KVW), rows(2 * A_KVW), rows(128), rows(B_WIDTH),
                   pl.BlockSpec((None, 8, tm, HEAD_DIM), lambda i, j: (i, 0, j, 0))],
        out_shape=[shape(A_Q), shape(2 * A_KVW), shape(2 * A_KVW), shape(2 * A_KVW), shape(128), shape(B_WIDTH),
                   jax.ShapeDtypeStruct((b, 8, t, HEAD_DIM), BF16)],
        compiler_params=pltpu.CompilerParams(
            dimension_semantics=("parallel", "parallel"), vmem_limit_bytes=V7X_VMEM_LIMIT_BYTES),
        name="inproj_nsa_s5",
    )(x, gain.reshape(1, d), w, tile2(q_gain), tile2(k_gain))


def _inproj_c_body(x_ref, gain_ref, w_ref, qg_ref, kg_ref, q_ref, kv_ref, kvt_ref):
    xn = _rms_rows(x_ref[...], gain_ref[...]).astype(BF16)
    z = jnp.dot(xn, w_ref[...], preferred_element_type=F32)
    n_q = C_HEADS * HEAD_DIM
    for j in range(n_q // 128):
        q_ref[:, j * 128:(j + 1) * 128] = _rms_heads128(z[:, j * 128:(j + 1) * 128], qg_ref[...])
    k = _rms_heads128(z[:, n_q:n_q + C_KV * HEAD_DIM], kg_ref[...])
    v = z[:, n_q + C_KV * HEAD_DIM:]
    kv_ref[:, :C_KV * HEAD_DIM] = k
    kv_ref[:, C_KV * HEAD_DIM:] = v
    _heads_first(kvt_ref, 0, k, v, C_KV)


def _inproj_c(x, gain, w_in, q_gain, k_gain):
    b, t, d = x.shape
    tm = min(t, PROJ_ROW_TILE)
    n_in = w_in.shape[1]
    tile2 = lambda g: jnp.tile(g, 2).reshape(1, 128)
    rows = lambda width: pl.BlockSpec((None, tm, width), lambda i, j: (i, j, 0))
    shape = lambda width: jax.ShapeDtypeStruct((b, t, width), F32)
    return pl.pallas_call(
        _inproj_c_body,
        grid=(b, t // tm),
        in_specs=[rows(d), _const_spec((1, d)), _const_spec((d, n_in)), _const_spec((1, 128)), _const_spec((1, 128))],
        out_specs=[rows(C_HEADS * HEAD_DIM), rows(2 * C_KV * HEAD_DIM),
                   pl.BlockSpec((None, 2 * C_KV, tm, HEAD_DIM), lambda i, j: (i, 0, j, 0))],
        out_shape=[shape(C_HEADS * HEAD_DIM), shape(2 * C_KV * HEAD_DIM),
                   jax.ShapeDtypeStruct((b, 2 * C_KV, t, HEAD_DIM), BF16)],
        compiler_params=pltpu.CompilerParams(
            dimension_semantics=("parallel", "parallel"), vmem_limit_bytes=V7X_VMEM_LIMIT_BYTES),
        name="inproj_swa",
    )(x, gain.reshape(1, d), w_in.astype(BF16), tile2(q_gain), tile2(k_gain))


FFN_ROW_TILE = 512
FFN_COL_CHUNK = 2816


def _mixer_residual(y_ref, x_ref, mix_refs, wout_refs):
    y_ref[...] = x_ref[...]
    for m_ref, w_ref in zip(mix_refs, wout_refs):
        y_ref[...] += jnp.dot(m_ref[...].astype(BF16), w_ref[...], preferred_element_type=F32)
    return y_ref[...]


def _tail_prompt_body(*refs, n_mix, tm, ffc):
    x_ref = refs[0]
    mix_refs = refs[1:1 + n_mix]
    wout_refs = refs[1 + n_mix:1 + 2 * n_mix]
    (gain_ref, prev_ref, wup_ref, wgate_ref, cw_ref, cb_ref, wdown_ref, y_ref, cs_ref, hbuf_ref) = refs[1 + 2 * n_mix:]
    t = pl.program_id(1)
    xn = _rms_rows(_mixer_residual(y_ref, x_ref, mix_refs, wout_refs), gain_ref[...]).astype(BF16)
    for c in range(D_FF // ffc):
        lo = c * ffc
        h = jnp.dot(xn, wup_ref[:, lo:lo + ffc], preferred_element_type=F32)
        g = jnp.dot(xn, wgate_ref[:, lo:lo + ffc], preferred_element_type=F32)

        @pl.when(t == 0)
        def _():
            hbuf_ref[c, 6:8, :] = prev_ref[:, lo:lo + ffc]

        hbuf_ref[c, 8:8 + tm, :] = h
        hm1 = hbuf_ref[c, 7:7 + tm, :]
        hm2 = hbuf_ref[c, 6:6 + tm, :]
        cw = cw_ref[:, lo:lo + ffc]
        hc = cw[0:1] * hm2 + cw[1:2] * hm1 + cw[2:3] * h + cb_ref[:, lo:lo + ffc]
        a = (_gelu_tanh(hc) * g).astype(BF16)
        y_ref[...] += jnp.dot(a, wdown_ref[lo:lo + ffc, :], preferred_element_type=F32)
        hbuf_ref[c, 0:8, :] = h[tm - 8:tm, :]
        cs_ref[:, lo:lo + ffc] = h[tm - 2:tm, :]


def _tail_prompt(x, mixes, wouts, gain, prev, wup, wgate, cw, cb, wdown):
    b, t, d = x.shape
    tm, ffc = FFN_ROW_TILE, FFN_COL_CHUNK
    rows = lambda width: pl.BlockSpec((None, tm, width), lambda i, j: (i, j, 0))
    state = pl.BlockSpec((None, CONV_W - 1, D_FF), lambda i, j: (i, 0, 0))
    return pl.pallas_call(
        functools.partial(_tail_prompt_body, n_mix=len(mixes), tm=tm, ffc=ffc),
        grid=(b, t // tm),
        in_specs=[rows(d)] + [rows(m.shape[-1]) for m in mixes] + [_const_spec(w.shape) for w in wouts] + [
            _const_spec((1, d)), state, _const_spec((d, D_FF)), _const_spec((d, D_FF)),
            _const_spec((CONV_W, D_FF)), _const_spec((1, D_FF)), _const_spec((D_FF, d))],
        out_specs=[rows(d), state],
        out_shape=[jax.ShapeDtypeStruct((b, t, d), F32), jax.ShapeDtypeStruct((b, CONV_W - 1, D_FF), F32)],
        scratch_shapes=[pltpu.VMEM((D_FF // ffc, 8 + tm, ffc), F32)],
        compiler_params=pltpu.CompilerParams(
            dimension_semantics=("parallel", "arbitrary"), vmem_limit_bytes=V7X_VMEM_LIMIT_BYTES),
        name="tail_prompt",
    )(x, *mixes, *wouts, gain, prev, wup, wgate, cw, cb, wdown)


def _tail_sample_body(*refs, n_mix, ffc):
    x_ref = refs[0]
    mix_refs = refs[1:1 + n_mix]
    wout_refs = refs[1 + n_mix:1 + 2 * n_mix]
    (gain_ref, prev_ref, wup_ref, wgate_ref, cw_ref, cb_ref, wdown_ref, y_ref, cs_ref) = refs[1 + 2 * n_mix:]
    xn = _rms_rows(_mixer_residual(y_ref, x_ref, mix_refs, wout_refs), gain_ref[...]).astype(BF16)
    for c in range(D_FF // ffc):
        lo = c * ffc
        h = jnp.dot(xn, wup_ref[:, lo:lo + ffc], preferred_element_type=F32)
        g = jnp.dot(xn, wgate_ref[:, lo:lo + ffc], preferred_element_type=F32)
        hm2 = prev_ref[:, lo:lo + ffc]
        hm1 = prev_ref[:, D_FF + lo:D_FF + lo + ffc]
        cw = cw_ref[:, lo:lo + ffc]
        hc = cw[0:1] * hm2 + cw[1:2] * hm1 + cw[2:3] * h + cb_ref[:, lo:lo + ffc]
        a = (_gelu_tanh(hc) * g).astype(BF16)
        y_ref[...] += jnp.dot(a, wdown_ref[lo:lo + ffc, :], preferred_element_type=F32)
        cs_ref[:, lo:lo + ffc] = hm1
        cs_ref[:, D_FF + lo:D_FF + lo + ffc] = h


def _tail_sample(x, mixes, wouts, gain, prev, wup, wgate, cw, cb, wdown):
    n, d = x.shape
    full = lambda shape: _const_spec(shape)
    return pl.pallas_call(
        functools.partial(_tail_sample_body, n_mix=len(mixes), ffc=FFN_COL_CHUNK),
        grid=(1,),
        in_specs=[full((n, d))] + [full(m.shape) for m in mixes] + [full(w.shape) for w in wouts] + [
            full((1, d)), full((n, (CONV_W - 1) * D_FF)), full((d, D_FF)), full((d, D_FF)),
            full((CONV_W, D_FF)), full((1, D_FF)), full((D_FF, d))],
        out_specs=[pl.BlockSpec((n, d), lambda i: (0, 0)), pl.BlockSpec((n, (CONV_W - 1) * D_FF), lambda i: (0, 0))],
        out_shape=[jax.ShapeDtypeStruct((n, d), F32), jax.ShapeDtypeStruct((n, (CONV_W - 1) * D_FF), F32)],
        compiler_params=pltpu.CompilerParams(
            dimension_semantics=("arbitrary",), vmem_limit_bytes=V7X_VMEM_LIMIT_BYTES),
        name="tail_sample",
    )(x, *mixes, *wouts, gain, prev, wup, wgate, cw, cb, wdown)


def kernel(x_prompt, x_sample, cache_nsa_cmp, cache_nsa_sel, cache_nsa_win, state_s5_re, state_s5_im,
           cache_swa, state_ffn_conv, page_table, rel_bias, norm_mix, norm_ffn, a_w_in, a_w_out,
           nsa_q_gain, nsa_k_gain, nsa_cmp_pos, nsa_cmp_w1, nsa_cmp_w2, s5_a_re, s5_a_im, s5_log_dt,
           s5_b_re, s5_b_im, s5_c_re, s5_c_im, s5_d, s5_w_glu, s5_b_glu, c_w_in, c_w_out, c_q_gain,
           c_k_gain, c_sinks, ffn_w_up, ffn_w_gate, ffn_conv_w, ffn_conv_b, ffn_w_down):
    bp, tp, _ = x_prompt.shape
    bs, ts, _ = x_sample.shape
    assert ts == 1 and DEPTH == 2
    tab_a = rel_bias[:, :A_HEADS].reshape(NUM_BUCKETS, A_KV, A_REP)
    tab_c = rel_bias[:, :C_HEADS].reshape(NUM_BUCKETS, C_KV, C_REP)
    kv6 = lambda x: x.reshape(x.shape[:-1] + (2, x.shape[-1] // (2 * HEAD_DIM), HEAD_DIM))
    hp, hs = x_prompt, x_sample.reshape(1, bs, D_MODEL)
    conv_p, conv_s = [], []

    def tail(layer, hp, hs, mixes_p, mixes_s, wouts):
        wouts = [w.astype(BF16) for w in wouts]
        ffn = (norm_ffn[layer].reshape(1, D_MODEL),)
        wts = (ffn_w_up[layer].astype(BF16), ffn_w_gate[layer].astype(BF16), ffn_conv_w[layer],
               ffn_conv_b[layer].reshape(1, D_FF), ffn_w_down[layer].astype(BF16))
        hp, cp = _tail_prompt(hp, mixes_p, wouts, *ffn, jnp.zeros((bp, CONV_W - 1, D_FF), F32), *wts)
        hs2, cs = _tail_sample(hs[0], [m.reshape(bs, -1) for m in mixes_s], wouts, *ffn,
                               state_ffn_conv[layer].reshape(bs, (CONV_W - 1) * D_FF), *wts)
        conv_p.append(cp)
        conv_s.append(cs.reshape(bs, CONV_W - 1, D_FF))
        return hp, hs2.reshape(1, bs, D_MODEL)

    proj = (norm_mix[0], a_w_in[0], nsa_q_gain[0], nsa_k_gain[0])
    qp, cmp_p, sel_p, win_p, gate_p, up, kvt_p = _inproj_a(hp, *proj)
    qs, cmp_s, sel_s, win_s, gate_s, us, _ = _inproj_a(hs, *proj)
    cmp_w = _compress_weights(nsa_cmp_pos[0], nsa_cmp_w1[0], nsa_cmp_w2[0], nsa_k_gain[0])
    kc, vc = _cmp_prompt(cmp_p, cmp_w)
    o_ap = _nsa_prompt_attend(qp, gate_p, kc, vc, kvt_p, tab_a)
    o_as, nsa_win_s = _nsa_sample(qs.reshape(bs, A_KV, A_REP, HEAD_DIM),
                                  gate_s[0, :, :A_GATE].reshape(bs, A_KV, A_REP, 3),
                                  sel_s.reshape(bs, 2, A_KV, HEAD_DIM), win_s.reshape(bs, 2, A_KV, HEAD_DIM),
                                  cache_nsa_cmp[0], cache_nsa_sel[0], cache_nsa_win[0], page_table, cmp_w, tab_a)
    s5w = _s5_weights(s5_a_re[0], s5_a_im[0], s5_log_dt[0], s5_b_re[0], s5_b_im[0], s5_c_re[0], s5_c_im[0],
                      s5_d[0], s5_w_glu[0], s5_b_glu[0])
    h0 = jnp.zeros((bp, S5_NS), F32)
    o_bp, hr_p, hi_p = _s5_mix(up, h0, h0, s5w)
    o_bs, hr_s, hi_s = _s5_mix(us.reshape(bs, 1, B_WIDTH), state_s5_re[0].reshape(bs, S5_NS),
                               state_s5_im[0].reshape(bs, S5_NS), s5w)
    hp, hs = tail(0, hp, hs, [o_ap, o_bp], [o_as, o_bs], [a_w_out[0][:A_Q], a_w_out[0][A_Q:]])

    proj = (norm_mix[1], c_w_in[0], c_q_gain[0], c_k_gain[0])
    qcp, kv_p, kvt_c = _inproj_c(hp, *proj)
    qcs, kv_s, _ = _inproj_c(hs, *proj)
    o_cp = _swa_prompt(qcp, kvt_c, c_sinks[0], tab_c)
    new_kv = kv_s.reshape(bs, 2, C_KV, HEAD_DIM)
    o_cs, swa_s = _swa_sample(qcs.reshape(bs, C_KV, C_REP, HEAD_DIM), cache_swa[0],
                              jnp.transpose(new_kv, (0, 2, 1, 3)), c_sinks[0], tab_c)
    hp, hs = tail(1, hp, hs, [o_cp], [o_cs], [c_w_out[0]])

    state = lambda x, n: x.reshape(1, n, B_GROUPS, B_STATE)
    return (hp, hs.reshape(bs, ts, D_MODEL),
            kv6(cmp_p)[None], kv6(cmp_s).reshape(1, bs, ts, 2, A_KV, HEAD_DIM),
            kv6(sel_p)[None], kv6(sel_s).reshape(1, bs, ts, 2, A_KV, HEAD_DIM),
            kv6(win_p)[None, :, -min(WIN_A, tp):], nsa_win_s[None],
            state(hr_p, bp), state(hi_p, bp), state(hr_s, bs), state(hi_s, bs),
            kv6(kv_p)[None, :, -min(WIN_C, tp):], swa_s[None],
            jnp.stack(conv_p), jnp.stack(conv_s))
```

```python
import functools
import math

import jax
import jax.numpy as jnp
import numpy as np
from jax import lax
from jax.experimental import pallas as pl
from jax.experimental.pallas import tpu as pltpu

D_MODEL = 1024
DEPTH = 2
PAGE_SIZE = 128
HEAD_DIM = 64
A_HEADS = 8
A_KV = 2
A_REP = A_HEADS // A_KV
A_Q = A_HEADS * HEAD_DIM
A_KVW = A_KV * HEAD_DIM
A_GATE = 3 * A_HEADS
CMP_LEN = 32
CMP_STRIDE = 16
SEL_BLOCK = 64
SEL_TOPK = 16
WIN_A = 512
FORCE_BONUS = 1000.0
B_WIDTH = D_MODEL // 2
B_GROUP = 16
B_GROUPS = B_WIDTH // B_GROUP
B_STATE = 64
C_HEADS = D_MODEL // HEAD_DIM
C_KV = 2
C_REP = C_HEADS // C_KV
WIN_C = 128
NUM_BUCKETS = 32
MAX_DISTANCE = 128
D_FF = 2816
CONV_W = 3
EPS = 1e-6

F32 = jnp.float32
BF16 = jnp.bfloat16

V7X_VMEM_LIMIT_BYTES = 56 * 1024 * 1024


def _gelu_tanh(x):
    c = math.sqrt(2.0 / math.pi)
    return 0.5 * x * (1.0 + jnp.tanh(x * (c + (c * 0.044715) * (x * x))))


def _rms_rows(x, gain):
    return x * lax.rsqrt(jnp.mean(x * x, axis=-1, keepdims=True) + EPS) * gain


def _const_spec(shape):
    zeros = (0,) * len(shape)
    return pl.BlockSpec(shape, lambda *_: zeros, pipeline_mode=pl.Buffered(1))


NSA_Q_TILE = 256
NSA_FAR_TILE = 512
NSA_HEADS_PER_PASS = 4
NEG_INF = float("-inf")


def _dot_nt(a, b):
    return lax.dot_general(a, b, (((1,), (1,)), ((), ())), preferred_element_type=F32)


def _dot_tn(a, b):
    return lax.dot_general(a, b, (((0,), (0,)), ((), ())), preferred_element_type=F32)


def _softmax_start(s, v):
    m = jnp.max(s, axis=0, keepdims=True)
    e = jnp.exp(s - m)
    return m, jnp.sum(e, axis=0, keepdims=True), _dot_tn(v, e.astype(BF16))


def _softmax_more(carry, s, v):
    m, l, acc = carry
    m_new = jnp.maximum(m, jnp.max(s, axis=0, keepdims=True))
    alpha = jnp.exp(m - m_new)
    e = jnp.exp(s - m_new)
    return m_new, alpha * l + jnp.sum(e, axis=0, keepdims=True), alpha * acc + _dot_tn(v, e.astype(BF16))


def _nsa_prompt_body(q_ref, gate_ref, kc_ref, vc_ref, kvt_ref, btile_ref, cover_ref, o_ref, sel_scr, *, tq):
    width = A_REP * HEAD_DIM
    gates_t = gate_ref[...].T
    for g in range(A_KV):
        _nsa_prompt_group(g, q_ref.at[:, g * width:(g + 1) * width], gates_t, kc_ref.at[g], vc_ref.at[g],
                          kvt_ref.at[g], kvt_ref.at[A_KV + g], kvt_ref.at[2 * A_KV + g], kvt_ref.at[3 * A_KV + g],
                          btile_ref.at[g], cover_ref, o_ref.at[:, g * width:(g + 1) * width], sel_scr.at[g], tq)


def _nsa_prompt_group(g, q_ref, gates_t, kc_ref, vc_ref, ks_ref, vs_ref, kw_ref, vw_ref, btile_ref, cover_ref,
                      o_ref, sel_ref, tq):
    qt = pl.program_id(1)
    q0 = pl.multiple_of(qt * tq, tq)
    cols = A_REP * tq
    q = q_ref[...]
    qs = jnp.concatenate([q[:, r * HEAD_DIM:(r + 1) * HEAD_DIM] for r in range(A_REP)], axis=0)
    qs = (qs * (HEAD_DIM ** -0.5)).astype(BF16)

    def q_pos(height):
        return q0 + (lax.broadcasted_iota(jnp.int32, (height, cols), 1) & (tq - 1))

    def key_idx(height):
        return lax.broadcasted_iota(jnp.int32, (height, cols), 0)

    n_idx = key_idx(128)
    valid_c = (n_idx * CMP_STRIDE + (CMP_LEN - 1) <= q_pos(128)) & (n_idx < 127)
    s_c = jnp.where(valid_c, _dot_nt(kc_ref[...], qs), NEG_INF)
    m_c = jnp.max(s_c, axis=0, keepdims=True)
    m_c = jnp.where(m_c == NEG_INF, 0.0, m_c)
    e_c = jnp.exp(s_c - m_c)
    d_c = jnp.sum(e_c, axis=0, keepdims=True)
    p_c = (e_c / jnp.where(d_c > 0, d_c, 1.0)).astype(BF16)
    o_c = _dot_tn(vc_ref[...], p_c)
    imp_heads = jnp.dot(cover_ref[...], p_c, preferred_element_type=F32)
    imp = sum(imp_heads[0:32, r * tq:(r + 1) * tq] for r in range(A_REP))
    s_idx = lax.broadcasted_iota(jnp.int32, (32, tq), 0)
    qblk = (q0 + lax.broadcasted_iota(jnp.int32, (32, tq), 1)) >> 6
    forced = (s_idx == 0) | (s_idx == qblk) | (s_idx == qblk - 1)
    allowed = s_idx <= qblk
    score = jnp.where(allowed, imp + jnp.where(forced, FORCE_BONUS, 0.0), NEG_INF)
    rank = jnp.zeros((32, tq), F32)
    for j in range(32):
        other = score[j:j + 1, :]
        beats = (other > score) | ((other == score) & (s_idx > j))
        rank = rank + jnp.where(beats, 1.0, 0.0)
    sel = jnp.where((rank < SEL_TOPK) & allowed, 1.0, 0.0)
    hcols = NSA_HEADS_PER_PASS * tq
    far_end = q0 - tq
    sel_far_only = jnp.where(s_idx < far_end // SEL_BLOCK, sel, 0.0)
    sel_ref[...] = jnp.concatenate([jnp.concatenate([sel] * NSA_HEADS_PER_PASS, axis=1),
                                    jnp.concatenate([sel_far_only] * NSA_HEADS_PER_PASS, axis=1)], axis=0)

    def block_mask(k0, n_blocks, base=0):
        first = base + k0 // SEL_BLOCK
        return jnp.concatenate([jnp.broadcast_to(sel_ref[pl.ds(first + j, 1), :], (SEL_BLOCK, hcols))
                                for j in range(n_blocks)], axis=0) > 0.5

    def hq_pos(height):
        return q0 + (lax.broadcasted_iota(jnp.int32, (height, hcols), 1) & (tq - 1))

    def hkey_idx(height):
        return lax.broadcasted_iota(jnp.int32, (height, hcols), 0)

    prev0 = pl.multiple_of(jnp.maximum(q0 - tq, 0), tq)
    sel_near = jnp.concatenate([block_mask(prev0, tq // SEL_BLOCK), block_mask(q0, tq // SEL_BLOCK)], axis=0)
    n_far = (jnp.maximum(far_end, 0) + NSA_FAR_TILE - 1) // NSA_FAR_TILE
    w_far = WIN_A - tq
    wf0 = pl.multiple_of(jnp.maximum(q0 - WIN_A, 0), tq)
    wpos = wf0 + hkey_idx(w_far)
    wmask = (hq_pos(w_far) - wpos < WIN_A) & (wpos < far_end)

    outs = []
    for h0 in range(0, A_REP, NSA_HEADS_PER_PASS):
        csl = slice(h0 * tq, h0 * tq + hcols)
        qh = qs[h0 * tq:h0 * tq + hcols]
        btile = btile_ref[:, csl]

        def near(k_ref, v_ref, extra_mask):
            k = jnp.concatenate([k_ref[pl.ds(prev0, tq), :], k_ref[pl.ds(q0, tq), :]], axis=0)
            v = jnp.concatenate([v_ref[pl.ds(prev0, tq), :], v_ref[pl.ds(q0, tq), :]], axis=0)
            s = _dot_nt(k, qh) + btile
            return _softmax_start(s if extra_mask is None else jnp.where(extra_mask, s, NEG_INF), v)

        def sel_far(i, carry):
            k0 = pl.multiple_of(i * NSA_FAR_TILE, NSA_FAR_TILE)
            s = _dot_nt(ks_ref[pl.ds(k0, NSA_FAR_TILE), :], qh)
            mask = block_mask(k0, NSA_FAR_TILE // SEL_BLOCK, base=32)
            return _softmax_more(carry, jnp.where(mask, s, NEG_INF), vs_ref[pl.ds(k0, NSA_FAR_TILE), :])

        _, l_s, acc_s = lax.fori_loop(0, n_far, sel_far, near(ks_ref, vs_ref, sel_near))
        o_s = acc_s / l_s

        s_w = jnp.where(wmask, _dot_nt(kw_ref[pl.ds(wf0, w_far), :], qh), NEG_INF)
        _, l_w, acc_w = _softmax_more(near(kw_ref, vw_ref, None), s_w, vw_ref[pl.ds(wf0, w_far), :])
        o_w = acc_w / l_w

        for i in range(NSA_HEADS_PER_PASS):
            r = h0 + i
            sl = slice(i * tq, (i + 1) * tq)
            row = 3 * (g * A_REP + r)
            out_t = (gates_t[row:row + 1] * o_c[:, r * tq:(r + 1) * tq] + gates_t[row + 1:row + 2] * o_s[:, sl]
                     + gates_t[row + 2:row + 3] * o_w[:, sl])
            outs.append(out_t.T)
    o_ref[...] = jnp.concatenate(outs, axis=1)


def _bucket_np(dist):
    n = np.maximum(dist, 0)
    exact = NUM_BUCKETS // 2
    nf = np.maximum(n, exact).astype(np.float64)
    large = exact + (np.log(nf / exact) / math.log(MAX_DISTANCE / exact) * (NUM_BUCKETS - exact)).astype(np.int64)
    return np.where(n < exact, n, np.minimum(large, NUM_BUCKETS - 1)).astype(np.int32)


def _bias_lookup(tab, dist):
    bucket = _bucket_np(np.asarray(dist))
    onehot = (jnp.asarray(bucket.reshape(-1, 1)) == jnp.arange(NUM_BUCKETS)[None, :]).astype(F32)
    flat = jnp.dot(onehot, tab.reshape(NUM_BUCKETS, -1), precision=lax.Precision.HIGHEST)
    return flat.reshape(bucket.shape + tab.shape[1:])


def _near_bias_tiles(tab, tq, window):
    i = np.arange(tq)[:, None]
    j = np.arange(2 * tq)[None, :]
    dist = tq + i - j
    ok = (dist >= 0) & (dist < window)
    bias = jnp.transpose(_bias_lookup(tab, dist), (2, 3, 0, 1))
    tiles = jnp.stack([jnp.where(jnp.asarray(ok & (j >= tq)), bias, NEG_INF), jnp.where(jnp.asarray(ok), bias, NEG_INF)])
    return tiles.reshape(2, tab.shape[1], tab.shape[2] * tq, 2 * tq)


def _cover_matrix(n_cmp_pad, n_sel):
    n = np.arange(n_cmp_pad)
    c_start = n * CMP_STRIDE
    c_end = c_start + CMP_LEN - 1
    s_start = np.arange(128) * SEL_BLOCK
    cover = (c_start[:, None] < s_start[None, :] + SEL_BLOCK) & (c_end[:, None] >= s_start[None, :])
    cover &= (np.arange(128)[None, :] < n_sel)
    return np.tile(cover.astype(np.float32), (A_REP, 1))


def _nsa_prompt_attend(q, gates, kc, vc, kvt, tab):
    b, t, _ = q.shape
    tq = NSA_Q_TILE
    assert t % NSA_FAR_TILE == 0 and t // SEL_BLOCK <= 32 and t >= WIN_A and kc.shape[2] == 128
    btile = jnp.swapaxes(_near_bias_tiles(tab - tab[NUM_BUCKETS - 1], tq, 2 * tq), -1, -2)
    cover = jnp.asarray(_cover_matrix(128, t // SEL_BLOCK)[:128].T, BF16)
    cmp_spec = pl.BlockSpec((None, A_KV, 128, HEAD_DIM), lambda i, j: (i, 0, 0, 0))
    return pl.pallas_call(
        functools.partial(_nsa_prompt_body, tq=tq),
        grid=(b, t // tq),
        in_specs=[
            pl.BlockSpec((None, tq, A_Q), lambda i, j: (i, j, 0)),
            pl.BlockSpec((None, tq, 128), lambda i, j: (i, j, 0)),
            cmp_spec, cmp_spec,
            pl.BlockSpec((None, 4 * A_KV, t, HEAD_DIM), lambda i, j: (i, 0, 0, 0)),
            pl.BlockSpec((None, A_KV, 2 * tq, A_REP * tq), lambda i, j: (jnp.minimum(j, 1), 0, 0, 0)),
            pl.BlockSpec((128, 128), lambda i, j: (0, 0)),
        ],
        out_specs=pl.BlockSpec((None, tq, A_Q), lambda i, j: (i, j, 0)),
        out_shape=jax.ShapeDtypeStruct((b, t, A_Q), F32),
        scratch_shapes=[pltpu.VMEM((A_KV, 2 * 32, NSA_HEADS_PER_PASS * tq), F32)],
        compiler_params=pltpu.CompilerParams(
            dimension_semantics=("parallel", "arbitrary"), vmem_limit_bytes=V7X_VMEM_LIMIT_BYTES),
        name="nsa_prompt",
    )(q, gates, kc, vc, kvt, btile, cover)


S5_NS = B_GROUPS * B_STATE
S5_T_CHUNK = 64
S5_STRIP = 512
S5_HALVES = 2


def _s5_body(u_ref, h0r_ref, h0i_ref, ar_ref, ai_ref, ldt_ref, wb_ref, wc_ref, d_ref, wglu_ref, bglu_ref,
             o_ref, hr_ref, hi_ref, coef_ref, st_ref, xbuf_ref, ubuf_ref, obuf_ref, *, nb, steps, interleave):
    c = pl.program_id(0)

    @pl.when(c == 0)
    def _():
        dt = jnp.exp(ldt_ref[...])
        ar, ai = ar_ref[...], ai_ref[...]
        mag = jnp.exp(ar * dt)
        abr, abi = mag * jnp.cos(ai * dt), mag * jnp.sin(ai * dt)
        den = ar * ar + ai * ai
        wr = ((abr - 1.0) * ar + abi * ai) / den
        wi = (abi * ar - (abr - 1.0) * ai) / den
        for k, val in enumerate((abr, abi, wr, wi)):
            coef_ref[k] = jnp.broadcast_to(val, (nb, S5_NS))
        st_ref[0] = h0r_ref[...]
        st_ref[1] = h0i_ref[...]

    if interleave:
        for b in range(nb):
            for j in range(B_WIDTH // 128):
                ubuf_ref.at[j][pl.ds(b, steps, stride=nb), :] = u_ref[b, :, j * 128:(j + 1) * 128]
        u = jnp.concatenate([ubuf_ref[j] for j in range(B_WIDTH // 128)], axis=1)
    else:
        u = u_ref[...]
    hc, hs = B_WIDTH // S5_HALVES, S5_NS // S5_HALVES
    ub = u.astype(BF16)
    for h in range(S5_HALVES):
        bu = jnp.dot(ub[:, h * hc:(h + 1) * hc], wb_ref[h], preferred_element_type=F32)
        xbuf_ref[:, h * hs:(h + 1) * hs] = bu[:, :hs]
        xbuf_ref[:, S5_NS + h * hs:S5_NS + (h + 1) * hs] = bu[:, hs:]

    for lo in range(0, S5_NS, S5_STRIP):
        re = slice(lo, lo + S5_STRIP)
        im = slice(S5_NS + lo, S5_NS + lo + S5_STRIP)
        abr, abi, wr, wi = (coef_ref[k, :, re] for k in range(4))

        def step(t, carry):
            sr, si = carry
            r0 = pl.multiple_of(t * nb, nb)
            bur = xbuf_ref[pl.ds(r0, nb), re]
            bui = xbuf_ref[pl.ds(r0, nb), im]
            nsr = abr * sr - abi * si + (wr * bur - wi * bui)
            nsi = abr * si + abi * sr + (wr * bui + wi * bur)
            xbuf_ref[pl.ds(r0, nb), re] = nsr
            xbuf_ref[pl.ds(r0, nb), im] = nsi
            return nsr, nsi

        sr, si = lax.fori_loop(0, steps, step, (st_ref[0, :, re], st_ref[1, :, re]),
                               unroll=min(steps, 8))
        st_ref[0, :, re] = sr
        st_ref[1, :, re] = si

    y = jnp.concatenate(
        [jnp.dot(jnp.concatenate([xbuf_ref[:, h * hs:(h + 1) * hs], xbuf_ref[:, S5_NS + h * hs:S5_NS + (h + 1) * hs]],
                                 axis=1).astype(BF16), wc_ref[h], preferred_element_type=F32)
         for h in range(S5_HALVES)], axis=1) + d_ref[...] * u
    z = _gelu_tanh(y)
    gate = jnp.dot(z.astype(BF16), wglu_ref[...], preferred_element_type=F32) + bglu_ref[...]
    out = z * (1.0 / (1.0 + jnp.exp(-gate)))
    if interleave:
        for j in range(B_WIDTH // 128):
            obuf_ref[j] = out[:, j * 128:(j + 1) * 128]
        for b in range(nb):
            for j in range(B_WIDTH // 128):
                o_ref[b, :, j * 128:(j + 1) * 128] = obuf_ref.at[j][pl.ds(b, steps, stride=nb), :]
    else:
        o_ref[...] = out
    hr_ref[...] = st_ref[0]
    hi_ref[...] = st_ref[1]


def _s5_weights(a_re, a_im, log_dt, b_re, b_im, c_re, c_im, d, w_glu, b_glu):
    eye = jnp.eye(B_GROUPS, dtype=F32)
    blk_in = lambda w: jnp.einsum('hg,gpc->hcgp', eye, w).reshape(B_WIDTH, S5_NS)
    blk_out = lambda w: jnp.einsum('gh,gcp->gphc', eye, w).reshape(S5_NS, B_WIDTH)
    hc, hs = B_WIDTH // S5_HALVES, S5_NS // S5_HALVES
    diag = lambda w, h: w[h * hc:(h + 1) * hc, h * hs:(h + 1) * hs]
    wb = jnp.stack([jnp.concatenate([diag(blk_in(b_re), h), diag(blk_in(b_im), h)], axis=1)
                    for h in range(S5_HALVES)]).astype(BF16)
    diag_t = lambda w, h: w[h * hs:(h + 1) * hs, h * hc:(h + 1) * hc]
    wc = jnp.stack([jnp.concatenate([diag_t(blk_out(c_re), h), -diag_t(blk_out(c_im), h)], axis=0)
                    for h in range(S5_HALVES)]).astype(BF16)
    flat = lambda x: x.reshape(1, S5_NS)
    return (flat(a_re), flat(a_im), flat(jnp.repeat(log_dt, B_STATE)), wb, wc, d.reshape(1, B_WIDTH),
            w_glu.astype(BF16), b_glu.reshape(1, B_WIDTH))


def _s5_mix(u, h_re, h_im, weights):
    nb, t, _ = u.shape
    interleave = t > 1
    steps = min(t, S5_T_CHUNK)
    rows = nb * steps
    body = functools.partial(_s5_body, nb=nb, steps=steps, interleave=interleave)
    if interleave:
        u_in = u
        u_spec = pl.BlockSpec((nb, steps, B_WIDTH), lambda c: (0, c, 0))
        o_shape = jax.ShapeDtypeStruct((nb, t, B_WIDTH), F32)
        scratch_rows = rows
    else:
        u_in = u.reshape(nb, B_WIDTH)
        u_spec = pl.BlockSpec((nb, B_WIDTH), lambda c: (0, 0))
        o_shape = jax.ShapeDtypeStruct((nb, B_WIDTH), F32)
        scratch_rows = 8
    o, hr, hi = pl.pallas_call(
        body,
        grid=(t // steps,),
        in_specs=[
            u_spec, _const_spec((nb, S5_NS)), _const_spec((nb, S5_NS)),
            _const_spec((1, S5_NS)), _const_spec((1, S5_NS)), _const_spec((1, S5_NS)),
            _const_spec((S5_HALVES, B_WIDTH // S5_HALVES, 2 * S5_NS // S5_HALVES)),
            _const_spec((S5_HALVES, 2 * S5_NS // S5_HALVES, B_WIDTH // S5_HALVES)),
            _const_spec((1, B_WIDTH)), _const_spec((B_WIDTH, B_WIDTH)), _const_spec((1, B_WIDTH)),
        ],
        out_specs=[u_spec, pl.BlockSpec((nb, S5_NS), lambda c: (0, 0)), pl.BlockSpec((nb, S5_NS), lambda c: (0, 0))],
        out_shape=[o_shape, jax.ShapeDtypeStruct((nb, S5_NS), F32), jax.ShapeDtypeStruct((nb, S5_NS), F32)],
        scratch_shapes=[
            pltpu.VMEM((4, nb, S5_NS), F32),
            pltpu.VMEM((2, nb, S5_NS), F32),
            pltpu.VMEM((rows, 2 * S5_NS), F32),
            pltpu.VMEM((B_WIDTH // 128, scratch_rows, 128), F32),
            pltpu.VMEM((B_WIDTH // 128, scratch_rows, 128), F32),
        ],
        compiler_params=pltpu.CompilerParams(
            dimension_semantics=("arbitrary",),
            vmem_limit_bytes=V7X_VMEM_LIMIT_BYTES),
        name="s5_mix",
    )(u_in, h_re, h_im, *weights)
    return o.reshape(nb, t, B_WIDTH), hr, hi


CHUNKS_PER_PAGE = PAGE_SIZE // CMP_STRIDE
SEL_PAD = 256


def _bf16_round(x):
    return x.astype(BF16).astype(F32)


def _compress_tail(c, pos, w2):
    c1 = c[:, 128:]
    nxt = jnp.concatenate([c1[1:], c1[:1]], axis=0)
    hid = (pos + c[:, :128]) + nxt
    return jnp.dot(_gelu_tanh(hid).astype(BF16), w2, preferred_element_type=F32)


def _rms_heads128(x, gain):
    left = lax.broadcasted_iota(jnp.int32, x.shape, 1) < HEAD_DIM
    sq = x * x
    s0 = jnp.sum(jnp.where(left, sq, 0.0), axis=-1, keepdims=True)
    s1 = jnp.sum(jnp.where(left, 0.0, sq), axis=-1, keepdims=True)
    ms = jnp.where(left, s0, s1) * (1.0 / HEAD_DIM)
    return x * lax.rsqrt(ms + EPS) * gain


def _pad_rows8(x):
    return jnp.concatenate([x, jnp.zeros((8 - x.shape[0], x.shape[1]), x.dtype)], axis=0)


def _compress_mlp(row_of_chunks, n_chunks, w_ref, pos, w2_ref):
    acc = jnp.zeros((n_chunks, 2 * A_KVW), F32)
    for sp in range(CMP_STRIDE // 2):
        x = jnp.concatenate([row_of_chunks(2 * sp + e) for e in range(2)], axis=1).astype(BF16)
        acc = acc + jnp.dot(x, w_ref[sp], preferred_element_type=F32)
    return _compress_tail(acc, pos, w2_ref[...])


def _compress_rows(rows_scr, kv, n_chunks, w_ref, pos, w2_ref):
    return _compress_mlp(lambda s: rows_scr.at[kv][pl.ds(s, n_chunks, stride=CMP_STRIDE), :],
                         n_chunks, w_ref, pos, w2_ref)


def _compress_weights(cmp_pos, cmp_w1, cmp_w2, k_gain):
    eye = jnp.eye(A_KV, dtype=F32)
    w1 = cmp_w1.reshape(2, 2, CMP_STRIDE, HEAD_DIM, HEAD_DIM)
    w_big = jnp.einsum('ajsdh,gk->asgdjkh', w1, eye).reshape(2, CMP_STRIDE // 2, 2 * A_KVW, 2 * A_KVW).astype(BF16)
    w_pos = jnp.concatenate([cmp_w1, cmp_w1], axis=-1).astype(BF16)
    pos = jnp.broadcast_to(cmp_pos.reshape(2, 1, CMP_LEN * HEAD_DIM), (2, 8, CMP_LEN * HEAD_DIM)).astype(BF16)
    w2_big = jnp.einsum('ahd,gk->aghkd', cmp_w2, eye).reshape(2, A_KVW, A_KVW).astype(BF16)
    gain2 = jnp.tile(k_gain, A_KV).reshape(1, A_KVW)
    return (pos[0], pos[1], w_pos[0], w_pos[1], w_big[0], w_big[1], w2_big[0], w2_big[1], gain2)


def _compress_specs(full):
    mlp_w = (CMP_STRIDE // 2, 2 * A_KVW, 2 * A_KVW)
    return [full((8, CMP_LEN * HEAD_DIM)), full((8, CMP_LEN * HEAD_DIM)),
            full((CMP_LEN * HEAD_DIM, A_KVW)), full((CMP_LEN * HEAD_DIM, A_KVW)),
            full(mlp_w), full(mlp_w), full((A_KVW, A_KVW)), full((A_KVW, A_KVW)), full((1, A_KVW))]


def _pos_terms(pos_scr, posk_ref, posv_ref, wpos_k_ref, wpos_v_ref):
    pos_scr[0] = jnp.dot(posk_ref[...], wpos_k_ref[...], preferred_element_type=F32)
    pos_scr[1] = jnp.dot(posv_ref[...], wpos_v_ref[...], preferred_element_type=F32)


def _cmp_prompt_body(cmp_ref, posk_ref, posv_ref, wpos_k_ref, wpos_v_ref, wk_ref, wv_ref, w2k_ref, w2v_ref,
                     kgain_ref, kc_ref, vc_ref, pos_scr, rows_scr, *, n_chunks):
    @pl.when(pl.program_id(0) == 0)
    def _():
        _pos_terms(pos_scr, posk_ref, posv_ref, wpos_k_ref, wpos_v_ref)

    rows_scr[0] = cmp_ref[:, :A_KVW]
    rows_scr[1] = cmp_ref[:, A_KVW:]
    kc = _rms_heads128(_compress_rows(rows_scr, 0, n_chunks, wk_ref, pos_scr[0, 0:1], w2k_ref),
                       kgain_ref[...]).astype(BF16)
    vc = _compress_rows(rows_scr, 1, n_chunks, wv_ref, pos_scr[1, 0:1], w2v_ref).astype(BF16)
    for g in range(A_KV):
        kc_ref[g] = kc[:, g * HEAD_DIM:(g + 1) * HEAD_DIM]
        vc_ref[g] = vc[:, g * HEAD_DIM:(g + 1) * HEAD_DIM]


def _cmp_prompt(cmp_rows, cmp_weights):
    b, t, _ = cmp_rows.shape
    n_chunks = t // CMP_STRIDE
    full = lambda shape: pl.BlockSpec(shape, lambda i: (0,) * len(shape))
    out_spec = pl.BlockSpec((None, A_KV, n_chunks, HEAD_DIM), lambda i: (i, 0, 0, 0))
    out_shape = jax.ShapeDtypeStruct((b, A_KV, n_chunks, HEAD_DIM), BF16)
    return pl.pallas_call(
        functools.partial(_cmp_prompt_body, n_chunks=n_chunks),
        grid=(b,),
        in_specs=[pl.BlockSpec((None, t, 2 * A_KVW), lambda i: (i, 0, 0))] + _compress_specs(full),
        out_specs=[out_spec, out_spec],
        out_shape=[out_shape, out_shape],
        scratch_shapes=[pltpu.VMEM((2, 8, A_KVW), F32), pltpu.VMEM((2, t, A_KVW), F32)],
        compiler_params=pltpu.CompilerParams(
            dimension_semantics=("arbitrary",), vmem_limit_bytes=V7X_VMEM_LIMIT_BYTES),
        name="nsa_cmp_prompt",
    )(cmp_rows, *cmp_weights)


def _nsa_sample_cmp_body(pt_ref, *refs, n_pages):
    page_refs = refs[:n_pages]
    (posk_ref, posv_ref, wpos_k_ref, wpos_v_ref, wk_ref, wv_ref, w2k_ref, w2v_ref, kgain_ref,
     kc_ref, vc_ref, pos_scr, rows_scr) = refs[n_pages:]

    @pl.when(pl.program_id(0) == 0)
    def _():
        _pos_terms(pos_scr, posk_ref, posv_ref, wpos_k_ref, wpos_v_ref)

    n_chunks = n_pages * CHUNKS_PER_PAGE
    for kv in range(2):
        for p in range(n_pages):
            rows = page_refs[p][kv].reshape(A_KVW, PAGE_SIZE).T
            by_s = pltpu.einshape("csl->scl", rows.reshape(CHUNKS_PER_PAGE, CMP_STRIDE, A_KVW))
            for s in range(CMP_STRIDE):
                rows_scr[kv, s, p * CHUNKS_PER_PAGE:(p + 1) * CHUNKS_PER_PAGE, :] = by_s[s]

    compress = lambda kv, w_ref, w2_ref: _compress_mlp(lambda s: rows_scr[kv, s], n_chunks, w_ref,
                                                       pos_scr[kv, 0:1], w2_ref)
    kc_ref[...] = _rms_heads128(compress(0, wk_ref, w2k_ref), kgain_ref[...]).astype(BF16)
    vc_ref[...] = compress(1, wv_ref, w2v_ref).astype(BF16)


NSA_PICK_BATCH = 16


def _nsa_sample_pick_body(q_ref, kc_ref, vc_ref, cover_ref, pt_ref, oc_ref, idx_ref, phys_ref, *, n_cmp, n_sel):
    nb, n_chunks = kc_ref.shape[0], kc_ref.shape[1]
    col = lax.broadcasted_iota(jnp.int32, (8, n_chunks), 1)
    imps = []
    for i in range(nb):
        for g in range(A_KV):
            hs = slice(g * HEAD_DIM, (g + 1) * HEAD_DIM)
            q8 = (_pad_rows8(q_ref[i, g]) * (HEAD_DIM ** -0.5)).astype(BF16)
            s = jnp.where(col < n_cmp, _dot_nt(q8, kc_ref[i, :, hs]), NEG_INF)
            e = jnp.exp(s - jnp.max(s, axis=-1, keepdims=True))
            p = (e / jnp.sum(e, axis=-1, keepdims=True)).astype(BF16)
            oc_ref[i, g] = jnp.dot(p, vc_ref[i, :, hs], preferred_element_type=F32)[0:A_REP]
            imps.append(jnp.sum(jnp.dot(p, cover_ref[...], preferred_element_type=F32)[0:A_REP], axis=0, keepdims=True))
    imp = jnp.concatenate(imps, axis=0)
    rows = nb * A_KV
    s_idx = lax.broadcasted_iota(jnp.int32, (rows, SEL_PAD), 1)
    s_idx_f = s_idx.astype(F32)
    forced = (s_idx == 0) | (s_idx == n_sel - 1) | (s_idx == n_sel - 2)
    score = jnp.where(s_idx < n_sel, imp + jnp.where(forced, FORCE_BONUS, 0.0), NEG_INF)
    rank = jnp.zeros((rows, SEL_PAD), F32)
    for j in range(n_sel):
        cj = score[:, j:j + 1]
        beats = (cj > score) | ((cj == score) & (s_idx > j))
        rank = rank + jnp.where(beats, 1.0, 0.0)
    lane = lax.broadcasted_iota(jnp.int32, (rows, 128), 1)
    picks = jnp.zeros((rows, 128), F32)
    for r in range(SEL_TOPK):
        block = jnp.sum(jnp.where(rank == float(r), s_idx_f, 0.0), axis=-1, keepdims=True)
        picks = picks + jnp.where(lane == r, block, 0.0)
    picks = picks.astype(jnp.int32)
    idx_ref[...] = picks
    page = jnp.minimum(picks, n_sel - 2) // (PAGE_SIZE // SEL_BLOCK)
    phys_ref[...] = jnp.take_along_axis(pt_ref[...], page, axis=1)


NSA_ATT_BATCH = 2


def _nsa_sample_att_body(idx_ref, pt_ref, *refs, n_past_blk):
    n_slots = NSA_ATT_BATCH * A_KV * SEL_TOPK
    slot_refs = refs[:n_slots]
    (q_all, gate_all, oc_all, newsel_all, wcache_all, newwin_all, newcol_all, selb_all, winb_all, newb_all,
     o_all, wout_all) = refs[n_slots:]
    for i in range(NSA_ATT_BATCH):
        b = pl.program_id(0) * NSA_ATT_BATCH + i
        for g in range(A_KV):
            first = (i * A_KV + g) * SEL_TOPK
            _nsa_sample_att_group(b, g, idx_ref, slot_refs[first:first + SEL_TOPK], q_all.at[i, g], gate_all.at[i, g],
                                  oc_all.at[i, g], newsel_all.at[i, g], wcache_all.at[i, :, g], newwin_all.at[i, g],
                                  selb_all.at[g], winb_all.at[g], newb_all.at[g], o_all.at[i, g], n_past_blk)
            for kv in range(2):
                wout_all[i, kv, g] = _shift_in(wcache_all[i, kv, g], newcol_all[i, g][:, kv:kv + 1])


def _shift_in(rows_last, new_col):
    w_len = rows_last.shape[-1]
    lane = lax.broadcasted_iota(jnp.int32, rows_last.shape, 1)
    return jnp.where(lane == w_len - 1, new_col, pltpu.roll(rows_last, w_len - 1, axis=1))


def _nsa_sample_att_group(b, g, idx_ref, page_refs, q_ref, gate_ref, oc_ref, newsel_ref, wcache_ref, newwin_ref,
                          selb_ref, winb_ref, newb_ref, o_ref, n_past_blk):
    blocks_per_page = PAGE_SIZE // SEL_BLOCK
    q8 = (_pad_rows8(q_ref[...]) * (HEAD_DIM ** -0.5)).astype(BF16)
    new_bias = newb_ref[:, 0:1]

    def attend(s, v_t, new_kv):
        k_new = _bf16_round(new_kv[0:1])
        v_new = _bf16_round(new_kv[1:2])
        s_new = jnp.sum(q8.astype(F32) * k_new, axis=-1, keepdims=True) + new_bias
        m = jnp.maximum(jnp.max(s, axis=-1, keepdims=True), s_new)
        e = jnp.exp(s - m)
        e_new = jnp.exp(s_new - m)
        den = jnp.sum(e, axis=-1, keepdims=True) + e_new
        return _dot_nt((e / den).astype(BF16), v_t) + _bf16_round(e_new / den) * v_new

    lane = lax.broadcasted_iota(jnp.int32, (8, PAGE_SIZE), 1)
    near = selb_ref[...]
    bias = []
    for k in range(SEL_TOPK):
        ik = idx_ref[(b * A_KV + g) * SEL_TOPK + k]
        blk = jnp.minimum(ik, n_past_blk - 1)
        near_k = jnp.where(blk // blocks_per_page == (n_past_blk - 1) // blocks_per_page, near, 0.0)
        keep = (lane // SEL_BLOCK == blk % blocks_per_page) & (ik < n_past_blk)
        bias.append(jnp.where(keep, near_k, NEG_INF))
    k_sel = jnp.concatenate([page_refs[k][0].astype(BF16) for k in range(SEL_TOPK)], axis=1)
    v_sel = jnp.concatenate([page_refs[k][1].astype(BF16) for k in range(SEL_TOPK)], axis=1)
    s_sel = jnp.dot(q8, k_sel, preferred_element_type=F32) + jnp.concatenate(bias, axis=1)
    o_s = attend(s_sel, v_sel, newsel_ref[...])

    w_len = wcache_ref.shape[-1]
    wcol = lax.broadcasted_iota(jnp.int32, (8, w_len), 1)
    s_w = jnp.dot(q8, wcache_ref[0].astype(BF16), preferred_element_type=F32) + winb_ref[...]
    o_w = attend(jnp.where(wcol >= 1, s_w, NEG_INF), wcache_ref[1].astype(BF16), newwin_ref[...])
    gates = gate_ref[...]
    o_ref[...] = gates[:, 0:1] * oc_ref[...] + gates[:, 1:2] * o_s[0:A_REP] + gates[:, 2:3] * o_w[0:A_REP]


def _rows_last(cache):
    nd = cache.ndim
    return jnp.transpose(cache, tuple(range(nd - 4)) + (nd - 3, nd - 2, nd - 1, nd - 4))


def _rows_first(cache):
    nd = cache.ndim
    return jnp.transpose(cache, tuple(range(nd - 4)) + (nd - 1, nd - 4, nd - 3, nd - 2))


def _nsa_sample(q, gates, new_sel, new_win, pool_cmp, pool_sel, win_cache, page_table, cmp_weights, tab):
    n, n_pages = page_table.shape
    past = n_pages * PAGE_SIZE
    n_past_blk = past // SEL_BLOCK
    n_sel = n_past_blk + 1
    n_chunks = n_pages * CHUNKS_PER_PAGE
    w_len = win_cache.shape[1]
    blocks_per_page = PAGE_SIZE // SEL_BLOCK
    assert n_sel <= SEL_PAD and w_len == WIN_A and past >= WIN_A and blocks_per_page == 2
    pt_flat = page_table.reshape(-1)

    c_idx = np.arange(n_chunks)
    s_start = np.arange(SEL_PAD) * SEL_BLOCK
    cover = ((c_idx[:, None] * CMP_STRIDE < s_start[None, :] + SEL_BLOCK)
             & (c_idx[:, None] * CMP_STRIDE + CMP_LEN - 1 >= s_start[None, :])
             & (c_idx[:, None] < n_chunks - 1) & (np.arange(SEL_PAD)[None, :] < n_sel))
    cover = jnp.asarray(cover.astype(np.float32), BF16)

    head_spec = lambda last: pl.BlockSpec((None, A_KV, A_REP, last), lambda i, *_: (i, 0, 0, 0))
    page_specs = [pl.BlockSpec((None, 2, A_KV, HEAD_DIM, PAGE_SIZE),
                               functools.partial(lambda i, pt, p: (pt[i * n_pages + p], 0, 0, 0, 0), p=p))
                  for p in range(n_pages)]
    full = lambda shape: pl.BlockSpec(shape, lambda i, *_: (0,) * len(shape))
    pool_c = _rows_last(pool_cmp)
    cmp_spec = pl.BlockSpec((None, n_chunks, A_KVW), lambda i, *_: (i, 0, 0))
    cmp_shape = jax.ShapeDtypeStruct((n, n_chunks, A_KVW), BF16)
    kc, vc = pl.pallas_call(
        functools.partial(_nsa_sample_cmp_body, n_pages=n_pages),
        grid_spec=pltpu.PrefetchScalarGridSpec(
            num_scalar_prefetch=1,
            grid=(n,),
            in_specs=page_specs + _compress_specs(full),
            out_specs=[cmp_spec, cmp_spec],
            scratch_shapes=[pltpu.VMEM((2, 8, A_KVW), F32), pltpu.VMEM((2, CMP_STRIDE, n_chunks, A_KVW), F32)],
        ),
        out_shape=[cmp_shape, cmp_shape],
        compiler_params=pltpu.CompilerParams(
            dimension_semantics=("arbitrary",), vmem_limit_bytes=V7X_VMEM_LIMIT_BYTES),
        name="nsa_sample_cmp",
    )(pt_flat, *([pool_c] * n_pages), *cmp_weights)

    nb = NSA_PICK_BATCH
    assert n % nb == 0
    batch = lambda *tail: pl.BlockSpec((nb,) + tail, lambda i: (i,) + (0,) * len(tail))
    assert n_pages <= 128
    pt_rows = jnp.pad(jnp.repeat(page_table, A_KV, axis=0), ((0, 0), (0, 128 - n_pages)))
    rows_spec = pl.BlockSpec((nb * A_KV, 128), lambda i: (i, 0))
    rows_shape = jax.ShapeDtypeStruct((n * A_KV, 128), jnp.int32)
    o_c, idx, phys = pl.pallas_call(
        functools.partial(_nsa_sample_pick_body, n_cmp=n_chunks - 1, n_sel=n_sel),
        grid=(n // nb,),
        in_specs=[batch(A_KV, A_REP, HEAD_DIM), batch(n_chunks, A_KVW), batch(n_chunks, A_KVW),
                  pl.BlockSpec((n_chunks, SEL_PAD), lambda i: (0, 0)), rows_spec],
        out_specs=[batch(A_KV, A_REP, HEAD_DIM), rows_spec, rows_spec],
        out_shape=[jax.ShapeDtypeStruct((n, A_KV, A_REP, HEAD_DIM), F32), rows_shape, rows_shape],
        compiler_params=pltpu.CompilerParams(
            dimension_semantics=("arbitrary",), vmem_limit_bytes=V7X_VMEM_LIMIT_BYTES),
        name="nsa_sample_pick",
    )(q, kc, vc, cover, pt_rows)

    rel = lambda dist: jnp.pad(jnp.transpose(_bias_lookup(tab - tab[NUM_BUCKETS - 1], dist), (1, 2, 0)),
                               ((0, 0), (0, 8 - A_REP), (0, 0)))
    sel_bias = rel(past - (past - PAGE_SIZE + np.arange(PAGE_SIZE)))
    win_bias = rel(w_len - np.arange(w_len))
    new_bias = rel(np.zeros(128, np.int64))

    idx_flat = idx[:, :SEL_TOPK].reshape(-1)
    phys_flat = phys[:, :SEL_TOPK].reshape(-1)

    nba = NSA_ATT_BATCH
    assert n % nba == 0

    def page_map(i, idx_s, phys_s, seq, g, slot):
        return (phys_s[i * (nba * A_KV * SEL_TOPK) + (seq * A_KV + g) * SEL_TOPK + slot], 0, g, 0, 0)

    slot_specs = [pl.BlockSpec((None, 2, None, HEAD_DIM, PAGE_SIZE), functools.partial(page_map, seq=e, g=g, slot=s))
                  for e in range(nba) for g in range(A_KV) for s in range(SEL_TOPK)]
    per_head = lambda rows, last: pl.BlockSpec((nba, A_KV, rows, last), lambda i, *_: (i, 0, 0, 0))
    per_group = lambda last: pl.BlockSpec((A_KV, 8, last), lambda i, *_: (0, 0, 0))
    win_spec = pl.BlockSpec((nba, 2, A_KV, HEAD_DIM, w_len), lambda i, *_: (i, 0, 0, 0, 0))
    o, win_next = pl.pallas_call(
        functools.partial(_nsa_sample_att_body, n_past_blk=n_past_blk),
        grid_spec=pltpu.PrefetchScalarGridSpec(
            num_scalar_prefetch=2,
            grid=(n // nba,),
            in_specs=slot_specs + [
                per_head(A_REP, HEAD_DIM), per_head(A_REP, 3), per_head(A_REP, HEAD_DIM), per_head(2, HEAD_DIM),
                win_spec, per_head(2, HEAD_DIM), per_head(HEAD_DIM, 2),
                per_group(PAGE_SIZE), per_group(w_len), per_group(128),
            ],
            out_specs=[per_head(A_REP, HEAD_DIM), win_spec],
        ),
        out_shape=[jax.ShapeDtypeStruct((n, A_KV, A_REP, HEAD_DIM), F32),
                   jax.ShapeDtypeStruct((n, 2, A_KV, HEAD_DIM, w_len), F32)],
        compiler_params=pltpu.CompilerParams(
            dimension_semantics=("arbitrary",), vmem_limit_bytes=V7X_VMEM_LIMIT_BYTES),
        name="nsa_sample_att",
    )(idx_flat, phys_flat, *([_rows_last(pool_sel)] * (nba * A_KV * SEL_TOPK)), q, gates, o_c,
      jnp.transpose(new_sel, (0, 2, 1, 3)), _rows_last(win_cache), jnp.transpose(new_win, (0, 2, 1, 3)),
      jnp.transpose(new_win, (0, 2, 3, 1)), sel_bias, win_bias, new_bias)
    return o, _rows_first(win_next)


SWA_Q_TILE = WIN_C


def _swa_prompt_body(q_ref, kv_ref, btile_ref, sink_ref, o_ref, *, tq):
    qt = pl.program_id(1)
    q0 = pl.multiple_of(qt * tq, tq)
    prev0 = pl.multiple_of(jnp.maximum(q0 - tq, 0), tq)
    width = C_REP * HEAD_DIM
    for g in range(C_KV):
        q = q_ref[:, g * width:(g + 1) * width]
        qs = jnp.concatenate([q[:, r * HEAD_DIM:(r + 1) * HEAD_DIM] for r in range(C_REP)], axis=0)
        qs = (qs * (HEAD_DIM ** -0.5)).astype(BF16)
        k_ref, v_ref = kv_ref.at[g], kv_ref.at[C_KV + g]
        k = jnp.concatenate([k_ref[pl.ds(prev0, tq), :], k_ref[pl.ds(q0, tq), :]], axis=0)
        v = jnp.concatenate([v_ref[pl.ds(prev0, tq), :], v_ref[pl.ds(q0, tq), :]], axis=0)
        s = _dot_nt(k, qs) + btile_ref[g]
        sinks = sink_ref[g]
        sink = jnp.concatenate([sinks[r:r + 1, :] for r in range(C_REP)], axis=1)
        m = jnp.maximum(jnp.max(s, axis=0, keepdims=True), sink)
        e = jnp.exp(s - m)
        p = e / (jnp.sum(e, axis=0, keepdims=True) + jnp.exp(sink - m))
        o = _dot_tn(v, p.astype(BF16))
        o_ref[:, g * width:(g + 1) * width] = jnp.concatenate([o[:, r * tq:(r + 1) * tq].T for r in range(C_REP)],
                                                              axis=1)


def _swa_prompt(q, kvt, sinks, tab):
    b, t, _ = q.shape
    tq = SWA_Q_TILE
    btile = jnp.swapaxes(_near_bias_tiles(tab, tq, WIN_C), -1, -2)
    qo_spec = pl.BlockSpec((None, tq, C_HEADS * HEAD_DIM), lambda i, j: (i, j, 0))
    sink_lanes = jnp.broadcast_to(sinks.reshape(C_KV, C_REP, 1), (C_KV, C_REP, 128))
    return pl.pallas_call(
        functools.partial(_swa_prompt_body, tq=tq),
        grid=(b, t // tq),
        in_specs=[qo_spec,
                  pl.BlockSpec((None, 2 * C_KV, t, HEAD_DIM), lambda i, j: (i, 0, 0, 0)),
                  pl.BlockSpec((None, C_KV, 2 * tq, C_REP * tq), lambda i, j: (jnp.minimum(j, 1), 0, 0, 0)),
                  pl.BlockSpec((C_KV, C_REP, 128), lambda i, j: (0, 0, 0))],
        out_specs=qo_spec,
        out_shape=jax.ShapeDtypeStruct((b, t, C_HEADS * HEAD_DIM), F32),
        compiler_params=pltpu.CompilerParams(
            dimension_semantics=("parallel", "arbitrary"), vmem_limit_bytes=V7X_VMEM_LIMIT_BYTES),
        name="swa_prompt",
    )(q, kvt, btile, sink_lanes)


SWA_DECODE_BATCH = 4


def _swa_sample_body(q_ref, cache_ref, new_ref, newcol_ref, bias_ref, newb_ref, sink_ref, o_ref, next_ref):
    w_len = cache_ref.shape[-1]
    wcol = lax.broadcasted_iota(jnp.int32, (C_REP, w_len), 1)
    for i in range(q_ref.shape[0]):
        for g in range(C_KV):
            q8 = (q_ref[i, g] * (HEAD_DIM ** -0.5)).astype(BF16)
            s = jnp.dot(q8, cache_ref[i, 0, g].astype(BF16), preferred_element_type=F32) + bias_ref[g]
            s = jnp.where(wcol >= 1, s, NEG_INF)
            k_new = _bf16_round(new_ref[i, g, 0:1])
            v_new = _bf16_round(new_ref[i, g, 1:2])
            s_new = jnp.sum(q8.astype(F32) * k_new, axis=-1, keepdims=True) + newb_ref[g][:, 0:1]
            sink = sink_ref[g][:, 0:1]
            m = jnp.maximum(jnp.maximum(jnp.max(s, axis=-1, keepdims=True), s_new), sink)
            e = jnp.exp(s - m)
            e_new = jnp.exp(s_new - m)
            den = jnp.sum(e, axis=-1, keepdims=True) + e_new + jnp.exp(sink - m)
            o_ref[i, g] = (_dot_nt((e / den).astype(BF16), cache_ref[i, 1, g].astype(BF16))
                           + _bf16_round(e_new / den) * v_new)
            for kv in range(2):
                next_ref[i, kv, g] = _shift_in(cache_ref[i, kv, g], newcol_ref[i, g][:, kv:kv + 1])


def _swa_sample(q, cache, new_kv, sinks, tab):
    n, w_len = cache.shape[:2]
    bias = jnp.transpose(_bias_lookup(tab, w_len - np.arange(w_len)), (1, 2, 0))
    lanes = lambda x: jnp.broadcast_to(x[:, :, None], (C_KV, C_REP, 128))
    full = lambda shape: pl.BlockSpec(shape, lambda i: (0,) * len(shape))
    nb = SWA_DECODE_BATCH
    assert n % nb == 0
    cache_spec = pl.BlockSpec((nb, 2, C_KV, HEAD_DIM, w_len), lambda i: (i, 0, 0, 0, 0))
    o_spec = pl.BlockSpec((nb, C_KV, C_REP, HEAD_DIM), lambda i: (i, 0, 0, 0))
    o, cache_next = pl.pallas_call(
        _swa_sample_body,
        grid=(n // nb,),
        in_specs=[
            o_spec, cache_spec,
            pl.BlockSpec((nb, C_KV, 2, HEAD_DIM), lambda i: (i, 0, 0, 0)),
            pl.BlockSpec((nb, C_KV, HEAD_DIM, 2), lambda i: (i, 0, 0, 0)),
            full((C_KV, C_REP, w_len)), full((C_KV, C_REP, 128)), full((C_KV, C_REP, 128)),
        ],
        out_specs=[o_spec, cache_spec],
        out_shape=[jax.ShapeDtypeStruct((n, C_KV, C_REP, HEAD_DIM), F32),
                   jax.ShapeDtypeStruct((n, 2, C_KV, HEAD_DIM, w_len), F32)],
        compiler_params=pltpu.CompilerParams(
            dimension_semantics=("arbitrary",), vmem_limit_bytes=V7X_VMEM_LIMIT_BYTES),
        name="swa_sample",
    )(q, _rows_last(cache), new_kv, jnp.transpose(new_kv, (0, 1, 3, 2)), bias, lanes(tab[0]),
      lanes(sinks.reshape(C_KV, C_REP)))
    return o, _rows_first(cache_next)


PROJ_ROW_TILE = 512
A_IN_PAD = 1920
A_U_COL = A_Q + 6 * A_KVW
A_GATE_COL = A_U_COL + B_WIDTH


def _heads_first(kvt_ref, slot, k, v, n_kv):
    for g in range(n_kv):
        kvt_ref[slot + g] = k[:, g * HEAD_DIM:(g + 1) * HEAD_DIM].astype(BF16)
        kvt_ref[slot + n_kv + g] = v[:, g * HEAD_DIM:(g + 1) * HEAD_DIM].astype(BF16)


def _inproj_a_body(x_ref, gain_ref, w_ref, qg_ref, kg_ref,
                   q_ref, cmp_ref, sel_ref, win_ref, gate_ref, u_ref, kvt_ref):
    xn = _rms_rows(x_ref[...], gain_ref[...]).astype(BF16)
    z = jnp.dot(xn, w_ref[...], preferred_element_type=F32)
    for j in range(A_Q // 128):
        q_ref[:, j * 128:(j + 1) * 128] = _rms_heads128(z[:, j * 128:(j + 1) * 128], qg_ref[...])
    for out_ref, off, slot in ((cmp_ref, A_Q, None), (sel_ref, A_Q + 2 * A_KVW, 0), (win_ref, A_Q + 4 * A_KVW, 4)):
        k = _rms_heads128(z[:, off:off + A_KVW], kg_ref[...])
        v = z[:, off + A_KVW:off + 2 * A_KVW]
        out_ref[:, :A_KVW] = k
        out_ref[:, A_KVW:] = v
        if slot is not None:
            _heads_first(kvt_ref, slot, k, v, A_KV)
    u_ref[...] = z[:, A_U_COL:A_GATE_COL]
    gate_ref[...] = 1.0 / (1.0 + jnp.exp(-z[:, A_GATE_COL:A_IN_PAD]))


def _inproj_a(x, gain, w_in, q_gain, k_gain):
    b, t, d = x.shape
    tm = min(t, PROJ_ROW_TILE)
    w = jnp.concatenate([w_in[:, :A_U_COL], w_in[:, A_U_COL + A_GATE:], w_in[:, A_U_COL:A_U_COL + A_GATE],
                         jnp.zeros((d, A_IN_PAD - A_GATE_COL - A_GATE), F32)], axis=1).astype(BF16)
    tile2 = lambda g: jnp.tile(g, 2).reshape(1, 128)
    rows = lambda width: pl.BlockSpec((None, tm, width), lambda i, j: (i, j, 0))
    shape = lambda width: jax.ShapeDtypeStruct((b, t, width), F32)
    return pl.pallas_call(
        _inproj_a_body,
        grid=(b, t // tm),
        in_specs=[rows(d), _const_spec((1, d)), _const_spec((d, A_IN_PAD)), _const_spec((1, 128)), _const_spec((1, 128))],
        out_specs=[rows(A_Q), rows(2 * A_KVW), rows(2 * A_KVW), rows(2 * A_KVW), rows(128), rows(B_WIDTH),
                   pl.BlockSpec((None, 8, tm, HEAD_DIM), lambda i, j: (i, 0, j, 0))],
        out_shape=[shape(A_Q), shape(2 * A_KVW), shape(2 * A_KVW), shape(2 * A_KVW), shape(128), shape(B_WIDTH),
                   jax.ShapeDtypeStruct((b, 8, t, HEAD_DIM), BF16)],
        compiler_params=pltpu.CompilerParams(
            dimension_semantics=("parallel", "parallel"), vmem_limit_bytes=V7X_VMEM_LIMIT_BYTES),
        name="inproj_nsa_s5",
    )(x, gain.reshape(1, d), w, tile2(q_gain), tile2(k_gain))


def _inproj_c_body(x_ref, gain_ref, w_ref, qg_ref, kg_ref, q_ref, kv_ref, kvt_ref):
    xn = _rms_rows(x_ref[...], gain_ref[...]).astype(BF16)
    z = jnp.dot(xn, w_ref[...], preferred_element_type=F32)
    n_q = C_HEADS * HEAD_DIM
    for j in range(n_q // 128):
        q_ref[:, j * 128:(j + 1) * 128] = _rms_heads128(z[:, j * 128:(j + 1) * 128], qg_ref[...])
    k = _rms_heads128(z[:, n_q:n_q + C_KV * HEAD_DIM], kg_ref[...])
    v = z[:, n_q + C_KV * HEAD_DIM:]
    kv_ref[:, :C_KV * HEAD_DIM] = k
    kv_ref[:, C_KV * HEAD_DIM:] = v
    _heads_first(kvt_ref, 0, k, v, C_KV)


def _inproj_c(x, gain, w_in, q_gain, k_gain):
    b, t, d = x.shape
    tm = min(t, PROJ_ROW_TILE)
    n_in = w_in.shape[1]
    tile2 = lambda g: jnp.tile(g, 2).reshape(1, 128)
    rows = lambda width: pl.BlockSpec((None, tm, width), lambda i, j: (i, j, 0))
    shape = lambda width: jax.ShapeDtypeStruct((b, t, width), F32)
    return pl.pallas_call(
        _inproj_c_body,
        grid=(b, t // tm),
        in_specs=[rows(d), _const_spec((1, d)), _const_spec((d, n_in)), _const_spec((1, 128)), _const_spec((1, 128))],
        out_specs=[rows(C_HEADS * HEAD_DIM), rows(2 * C_KV * HEAD_DIM),
                   pl.BlockSpec((None, 2 * C_KV, tm, HEAD_DIM), lambda i, j: (i, 0, j, 0))],
        out_shape=[shape(C_HEADS * HEAD_DIM), shape(2 * C_KV * HEAD_DIM),
                   jax.ShapeDtypeStruct((b, 2 * C_KV, t, HEAD_DIM), BF16)],
        compiler_params=pltpu.CompilerParams(
            dimension_semantics=("parallel", "parallel"), vmem_limit_bytes=V7X_VMEM_LIMIT_BYTES),
        name="inproj_swa",
    )(x, gain.reshape(1, d), w_in.astype(BF16), tile2(q_gain), tile2(k_gain))


FFN_ROW_TILE = 512
FFN_COL_CHUNK = 2816


def _mixer_residual(y_ref, x_ref, mix_refs, wout_refs):
    y_ref[...] = x_ref[...]
    for m_ref, w_ref in zip(mix_refs, wout_refs):
        y_ref[...] += jnp.dot(m_ref[...].astype(BF16), w_ref[...], preferred_element_type=F32)
    return y_ref[...]


def _tail_prompt_body(*refs, n_mix, tm, ffc):
    x_ref = refs[0]
    mix_refs = refs[1:1 + n_mix]
    wout_refs = refs[1 + n_mix:1 + 2 * n_mix]
    (gain_ref, prev_ref, wup_ref, wgate_ref, cw_ref, cb_ref, wdown_ref, y_ref, cs_ref, hbuf_ref) = refs[1 + 2 * n_mix:]
    t = pl.program_id(1)
    xn = _rms_rows(_mixer_residual(y_ref, x_ref, mix_refs, wout_refs), gain_ref[...]).astype(BF16)
    for c in range(D_FF // ffc):
        lo = c * ffc
        h = jnp.dot(xn, wup_ref[:, lo:lo + ffc], preferred_element_type=F32)
        g = jnp.dot(xn, wgate_ref[:, lo:lo + ffc], preferred_element_type=F32)

        @pl.when(t == 0)
        def _():
            hbuf_ref[c, 6:8, :] = prev_ref[:, lo:lo + ffc]

        hbuf_ref[c, 8:8 + tm, :] = h
        hm1 = hbuf_ref[c, 7:7 + tm, :]
        hm2 = hbuf_ref[c, 6:6 + tm, :]
        cw = cw_ref[:, lo:lo + ffc]
        hc = cw[0:1] * hm2 + cw[1:2] * hm1 + cw[2:3] * h + cb_ref[:, lo:lo + ffc]
        a = (_gelu_tanh(hc) * g).astype(BF16)
        y_ref[...] += jnp.dot(a, wdown_ref[lo:lo + ffc, :], preferred_element_type=F32)
        hbuf_ref[c, 0:8, :] = h[tm - 8:tm, :]
        cs_ref[:, lo:lo + ffc] = h[tm - 2:tm, :]


def _tail_prompt(x, mixes, wouts, gain, prev, wup, wgate, cw, cb, wdown):
    b, t, d = x.shape
    tm, ffc = FFN_ROW_TILE, FFN_COL_CHUNK
    rows = lambda width: pl.BlockSpec((None, tm, width), lambda i, j: (i, j, 0))
    state = pl.BlockSpec((None, CONV_W - 1, D_FF), lambda i, j: (i, 0, 0))
    return pl.pallas_call(
        functools.partial(_tail_prompt_body, n_mix=len(mixes), tm=tm, ffc=ffc),
        grid=(b, t // tm),
        in_specs=[rows(d)] + [rows(m.shape[-1]) for m in mixes] + [_const_spec(w.shape) for w in wouts] + [
            _const_spec((1, d)), state, _const_spec((d, D_FF)), _const_spec((d, D_FF)),
            _const_spec((CONV_W, D_FF)), _const_spec((1, D_FF)), _const_spec((D_FF, d))],
        out_specs=[rows(d), state],
        out_shape=[jax.ShapeDtypeStruct((b, t, d), F32), jax.ShapeDtypeStruct((b, CONV_W - 1, D_FF), F32)],
        scratch_shapes=[pltpu.VMEM((D_FF // ffc, 8 + tm, ffc), F32)],
        compiler_params=pltpu.CompilerParams(
            dimension_semantics=("parallel", "arbitrary"), vmem_limit_bytes=V7X_VMEM_LIMIT_BYTES),
        name="tail_prompt",
    )(x, *mixes, *wouts, gain, prev, wup, wgate, cw, cb, wdown)


def _tail_sample_body(*refs, n_mix, ffc):
    x_ref = refs[0]
    mix_refs = refs[1:1 + n_mix]
    wout_refs = refs[1 + n_mix:1 + 2 * n_mix]
    (gain_ref, prev_ref, wup_ref, wgate_ref, cw_ref, cb_ref, wdown_ref, y_ref, cs_ref) = refs[1 + 2 * n_mix:]
    xn = _rms_rows(_mixer_residual(y_ref, x_ref, mix_refs, wout_refs), gain_ref[...]).astype(BF16)
    for c in range(D_FF // ffc):
        lo = c * ffc
        h = jnp.dot(xn, wup_ref[:, lo:lo + ffc], preferred_element_type=F32)
        g = jnp.dot(xn, wgate_ref[:, lo:lo + ffc], preferred_element_type=F32)
        hm2 = prev_ref[:, lo:lo + ffc]
        hm1 = prev_ref[:, D_FF + lo:D_FF + lo + ffc]
        cw = cw_ref[:, lo:lo + ffc]
        hc = cw[0:1] * hm2 + cw[1:2] * hm1 + cw[2:3] * h + cb_ref[:, lo:lo + ffc]
        a = (_gelu_tanh(hc) * g).astype(BF16)
        y_ref[...] += jnp.dot(a, wdown_ref[lo:lo + ffc, :], preferred_element_type=F32)
        cs_ref[:, lo:lo + ffc] = hm1
        cs_ref[:, D_FF + lo:D_FF + lo + ffc] = h


def _tail_sample(x, mixes, wouts, gain, prev, wup, wgate, cw, cb, wdown):
    n, d = x.shape
    full = lambda shape: _const_spec(shape)
    return pl.pallas_call(
        functools.partial(_tail_sample_body, n_mix=len(mixes), ffc=FFN_COL_CHUNK),
        grid=(1,),
        in_specs=[full((n, d))] + [full(m.shape) for m in mixes] + [full(w.shape) for w in wouts] + [
            full((1, d)), full((n, (CONV_W - 1) * D_FF)), full((d, D_FF)), full((d, D_FF)),
            full((CONV_W, D_FF)), full((1, D_FF)), full((D_FF, d))],
        out_specs=[pl.BlockSpec((n, d), lambda i: (0, 0)), pl.BlockSpec((n, (CONV_W - 1) * D_FF), lambda i: (0, 0))],
        out_shape=[jax.ShapeDtypeStruct((n, d), F32), jax.ShapeDtypeStruct((n, (CONV_W - 1) * D_FF), F32)],
        compiler_params=pltpu.CompilerParams(
            dimension_semantics=("arbitrary",), vmem_limit_bytes=V7X_VMEM_LIMIT_BYTES),
        name="tail_sample",
    )(x, *mixes, *wouts, gain, prev, wup, wgate, cw, cb, wdown)


def kernel(x_prompt, x_sample, cache_nsa_cmp, cache_nsa_sel, cache_nsa_win, state_s5_re, state_s5_im,
           cache_swa, state_ffn_conv, page_table, rel_bias, norm_mix, norm_ffn, a_w_in, a_w_out,
           nsa_q_gain, nsa_k_gain, nsa_cmp_pos, nsa_cmp_w1, nsa_cmp_w2, s5_a_re, s5_a_im, s5_log_dt,
           s5_b_re, s5_b_im, s5_c_re, s5_c_im, s5_d, s5_w_glu, s5_b_glu, c_w_in, c_w_out, c_q_gain,
           c_k_gain, c_sinks, ffn_w_up, ffn_w_gate, ffn_conv_w, ffn_conv_b, ffn_w_down):
    bp, tp, _ = x_prompt.shape
    bs, ts, _ = x_sample.shape
    assert ts == 1 and DEPTH == 2
    tab_a = rel_bias[:, :A_HEADS].reshape(NUM_BUCKETS, A_KV, A_REP)
    tab_c = rel_bias[:, :C_HEADS].reshape(NUM_BUCKETS, C_KV, C_REP)
    kv6 = lambda x: x.reshape(x.shape[:-1] + (2, x.shape[-1] // (2 * HEAD_DIM), HEAD_DIM))
    hp, hs = x_prompt, x_sample.reshape(1, bs, D_MODEL)
    conv_p, conv_s = [], []

    def tail(layer, hp, hs, mixes_p, mixes_s, wouts):
        wouts = [w.astype(BF16) for w in wouts]
        ffn = (norm_ffn[layer].reshape(1, D_MODEL),)
        wts = (ffn_w_up[layer].astype(BF16), ffn_w_gate[layer].astype(BF16), ffn_conv_w[layer],
               ffn_conv_b[layer].reshape(1, D_FF), ffn_w_down[layer].astype(BF16))
        hp, cp = _tail_prompt(hp, mixes_p, wouts, *ffn, jnp.zeros((bp, CONV_W - 1, D_FF), F32), *wts)
        hs2, cs = _tail_sample(hs[0], [m.reshape(bs, -1) for m in mixes_s], wouts, *ffn,
                               state_ffn_conv[layer].reshape(bs, (CONV_W - 1) * D_FF), *wts)
        conv_p.append(cp)
        conv_s.append(cs.reshape(bs, CONV_W - 1, D_FF))
        return hp, hs2.reshape(1, bs, D_MODEL)

    proj = (norm_mix[0], a_w_in[0], nsa_q_gain[0], nsa_k_gain[0])
    qp, cmp_p, sel_p, win_p, gate_p, up, kvt_p = _inproj_a(hp, *proj)
    qs, cmp_s, sel_s, win_s, gate_s, us, _ = _inproj_a(hs, *proj)
    cmp_w = _compress_weights(nsa_cmp_pos[0], nsa_cmp_w1[0], nsa_cmp_w2[0], nsa_k_gain[0])
    kc, vc = _cmp_prompt(cmp_p, cmp_w)
    o_ap = _nsa_prompt_attend(qp, gate_p, kc, vc, kvt_p, tab_a)
    o_as, nsa_win_s = _nsa_sample(qs.reshape(bs, A_KV, A_REP, HEAD_DIM),
                                  gate_s[0, :, :A_GATE].reshape(bs, A_KV, A_REP, 3),
                                  sel_s.reshape(bs, 2, A_KV, HEAD_DIM), win_s.reshape(bs, 2, A_KV, HEAD_DIM),
                                  cache_nsa_cmp[0], cache_nsa_sel[0], cache_nsa_win[0], page_table, cmp_w, tab_a)
    s5w = _s5_weights(s5_a_re[0], s5_a_im[0], s5_log_dt[0], s5_b_re[0], s5_b_im[0], s5_c_re[0], s5_c_im[0],
                      s5_d[0], s5_w_glu[0], s5_b_glu[0])
    h0 = jnp.zeros((bp, S5_NS), F32)
    o_bp, hr_p, hi_p = _s5_mix(up, h0, h0, s5w)
    o_bs, hr_s, hi_s = _s5_mix(us.reshape(bs, 1, B_WIDTH), state_s5_re[0].reshape(bs, S5_NS),
                               state_s5_im[0].reshape(bs, S5_NS), s5w)
    hp, hs = tail(0, hp, hs, [o_ap, o_bp], [o_as, o_bs], [a_w_out[0][:A_Q], a_w_out[0][A_Q:]])

    proj = (norm_mix[1], c_w_in[0], c_q_gain[0], c_k_gain[0])
    qcp, kv_p, kvt_c = _inproj_c(hp, *proj)
    qcs, kv_s, _ = _inproj_c(hs, *proj)
    o_cp = _swa_prompt(qcp, kvt_c, c_sinks[0], tab_c)
    new_kv = kv_s.reshape(bs, 2, C_KV, HEAD_DIM)
    o_cs, swa_s = _swa_sample(qcs.reshape(bs, C_KV, C_REP, HEAD_DIM), cache_swa[0],
                              jnp.transpose(new_kv, (0, 2, 1, 3)), c_sinks[0], tab_c)
    hp, hs = tail(1, hp, hs, [o_cp], [o_cs], [c_w_out[0]])

    state = lambda x, n: x.reshape(1, n, B_GROUPS, B_STATE)
    return (hp, hs.reshape(bs, ts, D_MODEL),
            kv6(cmp_p)[None], kv6(cmp_s).reshape(1, bs, ts, 2, A_KV, HEAD_DIM),
            kv6(sel_p)[None], kv6(sel_s).reshape(1, bs, ts, 2, A_KV, HEAD_DIM),
            kv6(win_p)[None, :, -min(WIN_A, tp):], nsa_win_s[None],
            state(hr_p, bp), state(hi_p, bp), state(hr_s, bs), state(hi_s, bs),
            kv6(kv_p)[None, :, -min(WIN_C, tp):], swa_s[None],
            jnp.stack(conv_p), jnp.stack(conv_s))
```

```python
import functools
import math

import jax
import jax.numpy as jnp
import numpy as np
from jax import lax
from jax.experimental import pallas as pl
from jax.experimental.pallas import tpu as pltpu

D_MODEL = 1024
DEPTH = 2
PAGE_SIZE = 128
HEAD_DIM = 64
A_HEADS = 8
A_KV = 2
A_REP = A_HEADS // A_KV
A_Q = A_HEADS * HEAD_DIM
A_KVW = A_KV * HEAD_DIM
A_GATE = 3 * A_HEADS
CMP_LEN = 32
CMP_STRIDE = 16
SEL_BLOCK = 64
SEL_TOPK = 16
WIN_A = 512
FORCE_BONUS = 1000.0
B_WIDTH = D_MODEL // 2
B_GROUP = 16
B_GROUPS = B_WIDTH // B_GROUP
B_STATE = 64
C_HEADS = D_MODEL // HEAD_DIM
C_KV = 2
C_REP = C_HEADS // C_KV
WIN_C = 128
NUM_BUCKETS = 32
MAX_DISTANCE = 128
D_FF = 2816
CONV_W = 3
EPS = 1e-6

F32 = jnp.float32
BF16 = jnp.bfloat16

V7X_VMEM_LIMIT_BYTES = 56 * 1024 * 1024


def _gelu_tanh(x):
    c = math.sqrt(2.0 / math.pi)
    return 0.5 * x * (1.0 + jnp.tanh(x * (c + (c * 0.044715) * (x * x))))


def _rms_rows(x, gain):
    return x * lax.rsqrt(jnp.mean(x * x, axis=-1, keepdims=True) + EPS) * gain


def _const_spec(shape):
    zeros = (0,) * len(shape)
    return pl.BlockSpec(shape, lambda *_: zeros, pipeline_mode=pl.Buffered(1))


NSA_Q_TILE = 256
NSA_FAR_TILE = 512
NSA_HEADS_PER_PASS = 4
NEG_INF = float("-inf")


def _dot_nt(a, b):
    return lax.dot_general(a, b, (((1,), (1,)), ((), ())), preferred_element_type=F32)


def _dot_tn(a, b):
    return lax.dot_general(a, b, (((0,), (0,)), ((), ())), preferred_element_type=F32)


def _softmax_start(s, v):
    m = jnp.max(s, axis=0, keepdims=True)
    e = jnp.exp(s - m)
    return m, jnp.sum(e, axis=0, keepdims=True), _dot_tn(v, e.astype(BF16))


def _softmax_more(carry, s, v):
    m, l, acc = carry
    m_new = jnp.maximum(m, jnp.max(s, axis=0, keepdims=True))
    alpha = jnp.exp(m - m_new)
    e = jnp.exp(s - m_new)
    return m_new, alpha * l + jnp.sum(e, axis=0, keepdims=True), alpha * acc + _dot_tn(v, e.astype(BF16))


def _nsa_prompt_body(q_ref, gate_ref, kc_ref, vc_ref, kvt_ref, btile_ref, cover_ref, o_ref, sel_scr, *, tq):
    width = A_REP * HEAD_DIM
    gates_t = gate_ref[...].T
    for g in range(A_KV):
        _nsa_prompt_group(g, q_ref.at[:, g * width:(g + 1) * width], gates_t, kc_ref.at[g], vc_ref.at[g],
                          kvt_ref.at[g], kvt_ref.at[A_KV + g], kvt_ref.at[2 * A_KV + g], kvt_ref.at[3 * A_KV + g],
                          btile_ref.at[g], cover_ref, o_ref.at[:, g * width:(g + 1) * width], sel_scr.at[g], tq)


def _nsa_prompt_group(g, q_ref, gates_t, kc_ref, vc_ref, ks_ref, vs_ref, kw_ref, vw_ref, btile_ref, cover_ref,
                      o_ref, sel_ref, tq):
    qt = pl.program_id(1)
    q0 = pl.multiple_of(qt * tq, tq)
    cols = A_REP * tq
    q = q_ref[...]
    qs = jnp.concatenate([q[:, r * HEAD_DIM:(r + 1) * HEAD_DIM] for r in range(A_REP)], axis=0)
    qs = (qs * (HEAD_DIM ** -0.5)).astype(BF16)

    def q_pos(height):
        return q0 + (lax.broadcasted_iota(jnp.int32, (height, cols), 1) & (tq - 1))

    def key_idx(height):
        return lax.broadcasted_iota(jnp.int32, (height, cols), 0)

    n_idx = key_idx(128)
    valid_c = (n_idx * CMP_STRIDE + (CMP_LEN - 1) <= q_pos(128)) & (n_idx < 127)
    s_c = jnp.where(valid_c, _dot_nt(kc_ref[...], qs), NEG_INF)
    m_c = jnp.max(s_c, axis=0, keepdims=True)
    m_c = jnp.where(m_c == NEG_INF, 0.0, m_c)
    e_c = jnp.exp(s_c - m_c)
    d_c = jnp.sum(e_c, axis=0, keepdims=True)
    p_c = (e_c / jnp.where(d_c > 0, d_c, 1.0)).astype(BF16)
    o_c = _dot_tn(vc_ref[...], p_c)
    imp_heads = jnp.dot(cover_ref[...], p_c, preferred_element_type=F32)
    imp = sum(imp_heads[0:32, r * tq:(r + 1) * tq] for r in range(A_REP))
    s_idx = lax.broadcasted_iota(jnp.int32, (32, tq), 0)
    qblk = (q0 + lax.broadcasted_iota(jnp.int32, (32, tq), 1)) >> 6
    forced = (s_idx == 0) | (s_idx == qblk) | (s_idx == qblk - 1)
    allowed = s_idx <= qblk
    score = jnp.where(allowed, imp + jnp.where(forced, FORCE_BONUS, 0.0), NEG_INF)
    rank = jnp.zeros((32, tq), F32)
    for j in range(32):
        other = score[j:j + 1, :]
        beats = (other > score) | ((other == score) & (s_idx > j))
        rank = rank + jnp.where(beats, 1.0, 0.0)
    sel = jnp.where((rank < SEL_TOPK) & allowed, 1.0, 0.0)
    hcols = NSA_HEADS_PER_PASS * tq
    far_end = q0 - tq
    sel_far_only = jnp.where(s_idx < far_end // SEL_BLOCK, sel, 0.0)
    sel_ref[...] = jnp.concatenate([jnp.concatenate([sel] * NSA_HEADS_PER_PASS, axis=1),
                                    jnp.concatenate([sel_far_only] * NSA_HEADS_PER_PASS, axis=1)], axis=0)

    def block_mask(k0, n_blocks, base=0):
        first = base + k0 // SEL_BLOCK
        return jnp.concatenate([jnp.broadcast_to(sel_ref[pl.ds(first + j, 1), :], (SEL_BLOCK, hcols))
                                for j in range(n_blocks)], axis=0) > 0.5

    def hq_pos(height):
        return q0 + (lax.broadcasted_iota(jnp.int32, (height, hcols), 1) & (tq - 1))

    def hkey_idx(height):
        return lax.broadcasted_iota(jnp.int32, (height, hcols), 0)

    prev0 = pl.multiple_of(jnp.maximum(q0 - tq, 0), tq)
    sel_near = jnp.concatenate([block_mask(prev0, tq // SEL_BLOCK), block_mask(q0, tq // SEL_BLOCK)], axis=0)
    n_far = (jnp.maximum(far_end, 0) + NSA_FAR_TILE - 1) // NSA_FAR_TILE
    w_far = WIN_A - tq
    wf0 = pl.multiple_of(jnp.maximum(q0 - WIN_A, 0), tq)
    wpos = wf0 + hkey_idx(w_far)
    wmask = (hq_pos(w_far) - wpos < WIN_A) & (wpos < far_end)

    outs = []
    for h0 in range(0, A_REP, NSA_HEADS_PER_PASS):
        csl = slice(h0 * tq, h0 * tq + hcols)
        qh = qs[h0 * tq:h0 * tq + hcols]
        btile = btile_ref[:, csl]

        def near(k_ref, v_ref, extra_mask):
            k = jnp.concatenate([k_ref[pl.ds(prev0, tq), :], k_ref[pl.ds(q0, tq), :]], axis=0)
            v = jnp.concatenate([v_ref[pl.ds(prev0, tq), :], v_ref[pl.ds(q0, tq), :]], axis=0)
            s = _dot_nt(k, qh) + btile
            return _softmax_start(s if extra_mask is None else jnp.where(extra_mask, s, NEG_INF), v)

        def sel_far(i, carry):
            k0 = pl.multiple_of(i * NSA_FAR_TILE, NSA_FAR_TILE)
            s = _dot_nt(ks_ref[pl.ds(k0, NSA_FAR_TILE), :], qh)
            mask = block_mask(k0, NSA_FAR_TILE // SEL_BLOCK, base=32)
            return _softmax_more(carry, jnp.where(mask, s, NEG_INF), vs_ref[pl.ds(k0, NSA_FAR_TILE), :])

        _, l_s, acc_s = lax.fori_loop(0, n_far, sel_far, near(ks_ref, vs_ref, sel_near))
        o_s = acc_s / l_s

        s_w = jnp.where(wmask, _dot_nt(kw_ref[pl.ds(wf0, w_far), :], qh), NEG_INF)
        _, l_w, acc_w = _softmax_more(near(kw_ref, vw_ref, None), s_w, vw_ref[pl.ds(wf0, w_far), :])
        o_w = acc_w / l_w

        for i in range(NSA_HEADS_PER_PASS):
            r = h0 + i
            sl = slice(i * tq, (i + 1) * tq)
            row = 3 * (g * A_REP + r)
            out_t = (gates_t[row:row + 1] * o_c[:, r * tq:(r + 1) * tq] + gates_t[row + 1:row + 2] * o_s[:, sl]
                     + gates_t[row + 2:row + 3] * o_w[:, sl])
            outs.append(out_t.T)
    o_ref[...] = jnp.concatenate(outs, axis=1)


def _bucket_np(dist):
    n = np.maximum(dist, 0)
    exact = NUM_BUCKETS // 2
    nf = np.maximum(n, exact).astype(np.float64)
    large = exact + (np.log(nf / exact) / math.log(MAX_DISTANCE / exact) * (NUM_BUCKETS - exact)).astype(np.int64)
    return np.where(n < exact, n, np.minimum(large, NUM_BUCKETS - 1)).astype(np.int32)


def _bias_lookup(tab, dist):
    bucket = _bucket_np(np.asarray(dist))
    onehot = (jnp.asarray(bucket.reshape(-1, 1)) == jnp.arange(NUM_BUCKETS)[None, :]).astype(F32)
    flat = jnp.dot(onehot, tab.reshape(NUM_BUCKETS, -1), precision=lax.Precision.HIGHEST)
    return flat.reshape(bucket.shape + tab.shape[1:])


def _near_bias_tiles(tab, tq, window):
    i = np.arange(tq)[:, None]
    j = np.arange(2 * tq)[None, :]
    dist = tq + i - j
    ok = (dist >= 0) & (dist < window)
    bias = jnp.transpose(_bias_lookup(tab, dist), (2, 3, 0, 1))
    tiles = jnp.stack([jnp.where(jnp.asarray(ok & (j >= tq)), bias, NEG_INF), jnp.where(jnp.asarray(ok), bias, NEG_INF)])
    return tiles.reshape(2, tab.shape[1], tab.shape[2] * tq, 2 * tq)


def _cover_matrix(n_cmp_pad, n_sel):
    n = np.arange(n_cmp_pad)
    c_start = n * CMP_STRIDE
    c_end = c_start + CMP_LEN - 1
    s_start = np.arange(128) * SEL_BLOCK
    cover = (c_start[:, None] < s_start[None, :] + SEL_BLOCK) & (c_end[:, None] >= s_start[None, :])
    cover &= (np.arange(128)[None, :] < n_sel)
    return np.tile(cover.astype(np.float32), (A_REP, 1))


def _nsa_prompt_attend(q, gates, kc, vc, kvt, tab):
    b, t, _ = q.shape
    tq = NSA_Q_TILE
    assert t % NSA_FAR_TILE == 0 and t // SEL_BLOCK <= 32 and t >= WIN_A and kc.shape[2] == 128
    btile = jnp.swapaxes(_near_bias_tiles(tab - tab[NUM_BUCKETS - 1], tq, 2 * tq), -1, -2)
    cover = jnp.asarray(_cover_matrix(128, t // SEL_BLOCK)[:128].T, BF16)
    cmp_spec = pl.BlockSpec((None, A_KV, 128, HEAD_DIM), lambda i, j: (i, 0, 0, 0))
    return pl.pallas_call(
        functools.partial(_nsa_prompt_body, tq=tq),
        grid=(b, t // tq),
        in_specs=[
            pl.BlockSpec((None, tq, A_Q), lambda i, j: (i, j, 0)),
            pl.BlockSpec((None, tq, 128), lambda i, j: (i, j, 0)),
            cmp_spec, cmp_spec,
            pl.BlockSpec((None, 4 * A_KV, t, HEAD_DIM), lambda i, j: (i, 0, 0, 0)),
            pl.BlockSpec((None, A_KV, 2 * tq, A_REP * tq), lambda i, j: (jnp.minimum(j, 1), 0, 0, 0)),
            pl.BlockSpec((128, 128), lambda i, j: (0, 0)),
        ],
        out_specs=pl.BlockSpec((None, tq, A_Q), lambda i, j: (i, j, 0)),
        out_shape=jax.ShapeDtypeStruct((b, t, A_Q), F32),
        scratch_shapes=[pltpu.VMEM((A_KV, 2 * 32, NSA_HEADS_PER_PASS * tq), F32)],
        compiler_params=pltpu.CompilerParams(
            dimension_semantics=("parallel", "arbitrary"), vmem_limit_bytes=V7X_VMEM_LIMIT_BYTES),
        name="nsa_prompt",
    )(q, gates, kc, vc, kvt, btile, cover)


S5_NS = B_GROUPS * B_STATE
S5_T_CHUNK = 64
S5_STRIP = 512
S5_HALVES = 2


def _s5_body(u_ref, h0r_ref, h0i_ref, ar_ref, ai_ref, ldt_ref, wb_ref, wc_ref, d_ref, wglu_ref, bglu_ref,
             o_ref, hr_ref, hi_ref, coef_ref, st_ref, xbuf_ref, ubuf_ref, obuf_ref, *, nb, steps, interleave):
    c = pl.program_id(0)

    @pl.when(c == 0)
    def _():
        dt = jnp.exp(ldt_ref[...])
        ar, ai = ar_ref[...], ai_ref[...]
        mag = jnp.exp(ar * dt)
        abr, abi = mag * jnp.cos(ai * dt), mag * jnp.sin(ai * dt)
        den = ar * ar + ai * ai
        wr = ((abr - 1.0) * ar + abi * ai) / den
        wi = (abi * ar - (abr - 1.0) * ai) / den
        for k, val in enumerate((abr, abi, wr, wi)):
            coef_ref[k] = jnp.broadcast_to(val, (nb, S5_NS))
        st_ref[0] = h0r_ref[...]
        st_ref[1] = h0i_ref[...]

    if interleave:
        for b in range(nb):
            for j in range(B_WIDTH // 128):
                ubuf_ref.at[j][pl.ds(b, steps, stride=nb), :] = u_ref[b, :, j * 128:(j + 1) * 128]
        u = jnp.concatenate([ubuf_ref[j] for j in range(B_WIDTH // 128)], axis=1)
    else:
        u = u_ref[...]
    hc, hs = B_WIDTH // S5_HALVES, S5_NS // S5_HALVES
    ub = u.astype(BF16)
    for h in range(S5_HALVES):
        bu = jnp.dot(ub[:, h * hc:(h + 1) * hc], wb_ref[h], preferred_element_type=F32)
        xbuf_ref[:, h * hs:(h + 1) * hs] = bu[:, :hs]
        xbuf_ref[:, S5_NS + h * hs:S5_NS + (h + 1) * hs] = bu[:, hs:]

    for lo in range(0, S5_NS, S5_STRIP):
        re = slice(lo, lo + S5_STRIP)
        im = slice(S5_NS + lo, S5_NS + lo + S5_STRIP)
        abr, abi, wr, wi = (coef_ref[k, :, re] for k in range(4))

        def step(t, carry):
            sr, si = carry
            r0 = pl.multiple_of(t * nb, nb)
            bur = xbuf_ref[pl.ds(r0, nb), re]
            bui = xbuf_ref[pl.ds(r0, nb), im]
            nsr = abr * sr - abi * si + (wr * bur - wi * bui)
            nsi = abr * si + abi * sr + (wr * bui + wi * bur)
            xbuf_ref[pl.ds(r0, nb), re] = nsr
            xbuf_ref[pl.ds(r0, nb), im] = nsi
            return nsr, nsi

        sr, si = lax.fori_loop(0, steps, step, (st_ref[0, :, re], st_ref[1, :, re]),
                               unroll=min(steps, 8))
        st_ref[0, :, re] = sr
        st_ref[1, :, re] = si

    y = jnp.concatenate(
        [jnp.dot(jnp.concatenate([xbuf_ref[:, h * hs:(h + 1) * hs], xbuf_ref[:, S5_NS + h * hs:S5_NS + (h + 1) * hs]],
                                 axis=1).astype(BF16), wc_ref[h], preferred_element_type=F32)
         for h in range(S5_HALVES)], axis=1) + d_ref[...] * u
    z = _gelu_tanh(y)
    gate = jnp.dot(z.astype(BF16), wglu_ref[...], preferred_element_type=F32) + bglu_ref[...]
    out = z * (1.0 / (1.0 + jnp.exp(-gate)))
    if interleave:
        for j in range(B_WIDTH // 128):
            obuf_ref[j] = out[:, j * 128:(j + 1) * 128]
        for b in range(nb):
            for j in range(B_WIDTH // 128):
                o_ref[b, :, j * 128:(j + 1) * 128] = obuf_ref.at[j][pl.ds(b, steps, stride=nb), :]
    else:
        o_ref[...] = out
    hr_ref[...] = st_ref[0]
    hi_ref[...] = st_ref[1]


def _s5_weights(a_re, a_im, log_dt, b_re, b_im, c_re, c_im, d, w_glu, b_glu):
    eye = jnp.eye(B_GROUPS, dtype=F32)
    blk_in = lambda w: jnp.einsum('hg,gpc->hcgp', eye, w).reshape(B_WIDTH, S5_NS)
    blk_out = lambda w: jnp.einsum('gh,gcp->gphc', eye, w).reshape(S5_NS, B_WIDTH)
    hc, hs = B_WIDTH // S5_HALVES, S5_NS // S5_HALVES
    diag = lambda w, h: w[h * hc:(h + 1) * hc, h * hs:(h + 1) * hs]
    wb = jnp.stack([jnp.concatenate([diag(blk_in(b_re), h), diag(blk_in(b_im), h)], axis=1)
                    for h in range(S5_HALVES)]).astype(BF16)
    diag_t = lambda w, h: w[h * hs:(h + 1) * hs, h * hc:(h + 1) * hc]
    wc = jnp.stack([jnp.concatenate([diag_t(blk_out(c_re), h), -diag_t(blk_out(c_im), h)], axis=0)
                    for h in range(S5_HALVES)]).astype(BF16)
    flat = lambda x: x.reshape(1, S5_NS)
    return (flat(a_re), flat(a_im), flat(jnp.repeat(log_dt, B_STATE)), wb, wc, d.reshape(1, B_WIDTH),
            w_glu.astype(BF16), b_glu.reshape(1, B_WIDTH))


def _s5_mix(u, h_re, h_im, weights):
    nb, t, _ = u.shape
    interleave = t > 1
    steps = min(t, S5_T_CHUNK)
    rows = nb * steps
    body = functools.partial(_s5_body, nb=nb, steps=steps, interleave=interleave)
    if interleave:
        u_in = u
        u_spec = pl.BlockSpec((nb, steps, B_WIDTH), lambda c: (0, c, 0))
        o_shape = jax.ShapeDtypeStruct((nb, t, B_WIDTH), F32)
        scratch_rows = rows
    else:
        u_in = u.reshape(nb, B_WIDTH)
        u_spec = pl.BlockSpec((nb, B_WIDTH), lambda c: (0, 0))
        o_shape = jax.ShapeDtypeStruct((nb, B_WIDTH), F32)
        scratch_rows = 8
    o, hr, hi = pl.pallas_call(
        body,
        grid=(t // steps,),
        in_specs=[
            u_spec, _const_spec((nb, S5_NS)), _const_spec((nb, S5_NS)),
            _const_spec((1, S5_NS)), _const_spec((1, S5_NS)), _const_spec((1, S5_NS)),
            _const_spec((S5_HALVES, B_WIDTH // S5_HALVES, 2 * S5_NS // S5_HALVES)),
            _const_spec((S5_HALVES, 2 * S5_NS // S5_HALVES, B_WIDTH // S5_HALVES)),
            _const_spec((1, B_WIDTH)), _const_spec((B_WIDTH, B_WIDTH)), _const_spec((1, B_WIDTH)),
        ],
        out_specs=[u_spec, pl.BlockSpec((nb, S5_NS), lambda c: (0, 0)), pl.BlockSpec((nb, S5_NS), lambda c: (0, 0))],
        out_shape=[o_shape, jax.ShapeDtypeStruct((nb, S5_NS), F32), jax.ShapeDtypeStruct((nb, S5_NS), F32)],
        scratch_shapes=[
            pltpu.VMEM((4, nb, S5_NS), F32),
            pltpu.VMEM((2, nb, S5_NS), F32),
            pltpu.VMEM((rows, 2 * S5_NS), F32),
            pltpu.VMEM((B_WIDTH // 128, scratch_rows, 128), F32),
            pltpu.VMEM((B_WIDTH // 128, scratch_rows, 128), F32),
        ],
        compiler_params=pltpu.CompilerParams(
            dimension_semantics=("arbitrary",),
            vmem_limit_bytes=V7X_VMEM_LIMIT_BYTES),
        name="s5_mix",
    )(u_in, h_re, h_im, *weights)
    return o.reshape(nb, t, B_WIDTH), hr, hi


CHUNKS_PER_PAGE = PAGE_SIZE // CMP_STRIDE
SEL_PAD = 256


def _bf16_round(x):
    return x.astype(BF16).astype(F32)


def _compress_tail(c, pos, w2):
    c1 = c[:, 128:]
    nxt = jnp.concatenate([c1[1:], c1[:1]], axis=0)
    hid = (pos + c[:, :128]) + nxt
    return jnp.dot(_gelu_tanh(hid).astype(BF16), w2, preferred_element_type=F32)


def _rms_heads128(x, gain):
    left = lax.broadcasted_iota(jnp.int32, x.shape, 1) < HEAD_DIM
    sq = x * x
    s0 = jnp.sum(jnp.where(left, sq, 0.0), axis=-1, keepdims=True)
    s1 = jnp.sum(jnp.where(left, 0.0, sq), axis=-1, keepdims=True)
    ms = jnp.where(left, s0, s1) * (1.0 / HEAD_DIM)
    return x * lax.rsqrt(ms + EPS) * gain


def _pad_rows8(x):
    return jnp.concatenate([x, jnp.zeros((8 - x.shape[0], x.shape[1]), x.dtype)], axis=0)


def _compress_mlp(row_of_chunks, n_chunks, w_ref, pos, w2_ref):
    acc = jnp.zeros((n_chunks, 2 * A_KVW), F32)
    for sp in range(CMP_STRIDE // 2):
        x = jnp.concatenate([row_of_chunks(2 * sp + e) for e in range(2)], axis=1).astype(BF16)
        acc = acc + jnp.dot(x, w_ref[sp], preferred_element_type=F32)
    return _compress_tail(acc, pos, w2_ref[...])


def _compress_rows(rows_scr, kv, n_chunks, w_ref, pos, w2_ref):
    return _compress_mlp(lambda s: rows_scr.at[kv][pl.ds(s, n_chunks, stride=CMP_STRIDE), :],
                         n_chunks, w_ref, pos, w2_ref)


def _compress_weights(cmp_pos, cmp_w1, cmp_w2, k_gain):
    eye = jnp.eye(A_KV, dtype=F32)
    w1 = cmp_w1.reshape(2, 2, CMP_STRIDE, HEAD_DIM, HEAD_DIM)
    w_big = jnp.einsum('ajsdh,gk->asgdjkh', w1, eye).reshape(2, CMP_STRIDE // 2, 2 * A_KVW, 2 * A_KVW).astype(BF16)
    w_pos = jnp.concatenate([cmp_w1, cmp_w1], axis=-1).astype(BF16)
    pos = jnp.broadcast_to(cmp_pos.reshape(2, 1, CMP_LEN * HEAD_DIM), (2, 8, CMP_LEN * HEAD_DIM)).astype(BF16)
    w2_big = jnp.einsum('ahd,gk->aghkd', cmp_w2, eye).reshape(2, A_KVW, A_KVW).astype(BF16)
    gain2 = jnp.tile(k_gain, A_KV).reshape(1, A_KVW)
    return (pos[0], pos[1], w_pos[0], w_pos[1], w_big[0], w_big[1], w2_big[0], w2_big[1], gain2)


def _compress_specs(full):
    mlp_w = (CMP_STRIDE // 2, 2 * A_KVW, 2 * A_KVW)
    return [full((8, CMP_LEN * HEAD_DIM)), full((8, CMP_LEN * HEAD_DIM)),
            full((CMP_LEN * HEAD_DIM, A_KVW)), full((CMP_LEN * HEAD_DIM, A_KVW)),
            full(mlp_w), full(mlp_w), full((A_KVW, A_KVW)), full((A_KVW, A_KVW)), full((1, A_KVW))]


def _pos_terms(pos_scr, posk_ref, posv_ref, wpos_k_ref, wpos_v_ref):
    pos_scr[0] = jnp.dot(posk_ref[...], wpos_k_ref[...], preferred_element_type=F32)
    pos_scr[1] = jnp.dot(posv_ref[...], wpos_v_ref[...], preferred_element_type=F32)


def _cmp_prompt_body(cmp_ref, posk_ref, posv_ref, wpos_k_ref, wpos_v_ref, wk_ref, wv_ref, w2k_ref, w2v_ref,
                     kgain_ref, kc_ref, vc_ref, pos_scr, rows_scr, *, n_chunks):
    @pl.when(pl.program_id(0) == 0)
    def _():
        _pos_terms(pos_scr, posk_ref, posv_ref, wpos_k_ref, wpos_v_ref)

    rows_scr[0] = cmp_ref[:, :A_KVW]
    rows_scr[1] = cmp_ref[:, A_KVW:]
    kc = _rms_heads128(_compress_rows(rows_scr, 0, n_chunks, wk_ref, pos_scr[0, 0:1], w2k_ref),
                       kgain_ref[...]).astype(BF16)
    vc = _compress_rows(rows_scr, 1, n_chunks, wv_ref, pos_scr[1, 0:1], w2v_ref).astype(BF16)
    for g in range(A_KV):
        kc_ref[g] = kc[:, g * HEAD_DIM:(g + 1) * HEAD_DIM]
        vc_ref[g] = vc[:, g * HEAD_DIM:(g + 1) * HEAD_DIM]


def _cmp_prompt(cmp_rows, cmp_weights):
    b, t, _ = cmp_rows.shape
    n_chunks = t // CMP_STRIDE
    full = lambda shape: pl.BlockSpec(shape, lambda i: (0,) * len(shape))
    out_spec = pl.BlockSpec((None, A_KV, n_chunks, HEAD_DIM), lambda i: (i, 0, 0, 0))
    out_shape = jax.ShapeDtypeStruct((b, A_KV, n_chunks, HEAD_DIM), BF16)
    return pl.pallas_call(
        functools.partial(_cmp_prompt_body, n_chunks=n_chunks),
        grid=(b,),
        in_specs=[pl.BlockSpec((None, t, 2 * A_KVW), lambda i: (i, 0, 0))] + _compress_specs(full),
        out_specs=[out_spec, out_spec],
        out_shape=[out_shape, out_shape],
        scratch_shapes=[pltpu.VMEM((2, 8, A_KVW), F32), pltpu.VMEM((2, t, A_KVW), F32)],
        compiler_params=pltpu.CompilerParams(
            dimension_semantics=("arbitrary",), vmem_limit_bytes=V7X_VMEM_LIMIT_BYTES),
        name="nsa_cmp_prompt",
    )(cmp_rows, *cmp_weights)


def _nsa_sample_cmp_body(pt_ref, *refs, n_pages):
    page_refs = refs[:n_pages]
    (posk_ref, posv_ref, wpos_k_ref, wpos_v_ref, wk_ref, wv_ref, w2k_ref, w2v_ref, kgain_ref,
     kc_ref, vc_ref, pos_scr, rows_scr) = refs[n_pages:]

    @pl.when(pl.program_id(0) == 0)
    def _():
        _pos_terms(pos_scr, posk_ref, posv_ref, wpos_k_ref, wpos_v_ref)

    n_chunks = n_pages * CHUNKS_PER_PAGE
    for kv in range(2):
        for p in range(n_pages):
            rows = page_refs[p][kv].reshape(A_KVW, PAGE_SIZE).astype(BF16).T
            by_s = pltpu.einshape("csl->scl", rows.reshape(CHUNKS_PER_PAGE, CMP_STRIDE, A_KVW))
            for s in range(CMP_STRIDE):
                rows_scr[kv, s, p * CHUNKS_PER_PAGE:(p + 1) * CHUNKS_PER_PAGE, :] = by_s[s]

    compress = lambda kv, w_ref, w2_ref: _compress_mlp(lambda s: rows_scr[kv, s], n_chunks, w_ref,
                                                       pos_scr[kv, 0:1], w2_ref)
    kc_ref[...] = _rms_heads128(compress(0, wk_ref, w2k_ref), kgain_ref[...]).astype(BF16)
    vc_ref[...] = compress(1, wv_ref, w2v_ref).astype(BF16)


NSA_PICK_BATCH = 16


def _nsa_sample_pick_body(q_ref, kc_ref, vc_ref, cover_ref, pt_ref, oc_ref, idx_ref, phys_ref, *, n_cmp, n_sel):
    nb, n_chunks = kc_ref.shape[0], kc_ref.shape[1]
    col = lax.broadcasted_iota(jnp.int32, (8, n_chunks), 1)
    imps = []
    for i in range(nb):
        for g in range(A_KV):
            hs = slice(g * HEAD_DIM, (g + 1) * HEAD_DIM)
            q8 = (_pad_rows8(q_ref[i, g]) * (HEAD_DIM ** -0.5)).astype(BF16)
            s = jnp.where(col < n_cmp, _dot_nt(q8, kc_ref[i, :, hs]), NEG_INF)
            e = jnp.exp(s - jnp.max(s, axis=-1, keepdims=True))
            p = (e / jnp.sum(e, axis=-1, keepdims=True)).astype(BF16)
            oc_ref[i, g] = jnp.dot(p, vc_ref[i, :, hs], preferred_element_type=F32)[0:A_REP]
            imps.append(jnp.sum(jnp.dot(p, cover_ref[...], preferred_element_type=F32)[0:A_REP], axis=0, keepdims=True))
    imp = jnp.concatenate(imps, axis=0)
    rows = nb * A_KV
    s_idx = lax.broadcasted_iota(jnp.int32, (rows, SEL_PAD), 1)
    s_idx_f = s_idx.astype(F32)
    forced = (s_idx == 0) | (s_idx == n_sel - 1) | (s_idx == n_sel - 2)
    score = jnp.where(s_idx < n_sel, imp + jnp.where(forced, FORCE_BONUS, 0.0), NEG_INF)
    rank = jnp.zeros((rows, SEL_PAD), F32)
    for j in range(n_sel):
        cj = score[:, j:j + 1]
        beats = (cj > score) | ((cj == score) & (s_idx > j))
        rank = rank + jnp.where(beats, 1.0, 0.0)
    lane = lax.broadcasted_iota(jnp.int32, (rows, 128), 1)
    picks = jnp.zeros((rows, 128), F32)
    for r in range(SEL_TOPK):
        block = jnp.sum(jnp.where(rank == float(r), s_idx_f, 0.0), axis=-1, keepdims=True)
        picks = picks + jnp.where(lane == r, block, 0.0)
    picks = picks.astype(jnp.int32)
    idx_ref[...] = picks
    page = jnp.minimum(picks, n_sel - 2) // (PAGE_SIZE // SEL_BLOCK)
    phys_ref[...] = jnp.take_along_axis(pt_ref[...], page, axis=1)


NSA_ATT_BATCH = 2


def _nsa_sample_att_body(idx_ref, pt_ref, *refs, n_past_blk):
    n_slots = NSA_ATT_BATCH * A_KV * SEL_TOPK
    slot_refs = refs[:n_slots]
    (q_all, gate_all, oc_all, newsel_all, wcache_all, newwin_all, newcol_all, selb_all, winb_all, newb_all,
     o_all, wout_all) = refs[n_slots:]
    for i in range(NSA_ATT_BATCH):
        b = pl.program_id(0) * NSA_ATT_BATCH + i
        for g in range(A_KV):
            first = (i * A_KV + g) * SEL_TOPK
            _nsa_sample_att_group(b, g, idx_ref, slot_refs[first:first + SEL_TOPK], q_all.at[i, g], gate_all.at[i, g],
                                  oc_all.at[i, g], newsel_all.at[i, g], wcache_all.at[i, :, g], newwin_all.at[i, g],
                                  selb_all.at[g], winb_all.at[g], newb_all.at[g], o_all.at[i, g], n_past_blk)
            for kv in range(2):
                wout_all[i, kv, g] = _shift_in(wcache_all[i, kv, g], newcol_all[i, g][:, kv:kv + 1])


def _shift_in(rows_last, new_col):
    w_len = rows_last.shape[-1]
    lane = lax.broadcasted_iota(jnp.int32, rows_last.shape, 1)
    return jnp.where(lane == w_len - 1, new_col, pltpu.roll(rows_last, w_len - 1, axis=1))


def _nsa_sample_att_group(b, g, idx_ref, page_refs, q_ref, gate_ref, oc_ref, newsel_ref, wcache_ref, newwin_ref,
                          selb_ref, winb_ref, newb_ref, o_ref, n_past_blk):
    blocks_per_page = PAGE_SIZE // SEL_BLOCK
    q8 = (_pad_rows8(q_ref[...]) * (HEAD_DIM ** -0.5)).astype(BF16)
    new_bias = newb_ref[:, 0:1]

    def attend(s, v_t, new_kv):
        k_new = _bf16_round(new_kv[0:1])
        v_new = _bf16_round(new_kv[1:2])
        s_new = jnp.sum(q8.astype(F32) * k_new, axis=-1, keepdims=True) + new_bias
        m = jnp.maximum(jnp.max(s, axis=-1, keepdims=True), s_new)
        e = jnp.exp(s - m)
        e_new = jnp.exp(s_new - m)
        den = jnp.sum(e, axis=-1, keepdims=True) + e_new
        return _dot_nt((e / den).astype(BF16), v_t) + _bf16_round(e_new / den) * v_new

    lane = lax.broadcasted_iota(jnp.int32, (8, PAGE_SIZE), 1)
    near = selb_ref[...]
    bias = []
    for k in range(SEL_TOPK):
        ik = idx_ref[(b * A_KV + g) * SEL_TOPK + k]
        blk = jnp.minimum(ik, n_past_blk - 1)
        near_k = jnp.where(blk // blocks_per_page == (n_past_blk - 1) // blocks_per_page, near, 0.0)
        keep = (lane // SEL_BLOCK == blk % blocks_per_page) & (ik < n_past_blk)
        bias.append(jnp.where(keep, near_k, NEG_INF))
    k_sel = jnp.concatenate([page_refs[k][0].astype(BF16) for k in range(SEL_TOPK)], axis=1)
    v_sel = jnp.concatenate([page_refs[k][1].astype(BF16) for k in range(SEL_TOPK)], axis=1)
    s_sel = jnp.dot(q8, k_sel, preferred_element_type=F32) + jnp.concatenate(bias, axis=1)
    o_s = attend(s_sel, v_sel, newsel_ref[...])

    w_len = wcache_ref.shape[-1]
    wcol = lax.broadcasted_iota(jnp.int32, (8, w_len), 1)
    s_w = jnp.dot(q8, wcache_ref[0].astype(BF16), preferred_element_type=F32) + winb_ref[...]
    o_w = attend(jnp.where(wcol >= 1, s_w, NEG_INF), wcache_ref[1].astype(BF16), newwin_ref[...])
    gates = gate_ref[...]
    o_ref[...] = gates[:, 0:1] * oc_ref[...] + gates[:, 1:2] * o_s[0:A_REP] + gates[:, 2:3] * o_w[0:A_REP]


def _rows_last(cache):
    nd = cache.ndim
    return jnp.transpose(cache, tuple(range(nd - 4)) + (nd - 3, nd - 2, nd - 1, nd - 4))


def _rows_first(cache):
    nd = cache.ndim
    return jnp.transpose(cache, tuple(range(nd - 4)) + (nd - 1, nd - 4, nd - 3, nd - 2))


def _nsa_sample(q, gates, new_sel, new_win, pool_cmp, pool_sel, win_cache, page_table, cmp_weights, tab):
    n, n_pages = page_table.shape
    past = n_pages * PAGE_SIZE
    n_past_blk = past // SEL_BLOCK
    n_sel = n_past_blk + 1
    n_chunks = n_pages * CHUNKS_PER_PAGE
    w_len = win_cache.shape[1]
    blocks_per_page = PAGE_SIZE // SEL_BLOCK
    assert n_sel <= SEL_PAD and w_len == WIN_A and past >= WIN_A and blocks_per_page == 2
    pt_flat = page_table.reshape(-1)

    c_idx = np.arange(n_chunks)
    s_start = np.arange(SEL_PAD) * SEL_BLOCK
    cover = ((c_idx[:, None] * CMP_STRIDE < s_start[None, :] + SEL_BLOCK)
             & (c_idx[:, None] * CMP_STRIDE + CMP_LEN - 1 >= s_start[None, :])
             & (c_idx[:, None] < n_chunks - 1) & (np.arange(SEL_PAD)[None, :] < n_sel))
    cover = jnp.asarray(cover.astype(np.float32), BF16)

    head_spec = lambda last: pl.BlockSpec((None, A_KV, A_REP, last), lambda i, *_: (i, 0, 0, 0))
    page_specs = [pl.BlockSpec((None, 2, A_KV, HEAD_DIM, PAGE_SIZE),
                               functools.partial(lambda i, pt, p: (pt[i * n_pages + p], 0, 0, 0, 0), p=p))
                  for p in range(n_pages)]
    full = lambda shape: pl.BlockSpec(shape, lambda i, *_: (0,) * len(shape))
    pool_c = _rows_last(pool_cmp)
    cmp_spec = pl.BlockSpec((None, n_chunks, A_KVW), lambda i, *_: (i, 0, 0))
    cmp_shape = jax.ShapeDtypeStruct((n, n_chunks, A_KVW), BF16)
    kc, vc = pl.pallas_call(
        functools.partial(_nsa_sample_cmp_body, n_pages=n_pages),
        grid_spec=pltpu.PrefetchScalarGridSpec(
            num_scalar_prefetch=1,
            grid=(n,),
            in_specs=page_specs + _compress_specs(full),
            out_specs=[cmp_spec, cmp_spec],
            scratch_shapes=[pltpu.VMEM((2, 8, A_KVW), F32), pltpu.VMEM((2, CMP_STRIDE, n_chunks, A_KVW), BF16)],
        ),
        out_shape=[cmp_shape, cmp_shape],
        compiler_params=pltpu.CompilerParams(
            dimension_semantics=("arbitrary",), vmem_limit_bytes=V7X_VMEM_LIMIT_BYTES),
        name="nsa_sample_cmp",
    )(pt_flat, *([pool_c] * n_pages), *cmp_weights)

    nb = NSA_PICK_BATCH
    assert n % nb == 0
    batch = lambda *tail: pl.BlockSpec((nb,) + tail, lambda i: (i,) + (0,) * len(tail))
    assert n_pages <= 128
    pt_rows = jnp.pad(jnp.repeat(page_table, A_KV, axis=0), ((0, 0), (0, 128 - n_pages)))
    rows_spec = pl.BlockSpec((nb * A_KV, 128), lambda i: (i, 0))
    rows_shape = jax.ShapeDtypeStruct((n * A_KV, 128), jnp.int32)
    o_c, idx, phys = pl.pallas_call(
        functools.partial(_nsa_sample_pick_body, n_cmp=n_chunks - 1, n_sel=n_sel),
        grid=(n // nb,),
        in_specs=[batch(A_KV, A_REP, HEAD_DIM), batch(n_chunks, A_KVW), batch(n_chunks, A_KVW),
                  pl.BlockSpec((n_chunks, SEL_PAD), lambda i: (0, 0)), rows_spec],
        out_specs=[batch(A_KV, A_REP, HEAD_DIM), rows_spec, rows_spec],
        out_shape=[jax.ShapeDtypeStruct((n, A_KV, A_REP, HEAD_DIM), F32), rows_shape, rows_shape],
        compiler_params=pltpu.CompilerParams(
            dimension_semantics=("arbitrary",), vmem_limit_bytes=V7X_VMEM_LIMIT_BYTES),
        name="nsa_sample_pick",
    )(q, kc, vc, cover, pt_rows)

    rel = lambda dist: jnp.pad(jnp.transpose(_bias_lookup(tab - tab[NUM_BUCKETS - 1], dist), (1, 2, 0)),
                               ((0, 0), (0, 8 - A_REP), (0, 0)))
    sel_bias = rel(past - (past - PAGE_SIZE + np.arange(PAGE_SIZE)))
    win_bias = rel(w_len - np.arange(w_len))
    new_bias = rel(np.zeros(128, np.int64))

    idx_flat = idx[:, :SEL_TOPK].reshape(-1)
    phys_flat = phys[:, :SEL_TOPK].reshape(-1)

    nba = NSA_ATT_BATCH
    assert n % nba == 0

    def page_map(i, idx_s, phys_s, seq, g, slot):
        return (phys_s[i * (nba * A_KV * SEL_TOPK) + (seq * A_KV + g) * SEL_TOPK + slot], 0, g, 0, 0)

    slot_specs = [pl.BlockSpec((None, 2, None, HEAD_DIM, PAGE_SIZE), functools.partial(page_map, seq=e, g=g, slot=s))
                  for e in range(nba) for g in range(A_KV) for s in range(SEL_TOPK)]
    per_head = lambda rows, last: pl.BlockSpec((nba, A_KV, rows, last), lambda i, *_: (i, 0, 0, 0))
    per_group = lambda last: pl.BlockSpec((A_KV, 8, last), lambda i, *_: (0, 0, 0))
    win_spec = pl.BlockSpec((nba, 2, A_KV, HEAD_DIM, w_len), lambda i, *_: (i, 0, 0, 0, 0))
    o, win_next = pl.pallas_call(
        functools.partial(_nsa_sample_att_body, n_past_blk=n_past_blk),
        grid_spec=pltpu.PrefetchScalarGridSpec(
            num_scalar_prefetch=2,
            grid=(n // nba,),
            in_specs=slot_specs + [
                per_head(A_REP, HEAD_DIM), per_head(A_REP, 3), per_head(A_REP, HEAD_DIM), per_head(2, HEAD_DIM),
                win_spec, per_head(2, HEAD_DIM), per_head(HEAD_DIM, 2),
                per_group(PAGE_SIZE), per_group(w_len), per_group(128),
            ],
            out_specs=[per_head(A_REP, HEAD_DIM), win_spec],
        ),
        out_shape=[jax.ShapeDtypeStruct((n, A_KV, A_REP, HEAD_DIM), F32),
                   jax.ShapeDtypeStruct((n, 2, A_KV, HEAD_DIM, w_len), F32)],
        compiler_params=pltpu.CompilerParams(
            dimension_semantics=("arbitrary",), vmem_limit_bytes=V7X_VMEM_LIMIT_BYTES),
        name="nsa_sample_att",
    )(idx_flat, phys_flat, *([_rows_last(pool_sel)] * (nba * A_KV * SEL_TOPK)), q, gates, o_c,
      jnp.transpose(new_sel, (0, 2, 1, 3)), _rows_last(win_cache), jnp.transpose(new_win, (0, 2, 1, 3)),
      jnp.transpose(new_win, (0, 2, 3, 1)), sel_bias, win_bias, new_bias)
    return o, _rows_first(win_next)


SWA_Q_TILE = WIN_C


def _swa_prompt_body(q_ref, kv_ref, btile_ref, sink_ref, o_ref, *, tq):
    qt = pl.program_id(1)
    q0 = pl.multiple_of(qt * tq, tq)
    prev0 = pl.multiple_of(jnp.maximum(q0 - tq, 0), tq)
    width = C_REP * HEAD_DIM
    for g in range(C_KV):
        q = q_ref[:, g * width:(g + 1) * width]
        qs = jnp.concatenate([q[:, r * HEAD_DIM:(r + 1) * HEAD_DIM] for r in range(C_REP)], axis=0)
        qs = (qs * (HEAD_DIM ** -0.5)).astype(BF16)
        k_ref, v_ref = kv_ref.at[g], kv_ref.at[C_KV + g]
        k = jnp.concatenate([k_ref[pl.ds(prev0, tq), :], k_ref[pl.ds(q0, tq), :]], axis=0)
        v = jnp.concatenate([v_ref[pl.ds(prev0, tq), :], v_ref[pl.ds(q0, tq), :]], axis=0)
        s = _dot_nt(k, qs) + btile_ref[g]
        sinks = sink_ref[g]
        sink = jnp.concatenate([sinks[r:r + 1, :] for r in range(C_REP)], axis=1)
        m = jnp.maximum(jnp.max(s, axis=0, keepdims=True), sink)
        e = jnp.exp(s - m)
        p = e / (jnp.sum(e, axis=0, keepdims=True) + jnp.exp(sink - m))
        o = _dot_tn(v, p.astype(BF16))
        o_ref[:, g * width:(g + 1) * width] = jnp.concatenate([o[:, r * tq:(r + 1) * tq].T for r in range(C_REP)],
                                                              axis=1)


def _swa_prompt(q, kvt, sinks, tab):
    b, t, _ = q.shape
    tq = SWA_Q_TILE
    btile = jnp.swapaxes(_near_bias_tiles(tab, tq, WIN_C), -1, -2)
    qo_spec = pl.BlockSpec((None, tq, C_HEADS * HEAD_DIM), lambda i, j: (i, j, 0))
    sink_lanes = jnp.broadcast_to(sinks.reshape(C_KV, C_REP, 1), (C_KV, C_REP, 128))
    return pl.pallas_call(
        functools.partial(_swa_prompt_body, tq=tq),
        grid=(b, t // tq),
        in_specs=[qo_spec,
                  pl.BlockSpec((None, 2 * C_KV, t, HEAD_DIM), lambda i, j: (i, 0, 0, 0)),
                  pl.BlockSpec((None, C_KV, 2 * tq, C_REP * tq), lambda i, j: (jnp.minimum(j, 1), 0, 0, 0)),
                  pl.BlockSpec((C_KV, C_REP, 128), lambda i, j: (0, 0, 0))],
        out_specs=qo_spec,
        out_shape=jax.ShapeDtypeStruct((b, t, C_HEADS * HEAD_DIM), F32),
        compiler_params=pltpu.CompilerParams(
            dimension_semantics=("parallel", "arbitrary"), vmem_limit_bytes=V7X_VMEM_LIMIT_BYTES),
        name="swa_prompt",
    )(q, kvt, btile, sink_lanes)


SWA_DECODE_BATCH = 4


def _swa_sample_body(q_ref, cache_ref, new_ref, newcol_ref, bias_ref, newb_ref, sink_ref, o_ref, next_ref):
    w_len = cache_ref.shape[-1]
    wcol = lax.broadcasted_iota(jnp.int32, (C_REP, w_len), 1)
    for i in range(q_ref.shape[0]):
        for g in range(C_KV):
            q8 = (q_ref[i, g] * (HEAD_DIM ** -0.5)).astype(BF16)
            s = jnp.dot(q8, cache_ref[i, 0, g].astype(BF16), preferred_element_type=F32) + bias_ref[g]
            s = jnp.where(wcol >= 1, s, NEG_INF)
            k_new = _bf16_round(new_ref[i, g, 0:1])
            v_new = _bf16_round(new_ref[i, g, 1:2])
            s_new = jnp.sum(q8.astype(F32) * k_new, axis=-1, keepdims=True) + newb_ref[g][:, 0:1]
            sink = sink_ref[g][:, 0:1]
            m = jnp.maximum(jnp.maximum(jnp.max(s, axis=-1, keepdims=True), s_new), sink)
            e = jnp.exp(s - m)
            e_new = jnp.exp(s_new - m)
            den = jnp.sum(e, axis=-1, keepdims=True) + e_new + jnp.exp(sink - m)
            o_ref[i, g] = (_dot_nt((e / den).astype(BF16), cache_ref[i, 1, g].astype(BF16))
                           + _bf16_round(e_new / den) * v_new)
            for kv in range(2):
                next_ref[i, kv, g] = _shift_in(cache_ref[i, kv, g], newcol_ref[i, g][:, kv:kv + 1])


def _swa_sample(q, cache, new_kv, sinks, tab):
    n, w_len = cache.shape[:2]
    bias = jnp.transpose(_bias_lookup(tab, w_len - np.arange(w_len)), (1, 2, 0))
    lanes = lambda x: jnp.broadcast_to(x[:, :, None], (C_KV, C_REP, 128))
    full = lambda shape: pl.BlockSpec(shape, lambda i: (0,) * len(shape))
    nb = SWA_DECODE_BATCH
    assert n % nb == 0
    cache_spec = pl.BlockSpec((nb, 2, C_KV, HEAD_DIM, w_len), lambda i: (i, 0, 0, 0, 0))
    o_spec = pl.BlockSpec((nb, C_KV, C_REP, HEAD_DIM), lambda i: (i, 0, 0, 0))
    o, cache_next = pl.pallas_call(
        _swa_sample_body,
        grid=(n // nb,),
        in_specs=[
            o_spec, cache_spec,
            pl.BlockSpec((nb, C_KV, 2, HEAD_DIM), lambda i: (i, 0, 0, 0)),
            pl.BlockSpec((nb, C_KV, HEAD_DIM, 2), lambda i: (i, 0, 0, 0)),
            full((C_KV, C_REP, w_len)), full((C_KV, C_REP, 128)), full((C_KV, C_REP, 128)),
        ],
        out_specs=[o_spec, cache_spec],
        out_shape=[jax.ShapeDtypeStruct((n, C_KV, C_REP, HEAD_DIM), F32),
                   jax.ShapeDtypeStruct((n, 2, C_KV, HEAD_DIM, w_len), F32)],
        compiler_params=pltpu.CompilerParams(
            dimension_semantics=("arbitrary",), vmem_limit_bytes=V7X_VMEM_LIMIT_BYTES),
        name="swa_sample",
    )(q, _rows_last(cache), new_kv, jnp.transpose(new_kv, (0, 1, 3, 2)), bias, lanes(tab[0]),
      lanes(sinks.reshape(C_KV, C_REP)))
    return o, _rows_first(cache_next)


PROJ_ROW_TILE = 512
A_IN_PAD = 1920
A_U_COL = A_Q + 6 * A_KVW
A_GATE_COL = A_U_COL + B_WIDTH


def _heads_first(kvt_ref, slot, k, v, n_kv):
    for g in range(n_kv):
        kvt_ref[slot + g] = k[:, g * HEAD_DIM:(g + 1) * HEAD_DIM].astype(BF16)
        kvt_ref[slot + n_kv + g] = v[:, g * HEAD_DIM:(g + 1) * HEAD_DIM].astype(BF16)


def _inproj_a_body(x_ref, gain_ref, w_ref, qg_ref, kg_ref,
                   q_ref, cmp_ref, sel_ref, win_ref, gate_ref, u_ref, kvt_ref):
    xn = _rms_rows(x_ref[...], gain_ref[...]).astype(BF16)
    z = jnp.dot(xn, w_ref[...], preferred_element_type=F32)
    for j in range(A_Q // 128):
        q_ref[:, j * 128:(j + 1) * 128] = _rms_heads128(z[:, j * 128:(j + 1) * 128], qg_ref[...])
    for out_ref, off, slot in ((cmp_ref, A_Q, None), (sel_ref, A_Q + 2 * A_KVW, 0), (win_ref, A_Q + 4 * A_KVW, 4)):
        k = _rms_heads128(z[:, off:off + A_KVW], kg_ref[...])
        v = z[:, off + A_KVW:off + 2 * A_KVW]
        out_ref[:, :A_KVW] = k
        out_ref[:, A_KVW:] = v
        if slot is not None:
            _heads_first(kvt_ref, slot, k, v, A_KV)
    u_ref[...] = z[:, A_U_COL:A_GATE_COL]
    gate_ref[...] = 1.0 / (1.0 + jnp.exp(-z[:, A_GATE_COL:A_IN_PAD]))


def _inproj_a(x, gain, w_in, q_gain, k_gain):
    b, t, d = x.shape
    tm = min(t, PROJ_ROW_TILE)
    w = jnp.concatenate([w_in[:, :A_U_COL], w_in[:, A_U_COL + A_GATE:], w_in[:, A_U_COL:A_U_COL + A_GATE],
                         jnp.zeros((d, A_IN_PAD - A_GATE_COL - A_GATE), F32)], axis=1).astype(BF16)
    tile2 = lambda g: jnp.tile(g, 2).reshape(1, 128)
    rows = lambda width: pl.BlockSpec((None, tm, width), lambda i, j: (i, j, 0))
    shape = lambda width: jax.ShapeDtypeStruct((b, t, width), F32)
    return pl.pallas_call(
        _inproj_a_body,
        grid=(b, t // tm),
        in_specs=[rows(d), _const_spec((1, d)), _const_spec((d, A_IN_PAD)), _const_spec((1, 128)), _const_spec((1, 128))],
        out_specs=[rows(A_Q), rows(2 * A_KVW), rows(2 * A_KVW), rows(2 * A_KVW), rows(128), rows(B_WIDTH),
                   pl.BlockSpec((None, 8, tm, HEAD_DIM), lambda i, j: (i, 0, j, 0))],
        out_shape=[shape(A_Q), shape(2 * A_KVW), shape(2 * A_KVW), shape(2 * A_KVW), shape(128), shape(B_WIDTH),
                   jax.ShapeDtypeStruct((b, 8, t, HEAD_DIM), BF16)],
        compiler_params=pltpu.CompilerParams(
            dimension_semantics=("parallel", "parallel"), vmem_limit_bytes=V7X_VMEM_LIMIT_BYTES),
        name="inproj_nsa_s5",
    )(x, gain.reshape(1, d), w, tile2(q_gain), tile2(k_gain))


def _inproj_c_body(x_ref, gain_ref, w_ref, qg_ref, kg_ref, q_ref, kv_ref, kvt_ref):
    xn = _rms_rows(x_ref[...], gain_ref[...]).astype(BF16)
    z = jnp.dot(xn, w_ref[...], preferred_element_type=F32)
    n_q = C_HEADS * HEAD_DIM
    for j in range(n_q // 128):
        q_ref[:, j * 128:(j + 1) * 128] = _rms_heads128(z[:, j * 128:(j + 1) * 128], qg_ref[...])
    k = _rms_heads128(z[:, n_q:n_q + C_KV * HEAD_DIM], kg_ref[...])
    v = z[:, n_q + C_KV * HEAD_DIM:]
    kv_ref[:, :C_KV * HEAD_DIM] = k
    kv_ref[:, C_KV * HEAD_DIM:] = v
    _heads_first(kvt_ref, 0, k, v, C_KV)


def _inproj_c(x, gain, w_in, q_gain, k_gain):
    b, t, d = x.shape
    tm = min(t, PROJ_ROW_TILE)
    n_in = w_in.shape[1]
    tile2 = lambda g: jnp.tile(g, 2).reshape(1, 128)
    rows = lambda width: pl.BlockSpec((None, tm, width), lambda i, j: (i, j, 0))
    shape = lambda width: jax.ShapeDtypeStruct((b, t, width), F32)
    return pl.pallas_call(
        _inproj_c_body,
        grid=(b, t // tm),
        in_specs=[rows(d), _const_spec((1, d)), _const_spec((d, n_in)), _const_spec((1, 128)), _const_spec((1, 128))],
        out_specs=[rows(C_HEADS * HEAD_DIM), rows(2 * C_KV * HEAD_DIM),
                   pl.BlockSpec((None, 2 * C_KV, tm, HEAD_DIM), lambda i, j: (i, 0, j, 0))],
        out_shape=[shape(C_HEADS * HEAD_DIM), shape(2 * C_KV * HEAD_DIM),
                   jax.ShapeDtypeStruct((b, 2 * C_KV, t, HEAD_DIM), BF16)],
        compiler_params=pltpu.CompilerParams(
            dimension_semantics=("parallel", "parallel"), vmem_limit_bytes=V7X_VMEM_LIMIT_BYTES),
        name="inproj_swa",
    )(x, gain.reshape(1, d), w_in.astype(BF16), tile2(q_gain), tile2(k_gain))


FFN_ROW_TILE = 512
FFN_COL_CHUNK = 2816


def _mixer_residual(y_ref, x_ref, mix_refs, wout_refs):
    y_ref[...] = x_ref[...]
    for m_ref, w_ref in zip(mix_refs, wout_refs):
        y_ref[...] += jnp.dot(m_ref[...].astype(BF16), w_ref[...], preferred_element_type=F32)
    return y_ref[...]


def _tail_prompt_body(*refs, n_mix, tm, ffc):
    x_ref = refs[0]
    mix_refs = refs[1:1 + n_mix]
    wout_refs = refs[1 + n_mix:1 + 2 * n_mix]
    (gain_ref, prev_ref, wup_ref, wgate_ref, cw_ref, cb_ref, wdown_ref, y_ref, cs_ref, hbuf_ref) = refs[1 + 2 * n_mix:]
    t = pl.program_id(1)
    xn = _rms_rows(_mixer_residual(y_ref, x_ref, mix_refs, wout_refs), gain_ref[...]).astype(BF16)
    for c in range(D_FF // ffc):
        lo = c * ffc
        h = jnp.dot(xn, wup_ref[:, lo:lo + ffc], preferred_element_type=F32)
        g = jnp.dot(xn, wgate_ref[:, lo:lo + ffc], preferred_element_type=F32)

        @pl.when(t == 0)
        def _():
            hbuf_ref[c, 6:8, :] = prev_ref[:, lo:lo + ffc]

        hbuf_ref[c, 8:8 + tm, :] = h
        hm1 = hbuf_ref[c, 7:7 + tm, :]
        hm2 = hbuf_ref[c, 6:6 + tm, :]
        cw = cw_ref[:, lo:lo + ffc]
        hc = cw[0:1] * hm2 + cw[1:2] * hm1 + cw[2:3] * h + cb_ref[:, lo:lo + ffc]
        a = (_gelu_tanh(hc) * g).astype(BF16)
        y_ref[...] += jnp.dot(a, wdown_ref[lo:lo + ffc, :], preferred_element_type=F32)
        hbuf_ref[c, 0:8, :] = h[tm - 8:tm, :]
        cs_ref[:, lo:lo + ffc] = h[tm - 2:tm, :]


def _tail_prompt(x, mixes, wouts, gain, prev, wup, wgate, cw, cb, wdown):
    b, t, d = x.shape
    tm, ffc = FFN_ROW_TILE, FFN_COL_CHUNK
    rows = lambda width: pl.BlockSpec((None, tm, width), lambda i, j: (i, j, 0))
    state = pl.BlockSpec((None, CONV_W - 1, D_FF), lambda i, j: (i, 0, 0))
    return pl.pallas_call(
        functools.partial(_tail_prompt_body, n_mix=len(mixes), tm=tm, ffc=ffc),
        grid=(b, t // tm),
        in_specs=[rows(d)] + [rows(m.shape[-1]) for m in mixes] + [_const_spec(w.shape) for w in wouts] + [
            _const_spec((1, d)), state, _const_spec((d, D_FF)), _const_spec((d, D_FF)),
            _const_spec((CONV_W, D_FF)), _const_spec((1, D_FF)), _const_spec((D_FF, d))],
        out_specs=[rows(d), state],
        out_shape=[jax.ShapeDtypeStruct((b, t, d), F32), jax.ShapeDtypeStruct((b, CONV_W - 1, D_FF), F32)],
        scratch_shapes=[pltpu.VMEM((D_FF // ffc, 8 + tm, ffc), F32)],
        compiler_params=pltpu.CompilerParams(
            dimension_semantics=("parallel", "arbitrary"), vmem_limit_bytes=V7X_VMEM_LIMIT_BYTES),
        name="tail_prompt",
    )(x, *mixes, *wouts, gain, prev, wup, wgate, cw, cb, wdown)


def _tail_sample_body(*refs, n_mix, ffc):
    x_ref = refs[0]
    mix_refs = refs[1:1 + n_mix]
    wout_refs = refs[1 + n_mix:1 + 2 * n_mix]
    (gain_ref, prev_ref, wup_ref, wgate_ref, cw_ref, cb_ref, wdown_ref, y_ref, cs_ref) = refs[1 + 2 * n_mix:]
    xn = _rms_rows(_mixer_residual(y_ref, x_ref, mix_refs, wout_refs), gain_ref[...]).astype(BF16)
    for c in range(D_FF // ffc):
        lo = c * ffc
        h = jnp.dot(xn, wup_ref[:, lo:lo + ffc], preferred_element_type=F32)
        g = jnp.dot(xn, wgate_ref[:, lo:lo + ffc], preferred_element_type=F32)
        hm2 = prev_ref[:, lo:lo + ffc]
        hm1 = prev_ref[:, D_FF + lo:D_FF + lo + ffc]
        cw = cw_ref[:, lo:lo + ffc]
        hc = cw[0:1] * hm2 + cw[1:2] * hm1 + cw[2:3] * h + cb_ref[:, lo:lo + ffc]
        a = (_gelu_tanh(hc) * g).astype(BF16)
        y_ref[...] += jnp.dot(a, wdown_ref[lo:lo + ffc, :], preferred_element_type=F32)
        cs_ref[:, lo:lo + ffc] = hm1
        cs_ref[:, D_FF + lo:D_FF + lo + ffc] = h


def _tail_sample(x, mixes, wouts, gain, prev, wup, wgate, cw, cb, wdown):
    n, d = x.shape
    full = lambda shape: _const_spec(shape)
    return pl.pallas_call(
        functools.partial(_tail_sample_body, n_mix=len(mixes), ffc=FFN_COL_CHUNK),
        grid=(1,),
        in_specs=[full((n, d))] + [full(m.shape) for m in mixes] + [full(w.shape) for w in wouts] + [
            full((1, d)), full((n, (CONV_W - 1) * D_FF)), full((d, D_FF)), full((d, D_FF)),
            full((CONV_W, D_FF)), full((1, D_FF)), full((D_FF, d))],
        out_specs=[pl.BlockSpec((n, d), lambda i: (0, 0)), pl.BlockSpec((n, (CONV_W - 1) * D_FF), lambda i: (0, 0))],
        out_shape=[jax.ShapeDtypeStruct((n, d), F32), jax.ShapeDtypeStruct((n, (CONV_W - 1) * D_FF), F32)],
        compiler_params=pltpu.CompilerParams(
            dimension_semantics=("arbitrary",), vmem_limit_bytes=V7X_VMEM_LIMIT_BYTES),
        name="tail_sample",
    )(x, *mixes, *wouts, gain, prev, wup, wgate, cw, cb, wdown)


def kernel(x_prompt, x_sample, cache_nsa_cmp, cache_nsa_sel, cache_nsa_win, state_s5_re, state_s5_im,
           cache_swa, state_ffn_conv, page_table, rel_bias, norm_mix, norm_ffn, a_w_in, a_w_out,
           nsa_q_gain, nsa_k_gain, nsa_cmp_pos, nsa_cmp_w1, nsa_cmp_w2, s5_a_re, s5_a_im, s5_log_dt,
           s5_b_re, s5_b_im, s5_c_re, s5_c_im, s5_d, s5_w_glu, s5_b_glu, c_w_in, c_w_out, c_q_gain,
           c_k_gain, c_sinks, ffn_w_up, ffn_w_gate, ffn_conv_w, ffn_conv_b, ffn_w_down):
    bp, tp, _ = x_prompt.shape
    bs, ts, _ = x_sample.shape
    assert ts == 1 and DEPTH == 2
    tab_a = rel_bias[:, :A_HEADS].reshape(NUM_BUCKETS, A_KV, A_REP)
    tab_c = rel_bias[:, :C_HEADS].reshape(NUM_BUCKETS, C_KV, C_REP)
    kv6 = lambda x: x.reshape(x.shape[:-1] + (2, x.shape[-1] // (2 * HEAD_DIM), HEAD_DIM))
    hp, hs = x_prompt, x_sample.reshape(1, bs, D_MODEL)
    conv_p, conv_s = [], []

    def tail(layer, hp, hs, mixes_p, mixes_s, wouts):
        wouts = [w.astype(BF16) for w in wouts]
        ffn = (norm_ffn[layer].reshape(1, D_MODEL),)
        wts = (ffn_w_up[layer].astype(BF16), ffn_w_gate[layer].astype(BF16), ffn_conv_w[layer],
               ffn_conv_b[layer].reshape(1, D_FF), ffn_w_down[layer].astype(BF16))
        hp, cp = _tail_prompt(hp, mixes_p, wouts, *ffn, jnp.zeros((bp, CONV_W - 1, D_FF), F32), *wts)
        hs2, cs = _tail_sample(hs[0], [m.reshape(bs, -1) for m in mixes_s], wouts, *ffn,
                               state_ffn_conv[layer].reshape(bs, (CONV_W - 1) * D_FF), *wts)
        conv_p.append(cp)
        conv_s.append(cs.reshape(bs, CONV_W - 1, D_FF))
        return hp, hs2.reshape(1, bs, D_MODEL)

    proj = (norm_mix[0], a_w_in[0], nsa_q_gain[0], nsa_k_gain[0])
    qp, cmp_p, sel_p, win_p, gate_p, up, kvt_p = _inproj_a(hp, *proj)
    qs, cmp_s, sel_s, win_s, gate_s, us, _ = _inproj_a(hs, *proj)
    cmp_w = _compress_weights(nsa_cmp_pos[0], nsa_cmp_w1[0], nsa_cmp_w2[0], nsa_k_gain[0])
    kc, vc = _cmp_prompt(cmp_p, cmp_w)
    o_ap = _nsa_prompt_attend(qp, gate_p, kc, vc, kvt_p, tab_a)
    o_as, nsa_win_s = _nsa_sample(qs.reshape(bs, A_KV, A_REP, HEAD_DIM),
                                  gate_s[0, :, :A_GATE].reshape(bs, A_KV, A_REP, 3),
                                  sel_s.reshape(bs, 2, A_KV, HEAD_DIM), win_s.reshape(bs, 2, A_KV, HEAD_DIM),
                                  cache_nsa_cmp[0], cache_nsa_sel[0], cache_nsa_win[0], page_table, cmp_w, tab_a)
    s5w = _s5_weights(s5_a_re[0], s5_a_im[0], s5_log_dt[0], s5_b_re[0], s5_b_im[0], s5_c_re[0], s5_c_im[0],
                      s5_d[0], s5_w_glu[0], s5_b_glu[0])
    h0 = jnp.zeros((bp, S5_NS), F32)
    o_bp, hr_p, hi_p = _s5_mix(up, h0, h0, s5w)
    o_bs, hr_s, hi_s = _s5_mix(us.reshape(bs, 1, B_WIDTH), state_s5_re[0].reshape(bs, S5_NS),
                               state_s5_im[0].reshape(bs, S5_NS), s5w)
    hp, hs = tail(0, hp, hs, [o_ap, o_bp], [o_as, o_bs], [a_w_out[0][:A_Q], a_w_out[0][A_Q:]])

    proj = (norm_mix[1], c_w_in[0], c_q_gain[0], c_k_gain[0])
    qcp, kv_p, kvt_c = _inproj_c(hp, *proj)
    qcs, kv_s, _ = _inproj_c(hs, *proj)
    o_cp = _swa_prompt(qcp, kvt_c, c_sinks[0], tab_c)
    new_kv = kv_s.reshape(bs, 2, C_KV, HEAD_DIM)
    o_cs, swa_s = _swa_sample(qcs.reshape(bs, C_KV, C_REP, HEAD_DIM), cache_swa[0],
                              jnp.transpose(new_kv, (0, 2, 1, 3)), c_sinks[0], tab_c)
    hp, hs = tail(1, hp, hs, [o_cp], [o_cs], [c_w_out[0]])

    state = lambda x, n: x.reshape(1, n, B_GROUPS, B_STATE)
    return (hp, hs.reshape(bs, ts, D_MODEL),
            kv6(cmp_p)[None], kv6(cmp_s).reshape(1, bs, ts, 2, A_KV, HEAD_DIM),
            kv6(sel_p)[None], kv6(sel_s).reshape(1, bs, ts, 2, A_KV, HEAD_DIM),
            kv6(win_p)[None, :, -min(WIN_A, tp):], nsa_win_s[None],
            state(hr_p, bp), state(hi_p, bp), state(hr_s, bs), state(hi_s, bs),
            kv6(kv_p)[None, :, -min(WIN_C, tp):], swa_s[None],
            jnp.stack(conv_p), jnp.stack(conv_s))
```
